```python
import math
import jax, jax.numpy as jnp
from jax import lax
import numpy as np

D_MODEL = 1024
BATCH = 8
SEQ = 2048
DEPTH = 1

MEM_LEN = 256
SSM_HEAD_DIM = 64
SSM_HEADS = D_MODEL // SSM_HEAD_DIM
SSM_D_INNER = SSM_HEADS * SSM_HEAD_DIM
SSM_GROUPS = 2
SSM_STATE = 128
CONV_WIDTH = 4
CHUNK = 128
CONV_DIM = SSM_D_INNER + 2 * SSM_GROUPS * SSM_STATE
ATTN_HEAD_DIM = 64
ATTN_HEADS = D_MODEL // ATTN_HEAD_DIM
ATTN_WIDTH = ATTN_HEADS * ATTN_HEAD_DIM
Q_BLOCK = 128
MIX_WIDTH = SSM_D_INNER + ATTN_WIDTH
IN_COLS = 2 * SSM_D_INNER + 2 * SSM_GROUPS * SSM_STATE + SSM_HEADS + 3 * ATTN_WIDTH + ATTN_HEADS
XATTN_HEADS = 4
XATTN_HEAD_DIM = D_MODEL // XATTN_HEADS
D_FF = 4 * D_MODEL
EPS = 1e-5

kernel_name = "hymba_ssd_fox_memxattn_layer"


def rms_norm(u, g):
    uf = u.astype(jnp.float32)
    y = uf * lax.rsqrt(jnp.mean(uf * uf, axis=-1, keepdims=True) + EPS)
    return (y * g.astype(jnp.float32)).astype(u.dtype)


def segsum(a):
    T = a.shape[-1]
    x = jnp.broadcast_to(a[..., :, None], a.shape + (T,))
    x = jnp.where(jnp.tril(jnp.ones((T, T), dtype=bool), -1), x, 0.0)
    x = jnp.cumsum(x, axis=-2)
    return jnp.where(jnp.tril(jnp.ones((T, T), dtype=bool)), x, -jnp.inf)


def causal_depthwise_conv(u, w, b):
    c = u.shape[-1]
    out = lax.conv_general_dilated(
        u, w[:, None, :].astype(u.dtype), window_strides=(1,),
        padding=[(CONV_WIDTH - 1, 0)], dimension_numbers=("NWC", "WIO", "NWC"),
        feature_group_count=c)
    return out + b.astype(u.dtype)


def ssd_chunked(xh, dt, A, Bm, Cm):
    b, S, g, r, p = xh.shape
    n = Bm.shape[-1]
    c = S // CHUNK
    X = (xh * dt[..., None]).reshape(b, c, CHUNK, g, r, p)
    dA = (dt * A).reshape(b, c, CHUNK, g, r).transpose(0, 3, 4, 1, 2)
    Bc = Bm.reshape(b, c, CHUNK, g, n)
    Cc = Cm.reshape(b, c, CHUNK, g, n)
    A_cs = jnp.cumsum(dA, axis=-1)
    Lmat = jnp.exp(segsum(dA))
    CB = jnp.einsum("bclgn,bcsgn->bcgls", Cc, Bc)
    y_diag = jnp.einsum("bcgls,bgrcls,bcsgrp->bclgrp", CB, Lmat, X)
    decay_states = jnp.exp(A_cs[..., -1:] - A_cs)
    states = jnp.einsum("bclgn,bgrcl,bclgrp->bcgrpn", Bc, decay_states, X)
    states = jnp.concatenate([jnp.zeros_like(states[:, :1]), states], axis=1)
    A_last = jnp.pad(A_cs[..., -1], ((0, 0), (0, 0), (0, 0), (1, 0)))
    chunk_decay = jnp.exp(segsum(A_last))
    new_states = jnp.einsum("bgrzc,bcgrpn->bzgrpn", chunk_decay, states)
    states_in = new_states[:, :-1]
    y_off = jnp.einsum("bclgn,bcgrpn,bgrcl->bclgrp", Cc, states_in, jnp.exp(A_cs))
    return (y_diag + y_off).reshape(b, S, g, r, p)


def forgetting_attention(q, k, v, log_f):
    b, S, h, d = q.shape
    cum = jnp.cumsum(log_f, axis=1).transpose(0, 2, 1)
    scale = d ** -0.5
    outs = []
    for i in range(S // Q_BLOCK):
        qs, qe = i * Q_BLOCK, (i + 1) * Q_BLOCK
        s = jnp.einsum("bqhd,bkhd->bhqk", q[:, qs:qe], k[:, :qe]) * scale
        s = s + cum[:, :, qs:qe, None] - cum[:, :, None, :qe]
        mask = jnp.arange(qs, qe)[:, None] >= jnp.arange(qe)[None, :]
        s = jnp.where(mask, s, -jnp.inf)
        pr = jax.nn.softmax(s, axis=-1)
        outs.append(jnp.einsum("bhqk,bkhd->bqhd", pr, v[:, :qe]))
    return jnp.concatenate(outs, axis=1)


def parallel_mixer(h, w_in, conv_w, conv_b, dt_bias, a_log, d_skip, ssm_norm_w,
                   g_q, g_k, f_bias, w_out):
    b, S, _ = h.shape
    proj = h @ w_in
    sizes = [SSM_D_INNER, CONV_DIM, SSM_HEADS, ATTN_WIDTH, ATTN_WIDTH, ATTN_WIDTH]
    idx = list(np.cumsum(sizes))
    z, xbc, dt_raw, q, k, v, f_raw = jnp.split(proj, idx, axis=-1)
    xbc = jax.nn.silu(causal_depthwise_conv(xbc, conv_w, conv_b)).astype(jnp.float32)
    xs, Bm, Cm = jnp.split(xbc, [SSM_D_INNER, SSM_D_INNER + SSM_GROUPS * SSM_STATE], axis=-1)
    r = SSM_HEADS // SSM_GROUPS
    xs = xs.reshape(b, S, SSM_GROUPS, r, SSM_HEAD_DIM)
    Bm = Bm.reshape(b, S, SSM_GROUPS, SSM_STATE)
    Cm = Cm.reshape(b, S, SSM_GROUPS, SSM_STATE)
    dt = jax.nn.softplus(dt_raw.astype(jnp.float32) + dt_bias.astype(jnp.float32))
    dt = dt.reshape(b, S, SSM_GROUPS, r)
    A = -jnp.exp(a_log.astype(jnp.float32)).reshape(SSM_GROUPS, r)
    y = ssd_chunked(xs, dt, A, Bm, Cm)
    y = y + d_skip.astype(jnp.float32).reshape(SSM_GROUPS, r)[..., None] * xs
    y = y.reshape(b, S, SSM_D_INNER) * jax.nn.silu(z.astype(jnp.float32))
    y = y.reshape(b, S, SSM_GROUPS, SSM_D_INNER // SSM_GROUPS)
    y = y * lax.rsqrt(jnp.mean(y * y, axis=-1, keepdims=True) + EPS)
    y = y.reshape(b, S, SSM_D_INNER) * ssm_norm_w.astype(jnp.float32)
    q = rms_norm(q.astype(jnp.float32).reshape(b, S, ATTN_HEADS, ATTN_HEAD_DIM), g_q)
    k = rms_norm(k.astype(jnp.float32).reshape(b, S, ATTN_HEADS, ATTN_HEAD_DIM), g_k)
    v = v.astype(jnp.float32).reshape(b, S, ATTN_HEADS, ATTN_HEAD_DIM)
    log_f = jax.nn.log_sigmoid(f_raw.astype(jnp.float32) + f_bias.astype(jnp.float32))
    o = forgetting_attention(q, k, v, log_f).reshape(b, S, ATTN_WIDTH)
    mixed = jnp.concatenate([y, o], axis=-1).astype(h.dtype)
    return mixed @ w_out


def memory_cross_attention(h, mem_n, xq_w, xkv_w, xg_q, xg_k, xo_w):
    b, S, _ = h.shape
    q = (h @ xq_w).astype(jnp.float32).reshape(b, S, XATTN_HEADS, XATTN_HEAD_DIM)
    kv = (mem_n @ xkv_w).astype(jnp.float32)
    k, v = jnp.split(kv, 2, axis=-1)
    k = k.reshape(b, MEM_LEN, XATTN_HEADS, XATTN_HEAD_DIM)
    v = v.reshape(b, MEM_LEN, XATTN_HEADS, XATTN_HEAD_DIM)
    q = rms_norm(q, xg_q)
    k = rms_norm(k, xg_k)
    s = jnp.einsum("bqhd,bkhd->bhqk", q, k) * (XATTN_HEAD_DIM ** -0.5)
    pr = jax.nn.softmax(s, axis=-1)
    o = jnp.einsum("bhqk,bkhd->bqhd", pr, v).reshape(b, S, D_MODEL).astype(h.dtype)
    return o @ xo_w


def squared_relu_mlp(h, w_up, w_down):
    u = jax.nn.relu(h @ w_up)
    return (u * u) @ w_down


def _fwd_setup_inputs(seed: int = 0) -> dict:
    key = jax.random.key(seed)
    ks = jax.random.split(key, 24)
    f32 = jnp.float32

    def nrm(k, shape, fan_in):
        return jax.random.normal(k, shape, f32) * (fan_in ** -0.5)

    def gain(k, shape):
        return 1.0 + 0.02 * jax.random.normal(k, shape, f32)

    dt0 = jnp.exp(jax.random.uniform(ks[6], (DEPTH, SSM_HEADS), f32,
                                     math.log(1e-3), math.log(1e-1)))
    dt_bias = dt0 + jnp.log(-jnp.expm1(-dt0))
    return {
        "x": jax.random.normal(ks[0], (BATCH, SEQ, D_MODEL), f32),
        "mem": jax.random.normal(ks[1], (BATCH, MEM_LEN, D_MODEL), f32),
        "g_mix": gain(ks[2], (DEPTH, D_MODEL)),
        "w_in": nrm(ks[3], (DEPTH, D_MODEL, IN_COLS), D_MODEL),
        "conv_w": nrm(ks[4], (DEPTH, CONV_WIDTH, CONV_DIM), CONV_WIDTH),
        "conv_b": 0.02 * jax.random.normal(ks[5], (DEPTH, CONV_DIM), f32),
        "dt_bias": dt_bias,
        "a_log": jnp.log(jax.random.uniform(ks[7], (DEPTH, SSM_HEADS), f32, 1.0, 16.0)),
        "d_skip": gain(ks[8], (DEPTH, SSM_HEADS)),
        "ssm_norm_w": gain(ks[9], (DEPTH, SSM_D_INNER)),
        "g_q": gain(ks[10], (DEPTH, ATTN_HEAD_DIM)),
        "g_k": gain(ks[11], (DEPTH, ATTN_HEAD_DIM)),
        "f_bias": jax.random.uniform(ks[12], (DEPTH, ATTN_HEADS), f32, 2.0, 6.0),
        "w_out": nrm(ks[13], (DEPTH, MIX_WIDTH, D_MODEL), MIX_WIDTH),
        "g_xattn": gain(ks[14], (DEPTH, D_MODEL)),
        "g_mem": gain(ks[15], (DEPTH, D_MODEL)),
        "xq_w": nrm(ks[16], (DEPTH, D_MODEL, D_MODEL), D_MODEL),
        "xkv_w": nrm(ks[17], (DEPTH, D_MODEL, 2 * D_MODEL), D_MODEL),
        "xg_q": gain(ks[18], (DEPTH, XATTN_HEAD_DIM)),
        "xg_k": gain(ks[19], (DEPTH, XATTN_HEAD_DIM)),
        "xo_w": nrm(ks[20], (DEPTH, D_MODEL, D_MODEL), D_MODEL),
        "g_mlp": gain(ks[21], (DEPTH, D_MODEL)),
        "w_up": nrm(ks[22], (DEPTH, D_MODEL, D_FF), D_MODEL),
        "w_down": nrm(ks[23], (DEPTH, D_FF, D_MODEL), D_FF),
    }


def _fwd_reference(x, mem, g_mix, w_in, conv_w, conv_b, dt_bias, a_log, d_skip, ssm_norm_w,
              g_q, g_k, f_bias, w_out, g_xattn, g_mem, xq_w, xkv_w, xg_q, xg_k, xo_w,
              g_mlp, w_up, w_down):
    for l in range(DEPTH):
        h = rms_norm(x, g_mix[l])
        x = x + parallel_mixer(h, w_in[l], conv_w[l], conv_b[l], dt_bias[l], a_log[l],
                               d_skip[l], ssm_norm_w[l], g_q[l], g_k[l], f_bias[l], w_out[l])
        h = rms_norm(x, g_xattn[l])
        mem_n = rms_norm(mem, g_mem[l])
        x = x + memory_cross_attention(h, mem_n, xq_w[l], xkv_w[l], xg_q[l], xg_k[l], xo_w[l])
        h = rms_norm(x, g_mlp[l])
        x = x + squared_relu_mlp(h, w_up[l], w_down[l])
    return x


import jax as _jax
import jax.numpy as _jnp

TWIN_FORMAT = 'train_step'
FWD_PARAMS = ['x', 'mem', 'g_mix', 'w_in', 'conv_w', 'conv_b', 'dt_bias', 'a_log', 'd_skip', 'ssm_norm_w', 'g_q', 'g_k', 'f_bias', 'w_out', 'g_xattn', 'g_mem', 'xq_w', 'xkv_w', 'xg_q', 'xg_k', 'xo_w', 'g_mlp', 'w_up', 'w_down']
TWIN_WEIGHTS = ['g_mix', 'w_in', 'conv_w', 'conv_b', 'dt_bias', 'a_log', 'd_skip', 'ssm_norm_w', 'g_q', 'g_k', 'f_bias', 'w_out', 'g_xattn', 'g_mem', 'xq_w', 'xkv_w', 'xg_q', 'xg_k', 'xo_w', 'g_mlp', 'w_up', 'w_down']
TWIN_DIFF_INPUT = 'x'
TWIN_INPUTS = ['x', 'mem', 'g_mix', 'w_in', 'conv_w', 'conv_b', 'dt_bias', 'a_log', 'd_skip', 'ssm_norm_w', 'g_q', 'g_k', 'f_bias', 'w_out', 'g_xattn', 'g_mem', 'xq_w', 'xkv_w', 'xg_q', 'xg_k', 'xo_w', 'g_mlp', 'w_up', 'w_down', 'loss_target', 'm_g_mix', 'm_w_in', 'm_conv_w', 'm_conv_b', 'm_dt_bias', 'm_a_log', 'm_d_skip', 'm_ssm_norm_w', 'm_g_q', 'm_g_k', 'm_f_bias', 'm_w_out', 'm_g_xattn', 'm_g_mem', 'm_xq_w', 'm_xkv_w', 'm_xg_q', 'm_xg_k', 'm_xo_w', 'm_g_mlp', 'm_w_up', 'm_w_down', 'v_g_mix', 'v_w_in', 'v_conv_w', 'v_conv_b', 'v_dt_bias', 'v_a_log', 'v_d_skip', 'v_ssm_norm_w', 'v_g_q', 'v_g_k', 'v_f_bias', 'v_w_out', 'v_g_xattn', 'v_g_mem', 'v_xq_w', 'v_xkv_w', 'v_xg_q', 'v_xg_k', 'v_xo_w', 'v_g_mlp', 'v_w_up', 'v_w_down']
TWIN_OUTPUTS = ['loss', 'grad_x', 'grad_g_mix', 'grad_w_in', 'grad_conv_w', 'grad_conv_b', 'grad_dt_bias', 'grad_a_log', 'grad_d_skip', 'grad_ssm_norm_w', 'grad_g_q', 'grad_g_k', 'grad_f_bias', 'grad_w_out', 'grad_g_xattn', 'grad_g_mem', 'grad_xq_w', 'grad_xkv_w', 'grad_xg_q', 'grad_xg_k', 'grad_xo_w', 'grad_g_mlp', 'grad_w_up', 'grad_w_down', 'delta_g_mix', 'delta_w_in', 'delta_conv_w', 'delta_conv_b', 'delta_dt_bias', 'delta_a_log', 'delta_d_skip', 'delta_ssm_norm_w', 'delta_g_q', 'delta_g_k', 'delta_f_bias', 'delta_w_out', 'delta_g_xattn', 'delta_g_mem', 'delta_xq_w', 'delta_xkv_w', 'delta_xg_q', 'delta_xg_k', 'delta_xo_w', 'delta_g_mlp', 'delta_w_up', 'delta_w_down', 'new_m_g_mix', 'new_m_w_in', 'new_m_conv_w', 'new_m_conv_b', 'new_m_dt_bias', 'new_m_a_log', 'new_m_d_skip', 'new_m_ssm_norm_w', 'new_m_g_q', 'new_m_g_k', 'new_m_f_bias', 'new_m_w_out', 'new_m_g_xattn', 'new_m_g_mem', 'new_m_xq_w', 'new_m_xkv_w', 'new_m_xg_q', 'new_m_xg_k', 'new_m_xo_w', 'new_m_g_mlp', 'new_m_w_up', 'new_m_w_down', 'new_v_g_mix', 'new_v_w_in', 'new_v_conv_w', 'new_v_conv_b', 'new_v_dt_bias', 'new_v_a_log', 'new_v_d_skip', 'new_v_ssm_norm_w', 'new_v_g_q', 'new_v_g_k', 'new_v_f_bias', 'new_v_w_out', 'new_v_g_xattn', 'new_v_g_mem', 'new_v_xq_w', 'new_v_xkv_w', 'new_v_xg_q', 'new_v_xg_k', 'new_v_xo_w', 'new_v_g_mlp', 'new_v_w_up', 'new_v_w_down']
TWIN_LEAF_KINDS = {'loss': 'loss', 'grad_x': 'grad_x', 'grad_g_mix': 'grad_w', 'grad_w_in': 'grad_w', 'grad_conv_w': 'grad_w', 'grad_conv_b': 'grad_w', 'grad_dt_bias': 'grad_w', 'grad_a_log': 'grad_w', 'grad_d_skip': 'grad_w', 'grad_ssm_norm_w': 'grad_w', 'grad_g_q': 'grad_w', 'grad_g_k': 'grad_w', 'grad_f_bias': 'grad_w', 'grad_w_out': 'grad_w', 'grad_g_xattn': 'grad_w', 'grad_g_mem': 'grad_w', 'grad_xq_w': 'grad_w', 'grad_xkv_w': 'grad_w', 'grad_xg_q': 'grad_w', 'grad_xg_k': 'grad_w', 'grad_xo_w': 'grad_w', 'grad_g_mlp': 'grad_w', 'grad_w_up': 'grad_w', 'grad_w_down': 'grad_w', 'delta_g_mix': 'delta_w', 'delta_w_in': 'delta_w', 'delta_conv_w': 'delta_w', 'delta_conv_b': 'delta_w', 'delta_dt_bias': 'delta_w', 'delta_a_log': 'delta_w', 'delta_d_skip': 'delta_w', 'delta_ssm_norm_w': 'delta_w', 'delta_g_q': 'delta_w', 'delta_g_k': 'delta_w', 'delta_f_bias': 'delta_w', 'delta_w_out': 'delta_w', 'delta_g_xattn': 'delta_w', 'delta_g_mem': 'delta_w', 'delta_xq_w': 'delta_w', 'delta_xkv_w': 'delta_w', 'delta_xg_q': 'delta_w', 'delta_xg_k': 'delta_w', 'delta_xo_w': 'delta_w', 'delta_g_mlp': 'delta_w', 'delta_w_up': 'delta_w', 'delta_w_down': 'delta_w', 'new_m_g_mix': 'new_m', 'new_m_w_in': 'new_m', 'new_m_conv_w': 'new_m', 'new_m_conv_b': 'new_m', 'new_m_dt_bias': 'new_m', 'new_m_a_log': 'new_m', 'new_m_d_skip': 'new_m', 'new_m_ssm_norm_w': 'new_m', 'new_m_g_q': 'new_m', 'new_m_g_k': 'new_m', 'new_m_f_bias': 'new_m', 'new_m_w_out': 'new_m', 'new_m_g_xattn': 'new_m', 'new_m_g_mem': 'new_m', 'new_m_xq_w': 'new_m', 'new_m_xkv_w': 'new_m', 'new_m_xg_q': 'new_m', 'new_m_xg_k': 'new_m', 'new_m_xo_w': 'new_m', 'new_m_g_mlp': 'new_m', 'new_m_w_up': 'new_m', 'new_m_w_down': 'new_m', 'new_v_g_mix': 'new_v', 'new_v_w_in': 'new_v', 'new_v_conv_w': 'new_v', 'new_v_conv_b': 'new_v', 'new_v_dt_bias': 'new_v', 'new_v_a_log': 'new_v', 'new_v_d_skip': 'new_v', 'new_v_ssm_norm_w': 'new_v', 'new_v_g_q': 'new_v', 'new_v_g_k': 'new_v', 'new_v_f_bias': 'new_v', 'new_v_w_out': 'new_v', 'new_v_g_xattn': 'new_v', 'new_v_g_mem': 'new_v', 'new_v_xq_w': 'new_v', 'new_v_xkv_w': 'new_v', 'new_v_xg_q': 'new_v', 'new_v_xg_k': 'new_v', 'new_v_xo_w': 'new_v', 'new_v_g_mlp': 'new_v', 'new_v_w_up': 'new_v', 'new_v_w_down': 'new_v'}


def _forward(args):
    return _fwd_reference(*[args[k] for k in FWD_PARAMS])


def _output_shape():
    out = _jax.eval_shape(lambda: _forward(_fwd_setup_inputs(0)))
    return out.shape, out.dtype

N_MICROBATCH = 1
ADAM_LR = 0.001
ADAM_B1 = 0.9
ADAM_B2 = 0.999
ADAM_EPS = 1e-08
ADAM_WD = 0.01
ADAM_STEP = 10
PER_EXAMPLE_BATCH_AXIS = {'x': 0, 'mem': 0, 'loss_target': 0}
SHARED_INPUTS = []
_WEIGHT_DTYPES = {'g_mix': _jnp.float32, 'w_in': _jnp.float32, 'conv_w': _jnp.float32, 'conv_b': _jnp.float32, 'dt_bias': _jnp.float32, 'a_log': _jnp.float32, 'd_skip': _jnp.float32, 'ssm_norm_w': _jnp.float32, 'g_q': _jnp.float32, 'g_k': _jnp.float32, 'f_bias': _jnp.float32, 'w_out': _jnp.float32, 'g_xattn': _jnp.float32, 'g_mem': _jnp.float32, 'xq_w': _jnp.float32, 'xkv_w': _jnp.float32, 'xg_q': _jnp.float32, 'xg_k': _jnp.float32, 'xo_w': _jnp.float32, 'g_mlp': _jnp.float32, 'w_up': _jnp.float32, 'w_down': _jnp.float32}
MOMENT_SCALE = {'g_mix': 7.209094e-01, 'w_in': 2.532024e-01, 'conv_w': 1.105619e+00, 'conv_b': 3.799608e+00, 'dt_bias': 1.419540e+00, 'a_log': 6.839252e+00, 'd_skip': 4.108299e+00, 'ssm_norm_w': 1.218748e+01, 'g_q': 3.513404e+00, 'g_k': 3.527393e+00, 'f_bias': 1.084669e+01, 'w_out': 2.073698e+00, 'g_xattn': 8.318145e-02, 'g_mem': 4.798193e-01, 'xq_w': 8.391121e-02, 'xkv_w': 3.091495e-01, 'xg_q': 6.546346e-01, 'xg_k': 6.561084e-01, 'xo_w': 4.379692e-01, 'g_mlp': 4.844985e+01, 'w_up': 9.232363e-01, 'w_down': 4.364625e+00}


def _to_microbatches(a, axis):
    t = _jnp.moveaxis(a, axis, 0)
    t = t.reshape((N_MICROBATCH, t.shape[0] // N_MICROBATCH) + t.shape[1:])
    return _jnp.moveaxis(t, 1, axis + 1)


def setup_inputs(seed: int = 0) -> dict:
    inp = _fwd_setup_inputs(seed)
    key = _jax.random.fold_in(_jax.random.key(seed), 7919)
    shape, _ = _output_shape()
    out = dict(inp)
    out["loss_target"] = _jax.random.normal(_jax.random.fold_in(key, 0), shape, _jnp.float32)
    for i, name in enumerate(TWIN_WEIGHTS):
        w = inp[name].astype(_jnp.float32)
        if MOMENT_SCALE is None:
            s = _jnp.sqrt(_jnp.mean(_jnp.square(w)) + 1e-30)
        else:
            s = MOMENT_SCALE[name]
        km, kv = _jax.random.split(_jax.random.fold_in(key, i + 1))
        out[name] = w
        out["m_" + name] = s * _jax.random.normal(km, w.shape, _jnp.float32)
        out["v_" + name] = (s * s) * _jax.random.uniform(kv, w.shape, _jnp.float32, 0.5, 1.5)
    if N_MICROBATCH > 1:
        for name, axis in PER_EXAMPLE_BATCH_AXIS.items():
            out[name] = _to_microbatches(out[name], axis)
    return {'x': out['x'], 'mem': out['mem'], 'g_mix': out['g_mix'], 'w_in': out['w_in'], 'conv_w': out['conv_w'], 'conv_b': out['conv_b'], 'dt_bias': out['dt_bias'], 'a_log': out['a_log'], 'd_skip': out['d_skip'], 'ssm_norm_w': out['ssm_norm_w'], 'g_q': out['g_q'], 'g_k': out['g_k'], 'f_bias': out['f_bias'], 'w_out': out['w_out'], 'g_xattn': out['g_xattn'], 'g_mem': out['g_mem'], 'xq_w': out['xq_w'], 'xkv_w': out['xkv_w'], 'xg_q': out['xg_q'], 'xg_k': out['xg_k'], 'xo_w': out['xo_w'], 'g_mlp': out['g_mlp'], 'w_up': out['w_up'], 'w_down': out['w_down'], 'loss_target': out['loss_target'], 'm_g_mix': out['m_g_mix'], 'm_w_in': out['m_w_in'], 'm_conv_w': out['m_conv_w'], 'm_conv_b': out['m_conv_b'], 'm_dt_bias': out['m_dt_bias'], 'm_a_log': out['m_a_log'], 'm_d_skip': out['m_d_skip'], 'm_ssm_norm_w': out['m_ssm_norm_w'], 'm_g_q': out['m_g_q'], 'm_g_k': out['m_g_k'], 'm_f_bias': out['m_f_bias'], 'm_w_out': out['m_w_out'], 'm_g_xattn': out['m_g_xattn'], 'm_g_mem': out['m_g_mem'], 'm_xq_w': out['m_xq_w'], 'm_xkv_w': out['m_xkv_w'], 'm_xg_q': out['m_xg_q'], 'm_xg_k': out['m_xg_k'], 'm_xo_w': out['m_xo_w'], 'm_g_mlp': out['m_g_mlp'], 'm_w_up': out['m_w_up'], 'm_w_down': out['m_w_down'], 'v_g_mix': out['v_g_mix'], 'v_w_in': out['v_w_in'], 'v_conv_w': out['v_conv_w'], 'v_conv_b': out['v_conv_b'], 'v_dt_bias': out['v_dt_bias'], 'v_a_log': out['v_a_log'], 'v_d_skip': out['v_d_skip'], 'v_ssm_norm_w': out['v_ssm_norm_w'], 'v_g_q': out['v_g_q'], 'v_g_k': out['v_g_k'], 'v_f_bias': out['v_f_bias'], 'v_w_out': out['v_w_out'], 'v_g_xattn': out['v_g_xattn'], 'v_g_mem': out['v_g_mem'], 'v_xq_w': out['v_xq_w'], 'v_xkv_w': out['v_xkv_w'], 'v_xg_q': out['v_xg_q'], 'v_xg_k': out['v_xg_k'], 'v_xo_w': out['v_xo_w'], 'v_g_mlp': out['v_g_mlp'], 'v_w_up': out['v_w_up'], 'v_w_down': out['v_w_down']}


def _loss(weights, diff, rest, loss_target):
    with _jax.named_scope("forward"):
        args = {**rest, TWIN_DIFF_INPUT: diff, **{k: w.astype(_WEIGHT_DTYPES[k]) for k, w in weights.items()}}
        y = _forward(args)
    with _jax.named_scope("loss_head"):
        err = _jnp.square(y.astype(_jnp.float32) - loss_target)
        return 0.5 * _jnp.sum(_jnp.mean(err, axis=-1)) if err.ndim else 0.5 * err


def _adamw(w, g, m, v):
    m = ADAM_B1 * m + (1.0 - ADAM_B1) * g
    v = ADAM_B2 * v + (1.0 - ADAM_B2) * _jnp.square(g)
    m_hat = m / (1.0 - ADAM_B1 ** ADAM_STEP)
    v_hat = v / (1.0 - ADAM_B2 ** ADAM_STEP)
    delta = -ADAM_LR * (m_hat / (_jnp.sqrt(v_hat) + ADAM_EPS) + ADAM_WD * w)
    return delta, m, v


def reference(x, mem, g_mix, w_in, conv_w, conv_b, dt_bias, a_log, d_skip, ssm_norm_w, g_q, g_k, f_bias, w_out, g_xattn, g_mem, xq_w, xkv_w, xg_q, xg_k, xo_w, g_mlp, w_up, w_down, loss_target, m_g_mix, m_w_in, m_conv_w, m_conv_b, m_dt_bias, m_a_log, m_d_skip, m_ssm_norm_w, m_g_q, m_g_k, m_f_bias, m_w_out, m_g_xattn, m_g_mem, m_xq_w, m_xkv_w, m_xg_q, m_xg_k, m_xo_w, m_g_mlp, m_w_up, m_w_down, v_g_mix, v_w_in, v_conv_w, v_conv_b, v_dt_bias, v_a_log, v_d_skip, v_ssm_norm_w, v_g_q, v_g_k, v_f_bias, v_w_out, v_g_xattn, v_g_mem, v_xq_w, v_xkv_w, v_xg_q, v_xg_k, v_xo_w, v_g_mlp, v_w_up, v_w_down):
    given = dict(x=x, mem=mem, g_mix=g_mix, w_in=w_in, conv_w=conv_w, conv_b=conv_b, dt_bias=dt_bias, a_log=a_log, d_skip=d_skip, ssm_norm_w=ssm_norm_w, g_q=g_q, g_k=g_k, f_bias=f_bias, w_out=w_out, g_xattn=g_xattn, g_mem=g_mem, xq_w=xq_w, xkv_w=xkv_w, xg_q=xg_q, xg_k=xg_k, xo_w=xo_w, g_mlp=g_mlp, w_up=w_up, w_down=w_down, loss_target=loss_target, m_g_mix=m_g_mix, m_w_in=m_w_in, m_conv_w=m_conv_w, m_conv_b=m_conv_b, m_dt_bias=m_dt_bias, m_a_log=m_a_log, m_d_skip=m_d_skip, m_ssm_norm_w=m_ssm_norm_w, m_g_q=m_g_q, m_g_k=m_g_k, m_f_bias=m_f_bias, m_w_out=m_w_out, m_g_xattn=m_g_xattn, m_g_mem=m_g_mem, m_xq_w=m_xq_w, m_xkv_w=m_xkv_w, m_xg_q=m_xg_q, m_xg_k=m_xg_k, m_xo_w=m_xo_w, m_g_mlp=m_g_mlp, m_w_up=m_w_up, m_w_down=m_w_down, v_g_mix=v_g_mix, v_w_in=v_w_in, v_conv_w=v_conv_w, v_conv_b=v_conv_b, v_dt_bias=v_dt_bias, v_a_log=v_a_log, v_d_skip=v_d_skip, v_ssm_norm_w=v_ssm_norm_w, v_g_q=v_g_q, v_g_k=v_g_k, v_f_bias=v_f_bias, v_w_out=v_w_out, v_g_xattn=v_g_xattn, v_g_mem=v_g_mem, v_xq_w=v_xq_w, v_xkv_w=v_xkv_w, v_xg_q=v_xg_q, v_xg_k=v_xg_k, v_xo_w=v_xo_w, v_g_mlp=v_g_mlp, v_w_up=v_w_up, v_w_down=v_w_down)
    weights = {n: given[n] for n in TWIN_WEIGHTS}
    shared = {n: given[n] for n in SHARED_INPUTS}
    per_example = {n: given[n] for n in ['x', 'mem']}
    grad_fn = _jax.value_and_grad(_loss, argnums=(0, 1))

    def one_microbatch(ex, loss_target):
        ex = dict(ex)
        diff = ex.pop(TWIN_DIFF_INPUT)
        return grad_fn(weights, diff, {**shared, **ex}, loss_target)

    if N_MICROBATCH == 1:
        loss, (grad_w, grad_x) = one_microbatch(per_example, given["loss_target"])
    else:
        def body(carry, xs):
            loss_sum, grad_sum = carry
            l_k, (gw_k, gx_k) = one_microbatch(xs[0], xs[1])
            with _jax.named_scope("update"):
                return (loss_sum + l_k, _jax.tree.map(_jnp.add, grad_sum, gw_k)), gx_k

        init = (_jnp.zeros((), _jnp.float32), _jax.tree.map(_jnp.zeros_like, weights))
        (loss, grad_w), grad_x = _jax.lax.scan(body, init, (per_example, given["loss_target"]))
    with _jax.named_scope("update"):
        delta_w, new_m, new_v = {}, {}, {}
        for n in TWIN_WEIGHTS:
            delta_w[n], new_m[n], new_v[n] = _adamw(weights[n], grad_w[n], given["m_" + n], given["v_" + n])
    return (loss, grad_x, *[grad_w[n] for n in TWIN_WEIGHTS], *[delta_w[n] for n in TWIN_WEIGHTS],
            *[new_m[n] for n in TWIN_WEIGHTS], *[new_v[n] for n in TWIN_WEIGHTS])
```

```python
import functools
import math

import jax
import jax.numpy as jnp
from jax import lax
from jax.experimental import pallas as pl
from jax.experimental.pallas import tpu as pltpu

F32, BF16 = jnp.float32, jnp.bfloat16
SDS = jax.ShapeDtypeStruct
HI = lax.Precision.HIGHEST
MESH = pl.DeviceIdType.MESH

N_DEV = 8
EPS = 1e-5
D_MODEL = 1024
SSM_HEADS, SSM_P, SSM_N, SSM_GROUPS, CHUNK = 16, 64, 128, 2, 128
ATT_HEADS, ATT_D = 16, 64
X_HEADS, X_D = 4, 256
LANES = 128
VMEM_LIMIT = 48 * 1024 * 1024
NEG = -1e30

ADAM_LR, ADAM_B1, ADAM_B2, ADAM_EPS, ADAM_WD, ADAM_STEP = 0.001, 0.9, 0.999, 1e-08, 0.01, 10

C_Z, C_XS, C_Q, C_K, C_V, C_B, C_C, C_DTF, P_COLS = 0, 1024, 2048, 3072, 4096, 5120, 5376, 5632, 5760

_NN = (((1,), (0,)), ((), ()))
_NT = (((1,), (1,)), ((), ()))
_TN = (((0,), (0,)), ((), ()))


def _cparams(**kw):
    return pltpu.CompilerParams(vmem_limit_bytes=VMEM_LIMIT, **kw)


def _bdot(a, b, dn):
    return lax.dot_general(a.astype(BF16), b.astype(BF16), dn, preferred_element_type=F32)


@jax.custom_vjp
def mm_nn(a, b):
    return _bdot(a, b, _NN)


mm_nn.defvjp(lambda a, b: (mm_nn(a, b), (a, b)), lambda r, g: (_bdot(g, r[1], _NT), _bdot(r[0], g, _TN)))


@jax.custom_vjp
def mm_nt(a, b):
    return _bdot(a, b, _NT)


mm_nt.defvjp(lambda a, b: (mm_nt(a, b), (a, b)), lambda r, g: (_bdot(g, r[1], _NN), _bdot(g, r[0], _TN)))


@jax.custom_vjp
def mm_tn(a, b):
    return _bdot(a, b, _TN)


mm_tn.defvjp(lambda a, b: (mm_tn(a, b), (a, b)), lambda r, g: (_bdot(r[1], g, _NT), _bdot(r[0], g, _NN)))


def _cdot(x, c):
    return jnp.dot(x, c, precision=HI, preferred_element_type=F32)


def _iota(shape, dim):
    return lax.broadcasted_iota(jnp.int32, shape, dim)


def _matmul(a, b, *, mode, tm, tn, tk, name, out_dtype=F32, add=None):
    if mode == "tn":
        K, M = a.shape
    else:
        M, K = a.shape
    N = b.shape[0] if mode == "nt" else b.shape[1]
    tm, tn, tk = min(tm, M), min(tn, N), min(tk, K)
    assert M % tm == 0 and N % tn == 0 and K % tk == 0, (name, M, N, K, tm, tn, tk)
    nk = K // tk
    dn = {"nn": _NN, "nt": _NT, "tn": _TN}[mode]

    def body(*refs):
        if add is None:
            a_ref, b_ref, o_ref, acc_ref = refs
        else:
            a_ref, b_ref, add_ref, o_ref, acc_ref = refs
        k = pl.program_id(2)

        @pl.when(k == 0)
        def _():
            acc_ref[...] = jnp.zeros_like(acc_ref)

        acc_ref[...] += _bdot(a_ref[...], b_ref[...], dn)

        @pl.when(k == nk - 1)
        def _():
            o = acc_ref[...]
            if add is not None:
                o = o + add_ref[...]
            o_ref[...] = o.astype(o_ref.dtype)

    a_spec = pl.BlockSpec((tk, tm), lambda i, j, k: (k, i)) if mode == "tn" else pl.BlockSpec((tm, tk), lambda i, j, k: (i, k))
    b_spec = pl.BlockSpec((tn, tk), lambda i, j, k: (j, k)) if mode == "nt" else pl.BlockSpec((tk, tn), lambda i, j, k: (k, j))
    o_spec = pl.BlockSpec((tm, tn), lambda i, j, k: (i, j))
    in_specs, args = [a_spec, b_spec], [a, b]
    if add is not None:
        in_specs.append(o_spec)
        args.append(add)
    return pl.pallas_call(
        body, name=name, grid=(M // tm, N // tn, nk), in_specs=in_specs, out_specs=o_spec,
        out_shape=SDS((M, N), out_dtype), scratch_shapes=[pltpu.VMEM((tm, tn), F32)],
        compiler_params=_cparams(dimension_semantics=("parallel", "parallel", "arbitrary")),
    )(*args)


def _row_spec(tm, spec):
    _, c0, w = spec
    assert c0 % w == 0
    return pl.BlockSpec((tm, w), functools.partial(lambda i, cb: (i, cb), cb=c0 // w))


def _par_spec(spec):
    arr, c0, w = spec
    assert c0 % w == 0
    return pl.BlockSpec((arr.shape[0], w), functools.partial(lambda i, cb: (0, cb), cb=c0 // w))


def _whole(arr):
    return (arr, 0, arr.shape[1])


def _rw_fwd(fn, rows, params, outs, *, tm, name):
    M = rows[0][0].shape[0]
    nr, npar = len(rows), len(params)

    def body(*refs):
        rv = [r[...].astype(F32) for r in refs[:nr]]
        pv = [p[...].astype(F32) for p in refs[nr:nr + npar]]
        res = fn(*rv, *pv)
        for o_ref, v in zip(refs[nr + npar:], res, strict=True):
            o_ref[...] = v.astype(o_ref.dtype)

    return pl.pallas_call(
        body, name=name, grid=(M // tm,),
        in_specs=[_row_spec(tm, r) for r in rows] + [_par_spec(p) for p in params],
        out_specs=[pl.BlockSpec((tm, w), lambda i: (i, 0)) for w, _ in outs],
        out_shape=[SDS((M, w), dt) for w, dt in outs],
        compiler_params=_cparams(dimension_semantics=("parallel",)),
    )(*[r[0] for r in rows], *[p[0] for p in params])


def _rw_bwd(fn, rows, params, cts, *, tm, name, row_grads, adds=None):
    M = rows[0][0].shape[0]
    adds = adds or {}
    nr, npar, nc = len(rows), len(params), len(cts)
    add_keys = sorted(adds)
    want = [k for k in range(nr) if row_grads[k] is not None]

    def body(*refs):
        pos = 0
        r_refs = refs[pos:pos + nr]; pos += nr
        p_refs = refs[pos:pos + npar]; pos += npar
        c_refs = refs[pos:pos + nc]; pos += nc
        a_refs = dict(zip(add_keys, refs[pos:pos + len(add_keys)])); pos += len(add_keys)
        dr_refs = dict(zip(want, refs[pos:pos + len(want)])); pos += len(want)
        dp_refs = refs[pos:pos + npar]
        rv = [r[...].astype(F32) for r in r_refs]
        pv = [p[...].astype(F32) for p in p_refs]
        _, vjp = jax.vjp(fn, *rv, *pv)
        g = vjp(tuple(c[...].astype(F32) for c in c_refs))
        for k in want:
            v = g[k]
            if k in a_refs:
                v = v + a_refs[k][...].astype(F32)
            dr_refs[k][...] = v.astype(dr_refs[k].dtype)
        first = pl.program_id(0) == 0
        for j in range(npar):
            @pl.when(first)
            def _(j=j):
                dp_refs[j][...] = jnp.zeros_like(dp_refs[j])
            dp_refs[j][...] += g[nr + j]

    res = pl.pallas_call(
        body, name=name, grid=(M // tm,),
        in_specs=([_row_spec(tm, r) for r in rows] + [_par_spec(p) for p in params] + [_row_spec(tm, c) for c in cts]
                  + [_row_spec(tm, adds[k]) for k in add_keys]),
        out_specs=([pl.BlockSpec((tm, rows[k][2]), lambda i: (i, 0)) for k in want]
                   + [pl.BlockSpec((p[0].shape[0], p[2]), lambda i: (0, 0)) for p in params]),
        out_shape=([SDS((M, rows[k][2]), row_grads[k]) for k in want] + [SDS((p[0].shape[0], p[2]), F32) for p in params]),
        compiler_params=_cparams(dimension_semantics=("arbitrary",)),
    )(*[r[0] for r in rows], *[p[0] for p in params], *[c[0] for c in cts], *[adds[k][0] for k in add_keys])
    return res


def _rms_fn(x, g):
    r = lax.rsqrt(jnp.mean(x * x, axis=-1, keepdims=True) + EPS)
    return (x * r * g,)


def _seg_mats(width, seg):
    n = width // seg
    p = (_iota((width, n), 0) // seg == _iota((width, n), 1)).astype(F32)
    e = (_iota((n, width), 1) // seg == _iota((n, width), 0)).astype(F32)
    return p, e


def _headnorm_fn(q, g):
    p, e = _seg_mats(ATT_HEADS * ATT_D, ATT_D)
    t = (_iota((ATT_D, ATT_HEADS * ATT_D), 1) % ATT_D == _iota((ATT_D, ATT_HEADS * ATT_D), 0)).astype(F32)
    ms = _cdot(q * q, p) * (1.0 / ATT_D)
    r = _cdot(lax.rsqrt(ms + EPS), e)
    return (q * r * _cdot(g, t),)


def _gate_fn(y, xs, z, dskip, w):
    width = SSM_HEADS * SSM_P
    _, e = _seg_mats(width, SSM_P)
    y = (y + _cdot(dskip, e) * xs) * (z * jax.nn.sigmoid(z))
    g0 = _iota((1, width), 1) < width // SSM_GROUPS
    y2 = y * y
    gw = width // SSM_GROUPS
    ms0 = jnp.sum(jnp.where(g0, y2, 0.0), axis=-1, keepdims=True) * (1.0 / gw)
    ms1 = jnp.sum(jnp.where(g0, 0.0, y2), axis=-1, keepdims=True) * (1.0 / gw)
    r = jnp.where(g0, lax.rsqrt(ms0 + EPS), lax.rsqrt(ms1 + EPS))
    return (y * r * w,)


def _relu2_fn(a):
    u = jax.nn.relu(a)
    return (u * u,)


def _xattn_fn(q0, q1, q2, q3, k0, k1, k2, k3, v0, v1, v2, v3, gq, gk):
    def norm(u, g):
        return u * lax.rsqrt(jnp.mean(u * u, axis=-1, keepdims=True) + EPS) * g
    outs = []
    for q, k, v in ((q0, k0, v0), (q1, k1, v1), (q2, k2, v2), (q3, k3, v3)):
        s = mm_nt(norm(q, gq), norm(k, gk)) * (X_D ** -0.5)
        p = jnp.exp(s - lax.stop_gradient(jnp.max(s, axis=-1, keepdims=True)))
        p = p / jnp.sum(p, axis=-1, keepdims=True)
        outs.append(mm_nn(p, v))
    return (jnp.concatenate(outs, axis=-1),)


CONV_TC = 256


def _shift_down(u, k):
    if k == 0:
        return u
    return jnp.where(_iota(u.shape, 0) >= k, pltpu.roll(u, k, axis=0), 0.0)


def _shift_up(u, k):
    if k == 0:
        return u
    n = u.shape[0]
    return jnp.where(_iota(u.shape, 0) < n - k, pltpu.roll(u, n - k, axis=0), 0.0)


def _conv_pre(u, w_ref, b):
    pre = b + w_ref[3:4, :] * u
    for k in (1, 2, 3):
        pre = pre + w_ref[3 - k:4 - k, :] * _shift_down(u, k)
    return pre


def _conv_fwd(src, c0, width, w, b, *, name):
    S = src.shape[0]
    cb0 = c0 // CONV_TC

    def body(u_ref, w_ref, b_ref, o_ref):
        pre = _conv_pre(u_ref[...], w_ref, b_ref[...])
        o_ref[...] = pre * jax.nn.sigmoid(pre)

    return pl.pallas_call(
        body, name=name, grid=(width // CONV_TC,),
        in_specs=[pl.BlockSpec((S, CONV_TC), lambda j: (0, cb0 + j)), pl.BlockSpec((4, CONV_TC), lambda j: (0, j)),
                  pl.BlockSpec((1, CONV_TC), lambda j: (0, j))],
        out_specs=pl.BlockSpec((S, CONV_TC), lambda j: (0, j)), out_shape=SDS((S, width), F32),
        compiler_params=_cparams(dimension_semantics=("parallel",)),
    )(src, w, b)


def _conv_bwd(src, c0, width, w, b, douts, *, name):
    S = src.shape[0]
    cb0 = c0 // CONV_TC
    nd = len(douts)

    def body(*refs):
        u_ref, w_ref, b_ref = refs[:3]
        d_refs = refs[3:3 + nd]
        du_ref, dw_ref, db_ref = refs[3 + nd:]
        u = u_ref[...]
        pre = _conv_pre(u, w_ref, b_ref[...])
        sg = jax.nn.sigmoid(pre)
        dout = d_refs[0][...]
        for r in d_refs[1:]:
            dout = dout + r[...]
        dpre = dout * (sg * (1.0 + pre * (1.0 - sg)))
        du = w_ref[3:4, :] * dpre
        dw_ref[3:4, :] = jnp.sum(dpre * u, axis=0, keepdims=True)
        for k in (1, 2, 3):
            du = du + w_ref[3 - k:4 - k, :] * _shift_up(dpre, k)
            dw_ref[3 - k:4 - k, :] = jnp.sum(dpre * _shift_down(u, k), axis=0, keepdims=True)
        du_ref[...] = du.astype(du_ref.dtype)
        db_ref[...] = jnp.sum(dpre, axis=0, keepdims=True)

    return pl.pallas_call(
        body, name=name, grid=(width // CONV_TC,),
        in_specs=[pl.BlockSpec((S, CONV_TC), lambda j: (0, cb0 + j)), pl.BlockSpec((4, CONV_TC), lambda j: (0, j)),
                  pl.BlockSpec((1, CONV_TC), lambda j: (0, j))] + [pl.BlockSpec((S, CONV_TC), lambda j: (0, j))] * nd,
        out_specs=[pl.BlockSpec((S, CONV_TC), lambda j: (0, j)), pl.BlockSpec((4, CONV_TC), lambda j: (0, j)),
                   pl.BlockSpec((1, CONV_TC), lambda j: (0, j))],
        out_shape=[SDS((S, width), BF16), SDS((4, width), F32), SDS((1, width), F32)],
        compiler_params=_cparams(dimension_semantics=("parallel",)),
    )(src, w, b, *douts)


def _softplus(x):
    return jnp.maximum(x, 0.0) + jnp.log(1.0 + jnp.exp(-jnp.abs(x)))


def _prefix_sum(x, seg):
    n = x.shape[1]
    pos = _iota(x.shape, 1) % seg
    k = 1
    while k < seg:
        x = x + jnp.where(pos >= k, pltpu.roll(x, k, axis=1), 0.0)
        k *= 2
    return x


def _suffix_sum(x, seg):
    n = x.shape[1]
    pos = _iota(x.shape, 1) % seg
    k = 1
    while k < seg:
        x = x + jnp.where(pos + k < seg, pltpu.roll(x, n - k, axis=1), 0.0)
        k *= 2
    return x


def _dtf_fwd(dtf_t, dt_bias, a_log, f_bias):
    S = dtf_t.shape[1]

    def body(x_ref, db_ref, al_ref, fb_ref, dt_ref, acs_ref, cum_ref):
        dt = _softplus(x_ref[0:16, :] + db_ref[...])
        dt_ref[...] = dt
        acs_ref[...] = _prefix_sum(dt * (-jnp.exp(al_ref[...])), CHUNK)
        cum_ref[...] = _prefix_sum(-_softplus(-(x_ref[16:32, :] + fb_ref[...])), S)

    return pl.pallas_call(body, name="dtf_fwd", out_shape=[SDS((16, S), F32)] * 3, compiler_params=_cparams())(
        dtf_t, dt_bias, a_log, f_bias)


def _dtf_bwd(dtf_t, dt_bias, a_log, f_bias, d_dt, d_acs_a, d_acs_b, d_cum):
    S = dtf_t.shape[1]

    def body(x_ref, db_ref, al_ref, fb_ref, ddt_ref, da1_ref, da2_ref, dc_ref, dx_ref, ddb_ref, dal_ref, dfb_ref):
        xd = x_ref[0:16, :] + db_ref[...]
        dt = _softplus(xd)
        a = -jnp.exp(al_ref[...])
        d_da = _suffix_sum(da1_ref[...] + da2_ref[...], CHUNK)
        d_dt = ddt_ref[...] + d_da * a
        dal_ref[...] = jnp.sum(d_da * dt, axis=1, keepdims=True) * a
        d_xd = d_dt * jax.nn.sigmoid(xd)
        ddb_ref[...] = jnp.sum(d_xd, axis=1, keepdims=True)
        xf = x_ref[16:32, :] + fb_ref[...]
        d_xf = _suffix_sum(dc_ref[...], S) * jax.nn.sigmoid(-xf)
        dfb_ref[...] = jnp.sum(d_xf, axis=1, keepdims=True)
        dx_ref[0:16, :] = d_xd
        dx_ref[16:32, :] = d_xf

    return pl.pallas_call(body, name="dtf_bwd", out_shape=[SDS((32, S), F32)] + [SDS((16, 1), F32)] * 3,
                          compiler_params=_cparams())(dtf_t, dt_bias, a_log, f_bias, d_dt, d_acs_a, d_acs_b, d_cum)


def _ssd_chunk(xs, dtc, acol, arow, bm, cm, h, *, hp):
    L = CHUNK
    first = _iota((1, LANES), 1) < SSM_P
    i16, s16 = _iota((L, 16), 1), _iota((16, L), 0)
    ha, hb = 2 * hp, 2 * hp + 1

    def selc(blk, hh):
        return jnp.sum(jnp.where(i16 == hh, blk, 0.0), axis=1, keepdims=True)

    def selr(blk, hh):
        return jnp.sum(jnp.where(s16 == hh, blk, 0.0), axis=0, keepdims=True)

    x = xs * jnp.where(first, selc(dtc, ha), selc(dtc, hb))
    ca, cb, ra, rb = selc(acol, ha), selc(acol, hb), selr(arow, ha), selr(arow, hb)
    tri = _iota((L, L), 0) >= _iota((L, L), 1)
    cbm = mm_nt(cm, bm)
    la = jnp.exp(jnp.where(tri, ca - ra, NEG))
    lb = jnp.exp(jnp.where(tri, cb - rb, NEG))
    y = jnp.where(first, mm_nn(cbm * la, x), mm_nn(cbm * lb, x))
    y = y + jnp.where(first, jnp.exp(ca), jnp.exp(cb)) * mm_nn(cm, h)
    last = _iota((1, L), 1) == L - 1
    ala = jnp.sum(jnp.where(last, ra, 0.0), axis=1, keepdims=True)
    alb = jnp.sum(jnp.where(last, rb, 0.0), axis=1, keepdims=True)
    dec = jnp.where(first, jnp.exp(ala - ca), jnp.exp(alb - cb))
    hn = jnp.where(first, jnp.exp(ala), jnp.exp(alb)) * h + mm_tn(bm, x * dec)
    return y, hn


def _ssd_specs(nc):
    L = CHUNK
    hpg = SSM_HEADS // 2 // SSM_GROUPS

    def mk(rev):
        cidx = (lambda c: nc - 1 - c) if rev else (lambda c: c)
        return dict(
            xs=pl.BlockSpec((L, LANES), lambda c, hp: (cidx(c), hp)),
            col=pl.BlockSpec((L, 16), lambda c, hp: (cidx(c), 0)),
            row=pl.BlockSpec((16, L), lambda c, hp: (0, cidx(c))),
            b=pl.BlockSpec((L, SSM_N), lambda c, hp: (cidx(c), hp // hpg)),
            c=pl.BlockSpec((L, SSM_N), lambda c, hp: (cidx(c), SSM_GROUPS + hp // hpg)),
            grp=pl.BlockSpec((L, SSM_N), lambda c, hp: (cidx(c), hp // hpg)),
            st=pl.BlockSpec((1, 1, SSM_N, LANES), lambda c, hp: (cidx(c), hp, 0, 0)),
        )
    return mk


def _ssd_fwd(xs, dt_col, acs_col, acs_row, bc):
    S = xs.shape[0]
    nc, nhp = S // CHUNK, SSM_HEADS // 2
    sp = _ssd_specs(nc)(False)

    def body(xs_ref, dt_ref, ac_ref, ar_ref, b_ref, c_ref, y_ref, hs_ref, h_scr):
        c, hp = pl.program_id(0), pl.program_id(1)

        @pl.when(c == 0)
        def _():
            h_scr[hp] = jnp.zeros((SSM_N, LANES), F32)

        h = h_scr[hp]
        hs_ref[0, 0] = h
        y, hn = _ssd_chunk(xs_ref[...], dt_ref[...], ac_ref[...], ar_ref[...], b_ref[...], c_ref[...], h, hp=hp)
        y_ref[...] = y
        h_scr[hp] = hn

    return pl.pallas_call(
        body, name="ssd_fwd", grid=(nc, nhp),
        in_specs=[sp["xs"], sp["col"], sp["col"], sp["row"], sp["b"], sp["c"]],
        out_specs=[sp["xs"], sp["st"]],
        out_shape=[SDS((S, SSM_HEADS * SSM_P), F32), SDS((nc, nhp, SSM_N, LANES), F32)],
        scratch_shapes=[pltpu.VMEM((nhp, SSM_N, LANES), F32)],
        compiler_params=_cparams(dimension_semantics=("arbitrary", "arbitrary")),
    )(xs, dt_col, acs_col, acs_row, bc, bc)


def _ssd_bwd(xs, dt_col, acs_col, acs_row, bc, hs, dy):
    S = xs.shape[0]
    nc, nhp = S // CHUNK, SSM_HEADS // 2
    hpg = nhp // SSM_GROUPS
    sp = _ssd_specs(nc)(True)

    def body(xs_ref, dt_ref, ac_ref, ar_ref, b_ref, c_ref, hs_ref, dy_ref,
             dxs_ref, ddt_ref, dac_ref, dar_ref, db_ref, dc_ref, dh_scr):
        c, hp = pl.program_id(0), pl.program_id(1)

        @pl.when(c == 0)
        def _():
            dh_scr[hp] = jnp.zeros((SSM_N, LANES), F32)

        _, vjp = jax.vjp(functools.partial(_ssd_chunk, hp=hp), xs_ref[...], dt_ref[...], ac_ref[...], ar_ref[...],
                         b_ref[...], c_ref[...], hs_ref[0, 0])
        dxs, ddt, dac, dar, db, dc, dh = vjp((dy_ref[...], dh_scr[hp]))
        dxs_ref[...] = dxs
        dh_scr[hp] = dh

        @pl.when(hp == 0)
        def _():
            ddt_ref[...] = jnp.zeros_like(ddt_ref)
            dac_ref[...] = jnp.zeros_like(dac_ref)
            dar_ref[...] = jnp.zeros_like(dar_ref)

        ddt_ref[...] += ddt
        dac_ref[...] += dac
        dar_ref[...] += dar

        @pl.when(hp % hpg == 0)
        def _():
            db_ref[...] = jnp.zeros_like(db_ref)
            dc_ref[...] = jnp.zeros_like(dc_ref)

        db_ref[...] += db
        dc_ref[...] += dc

    return pl.pallas_call(
        body, name="ssd_bwd", grid=(nc, nhp),
        in_specs=[sp["xs"], sp["col"], sp["col"], sp["row"], sp["b"], sp["c"], sp["st"], sp["xs"]],
        out_specs=[sp["xs"], sp["col"], sp["col"], sp["row"], sp["grp"], sp["grp"]],
        out_shape=[SDS((S, SSM_HEADS * SSM_P), F32), SDS((S, 16), F32), SDS((S, 16), F32), SDS((16, S), F32),
                   SDS((S, SSM_GROUPS * SSM_N), F32), SDS((S, SSM_GROUPS * SSM_N), F32)],
        scratch_shapes=[pltpu.VMEM((nhp, SSM_N, LANES), F32)],
        compiler_params=_cparams(dimension_semantics=("arbitrary", "arbitrary")),
    )(xs, dt_col, acs_col, acs_row, bc, bc, hs, dy)


ATT_T = 128


def _pick_col(blk, h):
    return jnp.sum(jnp.where(_iota(blk.shape, 1) == h, blk, 0.0), axis=1, keepdims=True)


def _pick_row(blk, h):
    return jnp.sum(jnp.where(_iota(blk.shape, 0) == h, blk, 0.0), axis=0, keepdims=True)


def _fox_fwd(qn, kn, vsrc, v_c0, cum_col, cum_row3):
    S = qn.shape[0]
    T = ATT_T
    nq, nhp = S // T, ATT_HEADS // 2
    vb0 = v_c0 // LANES
    scale = ATT_D ** -0.5

    def body(q_ref, k_ref, v_ref, cc_ref, cr_ref, o_ref, l_ref):
        hp, i = pl.program_id(0), pl.program_id(1)
        first = _iota((1, LANES), 1) < ATT_D
        q = q_ref[...]
        zero = jnp.zeros_like(q)
        qs = (jnp.where(first, q, zero), jnp.where(first, zero, q))
        cc = cc_ref[...]
        cq = (_pick_col(cc, 2 * hp), _pick_col(cc, 2 * hp + 1))
        row_id = i * T + _iota((T, T), 0)

        def step(j, carry):
            off = pl.multiple_of(j * T, T)
            k = k_ref[pl.ds(off, T), :]
            v = v_ref[pl.ds(off, T), :].astype(BF16)
            cr = cr_ref[j]
            mask = row_id >= j * T + _iota((T, T), 1)
            out = []
            for hh in range(2):
                m, l, acc = carry[3 * hh:3 * hh + 3]
                s = _bdot(qs[hh], k, _NT) * scale + cq[hh] - _pick_row(cr, 2 * hp + hh)
                s = jnp.where(mask, s, NEG)
                m_new = jnp.maximum(m, jnp.max(s, axis=1, keepdims=True))
                alpha = jnp.exp(m - m_new)
                p = jnp.exp(s - m_new)
                out += [m_new, alpha * l + jnp.sum(p, axis=1, keepdims=True), alpha * acc + _bdot(p, v, _NN)]
            return tuple(out)

        init = (jnp.full((T, 1), NEG, F32), jnp.zeros((T, 1), F32), jnp.zeros((T, LANES), F32)) * 2
        ma, la, acca, mb, lb, accb = lax.fori_loop(0, i + 1, step, init)
        o_ref[...] = jnp.where(first, acca / la, accb / lb)
        l_ref[...] = jnp.where(first, ma + jnp.log(la), mb + jnp.log(lb))

    return pl.pallas_call(
        body, name="fox_fwd", grid=(nhp, nq),
        in_specs=[pl.BlockSpec((T, LANES), lambda hp, i: (i, hp)), pl.BlockSpec((S, LANES), lambda hp, i: (0, hp)),
                  pl.BlockSpec((S, LANES), lambda hp, i: (0, vb0 + hp)), pl.BlockSpec((T, 16), lambda hp, i: (i, 0)),
                  pl.BlockSpec((nq, 16, T), lambda hp, i: (0, 0, 0))],
        out_specs=[pl.BlockSpec((T, LANES), lambda hp, i: (i, hp))] * 2,
        out_shape=[SDS((S, ATT_HEADS * ATT_D), F32)] * 2,
        compiler_params=_cparams(dimension_semantics=("parallel", "arbitrary")),
    )(qn, kn, vsrc, cum_col, cum_row3)


def _fox_bwd(qn, kn, vsrc, v_c0, cum_col, cum_row3, o, lse, dsrc, d_c0):
    S = qn.shape[0]
    T = ATT_T
    nq, nhp = S // T, ATT_HEADS // 2
    vb0, db0 = v_c0 // LANES, d_c0 // LANES
    scale = ATT_D ** -0.5

    def body(q_ref, k_ref, v_ref, cc_ref, cr_ref, o_ref, l_ref, do_ref, dq_ref, dk_ref, dv_ref, dc_ref):
        hp, j = pl.program_id(0), pl.program_id(1)
        first = _iota((1, LANES), 1) < ATT_D
        k = k_ref[...]
        zk = jnp.zeros_like(k)
        ks = (jnp.where(first, k, zk), jnp.where(first, zk, k))
        v = v_ref[...].astype(BF16)
        cr = cr_ref[0]
        ck = (_pick_row(cr, 2 * hp), _pick_row(cr, 2 * hp + 1))
        col_id = j * T + _iota((T, T), 1)

        @pl.when(j == 0)
        def _():
            dq_ref[...] = jnp.zeros_like(dq_ref)

        def step(i, carry):
            dk, dv, dca, dcb = carry
            dcs = [dca, dcb]
            off = pl.multiple_of(i * T, T)
            q = q_ref[pl.ds(off, T), :]
            zq = jnp.zeros_like(q)
            qs = (jnp.where(first, q, zq), jnp.where(first, zq, q))
            do = do_ref[pl.ds(off, T), :]
            dd = do * o_ref[pl.ds(off, T), :]
            lse_blk = l_ref[pl.ds(off, T), :]
            cc = cc_ref[pl.ds(off, T), :]
            dob = do.astype(BF16)
            zd = jnp.zeros_like(dob)
            dos = (jnp.where(first, dob, zd), jnp.where(first, zd, dob))
            mask = off + _iota((T, T), 0) >= col_id
            dq = jnp.zeros((T, LANES), F32)
            for hh in range(2):
                sel = first if hh == 0 else jnp.logical_not(first)
                delta = jnp.sum(jnp.where(sel, dd, 0.0), axis=1, keepdims=True)
                lse_h = jnp.max(jnp.where(sel, lse_blk, NEG), axis=1, keepdims=True)
                s = _bdot(qs[hh], k, _NT) * scale + _pick_col(cc, 2 * hp + hh) - ck[hh]
                p = jnp.where(mask, jnp.exp(s - lse_h), 0.0)
                dp = _bdot(dos[hh], v, _NT)
                ds = p * (dp - delta)
                dv = dv + _bdot(p, dos[hh], _TN)
                dk = dk + _bdot(ds, qs[hh], _TN) * scale
                dq = dq + _bdot(ds, ks[hh], _NN) * scale
                dcs[hh] = dcs[hh] - jnp.sum(ds, axis=0, keepdims=True)
            dq_ref[pl.ds(off, T), :] += dq
            return dk, dv, dcs[0], dcs[1]

        z2 = jnp.zeros((T, LANES), F32)
        z1 = jnp.zeros((1, T), F32)
        dk, dv, dca, dcb = lax.fori_loop(j, nq, step, (z2, z2, z1, z1))
        dk_ref[...] = dk
        dv_ref[...] = dv.astype(dv_ref.dtype)
        dc_ref[0, 0] = jnp.zeros((8, T), F32)
        dc_ref[0, 0, 0:1, :] = dca
        dc_ref[0, 0, 1:2, :] = dcb

    return pl.pallas_call(
        body, name="fox_bwd", grid=(nhp, nq),
        in_specs=[pl.BlockSpec((S, LANES), lambda hp, j: (0, hp)), pl.BlockSpec((T, LANES), lambda hp, j: (j, hp)),
                  pl.BlockSpec((T, LANES), lambda hp, j: (j, vb0 + hp)), pl.BlockSpec((S, 16), lambda hp, j: (0, 0)),
                  pl.BlockSpec((1, 16, T), lambda hp, j: (j, 0, 0)), pl.BlockSpec((S, LANES), lambda hp, j: (0, hp)),
                  pl.BlockSpec((S, LANES), lambda hp, j: (0, hp)), pl.BlockSpec((S, LANES), lambda hp, j: (0, db0 + hp))],
        out_specs=[pl.BlockSpec((S, LANES), lambda hp, j: (0, hp)), pl.BlockSpec((T, LANES), lambda hp, j: (j, hp)),
                   pl.BlockSpec((T, LANES), lambda hp, j: (j, hp)), pl.BlockSpec((1, 1, 8, T), lambda hp, j: (hp, j, 0, 0))],
        out_shape=[SDS((S, ATT_HEADS * ATT_D), F32), SDS((S, ATT_HEADS * ATT_D), F32), SDS((S, ATT_HEADS * ATT_D), BF16),
                   SDS((nhp, nq, 8, T), F32)],
        compiler_params=_cparams(dimension_semantics=("parallel", "arbitrary")),
    )(qn, kn, vsrc, cum_col, cum_row3, o, lse, dsrc)


def _loss_head(y, target, *, tm):
    M, W = y.shape

    def body(y_ref, t_ref, dy_ref, loss_ref):
        @pl.when(pl.program_id(0) == 0)
        def _():
            loss_ref[...] = jnp.zeros_like(loss_ref)

        e = y_ref[...] - t_ref[...]
        dy_ref[...] = e * (1.0 / W)
        loss_ref[...] += jnp.sum(jnp.sum(e * e, axis=1, keepdims=True), axis=0, keepdims=True) * (0.5 / W)

    return pl.pallas_call(
        body, name="loss_head", grid=(M // tm,),
        in_specs=[pl.BlockSpec((tm, W), lambda i: (i, 0))] * 2,
        out_specs=[pl.BlockSpec((tm, W), lambda i: (i, 0)), pl.BlockSpec((1, 1), lambda i: (0, 0))],
        out_shape=[SDS((M, W), F32), SDS((1, 1), F32)],
        compiler_params=_cparams(dimension_semantics=("arbitrary",)),
    )(y, target)


def _adamw_math(w, g, m, v):
    m = ADAM_B1 * m + (1.0 - ADAM_B1) * g
    v = ADAM_B2 * v + (1.0 - ADAM_B2) * jnp.square(g)
    m_hat = m / (1.0 - ADAM_B1 ** ADAM_STEP)
    v_hat = v / (1.0 - ADAM_B2 ** ADAM_STEP)
    delta = -ADAM_LR * (m_hat / (jnp.sqrt(v_hat) + ADAM_EPS) + ADAM_WD * w)
    return delta, m, v


def _reduce_adamw(parts, w, m, v, *, tr, name):
    R, C = w.shape
    tr = min(tr, R)

    def body(p_ref, w_ref, m_ref, v_ref, g_ref, d_ref, nm_ref, nv_ref):
        g = p_ref[0]
        for s in range(1, N_DEV):
            g = g + p_ref[s]
        g_ref[...] = g
        d_ref[...], nm_ref[...], nv_ref[...] = _adamw_math(w_ref[...], g, m_ref[...], v_ref[...])

    blk = pl.BlockSpec((tr, C), lambda i: (i, 0))
    return pl.pallas_call(
        body, name=name, grid=(R // tr,),
        in_specs=[pl.BlockSpec((N_DEV, tr, C), lambda i: (0, i, 0)), blk, blk, blk], out_specs=[blk] * 4,
        out_shape=[SDS((R, C), F32)] * 4, compiler_params=_cparams(dimension_semantics=("parallel",)),
    )(parts, w, m, v)


def _adamw(w, g, m, v, *, name):
    def body(w_ref, g_ref, m_ref, v_ref, d_ref, nm_ref, nv_ref):
        d_ref[...], nm_ref[...], nv_ref[...] = _adamw_math(w_ref[...], g_ref[...], m_ref[...], v_ref[...])

    return pl.pallas_call(body, name=name, out_shape=[SDS(w.shape, F32)] * 3, compiler_params=_cparams())(w, g, m, v)


def _exchange(arrays, *, scatter, name):
    n = len(arrays)
    npeer = N_DEV - 1

    def body(*refs):
        ins, outs = refs[:n], refs[n:2 * n]
        send_sems, recv_sems, local_sems = refs[2 * n:]
        x, y, c = lax.axis_index("x"), lax.axis_index("y"), lax.axis_index("c")
        me = 4 * x + 2 * y + c
        peers = []
        for k in range(1, N_DEV):
            px, py, pc = x ^ ((k >> 2) & 1), y ^ ((k >> 1) & 1), c ^ (k & 1)
            peers.append(((px, py, pc), 4 * px + 2 * py + pc))
        started = []
        for a in range(n):
            src_me = ins[a].at[me] if scatter else ins[a]
            local = pltpu.make_async_copy(src_me, outs[a].at[me], local_sems.at[a])
            local.start()
            started.append(local)
        remote = []
        for a in range(n):
            for k, (dev, idx) in enumerate(peers):
                cp = pltpu.make_async_remote_copy(
                    src_ref=ins[a].at[idx] if scatter else ins[a], dst_ref=outs[a].at[me],
                    send_sem=send_sems.at[a * npeer + k], recv_sem=recv_sems.at[a * npeer + k],
                    device_id=dev, device_id_type=MESH)
                cp.start()
                remote.append((cp, a, k, idx))
        for cp, a, k, idx in remote:
            cp.wait_send()
        for cp, a, k, idx in remote:
            pltpu.make_async_remote_copy(
                src_ref=ins[a].at[idx] if scatter else ins[a], dst_ref=outs[a].at[idx],
                send_sem=send_sems.at[a * npeer + k], recv_sem=recv_sems.at[a * npeer + k],
                device_id=peers[k][0], device_id_type=MESH).wait_recv()
        for local in started:
            local.wait()

    def out_sds(arr):
        return SDS(arr.shape if scatter else (N_DEV,) + arr.shape, arr.dtype)

    any_spec = pl.BlockSpec(memory_space=pl.ANY)
    return pl.pallas_call(
        body, name=name, in_specs=[any_spec] * n, out_specs=[any_spec] * n, out_shape=[out_sds(a) for a in arrays],
        scratch_shapes=[pltpu.SemaphoreType.DMA((n * npeer,)), pltpu.SemaphoreType.DMA((n * npeer,)),
                        pltpu.SemaphoreType.DMA((n,))],
        compiler_params=pltpu.CompilerParams(has_side_effects=True),
    )(*arrays)


SMALL = (("g_mix", 1024), ("conv_w", 6144), ("conv_b", 1536), ("dt_bias", 16), ("a_log", 16), ("d_skip", 16),
         ("ssm_norm_w", 1024), ("g_q", 64), ("g_k", 64), ("f_bias", 16), ("g_xattn", 1024), ("g_mem", 1024),
         ("xg_q", 256), ("xg_k", 256), ("g_mlp", 1024))
SLAB_ROWS = 112
BIG = ("w_in", "w_out", "xq_w", "xkv_w", "xo_w", "w_up", "w_down")
WEIGHTS = ("g_mix", "w_in", "conv_w", "conv_b", "dt_bias", "a_log", "d_skip", "ssm_norm_w", "g_q", "g_k", "f_bias", "w_out",
           "g_xattn", "g_mem", "xq_w", "xkv_w", "xg_q", "xg_k", "xo_w", "g_mlp", "w_up", "w_down")
O_Z, O_XS, O_B, O_C, O_DT, O_Q, O_K, O_V, O_F, O_END = 0, 1024, 2048, 2304, 2560, 2576, 3600, 4624, 5648, 5664


def _pack_small(vals):
    rows = []
    for name, size in SMALL:
        flat = vals[name].reshape(-1).astype(F32)
        pad = -size % LANES
        rows.append(jnp.pad(flat, (0, pad)).reshape(-1, LANES))
    slab = jnp.concatenate(rows, axis=0)
    return jnp.pad(slab, ((0, SLAB_ROWS - slab.shape[0]), (0, 0)))


def _unpack_small(slab):
    out, r = {}, 0
    for name, size in SMALL:
        nr = -(-size // LANES)
        out[name] = slab[r:r + nr].reshape(-1)[:size]
        r += nr
    return out


def _cols(a, lo, hi):
    return a[:, lo:hi]


def _step(p, m, v, x, mem, target):
    S = x.shape[0]
    TM = 256
    me = 4 * lax.axis_index("x") + 2 * lax.axis_index("y") + lax.axis_index("c")

    gathered = _exchange([p[n].astype(BF16) for n in BIG] + [p["conv_w"]], scatter=False, name="allgather_weights")
    win_g, wout_g, xq_g, xkv_g, xo_g, wup_g, wdown_g, convw_g = gathered
    w_in_o = win_g.transpose(1, 0, 2).reshape(D_MODEL, O_END)
    w_in = jnp.concatenate(
        [_cols(w_in_o, O_Z, O_XS), _cols(w_in_o, O_XS, O_B), _cols(w_in_o, O_Q, O_K), _cols(w_in_o, O_K, O_V),
         _cols(w_in_o, O_V, O_F), _cols(w_in_o, O_B, O_C), _cols(w_in_o, O_C, O_DT), _cols(w_in_o, O_DT, O_Q),
         _cols(w_in_o, O_F, O_END), jnp.zeros((D_MODEL, P_COLS - C_DTF - 32), BF16)], axis=1)
    w_out = wout_g.reshape(2 * D_MODEL, D_MODEL)
    xq_w = xq_g.reshape(D_MODEL, D_MODEL)
    xkv_w = xkv_g.transpose(1, 0, 2).reshape(D_MODEL, 2 * D_MODEL)
    xo_w = xo_g.reshape(D_MODEL, D_MODEL)
    w_up = wup_g.transpose(1, 0, 2).reshape(D_MODEL, 4 * D_MODEL)
    w_down = wdown_g.reshape(4 * D_MODEL, D_MODEL)
    conv_w = convw_g.transpose(1, 0, 2).reshape(4, 1536)
    cw_xs, cw_bc = conv_w[:, :1024], conv_w[:, 1024:]
    cb_xs, cb_bc = p["conv_b"][:, :1024], p["conv_b"][:, 1024:]
    dt_bias, a_log, f_bias = p["dt_bias"].reshape(16, 1), p["a_log"].reshape(16, 1), p["f_bias"].reshape(16, 1)

    def rms(u, g, name):
        return _rw_fwd(_rms_fn, [_whole(u)], [_whole(g)], [(D_MODEL, BF16)], tm=TM, name=name)[0]

    h1 = rms(x, p["g_mix"], "rms_mix")
    proj = _matmul(h1, w_in, mode="nn", tm=1024, tn=640, tk=1024, name="mm_in")
    xs_c = _conv_fwd(proj, C_XS, 1024, cw_xs, cb_xs, name="conv_xs")
    bc_c = _conv_fwd(proj, C_B, 512, cw_bc, cb_bc, name="conv_bc")
    dtf_t = proj[:, C_DTF:C_DTF + 32].T
    dt_t, acs_t, cum_t = _dtf_fwd(dtf_t, dt_bias, a_log, f_bias)
    dt_col, acs_col, cum_col = dt_t.T, acs_t.T, cum_t.T
    cum_row3 = cum_t.reshape(16, S // ATT_T, ATT_T).transpose(1, 0, 2)
    y_ssd, hs = _ssd_fwd(xs_c, dt_col, acs_col, acs_t, bc_c)
    gate_rows = [_whole(y_ssd), _whole(xs_c), (proj, C_Z, 1024)]
    gate_pars = [_whole(p["d_skip"]), _whole(p["ssm_norm_w"])]
    y_ssm = _rw_fwd(_gate_fn, gate_rows, gate_pars, [(1024, BF16)], tm=TM, name="gate")[0]
    qn = _rw_fwd(_headnorm_fn, [(proj, C_Q, 1024)], [_whole(p["g_q"])], [(1024, BF16)], tm=TM, name="qnorm")[0]
    kn = _rw_fwd(_headnorm_fn, [(proj, C_K, 1024)], [_whole(p["g_k"])], [(1024, BF16)], tm=TM, name="knorm")[0]
    o, lse = _fox_fwd(qn, kn, proj, C_V, cum_col, cum_row3)
    mixed = jnp.concatenate([y_ssm, o.astype(BF16)], axis=1)
    x1 = _matmul(mixed, w_out, mode="nn", tm=1024, tn=512, tk=2048, add=x, name="mm_out")

    h2 = rms(x1, p["g_xattn"], "rms_xattn")
    mem_n = rms(mem, p["g_mem"], "rms_mem")
    q2 = _matmul(h2, xq_w, mode="nn", tm=1024, tn=512, tk=1024, name="mm_xq")
    kv = _matmul(mem_n, xkv_w, mode="nn", tm=256, tn=1024, tk=1024, name="mm_xkv")
    xa_rows = [(q2, X_D * h, X_D) for h in range(X_HEADS)]
    xa_pars = ([(kv, X_D * h, X_D) for h in range(X_HEADS)] + [(kv, D_MODEL + X_D * h, X_D) for h in range(X_HEADS)]
               + [_whole(p["xg_q"]), _whole(p["xg_k"])])
    o2 = _rw_fwd(_xattn_fn, xa_rows, xa_pars, [(D_MODEL, BF16)], tm=TM, name="xattn")[0]
    x2 = _matmul(o2, xo_w, mode="nn", tm=1024, tn=512, tk=1024, add=x1, name="mm_xo")

    h3 = rms(x2, p["g_mlp"], "rms_mlp")
    a = _matmul(h3, w_up, mode="nn", tm=1024, tn=1024, tk=1024, name="mm_up")
    usq = _rw_fwd(_relu2_fn, [_whole(a)], [], [(4 * D_MODEL, BF16)], tm=TM, name="relu2")[0]
    x3 = _matmul(usq, w_down, mode="nn", tm=1024, tn=512, tk=2048, add=x2, name="mm_down")
    dy, loss_part = _loss_head(x3, target, tm=TM)
    loss = lax.psum(loss_part[0, 0], ("x", "y", "c"))

    g = {}
    g["w_down"] = _matmul(usq, dy, mode="tn", tm=1024, tn=1024, tk=1024, name="mm_d_wdown")
    dusq = _matmul(dy, w_down, mode="nt", tm=1024, tn=1024, tk=1024, name="mm_d_usq")
    da = _rw_bwd(_relu2_fn, [_whole(a)], [], [_whole(dusq)], tm=TM, name="relu2_bwd", row_grads=[BF16])[0]
    g["w_up"] = _matmul(h3, da, mode="tn", tm=1024, tn=1024, tk=1024, name="mm_d_wup")
    dh3 = _matmul(da, w_up, mode="nt", tm=1024, tn=512, tk=2048, name="mm_d_h3")
    dx2, g["g_mlp"] = _rw_bwd(_rms_fn, [_whole(x2)], [_whole(p["g_mlp"])], [_whole(dh3)], tm=TM, name="rms_mlp_bwd",
                              row_grads=[F32], adds={0: _whole(dy)})

    g["xo_w"] = _matmul(o2, dx2, mode="tn", tm=1024, tn=1024, tk=1024, name="mm_d_wxo")
    do2 = _matmul(dx2, xo_w, mode="nt", tm=1024, tn=512, tk=1024, name="mm_d_o2")
    xa = _rw_bwd(_xattn_fn, xa_rows, xa_pars, [_whole(do2)], tm=TM, name="xattn_bwd", row_grads=[BF16] * X_HEADS)
    dq2 = jnp.concatenate(xa[:X_HEADS], axis=1)
    dkv = jnp.concatenate(xa[X_HEADS:3 * X_HEADS], axis=1)
    g["xg_q"], g["xg_k"] = xa[3 * X_HEADS], xa[3 * X_HEADS + 1]
    g["xq_w"] = _matmul(h2, dq2, mode="tn", tm=1024, tn=1024, tk=1024, name="mm_d_wxq")
    dh2 = _matmul(dq2, xq_w, mode="nt", tm=1024, tn=512, tk=1024, name="mm_d_h2")
    g["xkv_w"] = _matmul(mem_n, dkv, mode="tn", tm=1024, tn=1024, tk=256, name="mm_d_wxkv")
    dmem_n = _matmul(dkv, xkv_w, mode="nt", tm=256, tn=1024, tk=2048, name="mm_d_memn")
    g["g_mem"] = _rw_bwd(_rms_fn, [_whole(mem)], [_whole(p["g_mem"])], [_whole(dmem_n)], tm=TM, name="rms_mem_bwd",
                         row_grads=[None])[0]
    dx1, g["g_xattn"] = _rw_bwd(_rms_fn, [_whole(x1)], [_whole(p["g_xattn"])], [_whole(dh2)], tm=TM, name="rms_xattn_bwd",
                                row_grads=[F32], adds={0: _whole(dx2)})

    g["w_out"] = _matmul(mixed, dx1, mode="tn", tm=1024, tn=1024, tk=1024, name="mm_d_wout")
    dmixed = _matmul(dx1, w_out, mode="nt", tm=1024, tn=1024, tk=1024, name="mm_d_mixed")
    dqn, dkn, dv, dcum4 = _fox_bwd(qn, kn, proj, C_V, cum_col, cum_row3, o, lse, dmixed, 1024)
    dq, g["g_q"] = _rw_bwd(_headnorm_fn, [(proj, C_Q, 1024)], [_whole(p["g_q"])], [_whole(dqn)], tm=TM, name="qnorm_bwd",
                           row_grads=[BF16])
    dk, g["g_k"] = _rw_bwd(_headnorm_fn, [(proj, C_K, 1024)], [_whole(p["g_k"])], [_whole(dkn)], tm=TM, name="knorm_bwd",
                           row_grads=[BF16])
    dy_ssd, dxs_g, dz, g["d_skip"], g["ssm_norm_w"] = _rw_bwd(
        _gate_fn, gate_rows, gate_pars, [(dmixed, 0, 1024)], tm=TM, name="gate_bwd", row_grads=[F32, F32, BF16])
    dxs_s, ddt_col, dacs_col, dacs_row, d_b, d_c = _ssd_bwd(xs_c, dt_col, acs_col, acs_t, bc_c, hs, dy_ssd)
    dcum_t = dcum4[:, :, 0:2, :].transpose(0, 2, 1, 3).reshape(16, S)
    ddtf_t, ddtb, dalog, dfb = _dtf_bwd(dtf_t, dt_bias, a_log, f_bias, ddt_col.T, dacs_col.T, dacs_row, dcum_t)
    g["dt_bias"], g["a_log"], g["f_bias"] = ddtb, dalog, dfb
    dxs_raw, dcw_xs, dcb_xs = _conv_bwd(proj, C_XS, 1024, cw_xs, cb_xs, [dxs_s, dxs_g], name="conv_xs_bwd")
    dbc_raw, dcw_bc, dcb_bc = _conv_bwd(proj, C_B, 512, cw_bc, cb_bc, [jnp.concatenate([d_b, d_c], axis=1)],
                                        name="conv_bc_bwd")
    g["conv_w"] = jnp.concatenate([dcw_xs, dcw_bc], axis=1)
    g["conv_b"] = jnp.concatenate([dcb_xs, dcb_bc], axis=1)
    ddtf = jnp.pad(ddtf_t.T.astype(BF16), ((0, 0), (0, P_COLS - C_DTF - 32)))
    dproj = jnp.concatenate([dz, dxs_raw, dq, dk, dv, dbc_raw, ddtf], axis=1)
    dw_in_p = _matmul(h1, dproj, mode="tn", tm=1024, tn=640, tk=1024, name="mm_d_win")
    dh1 = _matmul(dproj, w_in, mode="nt", tm=1024, tn=512, tk=1920, name="mm_d_h1")
    grad_x, g["g_mix"] = _rw_bwd(_rms_fn, [_whole(x)], [_whole(p["g_mix"])], [_whole(dh1)], tm=TM, name="rms_mix_bwd",
                                 row_grads=[F32], adds={0: _whole(dx1)})
    g["w_in"] = jnp.concatenate(
        [_cols(dw_in_p, C_Z, C_Q), _cols(dw_in_p, C_B, C_DTF + 16), _cols(dw_in_p, C_Q, C_B),
         _cols(dw_in_p, C_DTF + 16, C_DTF + 32)], axis=1)

    def col_shards(a):
        r, c = a.shape
        return a.reshape(r, N_DEV, c // N_DEV).transpose(1, 0, 2)

    def row_shards(a):
        r, c = a.shape
        return a.reshape(N_DEV, r // N_DEV, c)

    send = [col_shards(g["w_in"]), row_shards(g["w_out"]), row_shards(g["xq_w"]), col_shards(g["xkv_w"]),
            row_shards(g["xo_w"]), col_shards(g["w_up"]), row_shards(g["w_down"])]
    parts = _exchange(send, scatter=True, name="exchange_grads")
    small_parts = _exchange([_pack_small(g)], scatter=False, name="gather_small_grads")[0]

    grads, delta, new_m, new_v = {}, {}, {}, {}
    for name, part in zip(BIG, parts, strict=True):
        grads[name], delta[name], new_m[name], new_v[name] = _reduce_adamw(part, p[name], m[name], v[name], tr=128,
                                                                            name="adamw_" + name)
    zeros_cw = jnp.zeros((4, 1536), F32)
    slabs = [_pack_small({**d, "conv_w": zeros_cw}) for d in (p, m, v)]
    sg, sd, sm, sv = _reduce_adamw(small_parts, *slabs, tr=SLAB_ROWS, name="adamw_small")
    for dst, slab in ((grads, sg), (delta, sd), (new_m, sm), (new_v, sv)):
        for name, flat in _unpack_small(slab).items():
            if name != "conv_w":
                dst[name] = flat.reshape(p[name].shape)
    cw_shard = p["conv_w"].shape[1]
    grads["conv_w"] = lax.dynamic_slice(_unpack_small(sg)["conv_w"].reshape(4, 1536), (0, me * cw_shard), (4, cw_shard))
    delta["conv_w"], new_m["conv_w"], new_v["conv_w"] = _adamw(p["conv_w"], grads["conv_w"], m["conv_w"], v["conv_w"],
                                                               name="adamw_conv_w")
    return loss, grad_x, grads, delta, new_m, new_v


def kernel(x, mem, g_mix, w_in, conv_w, conv_b, dt_bias, a_log, d_skip, ssm_norm_w, g_q, g_k, f_bias, w_out, g_xattn, g_mem, xq_w, xkv_w, xg_q, xg_k, xo_w, g_mlp, w_up, w_down, loss_target, m_g_mix, m_w_in, m_conv_w, m_conv_b, m_dt_bias, m_a_log, m_d_skip, m_ssm_norm_w, m_g_q, m_g_k, m_f_bias, m_w_out, m_g_xattn, m_g_mem, m_xq_w, m_xkv_w, m_xg_q, m_xg_k, m_xo_w, m_g_mlp, m_w_up, m_w_down, v_g_mix, v_w_in, v_conv_w, v_conv_b, v_dt_bias, v_a_log, v_d_skip, v_ssm_norm_w, v_g_q, v_g_k, v_f_bias, v_w_out, v_g_xattn, v_g_mem, v_xq_w, v_xkv_w, v_xg_q, v_xg_k, v_xo_w, v_g_mlp, v_w_up, v_w_down):
    args = locals()
    drop = lambda t: t[0] if t.ndim == 3 else t
    p = {n: drop(args[n]) for n in WEIGHTS}
    m = {n: drop(args["m_" + n]) for n in WEIGHTS}
    v = {n: drop(args["v_" + n]) for n in WEIGHTS}
    loss, grad_x, grads, delta, new_m, new_v = _step(p, m, v, x[0], mem[0], loss_target[0])
    outs = [loss, grad_x[None]]
    for d in (grads, delta, new_m, new_v):
        outs += [d[n].reshape(args[n].shape) for n in WEIGHTS]
    return tuple(outs)
```

```python
import functools
import math

import jax
import jax.numpy as jnp
from jax import lax
from jax.experimental import pallas as pl
from jax.experimental.pallas import tpu as pltpu

F32, BF16 = jnp.float32, jnp.bfloat16
SDS = jax.ShapeDtypeStruct
HI = lax.Precision.HIGHEST
MESH = pl.DeviceIdType.MESH

N_DEV = 8
EPS = 1e-5
D_MODEL = 1024
SSM_HEADS, SSM_P, SSM_N, SSM_GROUPS, CHUNK = 16, 64, 128, 2, 128
ATT_HEADS, ATT_D = 16, 64
X_HEADS, X_D = 4, 256
LANES = 128
VMEM_LIMIT = 48 * 1024 * 1024
NEG = -1e30

GRAD_WIRE = BF16
ADAM_LR, ADAM_B1, ADAM_B2, ADAM_EPS, ADAM_WD, ADAM_STEP = 0.001, 0.9, 0.999, 1e-08, 0.01, 10

C_Z, C_XS, C_Q, C_K, C_V, C_B, C_C, C_DTF, P_COLS = 0, 1024, 2048, 3072, 4096, 5120, 5376, 5632, 5760

_NN = (((1,), (0,)), ((), ()))
_NT = (((1,), (1,)), ((), ()))
_TN = (((0,), (0,)), ((), ()))


def _cparams(**kw):
    return pltpu.CompilerParams(vmem_limit_bytes=VMEM_LIMIT, **kw)


def _bdot(a, b, dn):
    return lax.dot_general(a.astype(BF16), b.astype(BF16), dn, preferred_element_type=F32)


@jax.custom_vjp
def mm_nn(a, b):
    return _bdot(a, b, _NN)


mm_nn.defvjp(lambda a, b: (mm_nn(a, b), (a, b)), lambda r, g: (_bdot(g, r[1], _NT), _bdot(r[0], g, _TN)))


@jax.custom_vjp
def mm_nt(a, b):
    return _bdot(a, b, _NT)


mm_nt.defvjp(lambda a, b: (mm_nt(a, b), (a, b)), lambda r, g: (_bdot(g, r[1], _NN), _bdot(g, r[0], _TN)))


@jax.custom_vjp
def mm_tn(a, b):
    return _bdot(a, b, _TN)


mm_tn.defvjp(lambda a, b: (mm_tn(a, b), (a, b)), lambda r, g: (_bdot(r[1], g, _NT), _bdot(r[0], g, _NN)))


def _cdot(x, c):
    return jnp.dot(x, c, precision=HI, preferred_element_type=F32)


def _iota(shape, dim):
    return lax.broadcasted_iota(jnp.int32, shape, dim)


def _matmul(a, b, *, mode, tm, tn, tk, name, out_dtype=F32, add=None):
    if mode == "tn":
        K, M = a.shape
    else:
        M, K = a.shape
    N = b.shape[0] if mode == "nt" else b.shape[1]
    tm, tn, tk = min(tm, M), min(tn, N), min(tk, K)
    assert M % tm == 0 and N % tn == 0 and K % tk == 0, (name, M, N, K, tm, tn, tk)
    nk = K // tk
    dn = {"nn": _NN, "nt": _NT, "tn": _TN}[mode]

    def body(*refs):
        if add is None:
            a_ref, b_ref, o_ref, acc_ref = refs
        else:
            a_ref, b_ref, add_ref, o_ref, acc_ref = refs
        k = pl.program_id(2)

        @pl.when(k == 0)
        def _():
            acc_ref[...] = jnp.zeros_like(acc_ref)

        acc_ref[...] += _bdot(a_ref[...], b_ref[...], dn)

        @pl.when(k == nk - 1)
        def _():
            o = acc_ref[...]
            if add is not None:
                o = o + add_ref[...]
            o_ref[...] = o.astype(o_ref.dtype)

    a_spec = pl.BlockSpec((tk, tm), lambda i, j, k: (k, i)) if mode == "tn" else pl.BlockSpec((tm, tk), lambda i, j, k: (i, k))
    b_spec = pl.BlockSpec((tn, tk), lambda i, j, k: (j, k)) if mode == "nt" else pl.BlockSpec((tk, tn), lambda i, j, k: (k, j))
    o_spec = pl.BlockSpec((tm, tn), lambda i, j, k: (i, j))
    in_specs, args = [a_spec, b_spec], [a, b]
    if add is not None:
        in_specs.append(o_spec)
        args.append(add)
    return pl.pallas_call(
        body, name=name, grid=(M // tm, N // tn, nk), in_specs=in_specs, out_specs=o_spec,
        out_shape=SDS((M, N), out_dtype), scratch_shapes=[pltpu.VMEM((tm, tn), F32)],
        compiler_params=_cparams(dimension_semantics=("parallel", "parallel", "arbitrary")),
    )(*args)


def _row_spec(tm, spec):
    _, c0, w = spec
    assert c0 % w == 0
    return pl.BlockSpec((tm, w), functools.partial(lambda i, cb: (i, cb), cb=c0 // w))


def _par_spec(spec):
    arr, c0, w = spec
    assert c0 % w == 0
    return pl.BlockSpec((arr.shape[0], w), functools.partial(lambda i, cb: (0, cb), cb=c0 // w))


def _whole(arr):
    return (arr, 0, arr.shape[1])


def _rw_fwd(fn, rows, params, outs, *, tm, name):
    M = rows[0][0].shape[0]
    nr, npar = len(rows), len(params)

    def body(*refs):
        rv = [r[...].astype(F32) for r in refs[:nr]]
        pv = [p[...].astype(F32) for p in refs[nr:nr + npar]]
        res = fn(*rv, *pv)
        for o_ref, v in zip(refs[nr + npar:], res, strict=True):
            o_ref[...] = v.astype(o_ref.dtype)

    return pl.pallas_call(
        body, name=name, grid=(M // tm,),
        in_specs=[_row_spec(tm, r) for r in rows] + [_par_spec(p) for p in params],
        out_specs=[pl.BlockSpec((tm, w), lambda i: (i, 0)) for w, _ in outs],
        out_shape=[SDS((M, w), dt) for w, dt in outs],
        compiler_params=_cparams(dimension_semantics=("parallel",)),
    )(*[r[0] for r in rows], *[p[0] for p in params])


def _rw_bwd(fn, rows, params, cts, *, tm, name, row_grads, adds=None):
    M = rows[0][0].shape[0]
    adds = adds or {}
    nr, npar, nc = len(rows), len(params), len(cts)
    add_keys = sorted(adds)
    want = [k for k in range(nr) if row_grads[k] is not None]

    def body(*refs):
        pos = 0
        r_refs = refs[pos:pos + nr]; pos += nr
        p_refs = refs[pos:pos + npar]; pos += npar
        c_refs = refs[pos:pos + nc]; pos += nc
        a_refs = dict(zip(add_keys, refs[pos:pos + len(add_keys)])); pos += len(add_keys)
        dr_refs = dict(zip(want, refs[pos:pos + len(want)])); pos += len(want)
        dp_refs = refs[pos:pos + npar]
        rv = [r[...].astype(F32) for r in r_refs]
        pv = [p[...].astype(F32) for p in p_refs]
        _, vjp = jax.vjp(fn, *rv, *pv)
        g = vjp(tuple(c[...].astype(F32) for c in c_refs))
        for k in want:
            v = g[k]
            if k in a_refs:
                v = v + a_refs[k][...].astype(F32)
            dr_refs[k][...] = v.astype(dr_refs[k].dtype)
        first = pl.program_id(0) == 0
        for j in range(npar):
            @pl.when(first)
            def _(j=j):
                dp_refs[j][...] = jnp.zeros_like(dp_refs[j])
            dp_refs[j][...] += g[nr + j]

    res = pl.pallas_call(
        body, name=name, grid=(M // tm,),
        in_specs=([_row_spec(tm, r) for r in rows] + [_par_spec(p) for p in params] + [_row_spec(tm, c) for c in cts]
                  + [_row_spec(tm, adds[k]) for k in add_keys]),
        out_specs=([pl.BlockSpec((tm, rows[k][2]), lambda i: (i, 0)) for k in want]
                   + [pl.BlockSpec((p[0].shape[0], p[2]), lambda i: (0, 0)) for p in params]),
        out_shape=([SDS((M, rows[k][2]), row_grads[k]) for k in want] + [SDS((p[0].shape[0], p[2]), F32) for p in params]),
        compiler_params=_cparams(dimension_semantics=("arbitrary",)),
    )(*[r[0] for r in rows], *[p[0] for p in params], *[c[0] for c in cts], *[adds[k][0] for k in add_keys])
    return res


def _rms_fn(x, g):
    r = lax.rsqrt(jnp.mean(x * x, axis=-1, keepdims=True) + EPS)
    return (x * r * g,)


def _seg_mats(width, seg):
    n = width // seg
    p = (_iota((width, n), 0) // seg == _iota((width, n), 1)).astype(F32)
    e = (_iota((n, width), 1) // seg == _iota((n, width), 0)).astype(F32)
    return p, e


def _headnorm_fn(q, g):
    p, e = _seg_mats(ATT_HEADS * ATT_D, ATT_D)
    t = (_iota((ATT_D, ATT_HEADS * ATT_D), 1) % ATT_D == _iota((ATT_D, ATT_HEADS * ATT_D), 0)).astype(F32)
    ms = _cdot(q * q, p) * (1.0 / ATT_D)
    r = _cdot(lax.rsqrt(ms + EPS), e)
    return (q * r * _cdot(g, t),)


def _gate_fn(y, xs, z, dskip, w):
    width = SSM_HEADS * SSM_P
    _, e = _seg_mats(width, SSM_P)
    y = (y + _cdot(dskip, e) * xs) * (z * jax.nn.sigmoid(z))
    g0 = _iota((1, width), 1) < width // SSM_GROUPS
    y2 = y * y
    gw = width // SSM_GROUPS
    ms0 = jnp.sum(jnp.where(g0, y2, 0.0), axis=-1, keepdims=True) * (1.0 / gw)
    ms1 = jnp.sum(jnp.where(g0, 0.0, y2), axis=-1, keepdims=True) * (1.0 / gw)
    r = jnp.where(g0, lax.rsqrt(ms0 + EPS), lax.rsqrt(ms1 + EPS))
    return (y * r * w,)


def _relu2_fn(a):
    u = jax.nn.relu(a)
    return (u * u,)


def _xattn_fn(q0, q1, q2, q3, k0, k1, k2, k3, v0, v1, v2, v3, gq, gk):
    def norm(u, g):
        return u * lax.rsqrt(jnp.mean(u * u, axis=-1, keepdims=True) + EPS) * g
    outs = []
    for q, k, v in ((q0, k0, v0), (q1, k1, v1), (q2, k2, v2), (q3, k3, v3)):
        s = mm_nt(norm(q, gq), norm(k, gk)) * (X_D ** -0.5)
        p = jnp.exp(s - lax.stop_gradient(jnp.max(s, axis=-1, keepdims=True)))
        p = p / jnp.sum(p, axis=-1, keepdims=True)
        outs.append(mm_nn(p, v))
    return (jnp.concatenate(outs, axis=-1),)


CONV_TC = 256


def _shift_down(u, k):
    if k == 0:
        return u
    return jnp.where(_iota(u.shape, 0) >= k, pltpu.roll(u, k, axis=0), 0.0)


def _shift_up(u, k):
    if k == 0:
        return u
    n = u.shape[0]
    return jnp.where(_iota(u.shape, 0) < n - k, pltpu.roll(u, n - k, axis=0), 0.0)


def _conv_pre(u, w_ref, b):
    pre = b + w_ref[3:4, :] * u
    for k in (1, 2, 3):
        pre = pre + w_ref[3 - k:4 - k, :] * _shift_down(u, k)
    return pre


def _conv_fwd(src, c0, width, w, b, *, name):
    S = src.shape[0]
    cb0 = c0 // CONV_TC

    def body(u_ref, w_ref, b_ref, o_ref):
        pre = _conv_pre(u_ref[...], w_ref, b_ref[...])
        o_ref[...] = pre * jax.nn.sigmoid(pre)

    return pl.pallas_call(
        body, name=name, grid=(width // CONV_TC,),
        in_specs=[pl.BlockSpec((S, CONV_TC), lambda j: (0, cb0 + j)), pl.BlockSpec((4, CONV_TC), lambda j: (0, j)),
                  pl.BlockSpec((1, CONV_TC), lambda j: (0, j))],
        out_specs=pl.BlockSpec((S, CONV_TC), lambda j: (0, j)), out_shape=SDS((S, width), F32),
        compiler_params=_cparams(dimension_semantics=("parallel",)),
    )(src, w, b)


def _conv_bwd(src, c0, width, w, b, douts, *, name):
    S = src.shape[0]
    cb0 = c0 // CONV_TC
    nd = len(douts)

    def body(*refs):
        u_ref, w_ref, b_ref = refs[:3]
        d_refs = refs[3:3 + nd]
        du_ref, dw_ref, db_ref = refs[3 + nd:]
        u = u_ref[...]
        pre = _conv_pre(u, w_ref, b_ref[...])
        sg = jax.nn.sigmoid(pre)
        dout = d_refs[0][...]
        for r in d_refs[1:]:
            dout = dout + r[...]
        dpre = dout * (sg * (1.0 + pre * (1.0 - sg)))
        du = w_ref[3:4, :] * dpre
        dw_ref[3:4, :] = jnp.sum(dpre * u, axis=0, keepdims=True)
        for k in (1, 2, 3):
            du = du + w_ref[3 - k:4 - k, :] * _shift_up(dpre, k)
            dw_ref[3 - k:4 - k, :] = jnp.sum(dpre * _shift_down(u, k), axis=0, keepdims=True)
        du_ref[...] = du.astype(du_ref.dtype)
        db_ref[...] = jnp.sum(dpre, axis=0, keepdims=True)

    return pl.pallas_call(
        body, name=name, grid=(width // CONV_TC,),
        in_specs=[pl.BlockSpec((S, CONV_TC), lambda j: (0, cb0 + j)), pl.BlockSpec((4, CONV_TC), lambda j: (0, j)),
                  pl.BlockSpec((1, CONV_TC), lambda j: (0, j))] + [pl.BlockSpec((S, CONV_TC), lambda j: (0, j))] * nd,
        out_specs=[pl.BlockSpec((S, CONV_TC), lambda j: (0, j)), pl.BlockSpec((4, CONV_TC), lambda j: (0, j)),
                   pl.BlockSpec((1, CONV_TC), lambda j: (0, j))],
        out_shape=[SDS((S, width), BF16), SDS((4, width), F32), SDS((1, width), F32)],
        compiler_params=_cparams(dimension_semantics=("parallel",)),
    )(src, w, b, *douts)


def _softplus(x):
    return jnp.maximum(x, 0.0) + jnp.log(1.0 + jnp.exp(-jnp.abs(x)))


def _prefix_sum(x, seg):
    n = x.shape[1]
    pos = _iota(x.shape, 1) % seg
    k = 1
    while k < seg:
        x = x + jnp.where(pos >= k, pltpu.roll(x, k, axis=1), 0.0)
        k *= 2
    return x


def _suffix_sum(x, seg):
    n = x.shape[1]
    pos = _iota(x.shape, 1) % seg
    k = 1
    while k < seg:
        x = x + jnp.where(pos + k < seg, pltpu.roll(x, n - k, axis=1), 0.0)
        k *= 2
    return x


def _dtf_fwd(dtf_t, dt_bias, a_log, f_bias):
    S = dtf_t.shape[1]

    def body(x_ref, db_ref, al_ref, fb_ref, dt_ref, acs_ref, cum_ref):
        dt = _softplus(x_ref[0:16, :] + db_ref[...])
        dt_ref[...] = dt
        acs_ref[...] = _prefix_sum(dt * (-jnp.exp(al_ref[...])), CHUNK)
        cum_ref[...] = _prefix_sum(-_softplus(-(x_ref[16:32, :] + fb_ref[...])), S)

    return pl.pallas_call(body, name="dtf_fwd", out_shape=[SDS((16, S), F32)] * 3, compiler_params=_cparams())(
        dtf_t, dt_bias, a_log, f_bias)


def _dtf_bwd(dtf_t, dt_bias, a_log, f_bias, d_dt, d_acs_a, d_acs_b, d_cum):
    S = dtf_t.shape[1]

    def body(x_ref, db_ref, al_ref, fb_ref, ddt_ref, da1_ref, da2_ref, dc_ref, dx_ref, ddb_ref, dal_ref, dfb_ref):
        xd = x_ref[0:16, :] + db_ref[...]
        dt = _softplus(xd)
        a = -jnp.exp(al_ref[...])
        d_da = _suffix_sum(da1_ref[...] + da2_ref[...], CHUNK)
        d_dt = ddt_ref[...] + d_da * a
        dal_ref[...] = jnp.sum(d_da * dt, axis=1, keepdims=True) * a
        d_xd = d_dt * jax.nn.sigmoid(xd)
        ddb_ref[...] = jnp.sum(d_xd, axis=1, keepdims=True)
        xf = x_ref[16:32, :] + fb_ref[...]
        d_xf = _suffix_sum(dc_ref[...], S) * jax.nn.sigmoid(-xf)
        dfb_ref[...] = jnp.sum(d_xf, axis=1, keepdims=True)
        dx_ref[0:16, :] = d_xd
        dx_ref[16:32, :] = d_xf

    return pl.pallas_call(body, name="dtf_bwd", out_shape=[SDS((32, S), F32)] + [SDS((16, 1), F32)] * 3,
                          compiler_params=_cparams())(dtf_t, dt_bias, a_log, f_bias, d_dt, d_acs_a, d_acs_b, d_cum)


def _ssd_chunk(xs, dtc, acol, arow, bm, cm, h, *, hp):
    L = CHUNK
    first = _iota((1, LANES), 1) < SSM_P
    i16, s16 = _iota((L, 16), 1), _iota((16, L), 0)
    ha, hb = 2 * hp, 2 * hp + 1

    def selc(blk, hh):
        return jnp.sum(jnp.where(i16 == hh, blk, 0.0), axis=1, keepdims=True)

    def selr(blk, hh):
        return jnp.sum(jnp.where(s16 == hh, blk, 0.0), axis=0, keepdims=True)

    x = xs * jnp.where(first, selc(dtc, ha), selc(dtc, hb))
    ca, cb, ra, rb = selc(acol, ha), selc(acol, hb), selr(arow, ha), selr(arow, hb)
    tri = _iota((L, L), 0) >= _iota((L, L), 1)
    cbm = mm_nt(cm, bm)
    la = jnp.exp(jnp.where(tri, ca - ra, NEG))
    lb = jnp.exp(jnp.where(tri, cb - rb, NEG))
    y = jnp.where(first, mm_nn(cbm * la, x), mm_nn(cbm * lb, x))
    y = y + jnp.where(first, jnp.exp(ca), jnp.exp(cb)) * mm_nn(cm, h)
    last = _iota((1, L), 1) == L - 1
    ala = jnp.sum(jnp.where(last, ra, 0.0), axis=1, keepdims=True)
    alb = jnp.sum(jnp.where(last, rb, 0.0), axis=1, keepdims=True)
    dec = jnp.where(first, jnp.exp(ala - ca), jnp.exp(alb - cb))
    hn = jnp.where(first, jnp.exp(ala), jnp.exp(alb)) * h + mm_tn(bm, x * dec)
    return y, hn


def _ssd_specs(nc):
    L = CHUNK
    hpg = SSM_HEADS // 2 // SSM_GROUPS

    def mk(rev):
        cidx = (lambda c: nc - 1 - c) if rev else (lambda c: c)
        return dict(
            xs=pl.BlockSpec((L, LANES), lambda c, hp: (cidx(c), hp)),
            col=pl.BlockSpec((L, 16), lambda c, hp: (cidx(c), 0)),
            row=pl.BlockSpec((16, L), lambda c, hp: (0, cidx(c))),
            b=pl.BlockSpec((L, SSM_N), lambda c, hp: (cidx(c), hp // hpg)),
            c=pl.BlockSpec((L, SSM_N), lambda c, hp: (cidx(c), SSM_GROUPS + hp // hpg)),
            grp=pl.BlockSpec((L, SSM_N), lambda c, hp: (cidx(c), hp // hpg)),
            st=pl.BlockSpec((1, 1, SSM_N, LANES), lambda c, hp: (cidx(c), hp, 0, 0)),
        )
    return mk


def _ssd_fwd(xs, dt_col, acs_col, acs_row, bc):
    S = xs.shape[0]
    nc, nhp = S // CHUNK, SSM_HEADS // 2
    sp = _ssd_specs(nc)(False)

    def body(xs_ref, dt_ref, ac_ref, ar_ref, b_ref, c_ref, y_ref, hs_ref, h_scr):
        c, hp = pl.program_id(0), pl.program_id(1)

        @pl.when(c == 0)
        def _():
            h_scr[hp] = jnp.zeros((SSM_N, LANES), F32)

        h = h_scr[hp]
        hs_ref[0, 0] = h
        y, hn = _ssd_chunk(xs_ref[...], dt_ref[...], ac_ref[...], ar_ref[...], b_ref[...], c_ref[...], h, hp=hp)
        y_ref[...] = y
        h_scr[hp] = hn

    return pl.pallas_call(
        body, name="ssd_fwd", grid=(nc, nhp),
        in_specs=[sp["xs"], sp["col"], sp["col"], sp["row"], sp["b"], sp["c"]],
        out_specs=[sp["xs"], sp["st"]],
        out_shape=[SDS((S, SSM_HEADS * SSM_P), F32), SDS((nc, nhp, SSM_N, LANES), F32)],
        scratch_shapes=[pltpu.VMEM((nhp, SSM_N, LANES), F32)],
        compiler_params=_cparams(dimension_semantics=("arbitrary", "arbitrary")),
    )(xs, dt_col, acs_col, acs_row, bc, bc)


def _ssd_bwd(xs, dt_col, acs_col, acs_row, bc, hs, dy):
    S = xs.shape[0]
    nc, nhp = S // CHUNK, SSM_HEADS // 2
    hpg = nhp // SSM_GROUPS
    sp = _ssd_specs(nc)(True)

    def body(xs_ref, dt_ref, ac_ref, ar_ref, b_ref, c_ref, hs_ref, dy_ref,
             dxs_ref, ddt_ref, dac_ref, dar_ref, db_ref, dc_ref, dh_scr):
        c, hp = pl.program_id(0), pl.program_id(1)

        @pl.when(c == 0)
        def _():
            dh_scr[hp] = jnp.zeros((SSM_N, LANES), F32)

        _, vjp = jax.vjp(functools.partial(_ssd_chunk, hp=hp), xs_ref[...], dt_ref[...], ac_ref[...], ar_ref[...],
                         b_ref[...], c_ref[...], hs_ref[0, 0])
        dxs, ddt, dac, dar, db, dc, dh = vjp((dy_ref[...], dh_scr[hp]))
        dxs_ref[...] = dxs
        dh_scr[hp] = dh

        @pl.when(hp == 0)
        def _():
            ddt_ref[...] = jnp.zeros_like(ddt_ref)
            dac_ref[...] = jnp.zeros_like(dac_ref)
            dar_ref[...] = jnp.zeros_like(dar_ref)

        ddt_ref[...] += ddt
        dac_ref[...] += dac
        dar_ref[...] += dar

        @pl.when(hp % hpg == 0)
        def _():
            db_ref[...] = jnp.zeros_like(db_ref)
            dc_ref[...] = jnp.zeros_like(dc_ref)

        db_ref[...] += db
        dc_ref[...] += dc

    return pl.pallas_call(
        body, name="ssd_bwd", grid=(nc, nhp),
        in_specs=[sp["xs"], sp["col"], sp["col"], sp["row"], sp["b"], sp["c"], sp["st"], sp["xs"]],
        out_specs=[sp["xs"], sp["col"], sp["col"], sp["row"], sp["grp"], sp["grp"]],
        out_shape=[SDS((S, SSM_HEADS * SSM_P), F32), SDS((S, 16), F32), SDS((S, 16), F32), SDS((16, S), F32),
                   SDS((S, SSM_GROUPS * SSM_N), F32), SDS((S, SSM_GROUPS * SSM_N), F32)],
        scratch_shapes=[pltpu.VMEM((nhp, SSM_N, LANES), F32)],
        compiler_params=_cparams(dimension_semantics=("arbitrary", "arbitrary")),
    )(xs, dt_col, acs_col, acs_row, bc, bc, hs, dy)


ATT_T = 512


def _pick_col(blk, h):
    return jnp.sum(jnp.where(_iota(blk.shape, 1) == h, blk, 0.0), axis=1, keepdims=True)


def _pick_row(blk, h):
    return jnp.sum(jnp.where(_iota(blk.shape, 0) == h, blk, 0.0), axis=0, keepdims=True)


def _fox_fwd(qn, kn, vsrc, v_c0, cum_col, cum_row3):
    S = qn.shape[0]
    T = ATT_T
    nq, nhp = S // T, ATT_HEADS // 2
    vb0 = v_c0 // LANES
    scale = ATT_D ** -0.5

    def body(q_ref, k_ref, v_ref, cc_ref, cr_ref, o_ref, l_ref):
        hp, i = pl.program_id(0), pl.program_id(1)
        first = _iota((1, LANES), 1) < ATT_D
        q = q_ref[...] * scale
        zero = jnp.zeros_like(q)
        qs = (jnp.where(first, q, zero), jnp.where(first, zero, q))
        cc = cc_ref[...]
        cq = (_pick_col(cc, 2 * hp), _pick_col(cc, 2 * hp + 1))
        tri = _iota((T, T), 0) >= _iota((T, T), 1)

        def tile(j, carry, diagonal):
            off = pl.multiple_of(j * T, T)
            k = k_ref[pl.ds(off, T), :]
            v = v_ref[pl.ds(off, T), :].astype(BF16)
            cr = cr_ref[j]
            out = []
            for hh in range(2):
                m, l, acc = carry[3 * hh:3 * hh + 3]
                s = _bdot(qs[hh], k, _NT) + (cq[hh] - _pick_row(cr, 2 * hp + hh))
                if diagonal:
                    s = jnp.where(tri, s, NEG)
                m_new = jnp.maximum(m, jnp.max(s, axis=1, keepdims=True))
                alpha = jnp.exp(m - m_new)
                p = jnp.exp(s - m_new)
                out += [m_new, alpha * l + jnp.sum(p, axis=1, keepdims=True), alpha * acc + _bdot(p, v, _NN)]
            return tuple(out)

        init = (jnp.full((T, 1), NEG, F32), jnp.zeros((T, 1), F32), jnp.zeros((T, LANES), F32)) * 2
        carry = lax.fori_loop(0, i, lambda j, c: tile(j, c, False), init)
        ma, la, acca, mb, lb, accb = tile(i, carry, True)
        o_ref[...] = jnp.where(first, acca / la, accb / lb)
        l_ref[...] = jnp.where(first, ma + jnp.log(la), mb + jnp.log(lb))

    return pl.pallas_call(
        body, name="fox_fwd", grid=(nhp, nq),
        in_specs=[pl.BlockSpec((T, LANES), lambda hp, i: (i, hp)), pl.BlockSpec((S, LANES), lambda hp, i: (0, hp)),
                  pl.BlockSpec((S, LANES), lambda hp, i: (0, vb0 + hp)), pl.BlockSpec((T, 16), lambda hp, i: (i, 0)),
                  pl.BlockSpec((nq, 16, T), lambda hp, i: (0, 0, 0))],
        out_specs=[pl.BlockSpec((T, LANES), lambda hp, i: (i, hp))] * 2,
        out_shape=[SDS((S, ATT_HEADS * ATT_D), F32)] * 2,
        compiler_params=_cparams(dimension_semantics=("parallel", "arbitrary")),
    )(qn, kn, vsrc, cum_col, cum_row3)


def _fox_bwd(qn, kn, vsrc, v_c0, cum_col, cum_row3, o, lse, dsrc, d_c0):
    S = qn.shape[0]
    T = ATT_T
    nq, nhp = S // T, ATT_HEADS // 2
    vb0, db0 = v_c0 // LANES, d_c0 // LANES
    scale = ATT_D ** -0.5

    def body(q_ref, k_ref, v_ref, cc_ref, cr_ref, o_ref, l_ref, do_ref, dq_ref, dk_ref, dv_ref, dc_ref, dv_scr):
        hp, j = pl.program_id(0), pl.program_id(1)
        first = _iota((1, LANES), 1) < ATT_D
        k = k_ref[...] * scale
        zk = jnp.zeros_like(k)
        ks = (jnp.where(first, k, zk), jnp.where(first, zk, k))
        v = v_ref[...].astype(BF16)
        cr = cr_ref[0]
        ck = (_pick_row(cr, 2 * hp), _pick_row(cr, 2 * hp + 1))
        tri = _iota((T, T), 0) >= _iota((T, T), 1)

        @pl.when(j == 0)
        def _():
            dq_ref[...] = jnp.zeros_like(dq_ref)

        def tile(i, carry, diagonal):
            dca, dcb = carry
            dcs = [dca, dcb]
            off = pl.multiple_of(i * T, T)
            q = q_ref[pl.ds(off, T), :]
            zq = jnp.zeros_like(q)
            qs = (jnp.where(first, q, zq), jnp.where(first, zq, q))
            do = do_ref[pl.ds(off, T), :]
            dd = do * o_ref[pl.ds(off, T), :]
            lse_blk = l_ref[pl.ds(off, T), :]
            cc = cc_ref[pl.ds(off, T), :]
            dob = do.astype(BF16)
            zd = jnp.zeros_like(dob)
            dos = (jnp.where(first, dob, zd), jnp.where(first, zd, dob))
            dq = jnp.zeros((T, LANES), F32)
            for hh in range(2):
                sel = first if hh == 0 else jnp.logical_not(first)
                delta = jnp.sum(jnp.where(sel, dd, 0.0), axis=1, keepdims=True)
                lse_h = jnp.max(jnp.where(sel, lse_blk, NEG), axis=1, keepdims=True)
                s = _bdot(qs[hh], k, _NT) + ((_pick_col(cc, 2 * hp + hh) - lse_h) - ck[hh])
                if diagonal:
                    s = jnp.where(tri, s, NEG)
                p = jnp.exp(s)
                ds = p * (_bdot(dos[hh], v, _NT) - delta)
                dv_scr[...] += _bdot(p, dos[hh], _TN)
                dk_ref[...] += _bdot(ds, qs[hh], _TN)
                dq = dq + _bdot(ds, ks[hh], _NN)
                dcs[hh] = dcs[hh] - jnp.sum(ds, axis=0, keepdims=True)
            dq_ref[pl.ds(off, T), :] += dq
            return dcs[0], dcs[1]

        dk_ref[...] = jnp.zeros_like(dk_ref)
        dv_scr[...] = jnp.zeros_like(dv_scr)
        z1 = jnp.zeros((1, T), F32)
        carry = tile(j, (z1, z1), True)
        dca, dcb = lax.fori_loop(j + 1, nq, lambda i, c: tile(i, c, False), carry)
        dk_ref[...] = dk_ref[...] * scale
        dv_ref[...] = dv_scr[...].astype(dv_ref.dtype)
        dc_ref[0, 0] = jnp.zeros((8, T), F32)
        dc_ref[0, 0, 0:1, :] = dca
        dc_ref[0, 0, 1:2, :] = dcb

    return pl.pallas_call(
        body, name="fox_bwd", grid=(nhp, nq),
        in_specs=[pl.BlockSpec((S, LANES), lambda hp, j: (0, hp)), pl.BlockSpec((T, LANES), lambda hp, j: (j, hp)),
                  pl.BlockSpec((T, LANES), lambda hp, j: (j, vb0 + hp)), pl.BlockSpec((S, 16), lambda hp, j: (0, 0)),
                  pl.BlockSpec((1, 16, T), lambda hp, j: (j, 0, 0)), pl.BlockSpec((S, LANES), lambda hp, j: (0, hp)),
                  pl.BlockSpec((S, LANES), lambda hp, j: (0, hp)), pl.BlockSpec((S, LANES), lambda hp, j: (0, db0 + hp))],
        out_specs=[pl.BlockSpec((S, LANES), lambda hp, j: (0, hp)), pl.BlockSpec((T, LANES), lambda hp, j: (j, hp)),
                   pl.BlockSpec((T, LANES), lambda hp, j: (j, hp)), pl.BlockSpec((1, 1, 8, T), lambda hp, j: (hp, j, 0, 0))],
        out_shape=[SDS((S, ATT_HEADS * ATT_D), F32), SDS((S, ATT_HEADS * ATT_D), F32), SDS((S, ATT_HEADS * ATT_D), BF16),
                   SDS((nhp, nq, 8, T), F32)],
        scratch_shapes=[pltpu.VMEM((T, LANES), F32)],
        compiler_params=_cparams(dimension_semantics=("parallel", "arbitrary")),
    )(qn, kn, vsrc, cum_col, cum_row3, o, lse, dsrc)


def _loss_head(y, target, *, tm):
    M, W = y.shape

    def body(y_ref, t_ref, dy_ref, loss_ref):
        @pl.when(pl.program_id(0) == 0)
        def _():
            loss_ref[...] = jnp.zeros_like(loss_ref)

        e = y_ref[...] - t_ref[...]
        dy_ref[...] = e * (1.0 / W)
        loss_ref[...] += jnp.sum(jnp.sum(e * e, axis=1, keepdims=True), axis=0, keepdims=True) * (0.5 / W)

    return pl.pallas_call(
        body, name="loss_head", grid=(M // tm,),
        in_specs=[pl.BlockSpec((tm, W), lambda i: (i, 0))] * 2,
        out_specs=[pl.BlockSpec((tm, W), lambda i: (i, 0)), pl.BlockSpec((1, 1), lambda i: (0, 0))],
        out_shape=[SDS((M, W), F32), SDS((1, 1), F32)],
        compiler_params=_cparams(dimension_semantics=("arbitrary",)),
    )(y, target)


def _adamw_math(w, g, m, v):
    m = ADAM_B1 * m + (1.0 - ADAM_B1) * g
    v = ADAM_B2 * v + (1.0 - ADAM_B2) * jnp.square(g)
    m_hat = m / (1.0 - ADAM_B1 ** ADAM_STEP)
    v_hat = v / (1.0 - ADAM_B2 ** ADAM_STEP)
    delta = -ADAM_LR * (m_hat / (jnp.sqrt(v_hat) + ADAM_EPS) + ADAM_WD * w)
    return delta, m, v


def _reduce_adamw(parts, w, m, v, *, tr, name):
    R, C = w.shape
    tr = min(tr, R)

    def body(p_ref, w_ref, m_ref, v_ref, g_ref, d_ref, nm_ref, nv_ref):
        g = p_ref[0].astype(F32)
        for s in range(1, N_DEV):
            g = g + p_ref[s].astype(F32)
        g_ref[...] = g
        d_ref[...], nm_ref[...], nv_ref[...] = _adamw_math(w_ref[...], g, m_ref[...], v_ref[...])

    blk = pl.BlockSpec((tr, C), lambda i: (i, 0))
    return pl.pallas_call(
        body, name=name, grid=(R // tr,),
        in_specs=[pl.BlockSpec((N_DEV, tr, C), lambda i: (0, i, 0)), blk, blk, blk], out_specs=[blk] * 4,
        out_shape=[SDS((R, C), F32)] * 4, compiler_params=_cparams(dimension_semantics=("parallel",)),
    )(parts, w, m, v)


def _adamw(w, g, m, v, *, name):
    def body(w_ref, g_ref, m_ref, v_ref, d_ref, nm_ref, nv_ref):
        d_ref[...], nm_ref[...], nv_ref[...] = _adamw_math(w_ref[...], g_ref[...], m_ref[...], v_ref[...])

    return pl.pallas_call(body, name=name, out_shape=[SDS(w.shape, F32)] * 3, compiler_params=_cparams())(w, g, m, v)


def _exchange(arrays, *, scatter, name):
    n = len(arrays)
    npeer = N_DEV - 1

    def body(*refs):
        ins, outs = refs[:n], refs[n:2 * n]
        send_sems, recv_sems, local_sems = refs[2 * n:]
        x, y, c = lax.axis_index("x"), lax.axis_index("y"), lax.axis_index("c")
        me = 4 * x + 2 * y + c
        peers = []
        for k in range(1, N_DEV):
            px, py, pc = x ^ ((k >> 2) & 1), y ^ ((k >> 1) & 1), c ^ (k & 1)
            peers.append(((px, py, pc), 4 * px + 2 * py + pc))
        started = []
        for a in range(n):
            src_me = ins[a].at[me] if scatter else ins[a]
            local = pltpu.make_async_copy(src_me, outs[a].at[me], local_sems.at[a])
            local.start()
            started.append(local)
        remote = []
        for a in range(n):
            for k, (dev, idx) in enumerate(peers):
                cp = pltpu.make_async_remote_copy(
                    src_ref=ins[a].at[idx] if scatter else ins[a], dst_ref=outs[a].at[me],
                    send_sem=send_sems.at[a * npeer + k], recv_sem=recv_sems.at[a * npeer + k],
                    device_id=dev, device_id_type=MESH)
                cp.start()
                remote.append((cp, a, k, idx))
        for cp, a, k, idx in remote:
            cp.wait_send()
        for cp, a, k, idx in remote:
            pltpu.make_async_remote_copy(
                src_ref=ins[a].at[idx] if scatter else ins[a], dst_ref=outs[a].at[idx],
                send_sem=send_sems.at[a * npeer + k], recv_sem=recv_sems.at[a * npeer + k],
                device_id=peers[k][0], device_id_type=MESH).wait_recv()
        for local in started:
            local.wait()

    def out_sds(arr):
        return SDS(arr.shape if scatter else (N_DEV,) + arr.shape, arr.dtype)

    any_spec = pl.BlockSpec(memory_space=pl.ANY)
    return pl.pallas_call(
        body, name=name, in_specs=[any_spec] * n, out_specs=[any_spec] * n, out_shape=[out_sds(a) for a in arrays],
        scratch_shapes=[pltpu.SemaphoreType.DMA((n * npeer,)), pltpu.SemaphoreType.DMA((n * npeer,)),
                        pltpu.SemaphoreType.DMA((n,))],
        compiler_params=pltpu.CompilerParams(has_side_effects=True),
    )(*arrays)


SMALL = (("g_mix", 1024), ("conv_w", 6144), ("conv_b", 1536), ("dt_bias", 16), ("a_log", 16), ("d_skip", 16),
         ("ssm_norm_w", 1024), ("g_q", 64), ("g_k", 64), ("f_bias", 16), ("g_xattn", 1024), ("g_mem", 1024),
         ("xg_q", 256), ("xg_k", 256), ("g_mlp", 1024))
SLAB_ROWS = 112
BIG = ("w_in", "w_out", "xq_w", "xkv_w", "xo_w", "w_up", "w_down")
WEIGHTS = ("g_mix", "w_in", "conv_w", "conv_b", "dt_bias", "a_log", "d_skip", "ssm_norm_w", "g_q", "g_k", "f_bias", "w_out",
           "g_xattn", "g_mem", "xq_w", "xkv_w", "xg_q", "xg_k", "xo_w", "g_mlp", "w_up", "w_down")
O_Z, O_XS, O_B, O_C, O_DT, O_Q, O_K, O_V, O_F, O_END = 0, 1024, 2048, 2304, 2560, 2576, 3600, 4624, 5648, 5664


def _pack_small(vals):
    rows = []
    for name, size in SMALL:
        flat = vals[name].reshape(-1).astype(F32)
        pad = -size % LANES
        rows.append(jnp.pad(flat, (0, pad)).reshape(-1, LANES))
    slab = jnp.concatenate(rows, axis=0)
    return jnp.pad(slab, ((0, SLAB_ROWS - slab.shape[0]), (0, 0)))


def _unpack_small(slab):
    out, r = {}, 0
    for name, size in SMALL:
        nr = -(-size // LANES)
        out[name] = slab[r:r + nr].reshape(-1)[:size]
        r += nr
    return out


def _cols(a, lo, hi):
    return a[:, lo:hi]


def _step(p, m, v, x, mem, target):
    S = x.shape[0]
    TM = 256
    me = 4 * lax.axis_index("x") + 2 * lax.axis_index("y") + lax.axis_index("c")

    gathered = _exchange([p[n].astype(BF16) for n in BIG] + [p["conv_w"]], scatter=False, name="allgather_weights")
    win_g, wout_g, xq_g, xkv_g, xo_g, wup_g, wdown_g, convw_g = gathered
    w_in_o = win_g.transpose(1, 0, 2).reshape(D_MODEL, O_END)
    w_in = jnp.concatenate(
        [_cols(w_in_o, O_Z, O_XS), _cols(w_in_o, O_XS, O_B), _cols(w_in_o, O_Q, O_K), _cols(w_in_o, O_K, O_V),
         _cols(w_in_o, O_V, O_F), _cols(w_in_o, O_B, O_C), _cols(w_in_o, O_C, O_DT), _cols(w_in_o, O_DT, O_Q),
         _cols(w_in_o, O_F, O_END), jnp.zeros((D_MODEL, P_COLS - C_DTF - 32), BF16)], axis=1)
    w_out = wout_g.reshape(2 * D_MODEL, D_MODEL)
    xq_w = xq_g.reshape(D_MODEL, D_MODEL)
    xkv_w = xkv_g.transpose(1, 0, 2).reshape(D_MODEL, 2 * D_MODEL)
    xo_w = xo_g.reshape(D_MODEL, D_MODEL)
    w_up = wup_g.transpose(1, 0, 2).reshape(D_MODEL, 4 * D_MODEL)
    w_down = wdown_g.reshape(4 * D_MODEL, D_MODEL)
    conv_w = convw_g.transpose(1, 0, 2).reshape(4, 1536)
    cw_xs, cw_bc = conv_w[:, :1024], conv_w[:, 1024:]
    cb_xs, cb_bc = p["conv_b"][:, :1024], p["conv_b"][:, 1024:]
    dt_bias, a_log, f_bias = p["dt_bias"].reshape(16, 1), p["a_log"].reshape(16, 1), p["f_bias"].reshape(16, 1)

    def rms(u, g, name):
        return _rw_fwd(_rms_fn, [_whole(u)], [_whole(g)], [(D_MODEL, BF16)], tm=TM, name=name)[0]

    h1 = rms(x, p["g_mix"], "rms_mix")
    proj = _matmul(h1, w_in, mode="nn", tm=1024, tn=640, tk=1024, name="mm_in")
    xs_c = _conv_fwd(proj, C_XS, 1024, cw_xs, cb_xs, name="conv_xs")
    bc_c = _conv_fwd(proj, C_B, 512, cw_bc, cb_bc, name="conv_bc")
    dtf_t = proj[:, C_DTF:C_DTF + 32].T
    dt_t, acs_t, cum_t = _dtf_fwd(dtf_t, dt_bias, a_log, f_bias)
    dt_col, acs_col, cum_col = dt_t.T, acs_t.T, cum_t.T
    cum_row3 = cum_t.reshape(16, S // ATT_T, ATT_T).transpose(1, 0, 2)
    y_ssd, hs = _ssd_fwd(xs_c, dt_col, acs_col, acs_t, bc_c)
    gate_rows = [_whole(y_ssd), _whole(xs_c), (proj, C_Z, 1024)]
    gate_pars = [_whole(p["d_skip"]), _whole(p["ssm_norm_w"])]
    y_ssm = _rw_fwd(_gate_fn, gate_rows, gate_pars, [(1024, BF16)], tm=TM, name="gate")[0]
    qn = _rw_fwd(_headnorm_fn, [(proj, C_Q, 1024)], [_whole(p["g_q"])], [(1024, BF16)], tm=TM, name="qnorm")[0]
    kn = _rw_fwd(_headnorm_fn, [(proj, C_K, 1024)], [_whole(p["g_k"])], [(1024, BF16)], tm=TM, name="knorm")[0]
    o, lse = _fox_fwd(qn, kn, proj, C_V, cum_col, cum_row3)
    mixed = jnp.concatenate([y_ssm, o.astype(BF16)], axis=1)
    x1 = _matmul(mixed, w_out, mode="nn", tm=1024, tn=512, tk=2048, add=x, name="mm_out")

    h2 = rms(x1, p["g_xattn"], "rms_xattn")
    mem_n = rms(mem, p["g_mem"], "rms_mem")
    q2 = _matmul(h2, xq_w, mode="nn", tm=1024, tn=512, tk=1024, name="mm_xq")
    kv = _matmul(mem_n, xkv_w, mode="nn", tm=256, tn=1024, tk=1024, name="mm_xkv")
    xa_rows = [(q2, X_D * h, X_D) for h in range(X_HEADS)]
    xa_pars = ([(kv, X_D * h, X_D) for h in range(X_HEADS)] + [(kv, D_MODEL + X_D * h, X_D) for h in range(X_HEADS)]
               + [_whole(p["xg_q"]), _whole(p["xg_k"])])
    o2 = _rw_fwd(_xattn_fn, xa_rows, xa_pars, [(D_MODEL, BF16)], tm=TM, name="xattn")[0]
    x2 = _matmul(o2, xo_w, mode="nn", tm=1024, tn=512, tk=1024, add=x1, name="mm_xo")

    h3 = rms(x2, p["g_mlp"], "rms_mlp")
    a = _matmul(h3, w_up, mode="nn", tm=1024, tn=1024, tk=1024, name="mm_up")
    usq = _rw_fwd(_relu2_fn, [_whole(a)], [], [(4 * D_MODEL, BF16)], tm=TM, name="relu2")[0]
    x3 = _matmul(usq, w_down, mode="nn", tm=1024, tn=512, tk=2048, add=x2, name="mm_down")
    dy, loss_part = _loss_head(x3, target, tm=TM)
    loss = lax.psum(loss_part[0, 0], ("x", "y", "c"))

    g = {}
    g["w_down"] = _matmul(usq, dy, mode="tn", out_dtype=GRAD_WIRE, tm=1024, tn=1024, tk=1024, name="mm_d_wdown")
    dusq = _matmul(dy, w_down, mode="nt", tm=1024, tn=1024, tk=1024, name="mm_d_usq")
    da = _rw_bwd(_relu2_fn, [_whole(a)], [], [_whole(dusq)], tm=TM, name="relu2_bwd", row_grads=[BF16])[0]
    g["w_up"] = _matmul(h3, da, mode="tn", out_dtype=GRAD_WIRE, tm=1024, tn=1024, tk=1024, name="mm_d_wup")
    dh3 = _matmul(da, w_up, mode="nt", tm=1024, tn=512, tk=2048, name="mm_d_h3")
    dx2, g["g_mlp"] = _rw_bwd(_rms_fn, [_whole(x2)], [_whole(p["g_mlp"])], [_whole(dh3)], tm=TM, name="rms_mlp_bwd",
                              row_grads=[F32], adds={0: _whole(dy)})

    g["xo_w"] = _matmul(o2, dx2, mode="tn", out_dtype=GRAD_WIRE, tm=1024, tn=1024, tk=1024, name="mm_d_wxo")
    do2 = _matmul(dx2, xo_w, mode="nt", tm=1024, tn=512, tk=1024, name="mm_d_o2")
    xa = _rw_bwd(_xattn_fn, xa_rows, xa_pars, [_whole(do2)], tm=TM, name="xattn_bwd", row_grads=[BF16] * X_HEADS)
    dq2 = jnp.concatenate(xa[:X_HEADS], axis=1)
    dkv = jnp.concatenate(xa[X_HEADS:3 * X_HEADS], axis=1)
    g["xg_q"], g["xg_k"] = xa[3 * X_HEADS], xa[3 * X_HEADS + 1]
    g["xq_w"] = _matmul(h2, dq2, mode="tn", out_dtype=GRAD_WIRE, tm=1024, tn=1024, tk=1024, name="mm_d_wxq")
    dh2 = _matmul(dq2, xq_w, mode="nt", tm=1024, tn=512, tk=1024, name="mm_d_h2")
    g["xkv_w"] = _matmul(mem_n, dkv, mode="tn", out_dtype=GRAD_WIRE, tm=1024, tn=1024, tk=256, name="mm_d_wxkv")
    dmem_n = _matmul(dkv, xkv_w, mode="nt", tm=256, tn=1024, tk=2048, name="mm_d_memn")
    g["g_mem"] = _rw_bwd(_rms_fn, [_whole(mem)], [_whole(p["g_mem"])], [_whole(dmem_n)], tm=TM, name="rms_mem_bwd",
                         row_grads=[None])[0]
    dx1, g["g_xattn"] = _rw_bwd(_rms_fn, [_whole(x1)], [_whole(p["g_xattn"])], [_whole(dh2)], tm=TM, name="rms_xattn_bwd",
                                row_grads=[F32], adds={0: _whole(dx2)})

    g["w_out"] = _matmul(mixed, dx1, mode="tn", out_dtype=GRAD_WIRE, tm=1024, tn=1024, tk=1024, name="mm_d_wout")
    dmixed = _matmul(dx1, w_out, mode="nt", tm=1024, tn=1024, tk=1024, name="mm_d_mixed")
    dqn, dkn, dv, dcum4 = _fox_bwd(qn, kn, proj, C_V, cum_col, cum_row3, o, lse, dmixed, 1024)
    dq, g["g_q"] = _rw_bwd(_headnorm_fn, [(proj, C_Q, 1024)], [_whole(p["g_q"])], [_whole(dqn)], tm=TM, name="qnorm_bwd",
                           row_grads=[BF16])
    dk, g["g_k"] = _rw_bwd(_headnorm_fn, [(proj, C_K, 1024)], [_whole(p["g_k"])], [_whole(dkn)], tm=TM, name="knorm_bwd",
                           row_grads=[BF16])
    dy_ssd, dxs_g, dz, g["d_skip"], g["ssm_norm_w"] = _rw_bwd(
        _gate_fn, gate_rows, gate_pars, [(dmixed, 0, 1024)], tm=TM, name="gate_bwd", row_grads=[F32, F32, BF16])
    dxs_s, ddt_col, dacs_col, dacs_row, d_b, d_c = _ssd_bwd(xs_c, dt_col, acs_col, acs_t, bc_c, hs, dy_ssd)
    dcum_t = dcum4[:, :, 0:2, :].transpose(0, 2, 1, 3).reshape(16, S)
    ddtf_t, ddtb, dalog, dfb = _dtf_bwd(dtf_t, dt_bias, a_log, f_bias, ddt_col.T, dacs_col.T, dacs_row, dcum_t)
    g["dt_bias"], g["a_log"], g["f_bias"] = ddtb, dalog, dfb
    dxs_raw, dcw_xs, dcb_xs = _conv_bwd(proj, C_XS, 1024, cw_xs, cb_xs, [dxs_s, dxs_g], name="conv_xs_bwd")
    dbc_raw, dcw_bc, dcb_bc = _conv_bwd(proj, C_B, 512, cw_bc, cb_bc, [jnp.concatenate([d_b, d_c], axis=1)],
                                        name="conv_bc_bwd")
    g["conv_w"] = jnp.concatenate([dcw_xs, dcw_bc], axis=1)
    g["conv_b"] = jnp.concatenate([dcb_xs, dcb_bc], axis=1)
    ddtf = jnp.pad(ddtf_t.T.astype(BF16), ((0, 0), (0, P_COLS - C_DTF - 32)))
    dproj = jnp.concatenate([dz, dxs_raw, dq, dk, dv, dbc_raw, ddtf], axis=1)
    dw_in_p = _matmul(h1, dproj, mode="tn", out_dtype=GRAD_WIRE, tm=1024, tn=640, tk=1024, name="mm_d_win")
    dh1 = _matmul(dproj, w_in, mode="nt", tm=1024, tn=512, tk=1920, name="mm_d_h1")
    grad_x, g["g_mix"] = _rw_bwd(_rms_fn, [_whole(x)], [_whole(p["g_mix"])], [_whole(dh1)], tm=TM, name="rms_mix_bwd",
                                 row_grads=[F32], adds={0: _whole(dx1)})
    g["w_in"] = jnp.concatenate(
        [_cols(dw_in_p, C_Z, C_Q), _cols(dw_in_p, C_B, C_DTF + 16), _cols(dw_in_p, C_Q, C_B),
         _cols(dw_in_p, C_DTF + 16, C_DTF + 32)], axis=1)

    def col_shards(a):
        r, c = a.shape
        return a.reshape(r, N_DEV, c // N_DEV).transpose(1, 0, 2)

    def row_shards(a):
        r, c = a.shape
        return a.reshape(N_DEV, r // N_DEV, c)

    send = [col_shards(g["w_in"]), row_shards(g["w_out"]), row_shards(g["xq_w"]), col_shards(g["xkv_w"]),
            row_shards(g["xo_w"]), col_shards(g["w_up"]), row_shards(g["w_down"])]
    parts = _exchange(send, scatter=True, name="exchange_grads")
    small_parts = _exchange([_pack_small(g)], scatter=False, name="gather_small_grads")[0]

    grads, delta, new_m, new_v = {}, {}, {}, {}
    for name, part in zip(BIG, parts, strict=True):
        grads[name], delta[name], new_m[name], new_v[name] = _reduce_adamw(part, p[name], m[name], v[name], tr=128,
                                                                            name="adamw_" + name)
    zeros_cw = jnp.zeros((4, 1536), F32)
    slabs = [_pack_small({**d, "conv_w": zeros_cw}) for d in (p, m, v)]
    sg, sd, sm, sv = _reduce_adamw(small_parts, *slabs, tr=SLAB_ROWS, name="adamw_small")
    for dst, slab in ((grads, sg), (delta, sd), (new_m, sm), (new_v, sv)):
        for name, flat in _unpack_small(slab).items():
            if name != "conv_w":
                dst[name] = flat.reshape(p[name].shape)
    cw_shard = p["conv_w"].shape[1]
    grads["conv_w"] = lax.dynamic_slice(_unpack_small(sg)["conv_w"].reshape(4, 1536), (0, me * cw_shard), (4, cw_shard))
    delta["conv_w"], new_m["conv_w"], new_v["conv_w"] = _adamw(p["conv_w"], grads["conv_w"], m["conv_w"], v["conv_w"],
                                                               name="adamw_conv_w")
    return loss, grad_x, grads, delta, new_m, new_v


def kernel(x, mem, g_mix, w_in, conv_w, conv_b, dt_bias, a_log, d_skip, ssm_norm_w, g_q, g_k, f_bias, w_out, g_xattn, g_mem, xq_w, xkv_w, xg_q, xg_k, xo_w, g_mlp, w_up, w_down, loss_target, m_g_mix, m_w_in, m_conv_w, m_conv_b, m_dt_bias, m_a_log, m_d_skip, m_ssm_norm_w, m_g_q, m_g_k, m_f_bias, m_w_out, m_g_xattn, m_g_mem, m_xq_w, m_xkv_w, m_xg_q, m_xg_k, m_xo_w, m_g_mlp, m_w_up, m_w_down, v_g_mix, v_w_in, v_conv_w, v_conv_b, v_dt_bias, v_a_log, v_d_skip, v_ssm_norm_w, v_g_q, v_g_k, v_f_bias, v_w_out, v_g_xattn, v_g_mem, v_xq_w, v_xkv_w, v_xg_q, v_xg_k, v_xo_w, v_g_mlp, v_w_up, v_w_down):
    args = locals()
    drop = lambda t: t[0] if t.ndim == 3 else t
    p = {n: drop(args[n]) for n in WEIGHTS}
    m = {n: drop(args["m_" + n]) for n in WEIGHTS}
    v = {n: drop(args["v_" + n]) for n in WEIGHTS}
    loss, grad_x, grads, delta, new_m, new_v = _step(p, m, v, x[0], mem[0], loss_target[0])
    outs = [loss, grad_x[None]]
    for d in (grads, delta, new_m, new_v):
        outs += [d[n].reshape(args[n].shape) for n in WEIGHTS]
    return tuple(outs)
```

```python
import functools
import math

import jax
import jax.numpy as jnp
from jax import lax
from jax.experimental import pallas as pl
from jax.experimental.pallas import tpu as pltpu

F32, BF16 = jnp.float32, jnp.bfloat16
SDS = jax.ShapeDtypeStruct
HI = lax.Precision.HIGHEST
MESH = pl.DeviceIdType.MESH

N_DEV = 8
EPS = 1e-5
D_MODEL = 1024
SSM_HEADS, SSM_P, SSM_N, SSM_GROUPS, CHUNK = 16, 64, 128, 2, 128
ATT_HEADS, ATT_D = 16, 64
X_HEADS, X_D = 4, 256
LANES = 128
VMEM_LIMIT = 48 * 1024 * 1024
NEG = -1e30

GRAD_WIRE = BF16
ADAM_LR, ADAM_B1, ADAM_B2, ADAM_EPS, ADAM_WD, ADAM_STEP = 0.001, 0.9, 0.999, 1e-08, 0.01, 10

C_Z, C_XS, C_Q, C_K, C_V, C_B, C_C, C_DTF, P_COLS = 0, 1024, 2048, 3072, 4096, 5120, 5376, 5632, 5760

_NN = (((1,), (0,)), ((), ()))
_NT = (((1,), (1,)), ((), ()))
_TN = (((0,), (0,)), ((), ()))


def _cparams(**kw):
    return pltpu.CompilerParams(vmem_limit_bytes=VMEM_LIMIT, **kw)


def _bdot(a, b, dn):
    return lax.dot_general(a.astype(BF16), b.astype(BF16), dn, preferred_element_type=F32)


@jax.custom_vjp
def mm_nn(a, b):
    return _bdot(a, b, _NN)


mm_nn.defvjp(lambda a, b: (mm_nn(a, b), (a, b)), lambda r, g: (_bdot(g, r[1], _NT), _bdot(r[0], g, _TN)))


@jax.custom_vjp
def mm_nt(a, b):
    return _bdot(a, b, _NT)


mm_nt.defvjp(lambda a, b: (mm_nt(a, b), (a, b)), lambda r, g: (_bdot(g, r[1], _NN), _bdot(g, r[0], _TN)))


@jax.custom_vjp
def mm_tn(a, b):
    return _bdot(a, b, _TN)


mm_tn.defvjp(lambda a, b: (mm_tn(a, b), (a, b)), lambda r, g: (_bdot(r[1], g, _NT), _bdot(r[0], g, _NN)))


def _cdot(x, c):
    return jnp.dot(x, c, precision=HI, preferred_element_type=F32)


def _iota(shape, dim):
    return lax.broadcasted_iota(jnp.int32, shape, dim)


def _matmul(a, b, *, mode, tm, tn, tk, name, out_dtype=F32, add=None):
    if mode == "tn":
        K, M = a.shape
    else:
        M, K = a.shape
    N = b.shape[0] if mode == "nt" else b.shape[1]
    tm, tn, tk = min(tm, M), min(tn, N), min(tk, K)
    assert M % tm == 0 and N % tn == 0 and K % tk == 0, (name, M, N, K, tm, tn, tk)
    nk = K // tk
    dn = {"nn": _NN, "nt": _NT, "tn": _TN}[mode]

    def body(*refs):
        if add is None:
            a_ref, b_ref, o_ref, acc_ref = refs
        else:
            a_ref, b_ref, add_ref, o_ref, acc_ref = refs
        k = pl.program_id(2)

        @pl.when(k == 0)
        def _():
            acc_ref[...] = jnp.zeros_like(acc_ref)

        acc_ref[...] += _bdot(a_ref[...], b_ref[...], dn)

        @pl.when(k == nk - 1)
        def _():
            o = acc_ref[...]
            if add is not None:
                o = o + add_ref[...]
            o_ref[...] = o.astype(o_ref.dtype)

    a_spec = pl.BlockSpec((tk, tm), lambda i, j, k: (k, i)) if mode == "tn" else pl.BlockSpec((tm, tk), lambda i, j, k: (i, k))
    b_spec = pl.BlockSpec((tn, tk), lambda i, j, k: (j, k)) if mode == "nt" else pl.BlockSpec((tk, tn), lambda i, j, k: (k, j))
    o_spec = pl.BlockSpec((tm, tn), lambda i, j, k: (i, j))
    in_specs, args = [a_spec, b_spec], [a, b]
    if add is not None:
        in_specs.append(o_spec)
        args.append(add)
    return pl.pallas_call(
        body, name=name, grid=(M // tm, N // tn, nk), in_specs=in_specs, out_specs=o_spec,
        out_shape=SDS((M, N), out_dtype), scratch_shapes=[pltpu.VMEM((tm, tn), F32)],
        compiler_params=_cparams(dimension_semantics=("parallel", "parallel", "arbitrary")),
    )(*args)


def _row_spec(tm, spec):
    _, c0, w = spec
    assert c0 % w == 0
    return pl.BlockSpec((tm, w), functools.partial(lambda i, cb: (i, cb), cb=c0 // w))


def _par_spec(spec):
    arr, c0, w = spec
    assert c0 % w == 0
    return pl.BlockSpec((arr.shape[0], w), functools.partial(lambda i, cb: (0, cb), cb=c0 // w))


def _whole(arr):
    return (arr, 0, arr.shape[1])


def _rw_fwd(fn, rows, params, outs, *, tm, name):
    M = rows[0][0].shape[0]
    nr, npar = len(rows), len(params)

    def body(*refs):
        rv = [r[...].astype(F32) for r in refs[:nr]]
        pv = [p[...].astype(F32) for p in refs[nr:nr + npar]]
        res = fn(*rv, *pv)
        for o_ref, v in zip(refs[nr + npar:], res, strict=True):
            o_ref[...] = v.astype(o_ref.dtype)

    return pl.pallas_call(
        body, name=name, grid=(M // tm,),
        in_specs=[_row_spec(tm, r) for r in rows] + [_par_spec(p) for p in params],
        out_specs=[pl.BlockSpec((tm, w), lambda i: (i, 0)) for w, _ in outs],
        out_shape=[SDS((M, w), dt) for w, dt in outs],
        compiler_params=_cparams(dimension_semantics=("parallel",)),
    )(*[r[0] for r in rows], *[p[0] for p in params])


def _rw_bwd(fn, rows, params, cts, *, tm, name, row_grads, adds=None):
    M = rows[0][0].shape[0]
    adds = adds or {}
    nr, npar, nc = len(rows), len(params), len(cts)
    add_keys = sorted(adds)
    want = [k for k in range(nr) if row_grads[k] is not None]

    def body(*refs):
        pos = 0
        r_refs = refs[pos:pos + nr]; pos += nr
        p_refs = refs[pos:pos + npar]; pos += npar
        c_refs = refs[pos:pos + nc]; pos += nc
        a_refs = dict(zip(add_keys, refs[pos:pos + len(add_keys)])); pos += len(add_keys)
        dr_refs = dict(zip(want, refs[pos:pos + len(want)])); pos += len(want)
        dp_refs = refs[pos:pos + npar]
        rv = [r[...].astype(F32) for r in r_refs]
        pv = [p[...].astype(F32) for p in p_refs]
        _, vjp = jax.vjp(fn, *rv, *pv)
        g = vjp(tuple(c[...].astype(F32) for c in c_refs))
        for k in want:
            v = g[k]
            if k in a_refs:
                v = v + a_refs[k][...].astype(F32)
            dr_refs[k][...] = v.astype(dr_refs[k].dtype)
        first = pl.program_id(0) == 0
        for j in range(npar):
            @pl.when(first)
            def _(j=j):
                dp_refs[j][...] = jnp.zeros_like(dp_refs[j])
            dp_refs[j][...] += g[nr + j]

    res = pl.pallas_call(
        body, name=name, grid=(M // tm,),
        in_specs=([_row_spec(tm, r) for r in rows] + [_par_spec(p) for p in params] + [_row_spec(tm, c) for c in cts]
                  + [_row_spec(tm, adds[k]) for k in add_keys]),
        out_specs=([pl.BlockSpec((tm, rows[k][2]), lambda i: (i, 0)) for k in want]
                   + [pl.BlockSpec((p[0].shape[0], p[2]), lambda i: (0, 0)) for p in params]),
        out_shape=([SDS((M, rows[k][2]), row_grads[k]) for k in want] + [SDS((p[0].shape[0], p[2]), F32) for p in params]),
        compiler_params=_cparams(dimension_semantics=("arbitrary",)),
    )(*[r[0] for r in rows], *[p[0] for p in params], *[c[0] for c in cts], *[adds[k][0] for k in add_keys])
    return res


def _rms_fn(x, g):
    r = lax.rsqrt(jnp.mean(x * x, axis=-1, keepdims=True) + EPS)
    return (x * r * g,)


def _seg_mats(width, seg):
    n = width // seg
    p = (_iota((width, n), 0) // seg == _iota((width, n), 1)).astype(F32)
    e = (_iota((n, width), 1) // seg == _iota((n, width), 0)).astype(F32)
    return p, e


def _headnorm_fn(q, g):
    p, e = _seg_mats(ATT_HEADS * ATT_D, ATT_D)
    t = (_iota((ATT_D, ATT_HEADS * ATT_D), 1) % ATT_D == _iota((ATT_D, ATT_HEADS * ATT_D), 0)).astype(F32)
    ms = _cdot(q * q, p) * (1.0 / ATT_D)
    r = _cdot(lax.rsqrt(ms + EPS), e)
    return (q * r * _cdot(g, t),)


def _gate_fn(y, xs, z, dskip, w):
    width = SSM_HEADS * SSM_P
    _, e = _seg_mats(width, SSM_P)
    y = (y + _cdot(dskip, e) * xs) * (z * jax.nn.sigmoid(z))
    g0 = _iota((1, width), 1) < width // SSM_GROUPS
    y2 = y * y
    gw = width // SSM_GROUPS
    ms0 = jnp.sum(jnp.where(g0, y2, 0.0), axis=-1, keepdims=True) * (1.0 / gw)
    ms1 = jnp.sum(jnp.where(g0, 0.0, y2), axis=-1, keepdims=True) * (1.0 / gw)
    r = jnp.where(g0, lax.rsqrt(ms0 + EPS), lax.rsqrt(ms1 + EPS))
    return (y * r * w,)


def _relu2_fn(a):
    u = jax.nn.relu(a)
    return (u * u,)


def _xattn_fn(q0, q1, q2, q3, k0, k1, k2, k3, v0, v1, v2, v3, gq, gk):
    def norm(u, g):
        return u * lax.rsqrt(jnp.mean(u * u, axis=-1, keepdims=True) + EPS) * g
    outs = []
    for q, k, v in ((q0, k0, v0), (q1, k1, v1), (q2, k2, v2), (q3, k3, v3)):
        s = mm_nt(norm(q, gq), norm(k, gk)) * (X_D ** -0.5)
        p = jnp.exp(s - lax.stop_gradient(jnp.max(s, axis=-1, keepdims=True)))
        p = p / jnp.sum(p, axis=-1, keepdims=True)
        outs.append(mm_nn(p, v))
    return (jnp.concatenate(outs, axis=-1),)


CONV_TC = 256


def _shift_down(u, k):
    if k == 0:
        return u
    return jnp.where(_iota(u.shape, 0) >= k, pltpu.roll(u, k, axis=0), 0.0)


def _shift_up(u, k):
    if k == 0:
        return u
    n = u.shape[0]
    return jnp.where(_iota(u.shape, 0) < n - k, pltpu.roll(u, n - k, axis=0), 0.0)


def _conv_pre(u, w_ref, b):
    pre = b + w_ref[3:4, :] * u
    for k in (1, 2, 3):
        pre = pre + w_ref[3 - k:4 - k, :] * _shift_down(u, k)
    return pre


def _conv_fwd(src, c0, width, w, b, *, name):
    S = src.shape[0]
    cb0 = c0 // CONV_TC

    def body(u_ref, w_ref, b_ref, o_ref):
        pre = _conv_pre(u_ref[...], w_ref, b_ref[...])
        o_ref[...] = pre * jax.nn.sigmoid(pre)

    return pl.pallas_call(
        body, name=name, grid=(width // CONV_TC,),
        in_specs=[pl.BlockSpec((S, CONV_TC), lambda j: (0, cb0 + j)), pl.BlockSpec((4, CONV_TC), lambda j: (0, j)),
                  pl.BlockSpec((1, CONV_TC), lambda j: (0, j))],
        out_specs=pl.BlockSpec((S, CONV_TC), lambda j: (0, j)), out_shape=SDS((S, width), F32),
        compiler_params=_cparams(dimension_semantics=("parallel",)),
    )(src, w, b)


def _conv_bwd(src, c0, width, w, b, douts, *, name):
    S = src.shape[0]
    cb0 = c0 // CONV_TC
    nd = len(douts)

    def body(*refs):
        u_ref, w_ref, b_ref = refs[:3]
        d_refs = refs[3:3 + nd]
        du_ref, dw_ref, db_ref = refs[3 + nd:]
        u = u_ref[...]
        pre = _conv_pre(u, w_ref, b_ref[...])
        sg = jax.nn.sigmoid(pre)
        dout = d_refs[0][...]
        for r in d_refs[1:]:
            dout = dout + r[...]
        dpre = dout * (sg * (1.0 + pre * (1.0 - sg)))
        du = w_ref[3:4, :] * dpre
        dw_ref[3:4, :] = jnp.sum(dpre * u, axis=0, keepdims=True)
        for k in (1, 2, 3):
            du = du + w_ref[3 - k:4 - k, :] * _shift_up(dpre, k)
            dw_ref[3 - k:4 - k, :] = jnp.sum(dpre * _shift_down(u, k), axis=0, keepdims=True)
        du_ref[...] = du.astype(du_ref.dtype)
        db_ref[...] = jnp.sum(dpre, axis=0, keepdims=True)

    return pl.pallas_call(
        body, name=name, grid=(width // CONV_TC,),
        in_specs=[pl.BlockSpec((S, CONV_TC), lambda j: (0, cb0 + j)), pl.BlockSpec((4, CONV_TC), lambda j: (0, j)),
                  pl.BlockSpec((1, CONV_TC), lambda j: (0, j))] + [pl.BlockSpec((S, CONV_TC), lambda j: (0, j))] * nd,
        out_specs=[pl.BlockSpec((S, CONV_TC), lambda j: (0, j)), pl.BlockSpec((4, CONV_TC), lambda j: (0, j)),
                   pl.BlockSpec((1, CONV_TC), lambda j: (0, j))],
        out_shape=[SDS((S, width), BF16), SDS((4, width), F32), SDS((1, width), F32)],
        compiler_params=_cparams(dimension_semantics=("parallel",)),
    )(src, w, b, *douts)


def _softplus(x):
    return jnp.maximum(x, 0.0) + jnp.log(1.0 + jnp.exp(-jnp.abs(x)))


def _prefix_sum(x, seg):
    n = x.shape[1]
    pos = _iota(x.shape, 1) % seg
    k = 1
    while k < seg:
        x = x + jnp.where(pos >= k, pltpu.roll(x, k, axis=1), 0.0)
        k *= 2
    return x


def _suffix_sum(x, seg):
    n = x.shape[1]
    pos = _iota(x.shape, 1) % seg
    k = 1
    while k < seg:
        x = x + jnp.where(pos + k < seg, pltpu.roll(x, n - k, axis=1), 0.0)
        k *= 2
    return x


def _dtf_fwd(dtf_t, dt_bias, a_log, f_bias):
    S = dtf_t.shape[1]

    def body(x_ref, db_ref, al_ref, fb_ref, dt_ref, acs_ref, cum_ref):
        dt = _softplus(x_ref[0:16, :] + db_ref[...])
        dt_ref[...] = dt
        acs_ref[...] = _prefix_sum(dt * (-jnp.exp(al_ref[...])), CHUNK)
        cum_ref[...] = _prefix_sum(-_softplus(-(x_ref[16:32, :] + fb_ref[...])), S)

    return pl.pallas_call(body, name="dtf_fwd", out_shape=[SDS((16, S), F32)] * 3, compiler_params=_cparams())(
        dtf_t, dt_bias, a_log, f_bias)


def _dtf_bwd(dtf_t, dt_bias, a_log, f_bias, d_dt, d_acs_a, d_acs_b, d_cum):
    S = dtf_t.shape[1]

    def body(x_ref, db_ref, al_ref, fb_ref, ddt_ref, da1_ref, da2_ref, dc_ref, dx_ref, ddb_ref, dal_ref, dfb_ref):
        xd = x_ref[0:16, :] + db_ref[...]
        dt = _softplus(xd)
        a = -jnp.exp(al_ref[...])
        d_da = _suffix_sum(da1_ref[...] + da2_ref[...], CHUNK)
        d_dt = ddt_ref[...] + d_da * a
        dal_ref[...] = jnp.sum(d_da * dt, axis=1, keepdims=True) * a
        d_xd = d_dt * jax.nn.sigmoid(xd)
        ddb_ref[...] = jnp.sum(d_xd, axis=1, keepdims=True)
        xf = x_ref[16:32, :] + fb_ref[...]
        d_xf = _suffix_sum(dc_ref[...], S) * jax.nn.sigmoid(-xf)
        dfb_ref[...] = jnp.sum(d_xf, axis=1, keepdims=True)
        dx_ref[0:16, :] = d_xd
        dx_ref[16:32, :] = d_xf

    return pl.pallas_call(body, name="dtf_bwd", out_shape=[SDS((32, S), F32)] + [SDS((16, 1), F32)] * 3,
                          compiler_params=_cparams())(dtf_t, dt_bias, a_log, f_bias, d_dt, d_acs_a, d_acs_b, d_cum)


def _ssd_chunk(xs, dtc, acol, arow, bm, cm, h, *, hp):
    L = CHUNK
    first = _iota((1, LANES), 1) < SSM_P
    i16, s16 = _iota((L, 16), 1), _iota((16, L), 0)
    ha, hb = 2 * hp, 2 * hp + 1

    def selc(blk, hh):
        return jnp.sum(jnp.where(i16 == hh, blk, 0.0), axis=1, keepdims=True)

    def selr(blk, hh):
        return jnp.sum(jnp.where(s16 == hh, blk, 0.0), axis=0, keepdims=True)

    x = xs * jnp.where(first, selc(dtc, ha), selc(dtc, hb))
    ca, cb, ra, rb = selc(acol, ha), selc(acol, hb), selr(arow, ha), selr(arow, hb)
    tri = _iota((L, L), 0) >= _iota((L, L), 1)
    cbm = mm_nt(cm, bm)
    la = jnp.exp(jnp.where(tri, ca - ra, NEG))
    lb = jnp.exp(jnp.where(tri, cb - rb, NEG))
    y = jnp.where(first, mm_nn(cbm * la, x), mm_nn(cbm * lb, x))
    y = y + jnp.where(first, jnp.exp(ca), jnp.exp(cb)) * mm_nn(cm, h)
    last = _iota((1, L), 1) == L - 1
    ala = jnp.sum(jnp.where(last, ra, 0.0), axis=1, keepdims=True)
    alb = jnp.sum(jnp.where(last, rb, 0.0), axis=1, keepdims=True)
    dec = jnp.where(first, jnp.exp(ala - ca), jnp.exp(alb - cb))
    hn = jnp.where(first, jnp.exp(ala), jnp.exp(alb)) * h + mm_tn(bm, x * dec)
    return y, hn


def _ssd_specs(nc):
    L = CHUNK
    hpg = SSM_HEADS // 2 // SSM_GROUPS

    def mk(rev):
        cidx = (lambda c: nc - 1 - c) if rev else (lambda c: c)
        return dict(
            xs=pl.BlockSpec((L, LANES), lambda c, hp: (cidx(c), hp)),
            col=pl.BlockSpec((L, 16), lambda c, hp: (cidx(c), 0)),
            row=pl.BlockSpec((16, L), lambda c, hp: (0, cidx(c))),
            b=pl.BlockSpec((L, SSM_N), lambda c, hp: (cidx(c), hp // hpg)),
            c=pl.BlockSpec((L, SSM_N), lambda c, hp: (cidx(c), SSM_GROUPS + hp // hpg)),
            grp=pl.BlockSpec((L, SSM_N), lambda c, hp: (cidx(c), hp // hpg)),
            st=pl.BlockSpec((1, 1, SSM_N, LANES), lambda c, hp: (cidx(c), hp, 0, 0)),
        )
    return mk


def _ssd_fwd(xs, dt_col, acs_col, acs_row, bc):
    S = xs.shape[0]
    nc, nhp = S // CHUNK, SSM_HEADS // 2
    sp = _ssd_specs(nc)(False)

    def body(xs_ref, dt_ref, ac_ref, ar_ref, b_ref, c_ref, y_ref, hs_ref, h_scr):
        c, hp = pl.program_id(0), pl.program_id(1)

        @pl.when(c == 0)
        def _():
            h_scr[hp] = jnp.zeros((SSM_N, LANES), F32)

        h = h_scr[hp]
        hs_ref[0, 0] = h
        y, hn = _ssd_chunk(xs_ref[...], dt_ref[...], ac_ref[...], ar_ref[...], b_ref[...], c_ref[...], h, hp=hp)
        y_ref[...] = y
        h_scr[hp] = hn

    return pl.pallas_call(
        body, name="ssd_fwd", grid=(nc, nhp),
        in_specs=[sp["xs"], sp["col"], sp["col"], sp["row"], sp["b"], sp["c"]],
        out_specs=[sp["xs"], sp["st"]],
        out_shape=[SDS((S, SSM_HEADS * SSM_P), F32), SDS((nc, nhp, SSM_N, LANES), F32)],
        scratch_shapes=[pltpu.VMEM((nhp, SSM_N, LANES), F32)],
        compiler_params=_cparams(dimension_semantics=("arbitrary", "arbitrary")),
    )(xs, dt_col, acs_col, acs_row, bc, bc)


def _ssd_bwd(xs, dt_col, acs_col, acs_row, bc, hs, dy):
    S = xs.shape[0]
    nc, nhp = S // CHUNK, SSM_HEADS // 2
    hpg = nhp // SSM_GROUPS
    sp = _ssd_specs(nc)(True)

    def body(xs_ref, dt_ref, ac_ref, ar_ref, b_ref, c_ref, hs_ref, dy_ref,
             dxs_ref, ddt_ref, dac_ref, dar_ref, db_ref, dc_ref, dh_scr):
        c, hp = pl.program_id(0), pl.program_id(1)

        @pl.when(c == 0)
        def _():
            dh_scr[hp] = jnp.zeros((SSM_N, LANES), F32)

        _, vjp = jax.vjp(functools.partial(_ssd_chunk, hp=hp), xs_ref[...], dt_ref[...], ac_ref[...], ar_ref[...],
                         b_ref[...], c_ref[...], hs_ref[0, 0])
        dxs, ddt, dac, dar, db, dc, dh = vjp((dy_ref[...], dh_scr[hp]))
        dxs_ref[...] = dxs
        dh_scr[hp] = dh

        @pl.when(hp == 0)
        def _():
            ddt_ref[...] = jnp.zeros_like(ddt_ref)
            dac_ref[...] = jnp.zeros_like(dac_ref)
            dar_ref[...] = jnp.zeros_like(dar_ref)

        ddt_ref[...] += ddt
        dac_ref[...] += dac
        dar_ref[...] += dar

        @pl.when(hp % hpg == 0)
        def _():
            db_ref[...] = jnp.zeros_like(db_ref)
            dc_ref[...] = jnp.zeros_like(dc_ref)

        db_ref[...] += db
        dc_ref[...] += dc

    return pl.pallas_call(
        body, name="ssd_bwd", grid=(nc, nhp),
        in_specs=[sp["xs"], sp["col"], sp["col"], sp["row"], sp["b"], sp["c"], sp["st"], sp["xs"]],
        out_specs=[sp["xs"], sp["col"], sp["col"], sp["row"], sp["grp"], sp["grp"]],
        out_shape=[SDS((S, SSM_HEADS * SSM_P), F32), SDS((S, 16), F32), SDS((S, 16), F32), SDS((16, S), F32),
                   SDS((S, SSM_GROUPS * SSM_N), F32), SDS((S, SSM_GROUPS * SSM_N), F32)],
        scratch_shapes=[pltpu.VMEM((nhp, SSM_N, LANES), F32)],
        compiler_params=_cparams(dimension_semantics=("arbitrary", "arbitrary")),
    )(xs, dt_col, acs_col, acs_row, bc, bc, hs, dy)


ATT_T = 512


def _pick_col(blk, h):
    return jnp.sum(jnp.where(_iota(blk.shape, 1) == h, blk, 0.0), axis=1, keepdims=True)


def _pick_row(blk, h):
    return jnp.sum(jnp.where(_iota(blk.shape, 0) == h, blk, 0.0), axis=0, keepdims=True)


def _fox_fwd(qn, kn, vsrc, v_c0, cum_col, cum_row3):
    S = qn.shape[0]
    T = ATT_T
    nq, nhp = S // T, ATT_HEADS // 2
    vb0 = v_c0 // LANES
    scale = ATT_D ** -0.5

    def body(q_ref, k_ref, v_ref, cc_ref, cr_ref, o_ref, l_ref):
        hp, i = pl.program_id(0), pl.program_id(1)
        first = _iota((1, LANES), 1) < ATT_D
        q = q_ref[...] * scale
        zero = jnp.zeros_like(q)
        qs = (jnp.where(first, q, zero), jnp.where(first, zero, q))
        cc = cc_ref[...]
        cq = (_pick_col(cc, 2 * hp), _pick_col(cc, 2 * hp + 1))
        tri = _iota((T, T), 0) >= _iota((T, T), 1)

        def tile(j, carry, diagonal):
            off = pl.multiple_of(j * T, T)
            k = k_ref[pl.ds(off, T), :]
            v = v_ref[pl.ds(off, T), :].astype(BF16)
            cr = cr_ref[j]
            out = []
            for hh in range(2):
                m, l, acc = carry[3 * hh:3 * hh + 3]
                s = _bdot(qs[hh], k, _NT) + (cq[hh] - _pick_row(cr, 2 * hp + hh))
                if diagonal:
                    s = jnp.where(tri, s, NEG)
                m_new = jnp.maximum(m, jnp.max(s, axis=1, keepdims=True))
                alpha = jnp.exp(m - m_new)
                p = jnp.exp(s - m_new)
                out += [m_new, alpha * l + jnp.sum(p, axis=1, keepdims=True), alpha * acc + _bdot(p, v, _NN)]
            return tuple(out)

        init = (jnp.full((T, 1), NEG, F32), jnp.zeros((T, 1), F32), jnp.zeros((T, LANES), F32)) * 2
        carry = lax.fori_loop(0, i, lambda j, c: tile(j, c, False), init)
        ma, la, acca, mb, lb, accb = tile(i, carry, True)
        o_ref[...] = jnp.where(first, acca / la, accb / lb)
        l_ref[...] = jnp.where(first, ma + jnp.log(la), mb + jnp.log(lb))

    return pl.pallas_call(
        body, name="fox_fwd", grid=(nhp, nq),
        in_specs=[pl.BlockSpec((T, LANES), lambda hp, i: (i, hp)), pl.BlockSpec((S, LANES), lambda hp, i: (0, hp)),
                  pl.BlockSpec((S, LANES), lambda hp, i: (0, vb0 + hp)), pl.BlockSpec((T, 16), lambda hp, i: (i, 0)),
                  pl.BlockSpec((nq, 16, T), lambda hp, i: (0, 0, 0))],
        out_specs=[pl.BlockSpec((T, LANES), lambda hp, i: (i, hp))] * 2,
        out_shape=[SDS((S, ATT_HEADS * ATT_D), F32)] * 2,
        compiler_params=_cparams(dimension_semantics=("parallel", "arbitrary")),
    )(qn, kn, vsrc, cum_col, cum_row3)


def _fox_bwd(qn, kn, vsrc, v_c0, cum_col, cum_row3, lse, dsrc, d_c0):
    S = qn.shape[0]
    T = ATT_T
    nq, nhp = S // T, ATT_HEADS // 2
    vb0, db0 = v_c0 // LANES, d_c0 // LANES
    scale = ATT_D ** -0.5

    def body(q_ref, k_ref, v_ref, cc_ref, cr_ref, l_ref, do_ref, dq_ref, dk_ref, dv_ref, dc_ref, p_scr, dp_scr):
        hp, i = pl.program_id(0), pl.program_id(1)
        first = _iota((1, LANES), 1) < ATT_D
        tri = _iota((T, T), 0) >= _iota((T, T), 1)

        @pl.when(i == 0)
        def _():
            dk_ref[...] = jnp.zeros_like(dk_ref)
            dv_ref[...] = jnp.zeros_like(dv_ref)
            dc_ref[...] = jnp.zeros_like(dc_ref)

        q = q_ref[...] * scale
        zq = jnp.zeros_like(q)
        dob = do_ref[...].astype(BF16)
        zd = jnp.zeros_like(dob)
        lse_blk, cc = l_ref[...], cc_ref[...]
        dq = jnp.zeros((T, LANES), F32)
        for hh in range(2):
            sel = first if hh == 0 else jnp.logical_not(first)
            qh, doh = jnp.where(sel, q, zq), jnp.where(sel, dob, zd)
            bias_q = _pick_col(cc, 2 * hp + hh) - jnp.max(jnp.where(sel, lse_blk, NEG), axis=1, keepdims=True)

            def probs(j, delta, diagonal):
                off = pl.multiple_of(j * T, T)
                s = _bdot(qh, k_ref[pl.ds(off, T), :], _NT) + (bias_q - _pick_row(cr_ref[j], 2 * hp + hh))
                if diagonal:
                    s = jnp.where(tri, s, NEG)
                p = jnp.exp(s)
                dp = _bdot(doh, v_ref[pl.ds(off, T), :], _NT)
                p_scr[j] = p
                dp_scr[j] = dp
                return delta + jnp.sum(p * dp, axis=1, keepdims=True)

            delta = lax.fori_loop(0, i, lambda j, d: probs(j, d, False), jnp.zeros((T, 1), F32))
            delta = probs(i, delta, True)

            def grads(j, dq):
                off = pl.multiple_of(j * T, T)
                p = p_scr[j]
                ds = p * (dp_scr[j] - delta)
                dv_ref[pl.ds(off, T), :] += _bdot(p, doh, _TN)
                dk_ref[pl.ds(off, T), :] += _bdot(ds, qh, _TN)
                dc_ref[0, j, hh:hh + 1, :] -= jnp.sum(ds, axis=0, keepdims=True)
                zk = jnp.zeros((T, LANES), BF16)
                return dq + _bdot(ds, jnp.where(sel, k_ref[pl.ds(off, T), :], zk), _NN)

            dq = lax.fori_loop(0, i + 1, grads, dq)
        dq_ref[...] = dq * scale

    return pl.pallas_call(
        body, name="fox_bwd", grid=(nhp, nq),
        in_specs=[pl.BlockSpec((T, LANES), lambda hp, i: (i, hp)), pl.BlockSpec((S, LANES), lambda hp, i: (0, hp)),
                  pl.BlockSpec((S, LANES), lambda hp, i: (0, vb0 + hp)), pl.BlockSpec((T, 16), lambda hp, i: (i, 0)),
                  pl.BlockSpec((nq, 16, T), lambda hp, i: (0, 0, 0)), pl.BlockSpec((T, LANES), lambda hp, i: (i, hp)),
                  pl.BlockSpec((T, LANES), lambda hp, i: (i, db0 + hp))],
        out_specs=[pl.BlockSpec((T, LANES), lambda hp, i: (i, hp)), pl.BlockSpec((S, LANES), lambda hp, i: (0, hp)),
                   pl.BlockSpec((S, LANES), lambda hp, i: (0, hp)), pl.BlockSpec((1, nq, 8, T), lambda hp, i: (hp, 0, 0, 0))],
        out_shape=[SDS((S, ATT_HEADS * ATT_D), F32), SDS((S, ATT_HEADS * ATT_D), F32), SDS((S, ATT_HEADS * ATT_D), F32),
                   SDS((nhp, nq, 8, T), F32)],
        scratch_shapes=[pltpu.VMEM((nq, T, T), F32), pltpu.VMEM((nq, T, T), F32)],
        compiler_params=_cparams(dimension_semantics=("parallel", "arbitrary")),
    )(qn, kn, vsrc, cum_col, cum_row3, lse, dsrc)


def _loss_head(y, target, *, tm):
    M, W = y.shape

    def body(y_ref, t_ref, dy_ref, loss_ref):
        @pl.when(pl.program_id(0) == 0)
        def _():
            loss_ref[...] = jnp.zeros_like(loss_ref)

        e = y_ref[...] - t_ref[...]
        dy_ref[...] = e * (1.0 / W)
        loss_ref[...] += jnp.sum(jnp.sum(e * e, axis=1, keepdims=True), axis=0, keepdims=True) * (0.5 / W)

    return pl.pallas_call(
        body, name="loss_head", grid=(M // tm,),
        in_specs=[pl.BlockSpec((tm, W), lambda i: (i, 0))] * 2,
        out_specs=[pl.BlockSpec((tm, W), lambda i: (i, 0)), pl.BlockSpec((1, 1), lambda i: (0, 0))],
        out_shape=[SDS((M, W), F32), SDS((1, 1), F32)],
        compiler_params=_cparams(dimension_semantics=("arbitrary",)),
    )(y, target)


def _adamw_math(w, g, m, v):
    m = ADAM_B1 * m + (1.0 - ADAM_B1) * g
    v = ADAM_B2 * v + (1.0 - ADAM_B2) * jnp.square(g)
    m_hat = m / (1.0 - ADAM_B1 ** ADAM_STEP)
    v_hat = v / (1.0 - ADAM_B2 ** ADAM_STEP)
    delta = -ADAM_LR * (m_hat / (jnp.sqrt(v_hat) + ADAM_EPS) + ADAM_WD * w)
    return delta, m, v


def _reduce_adamw(parts, w, m, v, *, tr, name):
    R, C = w.shape
    tr = min(tr, R)

    def body(p_ref, w_ref, m_ref, v_ref, g_ref, d_ref, nm_ref, nv_ref):
        g = p_ref[0].astype(F32)
        for s in range(1, N_DEV):
            g = g + p_ref[s].astype(F32)
        g_ref[...] = g
        d_ref[...], nm_ref[...], nv_ref[...] = _adamw_math(w_ref[...], g, m_ref[...], v_ref[...])

    blk = pl.BlockSpec((tr, C), lambda i: (i, 0))
    return pl.pallas_call(
        body, name=name, grid=(R // tr,),
        in_specs=[pl.BlockSpec((N_DEV, tr, C), lambda i: (0, i, 0)), blk, blk, blk], out_specs=[blk] * 4,
        out_shape=[SDS((R, C), F32)] * 4, compiler_params=_cparams(dimension_semantics=("parallel",)),
    )(parts, w, m, v)


def _adamw(w, g, m, v, *, name):
    def body(w_ref, g_ref, m_ref, v_ref, d_ref, nm_ref, nv_ref):
        d_ref[...], nm_ref[...], nv_ref[...] = _adamw_math(w_ref[...], g_ref[...], m_ref[...], v_ref[...])

    return pl.pallas_call(body, name=name, out_shape=[SDS(w.shape, F32)] * 3, compiler_params=_cparams())(w, g, m, v)


def _exchange(arrays, *, scatter, name):
    n = len(arrays)
    npeer = N_DEV - 1

    def body(*refs):
        ins, outs = refs[:n], refs[n:2 * n]
        send_sems, recv_sems, local_sems = refs[2 * n:]
        x, y, c = lax.axis_index("x"), lax.axis_index("y"), lax.axis_index("c")
        me = 4 * x + 2 * y + c
        peers = []
        for k in range(1, N_DEV):
            px, py, pc = x ^ ((k >> 2) & 1), y ^ ((k >> 1) & 1), c ^ (k & 1)
            peers.append(((px, py, pc), 4 * px + 2 * py + pc))
        started = []
        for a in range(n):
            src_me = ins[a].at[me] if scatter else ins[a]
            local = pltpu.make_async_copy(src_me, outs[a].at[me], local_sems.at[a])
            local.start()
            started.append(local)
        remote = []
        for a in range(n):
            for k, (dev, idx) in enumerate(peers):
                cp = pltpu.make_async_remote_copy(
                    src_ref=ins[a].at[idx] if scatter else ins[a], dst_ref=outs[a].at[me],
                    send_sem=send_sems.at[a * npeer + k], recv_sem=recv_sems.at[a * npeer + k],
                    device_id=dev, device_id_type=MESH)
                cp.start()
                remote.append((cp, a, k, idx))
        for cp, a, k, idx in remote:
            cp.wait_send()
        for cp, a, k, idx in remote:
            pltpu.make_async_remote_copy(
                src_ref=ins[a].at[idx] if scatter else ins[a], dst_ref=outs[a].at[idx],
                send_sem=send_sems.at[a * npeer + k], recv_sem=recv_sems.at[a * npeer + k],
                device_id=peers[k][0], device_id_type=MESH).wait_recv()
        for local in started:
            local.wait()

    def out_sds(arr):
        return SDS(arr.shape if scatter else (N_DEV,) + arr.shape, arr.dtype)

    any_spec = pl.BlockSpec(memory_space=pl.ANY)
    return pl.pallas_call(
        body, name=name, in_specs=[any_spec] * n, out_specs=[any_spec] * n, out_shape=[out_sds(a) for a in arrays],
        scratch_shapes=[pltpu.SemaphoreType.DMA((n * npeer,)), pltpu.SemaphoreType.DMA((n * npeer,)),
                        pltpu.SemaphoreType.DMA((n,))],
        compiler_params=pltpu.CompilerParams(has_side_effects=True),
    )(*arrays)


def _peers():
    x, y, c = lax.axis_index("x"), lax.axis_index("y"), lax.axis_index("c")
    out = []
    for k in range(1, N_DEV):
        px, py, pc = x ^ ((k >> 2) & 1), y ^ ((k >> 1) & 1), c ^ (k & 1)
        out.append(((px, py, pc), 4 * px + 2 * py + pc))
    return 4 * x + 2 * y + c, out


_HBM = pl.BlockSpec(memory_space=pltpu.HBM)
_SEM = pl.BlockSpec(memory_space=pltpu.SEMAPHORE)
_DATAFLOW = pltpu.SideEffectType.DATAFLOW_SIDE_EFFECTING


def _exchange_start(arrays, *, scatter, name):
    n, npeer = len(arrays), N_DEV - 1
    lands = [lax.empty(a.shape if scatter else (N_DEV,) + a.shape, a.dtype) for a in arrays]

    def body(*refs):
        ins, lds, sems, token = refs[:n], refs[n:2 * n], refs[2 * n:4 * n], refs[-1]
        me, peers = _peers()
        for a in range(n):
            for k, (dev, idx) in enumerate(peers):
                pltpu.make_async_remote_copy(
                    src_ref=ins[a].at[idx] if scatter else ins[a], dst_ref=lds[a].at[me],
                    send_sem=sems[2 * a].at[k], recv_sem=sems[2 * a + 1].at[k], device_id=dev, device_id_type=MESH).start()
        token[...] = jnp.zeros_like(token)

    res = pl.pallas_call(
        body, name=name,
        out_shape=([pltpu.SemaphoreType.DMA((npeer,))] * (2 * n) + [pltpu.HBM(a.shape, a.dtype) for a in arrays]
                   + [pltpu.HBM(l.shape, l.dtype) for l in lands] + [SDS((8, LANES), F32)]),
        in_specs=[_HBM] * (2 * n), out_specs=[_SEM] * (2 * n) + [_HBM] * (2 * n) + [pl.BlockSpec(memory_space=pltpu.VMEM)],
        input_output_aliases={i: 2 * n + i for i in range(2 * n)},
        compiler_params=pltpu.CompilerParams(has_side_effects=_DATAFLOW),
    )(*[pltpu.with_memory_space_constraint(a, pltpu.HBM) for a in arrays],
      *[pltpu.with_memory_space_constraint(l, pltpu.HBM) for l in lands])
    sems, thru, token = res[:2 * n], res[2 * n:4 * n], res[-1]
    return [(thru[a], thru[n + a], sems[2 * a], sems[2 * a + 1]) for a in range(n)], token


def _exchange_wait(handles, after, *, scatter, name):
    n = len(handles)

    def body(*refs):
        srcs, lds, sems = refs[:n], refs[n:2 * n], refs[2 * n:4 * n]
        me, peers = _peers()
        for a in range(n):
            for k, (dev, idx) in enumerate(peers):
                cp = pltpu.make_async_remote_copy(
                    src_ref=srcs[a].at[idx] if scatter else srcs[a], dst_ref=lds[a].at[idx],
                    send_sem=sems[2 * a].at[k], recv_sem=sems[2 * a + 1].at[k], device_id=dev, device_id_type=MESH)
                cp.wait_send()
                cp.wait_recv()

    flat_sems = [s for h in handles for s in (h[2], h[3])]
    res = pl.pallas_call(
        body, name=name,
        out_shape=[pltpu.HBM(h[0].shape, h[0].dtype) for h in handles] + [pltpu.HBM(h[1].shape, h[1].dtype) for h in handles],
        in_specs=[_HBM] * (2 * n) + [_SEM] * (2 * n) + [pl.BlockSpec(memory_space=pl.ANY)], out_specs=[_HBM] * (2 * n),
        input_output_aliases={i: i for i in range(2 * n)},
        compiler_params=pltpu.CompilerParams(has_side_effects=_DATAFLOW),
    )(*[h[0] for h in handles], *[h[1] for h in handles], *flat_sems, after)
    return [(res[a], res[n + a]) for a in range(n)]


def _own_slot(landed, own, me):
    return lax.dynamic_update_slice(landed, own[None], (me,) + (0,) * own.ndim)


SMALL = (("g_mix", 1024), ("conv_w", 6144), ("conv_b", 1536), ("dt_bias", 16), ("a_log", 16), ("d_skip", 16),
         ("ssm_norm_w", 1024), ("g_q", 64), ("g_k", 64), ("f_bias", 16), ("g_xattn", 1024), ("g_mem", 1024),
         ("xg_q", 256), ("xg_k", 256), ("g_mlp", 1024))
SLAB_ROWS = 112
BIG = ("w_in", "w_out", "xq_w", "xkv_w", "xo_w", "w_up", "w_down")
WEIGHTS = ("g_mix", "w_in", "conv_w", "conv_b", "dt_bias", "a_log", "d_skip", "ssm_norm_w", "g_q", "g_k", "f_bias", "w_out",
           "g_xattn", "g_mem", "xq_w", "xkv_w", "xg_q", "xg_k", "xo_w", "g_mlp", "w_up", "w_down")
O_Z, O_XS, O_B, O_C, O_DT, O_Q, O_K, O_V, O_F, O_END = 0, 1024, 2048, 2304, 2560, 2576, 3600, 4624, 5648, 5664


def _pack_small(vals):
    rows = []
    for name, size in SMALL:
        flat = vals[name].reshape(-1).astype(F32)
        pad = -size % LANES
        rows.append(jnp.pad(flat, (0, pad)).reshape(-1, LANES))
    slab = jnp.concatenate(rows, axis=0)
    return jnp.pad(slab, ((0, SLAB_ROWS - slab.shape[0]), (0, 0)))


def _unpack_small(slab):
    out, r = {}, 0
    for name, size in SMALL:
        nr = -(-size // LANES)
        out[name] = slab[r:r + nr].reshape(-1)[:size]
        r += nr
    return out


def _cols(a, lo, hi):
    return a[:, lo:hi]


def _step(p, m, v, x, mem, target):
    S = x.shape[0]
    TM = 256
    me = 4 * lax.axis_index("x") + 2 * lax.axis_index("y") + lax.axis_index("c")

    def rms(u, g, name):
        return _rw_fwd(_rms_fn, [_whole(u)], [_whole(g)], [(D_MODEL, BF16)], tm=TM, name=name)[0]

    def pin(param, token):
        return param + token[0:1, 0:1]

    def landed_with_own(pairs, scatter):
        out = []
        for src, land in pairs:
            own = lax.dynamic_index_in_dim(src, me, 0, keepdims=False) if scatter else src
            out.append(_own_slot(land, own, me))
        return out

    ag, ag_token = _exchange_start([p["w_in"].astype(BF16), p["conv_w"]] + [p[n].astype(BF16) for n in BIG[1:]],
                                   scatter=False, name="allgather_start")
    h1 = rms(x, pin(p["g_mix"], ag_token), "rms_mix")
    win_g, convw_g = landed_with_own(_exchange_wait(ag[:2], h1, scatter=False, name="allgather_wait_in"), False)
    w_in_o = win_g.transpose(1, 0, 2).reshape(D_MODEL, O_END)
    w_in = jnp.concatenate(
        [_cols(w_in_o, O_Z, O_XS), _cols(w_in_o, O_XS, O_B), _cols(w_in_o, O_Q, O_K), _cols(w_in_o, O_K, O_V),
         _cols(w_in_o, O_V, O_F), _cols(w_in_o, O_B, O_C), _cols(w_in_o, O_C, O_DT), _cols(w_in_o, O_DT, O_Q),
         _cols(w_in_o, O_F, O_END), jnp.zeros((D_MODEL, P_COLS - C_DTF - 32), BF16)], axis=1)
    conv_w = convw_g.transpose(1, 0, 2).reshape(4, 1536)
    cw_xs, cw_bc = conv_w[:, :1024], conv_w[:, 1024:]
    cb_xs, cb_bc = p["conv_b"][:, :1024], p["conv_b"][:, 1024:]
    dt_bias, a_log, f_bias = p["dt_bias"].reshape(16, 1), p["a_log"].reshape(16, 1), p["f_bias"].reshape(16, 1)

    proj = _matmul(h1, w_in, mode="nn", tm=1024, tn=640, tk=1024, name="mm_in")
    xs_c = _conv_fwd(proj, C_XS, 1024, cw_xs, cb_xs, name="conv_xs")
    bc_c = _conv_fwd(proj, C_B, 512, cw_bc, cb_bc, name="conv_bc")
    dtf_t = proj[:, C_DTF:C_DTF + 32].T
    dt_t, acs_t, cum_t = _dtf_fwd(dtf_t, dt_bias, a_log, f_bias)
    dt_col, acs_col, cum_col = dt_t.T, acs_t.T, cum_t.T
    cum_row3 = cum_t.reshape(16, S // ATT_T, ATT_T).transpose(1, 0, 2)
    y_ssd, hs = _ssd_fwd(xs_c, dt_col, acs_col, acs_t, bc_c)
    gate_rows = [_whole(y_ssd), _whole(xs_c), (proj, C_Z, 1024)]
    gate_pars = [_whole(p["d_skip"]), _whole(p["ssm_norm_w"])]
    y_ssm = _rw_fwd(_gate_fn, gate_rows, gate_pars, [(1024, BF16)], tm=TM, name="gate")[0]
    qn = _rw_fwd(_headnorm_fn, [(proj, C_Q, 1024)], [_whole(p["g_q"])], [(1024, BF16)], tm=TM, name="qnorm")[0]
    kn = _rw_fwd(_headnorm_fn, [(proj, C_K, 1024)], [_whole(p["g_k"])], [(1024, BF16)], tm=TM, name="knorm")[0]
    o, lse = _fox_fwd(qn, kn, proj, C_V, cum_col, cum_row3)
    mixed = jnp.concatenate([y_ssm, o.astype(BF16)], axis=1)
    wout_g, xq_g, xkv_g, xo_g, wup_g, wdown_g = landed_with_own(
        _exchange_wait(ag[2:], mixed, scatter=False, name="allgather_wait_rest"), False)
    w_out = wout_g.reshape(2 * D_MODEL, D_MODEL)
    xq_w = xq_g.reshape(D_MODEL, D_MODEL)
    xkv_w = xkv_g.transpose(1, 0, 2).reshape(D_MODEL, 2 * D_MODEL)
    xo_w = xo_g.reshape(D_MODEL, D_MODEL)
    w_up = wup_g.transpose(1, 0, 2).reshape(D_MODEL, 4 * D_MODEL)
    w_down = wdown_g.reshape(4 * D_MODEL, D_MODEL)
    x1 = _matmul(mixed, w_out, mode="nn", tm=1024, tn=512, tk=2048, add=x, name="mm_out")

    h2 = rms(x1, p["g_xattn"], "rms_xattn")
    mem_n = rms(mem, p["g_mem"], "rms_mem")
    q2 = _matmul(h2, xq_w, mode="nn", tm=1024, tn=512, tk=1024, name="mm_xq")
    kv = _matmul(mem_n, xkv_w, mode="nn", tm=256, tn=1024, tk=1024, name="mm_xkv")
    xa_rows = [(q2, X_D * h, X_D) for h in range(X_HEADS)]
    xa_pars = ([(kv, X_D * h, X_D) for h in range(X_HEADS)] + [(kv, D_MODEL + X_D * h, X_D) for h in range(X_HEADS)]
               + [_whole(p["xg_q"]), _whole(p["xg_k"])])
    o2 = _rw_fwd(_xattn_fn, xa_rows, xa_pars, [(D_MODEL, BF16)], tm=TM, name="xattn")[0]
    x2 = _matmul(o2, xo_w, mode="nn", tm=1024, tn=512, tk=1024, add=x1, name="mm_xo")

    h3 = rms(x2, p["g_mlp"], "rms_mlp")
    a = _matmul(h3, w_up, mode="nn", tm=1024, tn=1024, tk=1024, name="mm_up")
    usq = _rw_fwd(_relu2_fn, [_whole(a)], [], [(4 * D_MODEL, BF16)], tm=TM, name="relu2")[0]
    x3 = _matmul(usq, w_down, mode="nn", tm=1024, tn=512, tk=2048, add=x2, name="mm_down")
    dy, loss_part = _loss_head(x3, target, tm=TM)
    loss = lax.psum(loss_part[0, 0], ("x", "y", "c"))

    def col_shards(a):
        r, c = a.shape
        return a.reshape(r, N_DEV, c // N_DEV).transpose(1, 0, 2)

    def row_shards(a):
        r, c = a.shape
        return a.reshape(N_DEV, r // N_DEV, c)

    g = {}
    g["w_down"] = _matmul(usq, dy, mode="tn", out_dtype=GRAD_WIRE, tm=1024, tn=1024, tk=1024, name="mm_d_wdown")
    dusq = _matmul(dy, w_down, mode="nt", tm=1024, tn=1024, tk=1024, name="mm_d_usq")
    da = _rw_bwd(_relu2_fn, [_whole(a)], [], [_whole(dusq)], tm=TM, name="relu2_bwd", row_grads=[BF16])[0]
    g["w_up"] = _matmul(h3, da, mode="tn", out_dtype=GRAD_WIRE, tm=1024, tn=1024, tk=1024, name="mm_d_wup")
    sent_mlp, token = _exchange_start([row_shards(g["w_down"]), col_shards(g["w_up"])], scatter=True,
                                      name="grads_start_mlp")
    dh3 = _matmul(da, w_up, mode="nt", tm=1024, tn=512, tk=2048, name="mm_d_h3")
    dx2, g["g_mlp"] = _rw_bwd(_rms_fn, [_whole(x2)], [_whole(pin(p["g_mlp"], token))], [_whole(dh3)], tm=TM,
                              name="rms_mlp_bwd", row_grads=[F32], adds={0: _whole(dy)})

    g["xo_w"] = _matmul(o2, dx2, mode="tn", out_dtype=GRAD_WIRE, tm=1024, tn=1024, tk=1024, name="mm_d_wxo")
    do2 = _matmul(dx2, xo_w, mode="nt", tm=1024, tn=512, tk=1024, name="mm_d_o2")
    xa = _rw_bwd(_xattn_fn, xa_rows, xa_pars, [_whole(do2)], tm=TM, name="xattn_bwd", row_grads=[BF16] * X_HEADS)
    dq2 = jnp.concatenate(xa[:X_HEADS], axis=1)
    dkv = jnp.concatenate(xa[X_HEADS:3 * X_HEADS], axis=1)
    g["xg_q"], g["xg_k"] = xa[3 * X_HEADS], xa[3 * X_HEADS + 1]
    g["xq_w"] = _matmul(h2, dq2, mode="tn", out_dtype=GRAD_WIRE, tm=1024, tn=1024, tk=1024, name="mm_d_wxq")
    dh2 = _matmul(dq2, xq_w, mode="nt", tm=1024, tn=512, tk=1024, name="mm_d_h2")
    g["xkv_w"] = _matmul(mem_n, dkv, mode="tn", out_dtype=GRAD_WIRE, tm=1024, tn=1024, tk=256, name="mm_d_wxkv")
    dmem_n = _matmul(dkv, xkv_w, mode="nt", tm=256, tn=1024, tk=2048, name="mm_d_memn")
    g["g_mem"] = _rw_bwd(_rms_fn, [_whole(mem)], [_whole(p["g_mem"])], [_whole(dmem_n)], tm=TM, name="rms_mem_bwd",
                         row_grads=[None])[0]
    dx1, g["g_xattn"] = _rw_bwd(_rms_fn, [_whole(x1)], [_whole(p["g_xattn"])], [_whole(dh2)], tm=TM, name="rms_xattn_bwd",
                                row_grads=[F32], adds={0: _whole(dx2)})

    g["w_out"] = _matmul(mixed, dx1, mode="tn", out_dtype=GRAD_WIRE, tm=1024, tn=1024, tk=1024, name="mm_d_wout")
    sent_mid, token = _exchange_start(
        [row_shards(g["w_out"]), row_shards(g["xq_w"]), col_shards(g["xkv_w"]), row_shards(g["xo_w"])], scatter=True,
        name="grads_start_mid")
    dmixed = _matmul(dx1, w_out, mode="nt", tm=1024, tn=1024, tk=1024, name="mm_d_mixed")
    dqn, dkn, dv, dcum4 = _fox_bwd(qn, kn, proj, C_V, cum_col, cum_row3, lse, dmixed, 1024)
    dq, g["g_q"] = _rw_bwd(_headnorm_fn, [(proj, C_Q, 1024)], [_whole(pin(p["g_q"], token))], [_whole(dqn)], tm=TM,
                           name="qnorm_bwd", row_grads=[BF16])
    dk, g["g_k"] = _rw_bwd(_headnorm_fn, [(proj, C_K, 1024)], [_whole(p["g_k"])], [_whole(dkn)], tm=TM, name="knorm_bwd",
                           row_grads=[BF16])
    dy_ssd, dxs_g, dz, g["d_skip"], g["ssm_norm_w"] = _rw_bwd(
        _gate_fn, gate_rows, gate_pars, [(dmixed, 0, 1024)], tm=TM, name="gate_bwd", row_grads=[F32, F32, BF16])
    dxs_s, ddt_col, dacs_col, dacs_row, d_b, d_c = _ssd_bwd(xs_c, dt_col, acs_col, acs_t, bc_c, hs, dy_ssd)
    dcum_t = dcum4[:, :, 0:2, :].transpose(0, 2, 1, 3).reshape(16, S)
    ddtf_t, ddtb, dalog, dfb = _dtf_bwd(dtf_t, dt_bias, a_log, f_bias, ddt_col.T, dacs_col.T, dacs_row, dcum_t)
    g["dt_bias"], g["a_log"], g["f_bias"] = ddtb, dalog, dfb
    dxs_raw, dcw_xs, dcb_xs = _conv_bwd(proj, C_XS, 1024, cw_xs, cb_xs, [dxs_s, dxs_g], name="conv_xs_bwd")
    dbc_raw, dcw_bc, dcb_bc = _conv_bwd(proj, C_B, 512, cw_bc, cb_bc, [jnp.concatenate([d_b, d_c], axis=1)],
                                        name="conv_bc_bwd")
    g["conv_w"] = jnp.concatenate([dcw_xs, dcw_bc], axis=1)
    g["conv_b"] = jnp.concatenate([dcb_xs, dcb_bc], axis=1)
    ddtf = jnp.pad(ddtf_t.T.astype(BF16), ((0, 0), (0, P_COLS - C_DTF - 32)))
    dproj = jnp.concatenate([dz, dxs_raw, dq, dk, dv.astype(BF16), dbc_raw, ddtf], axis=1)
    dw_in_p = _matmul(h1, dproj, mode="tn", out_dtype=GRAD_WIRE, tm=1024, tn=640, tk=1024, name="mm_d_win")
    g["w_in"] = jnp.concatenate(
        [_cols(dw_in_p, C_Z, C_Q), _cols(dw_in_p, C_B, C_DTF + 16), _cols(dw_in_p, C_Q, C_B),
         _cols(dw_in_p, C_DTF + 16, C_DTF + 32)], axis=1)
    sent_in, token = _exchange_start([col_shards(g["w_in"])], scatter=True, name="grads_start_in")
    dh1 = _matmul(dproj, w_in, mode="nt", tm=1024, tn=512, tk=1920, name="mm_d_h1")
    grad_x, g["g_mix"] = _rw_bwd(_rms_fn, [_whole(x)], [_whole(pin(p["g_mix"], token))], [_whole(dh1)], tm=TM,
                                 name="rms_mix_bwd", row_grads=[F32], adds={0: _whole(dx1)})
    small_parts = _exchange([_pack_small(g)], scatter=False, name="gather_small_grads")[0]

    parts = dict(zip(("w_down", "w_up"), landed_with_own(
        _exchange_wait(sent_mlp, small_parts, scatter=True, name="grads_wait_mlp"), True), strict=True))
    parts.update(zip(("w_out", "xq_w", "xkv_w", "xo_w"), landed_with_own(
        _exchange_wait(sent_mid, small_parts, scatter=True, name="grads_wait_mid"), True), strict=True))
    parts.update(zip(("w_in",), landed_with_own(
        _exchange_wait(sent_in, small_parts, scatter=True, name="grads_wait_in"), True), strict=True))
    grads, delta, new_m, new_v = {}, {}, {}, {}
    for name in ("w_down", "w_up", "w_out", "xq_w", "xkv_w", "xo_w", "w_in"):
        part = parts[name]
        grads[name], delta[name], new_m[name], new_v[name] = _reduce_adamw(part, p[name], m[name], v[name], tr=128,
                                                                            name="adamw_" + name)
    zeros_cw = jnp.zeros((4, 1536), F32)
    slabs = [_pack_small({**d, "conv_w": zeros_cw}) for d in (p, m, v)]
    sg, sd, sm, sv = _reduce_adamw(small_parts, *slabs, tr=SLAB_ROWS, name="adamw_small")
    for dst, slab in ((grads, sg), (delta, sd), (new_m, sm), (new_v, sv)):
        for name, flat in _unpack_small(slab).items():
            if name != "conv_w":
                dst[name] = flat.reshape(p[name].shape)
    cw_shard = p["conv_w"].shape[1]
    grads["conv_w"] = lax.dynamic_slice(_unpack_small(sg)["conv_w"].reshape(4, 1536), (0, me * cw_shard), (4, cw_shard))
    delta["conv_w"], new_m["conv_w"], new_v["conv_w"] = _adamw(p["conv_w"], grads["conv_w"], m["conv_w"], v["conv_w"],
                                                               name="adamw_conv_w")
    return loss, grad_x, grads, delta, new_m, new_v


def kernel(x, mem, g_mix, w_in, conv_w, conv_b, dt_bias, a_log, d_skip, ssm_norm_w, g_q, g_k, f_bias, w_out, g_xattn, g_mem, xq_w, xkv_w, xg_q, xg_k, xo_w, g_mlp, w_up, w_down, loss_target, m_g_mix, m_w_in, m_conv_w, m_conv_b, m_dt_bias, m_a_log, m_d_skip, m_ssm_norm_w, m_g_q, m_g_k, m_f_bias, m_w_out, m_g_xattn, m_g_mem, m_xq_w, m_xkv_w, m_xg_q, m_xg_k, m_xo_w, m_g_mlp, m_w_up, m_w_down, v_g_mix, v_w_in, v_conv_w, v_conv_b, v_dt_bias, v_a_log, v_d_skip, v_ssm_norm_w, v_g_q, v_g_k, v_f_bias, v_w_out, v_g_xattn, v_g_mem, v_xq_w, v_xkv_w, v_xg_q, v_xg_k, v_xo_w, v_g_mlp, v_w_up, v_w_down):
    args = locals()
    drop = lambda t: t[0] if t.ndim == 3 else t
    p = {n: drop(args[n]) for n in WEIGHTS}
    m = {n: drop(args["m_" + n]) for n in WEIGHTS}
    v = {n: drop(args["v_" + n]) for n in WEIGHTS}
    loss, grad_x, grads, delta, new_m, new_v = _step(p, m, v, x[0], mem[0], loss_target[0])
    outs = [loss, grad_x[None]]
    for d in (grads, delta, new_m, new_v):
        outs += [d[n].reshape(args[n].shape) for n in WEIGHTS]
    return tuple(outs)
```

```python
import functools
import math

import jax
import jax.numpy as jnp
from jax import lax
from jax.experimental import pallas as pl
from jax.experimental.pallas import tpu as pltpu

F32, BF16 = jnp.float32, jnp.bfloat16
SDS = jax.ShapeDtypeStruct
HI = lax.Precision.HIGHEST
MESH = pl.DeviceIdType.MESH

N_DEV = 8
EPS = 1e-5
D_MODEL = 1024
SSM_HEADS, SSM_P, SSM_N, SSM_GROUPS, CHUNK = 16, 64, 128, 2, 128
ATT_HEADS, ATT_D = 16, 64
X_HEADS, X_D = 4, 256
LANES = 128
VMEM_LIMIT = 48 * 1024 * 1024
NEG = -1e30

GRAD_WIRE = BF16
ADAM_LR, ADAM_B1, ADAM_B2, ADAM_EPS, ADAM_WD, ADAM_STEP = 0.001, 0.9, 0.999, 1e-08, 0.01, 10

C_Z, C_XS, C_Q, C_K, C_V, C_B, C_C, C_DTF, P_COLS = 0, 1024, 2048, 3072, 4096, 5120, 5376, 5632, 5760

_NN = (((1,), (0,)), ((), ()))
_NT = (((1,), (1,)), ((), ()))
_TN = (((0,), (0,)), ((), ()))


def _cparams(**kw):
    return pltpu.CompilerParams(vmem_limit_bytes=VMEM_LIMIT, **kw)


def _bdot(a, b, dn):
    return lax.dot_general(a.astype(BF16), b.astype(BF16), dn, preferred_element_type=F32)


@jax.custom_vjp
def mm_nn(a, b):
    return _bdot(a, b, _NN)


mm_nn.defvjp(lambda a, b: (mm_nn(a, b), (a, b)), lambda r, g: (_bdot(g, r[1], _NT), _bdot(r[0], g, _TN)))


@jax.custom_vjp
def mm_nt(a, b):
    return _bdot(a, b, _NT)


mm_nt.defvjp(lambda a, b: (mm_nt(a, b), (a, b)), lambda r, g: (_bdot(g, r[1], _NN), _bdot(g, r[0], _TN)))


@jax.custom_vjp
def mm_tn(a, b):
    return _bdot(a, b, _TN)


mm_tn.defvjp(lambda a, b: (mm_tn(a, b), (a, b)), lambda r, g: (_bdot(r[1], g, _NT), _bdot(r[0], g, _NN)))


def _cdot(x, c):
    return jnp.dot(x, c, precision=HI, preferred_element_type=F32)


def _iota(shape, dim):
    return lax.broadcasted_iota(jnp.int32, shape, dim)


def _matmul(a, b, *, mode, tm, tn, tk, name, out_dtype=F32, add=None, extras=(), epilogue=None, out_dtypes=None):
    if mode == "tn":
        K, M = a.shape
    else:
        M, K = a.shape
    N = b.shape[0] if mode == "nt" else b.shape[1]
    tm, tn, tk = min(tm, M), min(tn, N), min(tk, K)
    assert M % tm == 0 and N % tn == 0 and K % tk == 0, (name, M, N, K, tm, tn, tk)
    nk = K // tk
    dn = {"nn": _NN, "nt": _NT, "tn": _TN}[mode]
    if add is not None:
        extras, epilogue = (add,), lambda acc, r: (acc + r,)
    elif epilogue is None:
        epilogue = lambda acc: (acc,)
    out_dtypes = out_dtypes or [out_dtype]
    ne, no = len(extras), len(out_dtypes)

    def body(*refs):
        a_ref, b_ref = refs[:2]
        e_refs, o_refs, acc_ref = refs[2:2 + ne], refs[2 + ne:2 + ne + no], refs[-1]
        k = pl.program_id(2)

        @pl.when(k == 0)
        def _():
            acc_ref[...] = jnp.zeros_like(acc_ref)

        acc_ref[...] += _bdot(a_ref[...], b_ref[...], dn)

        @pl.when(k == nk - 1)
        def _():
            res = epilogue(acc_ref[...], *[e[...] for e in e_refs])
            for o_ref, v in zip(o_refs, res, strict=True):
                o_ref[...] = v.astype(o_ref.dtype)

    a_spec = pl.BlockSpec((tk, tm), lambda i, j, k: (k, i)) if mode == "tn" else pl.BlockSpec((tm, tk), lambda i, j, k: (i, k))
    b_spec = pl.BlockSpec((tn, tk), lambda i, j, k: (j, k)) if mode == "nt" else pl.BlockSpec((tk, tn), lambda i, j, k: (k, j))
    o_spec = pl.BlockSpec((tm, tn), lambda i, j, k: (i, j))
    res = pl.pallas_call(
        body, name=name, grid=(M // tm, N // tn, nk), in_specs=[a_spec, b_spec] + [o_spec] * ne, out_specs=[o_spec] * no,
        out_shape=[SDS((M, N), dt) for dt in out_dtypes], scratch_shapes=[pltpu.VMEM((tm, tn), F32)],
        compiler_params=_cparams(dimension_semantics=("parallel", "parallel", "arbitrary")),
    )(a, b, *extras)
    return res[0] if no == 1 else res


def _row_spec(tm, spec):
    _, c0, w = spec
    assert c0 % w == 0
    return pl.BlockSpec((tm, w), functools.partial(lambda i, cb: (i, cb), cb=c0 // w))


def _par_spec(spec):
    arr, c0, w = spec
    assert c0 % w == 0
    return pl.BlockSpec((arr.shape[0], w), functools.partial(lambda i, cb: (0, cb), cb=c0 // w))


def _whole(arr):
    return (arr, 0, arr.shape[1])


def _rw_fwd(fn, rows, params, outs, *, tm, name):
    M = rows[0][0].shape[0]
    nr, npar = len(rows), len(params)

    def body(*refs):
        rv = [r[...].astype(F32) for r in refs[:nr]]
        pv = [p[...].astype(F32) for p in refs[nr:nr + npar]]
        res = fn(*rv, *pv)
        for o_ref, v in zip(refs[nr + npar:], res, strict=True):
            o_ref[...] = v.astype(o_ref.dtype)

    return pl.pallas_call(
        body, name=name, grid=(M // tm,),
        in_specs=[_row_spec(tm, r) for r in rows] + [_par_spec(p) for p in params],
        out_specs=[pl.BlockSpec((tm, w), lambda i: (i, 0)) for w, _ in outs],
        out_shape=[SDS((M, w), dt) for w, dt in outs],
        compiler_params=_cparams(dimension_semantics=("parallel",)),
    )(*[r[0] for r in rows], *[p[0] for p in params])


def _rw_bwd(fn, rows, params, cts, *, tm, name, row_grads, adds=None):
    M = rows[0][0].shape[0]
    adds = adds or {}
    nr, npar, nc = len(rows), len(params), len(cts)
    add_keys = sorted(adds)
    want = [k for k in range(nr) if row_grads[k] is not None]

    def body(*refs):
        pos = 0
        r_refs = refs[pos:pos + nr]; pos += nr
        p_refs = refs[pos:pos + npar]; pos += npar
        c_refs = refs[pos:pos + nc]; pos += nc
        a_refs = dict(zip(add_keys, refs[pos:pos + len(add_keys)])); pos += len(add_keys)
        dr_refs = dict(zip(want, refs[pos:pos + len(want)])); pos += len(want)
        dp_refs = refs[pos:pos + npar]
        rv = [r[...].astype(F32) for r in r_refs]
        pv = [p[...].astype(F32) for p in p_refs]
        _, vjp = jax.vjp(fn, *rv, *pv)
        g = vjp(tuple(c[...].astype(F32) for c in c_refs))
        for k in want:
            v = g[k]
            if k in a_refs:
                v = v + a_refs[k][...].astype(F32)
            dr_refs[k][...] = v.astype(dr_refs[k].dtype)
        first = pl.program_id(0) == 0
        for j in range(npar):
            @pl.when(first)
            def _(j=j):
                dp_refs[j][...] = jnp.zeros_like(dp_refs[j])
            dp_refs[j][...] += g[nr + j]

    res = pl.pallas_call(
        body, name=name, grid=(M // tm,),
        in_specs=([_row_spec(tm, r) for r in rows] + [_par_spec(p) for p in params] + [_row_spec(tm, c) for c in cts]
                  + [_row_spec(tm, adds[k]) for k in add_keys]),
        out_specs=([pl.BlockSpec((tm, rows[k][2]), lambda i: (i, 0)) for k in want]
                   + [pl.BlockSpec((p[0].shape[0], p[2]), lambda i: (0, 0)) for p in params]),
        out_shape=([SDS((M, rows[k][2]), row_grads[k]) for k in want] + [SDS((p[0].shape[0], p[2]), F32) for p in params]),
        compiler_params=_cparams(dimension_semantics=("arbitrary",)),
    )(*[r[0] for r in rows], *[p[0] for p in params], *[c[0] for c in cts], *[adds[k][0] for k in add_keys])
    return res


def _rms_fn(x, g):
    r = lax.rsqrt(jnp.mean(x * x, axis=-1, keepdims=True) + EPS)
    return (x * r * g,)


def _seg_mats(width, seg):
    n = width // seg
    p = (_iota((width, n), 0) // seg == _iota((width, n), 1)).astype(F32)
    e = (_iota((n, width), 1) // seg == _iota((n, width), 0)).astype(F32)
    return p, e


def _gate_fn(y, xs, z, dskip, w):
    width = SSM_HEADS * SSM_P
    _, e = _seg_mats(width, SSM_P)
    y = (y + _cdot(dskip, e) * xs) * (z * jax.nn.sigmoid(z))
    g0 = _iota((1, width), 1) < width // SSM_GROUPS
    y2 = y * y
    gw = width // SSM_GROUPS
    ms0 = jnp.sum(jnp.where(g0, y2, 0.0), axis=-1, keepdims=True) * (1.0 / gw)
    ms1 = jnp.sum(jnp.where(g0, 0.0, y2), axis=-1, keepdims=True) * (1.0 / gw)
    r = jnp.where(g0, lax.rsqrt(ms0 + EPS), lax.rsqrt(ms1 + EPS))
    return (y * r * w,)


def _xattn_fn(q0, q1, q2, q3, k0, k1, k2, k3, v0, v1, v2, v3, gq, gk):
    def norm(u, g):
        return u * lax.rsqrt(jnp.mean(u * u, axis=-1, keepdims=True) + EPS) * g
    outs = []
    for q, k, v in ((q0, k0, v0), (q1, k1, v1), (q2, k2, v2), (q3, k3, v3)):
        s = mm_nt(norm(q, gq), norm(k, gk)) * (X_D ** -0.5)
        p = jnp.exp(s - lax.stop_gradient(jnp.max(s, axis=-1, keepdims=True)))
        p = p / jnp.sum(p, axis=-1, keepdims=True)
        outs.append(mm_nn(p, v))
    return (jnp.concatenate(outs, axis=-1),)


CONV_TC = 256


def _shift_down(u, k):
    if k == 0:
        return u
    return jnp.where(_iota(u.shape, 0) >= k, pltpu.roll(u, k, axis=0), 0.0)


def _shift_up(u, k):
    if k == 0:
        return u
    n = u.shape[0]
    return jnp.where(_iota(u.shape, 0) < n - k, pltpu.roll(u, n - k, axis=0), 0.0)


def _conv_pre(u, w_ref, b):
    pre = b + w_ref[3:4, :] * u
    for k in (1, 2, 3):
        pre = pre + w_ref[3 - k:4 - k, :] * _shift_down(u, k)
    return pre


def _conv_fwd(src, c0, width, w, b, *, name):
    S = src.shape[0]
    cb0 = c0 // CONV_TC

    def body(u_ref, w_ref, b_ref, o_ref):
        pre = _conv_pre(u_ref[...], w_ref, b_ref[...])
        o_ref[...] = pre * jax.nn.sigmoid(pre)

    return pl.pallas_call(
        body, name=name, grid=(width // CONV_TC,),
        in_specs=[pl.BlockSpec((S, CONV_TC), lambda j: (0, cb0 + j)), pl.BlockSpec((4, CONV_TC), lambda j: (0, j)),
                  pl.BlockSpec((1, CONV_TC), lambda j: (0, j))],
        out_specs=pl.BlockSpec((S, CONV_TC), lambda j: (0, j)), out_shape=SDS((S, width), F32),
        compiler_params=_cparams(dimension_semantics=("parallel",)),
    )(src, w, b)


def _conv_bwd(src, c0, width, w, b, douts, *, name):
    S = src.shape[0]
    cb0 = c0 // CONV_TC
    nd = len(douts)

    def body(*refs):
        u_ref, w_ref, b_ref = refs[:3]
        d_refs = refs[3:3 + nd]
        du_ref, dw_ref, db_ref = refs[3 + nd:]
        u = u_ref[...]
        pre = _conv_pre(u, w_ref, b_ref[...])
        sg = jax.nn.sigmoid(pre)
        dout = d_refs[0][...]
        for r in d_refs[1:]:
            dout = dout + r[...]
        dpre = dout * (sg * (1.0 + pre * (1.0 - sg)))
        du = w_ref[3:4, :] * dpre
        dw_ref[3:4, :] = jnp.sum(dpre * u, axis=0, keepdims=True)
        for k in (1, 2, 3):
            du = du + w_ref[3 - k:4 - k, :] * _shift_up(dpre, k)
            dw_ref[3 - k:4 - k, :] = jnp.sum(dpre * _shift_down(u, k), axis=0, keepdims=True)
        du_ref[...] = du.astype(du_ref.dtype)
        db_ref[...] = jnp.sum(dpre, axis=0, keepdims=True)

    return pl.pallas_call(
        body, name=name, grid=(width // CONV_TC,),
        in_specs=[pl.BlockSpec((S, CONV_TC), lambda j: (0, cb0 + j)), pl.BlockSpec((4, CONV_TC), lambda j: (0, j)),
                  pl.BlockSpec((1, CONV_TC), lambda j: (0, j))] + [pl.BlockSpec((S, CONV_TC), lambda j: (0, j))] * nd,
        out_specs=[pl.BlockSpec((S, CONV_TC), lambda j: (0, j)), pl.BlockSpec((4, CONV_TC), lambda j: (0, j)),
                   pl.BlockSpec((1, CONV_TC), lambda j: (0, j))],
        out_shape=[SDS((S, width), BF16), SDS((4, width), F32), SDS((1, width), F32)],
        compiler_params=_cparams(dimension_semantics=("parallel",)),
    )(src, w, b, *douts)


def _softplus(x):
    return jnp.maximum(x, 0.0) + jnp.log(1.0 + jnp.exp(-jnp.abs(x)))


def _prefix_sum(x, seg):
    n = x.shape[1]
    pos = _iota(x.shape, 1) % seg
    k = 1
    while k < seg:
        x = x + jnp.where(pos >= k, pltpu.roll(x, k, axis=1), 0.0)
        k *= 2
    return x


def _suffix_sum(x, seg):
    n = x.shape[1]
    pos = _iota(x.shape, 1) % seg
    k = 1
    while k < seg:
        x = x + jnp.where(pos + k < seg, pltpu.roll(x, n - k, axis=1), 0.0)
        k *= 2
    return x


def _dtf_fwd(dtf_t, dt_bias, a_log, f_bias):
    S = dtf_t.shape[1]

    def body(x_ref, db_ref, al_ref, fb_ref, dt_ref, acs_ref, cum_ref):
        dt = _softplus(x_ref[0:16, :] + db_ref[...])
        dt_ref[...] = dt
        acs_ref[...] = _prefix_sum(dt * (-jnp.exp(al_ref[...])), CHUNK)
        cum_ref[...] = _prefix_sum(-_softplus(-(x_ref[16:32, :] + fb_ref[...])), S)

    return pl.pallas_call(body, name="dtf_fwd", out_shape=[SDS((16, S), F32)] * 3, compiler_params=_cparams())(
        dtf_t, dt_bias, a_log, f_bias)


def _dtf_bwd(dtf_t, dt_bias, a_log, f_bias, d_dt, d_acs_a, d_acs_b, d_cum):
    S = dtf_t.shape[1]

    def body(x_ref, db_ref, al_ref, fb_ref, ddt_ref, da1_ref, da2_ref, dc_ref, dx_ref, ddb_ref, dal_ref, dfb_ref):
        xd = x_ref[0:16, :] + db_ref[...]
        dt = _softplus(xd)
        a = -jnp.exp(al_ref[...])
        d_da = _suffix_sum(da1_ref[...] + da2_ref[...], CHUNK)
        d_dt = ddt_ref[...] + d_da * a
        dal_ref[...] = jnp.sum(d_da * dt, axis=1, keepdims=True) * a
        d_xd = d_dt * jax.nn.sigmoid(xd)
        ddb_ref[...] = jnp.sum(d_xd, axis=1, keepdims=True)
        xf = x_ref[16:32, :] + fb_ref[...]
        d_xf = _suffix_sum(dc_ref[...], S) * jax.nn.sigmoid(-xf)
        dfb_ref[...] = jnp.sum(d_xf, axis=1, keepdims=True)
        dx_ref[0:16, :] = d_xd
        dx_ref[16:32, :] = d_xf

    return pl.pallas_call(body, name="dtf_bwd", out_shape=[SDS((32, S), F32)] + [SDS((16, 1), F32)] * 3,
                          compiler_params=_cparams())(dtf_t, dt_bias, a_log, f_bias, d_dt, d_acs_a, d_acs_b, d_cum)


def _ssd_chunk(xs, dtc, acol, arow, bm, cm, h, *, hp):
    L = CHUNK
    first = _iota((1, LANES), 1) < SSM_P
    i16, s16 = _iota((L, 16), 1), _iota((16, L), 0)
    ha, hb = 2 * hp, 2 * hp + 1

    def selc(blk, hh):
        return jnp.sum(jnp.where(i16 == hh, blk, 0.0), axis=1, keepdims=True)

    def selr(blk, hh):
        return jnp.sum(jnp.where(s16 == hh, blk, 0.0), axis=0, keepdims=True)

    x = xs * jnp.where(first, selc(dtc, ha), selc(dtc, hb))
    ca, cb, ra, rb = selc(acol, ha), selc(acol, hb), selr(arow, ha), selr(arow, hb)
    tri = _iota((L, L), 0) >= _iota((L, L), 1)
    cbm = mm_nt(cm, bm)
    la = jnp.exp(jnp.where(tri, ca - ra, NEG))
    lb = jnp.exp(jnp.where(tri, cb - rb, NEG))
    y = jnp.where(first, mm_nn(cbm * la, x), mm_nn(cbm * lb, x))
    y = y + jnp.where(first, jnp.exp(ca), jnp.exp(cb)) * mm_nn(cm, h)
    last = _iota((1, L), 1) == L - 1
    ala = jnp.sum(jnp.where(last, ra, 0.0), axis=1, keepdims=True)
    alb = jnp.sum(jnp.where(last, rb, 0.0), axis=1, keepdims=True)
    dec = jnp.where(first, jnp.exp(ala - ca), jnp.exp(alb - cb))
    hn = jnp.where(first, jnp.exp(ala), jnp.exp(alb)) * h + mm_tn(bm, x * dec)
    return y, hn


def _ssd_specs(nc):
    L = CHUNK
    hpg = SSM_HEADS // 2 // SSM_GROUPS

    def mk(rev):
        cidx = (lambda c: nc - 1 - c) if rev else (lambda c: c)
        return dict(
            xs=pl.BlockSpec((L, LANES), lambda c, hp: (cidx(c), hp)),
            col=pl.BlockSpec((L, 16), lambda c, hp: (cidx(c), 0)),
            row=pl.BlockSpec((16, L), lambda c, hp: (0, cidx(c))),
            b=pl.BlockSpec((L, SSM_N), lambda c, hp: (cidx(c), hp // hpg)),
            c=pl.BlockSpec((L, SSM_N), lambda c, hp: (cidx(c), SSM_GROUPS + hp // hpg)),
            grp=pl.BlockSpec((L, SSM_N), lambda c, hp: (cidx(c), hp // hpg)),
            st=pl.BlockSpec((1, 1, SSM_N, LANES), lambda c, hp: (cidx(c), hp, 0, 0)),
        )
    return mk


def _ssd_fwd(xs, dt_col, acs_col, acs_row, bc):
    S = xs.shape[0]
    nc, nhp = S // CHUNK, SSM_HEADS // 2
    sp = _ssd_specs(nc)(False)

    def body(xs_ref, dt_ref, ac_ref, ar_ref, b_ref, c_ref, y_ref, hs_ref, h_scr):
        c, hp = pl.program_id(0), pl.program_id(1)

        @pl.when(c == 0)
        def _():
            h_scr[hp] = jnp.zeros((SSM_N, LANES), F32)

        h = h_scr[hp]
        hs_ref[0, 0] = h
        y, hn = _ssd_chunk(xs_ref[...], dt_ref[...], ac_ref[...], ar_ref[...], b_ref[...], c_ref[...], h, hp=hp)
        y_ref[...] = y
        h_scr[hp] = hn

    return pl.pallas_call(
        body, name="ssd_fwd", grid=(nc, nhp),
        in_specs=[sp["xs"], sp["col"], sp["col"], sp["row"], sp["b"], sp["c"]],
        out_specs=[sp["xs"], sp["st"]],
        out_shape=[SDS((S, SSM_HEADS * SSM_P), F32), SDS((nc, nhp, SSM_N, LANES), F32)],
        scratch_shapes=[pltpu.VMEM((nhp, SSM_N, LANES), F32)],
        compiler_params=_cparams(dimension_semantics=("arbitrary", "arbitrary")),
    )(xs, dt_col, acs_col, acs_row, bc, bc)


def _ssd_bwd(xs, dt_col, acs_col, acs_row, bc, hs, dy):
    S = xs.shape[0]
    nc, nhp = S // CHUNK, SSM_HEADS // 2
    hpg = nhp // SSM_GROUPS
    sp = _ssd_specs(nc)(True)

    def body(xs_ref, dt_ref, ac_ref, ar_ref, b_ref, c_ref, hs_ref, dy_ref,
             dxs_ref, ddt_ref, dac_ref, dar_ref, db_ref, dc_ref, dh_scr):
        c, hp = pl.program_id(0), pl.program_id(1)

        @pl.when(c == 0)
        def _():
            dh_scr[hp] = jnp.zeros((SSM_N, LANES), F32)

        _, vjp = jax.vjp(functools.partial(_ssd_chunk, hp=hp), xs_ref[...], dt_ref[...], ac_ref[...], ar_ref[...],
                         b_ref[...], c_ref[...], hs_ref[0, 0])
        dxs, ddt, dac, dar, db, dc, dh = vjp((dy_ref[...], dh_scr[hp]))
        dxs_ref[...] = dxs
        dh_scr[hp] = dh

        @pl.when(hp == 0)
        def _():
            ddt_ref[...] = jnp.zeros_like(ddt_ref)
            dac_ref[...] = jnp.zeros_like(dac_ref)
            dar_ref[...] = jnp.zeros_like(dar_ref)

        ddt_ref[...] += ddt
        dac_ref[...] += dac
        dar_ref[...] += dar

        @pl.when(hp % hpg == 0)
        def _():
            db_ref[...] = jnp.zeros_like(db_ref)
            dc_ref[...] = jnp.zeros_like(dc_ref)

        db_ref[...] += db
        dc_ref[...] += dc

    return pl.pallas_call(
        body, name="ssd_bwd", grid=(nc, nhp),
        in_specs=[sp["xs"], sp["col"], sp["col"], sp["row"], sp["b"], sp["c"], sp["st"], sp["xs"]],
        out_specs=[sp["xs"], sp["col"], sp["col"], sp["row"], sp["grp"], sp["grp"]],
        out_shape=[SDS((S, SSM_HEADS * SSM_P), F32), SDS((S, 16), F32), SDS((S, 16), F32), SDS((16, S), F32),
                   SDS((S, SSM_GROUPS * SSM_N), F32), SDS((S, SSM_GROUPS * SSM_N), F32)],
        scratch_shapes=[pltpu.VMEM((nhp, SSM_N, LANES), F32)],
        compiler_params=_cparams(dimension_semantics=("arbitrary", "arbitrary")),
    )(xs, dt_col, acs_col, acs_row, bc, bc, hs, dy)


ATT_T = 512


def _pick_col(blk, h):
    return jnp.sum(jnp.where(_iota(blk.shape, 1) == h, blk, 0.0), axis=1, keepdims=True)


def _pick_row(blk, h):
    return jnp.sum(jnp.where(_iota(blk.shape, 0) == h, blk, 0.0), axis=0, keepdims=True)


def _pair_norm(x, g2, first):
    x2 = x * x
    sa = jnp.sum(jnp.where(first, x2, 0.0), axis=1, keepdims=True)
    sb = jnp.sum(jnp.where(first, 0.0, x2), axis=1, keepdims=True)
    r = jnp.where(first, lax.rsqrt(sa * (1.0 / ATT_D) + EPS), lax.rsqrt(sb * (1.0 / ATT_D) + EPS))
    return x * r * g2, r


def _pair_norm_bwd(dxn, x, r, g2, first):
    t = dxn * g2
    tx = t * x
    ma = jnp.sum(jnp.where(first, tx, 0.0), axis=1, keepdims=True)
    mb = jnp.sum(jnp.where(first, 0.0, tx), axis=1, keepdims=True)
    dx = r * (t - x * (r * r) * (jnp.where(first, ma, mb) * (1.0 / ATT_D)))
    return dx, jnp.sum(dxn * x * r, axis=0, keepdims=True)


def _fox_fwd(src, q_c0, k_c0, v_c0, gq2, gk2, cum_col, cum_row3):
    S = src.shape[0]
    T = ATT_T
    nq, nhp = S // T, ATT_HEADS // 2
    qb0, kb0, vb0 = q_c0 // LANES, k_c0 // LANES, v_c0 // LANES
    scale = ATT_D ** -0.5

    def body(q_ref, kraw_ref, v_ref, gq_ref, gk_ref, cc_ref, cr_ref, o_ref, l_ref, k_ref):
        hp, i = pl.program_id(0), pl.program_id(1)
        first = _iota((1, LANES), 1) < ATT_D

        @pl.when(i == 0)
        def _():
            k_ref[...] = _pair_norm(kraw_ref[...], gk_ref[...], first)[0].astype(BF16)

        q = (_pair_norm(q_ref[...], gq_ref[...], first)[0] * scale).astype(BF16)
        zero = jnp.zeros_like(q)
        qs = (jnp.where(first, q, zero), jnp.where(first, zero, q))
        cc = cc_ref[...]
        cq = (_pick_col(cc, 2 * hp), _pick_col(cc, 2 * hp + 1))
        tri = _iota((T, T), 0) >= _iota((T, T), 1)

        def tile(j, carry, diagonal):
            off = pl.multiple_of(j * T, T)
            k = k_ref[pl.ds(off, T), :]
            v = v_ref[pl.ds(off, T), :].astype(BF16)
            cr = cr_ref[j]
            out = []
            for hh in range(2):
                m, l, acc = carry[3 * hh:3 * hh + 3]
                s = _bdot(qs[hh], k, _NT) + (cq[hh] - _pick_row(cr, 2 * hp + hh))
                if diagonal:
                    s = jnp.where(tri, s, NEG)
                m_new = jnp.maximum(m, jnp.max(s, axis=1, keepdims=True))
                alpha = jnp.exp(m - m_new)
                p = jnp.exp(s - m_new)
                out += [m_new, alpha * l + jnp.sum(p, axis=1, keepdims=True), alpha * acc + _bdot(p, v, _NN)]
            return tuple(out)

        init = (jnp.full((T, 1), NEG, F32), jnp.zeros((T, 1), F32), jnp.zeros((T, LANES), F32)) * 2
        carry = lax.fori_loop(0, i, lambda j, c: tile(j, c, False), init)
        ma, la, acca, mb, lb, accb = tile(i, carry, True)
        o_ref[...] = jnp.where(first, acca / la, accb / lb).astype(o_ref.dtype)
        l_ref[...] = jnp.where(first, ma + jnp.log(la), mb + jnp.log(lb))

    gain = pl.BlockSpec((1, LANES), lambda hp, i: (0, 0))
    return pl.pallas_call(
        body, name="fox_fwd", grid=(nhp, nq),
        in_specs=[pl.BlockSpec((T, LANES), lambda hp, i: (i, qb0 + hp)), pl.BlockSpec((S, LANES), lambda hp, i: (0, kb0 + hp)),
                  pl.BlockSpec((S, LANES), lambda hp, i: (0, vb0 + hp)), gain, gain,
                  pl.BlockSpec((T, 16), lambda hp, i: (i, 0)), pl.BlockSpec((nq, 16, T), lambda hp, i: (0, 0, 0))],
        out_specs=[pl.BlockSpec((T, LANES), lambda hp, i: (i, hp))] * 2,
        out_shape=[SDS((S, ATT_HEADS * ATT_D), BF16), SDS((S, ATT_HEADS * ATT_D), F32)],
        scratch_shapes=[pltpu.VMEM((S, LANES), BF16)],
        compiler_params=_cparams(dimension_semantics=("arbitrary", "arbitrary")),
    )(src, src, src, gq2, gk2, cum_col, cum_row3)


def _fox_bwd(src, q_c0, k_c0, v_c0, gq2, gk2, cum_col, cum_row3, lse, dsrc, d_c0):
    S = src.shape[0]
    T = ATT_T
    nq, nhp = S // T, ATT_HEADS // 2
    qb0, kb0, vb0, db0 = q_c0 // LANES, k_c0 // LANES, v_c0 // LANES, d_c0 // LANES
    scale = ATT_D ** -0.5

    def body(q_ref, kraw_ref, v_ref, gq_ref, gk_ref, cc_ref, cr_ref, l_ref, do_ref,
             dq_ref, dk_ref, dv_ref, dc_ref, dg_ref, k_ref, dk_acc, dv_acc, p_scr, dp_scr):
        hp, i = pl.program_id(0), pl.program_id(1)
        first = _iota((1, LANES), 1) < ATT_D
        tri = _iota((T, T), 0) >= _iota((T, T), 1)

        @pl.when(i == 0)
        def _():
            k_ref[...] = _pair_norm(kraw_ref[...], gk_ref[...], first)[0].astype(BF16)
            dk_acc[...] = jnp.zeros_like(dk_acc)
            dv_acc[...] = jnp.zeros_like(dv_acc)
            dc_ref[...] = jnp.zeros_like(dc_ref)
            dg_ref[...] = jnp.zeros_like(dg_ref)

        q_raw = q_ref[...]
        qn, rq = _pair_norm(q_raw, gq_ref[...], first)
        q = (qn * scale).astype(BF16)
        zq = jnp.zeros_like(q)
        dob = do_ref[...].astype(BF16)
        zd = jnp.zeros_like(dob)
        lse_blk, cc = l_ref[...], cc_ref[...]
        dq = jnp.zeros((T, LANES), F32)
        for hh in range(2):
            sel = first if hh == 0 else jnp.logical_not(first)
            qh, doh = jnp.where(sel, q, zq), jnp.where(sel, dob, zd)
            bias_q = _pick_col(cc, 2 * hp + hh) - jnp.max(jnp.where(sel, lse_blk, NEG), axis=1, keepdims=True)

            def probs(j, delta, diagonal):
                off = pl.multiple_of(j * T, T)
                s = _bdot(qh, k_ref[pl.ds(off, T), :], _NT) + (bias_q - _pick_row(cr_ref[j], 2 * hp + hh))
                if diagonal:
                    s = jnp.where(tri, s, NEG)
                p = jnp.exp(s)
                dp = _bdot(doh, v_ref[pl.ds(off, T), :], _NT)
                p_scr[j] = p
                dp_scr[j] = dp
                return delta + jnp.sum(p * dp, axis=1, keepdims=True)

            delta = lax.fori_loop(0, i, lambda j, d: probs(j, d, False), jnp.zeros((T, 1), F32))
            delta = probs(i, delta, True)

            def grads(j, dq):
                off = pl.multiple_of(j * T, T)
                p = p_scr[j]
                ds = p * (dp_scr[j] - delta)
                dv_acc[pl.ds(off, T), :] += _bdot(p, doh, _TN)
                dk_acc[pl.ds(off, T), :] += _bdot(ds, qh, _TN)
                dc_ref[0, j, hh:hh + 1, :] -= jnp.sum(ds, axis=0, keepdims=True)
                zk = jnp.zeros((T, LANES), BF16)
                return dq + _bdot(ds, jnp.where(sel, k_ref[pl.ds(off, T), :], zk), _NN)

            dq = lax.fori_loop(0, i + 1, grads, dq)
        dq_raw, dgq = _pair_norm_bwd(dq * scale, q_raw, rq, gq_ref[...], first)
        dq_ref[...] = dq_raw.astype(dq_ref.dtype)
        dg_ref[0, 0:1, :] += dgq

        @pl.when(i == nq - 1)
        def _():
            k_raw = kraw_ref[...]
            rk = _pair_norm(k_raw, gk_ref[...], first)[1]
            dk_raw, dgk = _pair_norm_bwd(dk_acc[...], k_raw, rk, gk_ref[...], first)
            dk_ref[...] = dk_raw.astype(dk_ref.dtype)
            dv_ref[...] = dv_acc[...].astype(dv_ref.dtype)
            dg_ref[0, 1:2, :] = dgk

    gain = pl.BlockSpec((1, LANES), lambda hp, i: (0, 0))
    band = SDS((S, ATT_HEADS * ATT_D), BF16)
    return pl.pallas_call(
        body, name="fox_bwd", grid=(nhp, nq),
        in_specs=[pl.BlockSpec((T, LANES), lambda hp, i: (i, qb0 + hp)), pl.BlockSpec((S, LANES), lambda hp, i: (0, kb0 + hp)),
                  pl.BlockSpec((S, LANES), lambda hp, i: (0, vb0 + hp)), gain, gain,
                  pl.BlockSpec((T, 16), lambda hp, i: (i, 0)), pl.BlockSpec((nq, 16, T), lambda hp, i: (0, 0, 0)),
                  pl.BlockSpec((T, LANES), lambda hp, i: (i, hp)), pl.BlockSpec((T, LANES), lambda hp, i: (i, db0 + hp))],
        out_specs=[pl.BlockSpec((T, LANES), lambda hp, i: (i, hp)), pl.BlockSpec((S, LANES), lambda hp, i: (0, hp)),
                   pl.BlockSpec((S, LANES), lambda hp, i: (0, hp)), pl.BlockSpec((1, nq, 8, T), lambda hp, i: (hp, 0, 0, 0)),
                   pl.BlockSpec((1, 8, LANES), lambda hp, i: (hp, 0, 0))],
        out_shape=[band, band, band, SDS((nhp, nq, 8, T), F32), SDS((nhp, 8, LANES), F32)],
        scratch_shapes=[pltpu.VMEM((S, LANES), BF16), pltpu.VMEM((S, LANES), F32), pltpu.VMEM((S, LANES), F32),
                        pltpu.VMEM((nq, T, T), F32), pltpu.VMEM((nq, T, T), F32)],
        compiler_params=_cparams(dimension_semantics=("arbitrary", "arbitrary")),
    )(src, src, src, gq2, gk2, cum_col, cum_row3, lse, dsrc)


def _fold_gains(dg):
    def body(d_ref, o_ref):
        t = d_ref[0]
        for h in range(1, dg.shape[0]):
            t = t + d_ref[h]
        o_ref[...] = t + pltpu.roll(t, ATT_D, axis=1)

    return pl.pallas_call(body, name="fold_gains", out_shape=SDS(dg.shape[1:], F32), compiler_params=_cparams())(dg)


def _loss_head(y, target, *, tm):
    M, W = y.shape

    def body(y_ref, t_ref, dy_ref, loss_ref):
        @pl.when(pl.program_id(0) == 0)
        def _():
            loss_ref[...] = jnp.zeros_like(loss_ref)

        e = y_ref[...] - t_ref[...]
        dy_ref[...] = e * (1.0 / W)
        loss_ref[...] += jnp.sum(jnp.sum(e * e, axis=1, keepdims=True), axis=0, keepdims=True) * (0.5 / W)

    return pl.pallas_call(
        body, name="loss_head", grid=(M // tm,),
        in_specs=[pl.BlockSpec((tm, W), lambda i: (i, 0))] * 2,
        out_specs=[pl.BlockSpec((tm, W), lambda i: (i, 0)), pl.BlockSpec((1, 1), lambda i: (0, 0))],
        out_shape=[SDS((M, W), F32), SDS((1, 1), F32)],
        compiler_params=_cparams(dimension_semantics=("arbitrary",)),
    )(y, target)


def _adamw_math(w, g, m, v):
    m = ADAM_B1 * m + (1.0 - ADAM_B1) * g
    v = ADAM_B2 * v + (1.0 - ADAM_B2) * jnp.square(g)
    m_hat = m / (1.0 - ADAM_B1 ** ADAM_STEP)
    v_hat = v / (1.0 - ADAM_B2 ** ADAM_STEP)
    delta = -ADAM_LR * (m_hat / (jnp.sqrt(v_hat) + ADAM_EPS) + ADAM_WD * w)
    return delta, m, v


def _reduce_adamw(parts, w, m, v, *, tr, name):
    R, C = w.shape
    tr = min(tr, R)

    def body(p_ref, w_ref, m_ref, v_ref, g_ref, d_ref, nm_ref, nv_ref):
        g = p_ref[0].astype(F32)
        for s in range(1, N_DEV):
            g = g + p_ref[s].astype(F32)
        g_ref[...] = g
        d_ref[...], nm_ref[...], nv_ref[...] = _adamw_math(w_ref[...], g, m_ref[...], v_ref[...])

    blk = pl.BlockSpec((tr, C), lambda i: (i, 0))
    return pl.pallas_call(
        body, name=name, grid=(R // tr,),
        in_specs=[pl.BlockSpec((N_DEV, tr, C), lambda i: (0, i, 0)), blk, blk, blk], out_specs=[blk] * 4,
        out_shape=[SDS((R, C), F32)] * 4, compiler_params=_cparams(dimension_semantics=("parallel",)),
    )(parts, w, m, v)


def _adamw(w, g, m, v, *, name):
    def body(w_ref, g_ref, m_ref, v_ref, d_ref, nm_ref, nv_ref):
        d_ref[...], nm_ref[...], nv_ref[...] = _adamw_math(w_ref[...], g_ref[...], m_ref[...], v_ref[...])

    return pl.pallas_call(body, name=name, out_shape=[SDS(w.shape, F32)] * 3, compiler_params=_cparams())(w, g, m, v)


def _peers():
    x, y, c = lax.axis_index("x"), lax.axis_index("y"), lax.axis_index("c")
    out = []
    for k in range(1, N_DEV):
        px, py, pc = x ^ ((k >> 2) & 1), y ^ ((k >> 1) & 1), c ^ (k & 1)
        out.append(((px, py, pc), 4 * px + 2 * py + pc))
    return 4 * x + 2 * y + c, out


_HBM = pl.BlockSpec(memory_space=pltpu.HBM)
_SEM = pl.BlockSpec(memory_space=pltpu.SEMAPHORE)
_DATAFLOW = pltpu.SideEffectType.DATAFLOW_SIDE_EFFECTING


def _exchange_start(arrays, *, scatter, name):
    n, npeer = len(arrays), N_DEV - 1
    lands = [lax.empty(a.shape if scatter else (N_DEV,) + a.shape, a.dtype) for a in arrays]

    def body(*refs):
        ins, lds, sems, token = refs[:n], refs[n:2 * n], refs[2 * n:4 * n], refs[-1]
        me, peers = _peers()
        for a in range(n):
            for k, (dev, idx) in enumerate(peers):
                pltpu.make_async_remote_copy(
                    src_ref=ins[a].at[idx] if scatter else ins[a], dst_ref=lds[a].at[me],
                    send_sem=sems[2 * a].at[k], recv_sem=sems[2 * a + 1].at[k], device_id=dev, device_id_type=MESH).start()
        token[...] = jnp.zeros_like(token)

    res = pl.pallas_call(
        body, name=name,
        out_shape=([pltpu.SemaphoreType.DMA((npeer,))] * (2 * n) + [pltpu.HBM(a.shape, a.dtype) for a in arrays]
                   + [pltpu.HBM(l.shape, l.dtype) for l in lands] + [SDS((8, LANES), F32)]),
        in_specs=[_HBM] * (2 * n), out_specs=[_SEM] * (2 * n) + [_HBM] * (2 * n) + [pl.BlockSpec(memory_space=pltpu.VMEM)],
        input_output_aliases={i: 2 * n + i for i in range(2 * n)},
        compiler_params=pltpu.CompilerParams(has_side_effects=_DATAFLOW),
    )(*[pltpu.with_memory_space_constraint(a, pltpu.HBM) for a in arrays],
      *[pltpu.with_memory_space_constraint(l, pltpu.HBM) for l in lands])
    sems, thru, token = res[:2 * n], res[2 * n:4 * n], res[-1]
    return [(thru[a], thru[n + a], sems[2 * a], sems[2 * a + 1]) for a in range(n)], token


def _exchange_wait(handles, after, *, scatter, name):
    n = len(handles)

    def body(*refs):
        srcs, lds, sems = refs[:n], refs[n:2 * n], refs[2 * n:4 * n]
        me, peers = _peers()
        for a in range(n):
            for k, (dev, idx) in enumerate(peers):
                cp = pltpu.make_async_remote_copy(
                    src_ref=srcs[a].at[idx] if scatter else srcs[a], dst_ref=lds[a].at[idx],
                    send_sem=sems[2 * a].at[k], recv_sem=sems[2 * a + 1].at[k], device_id=dev, device_id_type=MESH)
                cp.wait_send()
                cp.wait_recv()

    flat_sems = [s for h in handles for s in (h[2], h[3])]
    res = pl.pallas_call(
        body, name=name,
        out_shape=[pltpu.HBM(h[0].shape, h[0].dtype) for h in handles] + [pltpu.HBM(h[1].shape, h[1].dtype) for h in handles],
        in_specs=[_HBM] * (2 * n) + [_SEM] * (2 * n) + [pl.BlockSpec(memory_space=pl.ANY)], out_specs=[_HBM] * (2 * n),
        input_output_aliases={i: i for i in range(2 * n)},
        compiler_params=pltpu.CompilerParams(has_side_effects=_DATAFLOW),
    )(*[h[0] for h in handles], *[h[1] for h in handles], *flat_sems, after)
    return [(res[a], res[n + a]) for a in range(n)]


def _own_slot(landed, own, me):
    return lax.dynamic_update_slice(landed, own[None], (me,) + (0,) * own.ndim)


SMALL = (("g_mix", 1024), ("conv_w", 6144), ("conv_b", 1536), ("dt_bias", 16), ("a_log", 16), ("d_skip", 16),
         ("ssm_norm_w", 1024), ("g_q", 64), ("g_k", 64), ("f_bias", 16), ("g_xattn", 1024), ("g_mem", 1024),
         ("xg_q", 256), ("xg_k", 256), ("g_mlp", 1024))
SLAB_ROWS = 112
BIG = ("w_in", "w_out", "xq_w", "xkv_w", "xo_w", "w_up", "w_down")
WEIGHTS = ("g_mix", "w_in", "conv_w", "conv_b", "dt_bias", "a_log", "d_skip", "ssm_norm_w", "g_q", "g_k", "f_bias", "w_out",
           "g_xattn", "g_mem", "xq_w", "xkv_w", "xg_q", "xg_k", "xo_w", "g_mlp", "w_up", "w_down")
O_Z, O_XS, O_B, O_C, O_DT, O_Q, O_K, O_V, O_F, O_END = 0, 1024, 2048, 2304, 2560, 2576, 3600, 4624, 5648, 5664


def _pack_small(vals):
    rows = []
    for name, size in SMALL:
        flat = vals[name].reshape(-1).astype(F32)
        pad = -size % LANES
        rows.append(jnp.pad(flat, (0, pad)).reshape(-1, LANES))
    slab = jnp.concatenate(rows, axis=0)
    return jnp.pad(slab, ((0, SLAB_ROWS - slab.shape[0]), (0, 0)))


def _unpack_small(slab):
    out, r = {}, 0
    for name, size in SMALL:
        nr = -(-size // LANES)
        out[name] = slab[r:r + nr].reshape(-1)[:size]
        r += nr
    return out


def _cols(a, lo, hi):
    return a[:, lo:hi]


def _step(p, m, v, x, mem, target):
    S = x.shape[0]
    TM = 256
    me = 4 * lax.axis_index("x") + 2 * lax.axis_index("y") + lax.axis_index("c")

    def rms(u, g, name):
        return _rw_fwd(_rms_fn, [_whole(u)], [_whole(g)], [(D_MODEL, BF16)], tm=TM, name=name)[0]

    def pin(param, token):
        return param + token[0:1, 0:1]

    def landed_with_own(pairs, scatter):
        out = []
        for src, land in pairs:
            own = lax.dynamic_index_in_dim(src, me, 0, keepdims=False) if scatter else src
            out.append(_own_slot(land, own, me))
        return out

    ag, ag_token = _exchange_start([p["w_in"].astype(BF16), p["conv_w"]] + [p[n].astype(BF16) for n in BIG[1:]],
                                   scatter=False, name="allgather_start")
    h1 = rms(x, pin(p["g_mix"], ag_token), "rms_mix")
    win_g, convw_g = landed_with_own(_exchange_wait(ag[:2], h1, scatter=False, name="allgather_wait_in"), False)
    w_in_o = win_g.transpose(1, 0, 2).reshape(D_MODEL, O_END)
    w_in = jnp.concatenate(
        [_cols(w_in_o, O_Z, O_XS), _cols(w_in_o, O_XS, O_B), _cols(w_in_o, O_Q, O_K), _cols(w_in_o, O_K, O_V),
         _cols(w_in_o, O_V, O_F), _cols(w_in_o, O_B, O_C), _cols(w_in_o, O_C, O_DT), _cols(w_in_o, O_DT, O_Q),
         _cols(w_in_o, O_F, O_END), jnp.zeros((D_MODEL, P_COLS - C_DTF - 32), BF16)], axis=1)
    conv_w = convw_g.transpose(1, 0, 2).reshape(4, 1536)
    cw_xs, cw_bc = conv_w[:, :1024], conv_w[:, 1024:]
    cb_xs, cb_bc = p["conv_b"][:, :1024], p["conv_b"][:, 1024:]
    dt_bias, a_log, f_bias = p["dt_bias"].reshape(16, 1), p["a_log"].reshape(16, 1), p["f_bias"].reshape(16, 1)

    proj = _matmul(h1, w_in, mode="nn", tm=1024, tn=640, tk=1024, name="mm_in")
    xs_c = _conv_fwd(proj, C_XS, 1024, cw_xs, cb_xs, name="conv_xs")
    bc_c = _conv_fwd(proj, C_B, 512, cw_bc, cb_bc, name="conv_bc")
    dtf_t = proj[:, C_DTF:C_DTF + 32].T
    dt_t, acs_t, cum_t = _dtf_fwd(dtf_t, dt_bias, a_log, f_bias)
    dt_col, acs_col, cum_col = dt_t.T, acs_t.T, cum_t.T
    cum_row3 = cum_t.reshape(16, S // ATT_T, ATT_T).transpose(1, 0, 2)
    y_ssd, hs = _ssd_fwd(xs_c, dt_col, acs_col, acs_t, bc_c)
    gate_rows = [_whole(y_ssd), _whole(xs_c), (proj, C_Z, 1024)]
    gate_pars = [_whole(p["d_skip"]), _whole(p["ssm_norm_w"])]
    y_ssm = _rw_fwd(_gate_fn, gate_rows, gate_pars, [(1024, BF16)], tm=TM, name="gate")[0]
    gq2, gk2 = jnp.tile(p["g_q"], (1, 2)), jnp.tile(p["g_k"], (1, 2))
    o, lse = _fox_fwd(proj, C_Q, C_K, C_V, gq2, gk2, cum_col, cum_row3)
    mixed = jnp.concatenate([y_ssm, o], axis=1)
    wout_g, xq_g, xkv_g, xo_g, wup_g, wdown_g = landed_with_own(
        _exchange_wait(ag[2:], mixed, scatter=False, name="allgather_wait_rest"), False)
    w_out = wout_g.reshape(2 * D_MODEL, D_MODEL)
    xq_w = xq_g.reshape(D_MODEL, D_MODEL)
    xkv_w = xkv_g.transpose(1, 0, 2).reshape(D_MODEL, 2 * D_MODEL)
    xo_w = xo_g.reshape(D_MODEL, D_MODEL)
    w_up = wup_g.transpose(1, 0, 2).reshape(D_MODEL, 4 * D_MODEL)
    w_down = wdown_g.reshape(4 * D_MODEL, D_MODEL)
    x1 = _matmul(mixed, w_out, mode="nn", tm=1024, tn=512, tk=2048, add=x, name="mm_out")

    h2 = rms(x1, p["g_xattn"], "rms_xattn")
    mem_n = rms(mem, p["g_mem"], "rms_mem")
    q2 = _matmul(h2, xq_w, mode="nn", tm=1024, tn=512, tk=1024, name="mm_xq")
    kv = _matmul(mem_n, xkv_w, mode="nn", tm=256, tn=1024, tk=1024, name="mm_xkv")
    xa_rows = [(q2, X_D * h, X_D) for h in range(X_HEADS)]
    xa_pars = ([(kv, X_D * h, X_D) for h in range(X_HEADS)] + [(kv, D_MODEL + X_D * h, X_D) for h in range(X_HEADS)]
               + [_whole(p["xg_q"]), _whole(p["xg_k"])])
    o2 = _rw_fwd(_xattn_fn, xa_rows, xa_pars, [(D_MODEL, BF16)], tm=TM, name="xattn")[0]
    x2 = _matmul(o2, xo_w, mode="nn", tm=1024, tn=512, tk=1024, add=x1, name="mm_xo")

    h3 = rms(x2, p["g_mlp"], "rms_mlp")
    a, usq = _matmul(h3, w_up, mode="nn", tm=1024, tn=1024, tk=1024, name="mm_up", out_dtypes=[F32, BF16],
                     epilogue=lambda acc: (acc, jnp.square(jax.nn.relu(acc))))
    x3 = _matmul(usq, w_down, mode="nn", tm=1024, tn=512, tk=2048, add=x2, name="mm_down")
    dy, loss_part = _loss_head(x3, target, tm=TM)
    loss = lax.psum(loss_part[0, 0], ("x", "y", "c"))

    def col_shards(a):
        r, c = a.shape
        return a.reshape(r, N_DEV, c // N_DEV).transpose(1, 0, 2)

    def row_shards(a):
        r, c = a.shape
        return a.reshape(N_DEV, r // N_DEV, c)

    g = {}
    g["w_down"] = _matmul(usq, dy, mode="tn", out_dtype=GRAD_WIRE, tm=1024, tn=1024, tk=1024, name="mm_d_wdown")
    da = _matmul(dy, w_down, mode="nt", tm=1024, tn=1024, tk=1024, name="mm_d_usq", out_dtype=BF16, extras=(a,),
                 epilogue=lambda acc, av: (2.0 * jax.nn.relu(av) * acc,))
    g["w_up"] = _matmul(h3, da, mode="tn", out_dtype=GRAD_WIRE, tm=1024, tn=1024, tk=1024, name="mm_d_wup")
    sent_mlp, token = _exchange_start([row_shards(g["w_down"]), col_shards(g["w_up"])], scatter=True,
                                      name="grads_start_mlp")
    dh3 = _matmul(da, w_up, mode="nt", tm=1024, tn=512, tk=2048, name="mm_d_h3")
    dx2, g["g_mlp"] = _rw_bwd(_rms_fn, [_whole(x2)], [_whole(pin(p["g_mlp"], token))], [_whole(dh3)], tm=TM,
                              name="rms_mlp_bwd", row_grads=[F32], adds={0: _whole(dy)})

    g["xo_w"] = _matmul(o2, dx2, mode="tn", out_dtype=GRAD_WIRE, tm=1024, tn=1024, tk=1024, name="mm_d_wxo")
    do2 = _matmul(dx2, xo_w, mode="nt", tm=1024, tn=512, tk=1024, name="mm_d_o2")
    xa = _rw_bwd(_xattn_fn, xa_rows, xa_pars, [_whole(do2)], tm=TM, name="xattn_bwd", row_grads=[BF16] * X_HEADS)
    dq2 = jnp.concatenate(xa[:X_HEADS], axis=1)
    dkv = jnp.concatenate(xa[X_HEADS:3 * X_HEADS], axis=1)
    g["xg_q"], g["xg_k"] = xa[3 * X_HEADS], xa[3 * X_HEADS + 1]
    g["xq_w"] = _matmul(h2, dq2, mode="tn", out_dtype=GRAD_WIRE, tm=1024, tn=1024, tk=1024, name="mm_d_wxq")
    dh2 = _matmul(dq2, xq_w, mode="nt", tm=1024, tn=512, tk=1024, name="mm_d_h2")
    g["xkv_w"] = _matmul(mem_n, dkv, mode="tn", out_dtype=GRAD_WIRE, tm=1024, tn=1024, tk=256, name="mm_d_wxkv")
    dmem_n = _matmul(dkv, xkv_w, mode="nt", tm=256, tn=1024, tk=2048, name="mm_d_memn")
    g["g_mem"] = _rw_bwd(_rms_fn, [_whole(mem)], [_whole(p["g_mem"])], [_whole(dmem_n)], tm=TM, name="rms_mem_bwd",
                         row_grads=[None])[0]
    dx1, g["g_xattn"] = _rw_bwd(_rms_fn, [_whole(x1)], [_whole(p["g_xattn"])], [_whole(dh2)], tm=TM, name="rms_xattn_bwd",
                                row_grads=[F32], adds={0: _whole(dx2)})

    g["w_out"] = _matmul(mixed, dx1, mode="tn", out_dtype=GRAD_WIRE, tm=1024, tn=1024, tk=1024, name="mm_d_wout")
    sent_mid, token = _exchange_start(
        [row_shards(g["w_out"]), row_shards(g["xq_w"]), col_shards(g["xkv_w"]), row_shards(g["xo_w"])], scatter=True,
        name="grads_start_mid")
    dmixed = _matmul(dx1, w_out, mode="nt", tm=1024, tn=1024, tk=1024, name="mm_d_mixed")
    dq, dk, dv, dcum4, dgain = _fox_bwd(proj, C_Q, C_K, C_V, pin(gq2, token), gk2, cum_col, cum_row3, lse, dmixed, 1024)
    gains = _fold_gains(dgain)
    g["g_q"], g["g_k"] = gains[0:1, :ATT_D], gains[1:2, :ATT_D]
    dy_ssd, dxs_g, dz, g["d_skip"], g["ssm_norm_w"] = _rw_bwd(
        _gate_fn, gate_rows, gate_pars, [(dmixed, 0, 1024)], tm=TM, name="gate_bwd", row_grads=[F32, F32, BF16])
    dxs_s, ddt_col, dacs_col, dacs_row, d_b, d_c = _ssd_bwd(xs_c, dt_col, acs_col, acs_t, bc_c, hs, dy_ssd)
    dcum_t = dcum4[:, :, 0:2, :].transpose(0, 2, 1, 3).reshape(16, S)
    ddtf_t, ddtb, dalog, dfb = _dtf_bwd(dtf_t, dt_bias, a_log, f_bias, ddt_col.T, dacs_col.T, dacs_row, dcum_t)
    g["dt_bias"], g["a_log"], g["f_bias"] = ddtb, dalog, dfb
    dxs_raw, dcw_xs, dcb_xs = _conv_bwd(proj, C_XS, 1024, cw_xs, cb_xs, [dxs_s, dxs_g], name="conv_xs_bwd")
    dbc_raw, dcw_bc, dcb_bc = _conv_bwd(proj, C_B, 512, cw_bc, cb_bc, [jnp.concatenate([d_b, d_c], axis=1)],
                                        name="conv_bc_bwd")
    g["conv_w"] = jnp.concatenate([dcw_xs, dcw_bc], axis=1)
    g["conv_b"] = jnp.concatenate([dcb_xs, dcb_bc], axis=1)
    ddtf = jnp.pad(ddtf_t.T.astype(BF16), ((0, 0), (0, P_COLS - C_DTF - 32)))
    dproj = jnp.concatenate([dz, dxs_raw, dq, dk, dv, dbc_raw, ddtf], axis=1)
    dw_in_p = _matmul(h1, dproj, mode="tn", out_dtype=GRAD_WIRE, tm=1024, tn=640, tk=1024, name="mm_d_win")
    g["w_in"] = jnp.concatenate(
        [_cols(dw_in_p, C_Z, C_Q), _cols(dw_in_p, C_B, C_DTF + 16), _cols(dw_in_p, C_Q, C_B),
         _cols(dw_in_p, C_DTF + 16, C_DTF + 32)], axis=1)
    sent_in, token = _exchange_start([col_shards(g["w_in"])], scatter=True, name="grads_start_in")
    dh1 = _matmul(dproj, w_in, mode="nt", tm=1024, tn=512, tk=1920, name="mm_d_h1")
    grad_x, g["g_mix"] = _rw_bwd(_rms_fn, [_whole(x)], [_whole(pin(p["g_mix"], token))], [_whole(dh1)], tm=TM,
                                 name="rms_mix_bwd", row_grads=[F32], adds={0: _whole(dx1)})
    sent_small, _ = _exchange_start([_pack_small(g)], scatter=False, name="small_grads_start")

    grads, delta, new_m, new_v = {}, {}, {}, {}

    def update(names, sent, after, wait_name):
        parts = landed_with_own(_exchange_wait(sent, after, scatter=True, name=wait_name), True)
        for name, part in zip(names, parts, strict=True):
            grads[name], delta[name], new_m[name], new_v[name] = _reduce_adamw(part, p[name], m[name], v[name], tr=128,
                                                                                name="adamw_" + name)

    update(("w_down", "w_up"), sent_mlp, grad_x, "grads_wait_mlp")
    update(("w_out", "xq_w", "xkv_w", "xo_w"), sent_mid, delta["w_up"], "grads_wait_mid")
    update(("w_in",), sent_in, delta["xo_w"], "grads_wait_in")
    small_parts = landed_with_own(_exchange_wait(sent_small, delta["w_in"], scatter=False, name="small_grads_wait"), False)[0]
    zeros_cw = jnp.zeros((4, 1536), F32)
    slabs = [_pack_small({**d, "conv_w": zeros_cw}) for d in (p, m, v)]
    sg, sd, sm, sv = _reduce_adamw(small_parts, *slabs, tr=SLAB_ROWS, name="adamw_small")
    for dst, slab in ((grads, sg), (delta, sd), (new_m, sm), (new_v, sv)):
        for name, flat in _unpack_small(slab).items():
            if name != "conv_w":
                dst[name] = flat.reshape(p[name].shape)
    cw_shard = p["conv_w"].shape[1]
    grads["conv_w"] = lax.dynamic_slice(_unpack_small(sg)["conv_w"].reshape(4, 1536), (0, me * cw_shard), (4, cw_shard))
    delta["conv_w"], new_m["conv_w"], new_v["conv_w"] = _adamw(p["conv_w"], grads["conv_w"], m["conv_w"], v["conv_w"],
                                                               name="adamw_conv_w")
    return loss, grad_x, grads, delta, new_m, new_v


def kernel(x, mem, g_mix, w_in, conv_w, conv_b, dt_bias, a_log, d_skip, ssm_norm_w, g_q, g_k, f_bias, w_out, g_xattn, g_mem, xq_w, xkv_w, xg_q, xg_k, xo_w, g_mlp, w_up, w_down, loss_target, m_g_mix, m_w_in, m_conv_w, m_conv_b, m_dt_bias, m_a_log, m_d_skip, m_ssm_norm_w, m_g_q, m_g_k, m_f_bias, m_w_out, m_g_xattn, m_g_mem, m_xq_w, m_xkv_w, m_xg_q, m_xg_k, m_xo_w, m_g_mlp, m_w_up, m_w_down, v_g_mix, v_w_in, v_conv_w, v_conv_b, v_dt_bias, v_a_log, v_d_skip, v_ssm_norm_w, v_g_q, v_g_k, v_f_bias, v_w_out, v_g_xattn, v_g_mem, v_xq_w, v_xkv_w, v_xg_q, v_xg_k, v_xo_w, v_g_mlp, v_w_up, v_w_down):
    args = locals()
    drop = lambda t: t[0] if t.ndim == 3 else t
    p = {n: drop(args[n]) for n in WEIGHTS}
    m = {n: drop(args["m_" + n]) for n in WEIGHTS}
    v = {n: drop(args["v_" + n]) for n in WEIGHTS}
    loss, grad_x, grads, delta, new_m, new_v = _step(p, m, v, x[0], mem[0], loss_target[0])
    outs = [loss, grad_x[None]]
    for d in (grads, delta, new_m, new_v):
        outs += [d[n].reshape(args[n].shape) for n in WEIGHTS]
    return tuple(outs)
```

```python
import functools
import math

import jax
import jax.numpy as jnp
from jax import lax
from jax.experimental import pallas as pl
from jax.experimental.pallas import tpu as pltpu

F32, BF16 = jnp.float32, jnp.bfloat16
SDS = jax.ShapeDtypeStruct
HI = lax.Precision.HIGHEST
MESH = pl.DeviceIdType.MESH

N_DEV = 8
EPS = 1e-5
D_MODEL = 1024
SSM_HEADS, SSM_P, SSM_N, SSM_GROUPS, CHUNK = 16, 64, 128, 2, 128
ATT_HEADS, ATT_D = 16, 64
X_HEADS, X_D = 4, 256
LANES = 128
VMEM_LIMIT = 48 * 1024 * 1024
NEG = -1e30

GRAD_WIRE = BF16
ADAM_LR, ADAM_B1, ADAM_B2, ADAM_EPS, ADAM_WD, ADAM_STEP = 0.001, 0.9, 0.999, 1e-08, 0.01, 10

C_Z, C_XS, C_Q, C_K, C_V, C_B, C_C, C_DTF, P_COLS = 0, 1024, 2048, 3072, 4096, 5120, 5376, 5632, 5760

_NN = (((1,), (0,)), ((), ()))
_NT = (((1,), (1,)), ((), ()))
_TN = (((0,), (0,)), ((), ()))


def _cparams(**kw):
    return pltpu.CompilerParams(vmem_limit_bytes=VMEM_LIMIT, **kw)


def _bdot(a, b, dn):
    return lax.dot_general(a.astype(BF16), b.astype(BF16), dn, preferred_element_type=F32)


@jax.custom_vjp
def mm_nn(a, b):
    return _bdot(a, b, _NN)


mm_nn.defvjp(lambda a, b: (mm_nn(a, b), (a, b)), lambda r, g: (_bdot(g, r[1], _NT), _bdot(r[0], g, _TN)))


@jax.custom_vjp
def mm_nt(a, b):
    return _bdot(a, b, _NT)


mm_nt.defvjp(lambda a, b: (mm_nt(a, b), (a, b)), lambda r, g: (_bdot(g, r[1], _NN), _bdot(g, r[0], _TN)))


@jax.custom_vjp
def mm_tn(a, b):
    return _bdot(a, b, _TN)


mm_tn.defvjp(lambda a, b: (mm_tn(a, b), (a, b)), lambda r, g: (_bdot(r[1], g, _NT), _bdot(r[0], g, _NN)))


def _cdot(x, c):
    return jnp.dot(x, c, precision=HI, preferred_element_type=F32)


def _iota(shape, dim):
    return lax.broadcasted_iota(jnp.int32, shape, dim)


def _matmul(a, b, *, mode, tm, tn, tk, name, out_dtype=F32, add=None, extras=(), epilogue=None, out_dtypes=None):
    if mode == "tn":
        K, M = a.shape
    else:
        M, K = a.shape
    N = b.shape[0] if mode == "nt" else b.shape[1]
    tm, tn, tk = min(tm, M), min(tn, N), min(tk, K)
    assert M % tm == 0 and N % tn == 0 and K % tk == 0, (name, M, N, K, tm, tn, tk)
    nk = K // tk
    dn = {"nn": _NN, "nt": _NT, "tn": _TN}[mode]
    if add is not None:
        extras, epilogue = (add,), lambda acc, r: (acc + r,)
    elif epilogue is None:
        epilogue = lambda acc: (acc,)
    out_dtypes = out_dtypes or [out_dtype]
    ne, no = len(extras), len(out_dtypes)

    def body(*refs):
        a_ref, b_ref = refs[:2]
        e_refs, o_refs = refs[2:2 + ne], refs[2 + ne:2 + ne + no]

        def finish(acc):
            res = epilogue(acc, *[e[...] for e in e_refs])
            for o_ref, v in zip(o_refs, res, strict=True):
                o_ref[...] = v.astype(o_ref.dtype)

        prod = _bdot(a_ref[...], b_ref[...], dn)
        if nk == 1:
            finish(prod)
            return
        acc_ref = refs[-1]
        k = pl.program_id(2)

        @pl.when(k == 0)
        def _():
            acc_ref[...] = prod

        @pl.when(jnp.logical_and(k > 0, k < nk - 1))
        def _():
            acc_ref[...] += prod

        @pl.when(k == nk - 1)
        def _():
            finish(acc_ref[...] + prod)

    a_spec = pl.BlockSpec((tk, tm), lambda i, j, k: (k, i)) if mode == "tn" else pl.BlockSpec((tm, tk), lambda i, j, k: (i, k))
    b_spec = pl.BlockSpec((tn, tk), lambda i, j, k: (j, k)) if mode == "nt" else pl.BlockSpec((tk, tn), lambda i, j, k: (k, j))
    o_spec = pl.BlockSpec((tm, tn), lambda i, j, k: (i, j))
    res = pl.pallas_call(
        body, name=name, grid=(M // tm, N // tn, nk), in_specs=[a_spec, b_spec] + [o_spec] * ne, out_specs=[o_spec] * no,
        out_shape=[SDS((M, N), dt) for dt in out_dtypes], scratch_shapes=[pltpu.VMEM((tm, tn), F32)] if nk > 1 else [],
        compiler_params=_cparams(dimension_semantics=("parallel", "parallel", "arbitrary")),
    )(a, b, *extras)
    return res[0] if no == 1 else res


def _row_spec(tm, spec):
    _, c0, w = spec
    assert c0 % w == 0
    return pl.BlockSpec((tm, w), functools.partial(lambda i, cb: (i, cb), cb=c0 // w))


def _par_spec(spec):
    arr, c0, w = spec
    assert c0 % w == 0
    return pl.BlockSpec((arr.shape[0], w), functools.partial(lambda i, cb: (0, cb), cb=c0 // w))


def _whole(arr):
    return (arr, 0, arr.shape[1])


def _rw_fwd(fn, rows, params, outs, *, tm, name):
    M = rows[0][0].shape[0]
    nr, npar = len(rows), len(params)

    def body(*refs):
        rv = [r[...].astype(F32) for r in refs[:nr]]
        pv = [p[...].astype(F32) for p in refs[nr:nr + npar]]
        res = fn(*rv, *pv)
        for o_ref, v in zip(refs[nr + npar:], res, strict=True):
            o_ref[...] = v.astype(o_ref.dtype)

    return pl.pallas_call(
        body, name=name, grid=(M // tm,),
        in_specs=[_row_spec(tm, r) for r in rows] + [_par_spec(p) for p in params],
        out_specs=[pl.BlockSpec((tm, w), lambda i: (i, 0)) for w, _ in outs],
        out_shape=[SDS((M, w), dt) for w, dt in outs],
        compiler_params=_cparams(dimension_semantics=("parallel",)),
    )(*[r[0] for r in rows], *[p[0] for p in params])


def _rw_bwd(fn, rows, params, cts, *, tm, name, row_grads, adds=None):
    M = rows[0][0].shape[0]
    adds = adds or {}
    nr, npar, nc = len(rows), len(params), len(cts)
    add_keys = sorted(adds)
    want = [k for k in range(nr) if row_grads[k] is not None]

    def body(*refs):
        pos = 0
        r_refs = refs[pos:pos + nr]; pos += nr
        p_refs = refs[pos:pos + npar]; pos += npar
        c_refs = refs[pos:pos + nc]; pos += nc
        a_refs = dict(zip(add_keys, refs[pos:pos + len(add_keys)])); pos += len(add_keys)
        dr_refs = dict(zip(want, refs[pos:pos + len(want)])); pos += len(want)
        dp_refs = refs[pos:pos + npar]
        rv = [r[...].astype(F32) for r in r_refs]
        pv = [p[...].astype(F32) for p in p_refs]
        _, vjp = jax.vjp(fn, *rv, *pv)
        g = vjp(tuple(c[...].astype(F32) for c in c_refs))
        for k in want:
            v = g[k]
            if k in a_refs:
                v = v + a_refs[k][...].astype(F32)
            dr_refs[k][...] = v.astype(dr_refs[k].dtype)
        first = pl.program_id(0) == 0
        for j in range(npar):
            @pl.when(first)
            def _(j=j):
                dp_refs[j][...] = jnp.zeros_like(dp_refs[j])
            dp_refs[j][...] += g[nr + j]

    res = pl.pallas_call(
        body, name=name, grid=(M // tm,),
        in_specs=([_row_spec(tm, r) for r in rows] + [_par_spec(p) for p in params] + [_row_spec(tm, c) for c in cts]
                  + [_row_spec(tm, adds[k]) for k in add_keys]),
        out_specs=([pl.BlockSpec((tm, rows[k][2]), lambda i: (i, 0)) for k in want]
                   + [pl.BlockSpec((p[0].shape[0], p[2]), lambda i: (0, 0)) for p in params]),
        out_shape=([SDS((M, rows[k][2]), row_grads[k]) for k in want] + [SDS((p[0].shape[0], p[2]), F32) for p in params]),
        compiler_params=_cparams(dimension_semantics=("arbitrary",)),
    )(*[r[0] for r in rows], *[p[0] for p in params], *[c[0] for c in cts], *[adds[k][0] for k in add_keys])
    return res


def _rms_fn(x, g):
    r = lax.rsqrt(jnp.mean(x * x, axis=-1, keepdims=True) + EPS)
    return (x * r * g,)


def _seg_mats(width, seg):
    n = width // seg
    p = (_iota((width, n), 0) // seg == _iota((width, n), 1)).astype(F32)
    e = (_iota((n, width), 1) // seg == _iota((n, width), 0)).astype(F32)
    return p, e


def _gate_fn(y, xs, z, dskip, w):
    width = SSM_HEADS * SSM_P
    _, e = _seg_mats(width, SSM_P)
    y = (y + _cdot(dskip, e) * xs) * (z * jax.nn.sigmoid(z))
    g0 = _iota((1, width), 1) < width // SSM_GROUPS
    y2 = y * y
    gw = width // SSM_GROUPS
    ms0 = jnp.sum(jnp.where(g0, y2, 0.0), axis=-1, keepdims=True) * (1.0 / gw)
    ms1 = jnp.sum(jnp.where(g0, 0.0, y2), axis=-1, keepdims=True) * (1.0 / gw)
    r = jnp.where(g0, lax.rsqrt(ms0 + EPS), lax.rsqrt(ms1 + EPS))
    return (y * r * w,)


def _xattn_fn(q0, q1, q2, q3, k0, k1, k2, k3, v0, v1, v2, v3, gq, gk):
    def norm(u, g):
        return u * lax.rsqrt(jnp.mean(u * u, axis=-1, keepdims=True) + EPS) * g
    outs = []
    for q, k, v in ((q0, k0, v0), (q1, k1, v1), (q2, k2, v2), (q3, k3, v3)):
        s = mm_nt(norm(q, gq), norm(k, gk)) * (X_D ** -0.5)
        p = jnp.exp(s - lax.stop_gradient(jnp.max(s, axis=-1, keepdims=True)))
        p = p / jnp.sum(p, axis=-1, keepdims=True)
        outs.append(mm_nn(p, v))
    return (jnp.concatenate(outs, axis=-1),)


CONV_TC = 256


def _shift_down(u, k):
    if k == 0:
        return u
    return jnp.where(_iota(u.shape, 0) >= k, pltpu.roll(u, k, axis=0), 0.0)


def _shift_up(u, k):
    if k == 0:
        return u
    n = u.shape[0]
    return jnp.where(_iota(u.shape, 0) < n - k, pltpu.roll(u, n - k, axis=0), 0.0)


def _conv_pre(u, w_ref, b):
    pre = b + w_ref[3:4, :] * u
    for k in (1, 2, 3):
        pre = pre + w_ref[3 - k:4 - k, :] * _shift_down(u, k)
    return pre


def _conv_fwd(src, c0, width, w, b, *, name):
    S = src.shape[0]
    cb0 = c0 // CONV_TC

    def body(u_ref, w_ref, b_ref, o_ref):
        pre = _conv_pre(u_ref[...], w_ref, b_ref[...])
        o_ref[...] = pre * jax.nn.sigmoid(pre)

    return pl.pallas_call(
        body, name=name, grid=(width // CONV_TC,),
        in_specs=[pl.BlockSpec((S, CONV_TC), lambda j: (0, cb0 + j)), pl.BlockSpec((4, CONV_TC), lambda j: (0, j)),
                  pl.BlockSpec((1, CONV_TC), lambda j: (0, j))],
        out_specs=pl.BlockSpec((S, CONV_TC), lambda j: (0, j)), out_shape=SDS((S, width), F32),
        compiler_params=_cparams(dimension_semantics=("parallel",)),
    )(src, w, b)


def _conv_bwd(src, c0, width, w, b, douts, *, name):
    S = src.shape[0]
    cb0 = c0 // CONV_TC
    nd = len(douts)

    def body(*refs):
        u_ref, w_ref, b_ref = refs[:3]
        d_refs = refs[3:3 + nd]
        du_ref, dw_ref, db_ref = refs[3 + nd:]
        u = u_ref[...]
        pre = _conv_pre(u, w_ref, b_ref[...])
        sg = jax.nn.sigmoid(pre)
        dout = d_refs[0][...]
        for r in d_refs[1:]:
            dout = dout + r[...]
        dpre = dout * (sg * (1.0 + pre * (1.0 - sg)))
        du = w_ref[3:4, :] * dpre
        dw_ref[3:4, :] = jnp.sum(dpre * u, axis=0, keepdims=True)
        for k in (1, 2, 3):
            du = du + w_ref[3 - k:4 - k, :] * _shift_up(dpre, k)
            dw_ref[3 - k:4 - k, :] = jnp.sum(dpre * _shift_down(u, k), axis=0, keepdims=True)
        du_ref[...] = du.astype(du_ref.dtype)
        db_ref[...] = jnp.sum(dpre, axis=0, keepdims=True)

    return pl.pallas_call(
        body, name=name, grid=(width // CONV_TC,),
        in_specs=[pl.BlockSpec((S, CONV_TC), lambda j: (0, cb0 + j)), pl.BlockSpec((4, CONV_TC), lambda j: (0, j)),
                  pl.BlockSpec((1, CONV_TC), lambda j: (0, j))] + [pl.BlockSpec((S, CONV_TC), lambda j: (0, j))] * nd,
        out_specs=[pl.BlockSpec((S, CONV_TC), lambda j: (0, j)), pl.BlockSpec((4, CONV_TC), lambda j: (0, j)),
                   pl.BlockSpec((1, CONV_TC), lambda j: (0, j))],
        out_shape=[SDS((S, width), BF16), SDS((4, width), F32), SDS((1, width), F32)],
        compiler_params=_cparams(dimension_semantics=("parallel",)),
    )(src, w, b, *douts)


def _softplus(x):
    return jnp.maximum(x, 0.0) + jnp.log(1.0 + jnp.exp(-jnp.abs(x)))


def _prefix_sum(x, seg):
    n = x.shape[1]
    pos = _iota(x.shape, 1) % seg
    k = 1
    while k < seg:
        x = x + jnp.where(pos >= k, pltpu.roll(x, k, axis=1), 0.0)
        k *= 2
    return x


def _suffix_sum(x, seg):
    n = x.shape[1]
    pos = _iota(x.shape, 1) % seg
    k = 1
    while k < seg:
        x = x + jnp.where(pos + k < seg, pltpu.roll(x, n - k, axis=1), 0.0)
        k *= 2
    return x


def _dtf_fwd(dtf_t, dt_bias, a_log, f_bias):
    S = dtf_t.shape[1]

    def body(x_ref, db_ref, al_ref, fb_ref, dt_ref, acs_ref, cum_ref):
        dt = _softplus(x_ref[0:16, :] + db_ref[...])
        dt_ref[...] = dt
        acs_ref[...] = _prefix_sum(dt * (-jnp.exp(al_ref[...])), CHUNK)
        cum_ref[...] = _prefix_sum(-_softplus(-(x_ref[16:32, :] + fb_ref[...])), S)

    return pl.pallas_call(body, name="dtf_fwd", out_shape=[SDS((16, S), F32)] * 3, compiler_params=_cparams())(
        dtf_t, dt_bias, a_log, f_bias)


def _dtf_bwd(dtf_t, dt_bias, a_log, f_bias, d_dt, d_acs_a, d_acs_b, d_cum):
    S = dtf_t.shape[1]

    def body(x_ref, db_ref, al_ref, fb_ref, ddt_ref, da1_ref, da2_ref, dc_ref, dx_ref, ddb_ref, dal_ref, dfb_ref):
        xd = x_ref[0:16, :] + db_ref[...]
        dt = _softplus(xd)
        a = -jnp.exp(al_ref[...])
        d_da = _suffix_sum(da1_ref[...] + da2_ref[...], CHUNK)
        d_dt = ddt_ref[...] + d_da * a
        dal_ref[...] = jnp.sum(d_da * dt, axis=1, keepdims=True) * a
        d_xd = d_dt * jax.nn.sigmoid(xd)
        ddb_ref[...] = jnp.sum(d_xd, axis=1, keepdims=True)
        xf = x_ref[16:32, :] + fb_ref[...]
        d_xf = _suffix_sum(dc_ref[...], S) * jax.nn.sigmoid(-xf)
        dfb_ref[...] = jnp.sum(d_xf, axis=1, keepdims=True)
        dx_ref[0:16, :] = d_xd
        dx_ref[16:32, :] = d_xf

    return pl.pallas_call(body, name="dtf_bwd", out_shape=[SDS((32, S), F32)] + [SDS((16, 1), F32)] * 3,
                          compiler_params=_cparams())(dtf_t, dt_bias, a_log, f_bias, d_dt, d_acs_a, d_acs_b, d_cum)


SSM_PAIRS = SSM_HEADS // 2 // SSM_GROUPS


def _ssd_pair(xs, dtc, acol, arow, bm, cm, cbm, h, hp):
    L = CHUNK
    first = _iota((1, LANES), 1) < SSM_P
    i16, s16 = _iota((L, 16), 1), _iota((16, L), 0)
    ha, hb = 2 * hp, 2 * hp + 1

    def selc(blk, hh):
        return jnp.sum(jnp.where(i16 == hh, blk, 0.0), axis=1, keepdims=True)

    def selr(blk, hh):
        return jnp.sum(jnp.where(s16 == hh, blk, 0.0), axis=0, keepdims=True)

    x = xs * jnp.where(first, selc(dtc, ha), selc(dtc, hb))
    ca, cb, ra, rb = selc(acol, ha), selc(acol, hb), selr(arow, ha), selr(arow, hb)
    tri = _iota((L, L), 0) >= _iota((L, L), 1)
    la = jnp.exp(jnp.where(tri, ca - ra, NEG))
    lb = jnp.exp(jnp.where(tri, cb - rb, NEG))
    y = jnp.where(first, mm_nn(cbm * la, x), mm_nn(cbm * lb, x))
    y = y + jnp.where(first, jnp.exp(ca), jnp.exp(cb)) * mm_nn(cm, h)
    last = _iota((1, L), 1) == L - 1
    ala = jnp.sum(jnp.where(last, ra, 0.0), axis=1, keepdims=True)
    alb = jnp.sum(jnp.where(last, rb, 0.0), axis=1, keepdims=True)
    dec = jnp.where(first, jnp.exp(ala - ca), jnp.exp(alb - cb))
    hn = jnp.where(first, jnp.exp(ala), jnp.exp(alb)) * h + mm_tn(bm, x * dec)
    return y, hn


def _ssd_group(*args, grp):
    xs, (dtc, acol, arow, bm, cm), hs = args[:SSM_PAIRS], args[SSM_PAIRS:SSM_PAIRS + 5], args[SSM_PAIRS + 5:]
    cbm = mm_nt(cm, bm)
    res = [_ssd_pair(xs[j], dtc, acol, arow, bm, cm, cbm, hs[j], SSM_PAIRS * grp + j) for j in range(SSM_PAIRS)]
    return tuple(r[0] for r in res) + tuple(r[1] for r in res)


def _ssd_specs(nc, rev):
    L = CHUNK
    cidx = (lambda c: nc - 1 - c) if rev else (lambda c: c)
    return dict(
        xs=pl.BlockSpec((L, SSM_PAIRS * LANES), lambda c, g: (cidx(c), g)),
        col=pl.BlockSpec((L, 16), lambda c, g: (cidx(c), 0)),
        row=pl.BlockSpec((16, L), lambda c, g: (0, cidx(c))),
        b=pl.BlockSpec((L, SSM_N), lambda c, g: (cidx(c), g)),
        c=pl.BlockSpec((L, SSM_N), lambda c, g: (cidx(c), SSM_GROUPS + g)),
        st=pl.BlockSpec((1, SSM_PAIRS, SSM_N, LANES), lambda c, g: (cidx(c), g, 0, 0)),
    )


def _lane_pieces(v):
    return [v[:, LANES * j:LANES * (j + 1)] for j in range(v.shape[1] // LANES)]


def _ssd_fwd(xs, dt_col, acs_col, acs_row, bc):
    S = xs.shape[0]
    nc, nhp = S // CHUNK, SSM_HEADS // 2
    sp = _ssd_specs(nc, False)

    def body(xs_ref, dt_ref, ac_ref, ar_ref, b_ref, c_ref, y_ref, hs_ref, h_scr):
        c, g = pl.program_id(0), pl.program_id(1)

        @pl.when(c == 0)
        def _():
            for j in range(SSM_PAIRS):
                h_scr[SSM_PAIRS * g + j] = jnp.zeros((SSM_N, LANES), F32)

        hs = [h_scr[SSM_PAIRS * g + j] for j in range(SSM_PAIRS)]
        for j in range(SSM_PAIRS):
            hs_ref[0, j] = hs[j]
        res = _ssd_group(*_lane_pieces(xs_ref[...]), dt_ref[...], ac_ref[...], ar_ref[...], b_ref[...], c_ref[...], *hs,
                         grp=g)
        y_ref[...] = jnp.concatenate(res[:SSM_PAIRS], axis=1)
        for j in range(SSM_PAIRS):
            h_scr[SSM_PAIRS * g + j] = res[SSM_PAIRS + j]

    return pl.pallas_call(
        body, name="ssd_fwd", grid=(nc, SSM_GROUPS),
        in_specs=[sp["xs"], sp["col"], sp["col"], sp["row"], sp["b"], sp["c"]],
        out_specs=[sp["xs"], sp["st"]],
        out_shape=[SDS((S, SSM_HEADS * SSM_P), F32), SDS((nc, nhp, SSM_N, LANES), F32)],
        scratch_shapes=[pltpu.VMEM((nhp, SSM_N, LANES), F32)],
        compiler_params=_cparams(dimension_semantics=("arbitrary", "arbitrary")),
    )(xs, dt_col, acs_col, acs_row, bc, bc)


def _ssd_bwd(xs, dt_col, acs_col, acs_row, bc, hs, dy):
    S = xs.shape[0]
    nc, nhp = S // CHUNK, SSM_HEADS // 2
    sp = _ssd_specs(nc, True)

    def body(xs_ref, dt_ref, ac_ref, ar_ref, b_ref, c_ref, hs_ref, dy_ref,
             dxs_ref, ddt_ref, dac_ref, dar_ref, db_ref, dc_ref, dh_scr):
        c, g = pl.program_id(0), pl.program_id(1)

        @pl.when(c == 0)
        def _():
            for j in range(SSM_PAIRS):
                dh_scr[SSM_PAIRS * g + j] = jnp.zeros((SSM_N, LANES), F32)

        _, vjp = jax.vjp(functools.partial(_ssd_group, grp=g), *_lane_pieces(xs_ref[...]), dt_ref[...], ac_ref[...],
                         ar_ref[...], b_ref[...], c_ref[...], *[hs_ref[0, j] for j in range(SSM_PAIRS)])
        grads = vjp(tuple(_lane_pieces(dy_ref[...])) + tuple(dh_scr[SSM_PAIRS * g + j] for j in range(SSM_PAIRS)))
        dxs_ref[...] = jnp.concatenate(grads[:SSM_PAIRS], axis=1)
        ddt, dac, dar, db, dc = grads[SSM_PAIRS:SSM_PAIRS + 5]
        for j in range(SSM_PAIRS):
            dh_scr[SSM_PAIRS * g + j] = grads[SSM_PAIRS + 5 + j]
        db_ref[...] = db
        dc_ref[...] = dc

        @pl.when(g == 0)
        def _():
            ddt_ref[...] = ddt
            dac_ref[...] = dac
            dar_ref[...] = dar

        @pl.when(g > 0)
        def _():
            ddt_ref[...] += ddt
            dac_ref[...] += dac
            dar_ref[...] += dar

    return pl.pallas_call(
        body, name="ssd_bwd", grid=(nc, SSM_GROUPS),
        in_specs=[sp["xs"], sp["col"], sp["col"], sp["row"], sp["b"], sp["c"], sp["st"], sp["xs"]],
        out_specs=[sp["xs"], sp["col"], sp["col"], sp["row"], sp["b"], sp["b"]],
        out_shape=[SDS((S, SSM_HEADS * SSM_P), F32), SDS((S, 16), F32), SDS((S, 16), F32), SDS((16, S), F32),
                   SDS((S, SSM_GROUPS * SSM_N), F32), SDS((S, SSM_GROUPS * SSM_N), F32)],
        scratch_shapes=[pltpu.VMEM((nhp, SSM_N, LANES), F32)],
        compiler_params=_cparams(dimension_semantics=("arbitrary", "arbitrary")),
    )(xs, dt_col, acs_col, acs_row, bc, bc, hs, dy)


ATT_T = 512


def _pick_col(blk, h):
    return jnp.sum(jnp.where(_iota(blk.shape, 1) == h, blk, 0.0), axis=1, keepdims=True)


def _pick_row(blk, h):
    return jnp.sum(jnp.where(_iota(blk.shape, 0) == h, blk, 0.0), axis=0, keepdims=True)


def _pair_norm(x, g2, first):
    x2 = x * x
    sa = jnp.sum(jnp.where(first, x2, 0.0), axis=1, keepdims=True)
    sb = jnp.sum(jnp.where(first, 0.0, x2), axis=1, keepdims=True)
    r = jnp.where(first, lax.rsqrt(sa * (1.0 / ATT_D) + EPS), lax.rsqrt(sb * (1.0 / ATT_D) + EPS))
    return x * r * g2, r


def _pair_norm_bwd(dxn, x, r, g2, first):
    t = dxn * g2
    tx = t * x
    ma = jnp.sum(jnp.where(first, tx, 0.0), axis=1, keepdims=True)
    mb = jnp.sum(jnp.where(first, 0.0, tx), axis=1, keepdims=True)
    dx = r * (t - x * (r * r) * (jnp.where(first, ma, mb) * (1.0 / ATT_D)))
    return dx, jnp.sum(dxn * x * r, axis=0, keepdims=True)


def _fox_fwd(src, q_c0, k_c0, v_c0, gq2, gk2, cum_col, cum_row3):
    S = src.shape[0]
    T = ATT_T
    nq, nhp = S // T, ATT_HEADS // 2
    qb0, kb0, vb0 = q_c0 // LANES, k_c0 // LANES, v_c0 // LANES
    scale = ATT_D ** -0.5

    def body(q_ref, kraw_ref, v_ref, gq_ref, gk_ref, cc_ref, cr_ref, o_ref, l_ref, k_ref):
        hp, i = pl.program_id(0), pl.program_id(1)
        first = _iota((1, LANES), 1) < ATT_D

        @pl.when(i == 0)
        def _():
            k_ref[...] = _pair_norm(kraw_ref[...], gk_ref[...], first)[0].astype(BF16)

        q = (_pair_norm(q_ref[...], gq_ref[...], first)[0] * scale).astype(BF16)
        zero = jnp.zeros_like(q)
        qs = (jnp.where(first, q, zero), jnp.where(first, zero, q))
        cc = cc_ref[...]
        cq = (_pick_col(cc, 2 * hp), _pick_col(cc, 2 * hp + 1))
        tri = _iota((T, T), 0) >= _iota((T, T), 1)

        def tile(j, carry, diagonal):
            off = pl.multiple_of(j * T, T)
            k = k_ref[pl.ds(off, T), :]
            v = v_ref[pl.ds(off, T), :].astype(BF16)
            cr = cr_ref[j]
            out = []
            for hh in range(2):
                m, l, acc = carry[3 * hh:3 * hh + 3]
                s = _bdot(qs[hh], k, _NT) + (cq[hh] - _pick_row(cr, 2 * hp + hh))
                if diagonal:
                    s = jnp.where(tri, s, NEG)
                m_new = jnp.maximum(m, jnp.max(s, axis=1, keepdims=True))
                alpha = jnp.exp(m - m_new)
                p = jnp.exp(s - m_new)
                out += [m_new, alpha * l + jnp.sum(p, axis=1, keepdims=True), alpha * acc + _bdot(p, v, _NN)]
            return tuple(out)

        init = (jnp.full((T, 1), NEG, F32), jnp.zeros((T, 1), F32), jnp.zeros((T, LANES), F32)) * 2
        carry = lax.fori_loop(0, i, lambda j, c: tile(j, c, False), init)
        ma, la, acca, mb, lb, accb = tile(i, carry, True)
        o_ref[...] = jnp.where(first, acca / la, accb / lb).astype(o_ref.dtype)
        l_ref[...] = jnp.where(first, ma + jnp.log(la), mb + jnp.log(lb))

    gain = pl.BlockSpec((1, LANES), lambda hp, i: (0, 0))
    return pl.pallas_call(
        body, name="fox_fwd", grid=(nhp, nq),
        in_specs=[pl.BlockSpec((T, LANES), lambda hp, i: (i, qb0 + hp)), pl.BlockSpec((S, LANES), lambda hp, i: (0, kb0 + hp)),
                  pl.BlockSpec((S, LANES), lambda hp, i: (0, vb0 + hp)), gain, gain,
                  pl.BlockSpec((T, 16), lambda hp, i: (i, 0)), pl.BlockSpec((nq, 16, T), lambda hp, i: (0, 0, 0))],
        out_specs=[pl.BlockSpec((T, LANES), lambda hp, i: (i, hp))] * 2,
        out_shape=[SDS((S, ATT_HEADS * ATT_D), BF16), SDS((S, ATT_HEADS * ATT_D), F32)],
        scratch_shapes=[pltpu.VMEM((S, LANES), BF16)],
        compiler_params=_cparams(dimension_semantics=("arbitrary", "arbitrary")),
    )(src, src, src, gq2, gk2, cum_col, cum_row3)


def _fox_bwd(src, q_c0, k_c0, v_c0, gq2, gk2, cum_col, cum_row3, lse, dsrc, d_c0):
    S = src.shape[0]
    T = ATT_T
    nq, nhp = S // T, ATT_HEADS // 2
    qb0, kb0, vb0, db0 = q_c0 // LANES, k_c0 // LANES, v_c0 // LANES, d_c0 // LANES
    scale = ATT_D ** -0.5

    def body(q_ref, kraw_ref, v_ref, gq_ref, gk_ref, cc_ref, cr_ref, l_ref, do_ref,
             dq_ref, dk_ref, dv_ref, dc_ref, dg_ref, k_ref, dk_acc, dv_acc, p_scr, dp_scr):
        hp, i = pl.program_id(0), pl.program_id(1)
        first = _iota((1, LANES), 1) < ATT_D
        tri = _iota((T, T), 0) >= _iota((T, T), 1)

        @pl.when(i == 0)
        def _():
            k_ref[...] = _pair_norm(kraw_ref[...], gk_ref[...], first)[0].astype(BF16)
            dk_acc[...] = jnp.zeros_like(dk_acc)
            dv_acc[...] = jnp.zeros_like(dv_acc)
            dc_ref[...] = jnp.zeros_like(dc_ref)
            dg_ref[...] = jnp.zeros_like(dg_ref)

        q_raw = q_ref[...]
        qn, rq = _pair_norm(q_raw, gq_ref[...], first)
        q = (qn * scale).astype(BF16)
        zq = jnp.zeros_like(q)
        dob = do_ref[...].astype(BF16)
        zd = jnp.zeros_like(dob)
        lse_blk, cc = l_ref[...], cc_ref[...]
        dq = jnp.zeros((T, LANES), F32)
        for hh in range(2):
            sel = first if hh == 0 else jnp.logical_not(first)
            qh, doh = jnp.where(sel, q, zq), jnp.where(sel, dob, zd)
            bias_q = _pick_col(cc, 2 * hp + hh) - jnp.max(jnp.where(sel, lse_blk, NEG), axis=1, keepdims=True)

            def probs(j, delta, diagonal):
                off = pl.multiple_of(j * T, T)
                s = _bdot(qh, k_ref[pl.ds(off, T), :], _NT) + (bias_q - _pick_row(cr_ref[j], 2 * hp + hh))
                if diagonal:
                    s = jnp.where(tri, s, NEG)
                p = jnp.exp(s)
                dp = _bdot(doh, v_ref[pl.ds(off, T), :], _NT)
                p_scr[j] = p
                dp_scr[j] = dp
                return delta + jnp.sum(p * dp, axis=1, keepdims=True)

            delta = lax.fori_loop(0, i, lambda j, d: probs(j, d, False), jnp.zeros((T, 1), F32))
            delta = probs(i, delta, True)

            def grads(j, dq):
                off = pl.multiple_of(j * T, T)
                p = p_scr[j]
                ds = p * (dp_scr[j] - delta)
                dv_acc[pl.ds(off, T), :] += _bdot(p, doh, _TN)
                dk_acc[pl.ds(off, T), :] += _bdot(ds, qh, _TN)
                dc_ref[0, j, hh:hh + 1, :] -= jnp.sum(ds, axis=0, keepdims=True)
                zk = jnp.zeros((T, LANES), BF16)
                return dq + _bdot(ds, jnp.where(sel, k_ref[pl.ds(off, T), :], zk), _NN)

            dq = lax.fori_loop(0, i + 1, grads, dq)
        dq_raw, dgq = _pair_norm_bwd(dq * scale, q_raw, rq, gq_ref[...], first)
        dq_ref[...] = dq_raw.astype(dq_ref.dtype)
        dg_ref[0, 0:1, :] += dgq

        @pl.when(i == nq - 1)
        def _():
            k_raw = kraw_ref[...]
            rk = _pair_norm(k_raw, gk_ref[...], first)[1]
            dk_raw, dgk = _pair_norm_bwd(dk_acc[...], k_raw, rk, gk_ref[...], first)
            dk_ref[...] = dk_raw.astype(dk_ref.dtype)
            dv_ref[...] = dv_acc[...].astype(dv_ref.dtype)
            dg_ref[0, 1:2, :] = dgk

    gain = pl.BlockSpec((1, LANES), lambda hp, i: (0, 0))
    band = SDS((S, ATT_HEADS * ATT_D), BF16)
    return pl.pallas_call(
        body, name="fox_bwd", grid=(nhp, nq),
        in_specs=[pl.BlockSpec((T, LANES), lambda hp, i: (i, qb0 + hp)), pl.BlockSpec((S, LANES), lambda hp, i: (0, kb0 + hp)),
                  pl.BlockSpec((S, LANES), lambda hp, i: (0, vb0 + hp)), gain, gain,
                  pl.BlockSpec((T, 16), lambda hp, i: (i, 0)), pl.BlockSpec((nq, 16, T), lambda hp, i: (0, 0, 0)),
                  pl.BlockSpec((T, LANES), lambda hp, i: (i, hp)), pl.BlockSpec((T, LANES), lambda hp, i: (i, db0 + hp))],
        out_specs=[pl.BlockSpec((T, LANES), lambda hp, i: (i, hp)), pl.BlockSpec((S, LANES), lambda hp, i: (0, hp)),
                   pl.BlockSpec((S, LANES), lambda hp, i: (0, hp)), pl.BlockSpec((1, nq, 8, T), lambda hp, i: (hp, 0, 0, 0)),
                   pl.BlockSpec((1, 8, LANES), lambda hp, i: (hp, 0, 0))],
        out_shape=[band, band, band, SDS((nhp, nq, 8, T), F32), SDS((nhp, 8, LANES), F32)],
        scratch_shapes=[pltpu.VMEM((S, LANES), BF16), pltpu.VMEM((S, LANES), F32), pltpu.VMEM((S, LANES), F32),
                        pltpu.VMEM((nq, T, T), F32), pltpu.VMEM((nq, T, T), F32)],
        compiler_params=_cparams(dimension_semantics=("arbitrary", "arbitrary")),
    )(src, src, src, gq2, gk2, cum_col, cum_row3, lse, dsrc)


def _fold_gains(dg):
    def body(d_ref, o_ref):
        t = d_ref[0]
        for h in range(1, dg.shape[0]):
            t = t + d_ref[h]
        o_ref[...] = t + pltpu.roll(t, ATT_D, axis=1)

    return pl.pallas_call(body, name="fold_gains", out_shape=SDS(dg.shape[1:], F32), compiler_params=_cparams())(dg)


def _loss_head(y, target, *, tm):
    M, W = y.shape

    def body(y_ref, t_ref, dy_ref, loss_ref):
        @pl.when(pl.program_id(0) == 0)
        def _():
            loss_ref[...] = jnp.zeros_like(loss_ref)

        e = y_ref[...] - t_ref[...]
        dy_ref[...] = e * (1.0 / W)
        loss_ref[...] += jnp.sum(jnp.sum(e * e, axis=1, keepdims=True), axis=0, keepdims=True) * (0.5 / W)

    return pl.pallas_call(
        body, name="loss_head", grid=(M // tm,),
        in_specs=[pl.BlockSpec((tm, W), lambda i: (i, 0))] * 2,
        out_specs=[pl.BlockSpec((tm, W), lambda i: (i, 0)), pl.BlockSpec((1, 1), lambda i: (0, 0))],
        out_shape=[SDS((M, W), F32), SDS((1, 1), F32)],
        compiler_params=_cparams(dimension_semantics=("arbitrary",)),
    )(y, target)


def _adamw_math(w, g, m, v):
    m = ADAM_B1 * m + (1.0 - ADAM_B1) * g
    v = ADAM_B2 * v + (1.0 - ADAM_B2) * jnp.square(g)
    m_hat = m / (1.0 - ADAM_B1 ** ADAM_STEP)
    v_hat = v / (1.0 - ADAM_B2 ** ADAM_STEP)
    delta = -ADAM_LR * (m_hat / (jnp.sqrt(v_hat) + ADAM_EPS) + ADAM_WD * w)
    return delta, m, v


def _reduce_adamw(parts, w, m, v, *, tr, name, tc=None):
    R, C = w.shape
    tr, tc = min(tr, R), tc or C

    def body(p_ref, w_ref, m_ref, v_ref, g_ref, d_ref, nm_ref, nv_ref):
        g = p_ref[0].astype(F32)
        for s in range(1, N_DEV):
            g = g + p_ref[s].astype(F32)
        g_ref[...] = g
        d_ref[...], nm_ref[...], nv_ref[...] = _adamw_math(w_ref[...], g, m_ref[...], v_ref[...])

    blk = pl.BlockSpec((tr, tc), lambda i, j: (i, j))
    return pl.pallas_call(
        body, name=name, grid=(R // tr, C // tc),
        in_specs=[pl.BlockSpec((N_DEV, tr, tc), lambda i, j: (0, i, j)), blk, blk, blk], out_specs=[blk] * 4,
        out_shape=[SDS((R, C), F32)] * 4, compiler_params=_cparams(dimension_semantics=("parallel", "parallel")),
    )(parts, w, m, v)


def _adamw(w, g, m, v, *, name):
    def body(w_ref, g_ref, m_ref, v_ref, d_ref, nm_ref, nv_ref):
        d_ref[...], nm_ref[...], nv_ref[...] = _adamw_math(w_ref[...], g_ref[...], m_ref[...], v_ref[...])

    return pl.pallas_call(body, name=name, out_shape=[SDS(w.shape, F32)] * 3, compiler_params=_cparams())(w, g, m, v)


def _peers():
    x, y, c = lax.axis_index("x"), lax.axis_index("y"), lax.axis_index("c")
    out = []
    for k in range(1, N_DEV):
        px, py, pc = x ^ ((k >> 2) & 1), y ^ ((k >> 1) & 1), c ^ (k & 1)
        out.append(((px, py, pc), 4 * px + 2 * py + pc))
    return 4 * x + 2 * y + c, out


_HBM = pl.BlockSpec(memory_space=pltpu.HBM)
_SEM = pl.BlockSpec(memory_space=pltpu.SEMAPHORE)
_DATAFLOW = pltpu.SideEffectType.DATAFLOW_SIDE_EFFECTING


def _exchange_start(arrays, *, scatter, name):
    n, npeer = len(arrays), N_DEV - 1
    lands = [lax.empty(a.shape if scatter else (N_DEV,) + a.shape, a.dtype) for a in arrays]

    def body(*refs):
        ins, lds, sems, token = refs[:n], refs[n:2 * n], refs[2 * n:4 * n], refs[-1]
        me, peers = _peers()
        for a in range(n):
            for k, (dev, idx) in enumerate(peers):
                pltpu.make_async_remote_copy(
                    src_ref=ins[a].at[idx] if scatter else ins[a], dst_ref=lds[a].at[me],
                    send_sem=sems[2 * a].at[k], recv_sem=sems[2 * a + 1].at[k], device_id=dev, device_id_type=MESH).start()
        token[...] = jnp.zeros_like(token)

    res = pl.pallas_call(
        body, name=name,
        out_shape=([pltpu.SemaphoreType.DMA((npeer,))] * (2 * n) + [pltpu.HBM(a.shape, a.dtype) for a in arrays]
                   + [pltpu.HBM(l.shape, l.dtype) for l in lands] + [SDS((8, LANES), F32)]),
        in_specs=[_HBM] * (2 * n), out_specs=[_SEM] * (2 * n) + [_HBM] * (2 * n) + [pl.BlockSpec(memory_space=pltpu.VMEM)],
        input_output_aliases={i: 2 * n + i for i in range(2 * n)},
        compiler_params=pltpu.CompilerParams(has_side_effects=_DATAFLOW),
    )(*[pltpu.with_memory_space_constraint(a, pltpu.HBM) for a in arrays],
      *[pltpu.with_memory_space_constraint(l, pltpu.HBM) for l in lands])
    sems, thru, token = res[:2 * n], res[2 * n:4 * n], res[-1]
    return [(thru[a], thru[n + a], sems[2 * a], sems[2 * a + 1]) for a in range(n)], token


def _exchange_wait(handles, after, *, scatter, name):
    n = len(handles)

    def body(*refs):
        srcs, lds, sems = refs[:n], refs[n:2 * n], refs[2 * n:4 * n]
        me, peers = _peers()
        for a in range(n):
            for k, (dev, idx) in enumerate(peers):
                cp = pltpu.make_async_remote_copy(
                    src_ref=srcs[a].at[idx] if scatter else srcs[a], dst_ref=lds[a].at[idx],
                    send_sem=sems[2 * a].at[k], recv_sem=sems[2 * a + 1].at[k], device_id=dev, device_id_type=MESH)
                cp.wait_send()
                cp.wait_recv()

    flat_sems = [s for h in handles for s in (h[2], h[3])]
    res = pl.pallas_call(
        body, name=name,
        out_shape=[pltpu.HBM(h[0].shape, h[0].dtype) for h in handles] + [pltpu.HBM(h[1].shape, h[1].dtype) for h in handles],
        in_specs=[_HBM] * (2 * n) + [_SEM] * (2 * n) + [pl.BlockSpec(memory_space=pl.ANY)], out_specs=[_HBM] * (2 * n),
        input_output_aliases={i: i for i in range(2 * n)},
        compiler_params=pltpu.CompilerParams(has_side_effects=_DATAFLOW),
    )(*[h[0] for h in handles], *[h[1] for h in handles], *flat_sems, after)
    return [(res[a], res[n + a]) for a in range(n)]


def _own_slot(landed, own, me):
    return lax.dynamic_update_slice(landed, own[None], (me,) + (0,) * own.ndim)


SMALL = (("g_mix", 1024), ("conv_w", 6144), ("conv_b", 1536), ("dt_bias", 16), ("a_log", 16), ("d_skip", 16),
         ("ssm_norm_w", 1024), ("g_q", 64), ("g_k", 64), ("f_bias", 16), ("g_xattn", 1024), ("g_mem", 1024),
         ("xg_q", 256), ("xg_k", 256), ("g_mlp", 1024))
SLAB_ROWS = 112
BIG = ("w_in", "w_out", "xq_w", "xkv_w", "xo_w", "w_up", "w_down")
WEIGHTS = ("g_mix", "w_in", "conv_w", "conv_b", "dt_bias", "a_log", "d_skip", "ssm_norm_w", "g_q", "g_k", "f_bias", "w_out",
           "g_xattn", "g_mem", "xq_w", "xkv_w", "xg_q", "xg_k", "xo_w", "g_mlp", "w_up", "w_down")
O_Z, O_XS, O_B, O_C, O_DT, O_Q, O_K, O_V, O_F, O_END = 0, 1024, 2048, 2304, 2560, 2576, 3600, 4624, 5648, 5664


def _pack_small(vals):
    rows = []
    for name, size in SMALL:
        flat = vals[name].reshape(-1).astype(F32)
        pad = -size % LANES
        rows.append(jnp.pad(flat, (0, pad)).reshape(-1, LANES))
    slab = jnp.concatenate(rows, axis=0)
    return jnp.pad(slab, ((0, SLAB_ROWS - slab.shape[0]), (0, 0)))


def _unpack_small(slab):
    out, r = {}, 0
    for name, size in SMALL:
        nr = -(-size // LANES)
        out[name] = slab[r:r + nr].reshape(-1)[:size]
        r += nr
    return out


def _cols(a, lo, hi):
    return a[:, lo:hi]


def _step(p, m, v, x, mem, target):
    S = x.shape[0]
    TM = 256
    me = 4 * lax.axis_index("x") + 2 * lax.axis_index("y") + lax.axis_index("c")

    def rms(u, g, name):
        return _rw_fwd(_rms_fn, [_whole(u)], [_whole(g)], [(D_MODEL, BF16)], tm=TM, name=name)[0]

    def pin(param, token):
        return param + token[0:1, 0:1]

    def landed_with_own(pairs, scatter):
        out = []
        for src, land in pairs:
            own = lax.dynamic_index_in_dim(src, me, 0, keepdims=False) if scatter else src
            out.append(_own_slot(land, own, me))
        return out

    w_in_own, m_in_own, v_in_own = p["w_in"].T, m["w_in"].T, v["w_in"].T
    ag, ag_token = _exchange_start([w_in_own.astype(BF16), p["conv_w"]] + [p[n].astype(BF16) for n in BIG[1:]],
                                   scatter=False, name="allgather_start")
    h1 = rms(x, pin(p["g_mix"], ag_token), "rms_mix")
    win_g, convw_g = landed_with_own(_exchange_wait(ag[:2], h1, scatter=False, name="allgather_wait_in"), False)
    w_in_o = win_g.reshape(O_END, D_MODEL)
    w_in_t = jnp.concatenate(
        [w_in_o[O_Z:O_XS], w_in_o[O_XS:O_B], w_in_o[O_Q:O_K], w_in_o[O_K:O_V], w_in_o[O_V:O_F], w_in_o[O_B:O_C],
         w_in_o[O_C:O_DT], w_in_o[O_DT:O_Q], w_in_o[O_F:O_END], jnp.zeros((P_COLS - C_DTF - 32, D_MODEL), BF16)], axis=0)
    conv_w = convw_g.transpose(1, 0, 2).reshape(4, 1536)
    cw_xs, cw_bc = conv_w[:, :1024], conv_w[:, 1024:]
    cb_xs, cb_bc = p["conv_b"][:, :1024], p["conv_b"][:, 1024:]
    dt_bias, a_log, f_bias = p["dt_bias"].reshape(16, 1), p["a_log"].reshape(16, 1), p["f_bias"].reshape(16, 1)

    proj = _matmul(h1, w_in_t, mode="nt", tm=1024, tn=640, tk=1024, name="mm_in")
    xs_c = _conv_fwd(proj, C_XS, 1024, cw_xs, cb_xs, name="conv_xs")
    bc_c = _conv_fwd(proj, C_B, 512, cw_bc, cb_bc, name="conv_bc")
    dtf_t = proj[:, C_DTF:C_DTF + 32].T
    dt_t, acs_t, cum_t = _dtf_fwd(dtf_t, dt_bias, a_log, f_bias)
    dt_col, acs_col, cum_col = dt_t.T, acs_t.T, cum_t.T
    cum_row3 = cum_t.reshape(16, S // ATT_T, ATT_T).transpose(1, 0, 2)
    y_ssd, hs = _ssd_fwd(xs_c, dt_col, acs_col, acs_t, bc_c)
    gate_rows = [_whole(y_ssd), _whole(xs_c), (proj, C_Z, 1024)]
    gate_pars = [_whole(p["d_skip"]), _whole(p["ssm_norm_w"])]
    y_ssm = _rw_fwd(_gate_fn, gate_rows, gate_pars, [(1024, BF16)], tm=TM, name="gate")[0]
    gq2, gk2 = jnp.tile(p["g_q"], (1, 2)), jnp.tile(p["g_k"], (1, 2))
    o, lse = _fox_fwd(proj, C_Q, C_K, C_V, gq2, gk2, cum_col, cum_row3)
    mixed = jnp.concatenate([y_ssm, o], axis=1)
    wout_g, xq_g, xkv_g, xo_g, wup_g, wdown_g = landed_with_own(
        _exchange_wait(ag[2:], mixed, scatter=False, name="allgather_wait_rest"), False)
    w_out = wout_g.reshape(2 * D_MODEL, D_MODEL)
    xq_w = xq_g.reshape(D_MODEL, D_MODEL)
    xkv_w = xkv_g.transpose(1, 0, 2).reshape(D_MODEL, 2 * D_MODEL)
    xo_w = xo_g.reshape(D_MODEL, D_MODEL)
    w_up = wup_g.transpose(1, 0, 2).reshape(D_MODEL, 4 * D_MODEL)
    w_down = wdown_g.reshape(4 * D_MODEL, D_MODEL)
    x1 = _matmul(mixed, w_out, mode="nn", tm=1024, tn=512, tk=2048, add=x, name="mm_out")

    h2 = rms(x1, p["g_xattn"], "rms_xattn")
    mem_n = rms(mem, p["g_mem"], "rms_mem")
    q2 = _matmul(h2, xq_w, mode="nn", tm=1024, tn=512, tk=1024, name="mm_xq")
    kv = _matmul(mem_n, xkv_w, mode="nn", tm=256, tn=1024, tk=1024, name="mm_xkv")
    xa_rows = [(q2, X_D * h, X_D) for h in range(X_HEADS)]
    xa_pars = ([(kv, X_D * h, X_D) for h in range(X_HEADS)] + [(kv, D_MODEL + X_D * h, X_D) for h in range(X_HEADS)]
               + [_whole(p["xg_q"]), _whole(p["xg_k"])])
    o2 = _rw_fwd(_xattn_fn, xa_rows, xa_pars, [(D_MODEL, BF16)], tm=TM, name="xattn")[0]
    x2 = _matmul(o2, xo_w, mode="nn", tm=1024, tn=512, tk=1024, add=x1, name="mm_xo")

    h3 = rms(x2, p["g_mlp"], "rms_mlp")
    a, usq = _matmul(h3, w_up, mode="nn", tm=1024, tn=1024, tk=1024, name="mm_up", out_dtypes=[F32, BF16],
                     epilogue=lambda acc: (acc, jnp.square(jax.nn.relu(acc))))
    x3 = _matmul(usq, w_down, mode="nn", tm=1024, tn=512, tk=2048, add=x2, name="mm_down")
    dy, loss_part = _loss_head(x3, target, tm=TM)
    loss = lax.psum(loss_part[0, 0], ("x", "y", "c"))

    def col_shards(a):
        r, c = a.shape
        return a.reshape(r, N_DEV, c // N_DEV).transpose(1, 0, 2)

    def row_shards(a):
        r, c = a.shape
        return a.reshape(N_DEV, r // N_DEV, c)

    g = {}
    g["w_down"] = _matmul(usq, dy, mode="tn", out_dtype=GRAD_WIRE, tm=1024, tn=1024, tk=1024, name="mm_d_wdown")
    da = _matmul(dy, w_down, mode="nt", tm=1024, tn=1024, tk=1024, name="mm_d_usq", out_dtype=BF16, extras=(a,),
                 epilogue=lambda acc, av: (2.0 * jax.nn.relu(av) * acc,))
    g["w_up"] = _matmul(h3, da, mode="tn", out_dtype=GRAD_WIRE, tm=1024, tn=1024, tk=1024, name="mm_d_wup")
    sent_mlp, token = _exchange_start([row_shards(g["w_down"]), col_shards(g["w_up"])], scatter=True,
                                      name="grads_start_mlp")
    dh3 = _matmul(da, w_up, mode="nt", tm=1024, tn=512, tk=2048, name="mm_d_h3")
    dx2, g["g_mlp"] = _rw_bwd(_rms_fn, [_whole(x2)], [_whole(pin(p["g_mlp"], token))], [_whole(dh3)], tm=TM,
                              name="rms_mlp_bwd", row_grads=[F32], adds={0: _whole(dy)})

    g["xo_w"] = _matmul(o2, dx2, mode="tn", out_dtype=GRAD_WIRE, tm=1024, tn=1024, tk=1024, name="mm_d_wxo")
    do2 = _matmul(dx2, xo_w, mode="nt", tm=1024, tn=512, tk=1024, name="mm_d_o2")
    xa = _rw_bwd(_xattn_fn, xa_rows, xa_pars, [_whole(do2)], tm=TM, name="xattn_bwd", row_grads=[BF16] * X_HEADS)
    dq2 = jnp.concatenate(xa[:X_HEADS], axis=1)
    dkv = jnp.concatenate(xa[X_HEADS:3 * X_HEADS], axis=1)
    g["xg_q"], g["xg_k"] = xa[3 * X_HEADS], xa[3 * X_HEADS + 1]
    g["xq_w"] = _matmul(h2, dq2, mode="tn", out_dtype=GRAD_WIRE, tm=1024, tn=1024, tk=1024, name="mm_d_wxq")
    dh2 = _matmul(dq2, xq_w, mode="nt", tm=1024, tn=512, tk=1024, name="mm_d_h2")
    g["xkv_w"] = _matmul(mem_n, dkv, mode="tn", out_dtype=GRAD_WIRE, tm=1024, tn=1024, tk=256, name="mm_d_wxkv")
    dmem_n = _matmul(dkv, xkv_w, mode="nt", tm=256, tn=1024, tk=2048, name="mm_d_memn")
    g["g_mem"] = _rw_bwd(_rms_fn, [_whole(mem)], [_whole(p["g_mem"])], [_whole(dmem_n)], tm=TM, name="rms_mem_bwd",
                         row_grads=[None])[0]
    dx1, g["g_xattn"] = _rw_bwd(_rms_fn, [_whole(x1)], [_whole(p["g_xattn"])], [_whole(dh2)], tm=TM, name="rms_xattn_bwd",
                                row_grads=[F32], adds={0: _whole(dx2)})

    g["w_out"] = _matmul(mixed, dx1, mode="tn", out_dtype=GRAD_WIRE, tm=1024, tn=1024, tk=1024, name="mm_d_wout")
    sent_mid, token = _exchange_start(
        [row_shards(g["w_out"]), row_shards(g["xq_w"]), col_shards(g["xkv_w"]), row_shards(g["xo_w"])], scatter=True,
        name="grads_start_mid")
    dmixed = _matmul(dx1, w_out, mode="nt", tm=1024, tn=1024, tk=1024, name="mm_d_mixed")
    dq, dk, dv, dcum4, dgain = _fox_bwd(proj, C_Q, C_K, C_V, pin(gq2, token), gk2, cum_col, cum_row3, lse, dmixed, 1024)
    gains = _fold_gains(dgain)
    g["g_q"], g["g_k"] = gains[0:1, :ATT_D], gains[1:2, :ATT_D]
    dy_ssd, dxs_g, dz, g["d_skip"], g["ssm_norm_w"] = _rw_bwd(
        _gate_fn, gate_rows, gate_pars, [(dmixed, 0, 1024)], tm=TM, name="gate_bwd", row_grads=[F32, F32, BF16])
    dxs_s, ddt_col, dacs_col, dacs_row, d_b, d_c = _ssd_bwd(xs_c, dt_col, acs_col, acs_t, bc_c, hs, dy_ssd)
    dcum_t = dcum4[:, :, 0:2, :].transpose(0, 2, 1, 3).reshape(16, S)
    ddtf_t, ddtb, dalog, dfb = _dtf_bwd(dtf_t, dt_bias, a_log, f_bias, ddt_col.T, dacs_col.T, dacs_row, dcum_t)
    g["dt_bias"], g["a_log"], g["f_bias"] = ddtb, dalog, dfb
    dxs_raw, dcw_xs, dcb_xs = _conv_bwd(proj, C_XS, 1024, cw_xs, cb_xs, [dxs_s, dxs_g], name="conv_xs_bwd")
    dbc_raw, dcw_bc, dcb_bc = _conv_bwd(proj, C_B, 512, cw_bc, cb_bc, [jnp.concatenate([d_b, d_c], axis=1)],
                                        name="conv_bc_bwd")
    g["conv_w"] = jnp.concatenate([dcw_xs, dcw_bc], axis=1)
    g["conv_b"] = jnp.concatenate([dcb_xs, dcb_bc], axis=1)
    ddtf = jnp.pad(ddtf_t.T.astype(BF16), ((0, 0), (0, P_COLS - C_DTF - 32)))
    dproj = jnp.concatenate([dz, dxs_raw, dq, dk, dv, dbc_raw, ddtf], axis=1)
    dw_in_p = _matmul(dproj, h1, mode="tn", out_dtype=GRAD_WIRE, tm=640, tn=1024, tk=1024, name="mm_d_win")
    g["w_in"] = jnp.concatenate(
        [dw_in_p[C_Z:C_Q], dw_in_p[C_B:C_DTF + 16], dw_in_p[C_Q:C_B], dw_in_p[C_DTF + 16:C_DTF + 32]], axis=0)
    sent_in, token = _exchange_start([row_shards(g["w_in"])], scatter=True, name="grads_start_in")
    dh1 = _matmul(dproj, w_in_t, mode="nn", tm=1024, tn=512, tk=1920, name="mm_d_h1")
    grad_x, g["g_mix"] = _rw_bwd(_rms_fn, [_whole(x)], [_whole(pin(p["g_mix"], token))], [_whole(dh1)], tm=TM,
                                 name="rms_mix_bwd", row_grads=[F32], adds={0: _whole(dx1)})
    sent_small, _ = _exchange_start([_pack_small(g)], scatter=False, name="small_grads_start")

    grads, delta, new_m, new_v = {}, {}, {}, {}

    def update(names, sent, after, wait_name):
        parts = landed_with_own(_exchange_wait(sent, after, scatter=True, name=wait_name), True)
        for name, part in zip(names, parts, strict=True):
            if name == "w_in":
                res = _reduce_adamw(part, w_in_own, m_in_own, v_in_own, tr=part.shape[1], tc=256, name="adamw_" + name)
                res = [r.T for r in res]
            else:
                res = _reduce_adamw(part, p[name], m[name], v[name], tr=128, name="adamw_" + name)
            grads[name], delta[name], new_m[name], new_v[name] = res

    update(("w_down", "w_up"), sent_mlp, grad_x, "grads_wait_mlp")
    update(("w_out", "xq_w", "xkv_w", "xo_w"), sent_mid, delta["w_up"], "grads_wait_mid")
    update(("w_in",), sent_in, delta["xo_w"], "grads_wait_in")
    small_parts = landed_with_own(_exchange_wait(sent_small, delta["w_in"], scatter=False, name="small_grads_wait"), False)[0]
    zeros_cw = jnp.zeros((4, 1536), F32)
    slabs = [_pack_small({**d, "conv_w": zeros_cw}) for d in (p, m, v)]
    sg, sd, sm, sv = _reduce_adamw(small_parts, *slabs, tr=SLAB_ROWS, name="adamw_small")
    for dst, slab in ((grads, sg), (delta, sd), (new_m, sm), (new_v, sv)):
        for name, flat in _unpack_small(slab).items():
            if name != "conv_w":
                dst[name] = flat.reshape(p[name].shape)
    cw_shard = p["conv_w"].shape[1]
    grads["conv_w"] = lax.dynamic_slice(_unpack_small(sg)["conv_w"].reshape(4, 1536), (0, me * cw_shard), (4, cw_shard))
    delta["conv_w"], new_m["conv_w"], new_v["conv_w"] = _adamw(p["conv_w"], grads["conv_w"], m["conv_w"], v["conv_w"],
                                                               name="adamw_conv_w")
    return loss, grad_x, grads, delta, new_m, new_v


def kernel(x, mem, g_mix, w_in, conv_w, conv_b, dt_bias, a_log, d_skip, ssm_norm_w, g_q, g_k, f_bias, w_out, g_xattn, g_mem, xq_w, xkv_w, xg_q, xg_k, xo_w, g_mlp, w_up, w_down, loss_target, m_g_mix, m_w_in, m_conv_w, m_conv_b, m_dt_bias, m_a_log, m_d_skip, m_ssm_norm_w, m_g_q, m_g_k, m_f_bias, m_w_out, m_g_xattn, m_g_mem, m_xq_w, m_xkv_w, m_xg_q, m_xg_k, m_xo_w, m_g_mlp, m_w_up, m_w_down, v_g_mix, v_w_in, v_conv_w, v_conv_b, v_dt_bias, v_a_log, v_d_skip, v_ssm_norm_w, v_g_q, v_g_k, v_f_bias, v_w_out, v_g_xattn, v_g_mem, v_xq_w, v_xkv_w, v_xg_q, v_xg_k, v_xo_w, v_g_mlp, v_w_up, v_w_down):
    args = locals()
    drop = lambda t: t[0] if t.ndim == 3 else t
    p = {n: drop(args[n]) for n in WEIGHTS}
    m = {n: drop(args["m_" + n]) for n in WEIGHTS}
    v = {n: drop(args["v_" + n]) for n in WEIGHTS}
    loss, grad_x, grads, delta, new_m, new_v = _step(p, m, v, x[0], mem[0], loss_target[0])
    outs = [loss, grad_x[None]]
    for d in (grads, delta, new_m, new_v):
        outs += [d[n].reshape(args[n].shape) for n in WEIGHTS]
    return tuple(outs)
```

```python
import functools
import math

import jax
import jax.numpy as jnp
from jax import lax
from jax.experimental import pallas as pl
from jax.experimental.pallas import tpu as pltpu

F32, BF16 = jnp.float32, jnp.bfloat16
SDS = jax.ShapeDtypeStruct
HI = lax.Precision.HIGHEST
MESH = pl.DeviceIdType.MESH

N_DEV = 8
EPS = 1e-5
D_MODEL = 1024
SSM_HEADS, SSM_P, SSM_N, SSM_GROUPS, CHUNK = 16, 64, 128, 2, 128
ATT_HEADS, ATT_D = 16, 64
X_HEADS, X_D = 4, 256
LANES = 128
VMEM_LIMIT = 48 * 1024 * 1024
NEG = -1e30

GRAD_WIRE = BF16
ADAM_LR, ADAM_B1, ADAM_B2, ADAM_EPS, ADAM_WD, ADAM_STEP = 0.001, 0.9, 0.999, 1e-08, 0.01, 10

C_Z, C_XS, C_Q, C_K, C_V, C_B, C_C, C_DTF, P_COLS = 0, 1024, 2048, 3072, 4096, 5120, 5376, 5632, 5760

_NN = (((1,), (0,)), ((), ()))
_NT = (((1,), (1,)), ((), ()))
_TN = (((0,), (0,)), ((), ()))


def _cparams(**kw):
    return pltpu.CompilerParams(vmem_limit_bytes=VMEM_LIMIT, **kw)


def _bdot(a, b, dn):
    return lax.dot_general(a.astype(BF16), b.astype(BF16), dn, preferred_element_type=F32)


@jax.custom_vjp
def mm_nn(a, b):
    return _bdot(a, b, _NN)


mm_nn.defvjp(lambda a, b: (mm_nn(a, b), (a, b)), lambda r, g: (_bdot(g, r[1], _NT), _bdot(r[0], g, _TN)))


@jax.custom_vjp
def mm_nt(a, b):
    return _bdot(a, b, _NT)


mm_nt.defvjp(lambda a, b: (mm_nt(a, b), (a, b)), lambda r, g: (_bdot(g, r[1], _NN), _bdot(g, r[0], _TN)))


@jax.custom_vjp
def mm_tn(a, b):
    return _bdot(a, b, _TN)


mm_tn.defvjp(lambda a, b: (mm_tn(a, b), (a, b)), lambda r, g: (_bdot(r[1], g, _NT), _bdot(r[0], g, _NN)))


def _cdot(x, c):
    return jnp.dot(x, c, precision=HI, preferred_element_type=F32)


def _iota(shape, dim):
    return lax.broadcasted_iota(jnp.int32, shape, dim)


def _matmul(a, b, *, mode, tm, tn, tk, name, out_dtype=F32, add=None, extras=(), epilogue=None, out_dtypes=None):
    if mode == "tn":
        K, M = a.shape
    else:
        M, K = a.shape
    N = b.shape[0] if mode == "nt" else b.shape[1]
    tm, tn, tk = min(tm, M), min(tn, N), min(tk, K)
    assert M % tm == 0 and N % tn == 0 and K % tk == 0, (name, M, N, K, tm, tn, tk)
    nk = K // tk
    dn = {"nn": _NN, "nt": _NT, "tn": _TN}[mode]
    if add is not None:
        extras, epilogue = (add,), lambda acc, r: (acc + r,)
    elif epilogue is None:
        epilogue = lambda acc: (acc,)
    out_dtypes = out_dtypes or [out_dtype]
    ne, no = len(extras), len(out_dtypes)

    def body(*refs):
        a_ref, b_ref = refs[:2]
        e_refs, o_refs = refs[2:2 + ne], refs[2 + ne:2 + ne + no]

        def finish(acc):
            res = epilogue(acc, *[e[...] for e in e_refs])
            for o_ref, v in zip(o_refs, res, strict=True):
                o_ref[...] = v.astype(o_ref.dtype)

        prod = _bdot(a_ref[...], b_ref[...], dn)
        if nk == 1:
            finish(prod)
            return
        acc_ref = refs[-1]
        k = pl.program_id(2)

        @pl.when(k == 0)
        def _():
            acc_ref[...] = prod

        @pl.when(jnp.logical_and(k > 0, k < nk - 1))
        def _():
            acc_ref[...] += prod

        @pl.when(k == nk - 1)
        def _():
            finish(acc_ref[...] + prod)

    a_spec = pl.BlockSpec((tk, tm), lambda i, j, k: (k, i)) if mode == "tn" else pl.BlockSpec((tm, tk), lambda i, j, k: (i, k))
    b_spec = pl.BlockSpec((tn, tk), lambda i, j, k: (j, k)) if mode == "nt" else pl.BlockSpec((tk, tn), lambda i, j, k: (k, j))
    o_spec = pl.BlockSpec((tm, tn), lambda i, j, k: (i, j))
    res = pl.pallas_call(
        body, name=name, grid=(M // tm, N // tn, nk), in_specs=[a_spec, b_spec] + [o_spec] * ne, out_specs=[o_spec] * no,
        out_shape=[SDS((M, N), dt) for dt in out_dtypes], scratch_shapes=[pltpu.VMEM((tm, tn), F32)] if nk > 1 else [],
        compiler_params=_cparams(dimension_semantics=("parallel", "parallel", "arbitrary")),
    )(a, b, *extras)
    return res[0] if no == 1 else res


def _row_spec(tm, spec):
    _, c0, w = spec
    assert c0 % w == 0
    return pl.BlockSpec((tm, w), functools.partial(lambda i, cb: (i, cb), cb=c0 // w))


def _par_spec(spec):
    arr, c0, w = spec
    assert c0 % w == 0
    return pl.BlockSpec((arr.shape[0], w), functools.partial(lambda i, cb: (0, cb), cb=c0 // w))


def _whole(arr):
    return (arr, 0, arr.shape[1])


def _rw_fwd(fn, rows, params, outs, *, tm, name):
    M = rows[0][0].shape[0]
    nr, npar = len(rows), len(params)

    def body(*refs):
        rv = [r[...].astype(F32) for r in refs[:nr]]
        pv = [p[...].astype(F32) for p in refs[nr:nr + npar]]
        res = fn(*rv, *pv)
        for o_ref, v in zip(refs[nr + npar:], res, strict=True):
            o_ref[...] = v.astype(o_ref.dtype)

    return pl.pallas_call(
        body, name=name, grid=(M // tm,),
        in_specs=[_row_spec(tm, r) for r in rows] + [_par_spec(p) for p in params],
        out_specs=[pl.BlockSpec((tm, w), lambda i: (i, 0)) for w, _ in outs],
        out_shape=[SDS((M, w), dt) for w, dt in outs],
        compiler_params=_cparams(dimension_semantics=("parallel",)),
    )(*[r[0] for r in rows], *[p[0] for p in params])


def _rw_bwd(fn, rows, params, cts, *, tm, name, row_grads, adds=None):
    M = rows[0][0].shape[0]
    adds = adds or {}
    nr, npar, nc = len(rows), len(params), len(cts)
    add_keys = sorted(adds)
    want = [k for k in range(nr) if row_grads[k] is not None]

    def body(*refs):
        pos = 0
        r_refs = refs[pos:pos + nr]; pos += nr
        p_refs = refs[pos:pos + npar]; pos += npar
        c_refs = refs[pos:pos + nc]; pos += nc
        a_refs = dict(zip(add_keys, refs[pos:pos + len(add_keys)])); pos += len(add_keys)
        dr_refs = dict(zip(want, refs[pos:pos + len(want)])); pos += len(want)
        dp_refs = refs[pos:pos + npar]
        rv = [r[...].astype(F32) for r in r_refs]
        pv = [p[...].astype(F32) for p in p_refs]
        _, vjp = jax.vjp(fn, *rv, *pv)
        g = vjp(tuple(c[...].astype(F32) for c in c_refs))
        for k in want:
            v = g[k]
            if k in a_refs:
                v = v + a_refs[k][...].astype(F32)
            dr_refs[k][...] = v.astype(dr_refs[k].dtype)
        first = pl.program_id(0) == 0
        for j in range(npar):
            @pl.when(first)
            def _(j=j):
                dp_refs[j][...] = jnp.zeros_like(dp_refs[j])
            dp_refs[j][...] += g[nr + j]

    res = pl.pallas_call(
        body, name=name, grid=(M // tm,),
        in_specs=([_row_spec(tm, r) for r in rows] + [_par_spec(p) for p in params] + [_row_spec(tm, c) for c in cts]
                  + [_row_spec(tm, adds[k]) for k in add_keys]),
        out_specs=([pl.BlockSpec((tm, rows[k][2]), lambda i: (i, 0)) for k in want]
                   + [pl.BlockSpec((p[0].shape[0], p[2]), lambda i: (0, 0)) for p in params]),
        out_shape=([SDS((M, rows[k][2]), row_grads[k]) for k in want] + [SDS((p[0].shape[0], p[2]), F32) for p in params]),
        compiler_params=_cparams(dimension_semantics=("arbitrary",)),
    )(*[r[0] for r in rows], *[p[0] for p in params], *[c[0] for c in cts], *[adds[k][0] for k in add_keys])
    return res


def _rms_fn(x, g):
    r = lax.rsqrt(jnp.mean(x * x, axis=-1, keepdims=True) + EPS)
    return (x * r * g,)


def _seg_mats(width, seg):
    n = width // seg
    p = (_iota((width, n), 0) // seg == _iota((width, n), 1)).astype(F32)
    e = (_iota((n, width), 1) // seg == _iota((n, width), 0)).astype(F32)
    return p, e


def _gate_fn(y, xs, z, dskip, w):
    width = SSM_HEADS * SSM_P
    _, e = _seg_mats(width, SSM_P)
    y = (y + _cdot(dskip, e) * xs) * (z * jax.nn.sigmoid(z))
    g0 = _iota((1, width), 1) < width // SSM_GROUPS
    y2 = y * y
    gw = width // SSM_GROUPS
    ms0 = jnp.sum(jnp.where(g0, y2, 0.0), axis=-1, keepdims=True) * (1.0 / gw)
    ms1 = jnp.sum(jnp.where(g0, 0.0, y2), axis=-1, keepdims=True) * (1.0 / gw)
    r = jnp.where(g0, lax.rsqrt(ms0 + EPS), lax.rsqrt(ms1 + EPS))
    return (y * r * w,)


def _xattn_fn(q0, q1, q2, q3, k0, k1, k2, k3, v0, v1, v2, v3, gq, gk):
    def norm(u, g):
        return u * lax.rsqrt(jnp.mean(u * u, axis=-1, keepdims=True) + EPS) * g
    outs = []
    for q, k, v in ((q0, k0, v0), (q1, k1, v1), (q2, k2, v2), (q3, k3, v3)):
        s = mm_nt(norm(q, gq), norm(k, gk)) * (X_D ** -0.5)
        p = jnp.exp(s - lax.stop_gradient(jnp.max(s, axis=-1, keepdims=True)))
        p = p / jnp.sum(p, axis=-1, keepdims=True)
        outs.append(mm_nn(p, v))
    return (jnp.concatenate(outs, axis=-1),)


CONV_TC = 256


def _shift_down(u, k):
    if k == 0:
        return u
    return jnp.where(_iota(u.shape, 0) >= k, pltpu.roll(u, k, axis=0), 0.0)


def _shift_up(u, k):
    if k == 0:
        return u
    n = u.shape[0]
    return jnp.where(_iota(u.shape, 0) < n - k, pltpu.roll(u, n - k, axis=0), 0.0)


def _conv_pre(u, w_ref, b):
    pre = b + w_ref[3:4, :] * u
    for k in (1, 2, 3):
        pre = pre + w_ref[3 - k:4 - k, :] * _shift_down(u, k)
    return pre


def _conv_fwd(src, c0, width, w, b, *, name):
    S = src.shape[0]
    cb0 = c0 // CONV_TC

    def body(u_ref, w_ref, b_ref, o_ref):
        pre = _conv_pre(u_ref[...], w_ref, b_ref[...])
        o_ref[...] = pre * jax.nn.sigmoid(pre)

    return pl.pallas_call(
        body, name=name, grid=(width // CONV_TC,),
        in_specs=[pl.BlockSpec((S, CONV_TC), lambda j: (0, cb0 + j)), pl.BlockSpec((4, CONV_TC), lambda j: (0, j)),
                  pl.BlockSpec((1, CONV_TC), lambda j: (0, j))],
        out_specs=pl.BlockSpec((S, CONV_TC), lambda j: (0, j)), out_shape=SDS((S, width), F32),
        compiler_params=_cparams(dimension_semantics=("parallel",)),
    )(src, w, b)


def _conv_bwd(src, c0, width, w, b, douts, *, name):
    S = src.shape[0]
    cb0 = c0 // CONV_TC
    nd = len(douts)

    def body(*refs):
        u_ref, w_ref, b_ref = refs[:3]
        d_refs = refs[3:3 + nd]
        du_ref, dw_ref, db_ref = refs[3 + nd:]
        u = u_ref[...]
        pre = _conv_pre(u, w_ref, b_ref[...])
        sg = jax.nn.sigmoid(pre)
        dout = d_refs[0][...]
        for r in d_refs[1:]:
            dout = dout + r[...]
        dpre = dout * (sg * (1.0 + pre * (1.0 - sg)))
        du = w_ref[3:4, :] * dpre
        dw_ref[3:4, :] = jnp.sum(dpre * u, axis=0, keepdims=True)
        for k in (1, 2, 3):
            du = du + w_ref[3 - k:4 - k, :] * _shift_up(dpre, k)
            dw_ref[3 - k:4 - k, :] = jnp.sum(dpre * _shift_down(u, k), axis=0, keepdims=True)
        du_ref[...] = du.astype(du_ref.dtype)
        db_ref[...] = jnp.sum(dpre, axis=0, keepdims=True)

    return pl.pallas_call(
        body, name=name, grid=(width // CONV_TC,),
        in_specs=[pl.BlockSpec((S, CONV_TC), lambda j: (0, cb0 + j)), pl.BlockSpec((4, CONV_TC), lambda j: (0, j)),
                  pl.BlockSpec((1, CONV_TC), lambda j: (0, j))] + [pl.BlockSpec((S, CONV_TC), lambda j: (0, j))] * nd,
        out_specs=[pl.BlockSpec((S, CONV_TC), lambda j: (0, j)), pl.BlockSpec((4, CONV_TC), lambda j: (0, j)),
                   pl.BlockSpec((1, CONV_TC), lambda j: (0, j))],
        out_shape=[SDS((S, width), BF16), SDS((4, width), F32), SDS((1, width), F32)],
        compiler_params=_cparams(dimension_semantics=("parallel",)),
    )(src, w, b, *douts)


def _softplus(x):
    return jnp.maximum(x, 0.0) + jnp.log(1.0 + jnp.exp(-jnp.abs(x)))


def _prefix_sum(x, seg):
    n = x.shape[1]
    pos = _iota(x.shape, 1) % seg
    k = 1
    while k < seg:
        x = x + jnp.where(pos >= k, pltpu.roll(x, k, axis=1), 0.0)
        k *= 2
    return x


def _suffix_sum(x, seg):
    n = x.shape[1]
    pos = _iota(x.shape, 1) % seg
    k = 1
    while k < seg:
        x = x + jnp.where(pos + k < seg, pltpu.roll(x, n - k, axis=1), 0.0)
        k *= 2
    return x


def _dtf_fwd(dtf_t, dt_bias, a_log, f_bias):
    S = dtf_t.shape[1]

    def body(x_ref, db_ref, al_ref, fb_ref, dt_ref, acs_ref, cum_ref):
        dt = _softplus(x_ref[0:16, :] + db_ref[...])
        dt_ref[...] = dt
        acs_ref[...] = _prefix_sum(dt * (-jnp.exp(al_ref[...])), CHUNK)
        cum_ref[...] = _prefix_sum(-_softplus(-(x_ref[16:32, :] + fb_ref[...])), S)

    return pl.pallas_call(body, name="dtf_fwd", out_shape=[SDS((16, S), F32)] * 3, compiler_params=_cparams())(
        dtf_t, dt_bias, a_log, f_bias)


def _dtf_bwd(dtf_t, dt_bias, a_log, f_bias, d_dt, d_acs_a, d_acs_b, d_cum):
    S = dtf_t.shape[1]

    def body(x_ref, db_ref, al_ref, fb_ref, ddt_ref, da1_ref, da2_ref, dc_ref, dx_ref, ddb_ref, dal_ref, dfb_ref):
        xd = x_ref[0:16, :] + db_ref[...]
        dt = _softplus(xd)
        a = -jnp.exp(al_ref[...])
        d_da = _suffix_sum(da1_ref[...] + da2_ref[...], CHUNK)
        d_dt = ddt_ref[...] + d_da * a
        dal_ref[...] = jnp.sum(d_da * dt, axis=1, keepdims=True) * a
        d_xd = d_dt * jax.nn.sigmoid(xd)
        ddb_ref[...] = jnp.sum(d_xd, axis=1, keepdims=True)
        xf = x_ref[16:32, :] + fb_ref[...]
        d_xf = _suffix_sum(dc_ref[...], S) * jax.nn.sigmoid(-xf)
        dfb_ref[...] = jnp.sum(d_xf, axis=1, keepdims=True)
        dx_ref[0:16, :] = d_xd
        dx_ref[16:32, :] = d_xf

    return pl.pallas_call(body, name="dtf_bwd", out_shape=[SDS((32, S), F32)] + [SDS((16, 1), F32)] * 3,
                          compiler_params=_cparams())(dtf_t, dt_bias, a_log, f_bias, d_dt, d_acs_a, d_acs_b, d_cum)


SSM_PAIRS = SSM_HEADS // 2 // SSM_GROUPS


def _ssd_pair(xs, dtc, acol, arow, bm, cm, cbm, h, hp):
    L = CHUNK
    first = _iota((1, LANES), 1) < SSM_P
    i16, s16 = _iota((L, 16), 1), _iota((16, L), 0)
    ha, hb = 2 * hp, 2 * hp + 1

    def selc(blk, hh):
        return jnp.sum(jnp.where(i16 == hh, blk, 0.0), axis=1, keepdims=True)

    def selr(blk, hh):
        return jnp.sum(jnp.where(s16 == hh, blk, 0.0), axis=0, keepdims=True)

    x = xs * jnp.where(first, selc(dtc, ha), selc(dtc, hb))
    ca, cb, ra, rb = selc(acol, ha), selc(acol, hb), selr(arow, ha), selr(arow, hb)
    tri = _iota((L, L), 0) >= _iota((L, L), 1)
    la = jnp.exp(jnp.where(tri, ca - ra, NEG))
    lb = jnp.exp(jnp.where(tri, cb - rb, NEG))
    y = jnp.where(first, mm_nn(cbm * la, x), mm_nn(cbm * lb, x))
    y = y + jnp.where(first, jnp.exp(ca), jnp.exp(cb)) * mm_nn(cm, h)
    last = _iota((1, L), 1) == L - 1
    ala = jnp.sum(jnp.where(last, ra, 0.0), axis=1, keepdims=True)
    alb = jnp.sum(jnp.where(last, rb, 0.0), axis=1, keepdims=True)
    dec = jnp.where(first, jnp.exp(ala - ca), jnp.exp(alb - cb))
    hn = jnp.where(first, jnp.exp(ala), jnp.exp(alb)) * h + mm_tn(bm, x * dec)
    return y, hn


def _ssd_group(*args, grp):
    xs, (dtc, acol, arow, bm, cm), hs = args[:SSM_PAIRS], args[SSM_PAIRS:SSM_PAIRS + 5], args[SSM_PAIRS + 5:]
    cbm = mm_nt(cm, bm)
    res = [_ssd_pair(xs[j], dtc, acol, arow, bm, cm, cbm, hs[j], SSM_PAIRS * grp + j) for j in range(SSM_PAIRS)]
    return tuple(r[0] for r in res) + tuple(r[1] for r in res)


def _ssd_specs(nc, rev):
    L = CHUNK
    cidx = (lambda c: nc - 1 - c) if rev else (lambda c: c)
    return dict(
        xs=pl.BlockSpec((L, SSM_PAIRS * LANES), lambda c, g: (cidx(c), g)),
        col=pl.BlockSpec((L, 16), lambda c, g: (cidx(c), 0)),
        row=pl.BlockSpec((16, L), lambda c, g: (0, cidx(c))),
        b=pl.BlockSpec((L, SSM_N), lambda c, g: (cidx(c), g)),
        c=pl.BlockSpec((L, SSM_N), lambda c, g: (cidx(c), SSM_GROUPS + g)),
        st=pl.BlockSpec((1, SSM_PAIRS, SSM_N, LANES), lambda c, g: (cidx(c), g, 0, 0)),
    )


def _lane_pieces(v):
    return [v[:, LANES * j:LANES * (j + 1)] for j in range(v.shape[1] // LANES)]


def _ssd_fwd(xs, dt_col, acs_col, acs_row, bc):
    S = xs.shape[0]
    nc, nhp = S // CHUNK, SSM_HEADS // 2
    sp = _ssd_specs(nc, False)

    def body(xs_ref, dt_ref, ac_ref, ar_ref, b_ref, c_ref, y_ref, hs_ref, h_scr):
        c, g = pl.program_id(0), pl.program_id(1)

        @pl.when(c == 0)
        def _():
            for j in range(SSM_PAIRS):
                h_scr[SSM_PAIRS * g + j] = jnp.zeros((SSM_N, LANES), F32)

        hs = [h_scr[SSM_PAIRS * g + j] for j in range(SSM_PAIRS)]
        for j in range(SSM_PAIRS):
            hs_ref[0, j] = hs[j]
        res = _ssd_group(*_lane_pieces(xs_ref[...]), dt_ref[...], ac_ref[...], ar_ref[...], b_ref[...], c_ref[...], *hs,
                         grp=g)
        y_ref[...] = jnp.concatenate(res[:SSM_PAIRS], axis=1)
        for j in range(SSM_PAIRS):
            h_scr[SSM_PAIRS * g + j] = res[SSM_PAIRS + j]

    return pl.pallas_call(
        body, name="ssd_fwd", grid=(nc, SSM_GROUPS),
        in_specs=[sp["xs"], sp["col"], sp["col"], sp["row"], sp["b"], sp["c"]],
        out_specs=[sp["xs"], sp["st"]],
        out_shape=[SDS((S, SSM_HEADS * SSM_P), F32), SDS((nc, nhp, SSM_N, LANES), F32)],
        scratch_shapes=[pltpu.VMEM((nhp, SSM_N, LANES), F32)],
        compiler_params=_cparams(dimension_semantics=("arbitrary", "arbitrary")),
    )(xs, dt_col, acs_col, acs_row, bc, bc)


def _ssd_bwd(xs, dt_col, acs_col, acs_row, bc, hs, dy):
    S = xs.shape[0]
    nc, nhp = S // CHUNK, SSM_HEADS // 2
    sp = _ssd_specs(nc, True)

    def body(xs_ref, dt_ref, ac_ref, ar_ref, b_ref, c_ref, hs_ref, dy_ref,
             dxs_ref, ddt_ref, dac_ref, dar_ref, db_ref, dc_ref, dh_scr):
        c, g = pl.program_id(0), pl.program_id(1)

        @pl.when(c == 0)
        def _():
            for j in range(SSM_PAIRS):
                dh_scr[SSM_PAIRS * g + j] = jnp.zeros((SSM_N, LANES), F32)

        _, vjp = jax.vjp(functools.partial(_ssd_group, grp=g), *_lane_pieces(xs_ref[...]), dt_ref[...], ac_ref[...],
                         ar_ref[...], b_ref[...], c_ref[...], *[hs_ref[0, j] for j in range(SSM_PAIRS)])
        grads = vjp(tuple(_lane_pieces(dy_ref[...])) + tuple(dh_scr[SSM_PAIRS * g + j] for j in range(SSM_PAIRS)))
        dxs_ref[...] = jnp.concatenate(grads[:SSM_PAIRS], axis=1)
        ddt, dac, dar, db, dc = grads[SSM_PAIRS:SSM_PAIRS + 5]
        for j in range(SSM_PAIRS):
            dh_scr[SSM_PAIRS * g + j] = grads[SSM_PAIRS + 5 + j]
        db_ref[...] = db
        dc_ref[...] = dc

        @pl.when(g == 0)
        def _():
            ddt_ref[...] = ddt
            dac_ref[...] = dac
            dar_ref[...] = dar

        @pl.when(g > 0)
        def _():
            ddt_ref[...] += ddt
            dac_ref[...] += dac
            dar_ref[...] += dar

    return pl.pallas_call(
        body, name="ssd_bwd", grid=(nc, SSM_GROUPS),
        in_specs=[sp["xs"], sp["col"], sp["col"], sp["row"], sp["b"], sp["c"], sp["st"], sp["xs"]],
        out_specs=[sp["xs"], sp["col"], sp["col"], sp["row"], sp["b"], sp["b"]],
        out_shape=[SDS((S, SSM_HEADS * SSM_P), F32), SDS((S, 16), F32), SDS((S, 16), F32), SDS((16, S), F32),
                   SDS((S, SSM_GROUPS * SSM_N), F32), SDS((S, SSM_GROUPS * SSM_N), F32)],
        scratch_shapes=[pltpu.VMEM((nhp, SSM_N, LANES), F32)],
        compiler_params=_cparams(dimension_semantics=("arbitrary", "arbitrary")),
    )(xs, dt_col, acs_col, acs_row, bc, bc, hs, dy)


ATT_T = 512


def _pick_col(blk, h):
    return jnp.sum(jnp.where(_iota(blk.shape, 1) == h, blk, 0.0), axis=1, keepdims=True)


def _pick_row(blk, h):
    return jnp.sum(jnp.where(_iota(blk.shape, 0) == h, blk, 0.0), axis=0, keepdims=True)


def _pair_norm(x, g2, first):
    x2 = x * x
    sa = jnp.sum(jnp.where(first, x2, 0.0), axis=1, keepdims=True)
    sb = jnp.sum(jnp.where(first, 0.0, x2), axis=1, keepdims=True)
    r = jnp.where(first, lax.rsqrt(sa * (1.0 / ATT_D) + EPS), lax.rsqrt(sb * (1.0 / ATT_D) + EPS))
    return x * r * g2, r


def _pair_norm_bwd(dxn, x, r, g2, first):
    t = dxn * g2
    tx = t * x
    ma = jnp.sum(jnp.where(first, tx, 0.0), axis=1, keepdims=True)
    mb = jnp.sum(jnp.where(first, 0.0, tx), axis=1, keepdims=True)
    dx = r * (t - x * (r * r) * (jnp.where(first, ma, mb) * (1.0 / ATT_D)))
    return dx, jnp.sum(dxn * x * r, axis=0, keepdims=True)


def _fox_fwd(src, q_c0, k_c0, v_c0, gq2, gk2, cum_col, cum_row3):
    S = src.shape[0]
    T = ATT_T
    nq, nhp = S // T, ATT_HEADS // 2
    qb0, kb0, vb0 = q_c0 // LANES, k_c0 // LANES, v_c0 // LANES
    scale = ATT_D ** -0.5

    def body(q_ref, kraw_ref, v_ref, gq_ref, gk_ref, cc_ref, cr_ref, o_ref, l_ref, k_ref):
        hp, i = pl.program_id(0), pl.program_id(1)
        first = _iota((1, LANES), 1) < ATT_D

        @pl.when(i == 0)
        def _():
            k_ref[...] = _pair_norm(kraw_ref[...], gk_ref[...], first)[0].astype(BF16)

        q = (_pair_norm(q_ref[...], gq_ref[...], first)[0] * scale).astype(BF16)
        zero = jnp.zeros_like(q)
        qs = (jnp.where(first, q, zero), jnp.where(first, zero, q))
        cc = cc_ref[...]
        cq = (_pick_col(cc, 2 * hp), _pick_col(cc, 2 * hp + 1))
        tri = _iota((T, T), 0) >= _iota((T, T), 1)

        def tile(j, carry, diagonal):
            off = pl.multiple_of(j * T, T)
            k = k_ref[pl.ds(off, T), :]
            v = v_ref[pl.ds(off, T), :].astype(BF16)
            cr = cr_ref[j]
            out = []
            for hh in range(2):
                m, l, acc = carry[3 * hh:3 * hh + 3]
                s = _bdot(qs[hh], k, _NT) + (cq[hh] - _pick_row(cr, 2 * hp + hh))
                if diagonal:
                    s = jnp.where(tri, s, NEG)
                m_new = jnp.maximum(m, jnp.max(s, axis=1, keepdims=True))
                alpha = jnp.exp(m - m_new)
                p = jnp.exp(s - m_new)
                out += [m_new, alpha * l + jnp.sum(p, axis=1, keepdims=True), alpha * acc + _bdot(p, v, _NN)]
            return tuple(out)

        init = (jnp.full((T, 1), NEG, F32), jnp.zeros((T, 1), F32), jnp.zeros((T, LANES), F32)) * 2
        carry = lax.fori_loop(0, i, lambda j, c: tile(j, c, False), init)
        ma, la, acca, mb, lb, accb = tile(i, carry, True)
        o_ref[...] = jnp.where(first, acca / la, accb / lb).astype(o_ref.dtype)
        l_ref[...] = jnp.where(first, ma + jnp.log(la), mb + jnp.log(lb))

    gain = pl.BlockSpec((1, LANES), lambda hp, i: (0, 0))
    return pl.pallas_call(
        body, name="fox_fwd", grid=(nhp, nq),
        in_specs=[pl.BlockSpec((T, LANES), lambda hp, i: (i, qb0 + hp)), pl.BlockSpec((S, LANES), lambda hp, i: (0, kb0 + hp)),
                  pl.BlockSpec((S, LANES), lambda hp, i: (0, vb0 + hp)), gain, gain,
                  pl.BlockSpec((T, 16), lambda hp, i: (i, 0)), pl.BlockSpec((nq, 16, T), lambda hp, i: (0, 0, 0))],
        out_specs=[pl.BlockSpec((T, LANES), lambda hp, i: (i, hp))] * 2,
        out_shape=[SDS((S, ATT_HEADS * ATT_D), BF16), SDS((S, ATT_HEADS * ATT_D), F32)],
        scratch_shapes=[pltpu.VMEM((S, LANES), BF16)],
        compiler_params=_cparams(dimension_semantics=("arbitrary", "arbitrary")),
    )(src, src, src, gq2, gk2, cum_col, cum_row3)


def _fox_bwd(src, q_c0, k_c0, v_c0, gq2, gk2, cum_col, cum_row3, lse, dsrc, d_c0):
    S = src.shape[0]
    T = ATT_T
    nq, nhp = S // T, ATT_HEADS // 2
    qb0, kb0, vb0, db0 = q_c0 // LANES, k_c0 // LANES, v_c0 // LANES, d_c0 // LANES
    scale = ATT_D ** -0.5

    def body(q_ref, kraw_ref, v_ref, gq_ref, gk_ref, cc_ref, cr_ref, l_ref, do_ref,
             dq_ref, dk_ref, dv_ref, dc_ref, dg_ref, k_ref, dk_acc, dv_acc, p_scr, dp_scr):
        hp, i = pl.program_id(0), pl.program_id(1)
        first = _iota((1, LANES), 1) < ATT_D
        tri = _iota((T, T), 0) >= _iota((T, T), 1)

        @pl.when(i == 0)
        def _():
            k_ref[...] = _pair_norm(kraw_ref[...], gk_ref[...], first)[0].astype(BF16)
            dk_acc[...] = jnp.zeros_like(dk_acc)
            dv_acc[...] = jnp.zeros_like(dv_acc)
            dc_ref[...] = jnp.zeros_like(dc_ref)
            dg_ref[...] = jnp.zeros_like(dg_ref)

        q_raw = q_ref[...]
        qn, rq = _pair_norm(q_raw, gq_ref[...], first)
        q = (qn * scale).astype(BF16)
        zq = jnp.zeros_like(q)
        dob = do_ref[...].astype(BF16)
        zd = jnp.zeros_like(dob)
        lse_blk, cc = l_ref[...], cc_ref[...]
        dq = jnp.zeros((T, LANES), F32)
        for hh in range(2):
            sel = first if hh == 0 else jnp.logical_not(first)
            qh, doh = jnp.where(sel, q, zq), jnp.where(sel, dob, zd)
            bias_q = _pick_col(cc, 2 * hp + hh) - jnp.max(jnp.where(sel, lse_blk, NEG), axis=1, keepdims=True)

            def probs(j, delta, diagonal):
                off = pl.multiple_of(j * T, T)
                s = _bdot(qh, k_ref[pl.ds(off, T), :], _NT) + (bias_q - _pick_row(cr_ref[j], 2 * hp + hh))
                if diagonal:
                    s = jnp.where(tri, s, NEG)
                p = jnp.exp(s)
                dp = _bdot(doh, v_ref[pl.ds(off, T), :], _NT)
                p_scr[j] = p
                dp_scr[j] = dp
                return delta + jnp.sum(p * dp, axis=1, keepdims=True)

            delta = lax.fori_loop(0, i, lambda j, d: probs(j, d, False), jnp.zeros((T, 1), F32))
            delta = probs(i, delta, True)

            def grads(j, dq):
                off = pl.multiple_of(j * T, T)
                p = p_scr[j]
                ds = p * (dp_scr[j] - delta)
                dv_acc[pl.ds(off, T), :] += _bdot(p, doh, _TN)
                dk_acc[pl.ds(off, T), :] += _bdot(ds, qh, _TN)
                dc_ref[0, j, hh:hh + 1, :] -= jnp.sum(ds, axis=0, keepdims=True)
                zk = jnp.zeros((T, LANES), BF16)
                return dq + _bdot(ds, jnp.where(sel, k_ref[pl.ds(off, T), :], zk), _NN)

            dq = lax.fori_loop(0, i + 1, grads, dq)
        dq_raw, dgq = _pair_norm_bwd(dq * scale, q_raw, rq, gq_ref[...], first)
        dq_ref[...] = dq_raw.astype(dq_ref.dtype)
        dg_ref[0, 0:1, :] += dgq

        @pl.when(i == nq - 1)
        def _():
            k_raw = kraw_ref[...]
            rk = _pair_norm(k_raw, gk_ref[...], first)[1]
            dk_raw, dgk = _pair_norm_bwd(dk_acc[...], k_raw, rk, gk_ref[...], first)
            dk_ref[...] = dk_raw.astype(dk_ref.dtype)
            dv_ref[...] = dv_acc[...].astype(dv_ref.dtype)
            dg_ref[0, 1:2, :] = dgk

    gain = pl.BlockSpec((1, LANES), lambda hp, i: (0, 0))
    band = SDS((S, ATT_HEADS * ATT_D), BF16)
    return pl.pallas_call(
        body, name="fox_bwd", grid=(nhp, nq),
        in_specs=[pl.BlockSpec((T, LANES), lambda hp, i: (i, qb0 + hp)), pl.BlockSpec((S, LANES), lambda hp, i: (0, kb0 + hp)),
                  pl.BlockSpec((S, LANES), lambda hp, i: (0, vb0 + hp)), gain, gain,
                  pl.BlockSpec((T, 16), lambda hp, i: (i, 0)), pl.BlockSpec((nq, 16, T), lambda hp, i: (0, 0, 0)),
                  pl.BlockSpec((T, LANES), lambda hp, i: (i, hp)), pl.BlockSpec((T, LANES), lambda hp, i: (i, db0 + hp))],
        out_specs=[pl.BlockSpec((T, LANES), lambda hp, i: (i, hp)), pl.BlockSpec((S, LANES), lambda hp, i: (0, hp)),
                   pl.BlockSpec((S, LANES), lambda hp, i: (0, hp)), pl.BlockSpec((1, nq, 8, T), lambda hp, i: (hp, 0, 0, 0)),
                   pl.BlockSpec((1, 8, LANES), lambda hp, i: (hp, 0, 0))],
        out_shape=[band, band, band, SDS((nhp, nq, 8, T), F32), SDS((nhp, 8, LANES), F32)],
        scratch_shapes=[pltpu.VMEM((S, LANES), BF16), pltpu.VMEM((S, LANES), F32), pltpu.VMEM((S, LANES), F32),
                        pltpu.VMEM((nq, T, T), F32), pltpu.VMEM((nq, T, T), F32)],
        compiler_params=_cparams(dimension_semantics=("arbitrary", "arbitrary")),
    )(src, src, src, gq2, gk2, cum_col, cum_row3, lse, dsrc)


def _fold_gains(dg):
    def body(d_ref, o_ref):
        t = d_ref[0]
        for h in range(1, dg.shape[0]):
            t = t + d_ref[h]
        o_ref[...] = t + pltpu.roll(t, ATT_D, axis=1)

    return pl.pallas_call(body, name="fold_gains", out_shape=SDS(dg.shape[1:], F32), compiler_params=_cparams())(dg)


def _loss_head(y, target, *, tm):
    M, W = y.shape

    def body(y_ref, t_ref, dy_ref, loss_ref):
        @pl.when(pl.program_id(0) == 0)
        def _():
            loss_ref[...] = jnp.zeros_like(loss_ref)

        e = y_ref[...] - t_ref[...]
        dy_ref[...] = e * (1.0 / W)
        loss_ref[...] += jnp.sum(jnp.sum(e * e, axis=1, keepdims=True), axis=0, keepdims=True) * (0.5 / W)

    return pl.pallas_call(
        body, name="loss_head", grid=(M // tm,),
        in_specs=[pl.BlockSpec((tm, W), lambda i: (i, 0))] * 2,
        out_specs=[pl.BlockSpec((tm, W), lambda i: (i, 0)), pl.BlockSpec((1, 1), lambda i: (0, 0))],
        out_shape=[SDS((M, W), F32), SDS((1, 1), F32)],
        compiler_params=_cparams(dimension_semantics=("arbitrary",)),
    )(y, target)


def _adamw_math(w, g, m, v):
    m = ADAM_B1 * m + (1.0 - ADAM_B1) * g
    v = ADAM_B2 * v + (1.0 - ADAM_B2) * jnp.square(g)
    m_hat = m / (1.0 - ADAM_B1 ** ADAM_STEP)
    v_hat = v / (1.0 - ADAM_B2 ** ADAM_STEP)
    delta = -ADAM_LR * (m_hat / (jnp.sqrt(v_hat) + ADAM_EPS) + ADAM_WD * w)
    return delta, m, v


def _reduce_adamw(parts, w, m, v, *, tr, name, tc=None):
    R, C = w.shape
    tr, tc = min(tr, R), tc or C
    nparts = parts.shape[0]

    def body(p_ref, w_ref, m_ref, v_ref, g_ref, d_ref, nm_ref, nv_ref):
        g = p_ref[0].astype(F32)
        for s in range(1, nparts):
            g = g + p_ref[s].astype(F32)
        g_ref[...] = g
        d_ref[...], nm_ref[...], nv_ref[...] = _adamw_math(w_ref[...], g, m_ref[...], v_ref[...])

    blk = pl.BlockSpec((tr, tc), lambda i, j: (i, j))
    return pl.pallas_call(
        body, name=name, grid=(R // tr, C // tc),
        in_specs=[pl.BlockSpec((nparts, tr, tc), lambda i, j: (0, i, j)), blk, blk, blk], out_specs=[blk] * 4,
        out_shape=[SDS((R, C), F32)] * 4, compiler_params=_cparams(dimension_semantics=("parallel", "parallel")),
    )(parts, w, m, v)


def _adamw(w, g, m, v, *, name):
    def body(w_ref, g_ref, m_ref, v_ref, d_ref, nm_ref, nv_ref):
        d_ref[...], nm_ref[...], nv_ref[...] = _adamw_math(w_ref[...], g_ref[...], m_ref[...], v_ref[...])

    return pl.pallas_call(body, name=name, out_shape=[SDS(w.shape, F32)] * 3, compiler_params=_cparams())(w, g, m, v)


def _peers():
    x, y, c = lax.axis_index("x"), lax.axis_index("y"), lax.axis_index("c")
    out = []
    for k in range(1, N_DEV):
        px, py, pc = x ^ ((k >> 2) & 1), y ^ ((k >> 1) & 1), c ^ (k & 1)
        out.append(((px, py, pc), 4 * px + 2 * py + pc))
    return 4 * x + 2 * y + c, out


_HBM = pl.BlockSpec(memory_space=pltpu.HBM)
_SEM = pl.BlockSpec(memory_space=pltpu.SEMAPHORE)
_DATAFLOW = pltpu.SideEffectType.DATAFLOW_SIDE_EFFECTING


NEAR = (1, 2, 4, 6)


def _plan_peers(scatter, ks=tuple(range(1, N_DEV))):
    return lambda me, peers: [(peers[k - 1][0], peers[k - 1][1] if scatter else None, me, k - 1) for k in ks]


def _plan_relay(me, peers):
    return [(peers[0][0], peers[k - 1][1], peers[k - 1][1], j) for j, k in enumerate((2, 4, 6))]


def _plan_pair(me, peers):
    return [(peers[0][0], peers[k - 1][1], j, j) for j, k in enumerate((1, 3, 5, 7))]


def _plan_chips(me, peers):
    return [(peers[k - 1][0], k // 2, k // 2, k // 2) for k in (2, 4, 6)]


def _copy(src, dst, c, send_sems, recv_sems):
    dev, s_slot, d_slot, i = c
    return pltpu.make_async_remote_copy(
        src_ref=src if s_slot is None else src.at[s_slot], dst_ref=dst.at[d_slot], send_sem=send_sems.at[i],
        recv_sem=recv_sems.at[i], device_id=dev, device_id_type=MESH)


def _copies_start(items, *, name):
    n = len(items)
    bufs = [it[0] for it in items] + [it[1] for it in items if it[1] is not None]
    nb = len(bufs)

    def body(*refs):
        srcs, extra, sems, token = refs[:n], iter(refs[n:nb]), refs[nb:nb + 2 * n], refs[-1]
        me, peers = _peers()
        for a, (_, land, plan) in enumerate(items):
            dst = srcs[a] if land is None else next(extra)
            for c in plan(me, peers):
                _copy(srcs[a], dst, c, sems[2 * a], sems[2 * a + 1]).start()
        token[...] = jnp.zeros_like(token)

    res = pl.pallas_call(
        body, name=name,
        out_shape=([pltpu.SemaphoreType.DMA((N_DEV - 1,))] * (2 * n) + [pltpu.HBM(b.shape, b.dtype) for b in bufs]
                   + [SDS((8, LANES), F32)]),
        in_specs=[_HBM] * nb, out_specs=[_SEM] * (2 * n) + [_HBM] * nb + [pl.BlockSpec(memory_space=pltpu.VMEM)],
        input_output_aliases={i: 2 * n + i for i in range(nb)},
        compiler_params=pltpu.CompilerParams(has_side_effects=_DATAFLOW),
    )(*[pltpu.with_memory_space_constraint(b, pltpu.HBM) for b in bufs])
    sems, thru, token = res[:2 * n], list(res[2 * n:2 * n + nb]), res[-1]
    extra = iter(thru[n:])
    return [(thru[a], None if it[1] is None else next(extra), sems[2 * a], sems[2 * a + 1], it[2])
            for a, it in enumerate(items)], token


def _copies_wait(handles, after, *, name):
    n = len(handles)
    bufs = [h[0] for h in handles] + [h[1] for h in handles if h[1] is not None]
    nb = len(bufs)

    def body(*refs):
        srcs, extra, sems = refs[:n], iter(refs[n:nb]), refs[nb:nb + 2 * n]
        me, peers = _peers()
        for a, h in enumerate(handles):
            dst = srcs[a] if h[1] is None else next(extra)
            for c in h[4](me, peers):
                cp = _copy(srcs[a], dst, c, sems[2 * a], sems[2 * a + 1])
                cp.wait_send()
                cp.wait_recv()

    flat_sems = [s for h in handles for s in (h[2], h[3])]
    res = pl.pallas_call(
        body, name=name, out_shape=[pltpu.HBM(b.shape, b.dtype) for b in bufs],
        in_specs=[_HBM] * nb + [_SEM] * (2 * n) + [pl.BlockSpec(memory_space=pl.ANY)], out_specs=[_HBM] * nb,
        input_output_aliases={i: i for i in range(nb)},
        compiler_params=pltpu.CompilerParams(has_side_effects=_DATAFLOW),
    )(*bufs, *flat_sems, after)
    extra = iter(res[n:])
    return [(res[a], res[a] if h[1] is None else next(extra)) for a, h in enumerate(handles)]


def _exchange_start(arrays, *, scatter, name, near=()):
    items = []
    for a, arr in enumerate(arrays):
        land = lax.empty(arr.shape if scatter else (N_DEV,) + arr.shape, arr.dtype)
        items.append((arr, land, _plan_peers(scatter, NEAR) if a in near else _plan_peers(scatter)))
    return _copies_start(items, name=name)


def _pair_sum(a, b, *, name):
    n, R, C = a.shape
    tc = 256

    def body(a_ref, b_ref, o_ref):
        o_ref[...] = (a_ref[...].astype(F32) + b_ref[...].astype(F32)).astype(o_ref.dtype)

    blk = pl.BlockSpec((1, R, tc), lambda i, j: (i, 0, j))
    return pl.pallas_call(body, name=name, grid=(n, C // tc), in_specs=[blk, blk], out_specs=blk,
                          out_shape=SDS(a.shape, a.dtype), compiler_params=_cparams(dimension_semantics=("parallel", "parallel")))(a, b)


def _own_slot(landed, own, me):
    return lax.dynamic_update_slice(landed, own[None], (me,) + (0,) * own.ndim)


SMALL = (("g_mix", 1024), ("conv_w", 6144), ("conv_b", 1536), ("dt_bias", 16), ("a_log", 16), ("d_skip", 16),
         ("ssm_norm_w", 1024), ("g_q", 64), ("g_k", 64), ("f_bias", 16), ("g_xattn", 1024), ("g_mem", 1024),
         ("xg_q", 256), ("xg_k", 256), ("g_mlp", 1024))
SLAB_ROWS = 112
BIG = ("w_in", "w_out", "xq_w", "xkv_w", "xo_w", "w_up", "w_down")
WEIGHTS = ("g_mix", "w_in", "conv_w", "conv_b", "dt_bias", "a_log", "d_skip", "ssm_norm_w", "g_q", "g_k", "f_bias", "w_out",
           "g_xattn", "g_mem", "xq_w", "xkv_w", "xg_q", "xg_k", "xo_w", "g_mlp", "w_up", "w_down")
O_Z, O_XS, O_B, O_C, O_DT, O_Q, O_K, O_V, O_F, O_END = 0, 1024, 2048, 2304, 2560, 2576, 3600, 4624, 5648, 5664


def _pack_small(vals):
    rows = []
    for name, size in SMALL:
        flat = vals[name].reshape(-1).astype(F32)
        pad = -size % LANES
        rows.append(jnp.pad(flat, (0, pad)).reshape(-1, LANES))
    slab = jnp.concatenate(rows, axis=0)
    return jnp.pad(slab, ((0, SLAB_ROWS - slab.shape[0]), (0, 0)))


def _unpack_small(slab):
    out, r = {}, 0
    for name, size in SMALL:
        nr = -(-size // LANES)
        out[name] = slab[r:r + nr].reshape(-1)[:size]
        r += nr
    return out


def _cols(a, lo, hi):
    return a[:, lo:hi]


def _step(p, m, v, x, mem, target):
    S = x.shape[0]
    TM = 256
    me = 4 * lax.axis_index("x") + 2 * lax.axis_index("y") + lax.axis_index("c")

    def rms(u, g, name):
        return _rw_fwd(_rms_fn, [_whole(u)], [_whole(g)], [(D_MODEL, BF16)], tm=TM, name=name)[0]

    def pin(param, token):
        return param + token[0:1, 0:1]

    def landed_with_own(pairs, scatter):
        out = []
        for src, land in pairs:
            own = lax.dynamic_index_in_dim(src, me, 0, keepdims=False) if scatter else src
            out.append(_own_slot(land, own, me))
        return out

    w_in_own, m_in_own, v_in_own = p["w_in"].T, m["w_in"].T, v["w_in"].T
    ag, ag_token = _exchange_start([w_in_own.astype(BF16), p["conv_w"]] + [p[n].astype(BF16) for n in BIG[1:]],
                                   scatter=False, name="allgather_start", near=(0,))
    h1 = rms(x, pin(p["g_mix"], ag_token), "rms_mix")
    (win_src, win_land), convw_pair = _copies_wait(ag[:2], h1, name="allgather_wait_in")
    relay, token = _copies_start([(win_land, None, _plan_relay)], name="allgather_relay_start")
    win_land = _copies_wait(relay, token, name="allgather_relay_wait")[0][1]
    win_g, convw_g = landed_with_own([(win_src, win_land), convw_pair], False)
    w_in_o = win_g.reshape(O_END, D_MODEL)
    w_in_t = jnp.concatenate(
        [w_in_o[O_Z:O_XS], w_in_o[O_XS:O_B], w_in_o[O_Q:O_K], w_in_o[O_K:O_V], w_in_o[O_V:O_F], w_in_o[O_B:O_C],
         w_in_o[O_C:O_DT], w_in_o[O_DT:O_Q], w_in_o[O_F:O_END], jnp.zeros((P_COLS - C_DTF - 32, D_MODEL), BF16)], axis=0)
    conv_w = convw_g.transpose(1, 0, 2).reshape(4, 1536)
    cw_xs, cw_bc = conv_w[:, :1024], conv_w[:, 1024:]
    cb_xs, cb_bc = p["conv_b"][:, :1024], p["conv_b"][:, 1024:]
    dt_bias, a_log, f_bias = p["dt_bias"].reshape(16, 1), p["a_log"].reshape(16, 1), p["f_bias"].reshape(16, 1)

    proj = _matmul(h1, w_in_t, mode="nt", tm=1024, tn=640, tk=1024, name="mm_in")
    xs_c = _conv_fwd(proj, C_XS, 1024, cw_xs, cb_xs, name="conv_xs")
    bc_c = _conv_fwd(proj, C_B, 512, cw_bc, cb_bc, name="conv_bc")
    dtf_t = proj[:, C_DTF:C_DTF + 32].T
    dt_t, acs_t, cum_t = _dtf_fwd(dtf_t, dt_bias, a_log, f_bias)
    dt_col, acs_col, cum_col = dt_t.T, acs_t.T, cum_t.T
    cum_row3 = cum_t.reshape(16, S // ATT_T, ATT_T).transpose(1, 0, 2)
    y_ssd, hs = _ssd_fwd(xs_c, dt_col, acs_col, acs_t, bc_c)
    gate_rows = [_whole(y_ssd), _whole(xs_c), (proj, C_Z, 1024)]
    gate_pars = [_whole(p["d_skip"]), _whole(p["ssm_norm_w"])]
    y_ssm = _rw_fwd(_gate_fn, gate_rows, gate_pars, [(1024, BF16)], tm=TM, name="gate")[0]
    gq2, gk2 = jnp.tile(p["g_q"], (1, 2)), jnp.tile(p["g_k"], (1, 2))
    o, lse = _fox_fwd(proj, C_Q, C_K, C_V, gq2, gk2, cum_col, cum_row3)
    mixed = jnp.concatenate([y_ssm, o], axis=1)
    wout_g, xq_g, xkv_g, xo_g, wup_g, wdown_g = landed_with_own(
        _copies_wait(ag[2:], mixed, name="allgather_wait_rest"), False)
    w_out = wout_g.reshape(2 * D_MODEL, D_MODEL)
    xq_w = xq_g.reshape(D_MODEL, D_MODEL)
    xkv_w = xkv_g.transpose(1, 0, 2).reshape(D_MODEL, 2 * D_MODEL)
    xo_w = xo_g.reshape(D_MODEL, D_MODEL)
    w_up = wup_g.transpose(1, 0, 2).reshape(D_MODEL, 4 * D_MODEL)
    w_down = wdown_g.reshape(4 * D_MODEL, D_MODEL)
    x1 = _matmul(mixed, w_out, mode="nn", tm=1024, tn=512, tk=2048, add=x, name="mm_out")

    h2 = rms(x1, p["g_xattn"], "rms_xattn")
    mem_n = rms(mem, p["g_mem"], "rms_mem")
    q2 = _matmul(h2, xq_w, mode="nn", tm=1024, tn=512, tk=1024, name="mm_xq")
    kv = _matmul(mem_n, xkv_w, mode="nn", tm=256, tn=1024, tk=1024, name="mm_xkv")
    xa_rows = [(q2, X_D * h, X_D) for h in range(X_HEADS)]
    xa_pars = ([(kv, X_D * h, X_D) for h in range(X_HEADS)] + [(kv, D_MODEL + X_D * h, X_D) for h in range(X_HEADS)]
               + [_whole(p["xg_q"]), _whole(p["xg_k"])])
    o2 = _rw_fwd(_xattn_fn, xa_rows, xa_pars, [(D_MODEL, BF16)], tm=TM, name="xattn")[0]
    x2 = _matmul(o2, xo_w, mode="nn", tm=1024, tn=512, tk=1024, add=x1, name="mm_xo")

    h3 = rms(x2, p["g_mlp"], "rms_mlp")
    a, usq = _matmul(h3, w_up, mode="nn", tm=1024, tn=1024, tk=1024, name="mm_up", out_dtypes=[F32, BF16],
                     epilogue=lambda acc: (acc, jnp.square(jax.nn.relu(acc))))
    x3 = _matmul(usq, w_down, mode="nn", tm=1024, tn=512, tk=2048, add=x2, name="mm_down")
    dy, loss_part = _loss_head(x3, target, tm=TM)
    loss = lax.psum(loss_part[0, 0], ("x", "y", "c"))

    def col_shards(a):
        r, c = a.shape
        return a.reshape(r, N_DEV, c // N_DEV).transpose(1, 0, 2)

    def row_shards(a):
        r, c = a.shape
        return a.reshape(N_DEV, r // N_DEV, c)

    g = {}
    g["w_down"] = _matmul(usq, dy, mode="tn", out_dtype=GRAD_WIRE, tm=1024, tn=1024, tk=1024, name="mm_d_wdown")
    da = _matmul(dy, w_down, mode="nt", tm=1024, tn=1024, tk=1024, name="mm_d_usq", out_dtype=BF16, extras=(a,),
                 epilogue=lambda acc, av: (2.0 * jax.nn.relu(av) * acc,))
    g["w_up"] = _matmul(h3, da, mode="tn", out_dtype=GRAD_WIRE, tm=1024, tn=1024, tk=1024, name="mm_d_wup")
    sent_mlp, token = _exchange_start([row_shards(g["w_down"]), col_shards(g["w_up"])], scatter=True,
                                      name="grads_start_mlp")
    dh3 = _matmul(da, w_up, mode="nt", tm=1024, tn=512, tk=2048, name="mm_d_h3")
    dx2, g["g_mlp"] = _rw_bwd(_rms_fn, [_whole(x2)], [_whole(pin(p["g_mlp"], token))], [_whole(dh3)], tm=TM,
                              name="rms_mlp_bwd", row_grads=[F32], adds={0: _whole(dy)})

    g["xo_w"] = _matmul(o2, dx2, mode="tn", out_dtype=GRAD_WIRE, tm=1024, tn=1024, tk=1024, name="mm_d_wxo")
    do2 = _matmul(dx2, xo_w, mode="nt", tm=1024, tn=512, tk=1024, name="mm_d_o2")
    xa = _rw_bwd(_xattn_fn, xa_rows, xa_pars, [_whole(do2)], tm=TM, name="xattn_bwd", row_grads=[BF16] * X_HEADS)
    dq2 = jnp.concatenate(xa[:X_HEADS], axis=1)
    dkv = jnp.concatenate(xa[X_HEADS:3 * X_HEADS], axis=1)
    g["xg_q"], g["xg_k"] = xa[3 * X_HEADS], xa[3 * X_HEADS + 1]
    g["xq_w"] = _matmul(h2, dq2, mode="tn", out_dtype=GRAD_WIRE, tm=1024, tn=1024, tk=1024, name="mm_d_wxq")
    dh2 = _matmul(dq2, xq_w, mode="nt", tm=1024, tn=512, tk=1024, name="mm_d_h2")
    g["xkv_w"] = _matmul(mem_n, dkv, mode="tn", out_dtype=GRAD_WIRE, tm=1024, tn=1024, tk=256, name="mm_d_wxkv")
    dmem_n = _matmul(dkv, xkv_w, mode="nt", tm=256, tn=1024, tk=2048, name="mm_d_memn")
    g["g_mem"] = _rw_bwd(_rms_fn, [_whole(mem)], [_whole(p["g_mem"])], [_whole(dmem_n)], tm=TM, name="rms_mem_bwd",
                         row_grads=[None])[0]
    dx1, g["g_xattn"] = _rw_bwd(_rms_fn, [_whole(x1)], [_whole(p["g_xattn"])], [_whole(dh2)], tm=TM, name="rms_xattn_bwd",
                                row_grads=[F32], adds={0: _whole(dx2)})

    g["w_out"] = _matmul(mixed, dx1, mode="tn", out_dtype=GRAD_WIRE, tm=1024, tn=1024, tk=1024, name="mm_d_wout")
    sent_mid, token = _exchange_start(
        [row_shards(g["w_out"]), row_shards(g["xq_w"]), col_shards(g["xkv_w"]), row_shards(g["xo_w"])], scatter=True,
        name="grads_start_mid")
    dmixed = _matmul(dx1, w_out, mode="nt", tm=1024, tn=1024, tk=1024, name="mm_d_mixed")
    dq, dk, dv, dcum4, dgain = _fox_bwd(proj, C_Q, C_K, C_V, pin(gq2, token), gk2, cum_col, cum_row3, lse, dmixed, 1024)
    gains = _fold_gains(dgain)
    g["g_q"], g["g_k"] = gains[0:1, :ATT_D], gains[1:2, :ATT_D]
    dy_ssd, dxs_g, dz, g["d_skip"], g["ssm_norm_w"] = _rw_bwd(
        _gate_fn, gate_rows, gate_pars, [(dmixed, 0, 1024)], tm=TM, name="gate_bwd", row_grads=[F32, F32, BF16])
    dxs_s, ddt_col, dacs_col, dacs_row, d_b, d_c = _ssd_bwd(xs_c, dt_col, acs_col, acs_t, bc_c, hs, dy_ssd)
    dcum_t = dcum4[:, :, 0:2, :].transpose(0, 2, 1, 3).reshape(16, S)
    ddtf_t, ddtb, dalog, dfb = _dtf_bwd(dtf_t, dt_bias, a_log, f_bias, ddt_col.T, dacs_col.T, dacs_row, dcum_t)
    g["dt_bias"], g["a_log"], g["f_bias"] = ddtb, dalog, dfb
    dxs_raw, dcw_xs, dcb_xs = _conv_bwd(proj, C_XS, 1024, cw_xs, cb_xs, [dxs_s, dxs_g], name="conv_xs_bwd")
    dbc_raw, dcw_bc, dcb_bc = _conv_bwd(proj, C_B, 512, cw_bc, cb_bc, [jnp.concatenate([d_b, d_c], axis=1)],
                                        name="conv_bc_bwd")
    g["conv_w"] = jnp.concatenate([dcw_xs, dcw_bc], axis=1)
    g["conv_b"] = jnp.concatenate([dcb_xs, dcb_bc], axis=1)
    ddtf = jnp.pad(ddtf_t.T.astype(BF16), ((0, 0), (0, P_COLS - C_DTF - 32)))
    dproj = jnp.concatenate([dz, dxs_raw, dq, dk, dv, dbc_raw, ddtf], axis=1)
    dw_in_p = _matmul(dproj, h1, mode="tn", out_dtype=GRAD_WIRE, tm=640, tn=1024, tk=1024, name="mm_d_win")
    g["w_in"] = jnp.concatenate(
        [dw_in_p[C_Z:C_Q], dw_in_p[C_B:C_DTF + 16], dw_in_p[C_Q:C_B], dw_in_p[C_DTF + 16:C_DTF + 32]], axis=0)
    half = N_DEV // 2
    send_in = row_shards(g["w_in"])
    pair, _ = _copies_start([(send_in, lax.empty((half,) + send_in.shape[1:], send_in.dtype), _plan_pair)],
                            name="grads_in_pair_start")
    dh1 = _matmul(dproj, w_in_t, mode="nn", tm=1024, tn=512, tk=1920, name="mm_d_h1")
    send_in, from_sibling = _copies_wait(pair, dh1, name="grads_in_pair_wait")[0]
    mine = jnp.stack([lax.dynamic_index_in_dim(send_in, me ^ (2 * j), 0, keepdims=False) for j in range(half)])
    chip_sums = _pair_sum(mine, from_sibling, name="grads_in_pair_sum")
    sent_in, token = _copies_start([(chip_sums, lax.empty(chip_sums.shape, chip_sums.dtype), _plan_chips)],
                                   name="grads_in_chip_start")
    grad_x, g["g_mix"] = _rw_bwd(_rms_fn, [_whole(x)], [_whole(pin(p["g_mix"], token))], [_whole(dh1)], tm=TM,
                                 name="rms_mix_bwd", row_grads=[F32], adds={0: _whole(dx1)})
    sent_small, _ = _exchange_start([_pack_small(g)], scatter=False, name="small_grads_start")

    grads, delta, new_m, new_v = {}, {}, {}, {}

    def update(names, sent, after, wait_name):
        parts = landed_with_own(_copies_wait(sent, after, name=wait_name), True)
        for name, part in zip(names, parts, strict=True):
            grads[name], delta[name], new_m[name], new_v[name] = _reduce_adamw(part, p[name], m[name], v[name], tr=128,
                                                                                name="adamw_" + name)

    update(("w_down", "w_up"), sent_mlp, grad_x, "grads_wait_mlp")
    update(("w_out", "xq_w", "xkv_w", "xo_w"), sent_mid, delta["w_up"], "grads_wait_mid")
    chip_sums, landed = _copies_wait(sent_in, delta["xo_w"], name="grads_in_chip_wait")[0]
    part = lax.dynamic_update_slice(landed, chip_sums[0:1], (0, 0, 0))
    res = _reduce_adamw(part, w_in_own, m_in_own, v_in_own, tr=part.shape[1], tc=256, name="adamw_w_in")
    grads["w_in"], delta["w_in"], new_m["w_in"], new_v["w_in"] = [r.T for r in res]
    small_parts = landed_with_own(_copies_wait(sent_small, delta["w_in"], name="small_grads_wait"), False)[0]
    zeros_cw = jnp.zeros((4, 1536), F32)
    slabs = [_pack_small({**d, "conv_w": zeros_cw}) for d in (p, m, v)]
    sg, sd, sm, sv = _reduce_adamw(small_parts, *slabs, tr=SLAB_ROWS, name="adamw_small")
    for dst, slab in ((grads, sg), (delta, sd), (new_m, sm), (new_v, sv)):
        for name, flat in _unpack_small(slab).items():
            if name != "conv_w":
                dst[name] = flat.reshape(p[name].shape)
    cw_shard = p["conv_w"].shape[1]
    grads["conv_w"] = lax.dynamic_slice(_unpack_small(sg)["conv_w"].reshape(4, 1536), (0, me * cw_shard), (4, cw_shard))
    delta["conv_w"], new_m["conv_w"], new_v["conv_w"] = _adamw(p["conv_w"], grads["conv_w"], m["conv_w"], v["conv_w"],
                                                               name="adamw_conv_w")
    return loss, grad_x, grads, delta, new_m, new_v


def kernel(x, mem, g_mix, w_in, conv_w, conv_b, dt_bias, a_log, d_skip, ssm_norm_w, g_q, g_k, f_bias, w_out, g_xattn, g_mem, xq_w, xkv_w, xg_q, xg_k, xo_w, g_mlp, w_up, w_down, loss_target, m_g_mix, m_w_in, m_conv_w, m_conv_b, m_dt_bias, m_a_log, m_d_skip, m_ssm_norm_w, m_g_q, m_g_k, m_f_bias, m_w_out, m_g_xattn, m_g_mem, m_xq_w, m_xkv_w, m_xg_q, m_xg_k, m_xo_w, m_g_mlp, m_w_up, m_w_down, v_g_mix, v_w_in, v_conv_w, v_conv_b, v_dt_bias, v_a_log, v_d_skip, v_ssm_norm_w, v_g_q, v_g_k, v_f_bias, v_w_out, v_g_xattn, v_g_mem, v_xq_w, v_xkv_w, v_xg_q, v_xg_k, v_xo_w, v_g_mlp, v_w_up, v_w_down):
    args = locals()
    drop = lambda t: t[0] if t.ndim == 3 else t
    p = {n: drop(args[n]) for n in WEIGHTS}
    m = {n: drop(args["m_" + n]) for n in WEIGHTS}
    v = {n: drop(args["v_" + n]) for n in WEIGHTS}
    loss, grad_x, grads, delta, new_m, new_v = _step(p, m, v, x[0], mem[0], loss_target[0])
    outs = [loss, grad_x[None]]
    for d in (grads, delta, new_m, new_v):
        outs += [d[n].reshape(args[n].shape) for n in WEIGHTS]
    return tuple(outs)
```

```python
import functools
import math

import jax
import jax.numpy as jnp
from jax import lax
from jax.experimental import pallas as pl
from jax.experimental.pallas import tpu as pltpu

F32, BF16 = jnp.float32, jnp.bfloat16
SDS = jax.ShapeDtypeStruct
HI = lax.Precision.HIGHEST
MESH = pl.DeviceIdType.MESH

N_DEV = 8
EPS = 1e-5
D_MODEL = 1024
SSM_HEADS, SSM_P, SSM_N, SSM_GROUPS, CHUNK = 16, 64, 128, 2, 128
ATT_HEADS, ATT_D = 16, 64
X_HEADS, X_D = 4, 256
LANES = 128
VMEM_LIMIT = 48 * 1024 * 1024
NEG = -1e30

GRAD_WIRE = BF16
ADAM_LR, ADAM_B1, ADAM_B2, ADAM_EPS, ADAM_WD, ADAM_STEP = 0.001, 0.9, 0.999, 1e-08, 0.01, 10

C_Z, C_XS, C_Q, C_K, C_V, C_B, C_C, C_DTF, P_COLS = 0, 1024, 2048, 3072, 4096, 5120, 5376, 5632, 5760

_NN = (((1,), (0,)), ((), ()))
_NT = (((1,), (1,)), ((), ()))
_TN = (((0,), (0,)), ((), ()))


def _cparams(**kw):
    return pltpu.CompilerParams(vmem_limit_bytes=VMEM_LIMIT, **kw)


def _bdot(a, b, dn):
    return lax.dot_general(a.astype(BF16), b.astype(BF16), dn, preferred_element_type=F32)


@jax.custom_vjp
def mm_nn(a, b):
    return _bdot(a, b, _NN)


mm_nn.defvjp(lambda a, b: (mm_nn(a, b), (a, b)), lambda r, g: (_bdot(g, r[1], _NT), _bdot(r[0], g, _TN)))


@jax.custom_vjp
def mm_nt(a, b):
    return _bdot(a, b, _NT)


mm_nt.defvjp(lambda a, b: (mm_nt(a, b), (a, b)), lambda r, g: (_bdot(g, r[1], _NN), _bdot(g, r[0], _TN)))


@jax.custom_vjp
def mm_tn(a, b):
    return _bdot(a, b, _TN)


mm_tn.defvjp(lambda a, b: (mm_tn(a, b), (a, b)), lambda r, g: (_bdot(r[1], g, _NT), _bdot(r[0], g, _NN)))


def _cdot(x, c):
    return jnp.dot(x, c, precision=HI, preferred_element_type=F32)


def _iota(shape, dim):
    return lax.broadcasted_iota(jnp.int32, shape, dim)


def _matmul(a, b, *, mode, tm, tn, tk, name, out_dtype=F32, add=None, extras=(), epilogue=None, out_dtypes=None,
            b_shards=False, out_shards=False):
    if mode == "tn":
        K, M = a.shape
    else:
        M, K = a.shape
    if b_shards:
        N = b.shape[1] if mode == "nt" else b.shape[0] * b.shape[2]
        tn, tk = (tn, b.shape[2]) if mode == "nt" else (b.shape[2], tk)
    else:
        N = b.shape[0] if mode == "nt" else b.shape[1]
    tm, tn, tk = min(tm, M), min(tn, N), min(tk, K)
    assert M % tm == 0 and N % tn == 0 and K % tk == 0, (name, M, N, K, tm, tn, tk)
    assert not b_shards or (K // tk if mode == "nt" else N // tn) == b.shape[0], name
    assert not (out_shards and (extras or add is not None)), name
    nk = K // tk
    dn = {"nn": _NN, "nt": _NT, "tn": _TN}[mode]
    if add is not None:
        extras, epilogue = (add,), lambda acc, r: (acc + r,)
    elif epilogue is None:
        epilogue = lambda acc: (acc,)
    out_dtypes = out_dtypes or [out_dtype]
    ne, no = len(extras), len(out_dtypes)

    def body(*refs):
        a_ref, b_ref = refs[:2]
        e_refs, o_refs = refs[2:2 + ne], refs[2 + ne:2 + ne + no]

        def finish(acc):
            res = epilogue(acc, *[e[...] for e in e_refs])
            for o_ref, v in zip(o_refs, res, strict=True):
                o_ref[...] = v.astype(o_ref.dtype)

        prod = _bdot(a_ref[...], b_ref[...], dn)
        if nk == 1:
            finish(prod)
            return
        acc_ref = refs[-1]
        k = pl.program_id(2)

        @pl.when(k == 0)
        def _():
            acc_ref[...] = prod

        @pl.when(jnp.logical_and(k > 0, k < nk - 1))
        def _():
            acc_ref[...] += prod

        @pl.when(k == nk - 1)
        def _():
            finish(acc_ref[...] + prod)

    a_spec = pl.BlockSpec((tk, tm), lambda i, j, k: (k, i)) if mode == "tn" else pl.BlockSpec((tm, tk), lambda i, j, k: (i, k))
    if b_shards and mode == "nt":
        b_spec = pl.BlockSpec((None, tn, tk), lambda i, j, k: (k, j, 0))
    elif b_shards:
        b_spec = pl.BlockSpec((None, tk, tn), lambda i, j, k: (j, k, 0))
    elif mode == "nt":
        b_spec = pl.BlockSpec((tn, tk), lambda i, j, k: (j, k))
    else:
        b_spec = pl.BlockSpec((tk, tn), lambda i, j, k: (k, j))
    if out_shards:
        o_spec, o_shape = pl.BlockSpec((None, tm, tn), lambda i, j, k: (j, i, 0)), (N // tn, M, tn)
    else:
        o_spec, o_shape = pl.BlockSpec((tm, tn), lambda i, j, k: (i, j)), (M, N)
    res = pl.pallas_call(
        body, name=name, grid=(M // tm, N // tn, nk), in_specs=[a_spec, b_spec] + [o_spec] * ne, out_specs=[o_spec] * no,
        out_shape=[SDS(o_shape, dt) for dt in out_dtypes], scratch_shapes=[pltpu.VMEM((tm, tn), F32)] if nk > 1 else [],
        compiler_params=_cparams(dimension_semantics=("parallel", "parallel", "arbitrary")),
    )(a, b, *extras)
    return res[0] if no == 1 else res


def _row_spec(tm, spec):
    _, c0, w = spec
    assert c0 % w == 0
    return pl.BlockSpec((tm, w), functools.partial(lambda i, cb: (i, cb), cb=c0 // w))


def _par_spec(spec):
    arr, c0, w = spec
    assert c0 % w == 0
    return pl.BlockSpec((arr.shape[0], w), functools.partial(lambda i, cb: (0, cb), cb=c0 // w))


def _whole(arr):
    return (arr, 0, arr.shape[1])


def _rw_fwd(fn, rows, params, outs, *, tm, name):
    M = rows[0][0].shape[0]
    nr, npar = len(rows), len(params)

    def body(*refs):
        rv = [r[...].astype(F32) for r in refs[:nr]]
        pv = [p[...].astype(F32) for p in refs[nr:nr + npar]]
        res = fn(*rv, *pv)
        for o_ref, v in zip(refs[nr + npar:], res, strict=True):
            o_ref[...] = v.astype(o_ref.dtype)

    return pl.pallas_call(
        body, name=name, grid=(M // tm,),
        in_specs=[_row_spec(tm, r) for r in rows] + [_par_spec(p) for p in params],
        out_specs=[pl.BlockSpec((tm, w), lambda i: (i, 0)) for w, _ in outs],
        out_shape=[SDS((M, w), dt) for w, dt in outs],
        compiler_params=_cparams(dimension_semantics=("parallel",)),
    )(*[r[0] for r in rows], *[p[0] for p in params])


def _rw_bwd(fn, rows, params, cts, *, tm, name, row_grads, adds=None):
    M = rows[0][0].shape[0]
    adds = adds or {}
    nr, npar, nc = len(rows), len(params), len(cts)
    add_keys = sorted(adds)
    want = [k for k in range(nr) if row_grads[k] is not None]

    def body(*refs):
        pos = 0
        r_refs = refs[pos:pos + nr]; pos += nr
        p_refs = refs[pos:pos + npar]; pos += npar
        c_refs = refs[pos:pos + nc]; pos += nc
        a_refs = dict(zip(add_keys, refs[pos:pos + len(add_keys)])); pos += len(add_keys)
        dr_refs = dict(zip(want, refs[pos:pos + len(want)])); pos += len(want)
        dp_refs = refs[pos:pos + npar]
        rv = [r[...].astype(F32) for r in r_refs]
        pv = [p[...].astype(F32) for p in p_refs]
        _, vjp = jax.vjp(fn, *rv, *pv)
        g = vjp(tuple(c[...].astype(F32) for c in c_refs))
        for k in want:
            v = g[k]
            if k in a_refs:
                v = v + a_refs[k][...].astype(F32)
            dr_refs[k][...] = v.astype(dr_refs[k].dtype)
        first = pl.program_id(0) == 0
        for j in range(npar):
            @pl.when(first)
            def _(j=j):
                dp_refs[j][...] = jnp.zeros_like(dp_refs[j])
            dp_refs[j][...] += g[nr + j]

    res = pl.pallas_call(
        body, name=name, grid=(M // tm,),
        in_specs=([_row_spec(tm, r) for r in rows] + [_par_spec(p) for p in params] + [_row_spec(tm, c) for c in cts]
                  + [_row_spec(tm, adds[k]) for k in add_keys]),
        out_specs=([pl.BlockSpec((tm, rows[k][2]), lambda i: (i, 0)) for k in want]
                   + [pl.BlockSpec((p[0].shape[0], p[2]), lambda i: (0, 0)) for p in params]),
        out_shape=([SDS((M, rows[k][2]), row_grads[k]) for k in want] + [SDS((p[0].shape[0], p[2]), F32) for p in params]),
        compiler_params=_cparams(dimension_semantics=("arbitrary",)),
    )(*[r[0] for r in rows], *[p[0] for p in params], *[c[0] for c in cts], *[adds[k][0] for k in add_keys])
    return res


def _rms_fn(x, g):
    r = lax.rsqrt(jnp.mean(x * x, axis=-1, keepdims=True) + EPS)
    return (x * r * g,)


def _seg_mats(width, seg):
    n = width // seg
    p = (_iota((width, n), 0) // seg == _iota((width, n), 1)).astype(F32)
    e = (_iota((n, width), 1) // seg == _iota((n, width), 0)).astype(F32)
    return p, e


def _gate_fn(y, xs, z, dskip, w):
    width = SSM_HEADS * SSM_P
    _, e = _seg_mats(width, SSM_P)
    y = (y + _cdot(dskip, e) * xs) * (z * jax.nn.sigmoid(z))
    g0 = _iota((1, width), 1) < width // SSM_GROUPS
    y2 = y * y
    gw = width // SSM_GROUPS
    ms0 = jnp.sum(jnp.where(g0, y2, 0.0), axis=-1, keepdims=True) * (1.0 / gw)
    ms1 = jnp.sum(jnp.where(g0, 0.0, y2), axis=-1, keepdims=True) * (1.0 / gw)
    r = jnp.where(g0, lax.rsqrt(ms0 + EPS), lax.rsqrt(ms1 + EPS))
    return (y * r * w,)


def _xattn_fn(q0, q1, q2, q3, k0, k1, k2, k3, v0, v1, v2, v3, gq, gk):
    def norm(u, g):
        return u * lax.rsqrt(jnp.mean(u * u, axis=-1, keepdims=True) + EPS) * g
    outs = []
    for q, k, v in ((q0, k0, v0), (q1, k1, v1), (q2, k2, v2), (q3, k3, v3)):
        s = mm_nt(norm(q, gq), norm(k, gk)) * (X_D ** -0.5)
        p = jnp.exp(s - lax.stop_gradient(jnp.max(s, axis=-1, keepdims=True)))
        p = p / jnp.sum(p, axis=-1, keepdims=True)
        outs.append(mm_nn(p, v))
    return (jnp.concatenate(outs, axis=-1),)


CONV_TC = 256


def _shift_down(u, k):
    if k == 0:
        return u
    return jnp.where(_iota(u.shape, 0) >= k, pltpu.roll(u, k, axis=0), 0.0)


def _shift_up(u, k):
    if k == 0:
        return u
    n = u.shape[0]
    return jnp.where(_iota(u.shape, 0) < n - k, pltpu.roll(u, n - k, axis=0), 0.0)


def _conv_pre(u, w_ref, b):
    pre = b + w_ref[3:4, :] * u
    for k in (1, 2, 3):
        pre = pre + w_ref[3 - k:4 - k, :] * _shift_down(u, k)
    return pre


def _conv_fwd(src, c0, width, w, b, *, name):
    S = src.shape[0]
    cb0 = c0 // CONV_TC

    def body(u_ref, w_ref, b_ref, o_ref):
        pre = _conv_pre(u_ref[...], w_ref, b_ref[...])
        o_ref[...] = pre * jax.nn.sigmoid(pre)

    return pl.pallas_call(
        body, name=name, grid=(width // CONV_TC,),
        in_specs=[pl.BlockSpec((S, CONV_TC), lambda j: (0, cb0 + j)), pl.BlockSpec((4, CONV_TC), lambda j: (0, j)),
                  pl.BlockSpec((1, CONV_TC), lambda j: (0, j))],
        out_specs=pl.BlockSpec((S, CONV_TC), lambda j: (0, j)), out_shape=SDS((S, width), F32),
        compiler_params=_cparams(dimension_semantics=("parallel",)),
    )(src, w, b)


def _conv_bwd(src, c0, width, w, b, douts, *, name):
    S = src.shape[0]
    cb0 = c0 // CONV_TC
    nd = len(douts)

    def body(*refs):
        u_ref, w_ref, b_ref = refs[:3]
        d_refs = refs[3:3 + nd]
        du_ref, dw_ref, db_ref = refs[3 + nd:]
        u = u_ref[...]
        pre = _conv_pre(u, w_ref, b_ref[...])
        sg = jax.nn.sigmoid(pre)
        dout = d_refs[0][...]
        for r in d_refs[1:]:
            dout = dout + r[...]
        dpre = dout * (sg * (1.0 + pre * (1.0 - sg)))
        du = w_ref[3:4, :] * dpre
        dw_ref[3:4, :] = jnp.sum(dpre * u, axis=0, keepdims=True)
        for k in (1, 2, 3):
            du = du + w_ref[3 - k:4 - k, :] * _shift_up(dpre, k)
            dw_ref[3 - k:4 - k, :] = jnp.sum(dpre * _shift_down(u, k), axis=0, keepdims=True)
        du_ref[...] = du.astype(du_ref.dtype)
        db_ref[...] = jnp.sum(dpre, axis=0, keepdims=True)

    return pl.pallas_call(
        body, name=name, grid=(width // CONV_TC,),
        in_specs=[pl.BlockSpec((S, CONV_TC), lambda j: (0, cb0 + j)), pl.BlockSpec((4, CONV_TC), lambda j: (0, j)),
                  pl.BlockSpec((1, CONV_TC), lambda j: (0, j))] + [pl.BlockSpec((S, CONV_TC), lambda j: (0, j))] * nd,
        out_specs=[pl.BlockSpec((S, CONV_TC), lambda j: (0, j)), pl.BlockSpec((4, CONV_TC), lambda j: (0, j)),
                   pl.BlockSpec((1, CONV_TC), lambda j: (0, j))],
        out_shape=[SDS((S, width), BF16), SDS((4, width), F32), SDS((1, width), F32)],
        compiler_params=_cparams(dimension_semantics=("parallel",)),
    )(src, w, b, *douts)


def _softplus(x):
    return jnp.maximum(x, 0.0) + jnp.log(1.0 + jnp.exp(-jnp.abs(x)))


def _prefix_sum(x, seg):
    n = x.shape[1]
    pos = _iota(x.shape, 1) % seg
    k = 1
    while k < seg:
        x = x + jnp.where(pos >= k, pltpu.roll(x, k, axis=1), 0.0)
        k *= 2
    return x


def _suffix_sum(x, seg):
    n = x.shape[1]
    pos = _iota(x.shape, 1) % seg
    k = 1
    while k < seg:
        x = x + jnp.where(pos + k < seg, pltpu.roll(x, n - k, axis=1), 0.0)
        k *= 2
    return x


def _dtf_fwd(dtf_t, dt_bias, a_log, f_bias):
    S = dtf_t.shape[1]

    def body(x_ref, db_ref, al_ref, fb_ref, dt_ref, acs_ref, cum_ref):
        dt = _softplus(x_ref[0:16, :] + db_ref[...])
        dt_ref[...] = dt
        acs_ref[...] = _prefix_sum(dt * (-jnp.exp(al_ref[...])), CHUNK)
        cum_ref[...] = _prefix_sum(-_softplus(-(x_ref[16:32, :] + fb_ref[...])), S)

    return pl.pallas_call(body, name="dtf_fwd", out_shape=[SDS((16, S), F32)] * 3, compiler_params=_cparams())(
        dtf_t, dt_bias, a_log, f_bias)


def _dtf_bwd(dtf_t, dt_bias, a_log, f_bias, d_dt, d_acs_a, d_acs_b, d_cum):
    S = dtf_t.shape[1]

    def body(x_ref, db_ref, al_ref, fb_ref, ddt_ref, da1_ref, da2_ref, dc_ref, dx_ref, ddb_ref, dal_ref, dfb_ref):
        xd = x_ref[0:16, :] + db_ref[...]
        dt = _softplus(xd)
        a = -jnp.exp(al_ref[...])
        d_da = _suffix_sum(da1_ref[...] + da2_ref[...], CHUNK)
        d_dt = ddt_ref[...] + d_da * a
        dal_ref[...] = jnp.sum(d_da * dt, axis=1, keepdims=True) * a
        d_xd = d_dt * jax.nn.sigmoid(xd)
        ddb_ref[...] = jnp.sum(d_xd, axis=1, keepdims=True)
        xf = x_ref[16:32, :] + fb_ref[...]
        d_xf = _suffix_sum(dc_ref[...], S) * jax.nn.sigmoid(-xf)
        dfb_ref[...] = jnp.sum(d_xf, axis=1, keepdims=True)
        dx_ref[0:16, :] = d_xd
        dx_ref[16:32, :] = d_xf

    return pl.pallas_call(body, name="dtf_bwd", out_shape=[SDS((32, S), F32)] + [SDS((16, 1), F32)] * 3,
                          compiler_params=_cparams())(dtf_t, dt_bias, a_log, f_bias, d_dt, d_acs_a, d_acs_b, d_cum)


SSM_PAIRS = SSM_HEADS // 2 // SSM_GROUPS


def _ssd_pair(xs, dtc, acol, arow, bm, cm, cbm, h, hp):
    L = CHUNK
    first = _iota((1, LANES), 1) < SSM_P
    i16, s16 = _iota((L, 16), 1), _iota((16, L), 0)
    ha, hb = 2 * hp, 2 * hp + 1

    def selc(blk, hh):
        return jnp.sum(jnp.where(i16 == hh, blk, 0.0), axis=1, keepdims=True)

    def selr(blk, hh):
        return jnp.sum(jnp.where(s16 == hh, blk, 0.0), axis=0, keepdims=True)

    x = xs * jnp.where(first, selc(dtc, ha), selc(dtc, hb))
    ca, cb, ra, rb = selc(acol, ha), selc(acol, hb), selr(arow, ha), selr(arow, hb)
    tri = _iota((L, L), 0) >= _iota((L, L), 1)
    la = jnp.exp(jnp.where(tri, ca - ra, NEG))
    lb = jnp.exp(jnp.where(tri, cb - rb, NEG))
    y = jnp.where(first, mm_nn(cbm * la, x), mm_nn(cbm * lb, x))
    y = y + jnp.where(first, jnp.exp(ca), jnp.exp(cb)) * mm_nn(cm, h)
    last = _iota((1, L), 1) == L - 1
    ala = jnp.sum(jnp.where(last, ra, 0.0), axis=1, keepdims=True)
    alb = jnp.sum(jnp.where(last, rb, 0.0), axis=1, keepdims=True)
    dec = jnp.where(first, jnp.exp(ala - ca), jnp.exp(alb - cb))
    hn = jnp.where(first, jnp.exp(ala), jnp.exp(alb)) * h + mm_tn(bm, x * dec)
    return y, hn


def _ssd_group(*args, grp):
    xs, (dtc, acol, arow, bm, cm), hs = args[:SSM_PAIRS], args[SSM_PAIRS:SSM_PAIRS + 5], args[SSM_PAIRS + 5:]
    cbm = mm_nt(cm, bm)
    res = [_ssd_pair(xs[j], dtc, acol, arow, bm, cm, cbm, hs[j], SSM_PAIRS * grp + j) for j in range(SSM_PAIRS)]
    return tuple(r[0] for r in res) + tuple(r[1] for r in res)


def _ssd_specs(nc, rev):
    L = CHUNK
    cidx = (lambda c: nc - 1 - c) if rev else (lambda c: c)
    return dict(
        xs=pl.BlockSpec((L, SSM_PAIRS * LANES), lambda c, g: (cidx(c), g)),
        col=pl.BlockSpec((L, 16), lambda c, g: (cidx(c), 0)),
        row=pl.BlockSpec((16, L), lambda c, g: (0, cidx(c))),
        b=pl.BlockSpec((L, SSM_N), lambda c, g: (cidx(c), g)),
        c=pl.BlockSpec((L, SSM_N), lambda c, g: (cidx(c), SSM_GROUPS + g)),
        st=pl.BlockSpec((1, SSM_PAIRS, SSM_N, LANES), lambda c, g: (cidx(c), g, 0, 0)),
    )


def _lane_pieces(v):
    return [v[:, LANES * j:LANES * (j + 1)] for j in range(v.shape[1] // LANES)]


def _ssd_fwd(xs, dt_col, acs_col, acs_row, bc):
    S = xs.shape[0]
    nc, nhp = S // CHUNK, SSM_HEADS // 2
    sp = _ssd_specs(nc, False)

    def body(xs_ref, dt_ref, ac_ref, ar_ref, b_ref, c_ref, y_ref, hs_ref, h_scr):
        c, g = pl.program_id(0), pl.program_id(1)

        @pl.when(c == 0)
        def _():
            for j in range(SSM_PAIRS):
                h_scr[SSM_PAIRS * g + j] = jnp.zeros((SSM_N, LANES), F32)

        hs = [h_scr[SSM_PAIRS * g + j] for j in range(SSM_PAIRS)]
        for j in range(SSM_PAIRS):
            hs_ref[0, j] = hs[j]
        res = _ssd_group(*_lane_pieces(xs_ref[...]), dt_ref[...], ac_ref[...], ar_ref[...], b_ref[...], c_ref[...], *hs,
                         grp=g)
        y_ref[...] = jnp.concatenate(res[:SSM_PAIRS], axis=1)
        for j in range(SSM_PAIRS):
            h_scr[SSM_PAIRS * g + j] = res[SSM_PAIRS + j]

    return pl.pallas_call(
        body, name="ssd_fwd", grid=(nc, SSM_GROUPS),
        in_specs=[sp["xs"], sp["col"], sp["col"], sp["row"], sp["b"], sp["c"]],
        out_specs=[sp["xs"], sp["st"]],
        out_shape=[SDS((S, SSM_HEADS * SSM_P), F32), SDS((nc, nhp, SSM_N, LANES), F32)],
        scratch_shapes=[pltpu.VMEM((nhp, SSM_N, LANES), F32)],
        compiler_params=_cparams(dimension_semantics=("arbitrary", "arbitrary")),
    )(xs, dt_col, acs_col, acs_row, bc, bc)


def _ssd_bwd(xs, dt_col, acs_col, acs_row, bc, hs, dy):
    S = xs.shape[0]
    nc, nhp = S // CHUNK, SSM_HEADS // 2
    sp = _ssd_specs(nc, True)

    def body(xs_ref, dt_ref, ac_ref, ar_ref, b_ref, c_ref, hs_ref, dy_ref,
             dxs_ref, ddt_ref, dac_ref, dar_ref, db_ref, dc_ref, dh_scr):
        c, g = pl.program_id(0), pl.program_id(1)

        @pl.when(c == 0)
        def _():
            for j in range(SSM_PAIRS):
                dh_scr[SSM_PAIRS * g + j] = jnp.zeros((SSM_N, LANES), F32)

        _, vjp = jax.vjp(functools.partial(_ssd_group, grp=g), *_lane_pieces(xs_ref[...]), dt_ref[...], ac_ref[...],
                         ar_ref[...], b_ref[...], c_ref[...], *[hs_ref[0, j] for j in range(SSM_PAIRS)])
        grads = vjp(tuple(_lane_pieces(dy_ref[...])) + tuple(dh_scr[SSM_PAIRS * g + j] for j in range(SSM_PAIRS)))
        dxs_ref[...] = jnp.concatenate(grads[:SSM_PAIRS], axis=1)
        ddt, dac, dar, db, dc = grads[SSM_PAIRS:SSM_PAIRS + 5]
        for j in range(SSM_PAIRS):
            dh_scr[SSM_PAIRS * g + j] = grads[SSM_PAIRS + 5 + j]
        db_ref[...] = db
        dc_ref[...] = dc

        @pl.when(g == 0)
        def _():
            ddt_ref[...] = ddt
            dac_ref[...] = dac
            dar_ref[...] = dar

        @pl.when(g > 0)
        def _():
            ddt_ref[...] += ddt
            dac_ref[...] += dac
            dar_ref[...] += dar

    return pl.pallas_call(
        body, name="ssd_bwd", grid=(nc, SSM_GROUPS),
        in_specs=[sp["xs"], sp["col"], sp["col"], sp["row"], sp["b"], sp["c"], sp["st"], sp["xs"]],
        out_specs=[sp["xs"], sp["col"], sp["col"], sp["row"], sp["b"], sp["b"]],
        out_shape=[SDS((S, SSM_HEADS * SSM_P), F32), SDS((S, 16), F32), SDS((S, 16), F32), SDS((16, S), F32),
                   SDS((S, SSM_GROUPS * SSM_N), F32), SDS((S, SSM_GROUPS * SSM_N), F32)],
        scratch_shapes=[pltpu.VMEM((nhp, SSM_N, LANES), F32)],
        compiler_params=_cparams(dimension_semantics=("arbitrary", "arbitrary")),
    )(xs, dt_col, acs_col, acs_row, bc, bc, hs, dy)


ATT_T = 512


def _pick_col(blk, h):
    return jnp.sum(jnp.where(_iota(blk.shape, 1) == h, blk, 0.0), axis=1, keepdims=True)


def _pick_row(blk, h):
    return jnp.sum(jnp.where(_iota(blk.shape, 0) == h, blk, 0.0), axis=0, keepdims=True)


def _pair_norm(x, g2, first):
    x2 = x * x
    sa = jnp.sum(jnp.where(first, x2, 0.0), axis=1, keepdims=True)
    sb = jnp.sum(jnp.where(first, 0.0, x2), axis=1, keepdims=True)
    r = jnp.where(first, lax.rsqrt(sa * (1.0 / ATT_D) + EPS), lax.rsqrt(sb * (1.0 / ATT_D) + EPS))
    return x * r * g2, r


def _pair_norm_bwd(dxn, x, r, g2, first):
    t = dxn * g2
    tx = t * x
    ma = jnp.sum(jnp.where(first, tx, 0.0), axis=1, keepdims=True)
    mb = jnp.sum(jnp.where(first, 0.0, tx), axis=1, keepdims=True)
    dx = r * (t - x * (r * r) * (jnp.where(first, ma, mb) * (1.0 / ATT_D)))
    return dx, jnp.sum(dxn * x * r, axis=0, keepdims=True)


def _fox_fwd(src, q_c0, k_c0, v_c0, gq2, gk2, cum_col, cum_row3):
    S = src.shape[0]
    T = ATT_T
    nq, nhp = S // T, ATT_HEADS // 2
    qb0, kb0, vb0 = q_c0 // LANES, k_c0 // LANES, v_c0 // LANES
    scale = ATT_D ** -0.5

    def body(q_ref, kraw_ref, v_ref, gq_ref, gk_ref, cc_ref, cr_ref, o_ref, l_ref, k_ref):
        hp, i = pl.program_id(0), pl.program_id(1)
        first = _iota((1, LANES), 1) < ATT_D

        @pl.when(i == 0)
        def _():
            k_ref[...] = _pair_norm(kraw_ref[...], gk_ref[...], first)[0].astype(BF16)

        q = (_pair_norm(q_ref[...], gq_ref[...], first)[0] * scale).astype(BF16)
        zero = jnp.zeros_like(q)
        qs = (jnp.where(first, q, zero), jnp.where(first, zero, q))
        cc = cc_ref[...]
        cq = (_pick_col(cc, 2 * hp), _pick_col(cc, 2 * hp + 1))
        tri = _iota((T, T), 0) >= _iota((T, T), 1)

        def tile(j, carry, diagonal):
            off = pl.multiple_of(j * T, T)
            k = k_ref[pl.ds(off, T), :]
            v = v_ref[pl.ds(off, T), :].astype(BF16)
            cr = cr_ref[j]
            out = []
            for hh in range(2):
                m, l, acc = carry[3 * hh:3 * hh + 3]
                s = _bdot(qs[hh], k, _NT) + (cq[hh] - _pick_row(cr, 2 * hp + hh))
                if diagonal:
                    s = jnp.where(tri, s, NEG)
                m_new = jnp.maximum(m, jnp.max(s, axis=1, keepdims=True))
                alpha = jnp.exp(m - m_new)
                p = jnp.exp(s - m_new)
                out += [m_new, alpha * l + jnp.sum(p, axis=1, keepdims=True), alpha * acc + _bdot(p, v, _NN)]
            return tuple(out)

        init = (jnp.full((T, 1), NEG, F32), jnp.zeros((T, 1), F32), jnp.zeros((T, LANES), F32)) * 2
        carry = lax.fori_loop(0, i, lambda j, c: tile(j, c, False), init)
        ma, la, acca, mb, lb, accb = tile(i, carry, True)
        o_ref[...] = jnp.where(first, acca / la, accb / lb).astype(o_ref.dtype)
        l_ref[...] = jnp.where(first, ma + jnp.log(la), mb + jnp.log(lb))

    gain = pl.BlockSpec((1, LANES), lambda hp, i: (0, 0))
    return pl.pallas_call(
        body, name="fox_fwd", grid=(nhp, nq),
        in_specs=[pl.BlockSpec((T, LANES), lambda hp, i: (i, qb0 + hp)), pl.BlockSpec((S, LANES), lambda hp, i: (0, kb0 + hp)),
                  pl.BlockSpec((S, LANES), lambda hp, i: (0, vb0 + hp)), gain, gain,
                  pl.BlockSpec((T, 16), lambda hp, i: (i, 0)), pl.BlockSpec((nq, 16, T), lambda hp, i: (0, 0, 0))],
        out_specs=[pl.BlockSpec((T, LANES), lambda hp, i: (i, hp))] * 2,
        out_shape=[SDS((S, ATT_HEADS * ATT_D), BF16), SDS((S, ATT_HEADS * ATT_D), F32)],
        scratch_shapes=[pltpu.VMEM((S, LANES), BF16)],
        compiler_params=_cparams(dimension_semantics=("arbitrary", "arbitrary")),
    )(src, src, src, gq2, gk2, cum_col, cum_row3)


def _fox_bwd(src, q_c0, k_c0, v_c0, gq2, gk2, cum_col, cum_row3, lse, dsrc, d_c0):
    S = src.shape[0]
    T = ATT_T
    nq, nhp = S // T, ATT_HEADS // 2
    qb0, kb0, vb0, db0 = q_c0 // LANES, k_c0 // LANES, v_c0 // LANES, d_c0 // LANES
    scale = ATT_D ** -0.5

    def body(q_ref, kraw_ref, v_ref, gq_ref, gk_ref, cc_ref, cr_ref, l_ref, do_ref,
             dq_ref, dk_ref, dv_ref, dc_ref, dg_ref, k_ref, dk_acc, dv_acc, p_scr, dp_scr):
        hp, i = pl.program_id(0), pl.program_id(1)
        first = _iota((1, LANES), 1) < ATT_D
        tri = _iota((T, T), 0) >= _iota((T, T), 1)

        @pl.when(i == 0)
        def _():
            k_ref[...] = _pair_norm(kraw_ref[...], gk_ref[...], first)[0].astype(BF16)
            dk_acc[...] = jnp.zeros_like(dk_acc)
            dv_acc[...] = jnp.zeros_like(dv_acc)
            dc_ref[...] = jnp.zeros_like(dc_ref)
            dg_ref[...] = jnp.zeros_like(dg_ref)

        q_raw = q_ref[...]
        qn, rq = _pair_norm(q_raw, gq_ref[...], first)
        q = (qn * scale).astype(BF16)
        zq = jnp.zeros_like(q)
        dob = do_ref[...].astype(BF16)
        zd = jnp.zeros_like(dob)
        lse_blk, cc = l_ref[...], cc_ref[...]
        dq = jnp.zeros((T, LANES), F32)
        for hh in range(2):
            sel = first if hh == 0 else jnp.logical_not(first)
            qh, doh = jnp.where(sel, q, zq), jnp.where(sel, dob, zd)
            bias_q = _pick_col(cc, 2 * hp + hh) - jnp.max(jnp.where(sel, lse_blk, NEG), axis=1, keepdims=True)

            def probs(j, delta, diagonal):
                off = pl.multiple_of(j * T, T)
                s = _bdot(qh, k_ref[pl.ds(off, T), :], _NT) + (bias_q - _pick_row(cr_ref[j], 2 * hp + hh))
                if diagonal:
                    s = jnp.where(tri, s, NEG)
                p = jnp.exp(s)
                dp = _bdot(doh, v_ref[pl.ds(off, T), :], _NT)
                p_scr[j] = p
                dp_scr[j] = dp
                return delta + jnp.sum(p * dp, axis=1, keepdims=True)

            delta = lax.fori_loop(0, i, lambda j, d: probs(j, d, False), jnp.zeros((T, 1), F32))
            delta = probs(i, delta, True)

            def grads(j, dq):
                off = pl.multiple_of(j * T, T)
                p = p_scr[j]
                ds = p * (dp_scr[j] - delta)
                dv_acc[pl.ds(off, T), :] += _bdot(p, doh, _TN)
                dk_acc[pl.ds(off, T), :] += _bdot(ds, qh, _TN)
                dc_ref[0, j, hh:hh + 1, :] -= jnp.sum(ds, axis=0, keepdims=True)
                zk = jnp.zeros((T, LANES), BF16)
                return dq + _bdot(ds, jnp.where(sel, k_ref[pl.ds(off, T), :], zk), _NN)

            dq = lax.fori_loop(0, i + 1, grads, dq)
        dq_raw, dgq = _pair_norm_bwd(dq * scale, q_raw, rq, gq_ref[...], first)
        dq_ref[...] = dq_raw.astype(dq_ref.dtype)
        dg_ref[0, 0:1, :] += dgq

        @pl.when(i == nq - 1)
        def _():
            k_raw = kraw_ref[...]
            rk = _pair_norm(k_raw, gk_ref[...], first)[1]
            dk_raw, dgk = _pair_norm_bwd(dk_acc[...], k_raw, rk, gk_ref[...], first)
            dk_ref[...] = dk_raw.astype(dk_ref.dtype)
            dv_ref[...] = dv_acc[...].astype(dv_ref.dtype)
            dg_ref[0, 1:2, :] = dgk

    gain = pl.BlockSpec((1, LANES), lambda hp, i: (0, 0))
    band = SDS((S, ATT_HEADS * ATT_D), BF16)
    return pl.pallas_call(
        body, name="fox_bwd", grid=(nhp, nq),
        in_specs=[pl.BlockSpec((T, LANES), lambda hp, i: (i, qb0 + hp)), pl.BlockSpec((S, LANES), lambda hp, i: (0, kb0 + hp)),
                  pl.BlockSpec((S, LANES), lambda hp, i: (0, vb0 + hp)), gain, gain,
                  pl.BlockSpec((T, 16), lambda hp, i: (i, 0)), pl.BlockSpec((nq, 16, T), lambda hp, i: (0, 0, 0)),
                  pl.BlockSpec((T, LANES), lambda hp, i: (i, hp)), pl.BlockSpec((T, LANES), lambda hp, i: (i, db0 + hp))],
        out_specs=[pl.BlockSpec((T, LANES), lambda hp, i: (i, hp)), pl.BlockSpec((S, LANES), lambda hp, i: (0, hp)),
                   pl.BlockSpec((S, LANES), lambda hp, i: (0, hp)), pl.BlockSpec((1, nq, 8, T), lambda hp, i: (hp, 0, 0, 0)),
                   pl.BlockSpec((1, 8, LANES), lambda hp, i: (hp, 0, 0))],
        out_shape=[band, band, band, SDS((nhp, nq, 8, T), F32), SDS((nhp, 8, LANES), F32)],
        scratch_shapes=[pltpu.VMEM((S, LANES), BF16), pltpu.VMEM((S, LANES), F32), pltpu.VMEM((S, LANES), F32),
                        pltpu.VMEM((nq, T, T), F32), pltpu.VMEM((nq, T, T), F32)],
        compiler_params=_cparams(dimension_semantics=("arbitrary", "arbitrary")),
    )(src, src, src, gq2, gk2, cum_col, cum_row3, lse, dsrc)


def _fold_gains(dg):
    def body(d_ref, o_ref):
        t = d_ref[0]
        for h in range(1, dg.shape[0]):
            t = t + d_ref[h]
        o_ref[...] = t + pltpu.roll(t, ATT_D, axis=1)

    return pl.pallas_call(body, name="fold_gains", out_shape=SDS(dg.shape[1:], F32), compiler_params=_cparams())(dg)


def _loss_head(y, target, *, tm):
    M, W = y.shape

    def body(y_ref, t_ref, dy_ref, loss_ref):
        @pl.when(pl.program_id(0) == 0)
        def _():
            loss_ref[...] = jnp.zeros_like(loss_ref)

        e = y_ref[...] - t_ref[...]
        dy_ref[...] = e * (1.0 / W)
        loss_ref[...] += jnp.sum(jnp.sum(e * e, axis=1, keepdims=True), axis=0, keepdims=True) * (0.5 / W)

    return pl.pallas_call(
        body, name="loss_head", grid=(M // tm,),
        in_specs=[pl.BlockSpec((tm, W), lambda i: (i, 0))] * 2,
        out_specs=[pl.BlockSpec((tm, W), lambda i: (i, 0)), pl.BlockSpec((1, 1), lambda i: (0, 0))],
        out_shape=[SDS((M, W), F32), SDS((1, 1), F32)],
        compiler_params=_cparams(dimension_semantics=("arbitrary",)),
    )(y, target)


def _adamw_math(w, g, m, v):
    m = ADAM_B1 * m + (1.0 - ADAM_B1) * g
    v = ADAM_B2 * v + (1.0 - ADAM_B2) * jnp.square(g)
    m_hat = m / (1.0 - ADAM_B1 ** ADAM_STEP)
    v_hat = v / (1.0 - ADAM_B2 ** ADAM_STEP)
    delta = -ADAM_LR * (m_hat / (jnp.sqrt(v_hat) + ADAM_EPS) + ADAM_WD * w)
    return delta, m, v


def _reduce_adamw(parts, w, m, v, *, tr, name, tc=None):
    R, C = w.shape
    tr, tc = min(tr, R), tc or C
    nparts = parts.shape[0]

    def body(p_ref, w_ref, m_ref, v_ref, g_ref, d_ref, nm_ref, nv_ref):
        g = p_ref[0].astype(F32)
        for s in range(1, nparts):
            g = g + p_ref[s].astype(F32)
        g_ref[...] = g
        d_ref[...], nm_ref[...], nv_ref[...] = _adamw_math(w_ref[...], g, m_ref[...], v_ref[...])

    blk = pl.BlockSpec((tr, tc), lambda i, j: (i, j))
    return pl.pallas_call(
        body, name=name, grid=(R // tr, C // tc),
        in_specs=[pl.BlockSpec((nparts, tr, tc), lambda i, j: (0, i, j)), blk, blk, blk], out_specs=[blk] * 4,
        out_shape=[SDS((R, C), F32)] * 4, compiler_params=_cparams(dimension_semantics=("parallel", "parallel")),
    )(parts, w, m, v)


def _adamw(w, g, m, v, *, name):
    def body(w_ref, g_ref, m_ref, v_ref, d_ref, nm_ref, nv_ref):
        d_ref[...], nm_ref[...], nv_ref[...] = _adamw_math(w_ref[...], g_ref[...], m_ref[...], v_ref[...])

    return pl.pallas_call(body, name=name, out_shape=[SDS(w.shape, F32)] * 3, compiler_params=_cparams())(w, g, m, v)


def _peers():
    x, y, c = lax.axis_index("x"), lax.axis_index("y"), lax.axis_index("c")
    out = []
    for k in range(1, N_DEV):
        px, py, pc = x ^ ((k >> 2) & 1), y ^ ((k >> 1) & 1), c ^ (k & 1)
        out.append(((px, py, pc), 4 * px + 2 * py + pc))
    return 4 * x + 2 * y + c, out


_HBM = pl.BlockSpec(memory_space=pltpu.HBM)
_SEM = pl.BlockSpec(memory_space=pltpu.SEMAPHORE)
_DATAFLOW = pltpu.SideEffectType.DATAFLOW_SIDE_EFFECTING


NEAR = (1, 2, 4, 6)


def _plan_peers(scatter, ks=tuple(range(1, N_DEV))):
    return lambda me, peers: [(peers[k - 1][0], peers[k - 1][1] if scatter else None, me, k - 1) for k in ks]


def _plan_relay(me, peers):
    return [(peers[0][0], peers[k - 1][1], peers[k - 1][1], j) for j, k in enumerate((2, 4, 6))]


def _plan_pair(me, peers):
    return [(peers[0][0], peers[k - 1][1], j, j) for j, k in enumerate((1, 3, 5, 7))]


def _plan_chips(me, peers):
    return [(peers[k - 1][0], k // 2, k // 2, k // 2) for k in (2, 4, 6)]


def _copy(src, dst, c, send_sems, recv_sems):
    dev, s_slot, d_slot, i = c
    return pltpu.make_async_remote_copy(
        src_ref=src if s_slot is None else src.at[s_slot], dst_ref=dst.at[d_slot], send_sem=send_sems.at[i],
        recv_sem=recv_sems.at[i], device_id=dev, device_id_type=MESH)


def _copies_start(items, *, name):
    n = len(items)
    bufs = [it[0] for it in items] + [it[1] for it in items if it[1] is not None]
    nb = len(bufs)

    def body(*refs):
        srcs, extra, sems, token = refs[:n], iter(refs[n:nb]), refs[nb:nb + 2 * n], refs[-1]
        me, peers = _peers()
        for a, (_, land, plan) in enumerate(items):
            dst = srcs[a] if land is None else next(extra)
            for c in plan(me, peers):
                _copy(srcs[a], dst, c, sems[2 * a], sems[2 * a + 1]).start()
        token[...] = jnp.zeros_like(token)

    res = pl.pallas_call(
        body, name=name,
        out_shape=([pltpu.SemaphoreType.DMA((N_DEV - 1,))] * (2 * n) + [pltpu.HBM(b.shape, b.dtype) for b in bufs]
                   + [SDS((8, LANES), F32)]),
        in_specs=[_HBM] * nb, out_specs=[_SEM] * (2 * n) + [_HBM] * nb + [pl.BlockSpec(memory_space=pltpu.VMEM)],
        input_output_aliases={i: 2 * n + i for i in range(nb)},
        compiler_params=pltpu.CompilerParams(has_side_effects=_DATAFLOW),
    )(*[pltpu.with_memory_space_constraint(b, pltpu.HBM) for b in bufs])
    sems, thru, token = res[:2 * n], list(res[2 * n:2 * n + nb]), res[-1]
    extra = iter(thru[n:])
    return [(thru[a], None if it[1] is None else next(extra), sems[2 * a], sems[2 * a + 1], it[2])
            for a, it in enumerate(items)], token


def _copies_wait(handles, after, *, name):
    n = len(handles)
    bufs = [h[0] for h in handles] + [h[1] for h in handles if h[1] is not None]
    nb = len(bufs)

    def body(*refs):
        srcs, extra, sems = refs[:n], iter(refs[n:nb]), refs[nb:nb + 2 * n]
        me, peers = _peers()
        for a, h in enumerate(handles):
            dst = srcs[a] if h[1] is None else next(extra)
            for c in h[4](me, peers):
                cp = _copy(srcs[a], dst, c, sems[2 * a], sems[2 * a + 1])
                cp.wait_send()
                cp.wait_recv()

    flat_sems = [s for h in handles for s in (h[2], h[3])]
    res = pl.pallas_call(
        body, name=name, out_shape=[pltpu.HBM(b.shape, b.dtype) for b in bufs],
        in_specs=[_HBM] * nb + [_SEM] * (2 * n) + [pl.BlockSpec(memory_space=pl.ANY)], out_specs=[_HBM] * nb,
        input_output_aliases={i: i for i in range(nb)},
        compiler_params=pltpu.CompilerParams(has_side_effects=_DATAFLOW),
    )(*bufs, *flat_sems, after)
    extra = iter(res[n:])
    return [(res[a], res[a] if h[1] is None else next(extra)) for a, h in enumerate(handles)]


def _exchange_start(arrays, *, scatter, name, near=()):
    items = []
    for a, arr in enumerate(arrays):
        land = lax.empty(arr.shape if scatter else (N_DEV,) + arr.shape, arr.dtype)
        items.append((arr, land, _plan_peers(scatter, NEAR) if a in near else _plan_peers(scatter)))
    return _copies_start(items, name=name)


def _move_rows(src, moves, rows, *, name):
    C = src.shape[1]
    covered = max(dst + n for _, n, dst in moves)
    tail = rows - covered
    assert sum(n for _, n, _ in moves) == covered

    def body(src_ref, o_ref, sems, *zero):
        cps = [pltpu.make_async_copy(src_ref.at[pl.ds(lo, n)], o_ref.at[pl.ds(dst, n)], sems.at[i])
               for i, (lo, n, dst) in enumerate(moves)]
        for cp in cps:
            cp.start()
        if tail:
            zero[0][...] = jnp.zeros_like(zero[0])
            cps.append(pltpu.make_async_copy(zero[0], o_ref.at[pl.ds(covered, tail)], sems.at[len(moves)]))
            cps[-1].start()
        for cp in cps:
            cp.wait()

    anyspec = pl.BlockSpec(memory_space=pl.ANY)
    return pl.pallas_call(
        body, name=name, in_specs=[anyspec], out_specs=anyspec, out_shape=SDS((rows, C), src.dtype),
        scratch_shapes=[pltpu.SemaphoreType.DMA((len(moves) + 1,))] + ([pltpu.VMEM((tail, C), src.dtype)] if tail else []),
        compiler_params=_cparams())(src)


def _pair_sum(a, b, *, name):
    n, R, C = a.shape
    tc = 256

    def body(a_ref, b_ref, o_ref):
        o_ref[...] = (a_ref[...].astype(F32) + b_ref[...].astype(F32)).astype(o_ref.dtype)

    blk = pl.BlockSpec((1, R, tc), lambda i, j: (i, 0, j))
    return pl.pallas_call(body, name=name, grid=(n, C // tc), in_specs=[blk, blk], out_specs=blk,
                          out_shape=SDS(a.shape, a.dtype), compiler_params=_cparams(dimension_semantics=("parallel", "parallel")))(a, b)


def _own_slot(landed, own, me):
    return lax.dynamic_update_slice(landed, own[None], (me,) + (0,) * own.ndim)


SMALL = (("g_mix", 1024), ("conv_w", 6144), ("conv_b", 1536), ("dt_bias", 16), ("a_log", 16), ("d_skip", 16),
         ("ssm_norm_w", 1024), ("g_q", 64), ("g_k", 64), ("f_bias", 16), ("g_xattn", 1024), ("g_mem", 1024),
         ("xg_q", 256), ("xg_k", 256), ("g_mlp", 1024))
SLAB_ROWS = 112
BIG = ("w_in", "w_out", "xq_w", "xkv_w", "xo_w", "w_up", "w_down")
WEIGHTS = ("g_mix", "w_in", "conv_w", "conv_b", "dt_bias", "a_log", "d_skip", "ssm_norm_w", "g_q", "g_k", "f_bias", "w_out",
           "g_xattn", "g_mem", "xq_w", "xkv_w", "xg_q", "xg_k", "xo_w", "g_mlp", "w_up", "w_down")
O_Z, O_XS, O_B, O_C, O_DT, O_Q, O_K, O_V, O_F, O_END = 0, 1024, 2048, 2304, 2560, 2576, 3600, 4624, 5648, 5664
IN_ROW_MOVES = ((O_Z, O_B - O_Z, C_Z), (O_Q, O_F - O_Q, C_Q), (O_B, O_Q - O_B, C_B), (O_F, O_END - O_F, C_DTF + 16))


def _pack_small(vals):
    rows = []
    for name, size in SMALL:
        flat = vals[name].reshape(-1).astype(F32)
        pad = -size % LANES
        rows.append(jnp.pad(flat, (0, pad)).reshape(-1, LANES))
    slab = jnp.concatenate(rows, axis=0)
    return jnp.pad(slab, ((0, SLAB_ROWS - slab.shape[0]), (0, 0)))


def _unpack_small(slab):
    out, r = {}, 0
    for name, size in SMALL:
        nr = -(-size // LANES)
        out[name] = slab[r:r + nr].reshape(-1)[:size]
        r += nr
    return out


def _step(p, m, v, x, mem, target):
    S = x.shape[0]
    TM = 256
    me = 4 * lax.axis_index("x") + 2 * lax.axis_index("y") + lax.axis_index("c")

    def rms(u, g, name):
        return _rw_fwd(_rms_fn, [_whole(u)], [_whole(g)], [(D_MODEL, BF16)], tm=TM, name=name)[0]

    def pin(param, token):
        return param + token[0:1, 0:1]

    def landed_with_own(pairs, scatter):
        out = []
        for src, land in pairs:
            own = lax.dynamic_index_in_dim(src, me, 0, keepdims=False) if scatter else src
            out.append(_own_slot(land, own, me))
        return out

    w_in_own, m_in_own, v_in_own = p["w_in"].T, m["w_in"].T, v["w_in"].T
    ag, ag_token = _exchange_start([w_in_own.astype(BF16), p["conv_w"]] + [p[n].astype(BF16) for n in BIG[1:]],
                                   scatter=False, name="allgather_start", near=(0,))
    h1 = rms(x, pin(p["g_mix"], ag_token), "rms_mix")
    (win_src, win_land), convw_pair = _copies_wait(ag[:2], h1, name="allgather_wait_in")
    relay, token = _copies_start([(win_land, None, _plan_relay)], name="allgather_relay_start")
    win_land = _copies_wait(relay, token, name="allgather_relay_wait")[0][1]
    win_g, convw_g = landed_with_own([(win_src, win_land), convw_pair], False)
    w_in_o = win_g.reshape(O_END, D_MODEL)
    w_in_t = _move_rows(w_in_o, IN_ROW_MOVES, P_COLS, name="w_in_rows")
    conv_w = convw_g.transpose(1, 0, 2).reshape(4, 1536)
    cw_xs, cw_bc = conv_w[:, :1024], conv_w[:, 1024:]
    cb_xs, cb_bc = p["conv_b"][:, :1024], p["conv_b"][:, 1024:]
    dt_bias, a_log, f_bias = p["dt_bias"].reshape(16, 1), p["a_log"].reshape(16, 1), p["f_bias"].reshape(16, 1)

    proj = _matmul(h1, w_in_t, mode="nt", tm=1024, tn=640, tk=1024, name="mm_in")
    xs_c = _conv_fwd(proj, C_XS, 1024, cw_xs, cb_xs, name="conv_xs")
    bc_c = _conv_fwd(proj, C_B, 512, cw_bc, cb_bc, name="conv_bc")
    dtf_t = proj[:, C_DTF:C_DTF + 32].T
    dt_t, acs_t, cum_t = _dtf_fwd(dtf_t, dt_bias, a_log, f_bias)
    dt_col, acs_col, cum_col = dt_t.T, acs_t.T, cum_t.T
    cum_row3 = cum_t.reshape(16, S // ATT_T, ATT_T).transpose(1, 0, 2)
    y_ssd, hs = _ssd_fwd(xs_c, dt_col, acs_col, acs_t, bc_c)
    gate_rows = [_whole(y_ssd), _whole(xs_c), (proj, C_Z, 1024)]
    gate_pars = [_whole(p["d_skip"]), _whole(p["ssm_norm_w"])]
    y_ssm = _rw_fwd(_gate_fn, gate_rows, gate_pars, [(1024, BF16)], tm=TM, name="gate")[0]
    gq2, gk2 = jnp.tile(p["g_q"], (1, 2)), jnp.tile(p["g_k"], (1, 2))
    o, lse = _fox_fwd(proj, C_Q, C_K, C_V, gq2, gk2, cum_col, cum_row3)
    mixed = jnp.concatenate([y_ssm, o], axis=1)
    wout_g, = landed_with_own(_copies_wait(ag[2:3], mixed, name="allgather_wait_out"), False)
    w_out = wout_g.reshape(2 * D_MODEL, D_MODEL)
    x1 = _matmul(mixed, w_out, mode="nn", tm=1024, tn=512, tk=2048, add=x, name="mm_out")
    xq_g, xkv_w, xo_g, w_up, wdown_g = landed_with_own(_copies_wait(ag[3:], x1, name="allgather_wait_rest"), False)
    xq_w = xq_g.reshape(D_MODEL, D_MODEL)
    xo_w = xo_g.reshape(D_MODEL, D_MODEL)
    w_down = wdown_g.reshape(4 * D_MODEL, D_MODEL)

    h2 = rms(x1, p["g_xattn"], "rms_xattn")
    mem_n = rms(mem, p["g_mem"], "rms_mem")
    q2 = _matmul(h2, xq_w, mode="nn", tm=1024, tn=512, tk=1024, name="mm_xq")
    kv = _matmul(mem_n, xkv_w, mode="nn", b_shards=True, tm=256, tn=256, tk=1024, name="mm_xkv")
    xa_rows = [(q2, X_D * h, X_D) for h in range(X_HEADS)]
    xa_pars = ([(kv, X_D * h, X_D) for h in range(X_HEADS)] + [(kv, D_MODEL + X_D * h, X_D) for h in range(X_HEADS)]
               + [_whole(p["xg_q"]), _whole(p["xg_k"])])
    o2 = _rw_fwd(_xattn_fn, xa_rows, xa_pars, [(D_MODEL, BF16)], tm=TM, name="xattn")[0]
    x2 = _matmul(o2, xo_w, mode="nn", tm=1024, tn=512, tk=1024, add=x1, name="mm_xo")

    h3 = rms(x2, p["g_mlp"], "rms_mlp")
    a, usq = _matmul(h3, w_up, mode="nn", b_shards=True, tm=1024, tn=512, tk=1024, name="mm_up", out_dtypes=[F32, BF16],
                     epilogue=lambda acc: (acc, jnp.square(jax.nn.relu(acc))))
    x3 = _matmul(usq, w_down, mode="nn", tm=1024, tn=512, tk=2048, add=x2, name="mm_down")
    dy, loss_part = _loss_head(x3, target, tm=TM)
    loss = lax.psum(loss_part[0, 0], ("x", "y", "c"))

    def row_shards(a):
        r, c = a.shape
        return a.reshape(N_DEV, r // N_DEV, c)

    g = {}
    g["w_down"] = _matmul(usq, dy, mode="tn", out_dtype=GRAD_WIRE, tm=1024, tn=1024, tk=1024, name="mm_d_wdown")
    da = _matmul(dy, w_down, mode="nt", tm=1024, tn=1024, tk=1024, name="mm_d_usq", out_dtype=BF16, extras=(a,),
                 epilogue=lambda acc, av: (2.0 * jax.nn.relu(av) * acc,))
    g["w_up"] = _matmul(h3, da, mode="tn", out_shards=True, out_dtype=GRAD_WIRE, tm=1024, tn=512, tk=1024, name="mm_d_wup")
    sent_mlp, token = _exchange_start([row_shards(g["w_down"]), g["w_up"]], scatter=True,
                                      name="grads_start_mlp")
    dh3 = _matmul(da, w_up, mode="nt", b_shards=True, tm=1024, tn=1024, tk=512, name="mm_d_h3")
    dx2, g["g_mlp"] = _rw_bwd(_rms_fn, [_whole(x2)], [_whole(pin(p["g_mlp"], token))], [_whole(dh3)], tm=TM,
                              name="rms_mlp_bwd", row_grads=[F32], adds={0: _whole(dy)})

    g["xo_w"] = _matmul(o2, dx2, mode="tn", out_dtype=GRAD_WIRE, tm=1024, tn=1024, tk=1024, name="mm_d_wxo")
    do2 = _matmul(dx2, xo_w, mode="nt", tm=1024, tn=512, tk=1024, name="mm_d_o2")
    xa = _rw_bwd(_xattn_fn, xa_rows, xa_pars, [_whole(do2)], tm=TM, name="xattn_bwd", row_grads=[BF16] * X_HEADS)
    dq2 = jnp.concatenate(xa[:X_HEADS], axis=1)
    dkv = jnp.concatenate(xa[X_HEADS:3 * X_HEADS], axis=1)
    g["xg_q"], g["xg_k"] = xa[3 * X_HEADS], xa[3 * X_HEADS + 1]
    g["xq_w"] = _matmul(h2, dq2, mode="tn", out_dtype=GRAD_WIRE, tm=1024, tn=1024, tk=1024, name="mm_d_wxq")
    dh2 = _matmul(dq2, xq_w, mode="nt", tm=1024, tn=512, tk=1024, name="mm_d_h2")
    g["xkv_w"] = _matmul(mem_n, dkv, mode="tn", out_shards=True, out_dtype=GRAD_WIRE, tm=1024, tn=256, tk=256,
                         name="mm_d_wxkv")
    dmem_n = _matmul(dkv, xkv_w, mode="nt", b_shards=True, tm=256, tn=1024, tk=256, name="mm_d_memn")
    g["g_mem"] = _rw_bwd(_rms_fn, [_whole(mem)], [_whole(p["g_mem"])], [_whole(dmem_n)], tm=TM, name="rms_mem_bwd",
                         row_grads=[None])[0]
    dx1, g["g_xattn"] = _rw_bwd(_rms_fn, [_whole(x1)], [_whole(p["g_xattn"])], [_whole(dh2)], tm=TM, name="rms_xattn_bwd",
                                row_grads=[F32], adds={0: _whole(dx2)})

    g["w_out"] = _matmul(mixed, dx1, mode="tn", out_dtype=GRAD_WIRE, tm=1024, tn=1024, tk=1024, name="mm_d_wout")
    sent_mid, token = _exchange_start(
        [row_shards(g["w_out"]), row_shards(g["xq_w"]), g["xkv_w"], row_shards(g["xo_w"])], scatter=True,
        name="grads_start_mid")
    dmixed = _matmul(dx1, w_out, mode="nt", tm=1024, tn=1024, tk=1024, name="mm_d_mixed")
    dq, dk, dv, dcum4, dgain = _fox_bwd(proj, C_Q, C_K, C_V, pin(gq2, token), gk2, cum_col, cum_row3, lse, dmixed, 1024)
    gains = _fold_gains(dgain)
    g["g_q"], g["g_k"] = gains[0:1, :ATT_D], gains[1:2, :ATT_D]
    dy_ssd, dxs_g, dz, g["d_skip"], g["ssm_norm_w"] = _rw_bwd(
        _gate_fn, gate_rows, gate_pars, [(dmixed, 0, 1024)], tm=TM, name="gate_bwd", row_grads=[F32, F32, BF16])
    dxs_s, ddt_col, dacs_col, dacs_row, d_b, d_c = _ssd_bwd(xs_c, dt_col, acs_col, acs_t, bc_c, hs, dy_ssd)
    dcum_t = dcum4[:, :, 0:2, :].transpose(0, 2, 1, 3).reshape(16, S)
    ddtf_t, ddtb, dalog, dfb = _dtf_bwd(dtf_t, dt_bias, a_log, f_bias, ddt_col.T, dacs_col.T, dacs_row, dcum_t)
    g["dt_bias"], g["a_log"], g["f_bias"] = ddtb, dalog, dfb
    dxs_raw, dcw_xs, dcb_xs = _conv_bwd(proj, C_XS, 1024, cw_xs, cb_xs, [dxs_s, dxs_g], name="conv_xs_bwd")
    dbc_raw, dcw_bc, dcb_bc = _conv_bwd(proj, C_B, 512, cw_bc, cb_bc, [jnp.concatenate([d_b, d_c], axis=1)],
                                        name="conv_bc_bwd")
    g["conv_w"] = jnp.concatenate([dcw_xs, dcw_bc], axis=1)
    g["conv_b"] = jnp.concatenate([dcb_xs, dcb_bc], axis=1)
    ddtf = jnp.pad(ddtf_t.T.astype(BF16), ((0, 0), (0, P_COLS - C_DTF - 32)))
    dproj = jnp.concatenate([dz, dxs_raw, dq, dk, dv, dbc_raw, ddtf], axis=1)
    dw_in_p = _matmul(dproj, h1, mode="tn", out_dtype=GRAD_WIRE, tm=640, tn=1024, tk=1024, name="mm_d_win")
    g["w_in"] = _move_rows(dw_in_p, [(dst, n, lo) for lo, n, dst in IN_ROW_MOVES], O_END, name="d_w_in_rows")
    half = N_DEV // 2
    send_in = row_shards(g["w_in"])
    pair, _ = _copies_start([(send_in, lax.empty((half,) + send_in.shape[1:], send_in.dtype), _plan_pair)],
                            name="grads_in_pair_start")
    dh1 = _matmul(dproj, w_in_t, mode="nn", tm=1024, tn=512, tk=1920, name="mm_d_h1")
    send_in, from_sibling = _copies_wait(pair, dh1, name="grads_in_pair_wait")[0]
    mine = jnp.stack([lax.dynamic_index_in_dim(send_in, me ^ (2 * j), 0, keepdims=False) for j in range(half)])
    chip_sums = _pair_sum(mine, from_sibling, name="grads_in_pair_sum")
    sent_in, token = _copies_start([(chip_sums, lax.empty(chip_sums.shape, chip_sums.dtype), _plan_chips)],
                                   name="grads_in_chip_start")
    grad_x, g["g_mix"] = _rw_bwd(_rms_fn, [_whole(x)], [_whole(pin(p["g_mix"], token))], [_whole(dh1)], tm=TM,
                                 name="rms_mix_bwd", row_grads=[F32], adds={0: _whole(dx1)})
    sent_small, _ = _exchange_start([_pack_small(g)], scatter=False, name="small_grads_start")

    grads, delta, new_m, new_v = {}, {}, {}, {}

    def update(names, sent, after, wait_name):
        parts = landed_with_own(_copies_wait(sent, after, name=wait_name), True)
        for name, part in zip(names, parts, strict=True):
            grads[name], delta[name], new_m[name], new_v[name] = _reduce_adamw(part, p[name], m[name], v[name], tr=128,
                                                                                name="adamw_" + name)

    update(("w_down", "w_up"), sent_mlp, grad_x, "grads_wait_mlp")
    update(("w_out", "xq_w", "xkv_w", "xo_w"), sent_mid, delta["w_up"], "grads_wait_mid")
    chip_sums, landed = _copies_wait(sent_in, delta["xo_w"], name="grads_in_chip_wait")[0]
    part = lax.dynamic_update_slice(landed, chip_sums[0:1], (0, 0, 0))
    res = _reduce_adamw(part, w_in_own, m_in_own, v_in_own, tr=part.shape[1], tc=256, name="adamw_w_in")
    grads["w_in"], delta["w_in"], new_m["w_in"], new_v["w_in"] = [r.T for r in res]
    small_parts = landed_with_own(_copies_wait(sent_small, delta["w_in"], name="small_grads_wait"), False)[0]
    zeros_cw = jnp.zeros((4, 1536), F32)
    slabs = [_pack_small({**d, "conv_w": zeros_cw}) for d in (p, m, v)]
    sg, sd, sm, sv = _reduce_adamw(small_parts, *slabs, tr=SLAB_ROWS, name="adamw_small")
    for dst, slab in ((grads, sg), (delta, sd), (new_m, sm), (new_v, sv)):
        for name, flat in _unpack_small(slab).items():
            if name != "conv_w":
                dst[name] = flat.reshape(p[name].shape)
    cw_shard = p["conv_w"].shape[1]
    grads["conv_w"] = lax.dynamic_slice(_unpack_small(sg)["conv_w"].reshape(4, 1536), (0, me * cw_shard), (4, cw_shard))
    delta["conv_w"], new_m["conv_w"], new_v["conv_w"] = _adamw(p["conv_w"], grads["conv_w"], m["conv_w"], v["conv_w"],
                                                               name="adamw_conv_w")
    return loss, grad_x, grads, delta, new_m, new_v


def kernel(x, mem, g_mix, w_in, conv_w, conv_b, dt_bias, a_log, d_skip, ssm_norm_w, g_q, g_k, f_bias, w_out, g_xattn, g_mem, xq_w, xkv_w, xg_q, xg_k, xo_w, g_mlp, w_up, w_down, loss_target, m_g_mix, m_w_in, m_conv_w, m_conv_b, m_dt_bias, m_a_log, m_d_skip, m_ssm_norm_w, m_g_q, m_g_k, m_f_bias, m_w_out, m_g_xattn, m_g_mem, m_xq_w, m_xkv_w, m_xg_q, m_xg_k, m_xo_w, m_g_mlp, m_w_up, m_w_down, v_g_mix, v_w_in, v_conv_w, v_conv_b, v_dt_bias, v_a_log, v_d_skip, v_ssm_norm_w, v_g_q, v_g_k, v_f_bias, v_w_out, v_g_xattn, v_g_mem, v_xq_w, v_xkv_w, v_xg_q, v_xg_k, v_xo_w, v_g_mlp, v_w_up, v_w_down):
    args = locals()
    drop = lambda t: t[0] if t.ndim == 3 else t
    p = {n: drop(args[n]) for n in WEIGHTS}
    m = {n: drop(args["m_" + n]) for n in WEIGHTS}
    v = {n: drop(args["v_" + n]) for n in WEIGHTS}
    loss, grad_x, grads, delta, new_m, new_v = _step(p, m, v, x[0], mem[0], loss_target[0])
    outs = [loss, grad_x[None]]
    for d in (grads, delta, new_m, new_v):
        outs += [d[n].reshape(args[n].shape) for n in WEIGHTS]
    return tuple(outs)
```

```python
import functools
import math

import jax
import jax.numpy as jnp
from jax import lax
from jax.experimental import pallas as pl
from jax.experimental.pallas import tpu as pltpu

F32, BF16 = jnp.float32, jnp.bfloat16
SDS = jax.ShapeDtypeStruct
HI = lax.Precision.HIGHEST
MESH = pl.DeviceIdType.MESH

N_DEV = 8
EPS = 1e-5
D_MODEL = 1024
SSM_HEADS, SSM_P, SSM_N, SSM_GROUPS, CHUNK = 16, 64, 128, 2, 128
ATT_HEADS, ATT_D = 16, 64
X_HEADS, X_D = 4, 256
LANES = 128
VMEM_LIMIT = 48 * 1024 * 1024
NEG = -1e30

GRAD_WIRE = BF16
ADAM_LR, ADAM_B1, ADAM_B2, ADAM_EPS, ADAM_WD, ADAM_STEP = 0.001, 0.9, 0.999, 1e-08, 0.01, 10

C_Z, C_XS, C_Q, C_K, C_V, C_B, C_C, C_DTF, P_COLS = 0, 1024, 2048, 3072, 4096, 5120, 5376, 5632, 5760

_NN = (((1,), (0,)), ((), ()))
_NT = (((1,), (1,)), ((), ()))
_TN = (((0,), (0,)), ((), ()))


def _cparams(**kw):
    return pltpu.CompilerParams(vmem_limit_bytes=VMEM_LIMIT, **kw)


def _bdot(a, b, dn):
    return lax.dot_general(a.astype(BF16), b.astype(BF16), dn, preferred_element_type=F32)


@jax.custom_vjp
def mm_nn(a, b):
    return _bdot(a, b, _NN)


mm_nn.defvjp(lambda a, b: (mm_nn(a, b), (a, b)), lambda r, g: (_bdot(g, r[1], _NT), _bdot(r[0], g, _TN)))


@jax.custom_vjp
def mm_nt(a, b):
    return _bdot(a, b, _NT)


mm_nt.defvjp(lambda a, b: (mm_nt(a, b), (a, b)), lambda r, g: (_bdot(g, r[1], _NN), _bdot(g, r[0], _TN)))


@jax.custom_vjp
def mm_tn(a, b):
    return _bdot(a, b, _TN)


mm_tn.defvjp(lambda a, b: (mm_tn(a, b), (a, b)), lambda r, g: (_bdot(r[1], g, _NT), _bdot(r[0], g, _NN)))


def _cdot(x, c):
    return jnp.dot(x, c, precision=HI, preferred_element_type=F32)


def _iota(shape, dim):
    return lax.broadcasted_iota(jnp.int32, shape, dim)


def _matmul(a, b, *, mode, tm, tn, tk, name, out_dtype=F32, add=None, extras=(), epilogue=None, out_dtypes=None,
            b_shards=False, out_shards=False):
    if mode == "tn":
        K, M = a.shape
    else:
        M, K = a.shape
    if b_shards:
        N = b.shape[1] if mode == "nt" else b.shape[0] * b.shape[2]
        tn, tk = (tn, b.shape[2]) if mode == "nt" else (b.shape[2], tk)
    else:
        N = b.shape[0] if mode == "nt" else b.shape[1]
    tm, tn, tk = min(tm, M), min(tn, N), min(tk, K)
    assert M % tm == 0 and N % tn == 0 and K % tk == 0, (name, M, N, K, tm, tn, tk)
    assert not b_shards or (K // tk if mode == "nt" else N // tn) == b.shape[0], name
    assert not (out_shards and (extras or add is not None)), name
    nk = K // tk
    dn = {"nn": _NN, "nt": _NT, "tn": _TN}[mode]
    if add is not None:
        extras, epilogue = (add,), lambda acc, r: (acc + r,)
    elif epilogue is None:
        epilogue = lambda acc: (acc,)
    out_dtypes = out_dtypes or [out_dtype]
    ne, no = len(extras), len(out_dtypes)

    def body(*refs):
        a_ref, b_ref = refs[:2]
        e_refs, o_refs = refs[2:2 + ne], refs[2 + ne:2 + ne + no]

        def finish(acc):
            res = epilogue(acc, *[e[...] for e in e_refs])
            for o_ref, v in zip(o_refs, res, strict=True):
                o_ref[...] = v.astype(o_ref.dtype)

        prod = _bdot(a_ref[...], b_ref[...], dn)
        if nk == 1:
            finish(prod)
            return
        acc_ref = refs[-1]
        k = pl.program_id(2)

        @pl.when(k == 0)
        def _():
            acc_ref[...] = prod

        @pl.when(jnp.logical_and(k > 0, k < nk - 1))
        def _():
            acc_ref[...] += prod

        @pl.when(k == nk - 1)
        def _():
            finish(acc_ref[...] + prod)

    a_spec = pl.BlockSpec((tk, tm), lambda i, j, k: (k, i)) if mode == "tn" else pl.BlockSpec((tm, tk), lambda i, j, k: (i, k))
    if b_shards and mode == "nt":
        b_spec = pl.BlockSpec((None, tn, tk), lambda i, j, k: (k, j, 0))
    elif b_shards:
        b_spec = pl.BlockSpec((None, tk, tn), lambda i, j, k: (j, k, 0))
    elif mode == "nt":
        b_spec = pl.BlockSpec((tn, tk), lambda i, j, k: (j, k))
    else:
        b_spec = pl.BlockSpec((tk, tn), lambda i, j, k: (k, j))
    if out_shards:
        o_spec, o_shape = pl.BlockSpec((None, tm, tn), lambda i, j, k: (j, i, 0)), (N // tn, M, tn)
    else:
        o_spec, o_shape = pl.BlockSpec((tm, tn), lambda i, j, k: (i, j)), (M, N)
    res = pl.pallas_call(
        body, name=name, grid=(M // tm, N // tn, nk), in_specs=[a_spec, b_spec] + [o_spec] * ne, out_specs=[o_spec] * no,
        out_shape=[SDS(o_shape, dt) for dt in out_dtypes], scratch_shapes=[pltpu.VMEM((tm, tn), F32)] if nk > 1 else [],
        compiler_params=_cparams(dimension_semantics=("parallel", "parallel", "arbitrary")),
    )(a, b, *extras)
    return res[0] if no == 1 else res


def _row_spec(tm, spec):
    _, c0, w = spec
    assert c0 % w == 0
    return pl.BlockSpec((tm, w), functools.partial(lambda i, cb: (i, cb), cb=c0 // w))


def _par_spec(spec):
    arr, c0, w = spec
    assert c0 % w == 0
    return pl.BlockSpec((arr.shape[0], w), functools.partial(lambda i, cb: (0, cb), cb=c0 // w))


def _whole(arr):
    return (arr, 0, arr.shape[1])


def _rw_fwd(fn, rows, params, outs, *, tm, name):
    M = rows[0][0].shape[0]
    nr, npar = len(rows), len(params)

    def body(*refs):
        rv = [r[...].astype(F32) for r in refs[:nr]]
        pv = [p[...].astype(F32) for p in refs[nr:nr + npar]]
        res = fn(*rv, *pv)
        for o_ref, v in zip(refs[nr + npar:], res, strict=True):
            o_ref[...] = v.astype(o_ref.dtype)

    return pl.pallas_call(
        body, name=name, grid=(M // tm,),
        in_specs=[_row_spec(tm, r) for r in rows] + [_par_spec(p) for p in params],
        out_specs=[pl.BlockSpec((tm, w), lambda i: (i, 0)) for w, _ in outs],
        out_shape=[SDS((M, w), dt) for w, dt in outs],
        compiler_params=_cparams(dimension_semantics=("parallel",)),
    )(*[r[0] for r in rows], *[p[0] for p in params])


def _rw_bwd(fn, rows, params, cts, *, tm, name, row_grads, adds=None):
    M = rows[0][0].shape[0]
    adds = adds or {}
    nr, npar, nc = len(rows), len(params), len(cts)
    add_keys = sorted(adds)
    want = [k for k in range(nr) if row_grads[k] is not None]

    def body(*refs):
        pos = 0
        r_refs = refs[pos:pos + nr]; pos += nr
        p_refs = refs[pos:pos + npar]; pos += npar
        c_refs = refs[pos:pos + nc]; pos += nc
        a_refs = dict(zip(add_keys, refs[pos:pos + len(add_keys)])); pos += len(add_keys)
        dr_refs = dict(zip(want, refs[pos:pos + len(want)])); pos += len(want)
        dp_refs = refs[pos:pos + npar]
        rv = [r[...].astype(F32) for r in r_refs]
        pv = [p[...].astype(F32) for p in p_refs]
        _, vjp = jax.vjp(fn, *rv, *pv)
        g = vjp(tuple(c[...].astype(F32) for c in c_refs))
        for k in want:
            v = g[k]
            if k in a_refs:
                v = v + a_refs[k][...].astype(F32)
            dr_refs[k][...] = v.astype(dr_refs[k].dtype)
        first = pl.program_id(0) == 0
        for j in range(npar):
            @pl.when(first)
            def _(j=j):
                dp_refs[j][...] = jnp.zeros_like(dp_refs[j])
            dp_refs[j][...] += g[nr + j]

    res = pl.pallas_call(
        body, name=name, grid=(M // tm,),
        in_specs=([_row_spec(tm, r) for r in rows] + [_par_spec(p) for p in params] + [_row_spec(tm, c) for c in cts]
                  + [_row_spec(tm, adds[k]) for k in add_keys]),
        out_specs=([pl.BlockSpec((tm, rows[k][2]), lambda i: (i, 0)) for k in want]
                   + [pl.BlockSpec((p[0].shape[0], p[2]), lambda i: (0, 0)) for p in params]),
        out_shape=([SDS((M, rows[k][2]), row_grads[k]) for k in want] + [SDS((p[0].shape[0], p[2]), F32) for p in params]),
        compiler_params=_cparams(dimension_semantics=("arbitrary",)),
    )(*[r[0] for r in rows], *[p[0] for p in params], *[c[0] for c in cts], *[adds[k][0] for k in add_keys])
    return res


def _rms_fn(x, g):
    r = lax.rsqrt(jnp.mean(x * x, axis=-1, keepdims=True) + EPS)
    return (x * r * g,)


def _seg_mats(width, seg):
    n = width // seg
    p = (_iota((width, n), 0) // seg == _iota((width, n), 1)).astype(F32)
    e = (_iota((n, width), 1) // seg == _iota((n, width), 0)).astype(F32)
    return p, e


def _gate_fn(y, xs, z, dskip, w):
    width = SSM_HEADS * SSM_P
    _, e = _seg_mats(width, SSM_P)
    y = (y + _cdot(dskip, e) * xs) * (z * jax.nn.sigmoid(z))
    g0 = _iota((1, width), 1) < width // SSM_GROUPS
    y2 = y * y
    gw = width // SSM_GROUPS
    ms0 = jnp.sum(jnp.where(g0, y2, 0.0), axis=-1, keepdims=True) * (1.0 / gw)
    ms1 = jnp.sum(jnp.where(g0, 0.0, y2), axis=-1, keepdims=True) * (1.0 / gw)
    r = jnp.where(g0, lax.rsqrt(ms0 + EPS), lax.rsqrt(ms1 + EPS))
    return (y * r * w,)


def _xattn_fn(q0, q1, q2, q3, k0, k1, k2, k3, v0, v1, v2, v3, gq, gk):
    def norm(u, g):
        return u * lax.rsqrt(jnp.mean(u * u, axis=-1, keepdims=True) + EPS) * g
    outs = []
    for q, k, v in ((q0, k0, v0), (q1, k1, v1), (q2, k2, v2), (q3, k3, v3)):
        s = mm_nt(norm(q, gq), norm(k, gk)) * (X_D ** -0.5)
        p = jnp.exp(s - lax.stop_gradient(jnp.max(s, axis=-1, keepdims=True)))
        p = p / jnp.sum(p, axis=-1, keepdims=True)
        outs.append(mm_nn(p, v))
    return (jnp.concatenate(outs, axis=-1),)


CONV_TC = 256


def _shift_down(u, k):
    if k == 0:
        return u
    return jnp.where(_iota(u.shape, 0) >= k, pltpu.roll(u, k, axis=0), 0.0)


def _shift_up(u, k):
    if k == 0:
        return u
    n = u.shape[0]
    return jnp.where(_iota(u.shape, 0) < n - k, pltpu.roll(u, n - k, axis=0), 0.0)


def _conv_pre(u, w_ref, b):
    pre = b + w_ref[3:4, :] * u
    for k in (1, 2, 3):
        pre = pre + w_ref[3 - k:4 - k, :] * _shift_down(u, k)
    return pre


def _conv_fwd(src, c0, width, w, b, *, name):
    S = src.shape[0]
    cb0 = c0 // CONV_TC

    def body(u_ref, w_ref, b_ref, o_ref):
        pre = _conv_pre(u_ref[...], w_ref, b_ref[...])
        o_ref[...] = pre * jax.nn.sigmoid(pre)

    return pl.pallas_call(
        body, name=name, grid=(width // CONV_TC,),
        in_specs=[pl.BlockSpec((S, CONV_TC), lambda j: (0, cb0 + j)), pl.BlockSpec((4, CONV_TC), lambda j: (0, j)),
                  pl.BlockSpec((1, CONV_TC), lambda j: (0, j))],
        out_specs=pl.BlockSpec((S, CONV_TC), lambda j: (0, j)), out_shape=SDS((S, width), F32),
        compiler_params=_cparams(dimension_semantics=("parallel",)),
    )(src, w, b)


def _conv_bwd(src, c0, width, w, b, douts, *, name):
    S = src.shape[0]
    cb0 = c0 // CONV_TC
    nd = len(douts)

    def body(*refs):
        u_ref, w_ref, b_ref = refs[:3]
        d_refs = refs[3:3 + nd]
        du_ref, dw_ref, db_ref = refs[3 + nd:]
        u = u_ref[...]
        pre = _conv_pre(u, w_ref, b_ref[...])
        sg = jax.nn.sigmoid(pre)
        dout = d_refs[0][...]
        for r in d_refs[1:]:
            dout = dout + r[...]
        dpre = dout * (sg * (1.0 + pre * (1.0 - sg)))
        du = w_ref[3:4, :] * dpre
        dw_ref[3:4, :] = jnp.sum(dpre * u, axis=0, keepdims=True)
        for k in (1, 2, 3):
            du = du + w_ref[3 - k:4 - k, :] * _shift_up(dpre, k)
            dw_ref[3 - k:4 - k, :] = jnp.sum(dpre * _shift_down(u, k), axis=0, keepdims=True)
        du_ref[...] = du.astype(du_ref.dtype)
        db_ref[...] = jnp.sum(dpre, axis=0, keepdims=True)

    return pl.pallas_call(
        body, name=name, grid=(width // CONV_TC,),
        in_specs=[pl.BlockSpec((S, CONV_TC), lambda j: (0, cb0 + j)), pl.BlockSpec((4, CONV_TC), lambda j: (0, j)),
                  pl.BlockSpec((1, CONV_TC), lambda j: (0, j))] + [pl.BlockSpec((S, CONV_TC), lambda j: (0, j))] * nd,
        out_specs=[pl.BlockSpec((S, CONV_TC), lambda j: (0, j)), pl.BlockSpec((4, CONV_TC), lambda j: (0, j)),
                   pl.BlockSpec((1, CONV_TC), lambda j: (0, j))],
        out_shape=[SDS((S, width), BF16), SDS((4, width), F32), SDS((1, width), F32)],
        compiler_params=_cparams(dimension_semantics=("parallel",)),
    )(src, w, b, *douts)


def _softplus(x):
    return jnp.maximum(x, 0.0) + jnp.log(1.0 + jnp.exp(-jnp.abs(x)))


def _prefix_sum(x, seg):
    n = x.shape[1]
    pos = _iota(x.shape, 1) % seg
    k = 1
    while k < seg:
        x = x + jnp.where(pos >= k, pltpu.roll(x, k, axis=1), 0.0)
        k *= 2
    return x


def _suffix_sum(x, seg):
    n = x.shape[1]
    pos = _iota(x.shape, 1) % seg
    k = 1
    while k < seg:
        x = x + jnp.where(pos + k < seg, pltpu.roll(x, n - k, axis=1), 0.0)
        k *= 2
    return x


def _dtf_fwd(dtf_t, dt_bias, a_log, f_bias):
    S = dtf_t.shape[1]

    def body(x_ref, db_ref, al_ref, fb_ref, dt_ref, acs_ref, cum_ref):
        dt = _softplus(x_ref[0:16, :] + db_ref[...])
        dt_ref[...] = dt
        acs_ref[...] = _prefix_sum(dt * (-jnp.exp(al_ref[...])), CHUNK)
        cum_ref[...] = _prefix_sum(-_softplus(-(x_ref[16:32, :] + fb_ref[...])), S)

    return pl.pallas_call(body, name="dtf_fwd", out_shape=[SDS((16, S), F32)] * 3, compiler_params=_cparams())(
        dtf_t, dt_bias, a_log, f_bias)


def _dtf_bwd(dtf_t, dt_bias, a_log, f_bias, d_dt, d_acs_a, d_acs_b, d_cum):
    S = dtf_t.shape[1]

    def body(x_ref, db_ref, al_ref, fb_ref, ddt_ref, da1_ref, da2_ref, dc_ref, dx_ref, ddb_ref, dal_ref, dfb_ref):
        xd = x_ref[0:16, :] + db_ref[...]
        dt = _softplus(xd)
        a = -jnp.exp(al_ref[...])
        d_da = _suffix_sum(da1_ref[...] + da2_ref[...], CHUNK)
        d_dt = ddt_ref[...] + d_da * a
        dal_ref[...] = jnp.sum(d_da * dt, axis=1, keepdims=True) * a
        d_xd = d_dt * jax.nn.sigmoid(xd)
        ddb_ref[...] = jnp.sum(d_xd, axis=1, keepdims=True)
        xf = x_ref[16:32, :] + fb_ref[...]
        d_xf = _suffix_sum(dc_ref[...], S) * jax.nn.sigmoid(-xf)
        dfb_ref[...] = jnp.sum(d_xf, axis=1, keepdims=True)
        dx_ref[0:16, :] = d_xd
        dx_ref[16:32, :] = d_xf

    return pl.pallas_call(body, name="dtf_bwd", out_shape=[SDS((32, S), F32)] + [SDS((16, 1), F32)] * 3,
                          compiler_params=_cparams())(dtf_t, dt_bias, a_log, f_bias, d_dt, d_acs_a, d_acs_b, d_cum)


SSM_PAIRS = SSM_HEADS // 2 // SSM_GROUPS


def _ssd_pair(xs, dtc, acol, arow, bm, cm, cbm, h, hp):
    L = CHUNK
    first = _iota((1, LANES), 1) < SSM_P
    i16, s16 = _iota((L, 16), 1), _iota((16, L), 0)
    ha, hb = 2 * hp, 2 * hp + 1

    def selc(blk, hh):
        return jnp.sum(jnp.where(i16 == hh, blk, 0.0), axis=1, keepdims=True)

    def selr(blk, hh):
        return jnp.sum(jnp.where(s16 == hh, blk, 0.0), axis=0, keepdims=True)

    x = xs * jnp.where(first, selc(dtc, ha), selc(dtc, hb))
    ca, cb, ra, rb = selc(acol, ha), selc(acol, hb), selr(arow, ha), selr(arow, hb)
    tri = _iota((L, L), 0) >= _iota((L, L), 1)
    la = jnp.exp(jnp.where(tri, ca - ra, NEG))
    lb = jnp.exp(jnp.where(tri, cb - rb, NEG))
    y = jnp.where(first, mm_nn(cbm * la, x), mm_nn(cbm * lb, x))
    y = y + jnp.where(first, jnp.exp(ca), jnp.exp(cb)) * mm_nn(cm, h)
    last = _iota((1, L), 1) == L - 1
    ala = jnp.sum(jnp.where(last, ra, 0.0), axis=1, keepdims=True)
    alb = jnp.sum(jnp.where(last, rb, 0.0), axis=1, keepdims=True)
    dec = jnp.where(first, jnp.exp(ala - ca), jnp.exp(alb - cb))
    hn = jnp.where(first, jnp.exp(ala), jnp.exp(alb)) * h + mm_tn(bm, x * dec)
    return y, hn


def _ssd_group(*args, grp):
    xs, (dtc, acol, arow, bm, cm), hs = args[:SSM_PAIRS], args[SSM_PAIRS:SSM_PAIRS + 5], args[SSM_PAIRS + 5:]
    cbm = mm_nt(cm, bm)
    res = [_ssd_pair(xs[j], dtc, acol, arow, bm, cm, cbm, hs[j], SSM_PAIRS * grp + j) for j in range(SSM_PAIRS)]
    return tuple(r[0] for r in res) + tuple(r[1] for r in res)


def _ssd_specs(nc, rev):
    L = CHUNK
    cidx = (lambda c: nc - 1 - c) if rev else (lambda c: c)
    return dict(
        xs=pl.BlockSpec((L, SSM_PAIRS * LANES), lambda c, g: (cidx(c), g)),
        col=pl.BlockSpec((L, 16), lambda c, g: (cidx(c), 0)),
        row=pl.BlockSpec((16, L), lambda c, g: (0, cidx(c))),
        b=pl.BlockSpec((L, SSM_N), lambda c, g: (cidx(c), g)),
        c=pl.BlockSpec((L, SSM_N), lambda c, g: (cidx(c), SSM_GROUPS + g)),
        st=pl.BlockSpec((1, SSM_PAIRS, SSM_N, LANES), lambda c, g: (cidx(c), g, 0, 0)),
    )


def _lane_pieces(v):
    return [v[:, LANES * j:LANES * (j + 1)] for j in range(v.shape[1] // LANES)]


def _ssd_fwd(xs, dt_col, acs_col, acs_row, bc):
    S = xs.shape[0]
    nc, nhp = S // CHUNK, SSM_HEADS // 2
    sp = _ssd_specs(nc, False)

    def body(xs_ref, dt_ref, ac_ref, ar_ref, b_ref, c_ref, y_ref, hs_ref, h_scr):
        c, g = pl.program_id(0), pl.program_id(1)

        @pl.when(c == 0)
        def _():
            for j in range(SSM_PAIRS):
                h_scr[SSM_PAIRS * g + j] = jnp.zeros((SSM_N, LANES), F32)

        hs = [h_scr[SSM_PAIRS * g + j] for j in range(SSM_PAIRS)]
        for j in range(SSM_PAIRS):
            hs_ref[0, j] = hs[j]
        res = _ssd_group(*_lane_pieces(xs_ref[...]), dt_ref[...], ac_ref[...], ar_ref[...], b_ref[...], c_ref[...], *hs,
                         grp=g)
        y_ref[...] = jnp.concatenate(res[:SSM_PAIRS], axis=1)
        for j in range(SSM_PAIRS):
            h_scr[SSM_PAIRS * g + j] = res[SSM_PAIRS + j]

    return pl.pallas_call(
        body, name="ssd_fwd", grid=(nc, SSM_GROUPS),
        in_specs=[sp["xs"], sp["col"], sp["col"], sp["row"], sp["b"], sp["c"]],
        out_specs=[sp["xs"], sp["st"]],
        out_shape=[SDS((S, SSM_HEADS * SSM_P), F32), SDS((nc, nhp, SSM_N, LANES), F32)],
        scratch_shapes=[pltpu.VMEM((nhp, SSM_N, LANES), F32)],
        compiler_params=_cparams(dimension_semantics=("arbitrary", "arbitrary")),
    )(xs, dt_col, acs_col, acs_row, bc, bc)


def _ssd_bwd(xs, dt_col, acs_col, acs_row, bc, hs, dy):
    S = xs.shape[0]
    nc, nhp = S // CHUNK, SSM_HEADS // 2
    sp = _ssd_specs(nc, True)

    def body(xs_ref, dt_ref, ac_ref, ar_ref, b_ref, c_ref, hs_ref, dy_ref,
             dxs_ref, ddt_ref, dac_ref, dar_ref, db_ref, dc_ref, dh_scr):
        c, g = pl.program_id(0), pl.program_id(1)

        @pl.when(c == 0)
        def _():
            for j in range(SSM_PAIRS):
                dh_scr[SSM_PAIRS * g + j] = jnp.zeros((SSM_N, LANES), F32)

        _, vjp = jax.vjp(functools.partial(_ssd_group, grp=g), *_lane_pieces(xs_ref[...]), dt_ref[...], ac_ref[...],
                         ar_ref[...], b_ref[...], c_ref[...], *[hs_ref[0, j] for j in range(SSM_PAIRS)])
        grads = vjp(tuple(_lane_pieces(dy_ref[...])) + tuple(dh_scr[SSM_PAIRS * g + j] for j in range(SSM_PAIRS)))
        dxs_ref[...] = jnp.concatenate(grads[:SSM_PAIRS], axis=1)
        ddt, dac, dar, db, dc = grads[SSM_PAIRS:SSM_PAIRS + 5]
        for j in range(SSM_PAIRS):
            dh_scr[SSM_PAIRS * g + j] = grads[SSM_PAIRS + 5 + j]
        db_ref[...] = db
        dc_ref[...] = dc

        @pl.when(g == 0)
        def _():
            ddt_ref[...] = ddt
            dac_ref[...] = dac
            dar_ref[...] = dar

        @pl.when(g > 0)
        def _():
            ddt_ref[...] += ddt
            dac_ref[...] += dac
            dar_ref[...] += dar

    return pl.pallas_call(
        body, name="ssd_bwd", grid=(nc, SSM_GROUPS),
        in_specs=[sp["xs"], sp["col"], sp["col"], sp["row"], sp["b"], sp["c"], sp["st"], sp["xs"]],
        out_specs=[sp["xs"], sp["col"], sp["col"], sp["row"], sp["b"], sp["b"]],
        out_shape=[SDS((S, SSM_HEADS * SSM_P), F32), SDS((S, 16), F32), SDS((S, 16), F32), SDS((16, S), F32),
                   SDS((S, SSM_GROUPS * SSM_N), F32), SDS((S, SSM_GROUPS * SSM_N), F32)],
        scratch_shapes=[pltpu.VMEM((nhp, SSM_N, LANES), F32)],
        compiler_params=_cparams(dimension_semantics=("arbitrary", "arbitrary")),
    )(xs, dt_col, acs_col, acs_row, bc, bc, hs, dy)


ATT_T = 512


def _pick_col(blk, h):
    return jnp.sum(jnp.where(_iota(blk.shape, 1) == h, blk, 0.0), axis=1, keepdims=True)


def _pick_row(blk, h):
    return jnp.sum(jnp.where(_iota(blk.shape, 0) == h, blk, 0.0), axis=0, keepdims=True)


def _pair_norm(x, g2, first):
    x2 = x * x
    sa = jnp.sum(jnp.where(first, x2, 0.0), axis=1, keepdims=True)
    sb = jnp.sum(jnp.where(first, 0.0, x2), axis=1, keepdims=True)
    r = jnp.where(first, lax.rsqrt(sa * (1.0 / ATT_D) + EPS), lax.rsqrt(sb * (1.0 / ATT_D) + EPS))
    return x * r * g2, r


def _pair_norm_bwd(dxn, x, r, g2, first):
    t = dxn * g2
    tx = t * x
    ma = jnp.sum(jnp.where(first, tx, 0.0), axis=1, keepdims=True)
    mb = jnp.sum(jnp.where(first, 0.0, tx), axis=1, keepdims=True)
    dx = r * (t - x * (r * r) * (jnp.where(first, ma, mb) * (1.0 / ATT_D)))
    return dx, jnp.sum(dxn * x * r, axis=0, keepdims=True)


def _fox_fwd(src, q_c0, k_c0, v_c0, gq2, gk2, cum_col, cum_row3):
    S = src.shape[0]
    T = ATT_T
    nq, nhp = S // T, ATT_HEADS // 2
    qb0, kb0, vb0 = q_c0 // LANES, k_c0 // LANES, v_c0 // LANES
    scale = ATT_D ** -0.5

    def body(q_ref, kraw_ref, v_ref, gq_ref, gk_ref, cc_ref, cr_ref, o_ref, l_ref, k_ref):
        hp, i = pl.program_id(0), pl.program_id(1)
        first = _iota((1, LANES), 1) < ATT_D

        @pl.when(i == 0)
        def _():
            k_ref[...] = _pair_norm(kraw_ref[...], gk_ref[...], first)[0].astype(BF16)

        q = (_pair_norm(q_ref[...], gq_ref[...], first)[0] * scale).astype(BF16)
        zero = jnp.zeros_like(q)
        qs = (jnp.where(first, q, zero), jnp.where(first, zero, q))
        cc = cc_ref[...]
        cq = (_pick_col(cc, 2 * hp), _pick_col(cc, 2 * hp + 1))
        tri = _iota((T, T), 0) >= _iota((T, T), 1)

        def tile(j, carry, diagonal):
            off = pl.multiple_of(j * T, T)
            k = k_ref[pl.ds(off, T), :]
            v = v_ref[pl.ds(off, T), :].astype(BF16)
            cr = cr_ref[j]
            out = []
            for hh in range(2):
                m, l, acc = carry[3 * hh:3 * hh + 3]
                s = _bdot(qs[hh], k, _NT) + (cq[hh] - _pick_row(cr, 2 * hp + hh))
                if diagonal:
                    s = jnp.where(tri, s, NEG)
                m_new = jnp.maximum(m, jnp.max(s, axis=1, keepdims=True))
                alpha = jnp.exp(m - m_new)
                p = jnp.exp(s - m_new)
                out += [m_new, alpha * l + jnp.sum(p, axis=1, keepdims=True), alpha * acc + _bdot(p, v, _NN)]
            return tuple(out)

        init = (jnp.full((T, 1), NEG, F32), jnp.zeros((T, 1), F32), jnp.zeros((T, LANES), F32)) * 2
        carry = lax.fori_loop(0, i, lambda j, c: tile(j, c, False), init)
        ma, la, acca, mb, lb, accb = tile(i, carry, True)
        o_ref[...] = jnp.where(first, acca / la, accb / lb).astype(o_ref.dtype)
        l_ref[...] = jnp.where(first, ma + jnp.log(la), mb + jnp.log(lb))

    gain = pl.BlockSpec((1, LANES), lambda hp, i: (0, 0))
    return pl.pallas_call(
        body, name="fox_fwd", grid=(nhp, nq),
        in_specs=[pl.BlockSpec((T, LANES), lambda hp, i: (i, qb0 + hp)), pl.BlockSpec((S, LANES), lambda hp, i: (0, kb0 + hp)),
                  pl.BlockSpec((S, LANES), lambda hp, i: (0, vb0 + hp)), gain, gain,
                  pl.BlockSpec((T, 16), lambda hp, i: (i, 0)), pl.BlockSpec((nq, 16, T), lambda hp, i: (0, 0, 0))],
        out_specs=[pl.BlockSpec((T, LANES), lambda hp, i: (i, hp))] * 2,
        out_shape=[SDS((S, ATT_HEADS * ATT_D), BF16), SDS((S, ATT_HEADS * ATT_D), F32)],
        scratch_shapes=[pltpu.VMEM((S, LANES), BF16)],
        compiler_params=_cparams(dimension_semantics=("arbitrary", "arbitrary")),
    )(src, src, src, gq2, gk2, cum_col, cum_row3)


def _fox_bwd(src, q_c0, k_c0, v_c0, gq2, gk2, cum_col, cum_row3, lse, dsrc, d_c0):
    S = src.shape[0]
    T = ATT_T
    nq, nhp = S // T, ATT_HEADS // 2
    qb0, kb0, vb0, db0 = q_c0 // LANES, k_c0 // LANES, v_c0 // LANES, d_c0 // LANES
    scale = ATT_D ** -0.5

    def body(q_ref, kraw_ref, v_ref, gq_ref, gk_ref, cc_ref, cr_ref, l_ref, do_ref,
             dq_ref, dk_ref, dv_ref, dc_ref, dg_ref, k_ref, dk_acc, dv_acc, p_scr, dp_scr):
        hp, i = pl.program_id(0), pl.program_id(1)
        first = _iota((1, LANES), 1) < ATT_D
        tri = _iota((T, T), 0) >= _iota((T, T), 1)

        @pl.when(i == 0)
        def _():
            k_ref[...] = _pair_norm(kraw_ref[...], gk_ref[...], first)[0].astype(BF16)
            dk_acc[...] = jnp.zeros_like(dk_acc)
            dv_acc[...] = jnp.zeros_like(dv_acc)
            dc_ref[...] = jnp.zeros_like(dc_ref)
            dg_ref[...] = jnp.zeros_like(dg_ref)

        q_raw = q_ref[...]
        qn, rq = _pair_norm(q_raw, gq_ref[...], first)
        q = (qn * scale).astype(BF16)
        zq = jnp.zeros_like(q)
        dob = do_ref[...].astype(BF16)
        zd = jnp.zeros_like(dob)
        lse_blk, cc = l_ref[...], cc_ref[...]
        dq = jnp.zeros((T, LANES), F32)
        for hh in range(2):
            sel = first if hh == 0 else jnp.logical_not(first)
            qh, doh = jnp.where(sel, q, zq), jnp.where(sel, dob, zd)
            bias_q = _pick_col(cc, 2 * hp + hh) - jnp.max(jnp.where(sel, lse_blk, NEG), axis=1, keepdims=True)

            def probs(j, delta, diagonal):
                off = pl.multiple_of(j * T, T)
                s = _bdot(qh, k_ref[pl.ds(off, T), :], _NT) + (bias_q - _pick_row(cr_ref[j], 2 * hp + hh))
                if diagonal:
                    s = jnp.where(tri, s, NEG)
                p = jnp.exp(s)
                dp = _bdot(doh, v_ref[pl.ds(off, T), :], _NT)
                p_scr[j] = p
                dp_scr[j] = dp
                return delta + jnp.sum(p * dp, axis=1, keepdims=True)

            delta = lax.fori_loop(0, i, lambda j, d: probs(j, d, False), jnp.zeros((T, 1), F32))
            delta = probs(i, delta, True)

            def grads(j, dq):
                off = pl.multiple_of(j * T, T)
                p = p_scr[j]
                ds = p * (dp_scr[j] - delta)
                dv_acc[pl.ds(off, T), :] += _bdot(p, doh, _TN)
                dk_acc[pl.ds(off, T), :] += _bdot(ds, qh, _TN)
                dc_ref[0, j, hh:hh + 1, :] -= jnp.sum(ds, axis=0, keepdims=True)
                zk = jnp.zeros((T, LANES), BF16)
                return dq + _bdot(ds, jnp.where(sel, k_ref[pl.ds(off, T), :], zk), _NN)

            dq = lax.fori_loop(0, i + 1, grads, dq)
        dq_raw, dgq = _pair_norm_bwd(dq * scale, q_raw, rq, gq_ref[...], first)
        dq_ref[...] = dq_raw.astype(dq_ref.dtype)
        dg_ref[0, 0:1, :] += dgq

        @pl.when(i == nq - 1)
        def _():
            k_raw = kraw_ref[...]
            rk = _pair_norm(k_raw, gk_ref[...], first)[1]
            dk_raw, dgk = _pair_norm_bwd(dk_acc[...], k_raw, rk, gk_ref[...], first)
            dk_ref[...] = dk_raw.astype(dk_ref.dtype)
            dv_ref[...] = dv_acc[...].astype(dv_ref.dtype)
            dg_ref[0, 1:2, :] = dgk

    gain = pl.BlockSpec((1, LANES), lambda hp, i: (0, 0))
    band = SDS((S, ATT_HEADS * ATT_D), BF16)
    return pl.pallas_call(
        body, name="fox_bwd", grid=(nhp, nq),
        in_specs=[pl.BlockSpec((T, LANES), lambda hp, i: (i, qb0 + hp)), pl.BlockSpec((S, LANES), lambda hp, i: (0, kb0 + hp)),
                  pl.BlockSpec((S, LANES), lambda hp, i: (0, vb0 + hp)), gain, gain,
                  pl.BlockSpec((T, 16), lambda hp, i: (i, 0)), pl.BlockSpec((nq, 16, T), lambda hp, i: (0, 0, 0)),
                  pl.BlockSpec((T, LANES), lambda hp, i: (i, hp)), pl.BlockSpec((T, LANES), lambda hp, i: (i, db0 + hp))],
        out_specs=[pl.BlockSpec((T, LANES), lambda hp, i: (i, hp)), pl.BlockSpec((S, LANES), lambda hp, i: (0, hp)),
                   pl.BlockSpec((S, LANES), lambda hp, i: (0, hp)), pl.BlockSpec((1, nq, 8, T), lambda hp, i: (hp, 0, 0, 0)),
                   pl.BlockSpec((1, 8, LANES), lambda hp, i: (hp, 0, 0))],
        out_shape=[band, band, band, SDS((nhp, nq, 8, T), F32), SDS((nhp, 8, LANES), F32)],
        scratch_shapes=[pltpu.VMEM((S, LANES), BF16), pltpu.VMEM((S, LANES), F32), pltpu.VMEM((S, LANES), F32),
                        pltpu.VMEM((nq, T, T), F32), pltpu.VMEM((nq, T, T), F32)],
        compiler_params=_cparams(dimension_semantics=("arbitrary", "arbitrary")),
    )(src, src, src, gq2, gk2, cum_col, cum_row3, lse, dsrc)


def _fold_gains(dg):
    def body(d_ref, o_ref):
        t = d_ref[0]
        for h in range(1, dg.shape[0]):
            t = t + d_ref[h]
        o_ref[...] = t + pltpu.roll(t, ATT_D, axis=1)

    return pl.pallas_call(body, name="fold_gains", out_shape=SDS(dg.shape[1:], F32), compiler_params=_cparams())(dg)


def _loss_head(y, target, *, tm):
    M, W = y.shape

    def body(y_ref, t_ref, dy_ref, loss_ref):
        @pl.when(pl.program_id(0) == 0)
        def _():
            loss_ref[...] = jnp.zeros_like(loss_ref)

        e = y_ref[...] - t_ref[...]
        dy_ref[...] = e * (1.0 / W)
        loss_ref[...] += jnp.sum(jnp.sum(e * e, axis=1, keepdims=True), axis=0, keepdims=True) * (0.5 / W)

    return pl.pallas_call(
        body, name="loss_head", grid=(M // tm,),
        in_specs=[pl.BlockSpec((tm, W), lambda i: (i, 0))] * 2,
        out_specs=[pl.BlockSpec((tm, W), lambda i: (i, 0)), pl.BlockSpec((1, 1), lambda i: (0, 0))],
        out_shape=[SDS((M, W), F32), SDS((1, 1), F32)],
        compiler_params=_cparams(dimension_semantics=("arbitrary",)),
    )(y, target)


def _adamw_math(w, g, m, v):
    m = ADAM_B1 * m + (1.0 - ADAM_B1) * g
    v = ADAM_B2 * v + (1.0 - ADAM_B2) * jnp.square(g)
    m_hat = m / (1.0 - ADAM_B1 ** ADAM_STEP)
    v_hat = v / (1.0 - ADAM_B2 ** ADAM_STEP)
    delta = -ADAM_LR * (m_hat / (jnp.sqrt(v_hat) + ADAM_EPS) + ADAM_WD * w)
    return delta, m, v


def _reduce_adamw(parts, w, m, v, *, tr, name, tc=None):
    R, C = w.shape
    tr, tc = min(tr, R), tc or C
    nparts = parts.shape[0]

    def body(p_ref, w_ref, m_ref, v_ref, g_ref, d_ref, nm_ref, nv_ref):
        g = p_ref[0].astype(F32)
        for s in range(1, nparts):
            g = g + p_ref[s].astype(F32)
        g_ref[...] = g
        d_ref[...], nm_ref[...], nv_ref[...] = _adamw_math(w_ref[...], g, m_ref[...], v_ref[...])

    blk = pl.BlockSpec((tr, tc), lambda i, j: (i, j))
    return pl.pallas_call(
        body, name=name, grid=(R // tr, C // tc),
        in_specs=[pl.BlockSpec((nparts, tr, tc), lambda i, j: (0, i, j)), blk, blk, blk], out_specs=[blk] * 4,
        out_shape=[SDS((R, C), F32)] * 4, compiler_params=_cparams(dimension_semantics=("parallel", "parallel")),
    )(parts, w, m, v)


def _adamw(w, g, m, v, *, name):
    def body(w_ref, g_ref, m_ref, v_ref, d_ref, nm_ref, nv_ref):
        d_ref[...], nm_ref[...], nv_ref[...] = _adamw_math(w_ref[...], g_ref[...], m_ref[...], v_ref[...])

    return pl.pallas_call(body, name=name, out_shape=[SDS(w.shape, F32)] * 3, compiler_params=_cparams())(w, g, m, v)


def _peers():
    x, y, c = lax.axis_index("x"), lax.axis_index("y"), lax.axis_index("c")
    out = []
    for k in range(1, N_DEV):
        px, py, pc = x ^ ((k >> 2) & 1), y ^ ((k >> 1) & 1), c ^ (k & 1)
        out.append(((px, py, pc), 4 * px + 2 * py + pc))
    return 4 * x + 2 * y + c, out


_HBM = pl.BlockSpec(memory_space=pltpu.HBM)
_SEM = pl.BlockSpec(memory_space=pltpu.SEMAPHORE)
_DATAFLOW = pltpu.SideEffectType.DATAFLOW_SIDE_EFFECTING


NEAR = (1, 2, 4, 6)


def _plan_peers(scatter, ks=tuple(range(1, N_DEV))):
    return lambda me, peers: [(peers[k - 1][0], peers[k - 1][1] if scatter else None, me, k - 1) for k in ks]


def _plan_relay(me, peers):
    return [(peers[0][0], peers[k - 1][1], peers[k - 1][1], j) for j, k in enumerate((2, 4, 6))]


def _plan_pair(me, peers):
    return [(peers[0][0], peers[k - 1][1], j, j) for j, k in enumerate((1, 3, 5, 7))]


def _plan_chips(me, peers):
    return [(peers[k - 1][0], k // 2, k // 2, k // 2) for k in (2, 4, 6)]


def _copy(src, dst, c, send_sems, recv_sems):
    dev, s_slot, d_slot, i = c
    return pltpu.make_async_remote_copy(
        src_ref=src if s_slot is None else src.at[s_slot], dst_ref=dst.at[d_slot], send_sem=send_sems.at[i],
        recv_sem=recv_sems.at[i], device_id=dev, device_id_type=MESH)


def _copies_start(items, *, name):
    n = len(items)
    bufs = [it[0] for it in items] + [it[1] for it in items if it[1] is not None]
    nb = len(bufs)

    def body(*refs):
        srcs, extra, sems, token = refs[:n], iter(refs[n:nb]), refs[nb:nb + 2 * n], refs[-1]
        me, peers = _peers()
        for a, (_, land, plan) in enumerate(items):
            dst = srcs[a] if land is None else next(extra)
            for c in plan(me, peers):
                _copy(srcs[a], dst, c, sems[2 * a], sems[2 * a + 1]).start()
        token[...] = jnp.zeros_like(token)

    res = pl.pallas_call(
        body, name=name,
        out_shape=([pltpu.SemaphoreType.DMA((N_DEV - 1,))] * (2 * n) + [pltpu.HBM(b.shape, b.dtype) for b in bufs]
                   + [SDS((8, LANES), F32)]),
        in_specs=[_HBM] * nb, out_specs=[_SEM] * (2 * n) + [_HBM] * nb + [pl.BlockSpec(memory_space=pltpu.VMEM)],
        input_output_aliases={i: 2 * n + i for i in range(nb)},
        compiler_params=pltpu.CompilerParams(has_side_effects=_DATAFLOW),
    )(*[pltpu.with_memory_space_constraint(b, pltpu.HBM) for b in bufs])
    sems, thru, token = res[:2 * n], list(res[2 * n:2 * n + nb]), res[-1]
    extra = iter(thru[n:])
    return [(thru[a], None if it[1] is None else next(extra), sems[2 * a], sems[2 * a + 1], it[2])
            for a, it in enumerate(items)], token


def _copies_wait(handles, after, *, name):
    n = len(handles)
    bufs = [h[0] for h in handles] + [h[1] for h in handles if h[1] is not None]
    nb = len(bufs)

    def body(*refs):
        srcs, extra, sems = refs[:n], iter(refs[n:nb]), refs[nb:nb + 2 * n]
        me, peers = _peers()
        for a, h in enumerate(handles):
            dst = srcs[a] if h[1] is None else next(extra)
            for c in h[4](me, peers):
                cp = _copy(srcs[a], dst, c, sems[2 * a], sems[2 * a + 1])
                cp.wait_send()
                cp.wait_recv()

    flat_sems = [s for h in handles for s in (h[2], h[3])]
    res = pl.pallas_call(
        body, name=name, out_shape=[pltpu.HBM(b.shape, b.dtype) for b in bufs],
        in_specs=[_HBM] * nb + [_SEM] * (2 * n) + [pl.BlockSpec(memory_space=pl.ANY)], out_specs=[_HBM] * nb,
        input_output_aliases={i: i for i in range(nb)},
        compiler_params=pltpu.CompilerParams(has_side_effects=_DATAFLOW),
    )(*bufs, *flat_sems, after)
    extra = iter(res[n:])
    return [(res[a], res[a] if h[1] is None else next(extra)) for a, h in enumerate(handles)]


def _exchange_start(arrays, *, scatter, name, near=()):
    items = []
    for a, arr in enumerate(arrays):
        land = lax.empty(arr.shape if scatter else (N_DEV,) + arr.shape, arr.dtype)
        items.append((arr, land, _plan_peers(scatter, NEAR) if a in near else _plan_peers(scatter)))
    return _copies_start(items, name=name)


MOVE_ROWS, MOVE_SLOTS = 512, 3


def _move_rows(src, moves, rows, *, name):
    C = src.shape[1]
    covered = max(dst + n for _, n, dst in moves)
    tail = rows - covered
    assert sum(n for _, n, _ in moves) == covered
    chunks = [(lo + o, min(MOVE_ROWS, n - o), dst + o) for lo, n, dst in moves for o in range(0, n, MOVE_ROWS)]
    nch = len(chunks)

    def body(src_ref, o_ref, buf, sin, sout, *zero):
        def fetch(i):
            lo, n, _ = chunks[i]
            return pltpu.make_async_copy(src_ref.at[pl.ds(lo, n)], buf.at[i % MOVE_SLOTS, pl.ds(0, n)], sin.at[i % MOVE_SLOTS])

        def store(i):
            _, n, dst = chunks[i]
            return pltpu.make_async_copy(buf.at[i % MOVE_SLOTS, pl.ds(0, n)], o_ref.at[pl.ds(dst, n)], sout.at[i % MOVE_SLOTS])

        if tail:
            zero[0][...] = jnp.zeros_like(zero[0])
            fill = pltpu.make_async_copy(zero[0], o_ref.at[pl.ds(covered, tail)], zero[1])
            fill.start()
        for i in range(nch):
            if i >= MOVE_SLOTS:
                store(i - MOVE_SLOTS).wait()
            fetch(i).start()
            if i >= 1:
                fetch(i - 1).wait()
                store(i - 1).start()
        fetch(nch - 1).wait()
        store(nch - 1).start()
        for i in range(max(0, nch - MOVE_SLOTS), nch):
            store(i).wait()
        if tail:
            fill.wait()

    anyspec = pl.BlockSpec(memory_space=pl.ANY)
    dma = pltpu.SemaphoreType.DMA
    return pl.pallas_call(
        body, name=name, in_specs=[anyspec], out_specs=anyspec, out_shape=SDS((rows, C), src.dtype),
        scratch_shapes=([pltpu.VMEM((MOVE_SLOTS, MOVE_ROWS, C), src.dtype), dma((MOVE_SLOTS,)), dma((MOVE_SLOTS,))]
                        + ([pltpu.VMEM((tail, C), src.dtype), dma] if tail else [])),
        compiler_params=_cparams())(src)


def _pair_sum(a, b, *, name):
    n, R, C = a.shape
    tc = 256

    def body(a_ref, b_ref, o_ref):
        o_ref[...] = (a_ref[...].astype(F32) + b_ref[...].astype(F32)).astype(o_ref.dtype)

    blk = pl.BlockSpec((1, R, tc), lambda i, j: (i, 0, j))
    return pl.pallas_call(body, name=name, grid=(n, C // tc), in_specs=[blk, blk], out_specs=blk,
                          out_shape=SDS(a.shape, a.dtype), compiler_params=_cparams(dimension_semantics=("parallel", "parallel")))(a, b)


def _own_slot(landed, own, me):
    return lax.dynamic_update_slice(landed, own[None], (me,) + (0,) * own.ndim)


SMALL = (("g_mix", 1024), ("conv_w", 6144), ("conv_b", 1536), ("dt_bias", 16), ("a_log", 16), ("d_skip", 16),
         ("ssm_norm_w", 1024), ("g_q", 64), ("g_k", 64), ("f_bias", 16), ("g_xattn", 1024), ("g_mem", 1024),
         ("xg_q", 256), ("xg_k", 256), ("g_mlp", 1024))
SLAB_ROWS = 112
BIG = ("w_in", "w_out", "xq_w", "xkv_w", "xo_w", "w_up", "w_down")
WEIGHTS = ("g_mix", "w_in", "conv_w", "conv_b", "dt_bias", "a_log", "d_skip", "ssm_norm_w", "g_q", "g_k", "f_bias", "w_out",
           "g_xattn", "g_mem", "xq_w", "xkv_w", "xg_q", "xg_k", "xo_w", "g_mlp", "w_up", "w_down")
O_Z, O_XS, O_B, O_C, O_DT, O_Q, O_K, O_V, O_F, O_END = 0, 1024, 2048, 2304, 2560, 2576, 3600, 4624, 5648, 5664
IN_ROW_MOVES = ((O_Z, O_B - O_Z, C_Z), (O_Q, O_F - O_Q, C_Q), (O_B, O_Q - O_B, C_B), (O_F, O_END - O_F, C_DTF + 16))


def _pack_small(vals):
    rows = []
    for name, size in SMALL:
        flat = vals[name].reshape(-1).astype(F32)
        pad = -size % LANES
        rows.append(jnp.pad(flat, (0, pad)).reshape(-1, LANES))
    slab = jnp.concatenate(rows, axis=0)
    return jnp.pad(slab, ((0, SLAB_ROWS - slab.shape[0]), (0, 0)))


def _unpack_small(slab):
    out, r = {}, 0
    for name, size in SMALL:
        nr = -(-size // LANES)
        out[name] = slab[r:r + nr].reshape(-1)[:size]
        r += nr
    return out


def _step(p, m, v, x, mem, target):
    S = x.shape[0]
    TM = 256
    me = 4 * lax.axis_index("x") + 2 * lax.axis_index("y") + lax.axis_index("c")

    def rms(u, g, name):
        return _rw_fwd(_rms_fn, [_whole(u)], [_whole(g)], [(D_MODEL, BF16)], tm=TM, name=name)[0]

    def pin(param, token):
        return param + token[0:1, 0:1]

    def landed_with_own(pairs, scatter):
        out = []
        for src, land in pairs:
            own = lax.dynamic_index_in_dim(src, me, 0, keepdims=False) if scatter else src
            out.append(_own_slot(land, own, me))
        return out

    w_in_own, m_in_own, v_in_own = p["w_in"].T, m["w_in"].T, v["w_in"].T
    ag, ag_token = _exchange_start([w_in_own.astype(BF16), p["conv_w"]] + [p[n].astype(BF16) for n in BIG[1:]],
                                   scatter=False, name="allgather_start", near=(0,))
    h1 = rms(x, pin(p["g_mix"], ag_token), "rms_mix")
    (win_src, win_land), convw_pair = _copies_wait(ag[:2], h1, name="allgather_wait_in")
    relay, token = _copies_start([(win_land, None, _plan_relay)], name="allgather_relay_start")
    win_land = _copies_wait(relay, token, name="allgather_relay_wait")[0][1]
    win_g, convw_g = landed_with_own([(win_src, win_land), convw_pair], False)
    w_in_o = win_g.reshape(O_END, D_MODEL)
    w_in_t = _move_rows(w_in_o, IN_ROW_MOVES, P_COLS, name="w_in_rows")
    conv_w = convw_g.transpose(1, 0, 2).reshape(4, 1536)
    cw_xs, cw_bc = conv_w[:, :1024], conv_w[:, 1024:]
    cb_xs, cb_bc = p["conv_b"][:, :1024], p["conv_b"][:, 1024:]
    dt_bias, a_log, f_bias = p["dt_bias"].reshape(16, 1), p["a_log"].reshape(16, 1), p["f_bias"].reshape(16, 1)

    proj = _matmul(h1, w_in_t, mode="nt", tm=1024, tn=640, tk=1024, name="mm_in")
    xs_c = _conv_fwd(proj, C_XS, 1024, cw_xs, cb_xs, name="conv_xs")
    bc_c = _conv_fwd(proj, C_B, 512, cw_bc, cb_bc, name="conv_bc")
    dtf_t = proj[:, C_DTF:C_DTF + 32].T
    dt_t, acs_t, cum_t = _dtf_fwd(dtf_t, dt_bias, a_log, f_bias)
    dt_col, acs_col, cum_col = dt_t.T, acs_t.T, cum_t.T
    cum_row3 = cum_t.reshape(16, S // ATT_T, ATT_T).transpose(1, 0, 2)
    y_ssd, hs = _ssd_fwd(xs_c, dt_col, acs_col, acs_t, bc_c)
    gate_rows = [_whole(y_ssd), _whole(xs_c), (proj, C_Z, 1024)]
    gate_pars = [_whole(p["d_skip"]), _whole(p["ssm_norm_w"])]
    y_ssm = _rw_fwd(_gate_fn, gate_rows, gate_pars, [(1024, BF16)], tm=TM, name="gate")[0]
    gq2, gk2 = jnp.tile(p["g_q"], (1, 2)), jnp.tile(p["g_k"], (1, 2))
    o, lse = _fox_fwd(proj, C_Q, C_K, C_V, gq2, gk2, cum_col, cum_row3)
    mixed = jnp.concatenate([y_ssm, o], axis=1)
    wout_g, = landed_with_own(_copies_wait(ag[2:3], mixed, name="allgather_wait_out"), False)
    w_out = wout_g.reshape(2 * D_MODEL, D_MODEL)
    x1 = _matmul(mixed, w_out, mode="nn", tm=1024, tn=512, tk=2048, add=x, name="mm_out")
    xq_g, xkv_w, xo_g, w_up, wdown_g = landed_with_own(_copies_wait(ag[3:], x1, name="allgather_wait_rest"), False)
    xq_w = xq_g.reshape(D_MODEL, D_MODEL)
    xo_w = xo_g.reshape(D_MODEL, D_MODEL)
    w_down = wdown_g.reshape(4 * D_MODEL, D_MODEL)

    h2 = rms(x1, p["g_xattn"], "rms_xattn")
    mem_n = rms(mem, p["g_mem"], "rms_mem")
    q2 = _matmul(h2, xq_w, mode="nn", tm=1024, tn=512, tk=1024, name="mm_xq")
    kv = _matmul(mem_n, xkv_w, mode="nn", b_shards=True, tm=256, tn=256, tk=1024, name="mm_xkv")
    xa_rows = [(q2, X_D * h, X_D) for h in range(X_HEADS)]
    xa_pars = ([(kv, X_D * h, X_D) for h in range(X_HEADS)] + [(kv, D_MODEL + X_D * h, X_D) for h in range(X_HEADS)]
               + [_whole(p["xg_q"]), _whole(p["xg_k"])])
    o2 = _rw_fwd(_xattn_fn, xa_rows, xa_pars, [(D_MODEL, BF16)], tm=TM, name="xattn")[0]
    x2 = _matmul(o2, xo_w, mode="nn", tm=1024, tn=512, tk=1024, add=x1, name="mm_xo")

    h3 = rms(x2, p["g_mlp"], "rms_mlp")
    a, usq = _matmul(h3, w_up, mode="nn", b_shards=True, tm=2048, tn=512, tk=1024, name="mm_up", out_dtypes=[F32, BF16],
                     epilogue=lambda acc: (acc, jnp.square(jax.nn.relu(acc))))
    x3 = _matmul(usq, w_down, mode="nn", tm=1024, tn=512, tk=2048, add=x2, name="mm_down")
    dy, loss_part = _loss_head(x3, target, tm=TM)
    loss = lax.psum(loss_part[0, 0], ("x", "y", "c"))

    def row_shards(a):
        r, c = a.shape
        return a.reshape(N_DEV, r // N_DEV, c)

    g = {}
    g["w_down"] = _matmul(usq, dy, mode="tn", out_dtype=GRAD_WIRE, tm=1024, tn=1024, tk=1024, name="mm_d_wdown")
    da = _matmul(dy, w_down, mode="nt", tm=1024, tn=1024, tk=1024, name="mm_d_usq", out_dtype=BF16, extras=(a,),
                 epilogue=lambda acc, av: (2.0 * jax.nn.relu(av) * acc,))
    g["w_up"] = _matmul(h3, da, mode="tn", out_shards=True, out_dtype=GRAD_WIRE, tm=1024, tn=512, tk=1024, name="mm_d_wup")
    sent_mlp, token = _exchange_start([row_shards(g["w_down"]), g["w_up"]], scatter=True,
                                      name="grads_start_mlp")
    dh3 = _matmul(da, w_up, mode="nt", b_shards=True, tm=2048, tn=1024, tk=512, name="mm_d_h3")
    dx2, g["g_mlp"] = _rw_bwd(_rms_fn, [_whole(x2)], [_whole(pin(p["g_mlp"], token))], [_whole(dh3)], tm=TM,
                              name="rms_mlp_bwd", row_grads=[F32], adds={0: _whole(dy)})

    g["xo_w"] = _matmul(o2, dx2, mode="tn", out_dtype=GRAD_WIRE, tm=1024, tn=1024, tk=1024, name="mm_d_wxo")
    do2 = _matmul(dx2, xo_w, mode="nt", tm=1024, tn=512, tk=1024, name="mm_d_o2")
    xa = _rw_bwd(_xattn_fn, xa_rows, xa_pars, [_whole(do2)], tm=TM, name="xattn_bwd", row_grads=[BF16] * X_HEADS)
    dq2 = jnp.concatenate(xa[:X_HEADS], axis=1)
    dkv = jnp.concatenate(xa[X_HEADS:3 * X_HEADS], axis=1)
    g["xg_q"], g["xg_k"] = xa[3 * X_HEADS], xa[3 * X_HEADS + 1]
    g["xq_w"] = _matmul(h2, dq2, mode="tn", out_dtype=GRAD_WIRE, tm=1024, tn=1024, tk=1024, name="mm_d_wxq")
    dh2 = _matmul(dq2, xq_w, mode="nt", tm=1024, tn=512, tk=1024, name="mm_d_h2")
    g["xkv_w"] = _matmul(mem_n, dkv, mode="tn", out_shards=True, out_dtype=GRAD_WIRE, tm=1024, tn=256, tk=256,
                         name="mm_d_wxkv")
    dmem_n = _matmul(dkv, xkv_w, mode="nt", b_shards=True, tm=256, tn=1024, tk=256, name="mm_d_memn")
    g["g_mem"] = _rw_bwd(_rms_fn, [_whole(mem)], [_whole(p["g_mem"])], [_whole(dmem_n)], tm=TM, name="rms_mem_bwd",
                         row_grads=[None])[0]
    dx1, g["g_xattn"] = _rw_bwd(_rms_fn, [_whole(x1)], [_whole(p["g_xattn"])], [_whole(dh2)], tm=TM, name="rms_xattn_bwd",
                                row_grads=[F32], adds={0: _whole(dx2)})

    g["w_out"] = _matmul(mixed, dx1, mode="tn", out_dtype=GRAD_WIRE, tm=1024, tn=1024, tk=1024, name="mm_d_wout")
    sent_mid, token = _exchange_start(
        [row_shards(g["w_out"]), row_shards(g["xq_w"]), g["xkv_w"], row_shards(g["xo_w"])], scatter=True,
        name="grads_start_mid")
    dmixed = _matmul(dx1, w_out, mode="nt", tm=1024, tn=1024, tk=1024, name="mm_d_mixed")
    dq, dk, dv, dcum4, dgain = _fox_bwd(proj, C_Q, C_K, C_V, pin(gq2, token), gk2, cum_col, cum_row3, lse, dmixed, 1024)
    gains = _fold_gains(dgain)
    g["g_q"], g["g_k"] = gains[0:1, :ATT_D], gains[1:2, :ATT_D]
    dy_ssd, dxs_g, dz, g["d_skip"], g["ssm_norm_w"] = _rw_bwd(
        _gate_fn, gate_rows, gate_pars, [(dmixed, 0, 1024)], tm=TM, name="gate_bwd", row_grads=[F32, F32, BF16])
    dxs_s, ddt_col, dacs_col, dacs_row, d_b, d_c = _ssd_bwd(xs_c, dt_col, acs_col, acs_t, bc_c, hs, dy_ssd)
    dcum_t = dcum4[:, :, 0:2, :].transpose(0, 2, 1, 3).reshape(16, S)
    ddtf_t, ddtb, dalog, dfb = _dtf_bwd(dtf_t, dt_bias, a_log, f_bias, ddt_col.T, dacs_col.T, dacs_row, dcum_t)
    g["dt_bias"], g["a_log"], g["f_bias"] = ddtb, dalog, dfb
    dxs_raw, dcw_xs, dcb_xs = _conv_bwd(proj, C_XS, 1024, cw_xs, cb_xs, [dxs_s, dxs_g], name="conv_xs_bwd")
    dbc_raw, dcw_bc, dcb_bc = _conv_bwd(proj, C_B, 512, cw_bc, cb_bc, [jnp.concatenate([d_b, d_c], axis=1)],
                                        name="conv_bc_bwd")
    g["conv_w"] = jnp.concatenate([dcw_xs, dcw_bc], axis=1)
    g["conv_b"] = jnp.concatenate([dcb_xs, dcb_bc], axis=1)
    ddtf = jnp.pad(ddtf_t.T.astype(BF16), ((0, 0), (0, P_COLS - C_DTF - 32)))
    dproj = jnp.concatenate([dz, dxs_raw, dq, dk, dv, dbc_raw, ddtf], axis=1)
    dw_in_p = _matmul(dproj, h1, mode="tn", out_dtype=GRAD_WIRE, tm=640, tn=1024, tk=1024, name="mm_d_win")
    g["w_in"] = _move_rows(dw_in_p, [(dst, n, lo) for lo, n, dst in IN_ROW_MOVES], O_END, name="d_w_in_rows")
    half = N_DEV // 2
    send_in = row_shards(g["w_in"])
    pair, _ = _copies_start([(send_in, lax.empty((half,) + send_in.shape[1:], send_in.dtype), _plan_pair)],
                            name="grads_in_pair_start")
    dh1 = _matmul(dproj, w_in_t, mode="nn", tm=1024, tn=512, tk=1920, name="mm_d_h1")
    send_in, from_sibling = _copies_wait(pair, dh1, name="grads_in_pair_wait")[0]
    mine = jnp.stack([lax.dynamic_index_in_dim(send_in, me ^ (2 * j), 0, keepdims=False) for j in range(half)])
    chip_sums = _pair_sum(mine, from_sibling, name="grads_in_pair_sum")
    sent_in, token = _copies_start([(chip_sums, lax.empty(chip_sums.shape, chip_sums.dtype), _plan_chips)],
                                   name="grads_in_chip_start")
    grad_x, g["g_mix"] = _rw_bwd(_rms_fn, [_whole(x)], [_whole(pin(p["g_mix"], token))], [_whole(dh1)], tm=TM,
                                 name="rms_mix_bwd", row_grads=[F32], adds={0: _whole(dx1)})
    sent_small, _ = _exchange_start([_pack_small(g)], scatter=False, name="small_grads_start")

    grads, delta, new_m, new_v = {}, {}, {}, {}

    def update(names, sent, after, wait_name):
        parts = landed_with_own(_copies_wait(sent, after, name=wait_name), True)
        for name, part in zip(names, parts, strict=True):
            grads[name], delta[name], new_m[name], new_v[name] = _reduce_adamw(part, p[name], m[name], v[name], tr=128,
                                                                                name="adamw_" + name)

    update(("w_down", "w_up"), sent_mlp, grad_x, "grads_wait_mlp")
    update(("w_out", "xq_w", "xkv_w", "xo_w"), sent_mid, delta["w_up"], "grads_wait_mid")
    chip_sums, landed = _copies_wait(sent_in, delta["xo_w"], name="grads_in_chip_wait")[0]
    part = lax.dynamic_update_slice(landed, chip_sums[0:1], (0, 0, 0))
    res = _reduce_adamw(part, w_in_own, m_in_own, v_in_own, tr=part.shape[1], tc=256, name="adamw_w_in")
    grads["w_in"], delta["w_in"], new_m["w_in"], new_v["w_in"] = [r.T for r in res]
    small_parts = landed_with_own(_copies_wait(sent_small, delta["w_in"], name="small_grads_wait"), False)[0]
    zeros_cw = jnp.zeros((4, 1536), F32)
    slabs = [_pack_small({**d, "conv_w": zeros_cw}) for d in (p, m, v)]
    sg, sd, sm, sv = _reduce_adamw(small_parts, *slabs, tr=SLAB_ROWS, name="adamw_small")
    for dst, slab in ((grads, sg), (delta, sd), (new_m, sm), (new_v, sv)):
        for name, flat in _unpack_small(slab).items():
            if name != "conv_w":
                dst[name] = flat.reshape(p[name].shape)
    cw_shard = p["conv_w"].shape[1]
    grads["conv_w"] = lax.dynamic_slice(_unpack_small(sg)["conv_w"].reshape(4, 1536), (0, me * cw_shard), (4, cw_shard))
    delta["conv_w"], new_m["conv_w"], new_v["conv_w"] = _adamw(p["conv_w"], grads["conv_w"], m["conv_w"], v["conv_w"],
                                                               name="adamw_conv_w")
    return loss, grad_x, grads, delta, new_m, new_v


def kernel(x, mem, g_mix, w_in, conv_w, conv_b, dt_bias, a_log, d_skip, ssm_norm_w, g_q, g_k, f_bias, w_out, g_xattn, g_mem, xq_w, xkv_w, xg_q, xg_k, xo_w, g_mlp, w_up, w_down, loss_target, m_g_mix, m_w_in, m_conv_w, m_conv_b, m_dt_bias, m_a_log, m_d_skip, m_ssm_norm_w, m_g_q, m_g_k, m_f_bias, m_w_out, m_g_xattn, m_g_mem, m_xq_w, m_xkv_w, m_xg_q, m_xg_k, m_xo_w, m_g_mlp, m_w_up, m_w_down, v_g_mix, v_w_in, v_conv_w, v_conv_b, v_dt_bias, v_a_log, v_d_skip, v_ssm_norm_w, v_g_q, v_g_k, v_f_bias, v_w_out, v_g_xattn, v_g_mem, v_xq_w, v_xkv_w, v_xg_q, v_xg_k, v_xo_w, v_g_mlp, v_w_up, v_w_down):
    args = locals()
    drop = lambda t: t[0] if t.ndim == 3 else t
    p = {n: drop(args[n]) for n in WEIGHTS}
    m = {n: drop(args["m_" + n]) for n in WEIGHTS}
    v = {n: drop(args["v_" + n]) for n in WEIGHTS}
    loss, grad_x, grads, delta, new_m, new_v = _step(p, m, v, x[0], mem[0], loss_target[0])
    outs = [loss, grad_x[None]]
    for d in (grads, delta, new_m, new_v):
        outs += [d[n].reshape(args[n].shape) for n in WEIGHTS]
    return tuple(outs)
```

```python
import functools
import math

import jax
import jax.numpy as jnp
from jax import lax
from jax.experimental import pallas as pl
from jax.experimental.pallas import tpu as pltpu

F32, BF16 = jnp.float32, jnp.bfloat16
SDS = jax.ShapeDtypeStruct
HI = lax.Precision.HIGHEST
MESH = pl.DeviceIdType.MESH

N_DEV = 8
EPS = 1e-5
D_MODEL = 1024
SSM_HEADS, SSM_P, SSM_N, SSM_GROUPS, CHUNK = 16, 64, 128, 2, 128
ATT_HEADS, ATT_D = 16, 64
X_HEADS, X_D = 4, 256
LANES = 128
VMEM_LIMIT = 48 * 1024 * 1024
NEG = -1e30

GRAD_WIRE = BF16
ADAM_LR, ADAM_B1, ADAM_B2, ADAM_EPS, ADAM_WD, ADAM_STEP = 0.001, 0.9, 0.999, 1e-08, 0.01, 10

C_Z, C_XS, C_Q, C_K, C_V, C_B, C_C, C_DTF, P_COLS = 0, 1024, 2048, 3072, 4096, 5120, 5376, 5632, 5760

_NN = (((1,), (0,)), ((), ()))
_NT = (((1,), (1,)), ((), ()))
_TN = (((0,), (0,)), ((), ()))


def _cparams(**kw):
    return pltpu.CompilerParams(vmem_limit_bytes=VMEM_LIMIT, **kw)


def _bdot(a, b, dn):
    return lax.dot_general(a.astype(BF16), b.astype(BF16), dn, preferred_element_type=F32)


@jax.custom_vjp
def mm_nn(a, b):
    return _bdot(a, b, _NN)


mm_nn.defvjp(lambda a, b: (mm_nn(a, b), (a, b)), lambda r, g: (_bdot(g, r[1], _NT), _bdot(r[0], g, _TN)))


@jax.custom_vjp
def mm_nt(a, b):
    return _bdot(a, b, _NT)


mm_nt.defvjp(lambda a, b: (mm_nt(a, b), (a, b)), lambda r, g: (_bdot(g, r[1], _NN), _bdot(g, r[0], _TN)))


@jax.custom_vjp
def mm_tn(a, b):
    return _bdot(a, b, _TN)


mm_tn.defvjp(lambda a, b: (mm_tn(a, b), (a, b)), lambda r, g: (_bdot(r[1], g, _NT), _bdot(r[0], g, _NN)))


def _cdot(x, c):
    return jnp.dot(x, c, precision=HI, preferred_element_type=F32)


def _iota(shape, dim):
    return lax.broadcasted_iota(jnp.int32, shape, dim)


def _matmul(a, b, *, mode, tm, tn, tk, name, out_dtype=F32, add=None, extras=(), epilogue=None, out_dtypes=None,
            b_shards=False, out_shards=False):
    if mode == "tn":
        K, M = a.shape
    else:
        M, K = a.shape
    if b_shards:
        N = b.shape[1] if mode == "nt" else b.shape[0] * b.shape[2]
        tn, tk = (tn, b.shape[2]) if mode == "nt" else (b.shape[2], tk)
    else:
        N = b.shape[0] if mode == "nt" else b.shape[1]
    tm, tn, tk = min(tm, M), min(tn, N), min(tk, K)
    assert M % tm == 0 and N % tn == 0 and K % tk == 0, (name, M, N, K, tm, tn, tk)
    assert not b_shards or (K // tk if mode == "nt" else N // tn) == b.shape[0], name
    assert not (out_shards and (extras or add is not None)), name
    nk = K // tk
    dn = {"nn": _NN, "nt": _NT, "tn": _TN}[mode]
    if add is not None:
        extras, epilogue = (add,), lambda acc, r: (acc + r,)
    elif epilogue is None:
        epilogue = lambda acc: (acc,)
    out_dtypes = out_dtypes or [out_dtype]
    ne, no = len(extras), len(out_dtypes)

    def body(*refs):
        a_ref, b_ref = refs[:2]
        e_refs, o_refs = refs[2:2 + ne], refs[2 + ne:2 + ne + no]

        def finish(acc):
            res = epilogue(acc, *[e[...] for e in e_refs])
            for o_ref, v in zip(o_refs, res, strict=True):
                o_ref[...] = v.astype(o_ref.dtype)

        prod = _bdot(a_ref[...], b_ref[...], dn)
        if nk == 1:
            finish(prod)
            return
        acc_ref = refs[-1]
        k = pl.program_id(2)

        @pl.when(k == 0)
        def _():
            acc_ref[...] = prod

        @pl.when(jnp.logical_and(k > 0, k < nk - 1))
        def _():
            acc_ref[...] += prod

        @pl.when(k == nk - 1)
        def _():
            finish(acc_ref[...] + prod)

    a_spec = pl.BlockSpec((tk, tm), lambda i, j, k: (k, i)) if mode == "tn" else pl.BlockSpec((tm, tk), lambda i, j, k: (i, k))
    if b_shards and mode == "nt":
        b_spec = pl.BlockSpec((None, tn, tk), lambda i, j, k: (k, j, 0))
    elif b_shards:
        b_spec = pl.BlockSpec((None, tk, tn), lambda i, j, k: (j, k, 0))
    elif mode == "nt":
        b_spec = pl.BlockSpec((tn, tk), lambda i, j, k: (j, k))
    else:
        b_spec = pl.BlockSpec((tk, tn), lambda i, j, k: (k, j))
    if out_shards:
        o_spec, o_shape = pl.BlockSpec((None, tm, tn), lambda i, j, k: (j, i, 0)), (N // tn, M, tn)
    else:
        o_spec, o_shape = pl.BlockSpec((tm, tn), lambda i, j, k: (i, j)), (M, N)
    res = pl.pallas_call(
        body, name=name, grid=(M // tm, N // tn, nk), in_specs=[a_spec, b_spec] + [o_spec] * ne, out_specs=[o_spec] * no,
        out_shape=[SDS(o_shape, dt) for dt in out_dtypes], scratch_shapes=[pltpu.VMEM((tm, tn), F32)] if nk > 1 else [],
        compiler_params=_cparams(dimension_semantics=("parallel", "parallel", "arbitrary")),
    )(a, b, *extras)
    return res[0] if no == 1 else res


def _row_spec(tm, spec):
    _, c0, w = spec
    assert c0 % w == 0
    return pl.BlockSpec((tm, w), functools.partial(lambda i, cb: (i, cb), cb=c0 // w))


def _par_spec(spec):
    arr, c0, w = spec
    assert c0 % w == 0
    return pl.BlockSpec((arr.shape[0], w), functools.partial(lambda i, cb: (0, cb), cb=c0 // w))


def _whole(arr):
    return (arr, 0, arr.shape[1])


def _rw_fwd(fn, rows, params, outs, *, tm, name):
    M = rows[0][0].shape[0]
    nr, npar = len(rows), len(params)

    def body(*refs):
        rv = [r[...].astype(F32) for r in refs[:nr]]
        pv = [p[...].astype(F32) for p in refs[nr:nr + npar]]
        res = fn(*rv, *pv)
        for o_ref, v in zip(refs[nr + npar:], res, strict=True):
            o_ref[...] = v.astype(o_ref.dtype)

    return pl.pallas_call(
        body, name=name, grid=(M // tm,),
        in_specs=[_row_spec(tm, r) for r in rows] + [_par_spec(p) for p in params],
        out_specs=[pl.BlockSpec((tm, w), lambda i: (i, 0)) for w, _ in outs],
        out_shape=[SDS((M, w), dt) for w, dt in outs],
        compiler_params=_cparams(dimension_semantics=("parallel",)),
    )(*[r[0] for r in rows], *[p[0] for p in params])


def _rw_bwd(fn, rows, params, cts, *, tm, name, row_grads, adds=None):
    M = rows[0][0].shape[0]
    adds = adds or {}
    nr, npar, nc = len(rows), len(params), len(cts)
    add_keys = sorted(adds)
    want = [k for k in range(nr) if row_grads[k] is not None]

    def body(*refs):
        pos = 0
        r_refs = refs[pos:pos + nr]; pos += nr
        p_refs = refs[pos:pos + npar]; pos += npar
        c_refs = refs[pos:pos + nc]; pos += nc
        a_refs = dict(zip(add_keys, refs[pos:pos + len(add_keys)])); pos += len(add_keys)
        dr_refs = dict(zip(want, refs[pos:pos + len(want)])); pos += len(want)
        dp_refs = refs[pos:pos + npar]
        rv = [r[...].astype(F32) for r in r_refs]
        pv = [p[...].astype(F32) for p in p_refs]
        _, vjp = jax.vjp(fn, *rv, *pv)
        g = vjp(tuple(c[...].astype(F32) for c in c_refs))
        for k in want:
            v = g[k]
            if k in a_refs:
                v = v + a_refs[k][...].astype(F32)
            dr_refs[k][...] = v.astype(dr_refs[k].dtype)
        first = pl.program_id(0) == 0
        for j in range(npar):
            @pl.when(first)
            def _(j=j):
                dp_refs[j][...] = jnp.zeros_like(dp_refs[j])
            dp_refs[j][...] += g[nr + j]

    res = pl.pallas_call(
        body, name=name, grid=(M // tm,),
        in_specs=([_row_spec(tm, r) for r in rows] + [_par_spec(p) for p in params] + [_row_spec(tm, c) for c in cts]
                  + [_row_spec(tm, adds[k]) for k in add_keys]),
        out_specs=([pl.BlockSpec((tm, rows[k][2]), lambda i: (i, 0)) for k in want]
                   + [pl.BlockSpec((p[0].shape[0], p[2]), lambda i: (0, 0)) for p in params]),
        out_shape=([SDS((M, rows[k][2]), row_grads[k]) for k in want] + [SDS((p[0].shape[0], p[2]), F32) for p in params]),
        compiler_params=_cparams(dimension_semantics=("arbitrary",)),
    )(*[r[0] for r in rows], *[p[0] for p in params], *[c[0] for c in cts], *[adds[k][0] for k in add_keys])
    return res


def _rms_fn(x, g):
    r = lax.rsqrt(jnp.mean(x * x, axis=-1, keepdims=True) + EPS)
    return (x * r * g,)


def _seg_mats(width, seg):
    n = width // seg
    p = (_iota((width, n), 0) // seg == _iota((width, n), 1)).astype(F32)
    e = (_iota((n, width), 1) // seg == _iota((n, width), 0)).astype(F32)
    return p, e


def _gate_fn(y, xs, z, dskip, w):
    width = SSM_HEADS * SSM_P
    _, e = _seg_mats(width, SSM_P)
    y = (y + _cdot(dskip, e) * xs) * (z * jax.nn.sigmoid(z))
    g0 = _iota((1, width), 1) < width // SSM_GROUPS
    y2 = y * y
    gw = width // SSM_GROUPS
    ms0 = jnp.sum(jnp.where(g0, y2, 0.0), axis=-1, keepdims=True) * (1.0 / gw)
    ms1 = jnp.sum(jnp.where(g0, 0.0, y2), axis=-1, keepdims=True) * (1.0 / gw)
    r = jnp.where(g0, lax.rsqrt(ms0 + EPS), lax.rsqrt(ms1 + EPS))
    return (y * r * w,)


def _xattn_fn(q0, q1, q2, q3, k0, k1, k2, k3, v0, v1, v2, v3, gq, gk):
    def norm(u, g):
        return u * lax.rsqrt(jnp.mean(u * u, axis=-1, keepdims=True) + EPS) * g
    outs = []
    for q, k, v in ((q0, k0, v0), (q1, k1, v1), (q2, k2, v2), (q3, k3, v3)):
        s = mm_nt(norm(q, gq), norm(k, gk)) * (X_D ** -0.5)
        p = jnp.exp(s - lax.stop_gradient(jnp.max(s, axis=-1, keepdims=True)))
        p = p / jnp.sum(p, axis=-1, keepdims=True)
        outs.append(mm_nn(p, v))
    return (jnp.concatenate(outs, axis=-1),)


CONV_TC = 256


def _shift_down(u, k):
    if k == 0:
        return u
    return jnp.where(_iota(u.shape, 0) >= k, pltpu.roll(u, k, axis=0), 0.0)


def _shift_up(u, k):
    if k == 0:
        return u
    n = u.shape[0]
    return jnp.where(_iota(u.shape, 0) < n - k, pltpu.roll(u, n - k, axis=0), 0.0)


def _conv_pre(u, w_ref, b):
    pre = b + w_ref[3:4, :] * u
    for k in (1, 2, 3):
        pre = pre + w_ref[3 - k:4 - k, :] * _shift_down(u, k)
    return pre


def _conv_fwd(src, c0, width, w, b, *, name):
    S = src.shape[0]
    cb0 = c0 // CONV_TC

    def body(u_ref, w_ref, b_ref, o_ref):
        pre = _conv_pre(u_ref[...], w_ref, b_ref[...])
        o_ref[...] = pre * jax.nn.sigmoid(pre)

    return pl.pallas_call(
        body, name=name, grid=(width // CONV_TC,),
        in_specs=[pl.BlockSpec((S, CONV_TC), lambda j: (0, cb0 + j)), pl.BlockSpec((4, CONV_TC), lambda j: (0, j)),
                  pl.BlockSpec((1, CONV_TC), lambda j: (0, j))],
        out_specs=pl.BlockSpec((S, CONV_TC), lambda j: (0, j)), out_shape=SDS((S, width), F32),
        compiler_params=_cparams(dimension_semantics=("parallel",)),
    )(src, w, b)


def _conv_bwd(src, c0, width, w, b, douts, *, name):
    S = src.shape[0]
    cb0 = c0 // CONV_TC
    nd = len(douts)

    def body(*refs):
        u_ref, w_ref, b_ref = refs[:3]
        d_refs = refs[3:3 + nd]
        du_ref, dw_ref, db_ref = refs[3 + nd:]
        u = u_ref[...]
        pre = _conv_pre(u, w_ref, b_ref[...])
        sg = jax.nn.sigmoid(pre)
        dout = d_refs[0][...]
        for r in d_refs[1:]:
            dout = dout + r[...]
        dpre = dout * (sg * (1.0 + pre * (1.0 - sg)))
        du = w_ref[3:4, :] * dpre
        dw_ref[3:4, :] = jnp.sum(dpre * u, axis=0, keepdims=True)
        for k in (1, 2, 3):
            du = du + w_ref[3 - k:4 - k, :] * _shift_up(dpre, k)
            dw_ref[3 - k:4 - k, :] = jnp.sum(dpre * _shift_down(u, k), axis=0, keepdims=True)
        du_ref[...] = du.astype(du_ref.dtype)
        db_ref[...] = jnp.sum(dpre, axis=0, keepdims=True)

    return pl.pallas_call(
        body, name=name, grid=(width // CONV_TC,),
        in_specs=[pl.BlockSpec((S, CONV_TC), lambda j: (0, cb0 + j)), pl.BlockSpec((4, CONV_TC), lambda j: (0, j)),
                  pl.BlockSpec((1, CONV_TC), lambda j: (0, j))] + [pl.BlockSpec((S, CONV_TC), lambda j: (0, j))] * nd,
        out_specs=[pl.BlockSpec((S, CONV_TC), lambda j: (0, j)), pl.BlockSpec((4, CONV_TC), lambda j: (0, j)),
                   pl.BlockSpec((1, CONV_TC), lambda j: (0, j))],
        out_shape=[SDS((S, width), BF16), SDS((4, width), F32), SDS((1, width), F32)],
        compiler_params=_cparams(dimension_semantics=("parallel",)),
    )(src, w, b, *douts)


def _softplus(x):
    return jnp.maximum(x, 0.0) + jnp.log(1.0 + jnp.exp(-jnp.abs(x)))


def _prefix_sum(x, seg):
    n = x.shape[1]
    pos = _iota(x.shape, 1) % seg
    k = 1
    while k < seg:
        x = x + jnp.where(pos >= k, pltpu.roll(x, k, axis=1), 0.0)
        k *= 2
    return x


def _suffix_sum(x, seg):
    n = x.shape[1]
    pos = _iota(x.shape, 1) % seg
    k = 1
    while k < seg:
        x = x + jnp.where(pos + k < seg, pltpu.roll(x, n - k, axis=1), 0.0)
        k *= 2
    return x


def _dtf_fwd(dtf_t, dt_bias, a_log, f_bias):
    S = dtf_t.shape[1]

    def body(x_ref, db_ref, al_ref, fb_ref, dt_ref, acs_ref, cum_ref):
        dt = _softplus(x_ref[0:16, :] + db_ref[...])
        dt_ref[...] = dt
        acs_ref[...] = _prefix_sum(dt * (-jnp.exp(al_ref[...])), CHUNK)
        cum_ref[...] = _prefix_sum(-_softplus(-(x_ref[16:32, :] + fb_ref[...])), S)

    return pl.pallas_call(body, name="dtf_fwd", out_shape=[SDS((16, S), F32)] * 3, compiler_params=_cparams())(
        dtf_t, dt_bias, a_log, f_bias)


def _dtf_bwd(dtf_t, dt_bias, a_log, f_bias, d_dt, d_acs_a, d_acs_b, d_cum):
    S = dtf_t.shape[1]

    def body(x_ref, db_ref, al_ref, fb_ref, ddt_ref, da1_ref, da2_ref, dc_ref, dx_ref, ddb_ref, dal_ref, dfb_ref):
        xd = x_ref[0:16, :] + db_ref[...]
        dt = _softplus(xd)
        a = -jnp.exp(al_ref[...])
        d_da = _suffix_sum(da1_ref[...] + da2_ref[...], CHUNK)
        d_dt = ddt_ref[...] + d_da * a
        dal_ref[...] = jnp.sum(d_da * dt, axis=1, keepdims=True) * a
        d_xd = d_dt * jax.nn.sigmoid(xd)
        ddb_ref[...] = jnp.sum(d_xd, axis=1, keepdims=True)
        xf = x_ref[16:32, :] + fb_ref[...]
        d_xf = _suffix_sum(dc_ref[...], S) * jax.nn.sigmoid(-xf)
        dfb_ref[...] = jnp.sum(d_xf, axis=1, keepdims=True)
        dx_ref[0:16, :] = d_xd
        dx_ref[16:32, :] = d_xf

    return pl.pallas_call(body, name="dtf_bwd", out_shape=[SDS((32, S), F32)] + [SDS((16, 1), F32)] * 3,
                          compiler_params=_cparams())(dtf_t, dt_bias, a_log, f_bias, d_dt, d_acs_a, d_acs_b, d_cum)


SSM_PAIRS = SSM_HEADS // 2 // SSM_GROUPS


def _ssd_pair(xs, dtc, acol, arow, bm, cm, cbm, h, hp):
    L = CHUNK
    first = _iota((1, LANES), 1) < SSM_P
    i16, s16 = _iota((L, 16), 1), _iota((16, L), 0)
    ha, hb = 2 * hp, 2 * hp + 1

    def selc(blk, hh):
        return jnp.sum(jnp.where(i16 == hh, blk, 0.0), axis=1, keepdims=True)

    def selr(blk, hh):
        return jnp.sum(jnp.where(s16 == hh, blk, 0.0), axis=0, keepdims=True)

    x = xs * jnp.where(first, selc(dtc, ha), selc(dtc, hb))
    ca, cb, ra, rb = selc(acol, ha), selc(acol, hb), selr(arow, ha), selr(arow, hb)
    tri = _iota((L, L), 0) >= _iota((L, L), 1)
    la = jnp.exp(jnp.where(tri, ca - ra, NEG))
    lb = jnp.exp(jnp.where(tri, cb - rb, NEG))
    y = jnp.where(first, mm_nn(cbm * la, x), mm_nn(cbm * lb, x))
    y = y + jnp.where(first, jnp.exp(ca), jnp.exp(cb)) * mm_nn(cm, h)
    last = _iota((1, L), 1) == L - 1
    ala = jnp.sum(jnp.where(last, ra, 0.0), axis=1, keepdims=True)
    alb = jnp.sum(jnp.where(last, rb, 0.0), axis=1, keepdims=True)
    dec = jnp.where(first, jnp.exp(ala - ca), jnp.exp(alb - cb))
    hn = jnp.where(first, jnp.exp(ala), jnp.exp(alb)) * h + mm_tn(bm, x * dec)
    return y, hn


def _ssd_group(*args, grp):
    xs, (dtc, acol, arow, bm, cm), hs = args[:SSM_PAIRS], args[SSM_PAIRS:SSM_PAIRS + 5], args[SSM_PAIRS + 5:]
    cbm = mm_nt(cm, bm)
    res = [_ssd_pair(xs[j], dtc, acol, arow, bm, cm, cbm, hs[j], SSM_PAIRS * grp + j) for j in range(SSM_PAIRS)]
    return tuple(r[0] for r in res) + tuple(r[1] for r in res)


def _ssd_specs(nc, rev):
    L = CHUNK
    cidx = (lambda c: nc - 1 - c) if rev else (lambda c: c)
    return dict(
        xs=pl.BlockSpec((L, SSM_PAIRS * LANES), lambda c, g: (cidx(c), g)),
        col=pl.BlockSpec((L, 16), lambda c, g: (cidx(c), 0)),
        row=pl.BlockSpec((16, L), lambda c, g: (0, cidx(c))),
        b=pl.BlockSpec((L, SSM_N), lambda c, g: (cidx(c), g)),
        c=pl.BlockSpec((L, SSM_N), lambda c, g: (cidx(c), SSM_GROUPS + g)),
        st=pl.BlockSpec((1, SSM_PAIRS, SSM_N, LANES), lambda c, g: (cidx(c), g, 0, 0)),
    )


def _lane_pieces(v):
    return [v[:, LANES * j:LANES * (j + 1)] for j in range(v.shape[1] // LANES)]


def _ssd_fwd(xs, dt_col, acs_col, acs_row, bc):
    S = xs.shape[0]
    nc, nhp = S // CHUNK, SSM_HEADS // 2
    sp = _ssd_specs(nc, False)

    def body(xs_ref, dt_ref, ac_ref, ar_ref, b_ref, c_ref, y_ref, hs_ref, h_scr):
        c, g = pl.program_id(0), pl.program_id(1)

        @pl.when(c == 0)
        def _():
            for j in range(SSM_PAIRS):
                h_scr[SSM_PAIRS * g + j] = jnp.zeros((SSM_N, LANES), F32)

        hs = [h_scr[SSM_PAIRS * g + j] for j in range(SSM_PAIRS)]
        for j in range(SSM_PAIRS):
            hs_ref[0, j] = hs[j]
        res = _ssd_group(*_lane_pieces(xs_ref[...]), dt_ref[...], ac_ref[...], ar_ref[...], b_ref[...], c_ref[...], *hs,
                         grp=g)
        y_ref[...] = jnp.concatenate(res[:SSM_PAIRS], axis=1)
        for j in range(SSM_PAIRS):
            h_scr[SSM_PAIRS * g + j] = res[SSM_PAIRS + j]

    return pl.pallas_call(
        body, name="ssd_fwd", grid=(nc, SSM_GROUPS),
        in_specs=[sp["xs"], sp["col"], sp["col"], sp["row"], sp["b"], sp["c"]],
        out_specs=[sp["xs"], sp["st"]],
        out_shape=[SDS((S, SSM_HEADS * SSM_P), F32), SDS((nc, nhp, SSM_N, LANES), F32)],
        scratch_shapes=[pltpu.VMEM((nhp, SSM_N, LANES), F32)],
        compiler_params=_cparams(dimension_semantics=("arbitrary", "arbitrary")),
    )(xs, dt_col, acs_col, acs_row, bc, bc)


def _ssd_bwd(xs, dt_col, acs_col, acs_row, bc, hs, dy):
    S = xs.shape[0]
    nc, nhp = S // CHUNK, SSM_HEADS // 2
    sp = _ssd_specs(nc, True)

    def body(xs_ref, dt_ref, ac_ref, ar_ref, b_ref, c_ref, hs_ref, dy_ref,
             dxs_ref, ddt_ref, dac_ref, dar_ref, db_ref, dc_ref, dh_scr):
        c, g = pl.program_id(0), pl.program_id(1)

        @pl.when(c == 0)
        def _():
            for j in range(SSM_PAIRS):
                dh_scr[SSM_PAIRS * g + j] = jnp.zeros((SSM_N, LANES), F32)

        _, vjp = jax.vjp(functools.partial(_ssd_group, grp=g), *_lane_pieces(xs_ref[...]), dt_ref[...], ac_ref[...],
                         ar_ref[...], b_ref[...], c_ref[...], *[hs_ref[0, j] for j in range(SSM_PAIRS)])
        grads = vjp(tuple(_lane_pieces(dy_ref[...])) + tuple(dh_scr[SSM_PAIRS * g + j] for j in range(SSM_PAIRS)))
        dxs_ref[...] = jnp.concatenate(grads[:SSM_PAIRS], axis=1)
        ddt, dac, dar, db, dc = grads[SSM_PAIRS:SSM_PAIRS + 5]
        for j in range(SSM_PAIRS):
            dh_scr[SSM_PAIRS * g + j] = grads[SSM_PAIRS + 5 + j]
        db_ref[...] = db
        dc_ref[...] = dc

        @pl.when(g == 0)
        def _():
            ddt_ref[...] = ddt
            dac_ref[...] = dac
            dar_ref[...] = dar

        @pl.when(g > 0)
        def _():
            ddt_ref[...] += ddt
            dac_ref[...] += dac
            dar_ref[...] += dar

    return pl.pallas_call(
        body, name="ssd_bwd", grid=(nc, SSM_GROUPS),
        in_specs=[sp["xs"], sp["col"], sp["col"], sp["row"], sp["b"], sp["c"], sp["st"], sp["xs"]],
        out_specs=[sp["xs"], sp["col"], sp["col"], sp["row"], sp["b"], sp["b"]],
        out_shape=[SDS((S, SSM_HEADS * SSM_P), F32), SDS((S, 16), F32), SDS((S, 16), F32), SDS((16, S), F32),
                   SDS((S, SSM_GROUPS * SSM_N), F32), SDS((S, SSM_GROUPS * SSM_N), F32)],
        scratch_shapes=[pltpu.VMEM((nhp, SSM_N, LANES), F32)],
        compiler_params=_cparams(dimension_semantics=("arbitrary", "arbitrary")),
    )(xs, dt_col, acs_col, acs_row, bc, bc, hs, dy)


ATT_T = 1024


def _pick_col(blk, h):
    return jnp.sum(jnp.where(_iota(blk.shape, 1) == h, blk, 0.0), axis=1, keepdims=True)


def _pick_row(blk, h):
    return jnp.sum(jnp.where(_iota(blk.shape, 0) == h, blk, 0.0), axis=0, keepdims=True)


def _pair_norm(x, g2, first):
    x2 = x * x
    sa = jnp.sum(jnp.where(first, x2, 0.0), axis=1, keepdims=True)
    sb = jnp.sum(jnp.where(first, 0.0, x2), axis=1, keepdims=True)
    r = jnp.where(first, lax.rsqrt(sa * (1.0 / ATT_D) + EPS), lax.rsqrt(sb * (1.0 / ATT_D) + EPS))
    return x * r * g2, r


def _pair_norm_bwd(dxn, x, r, g2, first):
    t = dxn * g2
    tx = t * x
    ma = jnp.sum(jnp.where(first, tx, 0.0), axis=1, keepdims=True)
    mb = jnp.sum(jnp.where(first, 0.0, tx), axis=1, keepdims=True)
    dx = r * (t - x * (r * r) * (jnp.where(first, ma, mb) * (1.0 / ATT_D)))
    return dx, jnp.sum(dxn * x * r, axis=0, keepdims=True)


def _fox_fwd(src, q_c0, k_c0, v_c0, gq2, gk2, cum_col, cum_row3):
    S = src.shape[0]
    T = ATT_T
    nq, nhp = S // T, ATT_HEADS // 2
    qb0, kb0, vb0 = q_c0 // LANES, k_c0 // LANES, v_c0 // LANES
    scale = ATT_D ** -0.5

    def body(q_ref, kraw_ref, v_ref, gq_ref, gk_ref, cc_ref, cr_ref, o_ref, l_ref, k_ref):
        hp, i = pl.program_id(0), pl.program_id(1)
        first = _iota((1, LANES), 1) < ATT_D

        @pl.when(i == 0)
        def _():
            k_ref[...] = _pair_norm(kraw_ref[...], gk_ref[...], first)[0].astype(BF16)

        q = (_pair_norm(q_ref[...], gq_ref[...], first)[0] * scale).astype(BF16)
        zero = jnp.zeros_like(q)
        qs = (jnp.where(first, q, zero), jnp.where(first, zero, q))
        cc = cc_ref[...]
        cq = (_pick_col(cc, 2 * hp), _pick_col(cc, 2 * hp + 1))
        tri = _iota((T, T), 0) >= _iota((T, T), 1)

        def tile(j, carry, diagonal):
            off = pl.multiple_of(j * T, T)
            k = k_ref[pl.ds(off, T), :]
            v = v_ref[pl.ds(off, T), :].astype(BF16)
            cr = cr_ref[j]
            out = []
            for hh in range(2):
                m, l, acc = carry[3 * hh:3 * hh + 3]
                s = _bdot(qs[hh], k, _NT) + (cq[hh] - _pick_row(cr, 2 * hp + hh))
                if diagonal:
                    s = jnp.where(tri, s, NEG)
                m_new = jnp.maximum(m, jnp.max(s, axis=1, keepdims=True))
                alpha = jnp.exp(m - m_new)
                p = jnp.exp(s - m_new)
                out += [m_new, alpha * l + jnp.sum(p, axis=1, keepdims=True), alpha * acc + _bdot(p, v, _NN)]
            return tuple(out)

        init = (jnp.full((T, 1), NEG, F32), jnp.zeros((T, 1), F32), jnp.zeros((T, LANES), F32)) * 2
        carry = lax.fori_loop(0, i, lambda j, c: tile(j, c, False), init)
        ma, la, acca, mb, lb, accb = tile(i, carry, True)
        o_ref[...] = jnp.where(first, acca / la, accb / lb).astype(o_ref.dtype)
        l_ref[...] = jnp.where(first, ma + jnp.log(la), mb + jnp.log(lb))

    gain = pl.BlockSpec((1, LANES), lambda hp, i: (0, 0))
    return pl.pallas_call(
        body, name="fox_fwd", grid=(nhp, nq),
        in_specs=[pl.BlockSpec((T, LANES), lambda hp, i: (i, qb0 + hp)), pl.BlockSpec((S, LANES), lambda hp, i: (0, kb0 + hp)),
                  pl.BlockSpec((S, LANES), lambda hp, i: (0, vb0 + hp)), gain, gain,
                  pl.BlockSpec((T, 16), lambda hp, i: (i, 0)), pl.BlockSpec((nq, 16, T), lambda hp, i: (0, 0, 0))],
        out_specs=[pl.BlockSpec((T, LANES), lambda hp, i: (i, hp))] * 2,
        out_shape=[SDS((S, ATT_HEADS * ATT_D), BF16), SDS((S, ATT_HEADS * ATT_D), F32)],
        scratch_shapes=[pltpu.VMEM((S, LANES), BF16)],
        compiler_params=_cparams(dimension_semantics=("arbitrary", "arbitrary")),
    )(src, src, src, gq2, gk2, cum_col, cum_row3)


def _fox_bwd(src, q_c0, k_c0, v_c0, gq2, gk2, cum_col, cum_row3, lse, dsrc, d_c0):
    S = src.shape[0]
    T = ATT_T
    nq, nhp = S // T, ATT_HEADS // 2
    qb0, kb0, vb0, db0 = q_c0 // LANES, k_c0 // LANES, v_c0 // LANES, d_c0 // LANES
    scale = ATT_D ** -0.5

    def body(q_ref, kraw_ref, v_ref, gq_ref, gk_ref, cc_ref, cr_ref, l_ref, do_ref,
             dq_ref, dk_ref, dv_ref, dc_ref, dg_ref, k_ref, dk_acc, dv_acc, p_scr, dp_scr):
        hp, i = pl.program_id(0), pl.program_id(1)
        first = _iota((1, LANES), 1) < ATT_D
        tri = _iota((T, T), 0) >= _iota((T, T), 1)

        @pl.when(i == 0)
        def _():
            k_ref[...] = _pair_norm(kraw_ref[...], gk_ref[...], first)[0].astype(BF16)
            dk_acc[...] = jnp.zeros_like(dk_acc)
            dv_acc[...] = jnp.zeros_like(dv_acc)
            dc_ref[...] = jnp.zeros_like(dc_ref)
            dg_ref[...] = jnp.zeros_like(dg_ref)

        q_raw = q_ref[...]
        qn, rq = _pair_norm(q_raw, gq_ref[...], first)
        q = (qn * scale).astype(BF16)
        zq = jnp.zeros_like(q)
        dob = do_ref[...].astype(BF16)
        zd = jnp.zeros_like(dob)
        lse_blk, cc = l_ref[...], cc_ref[...]
        dq = jnp.zeros((T, LANES), F32)
        for hh in range(2):
            sel = first if hh == 0 else jnp.logical_not(first)
            qh, doh = jnp.where(sel, q, zq), jnp.where(sel, dob, zd)
            bias_q = _pick_col(cc, 2 * hp + hh) - jnp.max(jnp.where(sel, lse_blk, NEG), axis=1, keepdims=True)

            def probs(j, delta, diagonal):
                off = pl.multiple_of(j * T, T)
                s = _bdot(qh, k_ref[pl.ds(off, T), :], _NT) + (bias_q - _pick_row(cr_ref[j], 2 * hp + hh))
                if diagonal:
                    s = jnp.where(tri, s, NEG)
                p = jnp.exp(s)
                dp = _bdot(doh, v_ref[pl.ds(off, T), :], _NT)
                p_scr[j] = p
                dp_scr[j] = dp
                return delta + jnp.sum(p * dp, axis=1, keepdims=True)

            delta = lax.fori_loop(0, i, lambda j, d: probs(j, d, False), jnp.zeros((T, 1), F32))
            delta = probs(i, delta, True)

            def grads(j, dq):
                off = pl.multiple_of(j * T, T)
                p = p_scr[j]
                ds = p * (dp_scr[j] - delta)
                dv_acc[pl.ds(off, T), :] += _bdot(p, doh, _TN)
                dk_acc[pl.ds(off, T), :] += _bdot(ds, qh, _TN)
                dc_ref[0, j, hh:hh + 1, :] -= jnp.sum(ds, axis=0, keepdims=True)
                zk = jnp.zeros((T, LANES), BF16)
                return dq + _bdot(ds, jnp.where(sel, k_ref[pl.ds(off, T), :], zk), _NN)

            dq = lax.fori_loop(0, i + 1, grads, dq)
        dq_raw, dgq = _pair_norm_bwd(dq * scale, q_raw, rq, gq_ref[...], first)
        dq_ref[...] = dq_raw.astype(dq_ref.dtype)
        dg_ref[0, 0:1, :] += dgq

        @pl.when(i == nq - 1)
        def _():
            k_raw = kraw_ref[...]
            rk = _pair_norm(k_raw, gk_ref[...], first)[1]
            dk_raw, dgk = _pair_norm_bwd(dk_acc[...], k_raw, rk, gk_ref[...], first)
            dk_ref[...] = dk_raw.astype(dk_ref.dtype)
            dv_ref[...] = dv_acc[...].astype(dv_ref.dtype)
            dg_ref[0, 1:2, :] = dgk

    gain = pl.BlockSpec((1, LANES), lambda hp, i: (0, 0))
    band = SDS((S, ATT_HEADS * ATT_D), BF16)
    return pl.pallas_call(
        body, name="fox_bwd", grid=(nhp, nq),
        in_specs=[pl.BlockSpec((T, LANES), lambda hp, i: (i, qb0 + hp)), pl.BlockSpec((S, LANES), lambda hp, i: (0, kb0 + hp)),
                  pl.BlockSpec((S, LANES), lambda hp, i: (0, vb0 + hp)), gain, gain,
                  pl.BlockSpec((T, 16), lambda hp, i: (i, 0)), pl.BlockSpec((nq, 16, T), lambda hp, i: (0, 0, 0)),
                  pl.BlockSpec((T, LANES), lambda hp, i: (i, hp)), pl.BlockSpec((T, LANES), lambda hp, i: (i, db0 + hp))],
        out_specs=[pl.BlockSpec((T, LANES), lambda hp, i: (i, hp)), pl.BlockSpec((S, LANES), lambda hp, i: (0, hp)),
                   pl.BlockSpec((S, LANES), lambda hp, i: (0, hp)), pl.BlockSpec((1, nq, 8, T), lambda hp, i: (hp, 0, 0, 0)),
                   pl.BlockSpec((1, 8, LANES), lambda hp, i: (hp, 0, 0))],
        out_shape=[band, band, band, SDS((nhp, nq, 8, T), F32), SDS((nhp, 8, LANES), F32)],
        scratch_shapes=[pltpu.VMEM((S, LANES), BF16), pltpu.VMEM((S, LANES), F32), pltpu.VMEM((S, LANES), F32),
                        pltpu.VMEM((nq, T, T), F32), pltpu.VMEM((nq, T, T), F32)],
        compiler_params=_cparams(dimension_semantics=("arbitrary", "arbitrary")),
    )(src, src, src, gq2, gk2, cum_col, cum_row3, lse, dsrc)


def _fold_gains(dg):
    def body(d_ref, o_ref):
        t = d_ref[0]
        for h in range(1, dg.shape[0]):
            t = t + d_ref[h]
        o_ref[...] = t + pltpu.roll(t, ATT_D, axis=1)

    return pl.pallas_call(body, name="fold_gains", out_shape=SDS(dg.shape[1:], F32), compiler_params=_cparams())(dg)


def _loss_head(y, target, *, tm):
    M, W = y.shape

    def body(y_ref, t_ref, dy_ref, loss_ref):
        @pl.when(pl.program_id(0) == 0)
        def _():
            loss_ref[...] = jnp.zeros_like(loss_ref)

        e = y_ref[...] - t_ref[...]
        dy_ref[...] = e * (1.0 / W)
        loss_ref[...] += jnp.sum(jnp.sum(e * e, axis=1, keepdims=True), axis=0, keepdims=True) * (0.5 / W)

    return pl.pallas_call(
        body, name="loss_head", grid=(M // tm,),
        in_specs=[pl.BlockSpec((tm, W), lambda i: (i, 0))] * 2,
        out_specs=[pl.BlockSpec((tm, W), lambda i: (i, 0)), pl.BlockSpec((1, 1), lambda i: (0, 0))],
        out_shape=[SDS((M, W), F32), SDS((1, 1), F32)],
        compiler_params=_cparams(dimension_semantics=("arbitrary",)),
    )(y, target)


def _adamw_math(w, g, m, v):
    m = ADAM_B1 * m + (1.0 - ADAM_B1) * g
    v = ADAM_B2 * v + (1.0 - ADAM_B2) * jnp.square(g)
    m_hat = m / (1.0 - ADAM_B1 ** ADAM_STEP)
    v_hat = v / (1.0 - ADAM_B2 ** ADAM_STEP)
    delta = -ADAM_LR * (m_hat / (jnp.sqrt(v_hat) + ADAM_EPS) + ADAM_WD * w)
    return delta, m, v


def _reduce_adamw(parts, w, m, v, *, tr, name, tc=None):
    R, C = w.shape
    tr, tc = min(tr, R), tc or C
    nparts = parts.shape[0]

    def body(p_ref, w_ref, m_ref, v_ref, g_ref, d_ref, nm_ref, nv_ref):
        g = p_ref[0].astype(F32)
        for s in range(1, nparts):
            g = g + p_ref[s].astype(F32)
        g_ref[...] = g
        d_ref[...], nm_ref[...], nv_ref[...] = _adamw_math(w_ref[...], g, m_ref[...], v_ref[...])

    blk = pl.BlockSpec((tr, tc), lambda i, j: (i, j))
    return pl.pallas_call(
        body, name=name, grid=(R // tr, C // tc),
        in_specs=[pl.BlockSpec((nparts, tr, tc), lambda i, j: (0, i, j)), blk, blk, blk], out_specs=[blk] * 4,
        out_shape=[SDS((R, C), F32)] * 4, compiler_params=_cparams(dimension_semantics=("parallel", "parallel")),
    )(parts, w, m, v)


def _adamw(w, g, m, v, *, name):
    def body(w_ref, g_ref, m_ref, v_ref, d_ref, nm_ref, nv_ref):
        d_ref[...], nm_ref[...], nv_ref[...] = _adamw_math(w_ref[...], g_ref[...], m_ref[...], v_ref[...])

    return pl.pallas_call(body, name=name, out_shape=[SDS(w.shape, F32)] * 3, compiler_params=_cparams())(w, g, m, v)


def _peers():
    x, y, c = lax.axis_index("x"), lax.axis_index("y"), lax.axis_index("c")
    out = []
    for k in range(1, N_DEV):
        px, py, pc = x ^ ((k >> 2) & 1), y ^ ((k >> 1) & 1), c ^ (k & 1)
        out.append(((px, py, pc), 4 * px + 2 * py + pc))
    return 4 * x + 2 * y + c, out


_HBM = pl.BlockSpec(memory_space=pltpu.HBM)
_SEM = pl.BlockSpec(memory_space=pltpu.SEMAPHORE)
_DATAFLOW = pltpu.SideEffectType.DATAFLOW_SIDE_EFFECTING


NEAR = (1, 2, 4, 6)


def _plan_peers(scatter, ks=tuple(range(1, N_DEV))):
    return lambda me, peers: [(peers[k - 1][0], peers[k - 1][1] if scatter else None, me, k - 1) for k in ks]


def _plan_relay(me, peers):
    return [(peers[0][0], peers[k - 1][1], peers[k - 1][1], j) for j, k in enumerate((2, 4, 6))]


def _plan_pair(me, peers):
    return [(peers[0][0], peers[k - 1][1], j, j) for j, k in enumerate((1, 3, 5, 7))]


def _plan_chips(me, peers):
    return [(peers[k - 1][0], k // 2, k // 2, k // 2) for k in (2, 4, 6)]


def _copy(src, dst, c, send_sems, recv_sems):
    dev, s_slot, d_slot, i = c
    return pltpu.make_async_remote_copy(
        src_ref=src if s_slot is None else src.at[s_slot], dst_ref=dst.at[d_slot], send_sem=send_sems.at[i],
        recv_sem=recv_sems.at[i], device_id=dev, device_id_type=MESH)


def _copies_start(items, *, name):
    n = len(items)
    bufs = [it[0] for it in items] + [it[1] for it in items if it[1] is not None]
    nb = len(bufs)

    def body(*refs):
        srcs, extra, sems, token = refs[:n], iter(refs[n:nb]), refs[nb:nb + 2 * n], refs[-1]
        me, peers = _peers()
        for a, (_, land, plan) in enumerate(items):
            dst = srcs[a] if land is None else next(extra)
            for c in plan(me, peers):
                _copy(srcs[a], dst, c, sems[2 * a], sems[2 * a + 1]).start()
        token[...] = jnp.zeros_like(token)

    res = pl.pallas_call(
        body, name=name,
        out_shape=([pltpu.SemaphoreType.DMA((N_DEV - 1,))] * (2 * n) + [pltpu.HBM(b.shape, b.dtype) for b in bufs]
                   + [SDS((8, LANES), F32)]),
        in_specs=[_HBM] * nb, out_specs=[_SEM] * (2 * n) + [_HBM] * nb + [pl.BlockSpec(memory_space=pltpu.VMEM)],
        input_output_aliases={i: 2 * n + i for i in range(nb)},
        compiler_params=pltpu.CompilerParams(has_side_effects=_DATAFLOW),
    )(*[pltpu.with_memory_space_constraint(b, pltpu.HBM) for b in bufs])
    sems, thru, token = res[:2 * n], list(res[2 * n:2 * n + nb]), res[-1]
    extra = iter(thru[n:])
    return [(thru[a], None if it[1] is None else next(extra), sems[2 * a], sems[2 * a + 1], it[2])
            for a, it in enumerate(items)], token


def _copies_wait(handles, after, *, name):
    n = len(handles)
    bufs = [h[0] for h in handles] + [h[1] for h in handles if h[1] is not None]
    nb = len(bufs)

    def body(*refs):
        srcs, extra, sems = refs[:n], iter(refs[n:nb]), refs[nb:nb + 2 * n]
        me, peers = _peers()
        for a, h in enumerate(handles):
            dst = srcs[a] if h[1] is None else next(extra)
            for c in h[4](me, peers):
                cp = _copy(srcs[a], dst, c, sems[2 * a], sems[2 * a + 1])
                cp.wait_send()
                cp.wait_recv()

    flat_sems = [s for h in handles for s in (h[2], h[3])]
    res = pl.pallas_call(
        body, name=name, out_shape=[pltpu.HBM(b.shape, b.dtype) for b in bufs],
        in_specs=[_HBM] * nb + [_SEM] * (2 * n) + [pl.BlockSpec(memory_space=pl.ANY)], out_specs=[_HBM] * nb,
        input_output_aliases={i: i for i in range(nb)},
        compiler_params=pltpu.CompilerParams(has_side_effects=_DATAFLOW),
    )(*bufs, *flat_sems, after)
    extra = iter(res[n:])
    return [(res[a], res[a] if h[1] is None else next(extra)) for a, h in enumerate(handles)]


def _exchange_start(arrays, *, scatter, name, near=()):
    items = []
    for a, arr in enumerate(arrays):
        land = lax.empty(arr.shape if scatter else (N_DEV,) + arr.shape, arr.dtype)
        items.append((arr, land, _plan_peers(scatter, NEAR) if a in near else _plan_peers(scatter)))
    return _copies_start(items, name=name)


MOVE_ROWS, MOVE_SLOTS = 512, 3


def _move_rows(src, moves, rows, *, name):
    C = src.shape[1]
    covered = max(dst + n for _, n, dst in moves)
    tail = rows - covered
    assert sum(n for _, n, _ in moves) == covered
    chunks = [(lo + o, min(MOVE_ROWS, n - o), dst + o) for lo, n, dst in moves for o in range(0, n, MOVE_ROWS)]
    nch = len(chunks)

    def body(src_ref, o_ref, buf, sin, sout, *zero):
        def fetch(i):
            lo, n, _ = chunks[i]
            return pltpu.make_async_copy(src_ref.at[pl.ds(lo, n)], buf.at[i % MOVE_SLOTS, pl.ds(0, n)], sin.at[i % MOVE_SLOTS])

        def store(i):
            _, n, dst = chunks[i]
            return pltpu.make_async_copy(buf.at[i % MOVE_SLOTS, pl.ds(0, n)], o_ref.at[pl.ds(dst, n)], sout.at[i % MOVE_SLOTS])

        if tail:
            zero[0][...] = jnp.zeros_like(zero[0])
            fill = pltpu.make_async_copy(zero[0], o_ref.at[pl.ds(covered, tail)], zero[1])
            fill.start()
        for i in range(nch):
            if i >= MOVE_SLOTS:
                store(i - MOVE_SLOTS).wait()
            fetch(i).start()
            if i >= 1:
                fetch(i - 1).wait()
                store(i - 1).start()
        fetch(nch - 1).wait()
        store(nch - 1).start()
        for i in range(max(0, nch - MOVE_SLOTS), nch):
            store(i).wait()
        if tail:
            fill.wait()

    anyspec = pl.BlockSpec(memory_space=pl.ANY)
    dma = pltpu.SemaphoreType.DMA
    return pl.pallas_call(
        body, name=name, in_specs=[anyspec], out_specs=anyspec, out_shape=SDS((rows, C), src.dtype),
        scratch_shapes=([pltpu.VMEM((MOVE_SLOTS, MOVE_ROWS, C), src.dtype), dma((MOVE_SLOTS,)), dma((MOVE_SLOTS,))]
                        + ([pltpu.VMEM((tail, C), src.dtype), dma] if tail else [])),
        compiler_params=_cparams())(src)


def _pair_sum(a, b, *, name):
    n, R, C = a.shape
    tc = 256

    def body(a_ref, b_ref, o_ref):
        o_ref[...] = (a_ref[...].astype(F32) + b_ref[...].astype(F32)).astype(o_ref.dtype)

    blk = pl.BlockSpec((1, R, tc), lambda i, j: (i, 0, j))
    return pl.pallas_call(body, name=name, grid=(n, C // tc), in_specs=[blk, blk], out_specs=blk,
                          out_shape=SDS(a.shape, a.dtype), compiler_params=_cparams(dimension_semantics=("parallel", "parallel")))(a, b)


def _own_slot(landed, own, me):
    return lax.dynamic_update_slice(landed, own[None], (me,) + (0,) * own.ndim)


SMALL = (("g_mix", 1024), ("conv_w", 6144), ("conv_b", 1536), ("dt_bias", 16), ("a_log", 16), ("d_skip", 16),
         ("ssm_norm_w", 1024), ("g_q", 64), ("g_k", 64), ("f_bias", 16), ("g_xattn", 1024), ("g_mem", 1024),
         ("xg_q", 256), ("xg_k", 256), ("g_mlp", 1024))
SLAB_ROWS = 112
BIG = ("w_in", "w_out", "xq_w", "xkv_w", "xo_w", "w_up", "w_down")
WEIGHTS = ("g_mix", "w_in", "conv_w", "conv_b", "dt_bias", "a_log", "d_skip", "ssm_norm_w", "g_q", "g_k", "f_bias", "w_out",
           "g_xattn", "g_mem", "xq_w", "xkv_w", "xg_q", "xg_k", "xo_w", "g_mlp", "w_up", "w_down")
O_Z, O_XS, O_B, O_C, O_DT, O_Q, O_K, O_V, O_F, O_END = 0, 1024, 2048, 2304, 2560, 2576, 3600, 4624, 5648, 5664
IN_ROW_MOVES = ((O_Z, O_B - O_Z, C_Z), (O_Q, O_F - O_Q, C_Q), (O_B, O_Q - O_B, C_B), (O_F, O_END - O_F, C_DTF + 16))


def _pack_small(vals):
    rows = []
    for name, size in SMALL:
        flat = vals[name].reshape(-1).astype(F32)
        pad = -size % LANES
        rows.append(jnp.pad(flat, (0, pad)).reshape(-1, LANES))
    slab = jnp.concatenate(rows, axis=0)
    return jnp.pad(slab, ((0, SLAB_ROWS - slab.shape[0]), (0, 0)))


def _unpack_small(slab):
    out, r = {}, 0
    for name, size in SMALL:
        nr = -(-size // LANES)
        out[name] = slab[r:r + nr].reshape(-1)[:size]
        r += nr
    return out


def _step(p, m, v, x, mem, target):
    S = x.shape[0]
    TM = 256
    me = 4 * lax.axis_index("x") + 2 * lax.axis_index("y") + lax.axis_index("c")

    def rms(u, g, name):
        return _rw_fwd(_rms_fn, [_whole(u)], [_whole(g)], [(D_MODEL, BF16)], tm=TM, name=name)[0]

    def pin(param, token):
        return param + token[0:1, 0:1]

    def landed_with_own(pairs, scatter):
        out = []
        for src, land in pairs:
            own = lax.dynamic_index_in_dim(src, me, 0, keepdims=False) if scatter else src
            out.append(_own_slot(land, own, me))
        return out

    w_in_own, m_in_own, v_in_own = p["w_in"].T, m["w_in"].T, v["w_in"].T
    ag, ag_token = _exchange_start([w_in_own.astype(BF16), p["conv_w"]] + [p[n].astype(BF16) for n in BIG[1:]],
                                   scatter=False, name="allgather_start", near=(0,))
    h1 = rms(x, pin(p["g_mix"], ag_token), "rms_mix")
    (win_src, win_land), convw_pair = _copies_wait(ag[:2], h1, name="allgather_wait_in")
    relay, token = _copies_start([(win_land, None, _plan_relay)], name="allgather_relay_start")
    win_land = _copies_wait(relay, token, name="allgather_relay_wait")[0][1]
    win_g, convw_g = landed_with_own([(win_src, win_land), convw_pair], False)
    w_in_o = win_g.reshape(O_END, D_MODEL)
    w_in_t = _move_rows(w_in_o, IN_ROW_MOVES, P_COLS, name="w_in_rows")
    conv_w = convw_g.transpose(1, 0, 2).reshape(4, 1536)
    cw_xs, cw_bc = conv_w[:, :1024], conv_w[:, 1024:]
    cb_xs, cb_bc = p["conv_b"][:, :1024], p["conv_b"][:, 1024:]
    dt_bias, a_log, f_bias = p["dt_bias"].reshape(16, 1), p["a_log"].reshape(16, 1), p["f_bias"].reshape(16, 1)

    proj = _matmul(h1, w_in_t, mode="nt", tm=1024, tn=640, tk=1024, name="mm_in")
    xs_c = _conv_fwd(proj, C_XS, 1024, cw_xs, cb_xs, name="conv_xs")
    bc_c = _conv_fwd(proj, C_B, 512, cw_bc, cb_bc, name="conv_bc")
    dtf_t = proj[:, C_DTF:C_DTF + 32].T
    dt_t, acs_t, cum_t = _dtf_fwd(dtf_t, dt_bias, a_log, f_bias)
    dt_col, acs_col, cum_col = dt_t.T, acs_t.T, cum_t.T
    cum_row3 = cum_t.reshape(16, S // ATT_T, ATT_T).transpose(1, 0, 2)
    y_ssd, hs = _ssd_fwd(xs_c, dt_col, acs_col, acs_t, bc_c)
    gate_rows = [_whole(y_ssd), _whole(xs_c), (proj, C_Z, 1024)]
    gate_pars = [_whole(p["d_skip"]), _whole(p["ssm_norm_w"])]
    y_ssm = _rw_fwd(_gate_fn, gate_rows, gate_pars, [(1024, BF16)], tm=TM, name="gate")[0]
    gq2, gk2 = jnp.tile(p["g_q"], (1, 2)), jnp.tile(p["g_k"], (1, 2))
    o, lse = _fox_fwd(proj, C_Q, C_K, C_V, gq2, gk2, cum_col, cum_row3)
    mixed = jnp.concatenate([y_ssm, o], axis=1)
    wout_g, = landed_with_own(_copies_wait(ag[2:3], mixed, name="allgather_wait_out"), False)
    w_out = wout_g.reshape(2 * D_MODEL, D_MODEL)
    x1 = _matmul(mixed, w_out, mode="nn", tm=1024, tn=512, tk=2048, add=x, name="mm_out")
    xq_g, xkv_w, xo_g, w_up, wdown_g = landed_with_own(_copies_wait(ag[3:], x1, name="allgather_wait_rest"), False)
    xq_w = xq_g.reshape(D_MODEL, D_MODEL)
    xo_w = xo_g.reshape(D_MODEL, D_MODEL)
    w_down = wdown_g.reshape(4 * D_MODEL, D_MODEL)

    h2 = rms(x1, p["g_xattn"], "rms_xattn")
    mem_n = rms(mem, p["g_mem"], "rms_mem")
    q2 = _matmul(h2, xq_w, mode="nn", tm=1024, tn=512, tk=1024, name="mm_xq")
    kv = _matmul(mem_n, xkv_w, mode="nn", b_shards=True, tm=256, tn=256, tk=1024, name="mm_xkv")
    xa_rows = [(q2, X_D * h, X_D) for h in range(X_HEADS)]
    xa_pars = ([(kv, X_D * h, X_D) for h in range(X_HEADS)] + [(kv, D_MODEL + X_D * h, X_D) for h in range(X_HEADS)]
               + [_whole(p["xg_q"]), _whole(p["xg_k"])])
    o2 = _rw_fwd(_xattn_fn, xa_rows, xa_pars, [(D_MODEL, BF16)], tm=TM, name="xattn")[0]
    x2 = _matmul(o2, xo_w, mode="nn", tm=1024, tn=512, tk=1024, add=x1, name="mm_xo")

    h3 = rms(x2, p["g_mlp"], "rms_mlp")
    a, usq = _matmul(h3, w_up, mode="nn", b_shards=True, tm=2048, tn=512, tk=1024, name="mm_up", out_dtypes=[F32, BF16],
                     epilogue=lambda acc: (acc, jnp.square(jax.nn.relu(acc))))
    x3 = _matmul(usq, w_down, mode="nn", tm=1024, tn=512, tk=2048, add=x2, name="mm_down")
    dy, loss_part = _loss_head(x3, target, tm=TM)
    loss = lax.psum(loss_part[0, 0], ("x", "y", "c"))

    def row_shards(a):
        r, c = a.shape
        return a.reshape(N_DEV, r // N_DEV, c)

    g = {}
    g["w_down"] = _matmul(usq, dy, mode="tn", out_dtype=GRAD_WIRE, tm=1024, tn=1024, tk=1024, name="mm_d_wdown")
    da = _matmul(dy, w_down, mode="nt", tm=1024, tn=1024, tk=1024, name="mm_d_usq", out_dtype=BF16, extras=(a,),
                 epilogue=lambda acc, av: (2.0 * jax.nn.relu(av) * acc,))
    g["w_up"] = _matmul(h3, da, mode="tn", out_shards=True, out_dtype=GRAD_WIRE, tm=1024, tn=512, tk=1024, name="mm_d_wup")
    sent_mlp, token = _exchange_start([row_shards(g["w_down"]), g["w_up"]], scatter=True,
                                      name="grads_start_mlp")
    dh3 = _matmul(da, w_up, mode="nt", b_shards=True, tm=2048, tn=1024, tk=512, name="mm_d_h3")
    dx2, g["g_mlp"] = _rw_bwd(_rms_fn, [_whole(x2)], [_whole(pin(p["g_mlp"], token))], [_whole(dh3)], tm=TM,
                              name="rms_mlp_bwd", row_grads=[F32], adds={0: _whole(dy)})

    g["xo_w"] = _matmul(o2, dx2, mode="tn", out_dtype=GRAD_WIRE, tm=1024, tn=1024, tk=1024, name="mm_d_wxo")
    do2 = _matmul(dx2, xo_w, mode="nt", tm=1024, tn=512, tk=1024, name="mm_d_o2")
    xa = _rw_bwd(_xattn_fn, xa_rows, xa_pars, [_whole(do2)], tm=TM, name="xattn_bwd", row_grads=[BF16] * X_HEADS)
    dq2 = jnp.concatenate(xa[:X_HEADS], axis=1)
    dkv = jnp.concatenate(xa[X_HEADS:3 * X_HEADS], axis=1)
    g["xg_q"], g["xg_k"] = xa[3 * X_HEADS], xa[3 * X_HEADS + 1]
    g["xq_w"] = _matmul(h2, dq2, mode="tn", out_dtype=GRAD_WIRE, tm=1024, tn=1024, tk=1024, name="mm_d_wxq")
    dh2 = _matmul(dq2, xq_w, mode="nt", tm=1024, tn=512, tk=1024, name="mm_d_h2")
    g["xkv_w"] = _matmul(mem_n, dkv, mode="tn", out_shards=True, out_dtype=GRAD_WIRE, tm=1024, tn=256, tk=256,
                         name="mm_d_wxkv")
    dmem_n = _matmul(dkv, xkv_w, mode="nt", b_shards=True, tm=256, tn=1024, tk=256, name="mm_d_memn")
    g["g_mem"] = _rw_bwd(_rms_fn, [_whole(mem)], [_whole(p["g_mem"])], [_whole(dmem_n)], tm=TM, name="rms_mem_bwd",
                         row_grads=[None])[0]
    dx1, g["g_xattn"] = _rw_bwd(_rms_fn, [_whole(x1)], [_whole(p["g_xattn"])], [_whole(dh2)], tm=TM, name="rms_xattn_bwd",
                                row_grads=[F32], adds={0: _whole(dx2)})

    g["w_out"] = _matmul(mixed, dx1, mode="tn", out_dtype=GRAD_WIRE, tm=1024, tn=1024, tk=1024, name="mm_d_wout")
    sent_mid, token = _exchange_start(
        [row_shards(g["w_out"]), row_shards(g["xq_w"]), g["xkv_w"], row_shards(g["xo_w"])], scatter=True,
        name="grads_start_mid")
    dmixed = _matmul(dx1, w_out, mode="nt", tm=1024, tn=1024, tk=1024, name="mm_d_mixed")
    dq, dk, dv, dcum4, dgain = _fox_bwd(proj, C_Q, C_K, C_V, pin(gq2, token), gk2, cum_col, cum_row3, lse, dmixed, 1024)
    gains = _fold_gains(dgain)
    g["g_q"], g["g_k"] = gains[0:1, :ATT_D], gains[1:2, :ATT_D]
    dy_ssd, dxs_g, dz, g["d_skip"], g["ssm_norm_w"] = _rw_bwd(
        _gate_fn, gate_rows, gate_pars, [(dmixed, 0, 1024)], tm=TM, name="gate_bwd", row_grads=[F32, F32, BF16])
    dxs_s, ddt_col, dacs_col, dacs_row, d_b, d_c = _ssd_bwd(xs_c, dt_col, acs_col, acs_t, bc_c, hs, dy_ssd)
    dcum_t = dcum4[:, :, 0:2, :].transpose(0, 2, 1, 3).reshape(16, S)
    ddtf_t, ddtb, dalog, dfb = _dtf_bwd(dtf_t, dt_bias, a_log, f_bias, ddt_col.T, dacs_col.T, dacs_row, dcum_t)
    g["dt_bias"], g["a_log"], g["f_bias"] = ddtb, dalog, dfb
    dxs_raw, dcw_xs, dcb_xs = _conv_bwd(proj, C_XS, 1024, cw_xs, cb_xs, [dxs_s, dxs_g], name="conv_xs_bwd")
    dbc_raw, dcw_bc, dcb_bc = _conv_bwd(proj, C_B, 512, cw_bc, cb_bc, [jnp.concatenate([d_b, d_c], axis=1)],
                                        name="conv_bc_bwd")
    g["conv_w"] = jnp.concatenate([dcw_xs, dcw_bc], axis=1)
    g["conv_b"] = jnp.concatenate([dcb_xs, dcb_bc], axis=1)
    ddtf = jnp.pad(ddtf_t.T.astype(BF16), ((0, 0), (0, P_COLS - C_DTF - 32)))
    dproj = jnp.concatenate([dz, dxs_raw, dq, dk, dv, dbc_raw, ddtf], axis=1)
    dw_in_p = _matmul(dproj, h1, mode="tn", out_dtype=GRAD_WIRE, tm=640, tn=1024, tk=1024, name="mm_d_win")
    g["w_in"] = _move_rows(dw_in_p, [(dst, n, lo) for lo, n, dst in IN_ROW_MOVES], O_END, name="d_w_in_rows")
    half = N_DEV // 2
    send_in = row_shards(g["w_in"])
    pair, _ = _copies_start([(send_in, lax.empty((half,) + send_in.shape[1:], send_in.dtype), _plan_pair)],
                            name="grads_in_pair_start")
    dh1 = _matmul(dproj, w_in_t, mode="nn", tm=1024, tn=512, tk=1920, name="mm_d_h1")
    send_in, from_sibling = _copies_wait(pair, dh1, name="grads_in_pair_wait")[0]
    mine = jnp.stack([lax.dynamic_index_in_dim(send_in, me ^ (2 * j), 0, keepdims=False) for j in range(half)])
    chip_sums = _pair_sum(mine, from_sibling, name="grads_in_pair_sum")
    sent_in, token = _copies_start([(chip_sums, lax.empty(chip_sums.shape, chip_sums.dtype), _plan_chips)],
                                   name="grads_in_chip_start")
    grad_x, g["g_mix"] = _rw_bwd(_rms_fn, [_whole(x)], [_whole(pin(p["g_mix"], token))], [_whole(dh1)], tm=TM,
                                 name="rms_mix_bwd", row_grads=[F32], adds={0: _whole(dx1)})
    sent_small, _ = _exchange_start([_pack_small(g)], scatter=False, name="small_grads_start")

    grads, delta, new_m, new_v = {}, {}, {}, {}

    def update(names, sent, after, wait_name):
        parts = landed_with_own(_copies_wait(sent, after, name=wait_name), True)
        for name, part in zip(names, parts, strict=True):
            grads[name], delta[name], new_m[name], new_v[name] = _reduce_adamw(part, p[name], m[name], v[name], tr=128,
                                                                                name="adamw_" + name)

    update(("w_down", "w_up"), sent_mlp, grad_x, "grads_wait_mlp")
    update(("w_out", "xq_w", "xkv_w", "xo_w"), sent_mid, delta["w_up"], "grads_wait_mid")
    chip_sums, landed = _copies_wait(sent_in, delta["xo_w"], name="grads_in_chip_wait")[0]
    part = lax.dynamic_update_slice(landed, chip_sums[0:1], (0, 0, 0))
    res = _reduce_adamw(part, w_in_own, m_in_own, v_in_own, tr=part.shape[1], tc=256, name="adamw_w_in")
    grads["w_in"], delta["w_in"], new_m["w_in"], new_v["w_in"] = [r.T for r in res]
    small_parts = landed_with_own(_copies_wait(sent_small, delta["w_in"], name="small_grads_wait"), False)[0]
    zeros_cw = jnp.zeros((4, 1536), F32)
    slabs = [_pack_small({**d, "conv_w": zeros_cw}) for d in (p, m, v)]
    sg, sd, sm, sv = _reduce_adamw(small_parts, *slabs, tr=SLAB_ROWS, name="adamw_small")
    for dst, slab in ((grads, sg), (delta, sd), (new_m, sm), (new_v, sv)):
        for name, flat in _unpack_small(slab).items():
            if name != "conv_w":
                dst[name] = flat.reshape(p[name].shape)
    cw_shard = p["conv_w"].shape[1]
    grads["conv_w"] = lax.dynamic_slice(_unpack_small(sg)["conv_w"].reshape(4, 1536), (0, me * cw_shard), (4, cw_shard))
    delta["conv_w"], new_m["conv_w"], new_v["conv_w"] = _adamw(p["conv_w"], grads["conv_w"], m["conv_w"], v["conv_w"],
                                                               name="adamw_conv_w")
    return loss, grad_x, grads, delta, new_m, new_v


def kernel(x, mem, g_mix, w_in, conv_w, conv_b, dt_bias, a_log, d_skip, ssm_norm_w, g_q, g_k, f_bias, w_out, g_xattn, g_mem, xq_w, xkv_w, xg_q, xg_k, xo_w, g_mlp, w_up, w_down, loss_target, m_g_mix, m_w_in, m_conv_w, m_conv_b, m_dt_bias, m_a_log, m_d_skip, m_ssm_norm_w, m_g_q, m_g_k, m_f_bias, m_w_out, m_g_xattn, m_g_mem, m_xq_w, m_xkv_w, m_xg_q, m_xg_k, m_xo_w, m_g_mlp, m_w_up, m_w_down, v_g_mix, v_w_in, v_conv_w, v_conv_b, v_dt_bias, v_a_log, v_d_skip, v_ssm_norm_w, v_g_q, v_g_k, v_f_bias, v_w_out, v_g_xattn, v_g_mem, v_xq_w, v_xkv_w, v_xg_q, v_xg_k, v_xo_w, v_g_mlp, v_w_up, v_w_down):
    args = locals()
    drop = lambda t: t[0] if t.ndim == 3 else t
    p = {n: drop(args[n]) for n in WEIGHTS}
    m = {n: drop(args["m_" + n]) for n in WEIGHTS}
    v = {n: drop(args["v_" + n]) for n in WEIGHTS}
    loss, grad_x, grads, delta, new_m, new_v = _step(p, m, v, x[0], mem[0], loss_target[0])
    outs = [loss, grad_x[None]]
    for d in (grads, delta, new_m, new_v):
        outs += [d[n].reshape(args[n].shape) for n in WEIGHTS]
    return tuple(outs)
```

```python
import functools
import math

import jax
import jax.numpy as jnp
from jax import lax
from jax.experimental import pallas as pl
from jax.experimental.pallas import tpu as pltpu

F32, BF16 = jnp.float32, jnp.bfloat16
SDS = jax.ShapeDtypeStruct
HI = lax.Precision.HIGHEST
MESH = pl.DeviceIdType.MESH

N_DEV = 8
EPS = 1e-5
D_MODEL = 1024
SSM_HEADS, SSM_P, SSM_N, SSM_GROUPS, CHUNK = 16, 64, 128, 2, 128
ATT_HEADS, ATT_D = 16, 64
X_HEADS, X_D = 4, 256
LANES = 128
VMEM_LIMIT = 48 * 1024 * 1024
NEG = -1e30

GRAD_WIRE = BF16
ADAM_LR, ADAM_B1, ADAM_B2, ADAM_EPS, ADAM_WD, ADAM_STEP = 0.001, 0.9, 0.999, 1e-08, 0.01, 10

C_Z, C_XS, C_Q, C_K, C_V, C_B, C_C, C_DTF, P_COLS = 0, 1024, 2048, 3072, 4096, 5120, 5376, 5632, 5760

_NN = (((1,), (0,)), ((), ()))
_NT = (((1,), (1,)), ((), ()))
_TN = (((0,), (0,)), ((), ()))


def _cparams(**kw):
    return pltpu.CompilerParams(vmem_limit_bytes=VMEM_LIMIT, **kw)


def _bdot(a, b, dn):
    return lax.dot_general(a.astype(BF16), b.astype(BF16), dn, preferred_element_type=F32)


@jax.custom_vjp
def mm_nn(a, b):
    return _bdot(a, b, _NN)


mm_nn.defvjp(lambda a, b: (mm_nn(a, b), (a, b)), lambda r, g: (_bdot(g, r[1], _NT), _bdot(r[0], g, _TN)))


@jax.custom_vjp
def mm_nt(a, b):
    return _bdot(a, b, _NT)


mm_nt.defvjp(lambda a, b: (mm_nt(a, b), (a, b)), lambda r, g: (_bdot(g, r[1], _NN), _bdot(g, r[0], _TN)))


@jax.custom_vjp
def mm_tn(a, b):
    return _bdot(a, b, _TN)


mm_tn.defvjp(lambda a, b: (mm_tn(a, b), (a, b)), lambda r, g: (_bdot(r[1], g, _NT), _bdot(r[0], g, _NN)))


def _cdot(x, c):
    return jnp.dot(x, c, precision=HI, preferred_element_type=F32)


def _iota(shape, dim):
    return lax.broadcasted_iota(jnp.int32, shape, dim)


def _matmul(a, b, *, mode, tm, tn, tk, name, out_dtype=F32, add=None, extras=(), epilogue=None, out_dtypes=None,
            b_shards=False, out_shards=False, row_params=(), sums=()):
    if mode == "tn":
        K, M = a.shape
    else:
        M, K = a.shape
    if b_shards:
        N = b.shape[1] if mode == "nt" else b.shape[0] * b.shape[2]
        tn, tk = (tn, b.shape[2]) if mode == "nt" else (b.shape[2], tk)
    else:
        N = b.shape[0] if mode == "nt" else b.shape[1]
    tm, tn, tk = min(tm, M), min(tn, N), min(tk, K)
    assert M % tm == 0 and N % tn == 0 and K % tk == 0, (name, M, N, K, tm, tn, tk)
    assert not b_shards or (K // tk if mode == "nt" else N // tn) == b.shape[0], name
    assert not (out_shards and (extras or add is not None)), name
    nk = K // tk
    dn = {"nn": _NN, "nt": _NT, "tn": _TN}[mode]
    if add is not None:
        extras, epilogue = (add,), lambda acc, r: (acc + r,)
    elif epilogue is None:
        epilogue = lambda acc: (acc,)
    out_dtypes = out_dtypes or [out_dtype]
    ne, no, ns = len(extras) + len(row_params), len(out_dtypes), len(sums)
    assert all(s == (1, 1) or (s == (1, N) and tn == N) for s in sums), name

    def body(*refs):
        a_ref, b_ref = refs[:2]
        e_refs, o_refs, s_refs = refs[2:2 + ne], refs[2 + ne:2 + ne + no], refs[2 + ne + no:2 + ne + no + ns]

        def finish(acc):
            res = epilogue(acc, *[e[...] for e in e_refs])
            for o_ref, v in zip(o_refs, res[:no], strict=True):
                o_ref[...] = v.astype(o_ref.dtype)
            first_tile = jnp.logical_and(pl.program_id(0) == 0, pl.program_id(1) == 0)
            for s_ref, v in zip(s_refs, res[no:], strict=True):
                @pl.when(first_tile)
                def _(s_ref=s_ref, v=v):
                    s_ref[...] = v

                @pl.when(jnp.logical_not(first_tile))
                def _(s_ref=s_ref, v=v):
                    s_ref[...] += v

        prod = _bdot(a_ref[...], b_ref[...], dn)
        if nk == 1:
            finish(prod)
            return
        acc_ref = refs[-1]
        k = pl.program_id(2)

        @pl.when(k == 0)
        def _():
            acc_ref[...] = prod

        @pl.when(jnp.logical_and(k > 0, k < nk - 1))
        def _():
            acc_ref[...] += prod

        @pl.when(k == nk - 1)
        def _():
            finish(acc_ref[...] + prod)

    a_spec = pl.BlockSpec((tk, tm), lambda i, j, k: (k, i)) if mode == "tn" else pl.BlockSpec((tm, tk), lambda i, j, k: (i, k))
    if b_shards and mode == "nt":
        b_spec = pl.BlockSpec((None, tn, tk), lambda i, j, k: (k, j, 0))
    elif b_shards:
        b_spec = pl.BlockSpec((None, tk, tn), lambda i, j, k: (j, k, 0))
    elif mode == "nt":
        b_spec = pl.BlockSpec((tn, tk), lambda i, j, k: (j, k))
    else:
        b_spec = pl.BlockSpec((tk, tn), lambda i, j, k: (k, j))
    if out_shards:
        o_spec, o_shape = pl.BlockSpec((None, tm, tn), lambda i, j, k: (j, i, 0)), (N // tn, M, tn)
    else:
        o_spec, o_shape = pl.BlockSpec((tm, tn), lambda i, j, k: (i, j)), (M, N)
    row_spec = pl.BlockSpec((1, tn), lambda i, j, k: (0, j))
    sum_specs = [pl.BlockSpec(s, lambda i, j, k: (0, 0)) for s in sums]
    res = pl.pallas_call(
        body, name=name, grid=(M // tm, N // tn, nk),
        in_specs=[a_spec, b_spec] + [o_spec] * len(extras) + [row_spec] * len(row_params),
        out_specs=[o_spec] * no + sum_specs, out_shape=[SDS(o_shape, dt) for dt in out_dtypes] + [SDS(s, F32) for s in sums],
        scratch_shapes=[pltpu.VMEM((tm, tn), F32)] if nk > 1 else [],
        compiler_params=_cparams(dimension_semantics=(("arbitrary",) * 3 if sums else ("parallel", "parallel", "arbitrary"))),
    )(a, b, *extras, *row_params)
    return res[0] if no + ns == 1 else res


def _row_spec(tm, spec):
    _, c0, w = spec
    assert c0 % w == 0
    return pl.BlockSpec((tm, w), functools.partial(lambda i, cb: (i, cb), cb=c0 // w))


def _par_spec(spec):
    arr, c0, w = spec
    assert c0 % w == 0
    return pl.BlockSpec((arr.shape[0], w), functools.partial(lambda i, cb: (0, cb), cb=c0 // w))


def _whole(arr):
    return (arr, 0, arr.shape[1])


def _rw_fwd(fn, rows, params, outs, *, tm, name):
    M = rows[0][0].shape[0]
    nr, npar = len(rows), len(params)

    def body(*refs):
        rv = [r[...].astype(F32) for r in refs[:nr]]
        pv = [p[...].astype(F32) for p in refs[nr:nr + npar]]
        res = fn(*rv, *pv)
        for o_ref, v in zip(refs[nr + npar:], res, strict=True):
            o_ref[...] = v.astype(o_ref.dtype)

    return pl.pallas_call(
        body, name=name, grid=(M // tm,),
        in_specs=[_row_spec(tm, r) for r in rows] + [_par_spec(p) for p in params],
        out_specs=[pl.BlockSpec((tm, w), lambda i: (i, 0)) for w, _ in outs],
        out_shape=[SDS((M, w), dt) for w, dt in outs],
        compiler_params=_cparams(dimension_semantics=("parallel",)),
    )(*[r[0] for r in rows], *[p[0] for p in params])


def _rw_bwd(fn, rows, params, cts, *, tm, name, row_grads, adds=None):
    M = rows[0][0].shape[0]
    adds = adds or {}
    nr, npar, nc = len(rows), len(params), len(cts)
    add_keys = sorted(adds)
    want = [k for k in range(nr) if row_grads[k] is not None]

    def body(*refs):
        pos = 0
        r_refs = refs[pos:pos + nr]; pos += nr
        p_refs = refs[pos:pos + npar]; pos += npar
        c_refs = refs[pos:pos + nc]; pos += nc
        a_refs = dict(zip(add_keys, refs[pos:pos + len(add_keys)])); pos += len(add_keys)
        dr_refs = dict(zip(want, refs[pos:pos + len(want)])); pos += len(want)
        dp_refs = refs[pos:pos + npar]
        rv = [r[...].astype(F32) for r in r_refs]
        pv = [p[...].astype(F32) for p in p_refs]
        _, vjp = jax.vjp(fn, *rv, *pv)
        g = vjp(tuple(c[...].astype(F32) for c in c_refs))
        for k in want:
            v = g[k]
            if k in a_refs:
                v = v + a_refs[k][...].astype(F32)
            dr_refs[k][...] = v.astype(dr_refs[k].dtype)
        first = pl.program_id(0) == 0
        for j in range(npar):
            @pl.when(first)
            def _(j=j):
                dp_refs[j][...] = jnp.zeros_like(dp_refs[j])
            dp_refs[j][...] += g[nr + j]

    res = pl.pallas_call(
        body, name=name, grid=(M // tm,),
        in_specs=([_row_spec(tm, r) for r in rows] + [_par_spec(p) for p in params] + [_row_spec(tm, c) for c in cts]
                  + [_row_spec(tm, adds[k]) for k in add_keys]),
        out_specs=([pl.BlockSpec((tm, rows[k][2]), lambda i: (i, 0)) for k in want]
                   + [pl.BlockSpec((p[0].shape[0], p[2]), lambda i: (0, 0)) for p in params]),
        out_shape=([SDS((M, rows[k][2]), row_grads[k]) for k in want] + [SDS((p[0].shape[0], p[2]), F32) for p in params]),
        compiler_params=_cparams(dimension_semantics=("arbitrary",)),
    )(*[r[0] for r in rows], *[p[0] for p in params], *[c[0] for c in cts], *[adds[k][0] for k in add_keys])
    return res


def _rms_fn(x, g):
    r = lax.rsqrt(jnp.mean(x * x, axis=-1, keepdims=True) + EPS)
    return (x * r * g,)


def _ep_residual_rms(acc, res, g):
    x = acc + res
    return x, _rms_fn(x, g)[0]


def _ep_rms_bwd(dh, x, dres, g):
    r = lax.rsqrt(jnp.mean(x * x, axis=-1, keepdims=True) + EPS)
    t = dh * g
    dx = dres + r * (t - x * (r * r) * jnp.mean(t * x, axis=-1, keepdims=True))
    return dx, jnp.sum(dh * x * r, axis=0, keepdims=True)


def _ep_loss(acc, res, target, *, width):
    e = acc + res - target
    return e * (1.0 / width), jnp.sum(jnp.sum(e * e, axis=1, keepdims=True), axis=0, keepdims=True) * (0.5 / width)


def _seg_mats(width, seg):
    n = width // seg
    p = (_iota((width, n), 0) // seg == _iota((width, n), 1)).astype(F32)
    e = (_iota((n, width), 1) // seg == _iota((n, width), 0)).astype(F32)
    return p, e


def _gate_fn(y, xs, z, dskip, w):
    width = SSM_HEADS * SSM_P
    _, e = _seg_mats(width, SSM_P)
    y = (y + _cdot(dskip, e) * xs) * (z * jax.nn.sigmoid(z))
    g0 = _iota((1, width), 1) < width // SSM_GROUPS
    y2 = y * y
    gw = width // SSM_GROUPS
    ms0 = jnp.sum(jnp.where(g0, y2, 0.0), axis=-1, keepdims=True) * (1.0 / gw)
    ms1 = jnp.sum(jnp.where(g0, 0.0, y2), axis=-1, keepdims=True) * (1.0 / gw)
    r = jnp.where(g0, lax.rsqrt(ms0 + EPS), lax.rsqrt(ms1 + EPS))
    return (y * r * w,)


def _xattn_fn(q0, q1, q2, q3, k0, k1, k2, k3, v0, v1, v2, v3, gq, gk):
    def norm(u, g):
        return u * lax.rsqrt(jnp.mean(u * u, axis=-1, keepdims=True) + EPS) * g
    outs = []
    for q, k, v in ((q0, k0, v0), (q1, k1, v1), (q2, k2, v2), (q3, k3, v3)):
        s = mm_nt(norm(q, gq), norm(k, gk)) * (X_D ** -0.5)
        p = jnp.exp(s - lax.stop_gradient(jnp.max(s, axis=-1, keepdims=True)))
        p = p / jnp.sum(p, axis=-1, keepdims=True)
        outs.append(mm_nn(p, v))
    return (jnp.concatenate(outs, axis=-1),)


CONV_TC = 256


def _shift_down(u, k):
    if k == 0:
        return u
    return jnp.where(_iota(u.shape, 0) >= k, pltpu.roll(u, k, axis=0), 0.0)


def _shift_up(u, k):
    if k == 0:
        return u
    n = u.shape[0]
    return jnp.where(_iota(u.shape, 0) < n - k, pltpu.roll(u, n - k, axis=0), 0.0)


def _conv_pre(u, w_ref, b):
    pre = b + w_ref[3:4, :] * u
    for k in (1, 2, 3):
        pre = pre + w_ref[3 - k:4 - k, :] * _shift_down(u, k)
    return pre


def _conv_fwd(src, c0, width, w, b, *, name):
    S = src.shape[0]
    cb0 = c0 // CONV_TC

    def body(u_ref, w_ref, b_ref, o_ref):
        pre = _conv_pre(u_ref[...], w_ref, b_ref[...])
        o_ref[...] = pre * jax.nn.sigmoid(pre)

    return pl.pallas_call(
        body, name=name, grid=(width // CONV_TC,),
        in_specs=[pl.BlockSpec((S, CONV_TC), lambda j: (0, cb0 + j)), pl.BlockSpec((4, CONV_TC), lambda j: (0, j)),
                  pl.BlockSpec((1, CONV_TC), lambda j: (0, j))],
        out_specs=pl.BlockSpec((S, CONV_TC), lambda j: (0, j)), out_shape=SDS((S, width), F32),
        compiler_params=_cparams(dimension_semantics=("parallel",)),
    )(src, w, b)


def _conv_bwd(src, c0, width, w, b, douts, *, name):
    S = src.shape[0]
    cb0 = c0 // CONV_TC
    nd = len(douts)

    def body(*refs):
        u_ref, w_ref, b_ref = refs[:3]
        d_refs = refs[3:3 + nd]
        du_ref, dw_ref, db_ref = refs[3 + nd:]
        u = u_ref[...]
        pre = _conv_pre(u, w_ref, b_ref[...])
        sg = jax.nn.sigmoid(pre)
        dout = d_refs[0][...]
        for r in d_refs[1:]:
            dout = dout + r[...]
        dpre = dout * (sg * (1.0 + pre * (1.0 - sg)))
        du = w_ref[3:4, :] * dpre
        dw_ref[3:4, :] = jnp.sum(dpre * u, axis=0, keepdims=True)
        for k in (1, 2, 3):
            du = du + w_ref[3 - k:4 - k, :] * _shift_up(dpre, k)
            dw_ref[3 - k:4 - k, :] = jnp.sum(dpre * _shift_down(u, k), axis=0, keepdims=True)
        du_ref[...] = du.astype(du_ref.dtype)
        db_ref[...] = jnp.sum(dpre, axis=0, keepdims=True)

    return pl.pallas_call(
        body, name=name, grid=(width // CONV_TC,),
        in_specs=[pl.BlockSpec((S, CONV_TC), lambda j: (0, cb0 + j)), pl.BlockSpec((4, CONV_TC), lambda j: (0, j)),
                  pl.BlockSpec((1, CONV_TC), lambda j: (0, j))] + [pl.BlockSpec((S, CONV_TC), lambda j: (0, j))] * nd,
        out_specs=[pl.BlockSpec((S, CONV_TC), lambda j: (0, j)), pl.BlockSpec((4, CONV_TC), lambda j: (0, j)),
                   pl.BlockSpec((1, CONV_TC), lambda j: (0, j))],
        out_shape=[SDS((S, width), BF16), SDS((4, width), F32), SDS((1, width), F32)],
        compiler_params=_cparams(dimension_semantics=("parallel",)),
    )(src, w, b, *douts)


def _softplus(x):
    return jnp.maximum(x, 0.0) + jnp.log(1.0 + jnp.exp(-jnp.abs(x)))


def _prefix_sum(x, seg):
    n = x.shape[1]
    pos = _iota(x.shape, 1) % seg
    k = 1
    while k < seg:
        x = x + jnp.where(pos >= k, pltpu.roll(x, k, axis=1), 0.0)
        k *= 2
    return x


def _suffix_sum(x, seg):
    n = x.shape[1]
    pos = _iota(x.shape, 1) % seg
    k = 1
    while k < seg:
        x = x + jnp.where(pos + k < seg, pltpu.roll(x, n - k, axis=1), 0.0)
        k *= 2
    return x


def _dtf_fwd(dtf_t, dt_bias, a_log, f_bias):
    S = dtf_t.shape[1]

    def body(x_ref, db_ref, al_ref, fb_ref, dt_ref, acs_ref, cum_ref):
        dt = _softplus(x_ref[0:16, :] + db_ref[...])
        dt_ref[...] = dt
        acs_ref[...] = _prefix_sum(dt * (-jnp.exp(al_ref[...])), CHUNK)
        cum_ref[...] = _prefix_sum(-_softplus(-(x_ref[16:32, :] + fb_ref[...])), S)

    return pl.pallas_call(body, name="dtf_fwd", out_shape=[SDS((16, S), F32)] * 3, compiler_params=_cparams())(
        dtf_t, dt_bias, a_log, f_bias)


def _dtf_bwd(dtf_t, dt_bias, a_log, f_bias, d_dt, d_acs_a, d_acs_b, d_cum):
    S = dtf_t.shape[1]

    def body(x_ref, db_ref, al_ref, fb_ref, ddt_ref, da1_ref, da2_ref, dc_ref, dx_ref, ddb_ref, dal_ref, dfb_ref):
        xd = x_ref[0:16, :] + db_ref[...]
        dt = _softplus(xd)
        a = -jnp.exp(al_ref[...])
        d_da = _suffix_sum(da1_ref[...] + da2_ref[...], CHUNK)
        d_dt = ddt_ref[...] + d_da * a
        dal_ref[...] = jnp.sum(d_da * dt, axis=1, keepdims=True) * a
        d_xd = d_dt * jax.nn.sigmoid(xd)
        ddb_ref[...] = jnp.sum(d_xd, axis=1, keepdims=True)
        xf = x_ref[16:32, :] + fb_ref[...]
        d_xf = _suffix_sum(dc_ref[...], S) * jax.nn.sigmoid(-xf)
        dfb_ref[...] = jnp.sum(d_xf, axis=1, keepdims=True)
        dx_ref[0:16, :] = d_xd
        dx_ref[16:32, :] = d_xf

    return pl.pallas_call(body, name="dtf_bwd", out_shape=[SDS((32, S), F32)] + [SDS((16, 1), F32)] * 3,
                          compiler_params=_cparams())(dtf_t, dt_bias, a_log, f_bias, d_dt, d_acs_a, d_acs_b, d_cum)


SSM_PAIRS = SSM_HEADS // 2 // SSM_GROUPS


def _ssd_pair(xs, dtc, acol, arow, bm, cm, cbm, h, hp):
    L = CHUNK
    first = _iota((1, LANES), 1) < SSM_P
    i16, s16 = _iota((L, 16), 1), _iota((16, L), 0)
    ha, hb = 2 * hp, 2 * hp + 1

    def selc(blk, hh):
        return jnp.sum(jnp.where(i16 == hh, blk, 0.0), axis=1, keepdims=True)

    def selr(blk, hh):
        return jnp.sum(jnp.where(s16 == hh, blk, 0.0), axis=0, keepdims=True)

    x = xs * jnp.where(first, selc(dtc, ha), selc(dtc, hb))
    ca, cb, ra, rb = selc(acol, ha), selc(acol, hb), selr(arow, ha), selr(arow, hb)
    tri = _iota((L, L), 0) >= _iota((L, L), 1)
    la = jnp.exp(jnp.where(tri, ca - ra, NEG))
    lb = jnp.exp(jnp.where(tri, cb - rb, NEG))
    y = jnp.where(first, mm_nn(cbm * la, x), mm_nn(cbm * lb, x))
    y = y + jnp.where(first, jnp.exp(ca), jnp.exp(cb)) * mm_nn(cm, h)
    last = _iota((1, L), 1) == L - 1
    ala = jnp.sum(jnp.where(last, ra, 0.0), axis=1, keepdims=True)
    alb = jnp.sum(jnp.where(last, rb, 0.0), axis=1, keepdims=True)
    dec = jnp.where(first, jnp.exp(ala - ca), jnp.exp(alb - cb))
    hn = jnp.where(first, jnp.exp(ala), jnp.exp(alb)) * h + mm_tn(bm, x * dec)
    return y, hn


def _ssd_group(*args, grp):
    xs, (dtc, acol, arow, bm, cm), hs = args[:SSM_PAIRS], args[SSM_PAIRS:SSM_PAIRS + 5], args[SSM_PAIRS + 5:]
    cbm = mm_nt(cm, bm)
    res = [_ssd_pair(xs[j], dtc, acol, arow, bm, cm, cbm, hs[j], SSM_PAIRS * grp + j) for j in range(SSM_PAIRS)]
    return tuple(r[0] for r in res) + tuple(r[1] for r in res)


def _ssd_specs(nc, rev):
    L = CHUNK
    cidx = (lambda c: nc - 1 - c) if rev else (lambda c: c)
    return dict(
        xs=pl.BlockSpec((L, SSM_PAIRS * LANES), lambda c, g: (cidx(c), g)),
        col=pl.BlockSpec((L, 16), lambda c, g: (cidx(c), 0)),
        row=pl.BlockSpec((16, L), lambda c, g: (0, cidx(c))),
        b=pl.BlockSpec((L, SSM_N), lambda c, g: (cidx(c), g)),
        c=pl.BlockSpec((L, SSM_N), lambda c, g: (cidx(c), SSM_GROUPS + g)),
        st=pl.BlockSpec((1, SSM_PAIRS, SSM_N, LANES), lambda c, g: (cidx(c), g, 0, 0)),
    )


def _lane_pieces(v):
    return [v[:, LANES * j:LANES * (j + 1)] for j in range(v.shape[1] // LANES)]


def _ssd_fwd(xs, dt_col, acs_col, acs_row, bc):
    S = xs.shape[0]
    nc, nhp = S // CHUNK, SSM_HEADS // 2
    sp = _ssd_specs(nc, False)

    def body(xs_ref, dt_ref, ac_ref, ar_ref, b_ref, c_ref, y_ref, hs_ref, h_scr):
        c, g = pl.program_id(0), pl.program_id(1)

        @pl.when(c == 0)
        def _():
            for j in range(SSM_PAIRS):
                h_scr[SSM_PAIRS * g + j] = jnp.zeros((SSM_N, LANES), F32)

        hs = [h_scr[SSM_PAIRS * g + j] for j in range(SSM_PAIRS)]
        for j in range(SSM_PAIRS):
            hs_ref[0, j] = hs[j]
        res = _ssd_group(*_lane_pieces(xs_ref[...]), dt_ref[...], ac_ref[...], ar_ref[...], b_ref[...], c_ref[...], *hs,
                         grp=g)
        y_ref[...] = jnp.concatenate(res[:SSM_PAIRS], axis=1)
        for j in range(SSM_PAIRS):
            h_scr[SSM_PAIRS * g + j] = res[SSM_PAIRS + j]

    return pl.pallas_call(
        body, name="ssd_fwd", grid=(nc, SSM_GROUPS),
        in_specs=[sp["xs"], sp["col"], sp["col"], sp["row"], sp["b"], sp["c"]],
        out_specs=[sp["xs"], sp["st"]],
        out_shape=[SDS((S, SSM_HEADS * SSM_P), F32), SDS((nc, nhp, SSM_N, LANES), F32)],
        scratch_shapes=[pltpu.VMEM((nhp, SSM_N, LANES), F32)],
        compiler_params=_cparams(dimension_semantics=("arbitrary", "arbitrary")),
    )(xs, dt_col, acs_col, acs_row, bc, bc)


def _ssd_bwd(xs, dt_col, acs_col, acs_row, bc, hs, dy):
    S = xs.shape[0]
    nc, nhp = S // CHUNK, SSM_HEADS // 2
    sp = _ssd_specs(nc, True)

    def body(xs_ref, dt_ref, ac_ref, ar_ref, b_ref, c_ref, hs_ref, dy_ref,
             dxs_ref, ddt_ref, dac_ref, dar_ref, db_ref, dc_ref, dh_scr):
        c, g = pl.program_id(0), pl.program_id(1)

        @pl.when(c == 0)
        def _():
            for j in range(SSM_PAIRS):
                dh_scr[SSM_PAIRS * g + j] = jnp.zeros((SSM_N, LANES), F32)

        _, vjp = jax.vjp(functools.partial(_ssd_group, grp=g), *_lane_pieces(xs_ref[...]), dt_ref[...], ac_ref[...],
                         ar_ref[...], b_ref[...], c_ref[...], *[hs_ref[0, j] for j in range(SSM_PAIRS)])
        grads = vjp(tuple(_lane_pieces(dy_ref[...])) + tuple(dh_scr[SSM_PAIRS * g + j] for j in range(SSM_PAIRS)))
        dxs_ref[...] = jnp.concatenate(grads[:SSM_PAIRS], axis=1)
        ddt, dac, dar, db, dc = grads[SSM_PAIRS:SSM_PAIRS + 5]
        for j in range(SSM_PAIRS):
            dh_scr[SSM_PAIRS * g + j] = grads[SSM_PAIRS + 5 + j]
        db_ref[...] = db
        dc_ref[...] = dc

        @pl.when(g == 0)
        def _():
            ddt_ref[...] = ddt
            dac_ref[...] = dac
            dar_ref[...] = dar

        @pl.when(g > 0)
        def _():
            ddt_ref[...] += ddt
            dac_ref[...] += dac
            dar_ref[...] += dar

    return pl.pallas_call(
        body, name="ssd_bwd", grid=(nc, SSM_GROUPS),
        in_specs=[sp["xs"], sp["col"], sp["col"], sp["row"], sp["b"], sp["c"], sp["st"], sp["xs"]],
        out_specs=[sp["xs"], sp["col"], sp["col"], sp["row"], sp["b"], sp["b"]],
        out_shape=[SDS((S, SSM_HEADS * SSM_P), F32), SDS((S, 16), F32), SDS((S, 16), F32), SDS((16, S), F32),
                   SDS((S, SSM_GROUPS * SSM_N), F32), SDS((S, SSM_GROUPS * SSM_N), F32)],
        scratch_shapes=[pltpu.VMEM((nhp, SSM_N, LANES), F32)],
        compiler_params=_cparams(dimension_semantics=("arbitrary", "arbitrary")),
    )(xs, dt_col, acs_col, acs_row, bc, bc, hs, dy)


ATT_T = 1024


def _pick_col(blk, h):
    return jnp.sum(jnp.where(_iota(blk.shape, 1) == h, blk, 0.0), axis=1, keepdims=True)


def _pick_row(blk, h):
    return jnp.sum(jnp.where(_iota(blk.shape, 0) == h, blk, 0.0), axis=0, keepdims=True)


def _pair_norm(x, g2, first):
    x2 = x * x
    sa = jnp.sum(jnp.where(first, x2, 0.0), axis=1, keepdims=True)
    sb = jnp.sum(jnp.where(first, 0.0, x2), axis=1, keepdims=True)
    r = jnp.where(first, lax.rsqrt(sa * (1.0 / ATT_D) + EPS), lax.rsqrt(sb * (1.0 / ATT_D) + EPS))
    return x * r * g2, r


def _pair_norm_bwd(dxn, x, r, g2, first):
    t = dxn * g2
    tx = t * x
    ma = jnp.sum(jnp.where(first, tx, 0.0), axis=1, keepdims=True)
    mb = jnp.sum(jnp.where(first, 0.0, tx), axis=1, keepdims=True)
    dx = r * (t - x * (r * r) * (jnp.where(first, ma, mb) * (1.0 / ATT_D)))
    return dx, jnp.sum(dxn * x * r, axis=0, keepdims=True)


def _fox_fwd(src, q_c0, k_c0, v_c0, gq2, gk2, cum_col, cum_row3):
    S = src.shape[0]
    T = ATT_T
    nq, nhp = S // T, ATT_HEADS // 2
    qb0, kb0, vb0 = q_c0 // LANES, k_c0 // LANES, v_c0 // LANES
    scale = ATT_D ** -0.5

    def body(q_ref, kraw_ref, v_ref, gq_ref, gk_ref, cc_ref, cr_ref, o_ref, l_ref, k_ref):
        hp, i = pl.program_id(0), pl.program_id(1)
        first = _iota((1, LANES), 1) < ATT_D

        @pl.when(i == 0)
        def _():
            k_ref[...] = _pair_norm(kraw_ref[...], gk_ref[...], first)[0].astype(BF16)

        H = T // 2
        for half in range(2):
            rows = pl.ds(half * H, H)
            q = (_pair_norm(q_ref[rows, :], gq_ref[...], first)[0] * scale).astype(BF16)
            zero = jnp.zeros_like(q)
            qs = (jnp.where(first, q, zero), jnp.where(first, zero, q))
            cc = cc_ref[rows, :]
            cq = (_pick_col(cc, 2 * hp), _pick_col(cc, 2 * hp + 1))

            def attend(j, carry, klen, diagonal):
                off = pl.multiple_of(j * T, T)
                k = k_ref[pl.ds(off, klen), :]
                v = v_ref[pl.ds(off, klen), :].astype(BF16)
                cr = cr_ref[j]
                out = []
                for hh in range(2):
                    m, l, acc = carry[3 * hh:3 * hh + 3]
                    s = _bdot(qs[hh], k, _NT) + (cq[hh] - _pick_row(cr, 2 * hp + hh)[:, :klen])
                    if diagonal:
                        s = jnp.where(_iota((H, klen), 0) + half * H >= _iota((H, klen), 1), s, NEG)
                    m_new = jnp.maximum(m, jnp.max(s, axis=1, keepdims=True))
                    alpha = jnp.exp(m - m_new)
                    p = jnp.exp(s - m_new)
                    out += [m_new, alpha * l + jnp.sum(p, axis=1, keepdims=True), alpha * acc + _bdot(p, v, _NN)]
                return tuple(out)

            init = (jnp.full((H, 1), NEG, F32), jnp.zeros((H, 1), F32), jnp.zeros((H, LANES), F32)) * 2
            carry = lax.fori_loop(0, i, lambda j, c: attend(j, c, T, False), init)
            ma, la, acca, mb, lb, accb = attend(i, carry, (half + 1) * H, True)
            o_ref[rows, :] = jnp.where(first, acca / la, accb / lb).astype(o_ref.dtype)
            l_ref[rows, :] = jnp.where(first, ma + jnp.log(la), mb + jnp.log(lb))

    gain = pl.BlockSpec((1, LANES), lambda hp, i: (0, 0))
    return pl.pallas_call(
        body, name="fox_fwd", grid=(nhp, nq),
        in_specs=[pl.BlockSpec((T, LANES), lambda hp, i: (i, qb0 + hp)), pl.BlockSpec((S, LANES), lambda hp, i: (0, kb0 + hp)),
                  pl.BlockSpec((S, LANES), lambda hp, i: (0, vb0 + hp)), gain, gain,
                  pl.BlockSpec((T, 16), lambda hp, i: (i, 0)), pl.BlockSpec((nq, 16, T), lambda hp, i: (0, 0, 0))],
        out_specs=[pl.BlockSpec((T, LANES), lambda hp, i: (i, hp))] * 2,
        out_shape=[SDS((S, ATT_HEADS * ATT_D), BF16), SDS((S, ATT_HEADS * ATT_D), F32)],
        scratch_shapes=[pltpu.VMEM((S, LANES), BF16)],
        compiler_params=_cparams(dimension_semantics=("arbitrary", "arbitrary")),
    )(src, src, src, gq2, gk2, cum_col, cum_row3)


def _fox_bwd(src, q_c0, k_c0, v_c0, gq2, gk2, cum_col, cum_row3, lse, dsrc, d_c0):
    S = src.shape[0]
    T = ATT_T
    nq, nhp = S // T, ATT_HEADS // 2
    qb0, kb0, vb0, db0 = q_c0 // LANES, k_c0 // LANES, v_c0 // LANES, d_c0 // LANES
    scale = ATT_D ** -0.5

    def body(q_ref, kraw_ref, v_ref, gq_ref, gk_ref, cc_ref, cr_ref, l_ref, do_ref,
             dq_ref, dk_ref, dv_ref, dc_ref, dg_ref, k_ref, dk_acc, dv_acc, p_scr, dp_scr):
        hp, i = pl.program_id(0), pl.program_id(1)
        first = _iota((1, LANES), 1) < ATT_D

        @pl.when(i == 0)
        def _():
            k_ref[...] = _pair_norm(kraw_ref[...], gk_ref[...], first)[0].astype(BF16)
            dk_acc[...] = jnp.zeros_like(dk_acc)
            dv_acc[...] = jnp.zeros_like(dv_acc)
            dc_ref[...] = jnp.zeros_like(dc_ref)
            dg_ref[...] = jnp.zeros_like(dg_ref)

        H = T // 2
        for half in range(2):
            rows = pl.ds(half * H, H)
            dlen = (half + 1) * H
            q_raw = q_ref[rows, :]
            qn, rq = _pair_norm(q_raw, gq_ref[...], first)
            q = (qn * scale).astype(BF16)
            zq = jnp.zeros_like(q)
            dob = do_ref[rows, :].astype(BF16)
            zd = jnp.zeros_like(dob)
            lse_blk, cc = l_ref[rows, :], cc_ref[rows, :]
            dq = jnp.zeros((H, LANES), F32)
            for hh in range(2):
                sel = first if hh == 0 else jnp.logical_not(first)
                qh, doh = jnp.where(sel, q, zq), jnp.where(sel, dob, zd)
                bias_q = _pick_col(cc, 2 * hp + hh) - jnp.max(jnp.where(sel, lse_blk, NEG), axis=1, keepdims=True)

                def probs(j, delta, klen, diagonal):
                    off = pl.multiple_of(j * T, T)
                    s = _bdot(qh, k_ref[pl.ds(off, klen), :], _NT) + (bias_q - _pick_row(cr_ref[j], 2 * hp + hh)[:, :klen])
                    if diagonal:
                        s = jnp.where(_iota((H, klen), 0) + half * H >= _iota((H, klen), 1), s, NEG)
                    p = jnp.exp(s)
                    dp = _bdot(doh, v_ref[pl.ds(off, klen), :], _NT)
                    p_scr[j, :, 0:klen] = p
                    dp_scr[j, :, 0:klen] = dp
                    return delta + jnp.sum(p * dp, axis=1, keepdims=True)

                delta = lax.fori_loop(0, i, lambda j, d: probs(j, d, T, False), jnp.zeros((H, 1), F32))
                delta = probs(i, delta, dlen, True)

                def grads(j, dq, klen):
                    off = pl.multiple_of(j * T, T)
                    p = p_scr[j, :, 0:klen]
                    ds = p * (dp_scr[j, :, 0:klen] - delta)
                    dv_acc[pl.ds(off, klen), :] += _bdot(p, doh, _TN)
                    dk_acc[pl.ds(off, klen), :] += _bdot(ds, qh, _TN)
                    dc_ref[0, j, hh:hh + 1, 0:klen] -= jnp.sum(ds, axis=0, keepdims=True)
                    zk = jnp.zeros((klen, LANES), BF16)
                    return dq + _bdot(ds, jnp.where(sel, k_ref[pl.ds(off, klen), :], zk), _NN)

                dq = lax.fori_loop(0, i, lambda j, d: grads(j, d, T), dq)
                dq = grads(i, dq, dlen)
            dq_raw, dgq = _pair_norm_bwd(dq * scale, q_raw, rq, gq_ref[...], first)
            dq_ref[rows, :] = dq_raw.astype(dq_ref.dtype)
            dg_ref[0, 0:1, :] += dgq

        @pl.when(i == nq - 1)
        def _():
            k_raw = kraw_ref[...]
            rk = _pair_norm(k_raw, gk_ref[...], first)[1]
            dk_raw, dgk = _pair_norm_bwd(dk_acc[...], k_raw, rk, gk_ref[...], first)
            dk_ref[...] = dk_raw.astype(dk_ref.dtype)
            dv_ref[...] = dv_acc[...].astype(dv_ref.dtype)
            dg_ref[0, 1:2, :] = dgk

    gain = pl.BlockSpec((1, LANES), lambda hp, i: (0, 0))
    band = SDS((S, ATT_HEADS * ATT_D), BF16)
    return pl.pallas_call(
        body, name="fox_bwd", grid=(nhp, nq),
        in_specs=[pl.BlockSpec((T, LANES), lambda hp, i: (i, qb0 + hp)), pl.BlockSpec((S, LANES), lambda hp, i: (0, kb0 + hp)),
                  pl.BlockSpec((S, LANES), lambda hp, i: (0, vb0 + hp)), gain, gain,
                  pl.BlockSpec((T, 16), lambda hp, i: (i, 0)), pl.BlockSpec((nq, 16, T), lambda hp, i: (0, 0, 0)),
                  pl.BlockSpec((T, LANES), lambda hp, i: (i, hp)), pl.BlockSpec((T, LANES), lambda hp, i: (i, db0 + hp))],
        out_specs=[pl.BlockSpec((T, LANES), lambda hp, i: (i, hp)), pl.BlockSpec((S, LANES), lambda hp, i: (0, hp)),
                   pl.BlockSpec((S, LANES), lambda hp, i: (0, hp)), pl.BlockSpec((1, nq, 8, T), lambda hp, i: (hp, 0, 0, 0)),
                   pl.BlockSpec((1, 8, LANES), lambda hp, i: (hp, 0, 0))],
        out_shape=[band, band, band, SDS((nhp, nq, 8, T), F32), SDS((nhp, 8, LANES), F32)],
        scratch_shapes=[pltpu.VMEM((S, LANES), BF16), pltpu.VMEM((S, LANES), F32), pltpu.VMEM((S, LANES), F32),
                        pltpu.VMEM((nq, T // 2, T), F32), pltpu.VMEM((nq, T // 2, T), F32)],
        compiler_params=_cparams(dimension_semantics=("arbitrary", "arbitrary")),
    )(src, src, src, gq2, gk2, cum_col, cum_row3, lse, dsrc)


def _fold_gains(dg):
    def body(d_ref, o_ref):
        t = d_ref[0]
        for h in range(1, dg.shape[0]):
            t = t + d_ref[h]
        o_ref[...] = t + pltpu.roll(t, ATT_D, axis=1)

    return pl.pallas_call(body, name="fold_gains", out_shape=SDS(dg.shape[1:], F32), compiler_params=_cparams())(dg)


def _adamw_math(w, g, m, v):
    m = ADAM_B1 * m + (1.0 - ADAM_B1) * g
    v = ADAM_B2 * v + (1.0 - ADAM_B2) * jnp.square(g)
    m_hat = m / (1.0 - ADAM_B1 ** ADAM_STEP)
    v_hat = v / (1.0 - ADAM_B2 ** ADAM_STEP)
    delta = -ADAM_LR * (m_hat / (jnp.sqrt(v_hat) + ADAM_EPS) + ADAM_WD * w)
    return delta, m, v


def _reduce_adamw(parts, w, m, v, *, tr, name, tc=None):
    R, C = w.shape
    tr, tc = min(tr, R), tc or C
    nparts = parts.shape[0]

    def body(p_ref, w_ref, m_ref, v_ref, g_ref, d_ref, nm_ref, nv_ref):
        g = p_ref[0].astype(F32)
        for s in range(1, nparts):
            g = g + p_ref[s].astype(F32)
        g_ref[...] = g
        d_ref[...], nm_ref[...], nv_ref[...] = _adamw_math(w_ref[...], g, m_ref[...], v_ref[...])

    blk = pl.BlockSpec((tr, tc), lambda i, j: (i, j))
    return pl.pallas_call(
        body, name=name, grid=(R // tr, C // tc),
        in_specs=[pl.BlockSpec((nparts, tr, tc), lambda i, j: (0, i, j)), blk, blk, blk], out_specs=[blk] * 4,
        out_shape=[SDS((R, C), F32)] * 4, compiler_params=_cparams(dimension_semantics=("parallel", "parallel")),
    )(parts, w, m, v)


def _adamw(w, g, m, v, *, name):
    def body(w_ref, g_ref, m_ref, v_ref, d_ref, nm_ref, nv_ref):
        d_ref[...], nm_ref[...], nv_ref[...] = _adamw_math(w_ref[...], g_ref[...], m_ref[...], v_ref[...])

    return pl.pallas_call(body, name=name, out_shape=[SDS(w.shape, F32)] * 3, compiler_params=_cparams())(w, g, m, v)


def _peers():
    x, y, c = lax.axis_index("x"), lax.axis_index("y"), lax.axis_index("c")
    out = []
    for k in range(1, N_DEV):
        px, py, pc = x ^ ((k >> 2) & 1), y ^ ((k >> 1) & 1), c ^ (k & 1)
        out.append(((px, py, pc), 4 * px + 2 * py + pc))
    return 4 * x + 2 * y + c, out


_HBM = pl.BlockSpec(memory_space=pltpu.HBM)
_SEM = pl.BlockSpec(memory_space=pltpu.SEMAPHORE)
_DATAFLOW = pltpu.SideEffectType.DATAFLOW_SIDE_EFFECTING


NEAR = (1, 2, 4, 6)


def _plan_peers(scatter, ks=tuple(range(1, N_DEV))):
    return lambda me, peers: [(peers[k - 1][0], peers[k - 1][1] if scatter else None, me, k - 1) for k in ks]


def _plan_relay(me, peers):
    return [(peers[0][0], peers[k - 1][1], peers[k - 1][1], j) for j, k in enumerate((2, 4, 6))]


def _plan_pair(me, peers):
    return [(peers[0][0], peers[k - 1][1], j, j) for j, k in enumerate((1, 3, 5, 7))]


def _plan_chips(me, peers):
    return [(peers[k - 1][0], k // 2, k // 2, k // 2) for k in (2, 4, 6)]


def _copy(src, dst, c, send_sems, recv_sems):
    dev, s_slot, d_slot, i = c
    return pltpu.make_async_remote_copy(
        src_ref=src if s_slot is None else src.at[s_slot], dst_ref=dst.at[d_slot], send_sem=send_sems.at[i],
        recv_sem=recv_sems.at[i], device_id=dev, device_id_type=MESH)


def _copies_start(items, *, name):
    n = len(items)
    bufs = [it[0] for it in items] + [it[1] for it in items if it[1] is not None]
    nb = len(bufs)

    def body(*refs):
        srcs, extra, sems, token = refs[:n], iter(refs[n:nb]), refs[nb:nb + 2 * n], refs[-1]
        me, peers = _peers()
        for a, (_, land, plan) in enumerate(items):
            dst = srcs[a] if land is None else next(extra)
            for c in plan(me, peers):
                _copy(srcs[a], dst, c, sems[2 * a], sems[2 * a + 1]).start()
        token[...] = jnp.zeros_like(token)

    res = pl.pallas_call(
        body, name=name,
        out_shape=([pltpu.SemaphoreType.DMA((N_DEV - 1,))] * (2 * n) + [pltpu.HBM(b.shape, b.dtype) for b in bufs]
                   + [SDS((8, LANES), F32)]),
        in_specs=[_HBM] * nb, out_specs=[_SEM] * (2 * n) + [_HBM] * nb + [pl.BlockSpec(memory_space=pltpu.VMEM)],
        input_output_aliases={i: 2 * n + i for i in range(nb)},
        compiler_params=pltpu.CompilerParams(has_side_effects=_DATAFLOW),
    )(*[pltpu.with_memory_space_constraint(b, pltpu.HBM) for b in bufs])
    sems, thru, token = res[:2 * n], list(res[2 * n:2 * n + nb]), res[-1]
    extra = iter(thru[n:])
    return [(thru[a], None if it[1] is None else next(extra), sems[2 * a], sems[2 * a + 1], it[2])
            for a, it in enumerate(items)], token


def _copies_wait(handles, after, *, name):
    n = len(handles)
    bufs = [h[0] for h in handles] + [h[1] for h in handles if h[1] is not None]
    nb = len(bufs)

    def body(*refs):
        srcs, extra, sems = refs[:n], iter(refs[n:nb]), refs[nb:nb + 2 * n]
        me, peers = _peers()
        for a, h in enumerate(handles):
            dst = srcs[a] if h[1] is None else next(extra)
            for c in h[4](me, peers):
                cp = _copy(srcs[a], dst, c, sems[2 * a], sems[2 * a + 1])
                cp.wait_send()
                cp.wait_recv()

    flat_sems = [s for h in handles for s in (h[2], h[3])]
    res = pl.pallas_call(
        body, name=name, out_shape=[pltpu.HBM(b.shape, b.dtype) for b in bufs],
        in_specs=[_HBM] * nb + [_SEM] * (2 * n) + [pl.BlockSpec(memory_space=pl.ANY)], out_specs=[_HBM] * nb,
        input_output_aliases={i: i for i in range(nb)},
        compiler_params=pltpu.CompilerParams(has_side_effects=_DATAFLOW),
    )(*bufs, *flat_sems, after)
    extra = iter(res[n:])
    return [(res[a], res[a] if h[1] is None else next(extra)) for a, h in enumerate(handles)]


def _exchange_start(arrays, *, scatter, name, near=()):
    items = []
    for a, arr in enumerate(arrays):
        land = lax.empty(arr.shape if scatter else (N_DEV,) + arr.shape, arr.dtype)
        items.append((arr, land, _plan_peers(scatter, NEAR) if a in near else _plan_peers(scatter)))
    return _copies_start(items, name=name)


MOVE_ROWS, MOVE_SLOTS = 512, 3


def _move_rows(src, moves, rows, *, name):
    C = src.shape[1]
    covered = max(dst + n for _, n, dst in moves)
    tail = rows - covered
    assert sum(n for _, n, _ in moves) == covered
    chunks = [(lo + o, min(MOVE_ROWS, n - o), dst + o) for lo, n, dst in moves for o in range(0, n, MOVE_ROWS)]
    nch = len(chunks)

    def body(src_ref, o_ref, buf, sin, sout, *zero):
        def fetch(i):
            lo, n, _ = chunks[i]
            return pltpu.make_async_copy(src_ref.at[pl.ds(lo, n)], buf.at[i % MOVE_SLOTS, pl.ds(0, n)], sin.at[i % MOVE_SLOTS])

        def store(i):
            _, n, dst = chunks[i]
            return pltpu.make_async_copy(buf.at[i % MOVE_SLOTS, pl.ds(0, n)], o_ref.at[pl.ds(dst, n)], sout.at[i % MOVE_SLOTS])

        if tail:
            zero[0][...] = jnp.zeros_like(zero[0])
            fill = pltpu.make_async_copy(zero[0], o_ref.at[pl.ds(covered, tail)], zero[1])
            fill.start()
        for i in range(nch):
            if i >= MOVE_SLOTS:
                store(i - MOVE_SLOTS).wait()
            fetch(i).start()
            if i >= 1:
                fetch(i - 1).wait()
                store(i - 1).start()
        fetch(nch - 1).wait()
        store(nch - 1).start()
        for i in range(max(0, nch - MOVE_SLOTS), nch):
            store(i).wait()
        if tail:
            fill.wait()

    anyspec = pl.BlockSpec(memory_space=pl.ANY)
    dma = pltpu.SemaphoreType.DMA
    return pl.pallas_call(
        body, name=name, in_specs=[anyspec], out_specs=anyspec, out_shape=SDS((rows, C), src.dtype),
        scratch_shapes=([pltpu.VMEM((MOVE_SLOTS, MOVE_ROWS, C), src.dtype), dma((MOVE_SLOTS,)), dma((MOVE_SLOTS,))]
                        + ([pltpu.VMEM((tail, C), src.dtype), dma] if tail else [])),
        compiler_params=_cparams())(src)


def _pair_sum(a, b, *, name):
    n, R, C = a.shape
    tc = 256

    def body(a_ref, b_ref, o_ref):
        o_ref[...] = (a_ref[...].astype(F32) + b_ref[...].astype(F32)).astype(o_ref.dtype)

    blk = pl.BlockSpec((1, R, tc), lambda i, j: (i, 0, j))
    return pl.pallas_call(body, name=name, grid=(n, C // tc), in_specs=[blk, blk], out_specs=blk,
                          out_shape=SDS(a.shape, a.dtype), compiler_params=_cparams(dimension_semantics=("parallel", "parallel")))(a, b)


def _own_slot(landed, own, me):
    return lax.dynamic_update_slice(landed, own[None], (me,) + (0,) * own.ndim)


SMALL = (("g_mix", 1024), ("conv_w", 6144), ("conv_b", 1536), ("dt_bias", 16), ("a_log", 16), ("d_skip", 16),
         ("ssm_norm_w", 1024), ("g_q", 64), ("g_k", 64), ("f_bias", 16), ("g_xattn", 1024), ("g_mem", 1024),
         ("xg_q", 256), ("xg_k", 256), ("g_mlp", 1024))
SLAB_ROWS = 112
BIG = ("w_in", "w_out", "xq_w", "xkv_w", "xo_w", "w_up", "w_down")
WEIGHTS = ("g_mix", "w_in", "conv_w", "conv_b", "dt_bias", "a_log", "d_skip", "ssm_norm_w", "g_q", "g_k", "f_bias", "w_out",
           "g_xattn", "g_mem", "xq_w", "xkv_w", "xg_q", "xg_k", "xo_w", "g_mlp", "w_up", "w_down")
O_Z, O_XS, O_B, O_C, O_DT, O_Q, O_K, O_V, O_F, O_END = 0, 1024, 2048, 2304, 2560, 2576, 3600, 4624, 5648, 5664
IN_ROW_MOVES = ((O_Z, O_B - O_Z, C_Z), (O_Q, O_F - O_Q, C_Q), (O_B, O_Q - O_B, C_B), (O_F, O_END - O_F, C_DTF + 16))


def _pack_small(vals):
    rows = []
    for name, size in SMALL:
        flat = vals[name].reshape(-1).astype(F32)
        pad = -size % LANES
        rows.append(jnp.pad(flat, (0, pad)).reshape(-1, LANES))
    slab = jnp.concatenate(rows, axis=0)
    return jnp.pad(slab, ((0, SLAB_ROWS - slab.shape[0]), (0, 0)))


def _unpack_small(slab):
    out, r = {}, 0
    for name, size in SMALL:
        nr = -(-size // LANES)
        out[name] = slab[r:r + nr].reshape(-1)[:size]
        r += nr
    return out


def _step(p, m, v, x, mem, target):
    S = x.shape[0]
    TM = 256
    me = 4 * lax.axis_index("x") + 2 * lax.axis_index("y") + lax.axis_index("c")

    def rms(u, g, name):
        return _rw_fwd(_rms_fn, [_whole(u)], [_whole(g)], [(D_MODEL, BF16)], tm=TM, name=name)[0]

    def pin(param, token):
        return param + token[0:1, 0:1]

    def landed_with_own(pairs, scatter):
        out = []
        for src, land in pairs:
            own = lax.dynamic_index_in_dim(src, me, 0, keepdims=False) if scatter else src
            out.append(_own_slot(land, own, me))
        return out

    w_in_own, m_in_own, v_in_own = p["w_in"].T, m["w_in"].T, v["w_in"].T
    ag, ag_token = _exchange_start([w_in_own.astype(BF16), p["conv_w"]] + [p[n].astype(BF16) for n in BIG[1:]],
                                   scatter=False, name="allgather_start", near=(0,))
    h1 = rms(x, pin(p["g_mix"], ag_token), "rms_mix")
    (win_src, win_land), convw_pair = _copies_wait(ag[:2], h1, name="allgather_wait_in")
    relay, token = _copies_start([(win_land, None, _plan_relay)], name="allgather_relay_start")
    win_land = _copies_wait(relay, token, name="allgather_relay_wait")[0][1]
    win_g, convw_g = landed_with_own([(win_src, win_land), convw_pair], False)
    w_in_o = win_g.reshape(O_END, D_MODEL)
    w_in_t = _move_rows(w_in_o, IN_ROW_MOVES, P_COLS, name="w_in_rows")
    conv_w = convw_g.transpose(1, 0, 2).reshape(4, 1536)
    cw_xs, cw_bc = conv_w[:, :1024], conv_w[:, 1024:]
    cb_xs, cb_bc = p["conv_b"][:, :1024], p["conv_b"][:, 1024:]
    dt_bias, a_log, f_bias = p["dt_bias"].reshape(16, 1), p["a_log"].reshape(16, 1), p["f_bias"].reshape(16, 1)

    proj = _matmul(h1, w_in_t, mode="nt", tm=1024, tn=640, tk=1024, name="mm_in")
    xs_c = _conv_fwd(proj, C_XS, 1024, cw_xs, cb_xs, name="conv_xs")
    bc_c = _conv_fwd(proj, C_B, 512, cw_bc, cb_bc, name="conv_bc")
    dtf_t = proj[:, C_DTF:C_DTF + 32].T
    dt_t, acs_t, cum_t = _dtf_fwd(dtf_t, dt_bias, a_log, f_bias)
    dt_col, acs_col, cum_col = dt_t.T, acs_t.T, cum_t.T
    cum_row3 = cum_t.reshape(16, S // ATT_T, ATT_T).transpose(1, 0, 2)
    y_ssd, hs = _ssd_fwd(xs_c, dt_col, acs_col, acs_t, bc_c)
    gate_rows = [_whole(y_ssd), _whole(xs_c), (proj, C_Z, 1024)]
    gate_pars = [_whole(p["d_skip"]), _whole(p["ssm_norm_w"])]
    y_ssm = _rw_fwd(_gate_fn, gate_rows, gate_pars, [(1024, BF16)], tm=TM, name="gate")[0]
    gq2, gk2 = jnp.tile(p["g_q"], (1, 2)), jnp.tile(p["g_k"], (1, 2))
    o, lse = _fox_fwd(proj, C_Q, C_K, C_V, gq2, gk2, cum_col, cum_row3)
    mixed = jnp.concatenate([y_ssm, o], axis=1)
    wout_g, = landed_with_own(_copies_wait(ag[2:3], mixed, name="allgather_wait_out"), False)
    w_out = wout_g.reshape(2 * D_MODEL, D_MODEL)
    x1, h2 = _matmul(mixed, w_out, mode="nn", tm=1024, tn=1024, tk=2048, name="mm_out", extras=(x,),
                     row_params=(p["g_xattn"],), epilogue=_ep_residual_rms, out_dtypes=[F32, BF16])
    xq_g, xkv_w, xo_g, w_up, wdown_g = landed_with_own(_copies_wait(ag[3:], x1, name="allgather_wait_rest"), False)
    xq_w = xq_g.reshape(D_MODEL, D_MODEL)
    xo_w = xo_g.reshape(D_MODEL, D_MODEL)
    w_down = wdown_g.reshape(4 * D_MODEL, D_MODEL)

    mem_n = rms(mem, p["g_mem"], "rms_mem")
    q2 = _matmul(h2, xq_w, mode="nn", tm=1024, tn=512, tk=1024, name="mm_xq")
    kv = _matmul(mem_n, xkv_w, mode="nn", b_shards=True, tm=256, tn=256, tk=1024, name="mm_xkv")
    xa_rows = [(q2, X_D * h, X_D) for h in range(X_HEADS)]
    xa_pars = ([(kv, X_D * h, X_D) for h in range(X_HEADS)] + [(kv, D_MODEL + X_D * h, X_D) for h in range(X_HEADS)]
               + [_whole(p["xg_q"]), _whole(p["xg_k"])])
    o2 = _rw_fwd(_xattn_fn, xa_rows, xa_pars, [(D_MODEL, BF16)], tm=TM, name="xattn")[0]
    x2, h3 = _matmul(o2, xo_w, mode="nn", tm=1024, tn=1024, tk=1024, name="mm_xo", extras=(x1,),
                     row_params=(p["g_mlp"],), epilogue=_ep_residual_rms, out_dtypes=[F32, BF16])

    a, usq = _matmul(h3, w_up, mode="nn", b_shards=True, tm=2048, tn=512, tk=1024, name="mm_up", out_dtypes=[F32, BF16],
                     epilogue=lambda acc: (acc, jnp.square(jax.nn.relu(acc))))
    dy, loss_part = _matmul(usq, w_down, mode="nn", tm=1024, tn=512, tk=2048, name="mm_down", extras=(x2, target),
                            epilogue=functools.partial(_ep_loss, width=D_MODEL), sums=[(1, 1)])
    loss = lax.psum(loss_part[0, 0], ("x", "y", "c"))

    def row_shards(a):
        r, c = a.shape
        return a.reshape(N_DEV, r // N_DEV, c)

    g = {}
    g["w_down"] = _matmul(usq, dy, mode="tn", out_dtype=GRAD_WIRE, tm=1024, tn=1024, tk=1024, name="mm_d_wdown")
    da = _matmul(dy, w_down, mode="nt", tm=1024, tn=1024, tk=1024, name="mm_d_usq", out_dtype=BF16, extras=(a,),
                 epilogue=lambda acc, av: (2.0 * jax.nn.relu(av) * acc,))
    g["w_up"] = _matmul(h3, da, mode="tn", out_shards=True, out_dtype=GRAD_WIRE, tm=1024, tn=512, tk=1024, name="mm_d_wup")
    sent_mlp, token = _exchange_start([row_shards(g["w_down"]), g["w_up"]], scatter=True,
                                      name="grads_start_mlp")
    dx2, g["g_mlp"] = _matmul(da, w_up, mode="nt", b_shards=True, tm=1024, tn=1024, tk=512, name="mm_d_h3",
                              extras=(x2, dy), row_params=(pin(p["g_mlp"], token),), epilogue=_ep_rms_bwd,
                              sums=[(1, D_MODEL)])

    g["xo_w"] = _matmul(o2, dx2, mode="tn", out_dtype=GRAD_WIRE, tm=1024, tn=1024, tk=1024, name="mm_d_wxo")
    do2 = _matmul(dx2, xo_w, mode="nt", tm=1024, tn=512, tk=1024, name="mm_d_o2")
    xa = _rw_bwd(_xattn_fn, xa_rows, xa_pars, [_whole(do2)], tm=TM, name="xattn_bwd", row_grads=[BF16] * X_HEADS)
    dq2 = jnp.concatenate(xa[:X_HEADS], axis=1)
    dkv = jnp.concatenate(xa[X_HEADS:3 * X_HEADS], axis=1)
    g["xg_q"], g["xg_k"] = xa[3 * X_HEADS], xa[3 * X_HEADS + 1]
    g["xq_w"] = _matmul(h2, dq2, mode="tn", out_dtype=GRAD_WIRE, tm=1024, tn=1024, tk=1024, name="mm_d_wxq")
    dx1, g["g_xattn"] = _matmul(dq2, xq_w, mode="nt", tm=1024, tn=1024, tk=1024, name="mm_d_h2", extras=(x1, dx2),
                                row_params=(p["g_xattn"],), epilogue=_ep_rms_bwd, sums=[(1, D_MODEL)])
    g["xkv_w"] = _matmul(mem_n, dkv, mode="tn", out_shards=True, out_dtype=GRAD_WIRE, tm=1024, tn=256, tk=256,
                         name="mm_d_wxkv")
    dmem_n = _matmul(dkv, xkv_w, mode="nt", b_shards=True, tm=256, tn=1024, tk=256, name="mm_d_memn")
    g["g_mem"] = _rw_bwd(_rms_fn, [_whole(mem)], [_whole(p["g_mem"])], [_whole(dmem_n)], tm=TM, name="rms_mem_bwd",
                         row_grads=[None])[0]

    g["w_out"] = _matmul(mixed, dx1, mode="tn", out_dtype=GRAD_WIRE, tm=1024, tn=1024, tk=1024, name="mm_d_wout")
    sent_mid, token = _exchange_start(
        [row_shards(g["w_out"]), row_shards(g["xq_w"]), g["xkv_w"], row_shards(g["xo_w"])], scatter=True,
        name="grads_start_mid")
    dmixed = _matmul(dx1, w_out, mode="nt", tm=1024, tn=1024, tk=1024, name="mm_d_mixed")
    dq, dk, dv, dcum4, dgain = _fox_bwd(proj, C_Q, C_K, C_V, pin(gq2, token), gk2, cum_col, cum_row3, lse, dmixed, 1024)
    gains = _fold_gains(dgain)
    g["g_q"], g["g_k"] = gains[0:1, :ATT_D], gains[1:2, :ATT_D]
    dy_ssd, dxs_g, dz, g["d_skip"], g["ssm_norm_w"] = _rw_bwd(
        _gate_fn, gate_rows, gate_pars, [(dmixed, 0, 1024)], tm=TM, name="gate_bwd", row_grads=[F32, F32, BF16])
    dxs_s, ddt_col, dacs_col, dacs_row, d_b, d_c = _ssd_bwd(xs_c, dt_col, acs_col, acs_t, bc_c, hs, dy_ssd)
    dcum_t = dcum4[:, :, 0:2, :].transpose(0, 2, 1, 3).reshape(16, S)
    ddtf_t, ddtb, dalog, dfb = _dtf_bwd(dtf_t, dt_bias, a_log, f_bias, ddt_col.T, dacs_col.T, dacs_row, dcum_t)
    g["dt_bias"], g["a_log"], g["f_bias"] = ddtb, dalog, dfb
    dxs_raw, dcw_xs, dcb_xs = _conv_bwd(proj, C_XS, 1024, cw_xs, cb_xs, [dxs_s, dxs_g], name="conv_xs_bwd")
    dbc_raw, dcw_bc, dcb_bc = _conv_bwd(proj, C_B, 512, cw_bc, cb_bc, [jnp.concatenate([d_b, d_c], axis=1)],
                                        name="conv_bc_bwd")
    g["conv_w"] = jnp.concatenate([dcw_xs, dcw_bc], axis=1)
    g["conv_b"] = jnp.concatenate([dcb_xs, dcb_bc], axis=1)
    ddtf = jnp.pad(ddtf_t.T.astype(BF16), ((0, 0), (0, P_COLS - C_DTF - 32)))
    dproj = jnp.concatenate([dz, dxs_raw, dq, dk, dv, dbc_raw, ddtf], axis=1)
    dw_in_p = _matmul(dproj, h1, mode="tn", out_dtype=GRAD_WIRE, tm=640, tn=1024, tk=1024, name="mm_d_win")
    g["w_in"] = _move_rows(dw_in_p, [(dst, n, lo) for lo, n, dst in IN_ROW_MOVES], O_END, name="d_w_in_rows")
    half = N_DEV // 2
    send_in = row_shards(g["w_in"])
    pair, token = _copies_start([(send_in, lax.empty((half,) + send_in.shape[1:], send_in.dtype), _plan_pair)],
                                name="grads_in_pair_start")
    send_in, from_sibling = _copies_wait(pair, token, name="grads_in_pair_wait")[0]
    mine = jnp.stack([lax.dynamic_index_in_dim(send_in, me ^ (2 * j), 0, keepdims=False) for j in range(half)])
    chip_sums = _pair_sum(mine, from_sibling, name="grads_in_pair_sum")
    sent_in, token = _copies_start([(chip_sums, lax.empty(chip_sums.shape, chip_sums.dtype), _plan_chips)],
                                   name="grads_in_chip_start")
    grad_x, g["g_mix"] = _matmul(dproj, w_in_t, mode="nn", tm=1024, tn=1024, tk=1152, name="mm_d_h1", extras=(x, dx1),
                                 row_params=(pin(p["g_mix"], token),), epilogue=_ep_rms_bwd, sums=[(1, D_MODEL)])
    sent_small, _ = _exchange_start([_pack_small(g)], scatter=False, name="small_grads_start")

    grads, delta, new_m, new_v = {}, {}, {}, {}

    def update(names, sent, after, wait_name):
        parts = landed_with_own(_copies_wait(sent, after, name=wait_name), True)
        for name, part in zip(names, parts, strict=True):
            grads[name], delta[name], new_m[name], new_v[name] = _reduce_adamw(part, p[name], m[name], v[name], tr=128,
                                                                                name="adamw_" + name)

    update(("w_down", "w_up"), sent_mlp, grad_x, "grads_wait_mlp")
    update(("w_out", "xq_w", "xkv_w", "xo_w"), sent_mid, delta["w_up"], "grads_wait_mid")
    chip_sums, landed = _copies_wait(sent_in, delta["xo_w"], name="grads_in_chip_wait")[0]
    part = lax.dynamic_update_slice(landed, chip_sums[0:1], (0, 0, 0))
    res = _reduce_adamw(part, w_in_own, m_in_own, v_in_own, tr=part.shape[1], tc=256, name="adamw_w_in")
    grads["w_in"], delta["w_in"], new_m["w_in"], new_v["w_in"] = [r.T for r in res]
    small_parts = landed_with_own(_copies_wait(sent_small, delta["w_in"], name="small_grads_wait"), False)[0]
    zeros_cw = jnp.zeros((4, 1536), F32)
    slabs = [_pack_small({**d, "conv_w": zeros_cw}) for d in (p, m, v)]
    sg, sd, sm, sv = _reduce_adamw(small_parts, *slabs, tr=SLAB_ROWS, name="adamw_small")
    for dst, slab in ((grads, sg), (delta, sd), (new_m, sm), (new_v, sv)):
        for name, flat in _unpack_small(slab).items():
            if name != "conv_w":
                dst[name] = flat.reshape(p[name].shape)
    cw_shard = p["conv_w"].shape[1]
    grads["conv_w"] = lax.dynamic_slice(_unpack_small(sg)["conv_w"].reshape(4, 1536), (0, me * cw_shard), (4, cw_shard))
    delta["conv_w"], new_m["conv_w"], new_v["conv_w"] = _adamw(p["conv_w"], grads["conv_w"], m["conv_w"], v["conv_w"],
                                                               name="adamw_conv_w")
    return loss, grad_x, grads, delta, new_m, new_v


def kernel(x, mem, g_mix, w_in, conv_w, conv_b, dt_bias, a_log, d_skip, ssm_norm_w, g_q, g_k, f_bias, w_out, g_xattn, g_mem, xq_w, xkv_w, xg_q, xg_k, xo_w, g_mlp, w_up, w_down, loss_target, m_g_mix, m_w_in, m_conv_w, m_conv_b, m_dt_bias, m_a_log, m_d_skip, m_ssm_norm_w, m_g_q, m_g_k, m_f_bias, m_w_out, m_g_xattn, m_g_mem, m_xq_w, m_xkv_w, m_xg_q, m_xg_k, m_xo_w, m_g_mlp, m_w_up, m_w_down, v_g_mix, v_w_in, v_conv_w, v_conv_b, v_dt_bias, v_a_log, v_d_skip, v_ssm_norm_w, v_g_q, v_g_k, v_f_bias, v_w_out, v_g_xattn, v_g_mem, v_xq_w, v_xkv_w, v_xg_q, v_xg_k, v_xo_w, v_g_mlp, v_w_up, v_w_down):
    args = locals()
    drop = lambda t: t[0] if t.ndim == 3 else t
    p = {n: drop(args[n]) for n in WEIGHTS}
    m = {n: drop(args["m_" + n]) for n in WEIGHTS}
    v = {n: drop(args["v_" + n]) for n in WEIGHTS}
    loss, grad_x, grads, delta, new_m, new_v = _step(p, m, v, x[0], mem[0], loss_target[0])
    outs = [loss, grad_x[None]]
    for d in (grads, delta, new_m, new_v):
        outs += [d[n].reshape(args[n].shape) for n in WEIGHTS]
    return tuple(outs)
```

```python
import functools
import math

import jax
import jax.numpy as jnp
from jax import lax
from jax.experimental import pallas as pl
from jax.experimental.pallas import tpu as pltpu

F32, BF16 = jnp.float32, jnp.bfloat16
SDS = jax.ShapeDtypeStruct
HI = lax.Precision.HIGHEST
MESH = pl.DeviceIdType.MESH

N_DEV = 8
EPS = 1e-5
D_MODEL = 1024
SSM_HEADS, SSM_P, SSM_N, SSM_GROUPS, CHUNK = 16, 64, 128, 2, 128
ATT_HEADS, ATT_D = 16, 64
X_HEADS, X_D = 4, 256
LANES = 128
VMEM_LIMIT = 48 * 1024 * 1024
NEG = -1e30

GRAD_WIRE = BF16
ADAM_LR, ADAM_B1, ADAM_B2, ADAM_EPS, ADAM_WD, ADAM_STEP = 0.001, 0.9, 0.999, 1e-08, 0.01, 10

C_Z, C_XS, C_Q, C_K, C_V, C_B, C_C, C_DTF, P_COLS = 0, 1024, 2048, 3072, 4096, 5120, 5376, 5632, 5760

_NN = (((1,), (0,)), ((), ()))
_NT = (((1,), (1,)), ((), ()))
_TN = (((0,), (0,)), ((), ()))


def _cparams(**kw):
    return pltpu.CompilerParams(vmem_limit_bytes=VMEM_LIMIT, **kw)


def _bdot(a, b, dn):
    return lax.dot_general(a.astype(BF16), b.astype(BF16), dn, preferred_element_type=F32)


@jax.custom_vjp
def mm_nn(a, b):
    return _bdot(a, b, _NN)


mm_nn.defvjp(lambda a, b: (mm_nn(a, b), (a, b)), lambda r, g: (_bdot(g, r[1], _NT), _bdot(r[0], g, _TN)))


@jax.custom_vjp
def mm_nt(a, b):
    return _bdot(a, b, _NT)


mm_nt.defvjp(lambda a, b: (mm_nt(a, b), (a, b)), lambda r, g: (_bdot(g, r[1], _NN), _bdot(g, r[0], _TN)))


@jax.custom_vjp
def mm_tn(a, b):
    return _bdot(a, b, _TN)


mm_tn.defvjp(lambda a, b: (mm_tn(a, b), (a, b)), lambda r, g: (_bdot(r[1], g, _NT), _bdot(r[0], g, _NN)))


def _cdot(x, c):
    return jnp.dot(x, c, precision=HI, preferred_element_type=F32)


def _iota(shape, dim):
    return lax.broadcasted_iota(jnp.int32, shape, dim)


def _matmul(a, b, *, mode, tm, tn, tk, name, out_dtype=F32, add=None, extras=(), epilogue=None, out_dtypes=None,
            b_shards=False, out_shards=False, row_params=(), sums=()):
    if mode == "tn":
        K, M = a.shape
    else:
        M, K = a.shape
    if b_shards:
        N = b.shape[1] if mode == "nt" else b.shape[0] * b.shape[2]
        tn, tk = (tn, b.shape[2]) if mode == "nt" else (b.shape[2], tk)
    else:
        N = b.shape[0] if mode == "nt" else b.shape[1]
    tm, tn, tk = min(tm, M), min(tn, N), min(tk, K)
    assert M % tm == 0 and N % tn == 0 and K % tk == 0, (name, M, N, K, tm, tn, tk)
    assert not b_shards or (K // tk if mode == "nt" else N // tn) == b.shape[0], name
    assert not (out_shards and (extras or add is not None)), name
    nk = K // tk
    dn = {"nn": _NN, "nt": _NT, "tn": _TN}[mode]
    if add is not None:
        extras, epilogue = (add,), lambda acc, r: (acc + r,)
    elif epilogue is None:
        epilogue = lambda acc: (acc,)
    out_dtypes = out_dtypes or [out_dtype]
    ne, no, ns = len(extras) + len(row_params), len(out_dtypes), len(sums)
    assert all(s == (1, 1) or (s == (1, N) and tn == N) for s in sums), name

    def body(*refs):
        a_ref, b_ref = refs[:2]
        e_refs, o_refs, s_refs = refs[2:2 + ne], refs[2 + ne:2 + ne + no], refs[2 + ne + no:2 + ne + no + ns]

        def finish(acc):
            res = epilogue(acc, *[e[...] for e in e_refs])
            for o_ref, v in zip(o_refs, res[:no], strict=True):
                o_ref[...] = v.astype(o_ref.dtype)
            first_tile = jnp.logical_and(pl.program_id(0) == 0, pl.program_id(1) == 0)
            for s_ref, v in zip(s_refs, res[no:], strict=True):
                @pl.when(first_tile)
                def _(s_ref=s_ref, v=v):
                    s_ref[...] = v

                @pl.when(jnp.logical_not(first_tile))
                def _(s_ref=s_ref, v=v):
                    s_ref[...] += v

        prod = _bdot(a_ref[...], b_ref[...], dn)
        if nk == 1:
            finish(prod)
            return
        acc_ref = refs[-1]
        k = pl.program_id(2)

        @pl.when(k == 0)
        def _():
            acc_ref[...] = prod

        @pl.when(jnp.logical_and(k > 0, k < nk - 1))
        def _():
            acc_ref[...] += prod

        @pl.when(k == nk - 1)
        def _():
            finish(acc_ref[...] + prod)

    a_spec = pl.BlockSpec((tk, tm), lambda i, j, k: (k, i)) if mode == "tn" else pl.BlockSpec((tm, tk), lambda i, j, k: (i, k))
    if b_shards and mode == "nt":
        b_spec = pl.BlockSpec((None, tn, tk), lambda i, j, k: (k, j, 0))
    elif b_shards:
        b_spec = pl.BlockSpec((None, tk, tn), lambda i, j, k: (j, k, 0))
    elif mode == "nt":
        b_spec = pl.BlockSpec((tn, tk), lambda i, j, k: (j, k))
    else:
        b_spec = pl.BlockSpec((tk, tn), lambda i, j, k: (k, j))
    if out_shards:
        o_spec, o_shape = pl.BlockSpec((None, tm, tn), lambda i, j, k: (j, i, 0)), (N // tn, M, tn)
    else:
        o_spec, o_shape = pl.BlockSpec((tm, tn), lambda i, j, k: (i, j)), (M, N)
    row_spec = pl.BlockSpec((1, tn), lambda i, j, k: (0, j))
    sum_specs = [pl.BlockSpec(s, lambda i, j, k: (0, 0)) for s in sums]
    res = pl.pallas_call(
        body, name=name, grid=(M // tm, N // tn, nk),
        in_specs=[a_spec, b_spec] + [o_spec] * len(extras) + [row_spec] * len(row_params),
        out_specs=[o_spec] * no + sum_specs, out_shape=[SDS(o_shape, dt) for dt in out_dtypes] + [SDS(s, F32) for s in sums],
        scratch_shapes=[pltpu.VMEM((tm, tn), F32)] if nk > 1 else [],
        compiler_params=_cparams(dimension_semantics=(("arbitrary",) * 3 if sums else ("parallel", "parallel", "arbitrary"))),
    )(a, b, *extras, *row_params)
    return res[0] if no + ns == 1 else res


def _row_spec(tm, spec):
    _, c0, w = spec
    assert c0 % w == 0
    return pl.BlockSpec((tm, w), functools.partial(lambda i, cb: (i, cb), cb=c0 // w))


def _par_spec(spec):
    arr, c0, w = spec
    assert c0 % w == 0
    return pl.BlockSpec((arr.shape[0], w), functools.partial(lambda i, cb: (0, cb), cb=c0 // w))


def _whole(arr):
    return (arr, 0, arr.shape[1])


def _rw_fwd(fn, rows, params, outs, *, tm, name):
    M = rows[0][0].shape[0]
    nr, npar = len(rows), len(params)

    def body(*refs):
        rv = [r[...].astype(F32) for r in refs[:nr]]
        pv = [p[...].astype(F32) for p in refs[nr:nr + npar]]
        res = fn(*rv, *pv)
        for o_ref, v in zip(refs[nr + npar:], res, strict=True):
            o_ref[...] = v.astype(o_ref.dtype)

    return pl.pallas_call(
        body, name=name, grid=(M // tm,),
        in_specs=[_row_spec(tm, r) for r in rows] + [_par_spec(p) for p in params],
        out_specs=[pl.BlockSpec((tm, w), lambda i: (i, 0)) for w, _ in outs],
        out_shape=[SDS((M, w), dt) for w, dt in outs],
        compiler_params=_cparams(dimension_semantics=("parallel",)),
    )(*[r[0] for r in rows], *[p[0] for p in params])


def _rw_bwd(fn, rows, params, cts, *, tm, name, row_grads, adds=None):
    M = rows[0][0].shape[0]
    adds = adds or {}
    nr, npar, nc = len(rows), len(params), len(cts)
    add_keys = sorted(adds)
    want = [k for k in range(nr) if row_grads[k] is not None]

    def body(*refs):
        pos = 0
        r_refs = refs[pos:pos + nr]; pos += nr
        p_refs = refs[pos:pos + npar]; pos += npar
        c_refs = refs[pos:pos + nc]; pos += nc
        a_refs = dict(zip(add_keys, refs[pos:pos + len(add_keys)])); pos += len(add_keys)
        dr_refs = dict(zip(want, refs[pos:pos + len(want)])); pos += len(want)
        dp_refs = refs[pos:pos + npar]
        rv = [r[...].astype(F32) for r in r_refs]
        pv = [p[...].astype(F32) for p in p_refs]
        _, vjp = jax.vjp(fn, *rv, *pv)
        g = vjp(tuple(c[...].astype(F32) for c in c_refs))
        for k in want:
            v = g[k]
            if k in a_refs:
                v = v + a_refs[k][...].astype(F32)
            dr_refs[k][...] = v.astype(dr_refs[k].dtype)
        first = pl.program_id(0) == 0
        for j in range(npar):
            @pl.when(first)
            def _(j=j):
                dp_refs[j][...] = jnp.zeros_like(dp_refs[j])
            dp_refs[j][...] += g[nr + j]

    res = pl.pallas_call(
        body, name=name, grid=(M // tm,),
        in_specs=([_row_spec(tm, r) for r in rows] + [_par_spec(p) for p in params] + [_row_spec(tm, c) for c in cts]
                  + [_row_spec(tm, adds[k]) for k in add_keys]),
        out_specs=([pl.BlockSpec((tm, rows[k][2]), lambda i: (i, 0)) for k in want]
                   + [pl.BlockSpec((p[0].shape[0], p[2]), lambda i: (0, 0)) for p in params]),
        out_shape=([SDS((M, rows[k][2]), row_grads[k]) for k in want] + [SDS((p[0].shape[0], p[2]), F32) for p in params]),
        compiler_params=_cparams(dimension_semantics=("arbitrary",)),
    )(*[r[0] for r in rows], *[p[0] for p in params], *[c[0] for c in cts], *[adds[k][0] for k in add_keys])
    return res


def _rms_fn(x, g):
    r = lax.rsqrt(jnp.mean(x * x, axis=-1, keepdims=True) + EPS)
    return (x * r * g,)


def _ep_residual_rms(acc, res, g):
    x = acc + res
    return x, _rms_fn(x, g)[0]


def _ep_rms_bwd(dh, x, dres, g):
    r = lax.rsqrt(jnp.mean(x * x, axis=-1, keepdims=True) + EPS)
    t = dh * g
    dx = dres + r * (t - x * (r * r) * jnp.mean(t * x, axis=-1, keepdims=True))
    return dx, jnp.sum(dh * x * r, axis=0, keepdims=True)


def _ep_loss(acc, res, target, *, width):
    e = acc + res - target
    return e * (1.0 / width), jnp.sum(jnp.sum(e * e, axis=1, keepdims=True), axis=0, keepdims=True) * (0.5 / width)


def _seg_mats(width, seg):
    n = width // seg
    p = (_iota((width, n), 0) // seg == _iota((width, n), 1)).astype(F32)
    e = (_iota((n, width), 1) // seg == _iota((n, width), 0)).astype(F32)
    return p, e


def _gate_fn(y, xs, z, dskip, w):
    width = SSM_HEADS * SSM_P
    _, e = _seg_mats(width, SSM_P)
    y = (y + _cdot(dskip, e) * xs) * (z * jax.nn.sigmoid(z))
    g0 = _iota((1, width), 1) < width // SSM_GROUPS
    y2 = y * y
    gw = width // SSM_GROUPS
    ms0 = jnp.sum(jnp.where(g0, y2, 0.0), axis=-1, keepdims=True) * (1.0 / gw)
    ms1 = jnp.sum(jnp.where(g0, 0.0, y2), axis=-1, keepdims=True) * (1.0 / gw)
    r = jnp.where(g0, lax.rsqrt(ms0 + EPS), lax.rsqrt(ms1 + EPS))
    return (y * r * w,)


def _xattn_fn(q0, q1, q2, q3, k0, k1, k2, k3, v0, v1, v2, v3, gq, gk):
    def norm(u, g):
        return u * lax.rsqrt(jnp.mean(u * u, axis=-1, keepdims=True) + EPS) * g
    outs = []
    for q, k, v in ((q0, k0, v0), (q1, k1, v1), (q2, k2, v2), (q3, k3, v3)):
        s = mm_nt(norm(q, gq), norm(k, gk)) * (X_D ** -0.5)
        p = jnp.exp(s - lax.stop_gradient(jnp.max(s, axis=-1, keepdims=True)))
        p = p / jnp.sum(p, axis=-1, keepdims=True)
        outs.append(mm_nn(p, v))
    return (jnp.concatenate(outs, axis=-1),)


CONV_TC = 256


def _shift_down(u, k):
    if k == 0:
        return u
    return jnp.where(_iota(u.shape, 0) >= k, pltpu.roll(u, k, axis=0), 0.0)


def _shift_up(u, k):
    if k == 0:
        return u
    n = u.shape[0]
    return jnp.where(_iota(u.shape, 0) < n - k, pltpu.roll(u, n - k, axis=0), 0.0)


def _conv_pre(u, w_ref, b):
    pre = b + w_ref[3:4, :] * u
    for k in (1, 2, 3):
        pre = pre + w_ref[3 - k:4 - k, :] * _shift_down(u, k)
    return pre


def _conv_fwd(src, c0, width, w, b, *, name):
    S = src.shape[0]
    cb0 = c0 // CONV_TC

    def body(u_ref, w_ref, b_ref, o_ref):
        pre = _conv_pre(u_ref[...], w_ref, b_ref[...])
        o_ref[...] = pre * jax.nn.sigmoid(pre)

    return pl.pallas_call(
        body, name=name, grid=(width // CONV_TC,),
        in_specs=[pl.BlockSpec((S, CONV_TC), lambda j: (0, cb0 + j)), pl.BlockSpec((4, CONV_TC), lambda j: (0, j)),
                  pl.BlockSpec((1, CONV_TC), lambda j: (0, j))],
        out_specs=pl.BlockSpec((S, CONV_TC), lambda j: (0, j)), out_shape=SDS((S, width), F32),
        compiler_params=_cparams(dimension_semantics=("parallel",)),
    )(src, w, b)


def _conv_bwd(src, c0, width, w, b, douts, *, name):
    S = src.shape[0]
    cb0 = c0 // CONV_TC
    nd = len(douts)

    def body(*refs):
        u_ref, w_ref, b_ref = refs[:3]
        d_refs = refs[3:3 + nd]
        du_ref, dw_ref, db_ref = refs[3 + nd:]
        u = u_ref[...]
        pre = _conv_pre(u, w_ref, b_ref[...])
        sg = jax.nn.sigmoid(pre)
        dout = d_refs[0][...]
        for r in d_refs[1:]:
            dout = dout + r[...]
        dpre = dout * (sg * (1.0 + pre * (1.0 - sg)))
        du = w_ref[3:4, :] * dpre
        dw_ref[3:4, :] = jnp.sum(dpre * u, axis=0, keepdims=True)
        for k in (1, 2, 3):
            du = du + w_ref[3 - k:4 - k, :] * _shift_up(dpre, k)
            dw_ref[3 - k:4 - k, :] = jnp.sum(dpre * _shift_down(u, k), axis=0, keepdims=True)
        du_ref[...] = du.astype(du_ref.dtype)
        db_ref[...] = jnp.sum(dpre, axis=0, keepdims=True)

    return pl.pallas_call(
        body, name=name, grid=(width // CONV_TC,),
        in_specs=[pl.BlockSpec((S, CONV_TC), lambda j: (0, cb0 + j)), pl.BlockSpec((4, CONV_TC), lambda j: (0, j)),
                  pl.BlockSpec((1, CONV_TC), lambda j: (0, j))] + [pl.BlockSpec((S, CONV_TC), lambda j: (0, j))] * nd,
        out_specs=[pl.BlockSpec((S, CONV_TC), lambda j: (0, j)), pl.BlockSpec((4, CONV_TC), lambda j: (0, j)),
                   pl.BlockSpec((1, CONV_TC), lambda j: (0, j))],
        out_shape=[SDS((S, width), BF16), SDS((4, width), F32), SDS((1, width), F32)],
        compiler_params=_cparams(dimension_semantics=("parallel",)),
    )(src, w, b, *douts)


def _softplus(x):
    return jnp.maximum(x, 0.0) + jnp.log(1.0 + jnp.exp(-jnp.abs(x)))


def _prefix_sum(x, seg):
    n = x.shape[1]
    pos = _iota(x.shape, 1) % seg
    k = 1
    while k < seg:
        x = x + jnp.where(pos >= k, pltpu.roll(x, k, axis=1), 0.0)
        k *= 2
    return x


def _suffix_sum(x, seg):
    n = x.shape[1]
    pos = _iota(x.shape, 1) % seg
    k = 1
    while k < seg:
        x = x + jnp.where(pos + k < seg, pltpu.roll(x, n - k, axis=1), 0.0)
        k *= 2
    return x


def _dtf_fwd(dtf_t, dt_bias, a_log, f_bias):
    S = dtf_t.shape[1]

    def body(x_ref, db_ref, al_ref, fb_ref, dt_ref, acs_ref, cum_ref):
        dt = _softplus(x_ref[0:16, :] + db_ref[...])
        dt_ref[...] = dt
        acs_ref[...] = _prefix_sum(dt * (-jnp.exp(al_ref[...])), CHUNK)
        cum_ref[...] = _prefix_sum(-_softplus(-(x_ref[16:32, :] + fb_ref[...])), S)

    return pl.pallas_call(body, name="dtf_fwd", out_shape=[SDS((16, S), F32)] * 3, compiler_params=_cparams())(
        dtf_t, dt_bias, a_log, f_bias)


def _dtf_bwd(dtf_t, dt_bias, a_log, f_bias, d_dt, d_acs_a, d_acs_b, d_cum):
    S = dtf_t.shape[1]

    def body(x_ref, db_ref, al_ref, fb_ref, ddt_ref, da1_ref, da2_ref, dc_ref, dx_ref, ddb_ref, dal_ref, dfb_ref):
        xd = x_ref[0:16, :] + db_ref[...]
        dt = _softplus(xd)
        a = -jnp.exp(al_ref[...])
        d_da = _suffix_sum(da1_ref[...] + da2_ref[...], CHUNK)
        d_dt = ddt_ref[...] + d_da * a
        dal_ref[...] = jnp.sum(d_da * dt, axis=1, keepdims=True) * a
        d_xd = d_dt * jax.nn.sigmoid(xd)
        ddb_ref[...] = jnp.sum(d_xd, axis=1, keepdims=True)
        xf = x_ref[16:32, :] + fb_ref[...]
        d_xf = _suffix_sum(dc_ref[...], S) * jax.nn.sigmoid(-xf)
        dfb_ref[...] = jnp.sum(d_xf, axis=1, keepdims=True)
        dx_ref[0:16, :] = d_xd
        dx_ref[16:32, :] = d_xf

    return pl.pallas_call(body, name="dtf_bwd", out_shape=[SDS((32, S), F32)] + [SDS((16, 1), F32)] * 3,
                          compiler_params=_cparams())(dtf_t, dt_bias, a_log, f_bias, d_dt, d_acs_a, d_acs_b, d_cum)


SSM_PAIRS = SSM_HEADS // 2 // SSM_GROUPS


def _ssd_pair(xs, dtc, acol, arow, bm, cm, cbm, h, hp):
    L = CHUNK
    first = _iota((1, LANES), 1) < SSM_P
    i16, s16 = _iota((L, 16), 1), _iota((16, L), 0)
    ha, hb = 2 * hp, 2 * hp + 1

    def selc(blk, hh):
        return jnp.sum(jnp.where(i16 == hh, blk, 0.0), axis=1, keepdims=True)

    def selr(blk, hh):
        return jnp.sum(jnp.where(s16 == hh, blk, 0.0), axis=0, keepdims=True)

    x = xs * jnp.where(first, selc(dtc, ha), selc(dtc, hb))
    ca, cb, ra, rb = selc(acol, ha), selc(acol, hb), selr(arow, ha), selr(arow, hb)
    tri = _iota((L, L), 0) >= _iota((L, L), 1)
    la = jnp.exp(jnp.where(tri, ca - ra, NEG))
    lb = jnp.exp(jnp.where(tri, cb - rb, NEG))
    y = jnp.where(first, mm_nn(cbm * la, x), mm_nn(cbm * lb, x))
    y = y + jnp.where(first, jnp.exp(ca), jnp.exp(cb)) * mm_nn(cm, h)
    last = _iota((1, L), 1) == L - 1
    ala = jnp.sum(jnp.where(last, ra, 0.0), axis=1, keepdims=True)
    alb = jnp.sum(jnp.where(last, rb, 0.0), axis=1, keepdims=True)
    dec = jnp.where(first, jnp.exp(ala - ca), jnp.exp(alb - cb))
    hn = jnp.where(first, jnp.exp(ala), jnp.exp(alb)) * h + mm_tn(bm, x * dec)
    return y, hn


def _ssd_group(*args, grp):
    xs, (dtc, acol, arow, bm, cm), hs = args[:SSM_PAIRS], args[SSM_PAIRS:SSM_PAIRS + 5], args[SSM_PAIRS + 5:]
    cbm = mm_nt(cm, bm)
    res = [_ssd_pair(xs[j], dtc, acol, arow, bm, cm, cbm, hs[j], SSM_PAIRS * grp + j) for j in range(SSM_PAIRS)]
    return tuple(r[0] for r in res) + tuple(r[1] for r in res)


def _ssd_specs(nc, rev):
    L = CHUNK
    cidx = (lambda c: nc - 1 - c) if rev else (lambda c: c)
    return dict(
        xs=pl.BlockSpec((L, SSM_PAIRS * LANES), lambda c, g: (cidx(c), g)),
        col=pl.BlockSpec((L, 16), lambda c, g: (cidx(c), 0)),
        row=pl.BlockSpec((16, L), lambda c, g: (0, cidx(c))),
        b=pl.BlockSpec((L, SSM_N), lambda c, g: (cidx(c), g)),
        c=pl.BlockSpec((L, SSM_N), lambda c, g: (cidx(c), SSM_GROUPS + g)),
        st=pl.BlockSpec((1, SSM_PAIRS, SSM_N, LANES), lambda c, g: (cidx(c), g, 0, 0)),
    )


def _lane_pieces(v):
    return [v[:, LANES * j:LANES * (j + 1)] for j in range(v.shape[1] // LANES)]


def _ssd_fwd(xs, dt_col, acs_col, acs_row, bc):
    S = xs.shape[0]
    nc, nhp = S // CHUNK, SSM_HEADS // 2
    sp = _ssd_specs(nc, False)

    def body(xs_ref, dt_ref, ac_ref, ar_ref, b_ref, c_ref, y_ref, hs_ref, h_scr):
        c, g = pl.program_id(0), pl.program_id(1)

        @pl.when(c == 0)
        def _():
            for j in range(SSM_PAIRS):
                h_scr[SSM_PAIRS * g + j] = jnp.zeros((SSM_N, LANES), F32)

        hs = [h_scr[SSM_PAIRS * g + j] for j in range(SSM_PAIRS)]
        for j in range(SSM_PAIRS):
            hs_ref[0, j] = hs[j]
        res = _ssd_group(*_lane_pieces(xs_ref[...]), dt_ref[...], ac_ref[...], ar_ref[...], b_ref[...], c_ref[...], *hs,
                         grp=g)
        y_ref[...] = jnp.concatenate(res[:SSM_PAIRS], axis=1)
        for j in range(SSM_PAIRS):
            h_scr[SSM_PAIRS * g + j] = res[SSM_PAIRS + j]

    return pl.pallas_call(
        body, name="ssd_fwd", grid=(nc, SSM_GROUPS),
        in_specs=[sp["xs"], sp["col"], sp["col"], sp["row"], sp["b"], sp["c"]],
        out_specs=[sp["xs"], sp["st"]],
        out_shape=[SDS((S, SSM_HEADS * SSM_P), F32), SDS((nc, nhp, SSM_N, LANES), F32)],
        scratch_shapes=[pltpu.VMEM((nhp, SSM_N, LANES), F32)],
        compiler_params=_cparams(dimension_semantics=("arbitrary", "arbitrary")),
    )(xs, dt_col, acs_col, acs_row, bc, bc)


def _ssd_bwd(xs, dt_col, acs_col, acs_row, bc, hs, dy):
    S = xs.shape[0]
    nc, nhp = S // CHUNK, SSM_HEADS // 2
    sp = _ssd_specs(nc, True)

    def body(xs_ref, dt_ref, ac_ref, ar_ref, b_ref, c_ref, hs_ref, dy_ref,
             dxs_ref, ddt_ref, dac_ref, dar_ref, db_ref, dc_ref, dh_scr):
        c, g = pl.program_id(0), pl.program_id(1)

        @pl.when(c == 0)
        def _():
            for j in range(SSM_PAIRS):
                dh_scr[SSM_PAIRS * g + j] = jnp.zeros((SSM_N, LANES), F32)

        _, vjp = jax.vjp(functools.partial(_ssd_group, grp=g), *_lane_pieces(xs_ref[...]), dt_ref[...], ac_ref[...],
                         ar_ref[...], b_ref[...], c_ref[...], *[hs_ref[0, j] for j in range(SSM_PAIRS)])
        grads = vjp(tuple(_lane_pieces(dy_ref[...])) + tuple(dh_scr[SSM_PAIRS * g + j] for j in range(SSM_PAIRS)))
        dxs_ref[...] = jnp.concatenate(grads[:SSM_PAIRS], axis=1)
        ddt, dac, dar, db, dc = grads[SSM_PAIRS:SSM_PAIRS + 5]
        for j in range(SSM_PAIRS):
            dh_scr[SSM_PAIRS * g + j] = grads[SSM_PAIRS + 5 + j]
        db_ref[...] = db
        dc_ref[...] = dc

        @pl.when(g == 0)
        def _():
            ddt_ref[...] = ddt
            dac_ref[...] = dac
            dar_ref[...] = dar

        @pl.when(g > 0)
        def _():
            ddt_ref[...] += ddt
            dac_ref[...] += dac
            dar_ref[...] += dar

    return pl.pallas_call(
        body, name="ssd_bwd", grid=(nc, SSM_GROUPS),
        in_specs=[sp["xs"], sp["col"], sp["col"], sp["row"], sp["b"], sp["c"], sp["st"], sp["xs"]],
        out_specs=[sp["xs"], sp["col"], sp["col"], sp["row"], sp["b"], sp["b"]],
        out_shape=[SDS((S, SSM_HEADS * SSM_P), F32), SDS((S, 16), F32), SDS((S, 16), F32), SDS((16, S), F32),
                   SDS((S, SSM_GROUPS * SSM_N), F32), SDS((S, SSM_GROUPS * SSM_N), F32)],
        scratch_shapes=[pltpu.VMEM((nhp, SSM_N, LANES), F32)],
        compiler_params=_cparams(dimension_semantics=("arbitrary", "arbitrary")),
    )(xs, dt_col, acs_col, acs_row, bc, bc, hs, dy)


ATT_T = 1024


def _pick_col(blk, h):
    return jnp.sum(jnp.where(_iota(blk.shape, 1) == h, blk, 0.0), axis=1, keepdims=True)


def _pick_row(blk, h):
    return jnp.sum(jnp.where(_iota(blk.shape, 0) == h, blk, 0.0), axis=0, keepdims=True)


def _pair_norm(x, g2, first):
    x2 = x * x
    sa = jnp.sum(jnp.where(first, x2, 0.0), axis=1, keepdims=True)
    sb = jnp.sum(jnp.where(first, 0.0, x2), axis=1, keepdims=True)
    r = jnp.where(first, lax.rsqrt(sa * (1.0 / ATT_D) + EPS), lax.rsqrt(sb * (1.0 / ATT_D) + EPS))
    return x * r * g2, r


def _pair_norm_bwd(dxn, x, r, g2, first):
    t = dxn * g2
    tx = t * x
    ma = jnp.sum(jnp.where(first, tx, 0.0), axis=1, keepdims=True)
    mb = jnp.sum(jnp.where(first, 0.0, tx), axis=1, keepdims=True)
    dx = r * (t - x * (r * r) * (jnp.where(first, ma, mb) * (1.0 / ATT_D)))
    return dx, jnp.sum(dxn * x * r, axis=0, keepdims=True)


def _fox_fwd(src, q_c0, k_c0, v_c0, gq2, gk2, cum_col, cum_row3):
    S = src.shape[0]
    T = ATT_T
    nq, nhp = S // T, ATT_HEADS // 2
    qb0, kb0, vb0 = q_c0 // LANES, k_c0 // LANES, v_c0 // LANES
    scale = ATT_D ** -0.5

    def body(q_ref, kraw_ref, v_ref, gq_ref, gk_ref, cc_ref, cr_ref, o_ref, l_ref, k_ref):
        hp, i = pl.program_id(0), pl.program_id(1)
        first = _iota((1, LANES), 1) < ATT_D

        @pl.when(i == 0)
        def _():
            k_ref[...] = _pair_norm(kraw_ref[...], gk_ref[...], first)[0].astype(BF16)

        H = T // 2

        def attend(tile, half):
            rows = pl.ds(half * H, H)
            row0 = tile * T + half * H
            klen = row0 + H
            q = (_pair_norm(q_ref[rows, :], gq_ref[...], first)[0] * scale).astype(BF16)
            zero = jnp.zeros_like(q)
            cc = cc_ref[rows, :]
            k = k_ref[0:klen, :]
            v = v_ref[0:klen, :].astype(BF16)
            allowed = _iota((H, klen), 0) + row0 >= _iota((H, klen), 1)
            outs, lses = [], []
            for hh in range(2):
                sel = first if hh == 0 else jnp.logical_not(first)
                ck = jnp.concatenate([_pick_row(cr_ref[j], 2 * hp + hh) for j in range(tile + 1)], axis=1)[:, :klen]
                s = _bdot(jnp.where(sel, q, zero), k, _NT) + (_pick_col(cc, 2 * hp + hh) - ck)
                s = jnp.where(allowed, s, NEG)
                m = jnp.max(s, axis=1, keepdims=True)
                p = jnp.exp(s - m)
                l = jnp.sum(p, axis=1, keepdims=True)
                outs.append(_bdot(p, v, _NN) / l)
                lses.append(m + jnp.log(l))
            o_ref[rows, :] = jnp.where(first, outs[0], outs[1]).astype(o_ref.dtype)
            l_ref[rows, :] = jnp.where(first, lses[0], lses[1])

        for tile in range(nq):
            @pl.when(i == tile)
            def _(tile=tile):
                for half in range(2):
                    attend(tile, half)

    gain = pl.BlockSpec((1, LANES), lambda hp, i: (0, 0))
    return pl.pallas_call(
        body, name="fox_fwd", grid=(nhp, nq),
        in_specs=[pl.BlockSpec((T, LANES), lambda hp, i: (i, qb0 + hp)), pl.BlockSpec((S, LANES), lambda hp, i: (0, kb0 + hp)),
                  pl.BlockSpec((S, LANES), lambda hp, i: (0, vb0 + hp)), gain, gain,
                  pl.BlockSpec((T, 16), lambda hp, i: (i, 0)), pl.BlockSpec((nq, 16, T), lambda hp, i: (0, 0, 0))],
        out_specs=[pl.BlockSpec((T, LANES), lambda hp, i: (i, hp))] * 2,
        out_shape=[SDS((S, ATT_HEADS * ATT_D), BF16), SDS((S, ATT_HEADS * ATT_D), F32)],
        scratch_shapes=[pltpu.VMEM((S, LANES), BF16)],
        compiler_params=_cparams(dimension_semantics=("arbitrary", "arbitrary")),
    )(src, src, src, gq2, gk2, cum_col, cum_row3)


def _fox_bwd(src, q_c0, k_c0, v_c0, gq2, gk2, cum_col, cum_row3, lse, dsrc, d_c0):
    S = src.shape[0]
    T = ATT_T
    nq, nhp = S // T, ATT_HEADS // 2
    qb0, kb0, vb0, db0 = q_c0 // LANES, k_c0 // LANES, v_c0 // LANES, d_c0 // LANES
    scale = ATT_D ** -0.5

    def body(q_ref, kraw_ref, v_ref, gq_ref, gk_ref, cc_ref, cr_ref, l_ref, do_ref,
             dq_ref, dk_ref, dv_ref, dc_ref, dg_ref, k_ref, dk_acc, dv_acc, p_scr, dp_scr):
        hp, i = pl.program_id(0), pl.program_id(1)
        first = _iota((1, LANES), 1) < ATT_D

        @pl.when(i == 0)
        def _():
            k_ref[...] = _pair_norm(kraw_ref[...], gk_ref[...], first)[0].astype(BF16)
            dk_acc[...] = jnp.zeros_like(dk_acc)
            dv_acc[...] = jnp.zeros_like(dv_acc)
            dc_ref[...] = jnp.zeros_like(dc_ref)
            dg_ref[...] = jnp.zeros_like(dg_ref)

        H = T // 2
        for half in range(2):
            rows = pl.ds(half * H, H)
            dlen = (half + 1) * H
            q_raw = q_ref[rows, :]
            qn, rq = _pair_norm(q_raw, gq_ref[...], first)
            q = (qn * scale).astype(BF16)
            zq = jnp.zeros_like(q)
            dob = do_ref[rows, :].astype(BF16)
            zd = jnp.zeros_like(dob)
            lse_blk, cc = l_ref[rows, :], cc_ref[rows, :]
            dq = jnp.zeros((H, LANES), F32)
            for hh in range(2):
                sel = first if hh == 0 else jnp.logical_not(first)
                qh, doh = jnp.where(sel, q, zq), jnp.where(sel, dob, zd)
                bias_q = _pick_col(cc, 2 * hp + hh) - jnp.max(jnp.where(sel, lse_blk, NEG), axis=1, keepdims=True)

                def probs(j, delta, klen, diagonal):
                    off = pl.multiple_of(j * T, T)
                    s = _bdot(qh, k_ref[pl.ds(off, klen), :], _NT) + (bias_q - _pick_row(cr_ref[j], 2 * hp + hh)[:, :klen])
                    if diagonal:
                        s = jnp.where(_iota((H, klen), 0) + half * H >= _iota((H, klen), 1), s, NEG)
                    p = jnp.exp(s)
                    dp = _bdot(doh, v_ref[pl.ds(off, klen), :], _NT)
                    p_scr[j, :, 0:klen] = p
                    dp_scr[j, :, 0:klen] = dp
                    return delta + jnp.sum(p * dp, axis=1, keepdims=True)

                delta = lax.fori_loop(0, i, lambda j, d: probs(j, d, T, False), jnp.zeros((H, 1), F32))
                delta = probs(i, delta, dlen, True)

                def grads(j, dq, klen):
                    off = pl.multiple_of(j * T, T)
                    p = p_scr[j, :, 0:klen]
                    ds = p * (dp_scr[j, :, 0:klen] - delta)
                    dv_acc[pl.ds(off, klen), :] += _bdot(p, doh, _TN)
                    dk_acc[pl.ds(off, klen), :] += _bdot(ds, qh, _TN)
                    dc_ref[0, j, hh:hh + 1, 0:klen] -= jnp.sum(ds, axis=0, keepdims=True)
                    zk = jnp.zeros((klen, LANES), BF16)
                    return dq + _bdot(ds, jnp.where(sel, k_ref[pl.ds(off, klen), :], zk), _NN)

                dq = lax.fori_loop(0, i, lambda j, d: grads(j, d, T), dq)
                dq = grads(i, dq, dlen)
            dq_raw, dgq = _pair_norm_bwd(dq * scale, q_raw, rq, gq_ref[...], first)
            dq_ref[rows, :] = dq_raw.astype(dq_ref.dtype)
            dg_ref[0, 0:1, :] += dgq

        @pl.when(i == nq - 1)
        def _():
            k_raw = kraw_ref[...]
            rk = _pair_norm(k_raw, gk_ref[...], first)[1]
            dk_raw, dgk = _pair_norm_bwd(dk_acc[...], k_raw, rk, gk_ref[...], first)
            dk_ref[...] = dk_raw.astype(dk_ref.dtype)
            dv_ref[...] = dv_acc[...].astype(dv_ref.dtype)
            dg_ref[0, 1:2, :] = dgk

    gain = pl.BlockSpec((1, LANES), lambda hp, i: (0, 0))
    band = SDS((S, ATT_HEADS * ATT_D), BF16)
    return pl.pallas_call(
        body, name="fox_bwd", grid=(nhp, nq),
        in_specs=[pl.BlockSpec((T, LANES), lambda hp, i: (i, qb0 + hp)), pl.BlockSpec((S, LANES), lambda hp, i: (0, kb0 + hp)),
                  pl.BlockSpec((S, LANES), lambda hp, i: (0, vb0 + hp)), gain, gain,
                  pl.BlockSpec((T, 16), lambda hp, i: (i, 0)), pl.BlockSpec((nq, 16, T), lambda hp, i: (0, 0, 0)),
                  pl.BlockSpec((T, LANES), lambda hp, i: (i, hp)), pl.BlockSpec((T, LANES), lambda hp, i: (i, db0 + hp))],
        out_specs=[pl.BlockSpec((T, LANES), lambda hp, i: (i, hp)), pl.BlockSpec((S, LANES), lambda hp, i: (0, hp)),
                   pl.BlockSpec((S, LANES), lambda hp, i: (0, hp)), pl.BlockSpec((1, nq, 8, T), lambda hp, i: (hp, 0, 0, 0)),
                   pl.BlockSpec((1, 8, LANES), lambda hp, i: (hp, 0, 0))],
        out_shape=[band, band, band, SDS((nhp, nq, 8, T), F32), SDS((nhp, 8, LANES), F32)],
        scratch_shapes=[pltpu.VMEM((S, LANES), BF16), pltpu.VMEM((S, LANES), F32), pltpu.VMEM((S, LANES), F32),
                        pltpu.VMEM((nq, T // 2, T), F32), pltpu.VMEM((nq, T // 2, T), F32)],
        compiler_params=_cparams(dimension_semantics=("arbitrary", "arbitrary")),
    )(src, src, src, gq2, gk2, cum_col, cum_row3, lse, dsrc)


def _fold_gains(dg):
    def body(d_ref, o_ref):
        t = d_ref[0]
        for h in range(1, dg.shape[0]):
            t = t + d_ref[h]
        o_ref[...] = t + pltpu.roll(t, ATT_D, axis=1)

    return pl.pallas_call(body, name="fold_gains", out_shape=SDS(dg.shape[1:], F32), compiler_params=_cparams())(dg)


def _adamw_math(w, g, m, v):
    m = ADAM_B1 * m + (1.0 - ADAM_B1) * g
    v = ADAM_B2 * v + (1.0 - ADAM_B2) * jnp.square(g)
    m_hat = m / (1.0 - ADAM_B1 ** ADAM_STEP)
    v_hat = v / (1.0 - ADAM_B2 ** ADAM_STEP)
    delta = -ADAM_LR * (m_hat / (jnp.sqrt(v_hat) + ADAM_EPS) + ADAM_WD * w)
    return delta, m, v


def _reduce_adamw(parts, w, m, v, *, tr, name, tc=None):
    R, C = w.shape
    tr, tc = min(tr, R), tc or C
    nparts = parts.shape[0]

    def body(p_ref, w_ref, m_ref, v_ref, g_ref, d_ref, nm_ref, nv_ref):
        g = p_ref[0].astype(F32)
        for s in range(1, nparts):
            g = g + p_ref[s].astype(F32)
        g_ref[...] = g
        d_ref[...], nm_ref[...], nv_ref[...] = _adamw_math(w_ref[...], g, m_ref[...], v_ref[...])

    blk = pl.BlockSpec((tr, tc), lambda i, j: (i, j))
    return pl.pallas_call(
        body, name=name, grid=(R // tr, C // tc),
        in_specs=[pl.BlockSpec((nparts, tr, tc), lambda i, j: (0, i, j)), blk, blk, blk], out_specs=[blk] * 4,
        out_shape=[SDS((R, C), F32)] * 4, compiler_params=_cparams(dimension_semantics=("parallel", "parallel")),
    )(parts, w, m, v)


def _adamw(w, g, m, v, *, name):
    def body(w_ref, g_ref, m_ref, v_ref, d_ref, nm_ref, nv_ref):
        d_ref[...], nm_ref[...], nv_ref[...] = _adamw_math(w_ref[...], g_ref[...], m_ref[...], v_ref[...])

    return pl.pallas_call(body, name=name, out_shape=[SDS(w.shape, F32)] * 3, compiler_params=_cparams())(w, g, m, v)


def _peers():
    x, y, c = lax.axis_index("x"), lax.axis_index("y"), lax.axis_index("c")
    out = []
    for k in range(1, N_DEV):
        px, py, pc = x ^ ((k >> 2) & 1), y ^ ((k >> 1) & 1), c ^ (k & 1)
        out.append(((px, py, pc), 4 * px + 2 * py + pc))
    return 4 * x + 2 * y + c, out


_HBM = pl.BlockSpec(memory_space=pltpu.HBM)
_SEM = pl.BlockSpec(memory_space=pltpu.SEMAPHORE)
_DATAFLOW = pltpu.SideEffectType.DATAFLOW_SIDE_EFFECTING


NEAR = (1, 2, 4, 6)


def _plan_peers(scatter, ks=tuple(range(1, N_DEV))):
    return lambda me, peers: [(peers[k - 1][0], peers[k - 1][1] if scatter else None, me, k - 1) for k in ks]


def _plan_relay(me, peers):
    return [(peers[0][0], peers[k - 1][1], peers[k - 1][1], j) for j, k in enumerate((2, 4, 6))]


def _plan_pair(me, peers):
    return [(peers[0][0], peers[k - 1][1], j, j) for j, k in enumerate((1, 3, 5, 7))]


def _plan_chips(me, peers):
    return [(peers[k - 1][0], k // 2, k // 2, k // 2) for k in (2, 4, 6)]


def _copy(src, dst, c, send_sems, recv_sems):
    dev, s_slot, d_slot, i = c
    return pltpu.make_async_remote_copy(
        src_ref=src if s_slot is None else src.at[s_slot], dst_ref=dst.at[d_slot], send_sem=send_sems.at[i],
        recv_sem=recv_sems.at[i], device_id=dev, device_id_type=MESH)


def _copies_start(items, *, name):
    n = len(items)
    bufs = [it[0] for it in items] + [it[1] for it in items if it[1] is not None]
    nb = len(bufs)

    def body(*refs):
        srcs, extra, sems, token = refs[:n], iter(refs[n:nb]), refs[nb:nb + 2 * n], refs[-1]
        me, peers = _peers()
        for a, (_, land, plan) in enumerate(items):
            dst = srcs[a] if land is None else next(extra)
            for c in plan(me, peers):
                _copy(srcs[a], dst, c, sems[2 * a], sems[2 * a + 1]).start()
        token[...] = jnp.zeros_like(token)

    res = pl.pallas_call(
        body, name=name,
        out_shape=([pltpu.SemaphoreType.DMA((N_DEV - 1,))] * (2 * n) + [pltpu.HBM(b.shape, b.dtype) for b in bufs]
                   + [SDS((8, LANES), F32)]),
        in_specs=[_HBM] * nb, out_specs=[_SEM] * (2 * n) + [_HBM] * nb + [pl.BlockSpec(memory_space=pltpu.VMEM)],
        input_output_aliases={i: 2 * n + i for i in range(nb)},
        compiler_params=pltpu.CompilerParams(has_side_effects=_DATAFLOW),
    )(*[pltpu.with_memory_space_constraint(b, pltpu.HBM) for b in bufs])
    sems, thru, token = res[:2 * n], list(res[2 * n:2 * n + nb]), res[-1]
    extra = iter(thru[n:])
    return [(thru[a], None if it[1] is None else next(extra), sems[2 * a], sems[2 * a + 1], it[2])
            for a, it in enumerate(items)], token


def _copies_wait(handles, after, *, name):
    n = len(handles)
    bufs = [h[0] for h in handles] + [h[1] for h in handles if h[1] is not None]
    nb = len(bufs)

    def body(*refs):
        srcs, extra, sems = refs[:n], iter(refs[n:nb]), refs[nb:nb + 2 * n]
        me, peers = _peers()
        for a, h in enumerate(handles):
            dst = srcs[a] if h[1] is None else next(extra)
            for c in h[4](me, peers):
                cp = _copy(srcs[a], dst, c, sems[2 * a], sems[2 * a + 1])
                cp.wait_send()
                cp.wait_recv()

    flat_sems = [s for h in handles for s in (h[2], h[3])]
    res = pl.pallas_call(
        body, name=name, out_shape=[pltpu.HBM(b.shape, b.dtype) for b in bufs],
        in_specs=[_HBM] * nb + [_SEM] * (2 * n) + [pl.BlockSpec(memory_space=pl.ANY)], out_specs=[_HBM] * nb,
        input_output_aliases={i: i for i in range(nb)},
        compiler_params=pltpu.CompilerParams(has_side_effects=_DATAFLOW),
    )(*bufs, *flat_sems, after)
    extra = iter(res[n:])
    return [(res[a], res[a] if h[1] is None else next(extra)) for a, h in enumerate(handles)]


def _exchange_start(arrays, *, scatter, name, near=()):
    items = []
    for a, arr in enumerate(arrays):
        land = lax.empty(arr.shape if scatter else (N_DEV,) + arr.shape, arr.dtype)
        items.append((arr, land, _plan_peers(scatter, NEAR) if a in near else _plan_peers(scatter)))
    return _copies_start(items, name=name)


MOVE_ROWS, MOVE_SLOTS = 512, 3


def _move_rows(src, moves, rows, *, name):
    C = src.shape[1]
    covered = max(dst + n for _, n, dst in moves)
    tail = rows - covered
    assert sum(n for _, n, _ in moves) == covered
    chunks = [(lo + o, min(MOVE_ROWS, n - o), dst + o) for lo, n, dst in moves for o in range(0, n, MOVE_ROWS)]
    nch = len(chunks)

    def body(src_ref, o_ref, buf, sin, sout, *zero):
        def fetch(i):
            lo, n, _ = chunks[i]
            return pltpu.make_async_copy(src_ref.at[pl.ds(lo, n)], buf.at[i % MOVE_SLOTS, pl.ds(0, n)], sin.at[i % MOVE_SLOTS])

        def store(i):
            _, n, dst = chunks[i]
            return pltpu.make_async_copy(buf.at[i % MOVE_SLOTS, pl.ds(0, n)], o_ref.at[pl.ds(dst, n)], sout.at[i % MOVE_SLOTS])

        if tail:
            zero[0][...] = jnp.zeros_like(zero[0])
            fill = pltpu.make_async_copy(zero[0], o_ref.at[pl.ds(covered, tail)], zero[1])
            fill.start()
        for i in range(nch):
            if i >= MOVE_SLOTS:
                store(i - MOVE_SLOTS).wait()
            fetch(i).start()
            if i >= 1:
                fetch(i - 1).wait()
                store(i - 1).start()
        fetch(nch - 1).wait()
        store(nch - 1).start()
        for i in range(max(0, nch - MOVE_SLOTS), nch):
            store(i).wait()
        if tail:
            fill.wait()

    anyspec = pl.BlockSpec(memory_space=pl.ANY)
    dma = pltpu.SemaphoreType.DMA
    return pl.pallas_call(
        body, name=name, in_specs=[anyspec], out_specs=anyspec, out_shape=SDS((rows, C), src.dtype),
        scratch_shapes=([pltpu.VMEM((MOVE_SLOTS, MOVE_ROWS, C), src.dtype), dma((MOVE_SLOTS,)), dma((MOVE_SLOTS,))]
                        + ([pltpu.VMEM((tail, C), src.dtype), dma] if tail else [])),
        compiler_params=_cparams())(src)


def _pair_sum(a, b, *, name):
    n, R, C = a.shape
    tc = 256

    def body(a_ref, b_ref, o_ref):
        o_ref[...] = (a_ref[...].astype(F32) + b_ref[...].astype(F32)).astype(o_ref.dtype)

    blk = pl.BlockSpec((1, R, tc), lambda i, j: (i, 0, j))
    return pl.pallas_call(body, name=name, grid=(n, C // tc), in_specs=[blk, blk], out_specs=blk,
                          out_shape=SDS(a.shape, a.dtype), compiler_params=_cparams(dimension_semantics=("parallel", "parallel")))(a, b)


def _own_slot(landed, own, me):
    return lax.dynamic_update_slice(landed, own[None], (me,) + (0,) * own.ndim)


SMALL = (("g_mix", 1024), ("conv_w", 6144), ("conv_b", 1536), ("dt_bias", 16), ("a_log", 16), ("d_skip", 16),
         ("ssm_norm_w", 1024), ("g_q", 64), ("g_k", 64), ("f_bias", 16), ("g_xattn", 1024), ("g_mem", 1024),
         ("xg_q", 256), ("xg_k", 256), ("g_mlp", 1024))
SLAB_ROWS = 112
BIG = ("w_in", "w_out", "xq_w", "xkv_w", "xo_w", "w_up", "w_down")
WEIGHTS = ("g_mix", "w_in", "conv_w", "conv_b", "dt_bias", "a_log", "d_skip", "ssm_norm_w", "g_q", "g_k", "f_bias", "w_out",
           "g_xattn", "g_mem", "xq_w", "xkv_w", "xg_q", "xg_k", "xo_w", "g_mlp", "w_up", "w_down")
O_Z, O_XS, O_B, O_C, O_DT, O_Q, O_K, O_V, O_F, O_END = 0, 1024, 2048, 2304, 2560, 2576, 3600, 4624, 5648, 5664
IN_ROW_MOVES = ((O_Z, O_B - O_Z, C_Z), (O_Q, O_F - O_Q, C_Q), (O_B, O_Q - O_B, C_B), (O_F, O_END - O_F, C_DTF + 16))


def _pack_small(vals):
    rows = []
    for name, size in SMALL:
        flat = vals[name].reshape(-1).astype(F32)
        pad = -size % LANES
        rows.append(jnp.pad(flat, (0, pad)).reshape(-1, LANES))
    slab = jnp.concatenate(rows, axis=0)
    return jnp.pad(slab, ((0, SLAB_ROWS - slab.shape[0]), (0, 0)))


def _unpack_small(slab):
    out, r = {}, 0
    for name, size in SMALL:
        nr = -(-size // LANES)
        out[name] = slab[r:r + nr].reshape(-1)[:size]
        r += nr
    return out


def _step(p, m, v, x, mem, target):
    S = x.shape[0]
    TM = 256
    me = 4 * lax.axis_index("x") + 2 * lax.axis_index("y") + lax.axis_index("c")

    def rms(u, g, name):
        return _rw_fwd(_rms_fn, [_whole(u)], [_whole(g)], [(D_MODEL, BF16)], tm=TM, name=name)[0]

    def pin(param, token):
        return param + token[0:1, 0:1]

    def landed_with_own(pairs, scatter):
        out = []
        for src, land in pairs:
            own = lax.dynamic_index_in_dim(src, me, 0, keepdims=False) if scatter else src
            out.append(_own_slot(land, own, me))
        return out

    w_in_own, m_in_own, v_in_own = p["w_in"].T, m["w_in"].T, v["w_in"].T
    ag, ag_token = _exchange_start([w_in_own.astype(BF16), p["conv_w"]] + [p[n].astype(BF16) for n in BIG[1:]],
                                   scatter=False, name="allgather_start", near=(0, 2, 3, 4, 5, 6, 7))
    h1 = rms(x, pin(p["g_mix"], ag_token), "rms_mix")
    (win_src, win_land), convw_pair = _copies_wait(ag[:2], h1, name="allgather_wait_in")
    relay, token = _copies_start([(win_land, None, _plan_relay)], name="allgather_relay_start")
    win_land = _copies_wait(relay, token, name="allgather_relay_wait")[0][1]
    win_g, convw_g = landed_with_own([(win_src, win_land), convw_pair], False)
    w_in_o = win_g.reshape(O_END, D_MODEL)
    w_in_t = _move_rows(w_in_o, IN_ROW_MOVES, P_COLS, name="w_in_rows")
    conv_w = convw_g.transpose(1, 0, 2).reshape(4, 1536)
    cw_xs, cw_bc = conv_w[:, :1024], conv_w[:, 1024:]
    cb_xs, cb_bc = p["conv_b"][:, :1024], p["conv_b"][:, 1024:]
    dt_bias, a_log, f_bias = p["dt_bias"].reshape(16, 1), p["a_log"].reshape(16, 1), p["f_bias"].reshape(16, 1)

    proj = _matmul(h1, w_in_t, mode="nt", tm=1024, tn=640, tk=1024, name="mm_in")
    xs_c = _conv_fwd(proj, C_XS, 1024, cw_xs, cb_xs, name="conv_xs")
    bc_c = _conv_fwd(proj, C_B, 512, cw_bc, cb_bc, name="conv_bc")
    dtf_t = proj[:, C_DTF:C_DTF + 32].T
    dt_t, acs_t, cum_t = _dtf_fwd(dtf_t, dt_bias, a_log, f_bias)
    dt_col, acs_col, cum_col = dt_t.T, acs_t.T, cum_t.T
    cum_row3 = cum_t.reshape(16, S // ATT_T, ATT_T).transpose(1, 0, 2)
    y_ssd, hs = _ssd_fwd(xs_c, dt_col, acs_col, acs_t, bc_c)
    gate_rows = [_whole(y_ssd), _whole(xs_c), (proj, C_Z, 1024)]
    gate_pars = [_whole(p["d_skip"]), _whole(p["ssm_norm_w"])]
    y_ssm = _rw_fwd(_gate_fn, gate_rows, gate_pars, [(1024, BF16)], tm=TM, name="gate")[0]
    gq2, gk2 = jnp.tile(p["g_q"], (1, 2)), jnp.tile(p["g_k"], (1, 2))
    o, lse = _fox_fwd(proj, C_Q, C_K, C_V, gq2, gk2, cum_col, cum_row3)
    mixed = jnp.concatenate([y_ssm, o], axis=1)
    arrived = _copies_wait(ag[2:], mixed, name="allgather_wait_rest")
    relay, token = _copies_start([(land, None, _plan_relay) for _, land in arrived], name="allgather_relay_rest_start")
    wout_g, = landed_with_own([(arrived[0][0], _copies_wait(relay[:1], token, name="allgather_relay_out_wait")[0][1])], False)
    w_out = wout_g.reshape(2 * D_MODEL, D_MODEL)
    x1, h2 = _matmul(mixed, w_out, mode="nn", tm=1024, tn=1024, tk=2048, name="mm_out", extras=(x,),
                     row_params=(p["g_xattn"],), epilogue=_ep_residual_rms, out_dtypes=[F32, BF16])
    relayed = _copies_wait(relay[1:], x1, name="allgather_relay_rest_wait")
    xq_g, xkv_w, xo_g, w_up, wdown_g = landed_with_own(
        [(src, land) for (src, _), (_, land) in zip(arrived[1:], relayed, strict=True)], False)
    xq_w = xq_g.reshape(D_MODEL, D_MODEL)
    xo_w = xo_g.reshape(D_MODEL, D_MODEL)
    w_down = wdown_g.reshape(4 * D_MODEL, D_MODEL)

    mem_n = rms(mem, p["g_mem"], "rms_mem")
    q2 = _matmul(h2, xq_w, mode="nn", tm=1024, tn=512, tk=1024, name="mm_xq")
    kv = _matmul(mem_n, xkv_w, mode="nn", b_shards=True, tm=256, tn=256, tk=1024, name="mm_xkv")
    xa_rows = [(q2, X_D * h, X_D) for h in range(X_HEADS)]
    xa_pars = ([(kv, X_D * h, X_D) for h in range(X_HEADS)] + [(kv, D_MODEL + X_D * h, X_D) for h in range(X_HEADS)]
               + [_whole(p["xg_q"]), _whole(p["xg_k"])])
    o2 = _rw_fwd(_xattn_fn, xa_rows, xa_pars, [(D_MODEL, BF16)], tm=TM, name="xattn")[0]
    x2, h3 = _matmul(o2, xo_w, mode="nn", tm=1024, tn=1024, tk=1024, name="mm_xo", extras=(x1,),
                     row_params=(p["g_mlp"],), epilogue=_ep_residual_rms, out_dtypes=[F32, BF16])

    a, usq = _matmul(h3, w_up, mode="nn", b_shards=True, tm=2048, tn=512, tk=1024, name="mm_up", out_dtypes=[F32, BF16],
                     epilogue=lambda acc: (acc, jnp.square(jax.nn.relu(acc))))
    dy, loss_part = _matmul(usq, w_down, mode="nn", tm=1024, tn=512, tk=2048, name="mm_down", extras=(x2, target),
                            epilogue=functools.partial(_ep_loss, width=D_MODEL), sums=[(1, 1)])
    loss = lax.psum(loss_part[0, 0], ("x", "y", "c"))

    def row_shards(a):
        r, c = a.shape
        return a.reshape(N_DEV, r // N_DEV, c)

    g = {}
    g["w_down"] = _matmul(usq, dy, mode="tn", out_dtype=GRAD_WIRE, tm=1024, tn=1024, tk=1024, name="mm_d_wdown")
    da = _matmul(dy, w_down, mode="nt", tm=1024, tn=1024, tk=1024, name="mm_d_usq", out_dtype=BF16, extras=(a,),
                 epilogue=lambda acc, av: (2.0 * jax.nn.relu(av) * acc,))
    g["w_up"] = _matmul(h3, da, mode="tn", out_shards=True, out_dtype=GRAD_WIRE, tm=1024, tn=512, tk=1024, name="mm_d_wup")
    sent_mlp, token = _exchange_start([row_shards(g["w_down"]), g["w_up"]], scatter=True,
                                      name="grads_start_mlp")
    dx2, g["g_mlp"] = _matmul(da, w_up, mode="nt", b_shards=True, tm=1024, tn=1024, tk=512, name="mm_d_h3",
                              extras=(x2, dy), row_params=(pin(p["g_mlp"], token),), epilogue=_ep_rms_bwd,
                              sums=[(1, D_MODEL)])

    g["xo_w"] = _matmul(o2, dx2, mode="tn", out_dtype=GRAD_WIRE, tm=1024, tn=1024, tk=1024, name="mm_d_wxo")
    do2 = _matmul(dx2, xo_w, mode="nt", tm=1024, tn=512, tk=1024, name="mm_d_o2")
    xa = _rw_bwd(_xattn_fn, xa_rows, xa_pars, [_whole(do2)], tm=TM, name="xattn_bwd", row_grads=[BF16] * X_HEADS)
    dq2 = jnp.concatenate(xa[:X_HEADS], axis=1)
    dkv = jnp.concatenate(xa[X_HEADS:3 * X_HEADS], axis=1)
    g["xg_q"], g["xg_k"] = xa[3 * X_HEADS], xa[3 * X_HEADS + 1]
    g["xq_w"] = _matmul(h2, dq2, mode="tn", out_dtype=GRAD_WIRE, tm=1024, tn=1024, tk=1024, name="mm_d_wxq")
    dx1, g["g_xattn"] = _matmul(dq2, xq_w, mode="nt", tm=1024, tn=1024, tk=1024, name="mm_d_h2", extras=(x1, dx2),
                                row_params=(p["g_xattn"],), epilogue=_ep_rms_bwd, sums=[(1, D_MODEL)])
    g["xkv_w"] = _matmul(mem_n, dkv, mode="tn", out_shards=True, out_dtype=GRAD_WIRE, tm=1024, tn=256, tk=256,
                         name="mm_d_wxkv")
    dmem_n = _matmul(dkv, xkv_w, mode="nt", b_shards=True, tm=256, tn=1024, tk=256, name="mm_d_memn")
    g["g_mem"] = _rw_bwd(_rms_fn, [_whole(mem)], [_whole(p["g_mem"])], [_whole(dmem_n)], tm=TM, name="rms_mem_bwd",
                         row_grads=[None])[0]

    g["w_out"] = _matmul(mixed, dx1, mode="tn", out_dtype=GRAD_WIRE, tm=1024, tn=1024, tk=1024, name="mm_d_wout")
    sent_mid, token = _exchange_start(
        [row_shards(g["w_out"]), row_shards(g["xq_w"]), g["xkv_w"], row_shards(g["xo_w"])], scatter=True,
        name="grads_start_mid")
    dmixed = _matmul(dx1, w_out, mode="nt", tm=1024, tn=1024, tk=1024, name="mm_d_mixed")
    dq, dk, dv, dcum4, dgain = _fox_bwd(proj, C_Q, C_K, C_V, pin(gq2, token), gk2, cum_col, cum_row3, lse, dmixed, 1024)
    gains = _fold_gains(dgain)
    g["g_q"], g["g_k"] = gains[0:1, :ATT_D], gains[1:2, :ATT_D]
    dy_ssd, dxs_g, dz, g["d_skip"], g["ssm_norm_w"] = _rw_bwd(
        _gate_fn, gate_rows, gate_pars, [(dmixed, 0, 1024)], tm=TM, name="gate_bwd", row_grads=[F32, F32, BF16])
    dxs_s, ddt_col, dacs_col, dacs_row, d_b, d_c = _ssd_bwd(xs_c, dt_col, acs_col, acs_t, bc_c, hs, dy_ssd)
    dcum_t = dcum4[:, :, 0:2, :].transpose(0, 2, 1, 3).reshape(16, S)
    ddtf_t, ddtb, dalog, dfb = _dtf_bwd(dtf_t, dt_bias, a_log, f_bias, ddt_col.T, dacs_col.T, dacs_row, dcum_t)
    g["dt_bias"], g["a_log"], g["f_bias"] = ddtb, dalog, dfb
    dxs_raw, dcw_xs, dcb_xs = _conv_bwd(proj, C_XS, 1024, cw_xs, cb_xs, [dxs_s, dxs_g], name="conv_xs_bwd")
    dbc_raw, dcw_bc, dcb_bc = _conv_bwd(proj, C_B, 512, cw_bc, cb_bc, [jnp.concatenate([d_b, d_c], axis=1)],
                                        name="conv_bc_bwd")
    g["conv_w"] = jnp.concatenate([dcw_xs, dcw_bc], axis=1)
    g["conv_b"] = jnp.concatenate([dcb_xs, dcb_bc], axis=1)
    ddtf = jnp.pad(ddtf_t.T.astype(BF16), ((0, 0), (0, P_COLS - C_DTF - 32)))
    dproj = jnp.concatenate([dz, dxs_raw, dq, dk, dv, dbc_raw, ddtf], axis=1)
    dw_in_p = _matmul(dproj, h1, mode="tn", out_dtype=GRAD_WIRE, tm=640, tn=1024, tk=1024, name="mm_d_win")
    g["w_in"] = _move_rows(dw_in_p, [(dst, n, lo) for lo, n, dst in IN_ROW_MOVES], O_END, name="d_w_in_rows")
    half = N_DEV // 2
    send_in = row_shards(g["w_in"])
    pair, token = _copies_start([(send_in, lax.empty((half,) + send_in.shape[1:], send_in.dtype), _plan_pair)],
                                name="grads_in_pair_start")
    send_in, from_sibling = _copies_wait(pair, token, name="grads_in_pair_wait")[0]
    mine = jnp.stack([lax.dynamic_index_in_dim(send_in, me ^ (2 * j), 0, keepdims=False) for j in range(half)])
    chip_sums = _pair_sum(mine, from_sibling, name="grads_in_pair_sum")
    sent_in, token = _copies_start([(chip_sums, lax.empty(chip_sums.shape, chip_sums.dtype), _plan_chips)],
                                   name="grads_in_chip_start")
    grad_x, g["g_mix"] = _matmul(dproj, w_in_t, mode="nn", tm=1024, tn=1024, tk=1152, name="mm_d_h1", extras=(x, dx1),
                                 row_params=(pin(p["g_mix"], token),), epilogue=_ep_rms_bwd, sums=[(1, D_MODEL)])
    sent_small, _ = _exchange_start([_pack_small(g)], scatter=False, name="small_grads_start")

    grads, delta, new_m, new_v = {}, {}, {}, {}

    def update(names, sent, after, wait_name):
        parts = landed_with_own(_copies_wait(sent, after, name=wait_name), True)
        for name, part in zip(names, parts, strict=True):
            grads[name], delta[name], new_m[name], new_v[name] = _reduce_adamw(part, p[name], m[name], v[name], tr=128,
                                                                                name="adamw_" + name)

    update(("w_down", "w_up"), sent_mlp, grad_x, "grads_wait_mlp")
    update(("w_out", "xq_w", "xkv_w", "xo_w"), sent_mid, delta["w_up"], "grads_wait_mid")
    chip_sums, landed = _copies_wait(sent_in, delta["xo_w"], name="grads_in_chip_wait")[0]
    part = lax.dynamic_update_slice(landed, chip_sums[0:1], (0, 0, 0))
    res = _reduce_adamw(part, w_in_own, m_in_own, v_in_own, tr=part.shape[1], tc=256, name="adamw_w_in")
    grads["w_in"], delta["w_in"], new_m["w_in"], new_v["w_in"] = [r.T for r in res]
    small_parts = landed_with_own(_copies_wait(sent_small, delta["w_in"], name="small_grads_wait"), False)[0]
    zeros_cw = jnp.zeros((4, 1536), F32)
    slabs = [_pack_small({**d, "conv_w": zeros_cw}) for d in (p, m, v)]
    sg, sd, sm, sv = _reduce_adamw(small_parts, *slabs, tr=SLAB_ROWS, name="adamw_small")
    for dst, slab in ((grads, sg), (delta, sd), (new_m, sm), (new_v, sv)):
        for name, flat in _unpack_small(slab).items():
            if name != "conv_w":
                dst[name] = flat.reshape(p[name].shape)
    cw_shard = p["conv_w"].shape[1]
    grads["conv_w"] = lax.dynamic_slice(_unpack_small(sg)["conv_w"].reshape(4, 1536), (0, me * cw_shard), (4, cw_shard))
    delta["conv_w"], new_m["conv_w"], new_v["conv_w"] = _adamw(p["conv_w"], grads["conv_w"], m["conv_w"], v["conv_w"],
                                                               name="adamw_conv_w")
    return loss, grad_x, grads, delta, new_m, new_v


def kernel(x, mem, g_mix, w_in, conv_w, conv_b, dt_bias, a_log, d_skip, ssm_norm_w, g_q, g_k, f_bias, w_out, g_xattn, g_mem, xq_w, xkv_w, xg_q, xg_k, xo_w, g_mlp, w_up, w_down, loss_target, m_g_mix, m_w_in, m_conv_w, m_conv_b, m_dt_bias, m_a_log, m_d_skip, m_ssm_norm_w, m_g_q, m_g_k, m_f_bias, m_w_out, m_g_xattn, m_g_mem, m_xq_w, m_xkv_w, m_xg_q, m_xg_k, m_xo_w, m_g_mlp, m_w_up, m_w_down, v_g_mix, v_w_in, v_conv_w, v_conv_b, v_dt_bias, v_a_log, v_d_skip, v_ssm_norm_w, v_g_q, v_g_k, v_f_bias, v_w_out, v_g_xattn, v_g_mem, v_xq_w, v_xkv_w, v_xg_q, v_xg_k, v_xo_w, v_g_mlp, v_w_up, v_w_down):
    args = locals()
    drop = lambda t: t[0] if t.ndim == 3 else t
    p = {n: drop(args[n]) for n in WEIGHTS}
    m = {n: drop(args["m_" + n]) for n in WEIGHTS}
    v = {n: drop(args["v_" + n]) for n in WEIGHTS}
    loss, grad_x, grads, delta, new_m, new_v = _step(p, m, v, x[0], mem[0], loss_target[0])
    outs = [loss, grad_x[None]]
    for d in (grads, delta, new_m, new_v):
        outs += [d[n].reshape(args[n].shape) for n in WEIGHTS]
    return tuple(outs)
```

```python
import functools
import math

import jax
import jax.numpy as jnp
from jax import lax
from jax.experimental import pallas as pl
from jax.experimental.pallas import tpu as pltpu

F32, BF16 = jnp.float32, jnp.bfloat16
SDS = jax.ShapeDtypeStruct
HI = lax.Precision.HIGHEST
MESH = pl.DeviceIdType.MESH

N_DEV = 8
EPS = 1e-5
D_MODEL = 1024
SSM_HEADS, SSM_P, SSM_N, SSM_GROUPS, CHUNK = 16, 64, 128, 2, 128
ATT_HEADS, ATT_D = 16, 64
X_HEADS, X_D = 4, 256
LANES = 128
VMEM_LIMIT = 48 * 1024 * 1024
NEG = -1e30

GRAD_WIRE = BF16
ADAM_LR, ADAM_B1, ADAM_B2, ADAM_EPS, ADAM_WD, ADAM_STEP = 0.001, 0.9, 0.999, 1e-08, 0.01, 10

C_Z, C_XS, C_Q, C_K, C_V, C_B, C_C, C_DTF, P_COLS = 0, 1024, 2048, 3072, 4096, 5120, 5376, 5632, 5760

_NN = (((1,), (0,)), ((), ()))
_NT = (((1,), (1,)), ((), ()))
_TN = (((0,), (0,)), ((), ()))


def _cparams(**kw):
    return pltpu.CompilerParams(vmem_limit_bytes=VMEM_LIMIT, **kw)


def _bdot(a, b, dn):
    return lax.dot_general(a.astype(BF16), b.astype(BF16), dn, preferred_element_type=F32)


@jax.custom_vjp
def mm_nn(a, b):
    return _bdot(a, b, _NN)


mm_nn.defvjp(lambda a, b: (mm_nn(a, b), (a, b)), lambda r, g: (_bdot(g, r[1], _NT), _bdot(r[0], g, _TN)))


@jax.custom_vjp
def mm_nt(a, b):
    return _bdot(a, b, _NT)


mm_nt.defvjp(lambda a, b: (mm_nt(a, b), (a, b)), lambda r, g: (_bdot(g, r[1], _NN), _bdot(g, r[0], _TN)))


@jax.custom_vjp
def mm_tn(a, b):
    return _bdot(a, b, _TN)


mm_tn.defvjp(lambda a, b: (mm_tn(a, b), (a, b)), lambda r, g: (_bdot(r[1], g, _NT), _bdot(r[0], g, _NN)))


def _cdot(x, c):
    return jnp.dot(x, c, precision=HI, preferred_element_type=F32)


def _iota(shape, dim):
    return lax.broadcasted_iota(jnp.int32, shape, dim)


def _matmul(a, b, *, mode, tm, tn, tk, name, out_dtype=F32, add=None, extras=(), epilogue=None, out_dtypes=None,
            b_shards=False, out_shards=False, row_params=(), sums=()):
    if mode == "tn":
        K, M = a.shape
    else:
        M, K = a.shape
    if b_shards:
        N = b.shape[1] if mode == "nt" else b.shape[0] * b.shape[2]
        tn, tk = (tn, b.shape[2]) if mode == "nt" else (b.shape[2], tk)
    else:
        N = b.shape[0] if mode == "nt" else b.shape[1]
    tm, tn, tk = min(tm, M), min(tn, N), min(tk, K)
    assert M % tm == 0 and N % tn == 0 and K % tk == 0, (name, M, N, K, tm, tn, tk)
    assert not b_shards or (K // tk if mode == "nt" else N // tn) == b.shape[0], name
    assert not (out_shards and (extras or add is not None)), name
    nk = K // tk
    dn = {"nn": _NN, "nt": _NT, "tn": _TN}[mode]
    if add is not None:
        extras, epilogue = (add,), lambda acc, r: (acc + r,)
    elif epilogue is None:
        epilogue = lambda acc: (acc,)
    out_dtypes = out_dtypes or [out_dtype]
    ne, no, ns = len(extras) + len(row_params), len(out_dtypes), len(sums)
    assert all(s == (1, 1) or (s == (1, N) and tn == N) for s in sums), name

    def body(*refs):
        a_ref, b_ref = refs[:2]
        e_refs, o_refs, s_refs = refs[2:2 + ne], refs[2 + ne:2 + ne + no], refs[2 + ne + no:2 + ne + no + ns]

        def finish(acc):
            res = epilogue(acc, *[e[...] for e in e_refs])
            for o_ref, v in zip(o_refs, res[:no], strict=True):
                o_ref[...] = v.astype(o_ref.dtype)
            first_tile = jnp.logical_and(pl.program_id(0) == 0, pl.program_id(1) == 0)
            for s_ref, v in zip(s_refs, res[no:], strict=True):
                @pl.when(first_tile)
                def _(s_ref=s_ref, v=v):
                    s_ref[...] = v

                @pl.when(jnp.logical_not(first_tile))
                def _(s_ref=s_ref, v=v):
                    s_ref[...] += v

        prod = _bdot(a_ref[...], b_ref[...], dn)
        if nk == 1:
            finish(prod)
            return
        acc_ref = refs[-1]
        k = pl.program_id(2)

        @pl.when(k == 0)
        def _():
            acc_ref[...] = prod

        @pl.when(jnp.logical_and(k > 0, k < nk - 1))
        def _():
            acc_ref[...] += prod

        @pl.when(k == nk - 1)
        def _():
            finish(acc_ref[...] + prod)

    a_spec = pl.BlockSpec((tk, tm), lambda i, j, k: (k, i)) if mode == "tn" else pl.BlockSpec((tm, tk), lambda i, j, k: (i, k))
    if b_shards and mode == "nt":
        b_spec = pl.BlockSpec((None, tn, tk), lambda i, j, k: (k, j, 0))
    elif b_shards:
        b_spec = pl.BlockSpec((None, tk, tn), lambda i, j, k: (j, k, 0))
    elif mode == "nt":
        b_spec = pl.BlockSpec((tn, tk), lambda i, j, k: (j, k))
    else:
        b_spec = pl.BlockSpec((tk, tn), lambda i, j, k: (k, j))
    if out_shards:
        o_spec, o_shape = pl.BlockSpec((None, tm, tn), lambda i, j, k: (j, i, 0)), (N // tn, M, tn)
    else:
        o_spec, o_shape = pl.BlockSpec((tm, tn), lambda i, j, k: (i, j)), (M, N)
    row_spec = pl.BlockSpec((1, tn), lambda i, j, k: (0, j))
    sum_specs = [pl.BlockSpec(s, lambda i, j, k: (0, 0)) for s in sums]
    res = pl.pallas_call(
        body, name=name, grid=(M // tm, N // tn, nk),
        in_specs=[a_spec, b_spec] + [o_spec] * len(extras) + [row_spec] * len(row_params),
        out_specs=[o_spec] * no + sum_specs, out_shape=[SDS(o_shape, dt) for dt in out_dtypes] + [SDS(s, F32) for s in sums],
        scratch_shapes=[pltpu.VMEM((tm, tn), F32)] if nk > 1 else [],
        compiler_params=_cparams(dimension_semantics=(("arbitrary",) * 3 if sums else ("parallel", "parallel", "arbitrary"))),
    )(a, b, *extras, *row_params)
    return res[0] if no + ns == 1 else res


def _row_spec(tm, spec):
    _, c0, w = spec
    assert c0 % w == 0
    return pl.BlockSpec((tm, w), functools.partial(lambda i, cb: (i, cb), cb=c0 // w))


def _par_spec(spec):
    arr, c0, w = spec
    assert c0 % w == 0
    return pl.BlockSpec((arr.shape[0], w), functools.partial(lambda i, cb: (0, cb), cb=c0 // w))


def _whole(arr):
    return (arr, 0, arr.shape[1])


def _rw_fwd(fn, rows, params, outs, *, tm, name):
    M = rows[0][0].shape[0]
    nr, npar = len(rows), len(params)

    def body(*refs):
        rv = [r[...].astype(F32) for r in refs[:nr]]
        pv = [p[...].astype(F32) for p in refs[nr:nr + npar]]
        res = fn(*rv, *pv)
        for o_ref, v in zip(refs[nr + npar:], res, strict=True):
            o_ref[...] = v.astype(o_ref.dtype)

    return pl.pallas_call(
        body, name=name, grid=(M // tm,),
        in_specs=[_row_spec(tm, r) for r in rows] + [_par_spec(p) for p in params],
        out_specs=[pl.BlockSpec((tm, w), lambda i: (i, 0)) for w, _ in outs],
        out_shape=[SDS((M, w), dt) for w, dt in outs],
        compiler_params=_cparams(dimension_semantics=("parallel",)),
    )(*[r[0] for r in rows], *[p[0] for p in params])


def _rw_bwd(fn, rows, params, cts, *, tm, name, row_grads, adds=None):
    M = rows[0][0].shape[0]
    adds = adds or {}
    nr, npar, nc = len(rows), len(params), len(cts)
    add_keys = sorted(adds)
    want = [k for k in range(nr) if row_grads[k] is not None]

    def body(*refs):
        pos = 0
        r_refs = refs[pos:pos + nr]; pos += nr
        p_refs = refs[pos:pos + npar]; pos += npar
        c_refs = refs[pos:pos + nc]; pos += nc
        a_refs = dict(zip(add_keys, refs[pos:pos + len(add_keys)])); pos += len(add_keys)
        dr_refs = dict(zip(want, refs[pos:pos + len(want)])); pos += len(want)
        dp_refs = refs[pos:pos + npar]
        rv = [r[...].astype(F32) for r in r_refs]
        pv = [p[...].astype(F32) for p in p_refs]
        _, vjp = jax.vjp(fn, *rv, *pv)
        g = vjp(tuple(c[...].astype(F32) for c in c_refs))
        for k in want:
            v = g[k]
            if k in a_refs:
                v = v + a_refs[k][...].astype(F32)
            dr_refs[k][...] = v.astype(dr_refs[k].dtype)
        first = pl.program_id(0) == 0
        for j in range(npar):
            @pl.when(first)
            def _(j=j):
                dp_refs[j][...] = jnp.zeros_like(dp_refs[j])
            dp_refs[j][...] += g[nr + j]

    res = pl.pallas_call(
        body, name=name, grid=(M // tm,),
        in_specs=([_row_spec(tm, r) for r in rows] + [_par_spec(p) for p in params] + [_row_spec(tm, c) for c in cts]
                  + [_row_spec(tm, adds[k]) for k in add_keys]),
        out_specs=([pl.BlockSpec((tm, rows[k][2]), lambda i: (i, 0)) for k in want]
                   + [pl.BlockSpec((p[0].shape[0], p[2]), lambda i: (0, 0)) for p in params]),
        out_shape=([SDS((M, rows[k][2]), row_grads[k]) for k in want] + [SDS((p[0].shape[0], p[2]), F32) for p in params]),
        compiler_params=_cparams(dimension_semantics=("arbitrary",)),
    )(*[r[0] for r in rows], *[p[0] for p in params], *[c[0] for c in cts], *[adds[k][0] for k in add_keys])
    return res


def _rms_fn(x, g):
    r = lax.rsqrt(jnp.mean(x * x, axis=-1, keepdims=True) + EPS)
    return (x * r * g,)


def _ep_residual_rms(acc, res, g):
    x = acc + res
    return x, _rms_fn(x, g)[0]


def _ep_rms_bwd(dh, x, dres, g):
    r = lax.rsqrt(jnp.mean(x * x, axis=-1, keepdims=True) + EPS)
    t = dh * g
    dx = dres + r * (t - x * (r * r) * jnp.mean(t * x, axis=-1, keepdims=True))
    return dx, jnp.sum(dh * x * r, axis=0, keepdims=True)


def _ep_loss(acc, res, target, *, width):
    e = acc + res - target
    return e * (1.0 / width), jnp.sum(jnp.sum(e * e, axis=1, keepdims=True), axis=0, keepdims=True) * (0.5 / width)


def _seg_mats(width, seg):
    n = width // seg
    p = (_iota((width, n), 0) // seg == _iota((width, n), 1)).astype(F32)
    e = (_iota((n, width), 1) // seg == _iota((n, width), 0)).astype(F32)
    return p, e


def _gate_fn(y, xs, z, dskip, w):
    width = SSM_HEADS * SSM_P
    _, e = _seg_mats(width, SSM_P)
    y = (y + _cdot(dskip, e) * xs) * (z * jax.nn.sigmoid(z))
    g0 = _iota((1, width), 1) < width // SSM_GROUPS
    y2 = y * y
    gw = width // SSM_GROUPS
    ms0 = jnp.sum(jnp.where(g0, y2, 0.0), axis=-1, keepdims=True) * (1.0 / gw)
    ms1 = jnp.sum(jnp.where(g0, 0.0, y2), axis=-1, keepdims=True) * (1.0 / gw)
    r = jnp.where(g0, lax.rsqrt(ms0 + EPS), lax.rsqrt(ms1 + EPS))
    return (y * r * w,)


def _xattn_fn(q0, q1, q2, q3, k0, k1, k2, k3, v0, v1, v2, v3, gq, gk):
    def norm(u, g):
        return u * lax.rsqrt(jnp.mean(u * u, axis=-1, keepdims=True) + EPS) * g
    outs = []
    for q, k, v in ((q0, k0, v0), (q1, k1, v1), (q2, k2, v2), (q3, k3, v3)):
        s = mm_nt(norm(q, gq), norm(k, gk)) * (X_D ** -0.5)
        p = jnp.exp(s - lax.stop_gradient(jnp.max(s, axis=-1, keepdims=True)))
        p = p / jnp.sum(p, axis=-1, keepdims=True)
        outs.append(mm_nn(p, v))
    return (jnp.concatenate(outs, axis=-1),)


CONV_TC = 256


def _shift_down(u, k):
    if k == 0:
        return u
    return jnp.where(_iota(u.shape, 0) >= k, pltpu.roll(u, k, axis=0), 0.0)


def _shift_up(u, k):
    if k == 0:
        return u
    n = u.shape[0]
    return jnp.where(_iota(u.shape, 0) < n - k, pltpu.roll(u, n - k, axis=0), 0.0)


def _conv_pre(u, w_ref, b):
    pre = b + w_ref[3:4, :] * u
    for k in (1, 2, 3):
        pre = pre + w_ref[3 - k:4 - k, :] * _shift_down(u, k)
    return pre


def _conv_fwd(src, c0, width, w, b, *, name):
    S = src.shape[0]
    cb0 = c0 // CONV_TC

    def body(u_ref, w_ref, b_ref, o_ref):
        pre = _conv_pre(u_ref[...], w_ref, b_ref[...])
        o_ref[...] = pre * jax.nn.sigmoid(pre)

    return pl.pallas_call(
        body, name=name, grid=(width // CONV_TC,),
        in_specs=[pl.BlockSpec((S, CONV_TC), lambda j: (0, cb0 + j)), pl.BlockSpec((4, CONV_TC), lambda j: (0, j)),
                  pl.BlockSpec((1, CONV_TC), lambda j: (0, j))],
        out_specs=pl.BlockSpec((S, CONV_TC), lambda j: (0, j)), out_shape=SDS((S, width), F32),
        compiler_params=_cparams(dimension_semantics=("parallel",)),
    )(src, w, b)


def _conv_bwd(src, c0, width, w, b, douts, *, name):
    S = src.shape[0]
    cb0 = c0 // CONV_TC
    nd = len(douts)

    def body(*refs):
        u_ref, w_ref, b_ref = refs[:3]
        d_refs = refs[3:3 + nd]
        du_ref, dw_ref, db_ref = refs[3 + nd:]
        u = u_ref[...]
        pre = _conv_pre(u, w_ref, b_ref[...])
        sg = jax.nn.sigmoid(pre)
        dout = d_refs[0][...]
        for r in d_refs[1:]:
            dout = dout + r[...]
        dpre = dout * (sg * (1.0 + pre * (1.0 - sg)))
        du = w_ref[3:4, :] * dpre
        dw_ref[3:4, :] = jnp.sum(dpre * u, axis=0, keepdims=True)
        for k in (1, 2, 3):
            du = du + w_ref[3 - k:4 - k, :] * _shift_up(dpre, k)
            dw_ref[3 - k:4 - k, :] = jnp.sum(dpre * _shift_down(u, k), axis=0, keepdims=True)
        du_ref[...] = du.astype(du_ref.dtype)
        db_ref[...] = jnp.sum(dpre, axis=0, keepdims=True)

    return pl.pallas_call(
        body, name=name, grid=(width // CONV_TC,),
        in_specs=[pl.BlockSpec((S, CONV_TC), lambda j: (0, cb0 + j)), pl.BlockSpec((4, CONV_TC), lambda j: (0, j)),
                  pl.BlockSpec((1, CONV_TC), lambda j: (0, j))] + [pl.BlockSpec((S, CONV_TC), lambda j: (0, j))] * nd,
        out_specs=[pl.BlockSpec((S, CONV_TC), lambda j: (0, j)), pl.BlockSpec((4, CONV_TC), lambda j: (0, j)),
                   pl.BlockSpec((1, CONV_TC), lambda j: (0, j))],
        out_shape=[SDS((S, width), BF16), SDS((4, width), F32), SDS((1, width), F32)],
        compiler_params=_cparams(dimension_semantics=("parallel",)),
    )(src, w, b, *douts)


def _softplus(x):
    return jnp.maximum(x, 0.0) + jnp.log(1.0 + jnp.exp(-jnp.abs(x)))


def _prefix_sum(x, seg):
    n = x.shape[1]
    pos = _iota(x.shape, 1) % seg
    k = 1
    while k < seg:
        x = x + jnp.where(pos >= k, pltpu.roll(x, k, axis=1), 0.0)
        k *= 2
    return x


def _suffix_sum(x, seg):
    n = x.shape[1]
    pos = _iota(x.shape, 1) % seg
    k = 1
    while k < seg:
        x = x + jnp.where(pos + k < seg, pltpu.roll(x, n - k, axis=1), 0.0)
        k *= 2
    return x


def _dtf_fwd(dtf_t, dt_bias, a_log, f_bias):
    S = dtf_t.shape[1]

    def body(x_ref, db_ref, al_ref, fb_ref, dt_ref, acs_ref, cum_ref):
        dt = _softplus(x_ref[0:16, :] + db_ref[...])
        dt_ref[...] = dt
        acs_ref[...] = _prefix_sum(dt * (-jnp.exp(al_ref[...])), CHUNK)
        cum_ref[...] = _prefix_sum(-_softplus(-(x_ref[16:32, :] + fb_ref[...])), S)

    return pl.pallas_call(body, name="dtf_fwd", out_shape=[SDS((16, S), F32)] * 3, compiler_params=_cparams())(
        dtf_t, dt_bias, a_log, f_bias)


def _dtf_bwd(dtf_t, dt_bias, a_log, f_bias, d_dt, d_acs_a, d_acs_b, d_cum):
    S = dtf_t.shape[1]

    def body(x_ref, db_ref, al_ref, fb_ref, ddt_ref, da1_ref, da2_ref, dc_ref, dx_ref, ddb_ref, dal_ref, dfb_ref):
        xd = x_ref[0:16, :] + db_ref[...]
        dt = _softplus(xd)
        a = -jnp.exp(al_ref[...])
        d_da = _suffix_sum(da1_ref[...] + da2_ref[...], CHUNK)
        d_dt = ddt_ref[...] + d_da * a
        dal_ref[...] = jnp.sum(d_da * dt, axis=1, keepdims=True) * a
        d_xd = d_dt * jax.nn.sigmoid(xd)
        ddb_ref[...] = jnp.sum(d_xd, axis=1, keepdims=True)
        xf = x_ref[16:32, :] + fb_ref[...]
        d_xf = _suffix_sum(dc_ref[...], S) * jax.nn.sigmoid(-xf)
        dfb_ref[...] = jnp.sum(d_xf, axis=1, keepdims=True)
        dx_ref[0:16, :] = d_xd
        dx_ref[16:32, :] = d_xf

    return pl.pallas_call(body, name="dtf_bwd", out_shape=[SDS((32, S), F32)] + [SDS((16, 1), F32)] * 3,
                          compiler_params=_cparams())(dtf_t, dt_bias, a_log, f_bias, d_dt, d_acs_a, d_acs_b, d_cum)


SSM_PAIRS = SSM_HEADS // 2 // SSM_GROUPS


def _ssd_pair(xs, dtc, acol, arow, bm, cm, cbm, h, hp):
    L = CHUNK
    first = _iota((1, LANES), 1) < SSM_P
    i16, s16 = _iota((L, 16), 1), _iota((16, L), 0)
    ha, hb = 2 * hp, 2 * hp + 1

    def selc(blk, hh):
        return jnp.sum(jnp.where(i16 == hh, blk, 0.0), axis=1, keepdims=True)

    def selr(blk, hh):
        return jnp.sum(jnp.where(s16 == hh, blk, 0.0), axis=0, keepdims=True)

    x = xs * jnp.where(first, selc(dtc, ha), selc(dtc, hb))
    ca, cb, ra, rb = selc(acol, ha), selc(acol, hb), selr(arow, ha), selr(arow, hb)
    tri = _iota((L, L), 0) >= _iota((L, L), 1)
    la = jnp.exp(jnp.where(tri, ca - ra, NEG))
    lb = jnp.exp(jnp.where(tri, cb - rb, NEG))
    y = jnp.where(first, mm_nn(cbm * la, x), mm_nn(cbm * lb, x))
    y = y + jnp.where(first, jnp.exp(ca), jnp.exp(cb)) * mm_nn(cm, h)
    last = _iota((1, L), 1) == L - 1
    ala = jnp.sum(jnp.where(last, ra, 0.0), axis=1, keepdims=True)
    alb = jnp.sum(jnp.where(last, rb, 0.0), axis=1, keepdims=True)
    dec = jnp.where(first, jnp.exp(ala - ca), jnp.exp(alb - cb))
    hn = jnp.where(first, jnp.exp(ala), jnp.exp(alb)) * h + mm_tn(bm, x * dec)
    return y, hn


def _ssd_group(*args, grp):
    xs, (dtc, acol, arow, bm, cm), hs = args[:SSM_PAIRS], args[SSM_PAIRS:SSM_PAIRS + 5], args[SSM_PAIRS + 5:]
    cbm = mm_nt(cm, bm)
    res = [_ssd_pair(xs[j], dtc, acol, arow, bm, cm, cbm, hs[j], SSM_PAIRS * grp + j) for j in range(SSM_PAIRS)]
    return tuple(r[0] for r in res) + tuple(r[1] for r in res)


def _ssd_specs(nc, rev):
    L = CHUNK
    cidx = (lambda c: nc - 1 - c) if rev else (lambda c: c)
    return dict(
        xs=pl.BlockSpec((L, SSM_PAIRS * LANES), lambda c, g: (cidx(c), g)),
        col=pl.BlockSpec((L, 16), lambda c, g: (cidx(c), 0)),
        row=pl.BlockSpec((16, L), lambda c, g: (0, cidx(c))),
        b=pl.BlockSpec((L, SSM_N), lambda c, g: (cidx(c), g)),
        c=pl.BlockSpec((L, SSM_N), lambda c, g: (cidx(c), SSM_GROUPS + g)),
        st=pl.BlockSpec((1, SSM_PAIRS, SSM_N, LANES), lambda c, g: (cidx(c), g, 0, 0)),
    )


def _lane_pieces(v):
    return [v[:, LANES * j:LANES * (j + 1)] for j in range(v.shape[1] // LANES)]


def _ssd_fwd(xs, dt_col, acs_col, acs_row, bc):
    S = xs.shape[0]
    nc, nhp = S // CHUNK, SSM_HEADS // 2
    sp = _ssd_specs(nc, False)

    def body(xs_ref, dt_ref, ac_ref, ar_ref, b_ref, c_ref, y_ref, hs_ref, h_scr):
        c, g = pl.program_id(0), pl.program_id(1)

        @pl.when(c == 0)
        def _():
            for j in range(SSM_PAIRS):
                h_scr[SSM_PAIRS * g + j] = jnp.zeros((SSM_N, LANES), F32)

        hs = [h_scr[SSM_PAIRS * g + j] for j in range(SSM_PAIRS)]
        for j in range(SSM_PAIRS):
            hs_ref[0, j] = hs[j]
        res = _ssd_group(*_lane_pieces(xs_ref[...]), dt_ref[...], ac_ref[...], ar_ref[...], b_ref[...], c_ref[...], *hs,
                         grp=g)
        y_ref[...] = jnp.concatenate(res[:SSM_PAIRS], axis=1)
        for j in range(SSM_PAIRS):
            h_scr[SSM_PAIRS * g + j] = res[SSM_PAIRS + j]

    return pl.pallas_call(
        body, name="ssd_fwd", grid=(nc, SSM_GROUPS),
        in_specs=[sp["xs"], sp["col"], sp["col"], sp["row"], sp["b"], sp["c"]],
        out_specs=[sp["xs"], sp["st"]],
        out_shape=[SDS((S, SSM_HEADS * SSM_P), F32), SDS((nc, nhp, SSM_N, LANES), F32)],
        scratch_shapes=[pltpu.VMEM((nhp, SSM_N, LANES), F32)],
        compiler_params=_cparams(dimension_semantics=("arbitrary", "arbitrary")),
    )(xs, dt_col, acs_col, acs_row, bc, bc)


def _ssd_bwd(xs, dt_col, acs_col, acs_row, bc, hs, dy):
    S = xs.shape[0]
    nc, nhp = S // CHUNK, SSM_HEADS // 2
    sp = _ssd_specs(nc, True)

    def body(xs_ref, dt_ref, ac_ref, ar_ref, b_ref, c_ref, hs_ref, dy_ref,
             dxs_ref, ddt_ref, dac_ref, dar_ref, db_ref, dc_ref, dh_scr):
        c, g = pl.program_id(0), pl.program_id(1)

        @pl.when(c == 0)
        def _():
            for j in range(SSM_PAIRS):
                dh_scr[SSM_PAIRS * g + j] = jnp.zeros((SSM_N, LANES), F32)

        _, vjp = jax.vjp(functools.partial(_ssd_group, grp=g), *_lane_pieces(xs_ref[...]), dt_ref[...], ac_ref[...],
                         ar_ref[...], b_ref[...], c_ref[...], *[hs_ref[0, j] for j in range(SSM_PAIRS)])
        grads = vjp(tuple(_lane_pieces(dy_ref[...])) + tuple(dh_scr[SSM_PAIRS * g + j] for j in range(SSM_PAIRS)))
        dxs_ref[...] = jnp.concatenate(grads[:SSM_PAIRS], axis=1)
        ddt, dac, dar, db, dc = grads[SSM_PAIRS:SSM_PAIRS + 5]
        for j in range(SSM_PAIRS):
            dh_scr[SSM_PAIRS * g + j] = grads[SSM_PAIRS + 5 + j]
        db_ref[...] = db
        dc_ref[...] = dc

        @pl.when(g == 0)
        def _():
            ddt_ref[...] = ddt
            dac_ref[...] = dac
            dar_ref[...] = dar

        @pl.when(g > 0)
        def _():
            ddt_ref[...] += ddt
            dac_ref[...] += dac
            dar_ref[...] += dar

    return pl.pallas_call(
        body, name="ssd_bwd", grid=(nc, SSM_GROUPS),
        in_specs=[sp["xs"], sp["col"], sp["col"], sp["row"], sp["b"], sp["c"], sp["st"], sp["xs"]],
        out_specs=[sp["xs"], sp["col"], sp["col"], sp["row"], sp["b"], sp["b"]],
        out_shape=[SDS((S, SSM_HEADS * SSM_P), F32), SDS((S, 16), F32), SDS((S, 16), F32), SDS((16, S), F32),
                   SDS((S, SSM_GROUPS * SSM_N), F32), SDS((S, SSM_GROUPS * SSM_N), F32)],
        scratch_shapes=[pltpu.VMEM((nhp, SSM_N, LANES), F32)],
        compiler_params=_cparams(dimension_semantics=("arbitrary", "arbitrary")),
    )(xs, dt_col, acs_col, acs_row, bc, bc, hs, dy)


ATT_T = 1024


def _pick_col(blk, h):
    return jnp.sum(jnp.where(_iota(blk.shape, 1) == h, blk, 0.0), axis=1, keepdims=True)


def _pick_row(blk, h):
    return jnp.sum(jnp.where(_iota(blk.shape, 0) == h, blk, 0.0), axis=0, keepdims=True)


def _pair_norm(x, g2, first):
    x2 = x * x
    sa = jnp.sum(jnp.where(first, x2, 0.0), axis=1, keepdims=True)
    sb = jnp.sum(jnp.where(first, 0.0, x2), axis=1, keepdims=True)
    r = jnp.where(first, lax.rsqrt(sa * (1.0 / ATT_D) + EPS), lax.rsqrt(sb * (1.0 / ATT_D) + EPS))
    return x * r * g2, r


def _pair_norm_bwd(dxn, x, r, g2, first):
    t = dxn * g2
    tx = t * x
    ma = jnp.sum(jnp.where(first, tx, 0.0), axis=1, keepdims=True)
    mb = jnp.sum(jnp.where(first, 0.0, tx), axis=1, keepdims=True)
    dx = r * (t - x * (r * r) * (jnp.where(first, ma, mb) * (1.0 / ATT_D)))
    return dx, jnp.sum(dxn * x * r, axis=0, keepdims=True)


def _fox_fwd(src, q_c0, k_c0, v_c0, gq2, gk2, cum_col, cum_row3):
    S = src.shape[0]
    T = ATT_T
    nq, nhp = S // T, ATT_HEADS // 2
    qb0, kb0, vb0 = q_c0 // LANES, k_c0 // LANES, v_c0 // LANES
    scale = ATT_D ** -0.5

    def body(q_ref, kraw_ref, v_ref, gq_ref, gk_ref, cc_ref, cr_ref, o_ref, l_ref, k_ref):
        hp, i = pl.program_id(0), pl.program_id(1)
        first = _iota((1, LANES), 1) < ATT_D

        @pl.when(i == 0)
        def _():
            k_ref[...] = _pair_norm(kraw_ref[...], gk_ref[...], first)[0].astype(BF16)

        H = T // 2

        def attend(tile, half):
            rows = pl.ds(half * H, H)
            row0 = tile * T + half * H
            klen = row0 + H
            q = (_pair_norm(q_ref[rows, :], gq_ref[...], first)[0] * scale).astype(BF16)
            zero = jnp.zeros_like(q)
            cc = cc_ref[rows, :]
            k = k_ref[0:klen, :]
            v = v_ref[0:klen, :].astype(BF16)
            allowed = _iota((H, klen), 0) + row0 >= _iota((H, klen), 1)
            outs, lses = [], []
            for hh in range(2):
                sel = first if hh == 0 else jnp.logical_not(first)
                ck = jnp.concatenate([_pick_row(cr_ref[j], 2 * hp + hh) for j in range(tile + 1)], axis=1)[:, :klen]
                s = _bdot(jnp.where(sel, q, zero), k, _NT) + (_pick_col(cc, 2 * hp + hh) - ck)
                s = jnp.where(allowed, s, NEG)
                m = jnp.max(s, axis=1, keepdims=True)
                p = jnp.exp(s - m)
                l = jnp.sum(p, axis=1, keepdims=True)
                outs.append(_bdot(p, v, _NN) / l)
                lses.append(m + jnp.log(l))
            o_ref[rows, :] = jnp.where(first, outs[0], outs[1]).astype(o_ref.dtype)
            l_ref[rows, :] = jnp.where(first, lses[0], lses[1])

        for tile in range(nq):
            @pl.when(i == tile)
            def _(tile=tile):
                for half in range(2):
                    attend(tile, half)

    gain = pl.BlockSpec((1, LANES), lambda hp, i: (0, 0))
    return pl.pallas_call(
        body, name="fox_fwd", grid=(nhp, nq),
        in_specs=[pl.BlockSpec((T, LANES), lambda hp, i: (i, qb0 + hp)), pl.BlockSpec((S, LANES), lambda hp, i: (0, kb0 + hp)),
                  pl.BlockSpec((S, LANES), lambda hp, i: (0, vb0 + hp)), gain, gain,
                  pl.BlockSpec((T, 16), lambda hp, i: (i, 0)), pl.BlockSpec((nq, 16, T), lambda hp, i: (0, 0, 0))],
        out_specs=[pl.BlockSpec((T, LANES), lambda hp, i: (i, hp))] * 2,
        out_shape=[SDS((S, ATT_HEADS * ATT_D), BF16), SDS((S, ATT_HEADS * ATT_D), F32)],
        scratch_shapes=[pltpu.VMEM((S, LANES), BF16)],
        compiler_params=_cparams(dimension_semantics=("arbitrary", "arbitrary")),
    )(src, src, src, gq2, gk2, cum_col, cum_row3)


def _fox_bwd(src, q_c0, k_c0, v_c0, gq2, gk2, cum_col, cum_row3, lse, dsrc, d_c0):
    S = src.shape[0]
    T = ATT_T
    nq, nhp = S // T, ATT_HEADS // 2
    qb0, kb0, vb0, db0 = q_c0 // LANES, k_c0 // LANES, v_c0 // LANES, d_c0 // LANES
    scale = ATT_D ** -0.5

    def body(q_ref, kraw_ref, v_ref, gq_ref, gk_ref, cc_ref, cr_ref, l_ref, do_ref,
             dq_ref, dk_ref, dv_ref, dc_ref, dg_ref, k_ref, dk_acc, dv_acc):
        hp, i = pl.program_id(0), pl.program_id(1)
        first = _iota((1, LANES), 1) < ATT_D

        @pl.when(i == 0)
        def _():
            k_ref[...] = _pair_norm(kraw_ref[...], gk_ref[...], first)[0].astype(BF16)
            dk_acc[...] = jnp.zeros_like(dk_acc)
            dv_acc[...] = jnp.zeros_like(dv_acc)
            dc_ref[...] = jnp.zeros_like(dc_ref)
            dg_ref[...] = jnp.zeros_like(dg_ref)

        H = T // 2

        def backprop(tile, half):
            rows = pl.ds(half * H, H)
            row0 = tile * T + half * H
            klen = row0 + H
            q_raw = q_ref[rows, :]
            qn, rq = _pair_norm(q_raw, gq_ref[...], first)
            q = (qn * scale).astype(BF16)
            zq = jnp.zeros_like(q)
            dob = do_ref[rows, :].astype(BF16)
            zd = jnp.zeros_like(dob)
            lse_blk, cc = l_ref[rows, :], cc_ref[rows, :]
            k = k_ref[0:klen, :]
            zk = jnp.zeros_like(k)
            allowed = _iota((H, klen), 0) + row0 >= _iota((H, klen), 1)
            dq = jnp.zeros((H, LANES), F32)
            for hh in range(2):
                sel = first if hh == 0 else jnp.logical_not(first)
                qh, doh = jnp.where(sel, q, zq), jnp.where(sel, dob, zd)
                bias_q = _pick_col(cc, 2 * hp + hh) - jnp.max(jnp.where(sel, lse_blk, NEG), axis=1, keepdims=True)
                ck = jnp.concatenate([_pick_row(cr_ref[j], 2 * hp + hh) for j in range(tile + 1)], axis=1)[:, :klen]
                p = jnp.exp(jnp.where(allowed, _bdot(qh, k, _NT) + (bias_q - ck), NEG))
                dp = _bdot(doh, v_ref[0:klen, :], _NT)
                ds = p * (dp - jnp.sum(p * dp, axis=1, keepdims=True))
                dv_acc[0:klen, :] += _bdot(p, doh, _TN)
                dk_acc[0:klen, :] += _bdot(ds, qh, _TN)
                dcs = jnp.sum(ds, axis=0, keepdims=True)
                for j in range(tile + 1):
                    n = min(T, klen - j * T)
                    dc_ref[0, j, hh:hh + 1, 0:n] -= dcs[:, j * T:j * T + n]
                dq = dq + _bdot(ds, jnp.where(sel, k, zk), _NN)
            dq_raw, dgq = _pair_norm_bwd(dq * scale, q_raw, rq, gq_ref[...], first)
            dq_ref[rows, :] = dq_raw.astype(dq_ref.dtype)
            dg_ref[0, 0:1, :] += dgq

        for tile in range(nq):
            @pl.when(i == tile)
            def _(tile=tile):
                for half in range(2):
                    backprop(tile, half)

        @pl.when(i == nq - 1)
        def _():
            k_raw = kraw_ref[...]
            rk = _pair_norm(k_raw, gk_ref[...], first)[1]
            dk_raw, dgk = _pair_norm_bwd(dk_acc[...], k_raw, rk, gk_ref[...], first)
            dk_ref[...] = dk_raw.astype(dk_ref.dtype)
            dv_ref[...] = dv_acc[...].astype(dv_ref.dtype)
            dg_ref[0, 1:2, :] = dgk

    gain = pl.BlockSpec((1, LANES), lambda hp, i: (0, 0))
    band = SDS((S, ATT_HEADS * ATT_D), BF16)
    return pl.pallas_call(
        body, name="fox_bwd", grid=(nhp, nq),
        in_specs=[pl.BlockSpec((T, LANES), lambda hp, i: (i, qb0 + hp)), pl.BlockSpec((S, LANES), lambda hp, i: (0, kb0 + hp)),
                  pl.BlockSpec((S, LANES), lambda hp, i: (0, vb0 + hp)), gain, gain,
                  pl.BlockSpec((T, 16), lambda hp, i: (i, 0)), pl.BlockSpec((nq, 16, T), lambda hp, i: (0, 0, 0)),
                  pl.BlockSpec((T, LANES), lambda hp, i: (i, hp)), pl.BlockSpec((T, LANES), lambda hp, i: (i, db0 + hp))],
        out_specs=[pl.BlockSpec((T, LANES), lambda hp, i: (i, hp)), pl.BlockSpec((S, LANES), lambda hp, i: (0, hp)),
                   pl.BlockSpec((S, LANES), lambda hp, i: (0, hp)), pl.BlockSpec((1, nq, 8, T), lambda hp, i: (hp, 0, 0, 0)),
                   pl.BlockSpec((1, 8, LANES), lambda hp, i: (hp, 0, 0))],
        out_shape=[band, band, band, SDS((nhp, nq, 8, T), F32), SDS((nhp, 8, LANES), F32)],
        scratch_shapes=[pltpu.VMEM((S, LANES), BF16), pltpu.VMEM((S, LANES), F32), pltpu.VMEM((S, LANES), F32)],
        compiler_params=_cparams(dimension_semantics=("arbitrary", "arbitrary")),
    )(src, src, src, gq2, gk2, cum_col, cum_row3, lse, dsrc)


def _fold_gains(dg):
    def body(d_ref, o_ref):
        t = d_ref[0]
        for h in range(1, dg.shape[0]):
            t = t + d_ref[h]
        o_ref[...] = t + pltpu.roll(t, ATT_D, axis=1)

    return pl.pallas_call(body, name="fold_gains", out_shape=SDS(dg.shape[1:], F32), compiler_params=_cparams())(dg)


def _adamw_math(w, g, m, v):
    m = ADAM_B1 * m + (1.0 - ADAM_B1) * g
    v = ADAM_B2 * v + (1.0 - ADAM_B2) * jnp.square(g)
    m_hat = m / (1.0 - ADAM_B1 ** ADAM_STEP)
    v_hat = v / (1.0 - ADAM_B2 ** ADAM_STEP)
    delta = -ADAM_LR * (m_hat / (jnp.sqrt(v_hat) + ADAM_EPS) + ADAM_WD * w)
    return delta, m, v


def _reduce_adamw(parts, w, m, v, *, tr, name, tc=None):
    R, C = w.shape
    tr, tc = min(tr, R), tc or C
    nparts = parts.shape[0]

    def body(p_ref, w_ref, m_ref, v_ref, g_ref, d_ref, nm_ref, nv_ref):
        g = p_ref[0].astype(F32)
        for s in range(1, nparts):
            g = g + p_ref[s].astype(F32)
        g_ref[...] = g
        d_ref[...], nm_ref[...], nv_ref[...] = _adamw_math(w_ref[...], g, m_ref[...], v_ref[...])

    blk = pl.BlockSpec((tr, tc), lambda i, j: (i, j))
    return pl.pallas_call(
        body, name=name, grid=(R // tr, C // tc),
        in_specs=[pl.BlockSpec((nparts, tr, tc), lambda i, j: (0, i, j)), blk, blk, blk], out_specs=[blk] * 4,
        out_shape=[SDS((R, C), F32)] * 4, compiler_params=_cparams(dimension_semantics=("parallel", "parallel")),
    )(parts, w, m, v)


def _adamw(w, g, m, v, *, name):
    def body(w_ref, g_ref, m_ref, v_ref, d_ref, nm_ref, nv_ref):
        d_ref[...], nm_ref[...], nv_ref[...] = _adamw_math(w_ref[...], g_ref[...], m_ref[...], v_ref[...])

    return pl.pallas_call(body, name=name, out_shape=[SDS(w.shape, F32)] * 3, compiler_params=_cparams())(w, g, m, v)


def _peers():
    x, y, c = lax.axis_index("x"), lax.axis_index("y"), lax.axis_index("c")
    out = []
    for k in range(1, N_DEV):
        px, py, pc = x ^ ((k >> 2) & 1), y ^ ((k >> 1) & 1), c ^ (k & 1)
        out.append(((px, py, pc), 4 * px + 2 * py + pc))
    return 4 * x + 2 * y + c, out


_HBM = pl.BlockSpec(memory_space=pltpu.HBM)
_SEM = pl.BlockSpec(memory_space=pltpu.SEMAPHORE)
_DATAFLOW = pltpu.SideEffectType.DATAFLOW_SIDE_EFFECTING


NEAR = (1, 2, 4, 6)


def _plan_peers(scatter, ks=tuple(range(1, N_DEV))):
    return lambda me, peers: [(peers[k - 1][0], peers[k - 1][1] if scatter else None, me, k - 1) for k in ks]


def _plan_relay(me, peers):
    return [(peers[0][0], peers[k - 1][1], peers[k - 1][1], j) for j, k in enumerate((2, 4, 6))]


def _plan_pair(me, peers):
    return [(peers[0][0], peers[k - 1][1], j, j) for j, k in enumerate((1, 3, 5, 7))]


def _plan_chips(me, peers):
    return [(peers[k - 1][0], k // 2, k // 2, k // 2) for k in (2, 4, 6)]


def _copy(src, dst, c, send_sems, recv_sems):
    dev, s_slot, d_slot, i = c
    return pltpu.make_async_remote_copy(
        src_ref=src if s_slot is None else src.at[s_slot], dst_ref=dst.at[d_slot], send_sem=send_sems.at[i],
        recv_sem=recv_sems.at[i], device_id=dev, device_id_type=MESH)


def _copies_start(items, *, name):
    n = len(items)
    bufs = [it[0] for it in items] + [it[1] for it in items if it[1] is not None]
    nb = len(bufs)

    def body(*refs):
        srcs, extra, sems, token = refs[:n], iter(refs[n:nb]), refs[nb:nb + 2 * n], refs[-1]
        me, peers = _peers()
        for a, (_, land, plan) in enumerate(items):
            dst = srcs[a] if land is None else next(extra)
            for c in plan(me, peers):
                _copy(srcs[a], dst, c, sems[2 * a], sems[2 * a + 1]).start()
        token[...] = jnp.zeros_like(token)

    res = pl.pallas_call(
        body, name=name,
        out_shape=([pltpu.SemaphoreType.DMA((N_DEV - 1,))] * (2 * n) + [pltpu.HBM(b.shape, b.dtype) for b in bufs]
                   + [SDS((8, LANES), F32)]),
        in_specs=[_HBM] * nb, out_specs=[_SEM] * (2 * n) + [_HBM] * nb + [pl.BlockSpec(memory_space=pltpu.VMEM)],
        input_output_aliases={i: 2 * n + i for i in range(nb)},
        compiler_params=pltpu.CompilerParams(has_side_effects=_DATAFLOW),
    )(*[pltpu.with_memory_space_constraint(b, pltpu.HBM) for b in bufs])
    sems, thru, token = res[:2 * n], list(res[2 * n:2 * n + nb]), res[-1]
    extra = iter(thru[n:])
    return [(thru[a], None if it[1] is None else next(extra), sems[2 * a], sems[2 * a + 1], it[2])
            for a, it in enumerate(items)], token


def _copies_wait(handles, after, *, name):
    n = len(handles)
    bufs = [h[0] for h in handles] + [h[1] for h in handles if h[1] is not None]
    nb = len(bufs)

    def body(*refs):
        srcs, extra, sems = refs[:n], iter(refs[n:nb]), refs[nb:nb + 2 * n]
        me, peers = _peers()
        for a, h in enumerate(handles):
            dst = srcs[a] if h[1] is None else next(extra)
            for c in h[4](me, peers):
                cp = _copy(srcs[a], dst, c, sems[2 * a], sems[2 * a + 1])
                cp.wait_send()
                cp.wait_recv()

    flat_sems = [s for h in handles for s in (h[2], h[3])]
    res = pl.pallas_call(
        body, name=name, out_shape=[pltpu.HBM(b.shape, b.dtype) for b in bufs],
        in_specs=[_HBM] * nb + [_SEM] * (2 * n) + [pl.BlockSpec(memory_space=pl.ANY)], out_specs=[_HBM] * nb,
        input_output_aliases={i: i for i in range(nb)},
        compiler_params=pltpu.CompilerParams(has_side_effects=_DATAFLOW),
    )(*bufs, *flat_sems, after)
    extra = iter(res[n:])
    return [(res[a], res[a] if h[1] is None else next(extra)) for a, h in enumerate(handles)]


def _exchange_start(arrays, *, scatter, name, near=()):
    items = []
    for a, arr in enumerate(arrays):
        land = lax.empty(arr.shape if scatter else (N_DEV,) + arr.shape, arr.dtype)
        items.append((arr, land, _plan_peers(scatter, NEAR) if a in near else _plan_peers(scatter)))
    return _copies_start(items, name=name)


MOVE_ROWS, MOVE_SLOTS = 512, 3


def _move_rows(src, moves, rows, *, name):
    C = src.shape[1]
    covered = max(dst + n for _, n, dst in moves)
    tail = rows - covered
    assert sum(n for _, n, _ in moves) == covered
    chunks = [(lo + o, min(MOVE_ROWS, n - o), dst + o) for lo, n, dst in moves for o in range(0, n, MOVE_ROWS)]
    nch = len(chunks)

    def body(src_ref, o_ref, buf, sin, sout, *zero):
        def fetch(i):
            lo, n, _ = chunks[i]
            return pltpu.make_async_copy(src_ref.at[pl.ds(lo, n)], buf.at[i % MOVE_SLOTS, pl.ds(0, n)], sin.at[i % MOVE_SLOTS])

        def store(i):
            _, n, dst = chunks[i]
            return pltpu.make_async_copy(buf.at[i % MOVE_SLOTS, pl.ds(0, n)], o_ref.at[pl.ds(dst, n)], sout.at[i % MOVE_SLOTS])

        if tail:
            zero[0][...] = jnp.zeros_like(zero[0])
            fill = pltpu.make_async_copy(zero[0], o_ref.at[pl.ds(covered, tail)], zero[1])
            fill.start()
        for i in range(nch):
            if i >= MOVE_SLOTS:
                store(i - MOVE_SLOTS).wait()
            fetch(i).start()
            if i >= 1:
                fetch(i - 1).wait()
                store(i - 1).start()
        fetch(nch - 1).wait()
        store(nch - 1).start()
        for i in range(max(0, nch - MOVE_SLOTS), nch):
            store(i).wait()
        if tail:
            fill.wait()

    anyspec = pl.BlockSpec(memory_space=pl.ANY)
    dma = pltpu.SemaphoreType.DMA
    return pl.pallas_call(
        body, name=name, in_specs=[anyspec], out_specs=anyspec, out_shape=SDS((rows, C), src.dtype),
        scratch_shapes=([pltpu.VMEM((MOVE_SLOTS, MOVE_ROWS, C), src.dtype), dma((MOVE_SLOTS,)), dma((MOVE_SLOTS,))]
                        + ([pltpu.VMEM((tail, C), src.dtype), dma] if tail else [])),
        compiler_params=_cparams())(src)


def _pair_sum(a, b, *, name):
    n, R, C = a.shape
    tc = 256

    def body(a_ref, b_ref, o_ref):
        o_ref[...] = (a_ref[...].astype(F32) + b_ref[...].astype(F32)).astype(o_ref.dtype)

    blk = pl.BlockSpec((1, R, tc), lambda i, j: (i, 0, j))
    return pl.pallas_call(body, name=name, grid=(n, C // tc), in_specs=[blk, blk], out_specs=blk,
                          out_shape=SDS(a.shape, a.dtype), compiler_params=_cparams(dimension_semantics=("parallel", "parallel")))(a, b)


def _own_slot(landed, own, me):
    return lax.dynamic_update_slice(landed, own[None], (me,) + (0,) * own.ndim)


SMALL = (("g_mix", 1024), ("conv_w", 6144), ("conv_b", 1536), ("dt_bias", 16), ("a_log", 16), ("d_skip", 16),
         ("ssm_norm_w", 1024), ("g_q", 64), ("g_k", 64), ("f_bias", 16), ("g_xattn", 1024), ("g_mem", 1024),
         ("xg_q", 256), ("xg_k", 256), ("g_mlp", 1024))
SLAB_ROWS = 112
BIG = ("w_in", "w_out", "xq_w", "xkv_w", "xo_w", "w_up", "w_down")
WEIGHTS = ("g_mix", "w_in", "conv_w", "conv_b", "dt_bias", "a_log", "d_skip", "ssm_norm_w", "g_q", "g_k", "f_bias", "w_out",
           "g_xattn", "g_mem", "xq_w", "xkv_w", "xg_q", "xg_k", "xo_w", "g_mlp", "w_up", "w_down")
O_Z, O_XS, O_B, O_C, O_DT, O_Q, O_K, O_V, O_F, O_END = 0, 1024, 2048, 2304, 2560, 2576, 3600, 4624, 5648, 5664
IN_ROW_MOVES = ((O_Z, O_B - O_Z, C_Z), (O_Q, O_F - O_Q, C_Q), (O_B, O_Q - O_B, C_B), (O_F, O_END - O_F, C_DTF + 16))


def _pack_small(vals):
    rows = []
    for name, size in SMALL:
        flat = vals[name].reshape(-1).astype(F32)
        pad = -size % LANES
        rows.append(jnp.pad(flat, (0, pad)).reshape(-1, LANES))
    slab = jnp.concatenate(rows, axis=0)
    return jnp.pad(slab, ((0, SLAB_ROWS - slab.shape[0]), (0, 0)))


def _unpack_small(slab):
    out, r = {}, 0
    for name, size in SMALL:
        nr = -(-size // LANES)
        out[name] = slab[r:r + nr].reshape(-1)[:size]
        r += nr
    return out


def _step(p, m, v, x, mem, target):
    S = x.shape[0]
    TM = 256
    me = 4 * lax.axis_index("x") + 2 * lax.axis_index("y") + lax.axis_index("c")

    def rms(u, g, name):
        return _rw_fwd(_rms_fn, [_whole(u)], [_whole(g)], [(D_MODEL, BF16)], tm=TM, name=name)[0]

    def pin(param, token):
        return param + token[0:1, 0:1]

    def landed_with_own(pairs, scatter):
        out = []
        for src, land in pairs:
            own = lax.dynamic_index_in_dim(src, me, 0, keepdims=False) if scatter else src
            out.append(_own_slot(land, own, me))
        return out

    w_in_own, m_in_own, v_in_own = p["w_in"].T, m["w_in"].T, v["w_in"].T
    ag, ag_token = _exchange_start([w_in_own.astype(BF16), p["conv_w"]] + [p[n].astype(BF16) for n in BIG[1:]],
                                   scatter=False, name="allgather_start", near=(0, 2, 3, 4, 5, 6, 7))
    h1 = rms(x, pin(p["g_mix"], ag_token), "rms_mix")
    (win_src, win_land), convw_pair = _copies_wait(ag[:2], h1, name="allgather_wait_in")
    relay, token = _copies_start([(win_land, None, _plan_relay)], name="allgather_relay_start")
    win_land = _copies_wait(relay, token, name="allgather_relay_wait")[0][1]
    win_g, convw_g = landed_with_own([(win_src, win_land), convw_pair], False)
    w_in_o = win_g.reshape(O_END, D_MODEL)
    w_in_t = _move_rows(w_in_o, IN_ROW_MOVES, P_COLS, name="w_in_rows")
    conv_w = convw_g.transpose(1, 0, 2).reshape(4, 1536)
    cw_xs, cw_bc = conv_w[:, :1024], conv_w[:, 1024:]
    cb_xs, cb_bc = p["conv_b"][:, :1024], p["conv_b"][:, 1024:]
    dt_bias, a_log, f_bias = p["dt_bias"].reshape(16, 1), p["a_log"].reshape(16, 1), p["f_bias"].reshape(16, 1)

    proj = _matmul(h1, w_in_t, mode="nt", tm=1024, tn=640, tk=1024, name="mm_in")
    xs_c = _conv_fwd(proj, C_XS, 1024, cw_xs, cb_xs, name="conv_xs")
    bc_c = _conv_fwd(proj, C_B, 512, cw_bc, cb_bc, name="conv_bc")
    dtf_t = proj[:, C_DTF:C_DTF + 32].T
    dt_t, acs_t, cum_t = _dtf_fwd(dtf_t, dt_bias, a_log, f_bias)
    dt_col, acs_col, cum_col = dt_t.T, acs_t.T, cum_t.T
    cum_row3 = cum_t.reshape(16, S // ATT_T, ATT_T).transpose(1, 0, 2)
    y_ssd, hs = _ssd_fwd(xs_c, dt_col, acs_col, acs_t, bc_c)
    gate_rows = [_whole(y_ssd), _whole(xs_c), (proj, C_Z, 1024)]
    gate_pars = [_whole(p["d_skip"]), _whole(p["ssm_norm_w"])]
    y_ssm = _rw_fwd(_gate_fn, gate_rows, gate_pars, [(1024, BF16)], tm=TM, name="gate")[0]
    gq2, gk2 = jnp.tile(p["g_q"], (1, 2)), jnp.tile(p["g_k"], (1, 2))
    o, lse = _fox_fwd(proj, C_Q, C_K, C_V, gq2, gk2, cum_col, cum_row3)
    mixed = jnp.concatenate([y_ssm, o], axis=1)
    arrived = _copies_wait(ag[2:], mixed, name="allgather_wait_rest")
    relay, token = _copies_start([(land, None, _plan_relay) for _, land in arrived], name="allgather_relay_rest_start")
    wout_g, = landed_with_own([(arrived[0][0], _copies_wait(relay[:1], token, name="allgather_relay_out_wait")[0][1])], False)
    w_out = wout_g.reshape(2 * D_MODEL, D_MODEL)
    x1, h2 = _matmul(mixed, w_out, mode="nn", tm=1024, tn=1024, tk=2048, name="mm_out", extras=(x,),
                     row_params=(p["g_xattn"],), epilogue=_ep_residual_rms, out_dtypes=[F32, BF16])
    relayed = _copies_wait(relay[1:], x1, name="allgather_relay_rest_wait")
    xq_g, xkv_w, xo_g, w_up, wdown_g = landed_with_own(
        [(src, land) for (src, _), (_, land) in zip(arrived[1:], relayed, strict=True)], False)
    xq_w = xq_g.reshape(D_MODEL, D_MODEL)
    xo_w = xo_g.reshape(D_MODEL, D_MODEL)
    w_down = wdown_g.reshape(4 * D_MODEL, D_MODEL)

    mem_n = rms(mem, p["g_mem"], "rms_mem")
    q2 = _matmul(h2, xq_w, mode="nn", tm=1024, tn=512, tk=1024, name="mm_xq")
    kv = _matmul(mem_n, xkv_w, mode="nn", b_shards=True, tm=256, tn=256, tk=1024, name="mm_xkv")
    xa_rows = [(q2, X_D * h, X_D) for h in range(X_HEADS)]
    xa_pars = ([(kv, X_D * h, X_D) for h in range(X_HEADS)] + [(kv, D_MODEL + X_D * h, X_D) for h in range(X_HEADS)]
               + [_whole(p["xg_q"]), _whole(p["xg_k"])])
    o2 = _rw_fwd(_xattn_fn, xa_rows, xa_pars, [(D_MODEL, BF16)], tm=TM, name="xattn")[0]
    x2, h3 = _matmul(o2, xo_w, mode="nn", tm=1024, tn=1024, tk=1024, name="mm_xo", extras=(x1,),
                     row_params=(p["g_mlp"],), epilogue=_ep_residual_rms, out_dtypes=[F32, BF16])

    a, usq = _matmul(h3, w_up, mode="nn", b_shards=True, tm=2048, tn=512, tk=1024, name="mm_up", out_dtypes=[F32, BF16],
                     epilogue=lambda acc: (acc, jnp.square(jax.nn.relu(acc))))
    dy, loss_part = _matmul(usq, w_down, mode="nn", tm=1024, tn=512, tk=2048, name="mm_down", extras=(x2, target),
                            epilogue=functools.partial(_ep_loss, width=D_MODEL), sums=[(1, 1)])
    loss = lax.psum(loss_part[0, 0], ("x", "y", "c"))

    def row_shards(a):
        r, c = a.shape
        return a.reshape(N_DEV, r // N_DEV, c)

    g = {}
    g["w_down"] = _matmul(usq, dy, mode="tn", out_dtype=GRAD_WIRE, tm=1024, tn=1024, tk=1024, name="mm_d_wdown")
    da = _matmul(dy, w_down, mode="nt", tm=1024, tn=1024, tk=1024, name="mm_d_usq", out_dtype=BF16, extras=(a,),
                 epilogue=lambda acc, av: (2.0 * jax.nn.relu(av) * acc,))
    g["w_up"] = _matmul(h3, da, mode="tn", out_shards=True, out_dtype=GRAD_WIRE, tm=1024, tn=512, tk=1024, name="mm_d_wup")
    sent_mlp, token = _exchange_start([row_shards(g["w_down"]), g["w_up"]], scatter=True,
                                      name="grads_start_mlp")
    dx2, g["g_mlp"] = _matmul(da, w_up, mode="nt", b_shards=True, tm=1024, tn=1024, tk=512, name="mm_d_h3",
                              extras=(x2, dy), row_params=(pin(p["g_mlp"], token),), epilogue=_ep_rms_bwd,
                              sums=[(1, D_MODEL)])

    g["xo_w"] = _matmul(o2, dx2, mode="tn", out_dtype=GRAD_WIRE, tm=1024, tn=1024, tk=1024, name="mm_d_wxo")
    do2 = _matmul(dx2, xo_w, mode="nt", tm=1024, tn=512, tk=1024, name="mm_d_o2")
    xa = _rw_bwd(_xattn_fn, xa_rows, xa_pars, [_whole(do2)], tm=TM, name="xattn_bwd", row_grads=[BF16] * X_HEADS)
    dq2 = jnp.concatenate(xa[:X_HEADS], axis=1)
    dkv = jnp.concatenate(xa[X_HEADS:3 * X_HEADS], axis=1)
    g["xg_q"], g["xg_k"] = xa[3 * X_HEADS], xa[3 * X_HEADS + 1]
    g["xq_w"] = _matmul(h2, dq2, mode="tn", out_dtype=GRAD_WIRE, tm=1024, tn=1024, tk=1024, name="mm_d_wxq")
    dx1, g["g_xattn"] = _matmul(dq2, xq_w, mode="nt", tm=1024, tn=1024, tk=1024, name="mm_d_h2", extras=(x1, dx2),
                                row_params=(p["g_xattn"],), epilogue=_ep_rms_bwd, sums=[(1, D_MODEL)])
    g["xkv_w"] = _matmul(mem_n, dkv, mode="tn", out_shards=True, out_dtype=GRAD_WIRE, tm=1024, tn=256, tk=256,
                         name="mm_d_wxkv")
    dmem_n = _matmul(dkv, xkv_w, mode="nt", b_shards=True, tm=256, tn=1024, tk=256, name="mm_d_memn")
    g["g_mem"] = _rw_bwd(_rms_fn, [_whole(mem)], [_whole(p["g_mem"])], [_whole(dmem_n)], tm=TM, name="rms_mem_bwd",
                         row_grads=[None])[0]

    g["w_out"] = _matmul(mixed, dx1, mode="tn", out_dtype=GRAD_WIRE, tm=1024, tn=1024, tk=1024, name="mm_d_wout")
    sent_mid, token = _exchange_start(
        [row_shards(g["w_out"]), row_shards(g["xq_w"]), g["xkv_w"], row_shards(g["xo_w"])], scatter=True,
        name="grads_start_mid")
    dmixed = _matmul(dx1, w_out, mode="nt", tm=1024, tn=1024, tk=1024, name="mm_d_mixed")
    dq, dk, dv, dcum4, dgain = _fox_bwd(proj, C_Q, C_K, C_V, pin(gq2, token), gk2, cum_col, cum_row3, lse, dmixed, 1024)
    gains = _fold_gains(dgain)
    g["g_q"], g["g_k"] = gains[0:1, :ATT_D], gains[1:2, :ATT_D]
    dy_ssd, dxs_g, dz, g["d_skip"], g["ssm_norm_w"] = _rw_bwd(
        _gate_fn, gate_rows, gate_pars, [(dmixed, 0, 1024)], tm=TM, name="gate_bwd", row_grads=[F32, F32, BF16])
    dxs_s, ddt_col, dacs_col, dacs_row, d_b, d_c = _ssd_bwd(xs_c, dt_col, acs_col, acs_t, bc_c, hs, dy_ssd)
    dcum_t = dcum4[:, :, 0:2, :].transpose(0, 2, 1, 3).reshape(16, S)
    ddtf_t, ddtb, dalog, dfb = _dtf_bwd(dtf_t, dt_bias, a_log, f_bias, ddt_col.T, dacs_col.T, dacs_row, dcum_t)
    g["dt_bias"], g["a_log"], g["f_bias"] = ddtb, dalog, dfb
    dxs_raw, dcw_xs, dcb_xs = _conv_bwd(proj, C_XS, 1024, cw_xs, cb_xs, [dxs_s, dxs_g], name="conv_xs_bwd")
    dbc_raw, dcw_bc, dcb_bc = _conv_bwd(proj, C_B, 512, cw_bc, cb_bc, [jnp.concatenate([d_b, d_c], axis=1)],
                                        name="conv_bc_bwd")
    g["conv_w"] = jnp.concatenate([dcw_xs, dcw_bc], axis=1)
    g["conv_b"] = jnp.concatenate([dcb_xs, dcb_bc], axis=1)
    ddtf = jnp.pad(ddtf_t.T.astype(BF16), ((0, 0), (0, P_COLS - C_DTF - 32)))
    dproj = jnp.concatenate([dz, dxs_raw, dq, dk, dv, dbc_raw, ddtf], axis=1)
    dw_in_p = _matmul(dproj, h1, mode="tn", out_dtype=GRAD_WIRE, tm=640, tn=1024, tk=1024, name="mm_d_win")
    g["w_in"] = _move_rows(dw_in_p, [(dst, n, lo) for lo, n, dst in IN_ROW_MOVES], O_END, name="d_w_in_rows")
    half = N_DEV // 2
    send_in = row_shards(g["w_in"])
    pair, token = _copies_start([(send_in, lax.empty((half,) + send_in.shape[1:], send_in.dtype), _plan_pair)],
                                name="grads_in_pair_start")
    send_in, from_sibling = _copies_wait(pair, token, name="grads_in_pair_wait")[0]
    mine = jnp.stack([lax.dynamic_index_in_dim(send_in, me ^ (2 * j), 0, keepdims=False) for j in range(half)])
    chip_sums = _pair_sum(mine, from_sibling, name="grads_in_pair_sum")
    sent_in, token = _copies_start([(chip_sums, lax.empty(chip_sums.shape, chip_sums.dtype), _plan_chips)],
                                   name="grads_in_chip_start")
    grad_x, g["g_mix"] = _matmul(dproj, w_in_t, mode="nn", tm=1024, tn=1024, tk=1152, name="mm_d_h1", extras=(x, dx1),
                                 row_params=(pin(p["g_mix"], token),), epilogue=_ep_rms_bwd, sums=[(1, D_MODEL)])
    sent_small, _ = _exchange_start([_pack_small(g)], scatter=False, name="small_grads_start")

    grads, delta, new_m, new_v = {}, {}, {}, {}

    def update(names, sent, after, wait_name):
        parts = landed_with_own(_copies_wait(sent, after, name=wait_name), True)
        for name, part in zip(names, parts, strict=True):
            grads[name], delta[name], new_m[name], new_v[name] = _reduce_adamw(part, p[name], m[name], v[name], tr=128,
                                                                                name="adamw_" + name)

    update(("w_down", "w_up"), sent_mlp, grad_x, "grads_wait_mlp")
    update(("w_out", "xq_w", "xkv_w", "xo_w"), sent_mid, delta["w_up"], "grads_wait_mid")
    chip_sums, landed = _copies_wait(sent_in, delta["xo_w"], name="grads_in_chip_wait")[0]
    part = lax.dynamic_update_slice(landed, chip_sums[0:1], (0, 0, 0))
    res = _reduce_adamw(part, w_in_own, m_in_own, v_in_own, tr=part.shape[1], tc=256, name="adamw_w_in")
    grads["w_in"], delta["w_in"], new_m["w_in"], new_v["w_in"] = [r.T for r in res]
    small_parts = landed_with_own(_copies_wait(sent_small, delta["w_in"], name="small_grads_wait"), False)[0]
    zeros_cw = jnp.zeros((4, 1536), F32)
    slabs = [_pack_small({**d, "conv_w": zeros_cw}) for d in (p, m, v)]
    sg, sd, sm, sv = _reduce_adamw(small_parts, *slabs, tr=SLAB_ROWS, name="adamw_small")
    for dst, slab in ((grads, sg), (delta, sd), (new_m, sm), (new_v, sv)):
        for name, flat in _unpack_small(slab).items():
            if name != "conv_w":
                dst[name] = flat.reshape(p[name].shape)
    cw_shard = p["conv_w"].shape[1]
    grads["conv_w"] = lax.dynamic_slice(_unpack_small(sg)["conv_w"].reshape(4, 1536), (0, me * cw_shard), (4, cw_shard))
    delta["conv_w"], new_m["conv_w"], new_v["conv_w"] = _adamw(p["conv_w"], grads["conv_w"], m["conv_w"], v["conv_w"],
                                                               name="adamw_conv_w")
    return loss, grad_x, grads, delta, new_m, new_v


def kernel(x, mem, g_mix, w_in, conv_w, conv_b, dt_bias, a_log, d_skip, ssm_norm_w, g_q, g_k, f_bias, w_out, g_xattn, g_mem, xq_w, xkv_w, xg_q, xg_k, xo_w, g_mlp, w_up, w_down, loss_target, m_g_mix, m_w_in, m_conv_w, m_conv_b, m_dt_bias, m_a_log, m_d_skip, m_ssm_norm_w, m_g_q, m_g_k, m_f_bias, m_w_out, m_g_xattn, m_g_mem, m_xq_w, m_xkv_w, m_xg_q, m_xg_k, m_xo_w, m_g_mlp, m_w_up, m_w_down, v_g_mix, v_w_in, v_conv_w, v_conv_b, v_dt_bias, v_a_log, v_d_skip, v_ssm_norm_w, v_g_q, v_g_k, v_f_bias, v_w_out, v_g_xattn, v_g_mem, v_xq_w, v_xkv_w, v_xg_q, v_xg_k, v_xo_w, v_g_mlp, v_w_up, v_w_down):
    args = locals()
    drop = lambda t: t[0] if t.ndim == 3 else t
    p = {n: drop(args[n]) for n in WEIGHTS}
    m = {n: drop(args["m_" + n]) for n in WEIGHTS}
    v = {n: drop(args["v_" + n]) for n in WEIGHTS}
    loss, grad_x, grads, delta, new_m, new_v = _step(p, m, v, x[0], mem[0], loss_target[0])
    outs = [loss, grad_x[None]]
    for d in (grads, delta, new_m, new_v):
        outs += [d[n].reshape(args[n].shape) for n in WEIGHTS]
    return tuple(outs)
```

```python
import functools
import math

import jax
import jax.numpy as jnp
from jax import lax
from jax.experimental import pallas as pl
from jax.experimental.pallas import tpu as pltpu

F32, BF16 = jnp.float32, jnp.bfloat16
SDS = jax.ShapeDtypeStruct
HI = lax.Precision.HIGHEST
MESH = pl.DeviceIdType.MESH

N_DEV = 8
EPS = 1e-5
D_MODEL = 1024
SSM_HEADS, SSM_P, SSM_N, SSM_GROUPS, CHUNK = 16, 64, 128, 2, 128
ATT_HEADS, ATT_D = 16, 64
X_HEADS, X_D = 4, 256
LANES = 128
VMEM_LIMIT = 48 * 1024 * 1024
NEG = -1e30

GRAD_WIRE = BF16
ADAM_LR, ADAM_B1, ADAM_B2, ADAM_EPS, ADAM_WD, ADAM_STEP = 0.001, 0.9, 0.999, 1e-08, 0.01, 10

C_Z, C_XS, C_Q, C_K, C_V, C_B, C_C, C_DTF, P_COLS = 0, 1024, 2048, 3072, 4096, 5120, 5376, 5632, 5760

_NN = (((1,), (0,)), ((), ()))
_NT = (((1,), (1,)), ((), ()))
_TN = (((0,), (0,)), ((), ()))


def _cparams(**kw):
    return pltpu.CompilerParams(vmem_limit_bytes=VMEM_LIMIT, **kw)


def _bdot(a, b, dn):
    return lax.dot_general(a.astype(BF16), b.astype(BF16), dn, preferred_element_type=F32)


@jax.custom_vjp
def mm_nn(a, b):
    return _bdot(a, b, _NN)


mm_nn.defvjp(lambda a, b: (mm_nn(a, b), (a, b)), lambda r, g: (_bdot(g, r[1], _NT), _bdot(r[0], g, _TN)))


@jax.custom_vjp
def mm_nt(a, b):
    return _bdot(a, b, _NT)


mm_nt.defvjp(lambda a, b: (mm_nt(a, b), (a, b)), lambda r, g: (_bdot(g, r[1], _NN), _bdot(g, r[0], _TN)))


@jax.custom_vjp
def mm_tn(a, b):
    return _bdot(a, b, _TN)


mm_tn.defvjp(lambda a, b: (mm_tn(a, b), (a, b)), lambda r, g: (_bdot(r[1], g, _NT), _bdot(r[0], g, _NN)))


def _cdot(x, c):
    return jnp.dot(x, c, precision=HI, preferred_element_type=F32)


def _iota(shape, dim):
    return lax.broadcasted_iota(jnp.int32, shape, dim)


def _matmul(a, b, *, mode, tm, tn, tk, name, out_dtype=F32, add=None, extras=(), epilogue=None, out_dtypes=None,
            b_shards=False, out_shards=False, row_params=(), sums=()):
    if mode == "tn":
        K, M = a.shape
    else:
        M, K = a.shape
    if b_shards:
        N = b.shape[1] if mode == "nt" else b.shape[0] * b.shape[2]
        tn, tk = (tn, b.shape[2]) if mode == "nt" else (b.shape[2], tk)
    else:
        N = b.shape[0] if mode == "nt" else b.shape[1]
    tm, tn, tk = min(tm, M), min(tn, N), min(tk, K)
    assert M % tm == 0 and N % tn == 0 and K % tk == 0, (name, M, N, K, tm, tn, tk)
    assert not b_shards or (K // tk if mode == "nt" else N // tn) == b.shape[0], name
    assert not (out_shards and (extras or add is not None)), name
    nk = K // tk
    dn = {"nn": _NN, "nt": _NT, "tn": _TN}[mode]
    if add is not None:
        extras, epilogue = (add,), lambda acc, r: (acc + r,)
    elif epilogue is None:
        epilogue = lambda acc: (acc,)
    out_dtypes = out_dtypes or [out_dtype]
    ne, no, ns = len(extras) + len(row_params), len(out_dtypes), len(sums)
    assert all(s == (1, 1) or (s == (1, N) and tn == N) for s in sums), name

    def body(*refs):
        a_ref, b_ref = refs[:2]
        e_refs, o_refs, s_refs = refs[2:2 + ne], refs[2 + ne:2 + ne + no], refs[2 + ne + no:2 + ne + no + ns]

        def finish(acc):
            res = epilogue(acc, *[e[...] for e in e_refs])
            for o_ref, v in zip(o_refs, res[:no], strict=True):
                o_ref[...] = v.astype(o_ref.dtype)
            first_tile = jnp.logical_and(pl.program_id(0) == 0, pl.program_id(1) == 0)
            for s_ref, v in zip(s_refs, res[no:], strict=True):
                @pl.when(first_tile)
                def _(s_ref=s_ref, v=v):
                    s_ref[...] = v

                @pl.when(jnp.logical_not(first_tile))
                def _(s_ref=s_ref, v=v):
                    s_ref[...] += v

        prod = _bdot(a_ref[...], b_ref[...], dn)
        if nk == 1:
            finish(prod)
            return
        acc_ref = refs[-1]
        k = pl.program_id(2)

        @pl.when(k == 0)
        def _():
            acc_ref[...] = prod

        @pl.when(jnp.logical_and(k > 0, k < nk - 1))
        def _():
            acc_ref[...] += prod

        @pl.when(k == nk - 1)
        def _():
            finish(acc_ref[...] + prod)

    a_spec = pl.BlockSpec((tk, tm), lambda i, j, k: (k, i)) if mode == "tn" else pl.BlockSpec((tm, tk), lambda i, j, k: (i, k))
    if b_shards and mode == "nt":
        b_spec = pl.BlockSpec((None, tn, tk), lambda i, j, k: (k, j, 0))
    elif b_shards:
        b_spec = pl.BlockSpec((None, tk, tn), lambda i, j, k: (j, k, 0))
    elif mode == "nt":
        b_spec = pl.BlockSpec((tn, tk), lambda i, j, k: (j, k))
    else:
        b_spec = pl.BlockSpec((tk, tn), lambda i, j, k: (k, j))
    if out_shards:
        o_spec, o_shape = pl.BlockSpec((None, tm, tn), lambda i, j, k: (j, i, 0)), (N // tn, M, tn)
    else:
        o_spec, o_shape = pl.BlockSpec((tm, tn), lambda i, j, k: (i, j)), (M, N)
    row_spec = pl.BlockSpec((1, tn), lambda i, j, k: (0, j))
    sum_specs = [pl.BlockSpec(s, lambda i, j, k: (0, 0)) for s in sums]
    res = pl.pallas_call(
        body, name=name, grid=(M // tm, N // tn, nk),
        in_specs=[a_spec, b_spec] + [o_spec] * len(extras) + [row_spec] * len(row_params),
        out_specs=[o_spec] * no + sum_specs, out_shape=[SDS(o_shape, dt) for dt in out_dtypes] + [SDS(s, F32) for s in sums],
        scratch_shapes=[pltpu.VMEM((tm, tn), F32)] if nk > 1 else [],
        compiler_params=_cparams(dimension_semantics=(("arbitrary",) * 3 if sums else ("parallel", "parallel", "arbitrary"))),
    )(a, b, *extras, *row_params)
    return res[0] if no + ns == 1 else res


def _row_spec(tm, spec):
    _, c0, w = spec
    assert c0 % w == 0
    return pl.BlockSpec((tm, w), functools.partial(lambda i, cb: (i, cb), cb=c0 // w))


def _par_spec(spec):
    arr, c0, w = spec
    assert c0 % w == 0
    return pl.BlockSpec((arr.shape[0], w), functools.partial(lambda i, cb: (0, cb), cb=c0 // w))


def _whole(arr):
    return (arr, 0, arr.shape[1])


def _rw_fwd(fn, rows, params, outs, *, tm, name):
    M = rows[0][0].shape[0]
    nr, npar = len(rows), len(params)

    def body(*refs):
        rv = [r[...].astype(F32) for r in refs[:nr]]
        pv = [p[...].astype(F32) for p in refs[nr:nr + npar]]
        res = fn(*rv, *pv)
        for o_ref, v in zip(refs[nr + npar:], res, strict=True):
            o_ref[...] = v.astype(o_ref.dtype)

    return pl.pallas_call(
        body, name=name, grid=(M // tm,),
        in_specs=[_row_spec(tm, r) for r in rows] + [_par_spec(p) for p in params],
        out_specs=[pl.BlockSpec((tm, w), lambda i: (i, 0)) for w, _ in outs],
        out_shape=[SDS((M, w), dt) for w, dt in outs],
        compiler_params=_cparams(dimension_semantics=("parallel",)),
    )(*[r[0] for r in rows], *[p[0] for p in params])


def _rw_bwd(fn, rows, params, cts, *, tm, name, row_grads, adds=None, join_rows=False, join_params=0):
    M = rows[0][0].shape[0]
    adds = adds or {}
    nr, npar, nc = len(rows), len(params), len(cts)
    add_keys = sorted(adds)
    want = [k for k in range(nr) if row_grads[k] is not None]

    def body(*refs):
        pos = 0
        r_refs = refs[pos:pos + nr]; pos += nr
        p_refs = refs[pos:pos + npar]; pos += npar
        c_refs = refs[pos:pos + nc]; pos += nc
        a_refs = dict(zip(add_keys, refs[pos:pos + len(add_keys)])); pos += len(add_keys)
        n_row_out = 1 if join_rows else len(want)
        dr_refs = refs[pos:pos + n_row_out]; pos += n_row_out
        dp_refs = refs[pos:]
        rv = [r[...].astype(F32) for r in r_refs]
        pv = [p[...].astype(F32) for p in p_refs]
        _, vjp = jax.vjp(fn, *rv, *pv)
        g = vjp(tuple(c[...].astype(F32) for c in c_refs))
        row_vals = []
        for k in want:
            v = g[k]
            if k in a_refs:
                v = v + a_refs[k][...].astype(F32)
            row_vals.append(v)
        if join_rows:
            row_vals = [jnp.concatenate(row_vals, axis=1)]
        for ref, v in zip(dr_refs, row_vals, strict=True):
            ref[...] = v.astype(ref.dtype)
        par_vals = list(g[nr:])
        if join_params:
            par_vals = [jnp.concatenate(par_vals[:join_params], axis=1)] + par_vals[join_params:]
        first = pl.program_id(0) == 0
        for ref, v in zip(dp_refs, par_vals, strict=True):
            @pl.when(first)
            def _(ref=ref, v=v):
                ref[...] = v

            @pl.when(jnp.logical_not(first))
            def _(ref=ref, v=v):
                ref[...] += v

    row_out = [(rows[k][2], row_grads[k]) for k in want]
    if join_rows:
        row_out = [(sum(w for w, _ in row_out), row_out[0][1])]
    par_out = [(p[0].shape[0], p[2]) for p in params]
    if join_params:
        par_out = [(par_out[0][0], sum(w for _, w in par_out[:join_params]))] + par_out[join_params:]
    res = pl.pallas_call(
        body, name=name, grid=(M // tm,),
        in_specs=([_row_spec(tm, r) for r in rows] + [_par_spec(p) for p in params] + [_row_spec(tm, c) for c in cts]
                  + [_row_spec(tm, adds[k]) for k in add_keys]),
        out_specs=([pl.BlockSpec((tm, w), lambda i: (i, 0)) for w, _ in row_out]
                   + [pl.BlockSpec(s, lambda i: (0, 0)) for s in par_out]),
        out_shape=[SDS((M, w), dt) for w, dt in row_out] + [SDS(s, F32) for s in par_out],
        compiler_params=_cparams(dimension_semantics=("arbitrary",)),
    )(*[r[0] for r in rows], *[p[0] for p in params], *[c[0] for c in cts], *[adds[k][0] for k in add_keys])
    return res


def _rms_fn(x, g):
    r = lax.rsqrt(jnp.mean(x * x, axis=-1, keepdims=True) + EPS)
    return (x * r * g,)


def _ep_residual_rms(acc, res, g):
    x = acc + res
    return x, _rms_fn(x, g)[0]


def _ep_rms_bwd(dh, x, dres, g):
    r = lax.rsqrt(jnp.mean(x * x, axis=-1, keepdims=True) + EPS)
    t = dh * g
    dx = dres + r * (t - x * (r * r) * jnp.mean(t * x, axis=-1, keepdims=True))
    return dx, jnp.sum(dh * x * r, axis=0, keepdims=True)


def _ep_loss(acc, res, target, *, width):
    e = acc + res - target
    return e * (1.0 / width), jnp.sum(jnp.sum(e * e, axis=1, keepdims=True), axis=0, keepdims=True) * (0.5 / width)


def _seg_mats(width, seg):
    n = width // seg
    p = (_iota((width, n), 0) // seg == _iota((width, n), 1)).astype(F32)
    e = (_iota((n, width), 1) // seg == _iota((n, width), 0)).astype(F32)
    return p, e


def _gate_fn(y, xs, z, dskip, w):
    width = SSM_HEADS * SSM_P
    _, e = _seg_mats(width, SSM_P)
    y = (y + _cdot(dskip, e) * xs) * (z * jax.nn.sigmoid(z))
    g0 = _iota((1, width), 1) < width // SSM_GROUPS
    y2 = y * y
    gw = width // SSM_GROUPS
    ms0 = jnp.sum(jnp.where(g0, y2, 0.0), axis=-1, keepdims=True) * (1.0 / gw)
    ms1 = jnp.sum(jnp.where(g0, 0.0, y2), axis=-1, keepdims=True) * (1.0 / gw)
    r = jnp.where(g0, lax.rsqrt(ms0 + EPS), lax.rsqrt(ms1 + EPS))
    return (y * r * w,)


def _xattn_fn(q0, q1, q2, q3, k0, k1, k2, k3, v0, v1, v2, v3, gq, gk):
    def norm(u, g):
        return u * lax.rsqrt(jnp.mean(u * u, axis=-1, keepdims=True) + EPS) * g
    outs = []
    for q, k, v in ((q0, k0, v0), (q1, k1, v1), (q2, k2, v2), (q3, k3, v3)):
        s = mm_nt(norm(q, gq), norm(k, gk)) * (X_D ** -0.5)
        p = jnp.exp(s - lax.stop_gradient(jnp.max(s, axis=-1, keepdims=True)))
        p = p / jnp.sum(p, axis=-1, keepdims=True)
        outs.append(mm_nn(p, v))
    return (jnp.concatenate(outs, axis=-1),)


CONV_TC = 256


def _shift_down(u, k):
    if k == 0:
        return u
    return jnp.where(_iota(u.shape, 0) >= k, pltpu.roll(u, k, axis=0), 0.0)


def _shift_up(u, k):
    if k == 0:
        return u
    n = u.shape[0]
    return jnp.where(_iota(u.shape, 0) < n - k, pltpu.roll(u, n - k, axis=0), 0.0)


def _conv_pre(u, w_ref, b):
    pre = b + w_ref[3:4, :] * u
    for k in (1, 2, 3):
        pre = pre + w_ref[3 - k:4 - k, :] * _shift_down(u, k)
    return pre


def _conv_fwd(src, c0, width, w, b, *, name):
    S = src.shape[0]
    cb0 = c0 // CONV_TC

    def body(u_ref, w_ref, b_ref, o_ref):
        pre = _conv_pre(u_ref[...], w_ref, b_ref[...])
        o_ref[...] = pre * jax.nn.sigmoid(pre)

    return pl.pallas_call(
        body, name=name, grid=(width // CONV_TC,),
        in_specs=[pl.BlockSpec((S, CONV_TC), lambda j: (0, cb0 + j)), pl.BlockSpec((4, CONV_TC), lambda j: (0, j)),
                  pl.BlockSpec((1, CONV_TC), lambda j: (0, j))],
        out_specs=pl.BlockSpec((S, CONV_TC), lambda j: (0, j)), out_shape=SDS((S, width), F32),
        compiler_params=_cparams(dimension_semantics=("parallel",)),
    )(src, w, b)


def _conv_bwd(src, c0, width, w, b, douts, *, name):
    S = src.shape[0]
    cb0 = c0 // CONV_TC
    nd = len(douts)

    def body(*refs):
        u_ref, w_ref, b_ref = refs[:3]
        d_refs = refs[3:3 + nd]
        du_ref, dw_ref, db_ref = refs[3 + nd:]
        u = u_ref[...]
        pre = _conv_pre(u, w_ref, b_ref[...])
        sg = jax.nn.sigmoid(pre)
        dout = d_refs[0][...]
        for r in d_refs[1:]:
            dout = dout + r[...]
        dpre = dout * (sg * (1.0 + pre * (1.0 - sg)))
        du = w_ref[3:4, :] * dpre
        dw_ref[3:4, :] = jnp.sum(dpre * u, axis=0, keepdims=True)
        for k in (1, 2, 3):
            du = du + w_ref[3 - k:4 - k, :] * _shift_up(dpre, k)
            dw_ref[3 - k:4 - k, :] = jnp.sum(dpre * _shift_down(u, k), axis=0, keepdims=True)
        du_ref[...] = du.astype(du_ref.dtype)
        db_ref[...] = jnp.sum(dpre, axis=0, keepdims=True)

    return pl.pallas_call(
        body, name=name, grid=(width // CONV_TC,),
        in_specs=[pl.BlockSpec((S, CONV_TC), lambda j: (0, cb0 + j)), pl.BlockSpec((4, CONV_TC), lambda j: (0, j)),
                  pl.BlockSpec((1, CONV_TC), lambda j: (0, j))] + [pl.BlockSpec((S, CONV_TC), lambda j: (0, j))] * nd,
        out_specs=[pl.BlockSpec((S, CONV_TC), lambda j: (0, j)), pl.BlockSpec((4, CONV_TC), lambda j: (0, j)),
                   pl.BlockSpec((1, CONV_TC), lambda j: (0, j))],
        out_shape=[SDS((S, width), BF16), SDS((4, width), F32), SDS((1, width), F32)],
        compiler_params=_cparams(dimension_semantics=("parallel",)),
    )(src, w, b, *douts)


def _softplus(x):
    return jnp.maximum(x, 0.0) + jnp.log(1.0 + jnp.exp(-jnp.abs(x)))


def _prefix_sum(x, seg):
    n = x.shape[1]
    pos = _iota(x.shape, 1) % seg
    k = 1
    while k < seg:
        x = x + jnp.where(pos >= k, pltpu.roll(x, k, axis=1), 0.0)
        k *= 2
    return x


def _suffix_sum(x, seg):
    n = x.shape[1]
    pos = _iota(x.shape, 1) % seg
    k = 1
    while k < seg:
        x = x + jnp.where(pos + k < seg, pltpu.roll(x, n - k, axis=1), 0.0)
        k *= 2
    return x


def _dtf_fwd(dtf_t, dt_bias, a_log, f_bias):
    S = dtf_t.shape[1]

    def body(x_ref, db_ref, al_ref, fb_ref, dt_ref, acs_ref, cum_ref):
        dt = _softplus(x_ref[0:16, :] + db_ref[...])
        dt_ref[...] = dt
        acs_ref[...] = _prefix_sum(dt * (-jnp.exp(al_ref[...])), CHUNK)
        cum_ref[...] = _prefix_sum(-_softplus(-(x_ref[16:32, :] + fb_ref[...])), S)

    return pl.pallas_call(body, name="dtf_fwd", out_shape=[SDS((16, S), F32)] * 3, compiler_params=_cparams())(
        dtf_t, dt_bias, a_log, f_bias)


def _dtf_bwd(dtf_t, dt_bias, a_log, f_bias, d_dt, d_acs_a, d_acs_b, d_cum):
    S = dtf_t.shape[1]

    def body(x_ref, db_ref, al_ref, fb_ref, ddt_ref, da1_ref, da2_ref, dc_ref, dx_ref, ddb_ref, dal_ref, dfb_ref):
        xd = x_ref[0:16, :] + db_ref[...]
        dt = _softplus(xd)
        a = -jnp.exp(al_ref[...])
        d_da = _suffix_sum(da1_ref[...] + da2_ref[...], CHUNK)
        d_dt = ddt_ref[...] + d_da * a
        dal_ref[...] = jnp.sum(d_da * dt, axis=1, keepdims=True) * a
        d_xd = d_dt * jax.nn.sigmoid(xd)
        ddb_ref[...] = jnp.sum(d_xd, axis=1, keepdims=True)
        xf = x_ref[16:32, :] + fb_ref[...]
        d_xf = _suffix_sum(dc_ref[...], S) * jax.nn.sigmoid(-xf)
        dfb_ref[...] = jnp.sum(d_xf, axis=1, keepdims=True)
        dx_ref[0:16, :] = d_xd
        dx_ref[16:32, :] = d_xf

    return pl.pallas_call(body, name="dtf_bwd", out_shape=[SDS((32, S), F32)] + [SDS((16, 1), F32)] * 3,
                          compiler_params=_cparams())(dtf_t, dt_bias, a_log, f_bias, d_dt, d_acs_a, d_acs_b, d_cum)


SSM_PAIRS = SSM_HEADS // 2 // SSM_GROUPS


def _ssd_pair(xs, dtc, acol, arow, bm, cm, cbm, h, hp):
    L = CHUNK
    first = _iota((1, LANES), 1) < SSM_P
    i16, s16 = _iota((L, 16), 1), _iota((16, L), 0)
    ha, hb = 2 * hp, 2 * hp + 1

    def selc(blk, hh):
        return jnp.sum(jnp.where(i16 == hh, blk, 0.0), axis=1, keepdims=True)

    def selr(blk, hh):
        return jnp.sum(jnp.where(s16 == hh, blk, 0.0), axis=0, keepdims=True)

    x = xs * jnp.where(first, selc(dtc, ha), selc(dtc, hb))
    ca, cb, ra, rb = selc(acol, ha), selc(acol, hb), selr(arow, ha), selr(arow, hb)
    tri = _iota((L, L), 0) >= _iota((L, L), 1)
    la = jnp.exp(jnp.where(tri, ca - ra, NEG))
    lb = jnp.exp(jnp.where(tri, cb - rb, NEG))
    y = jnp.where(first, mm_nn(cbm * la, x), mm_nn(cbm * lb, x))
    y = y + jnp.where(first, jnp.exp(ca), jnp.exp(cb)) * mm_nn(cm, h)
    last = _iota((1, L), 1) == L - 1
    ala = jnp.sum(jnp.where(last, ra, 0.0), axis=1, keepdims=True)
    alb = jnp.sum(jnp.where(last, rb, 0.0), axis=1, keepdims=True)
    dec = jnp.where(first, jnp.exp(ala - ca), jnp.exp(alb - cb))
    hn = jnp.where(first, jnp.exp(ala), jnp.exp(alb)) * h + mm_tn(bm, x * dec)
    return y, hn


def _ssd_group(*args, grp):
    xs, (dtc, acol, arow, bm, cm), hs = args[:SSM_PAIRS], args[SSM_PAIRS:SSM_PAIRS + 5], args[SSM_PAIRS + 5:]
    cbm = mm_nt(cm, bm)
    res = [_ssd_pair(xs[j], dtc, acol, arow, bm, cm, cbm, hs[j], SSM_PAIRS * grp + j) for j in range(SSM_PAIRS)]
    return tuple(r[0] for r in res) + tuple(r[1] for r in res)


def _ssd_specs(nc, rev):
    L = CHUNK
    cidx = (lambda c: nc - 1 - c) if rev else (lambda c: c)
    return dict(
        xs=pl.BlockSpec((L, SSM_PAIRS * LANES), lambda c, g: (cidx(c), g)),
        col=pl.BlockSpec((L, 16), lambda c, g: (cidx(c), 0)),
        row=pl.BlockSpec((16, L), lambda c, g: (0, cidx(c))),
        b=pl.BlockSpec((L, SSM_N), lambda c, g: (cidx(c), g)),
        c=pl.BlockSpec((L, SSM_N), lambda c, g: (cidx(c), SSM_GROUPS + g)),
        st=pl.BlockSpec((1, SSM_PAIRS, SSM_N, LANES), lambda c, g: (cidx(c), g, 0, 0)),
    )


def _lane_pieces(v):
    return [v[:, LANES * j:LANES * (j + 1)] for j in range(v.shape[1] // LANES)]


def _ssd_fwd(xs, dt_col, acs_col, acs_row, bc):
    S = xs.shape[0]
    nc, nhp = S // CHUNK, SSM_HEADS // 2
    sp = _ssd_specs(nc, False)

    def body(xs_ref, dt_ref, ac_ref, ar_ref, b_ref, c_ref, y_ref, hs_ref, h_scr):
        c, g = pl.program_id(0), pl.program_id(1)

        @pl.when(c == 0)
        def _():
            for j in range(SSM_PAIRS):
                h_scr[SSM_PAIRS * g + j] = jnp.zeros((SSM_N, LANES), F32)

        hs = [h_scr[SSM_PAIRS * g + j] for j in range(SSM_PAIRS)]
        for j in range(SSM_PAIRS):
            hs_ref[0, j] = hs[j]
        res = _ssd_group(*_lane_pieces(xs_ref[...]), dt_ref[...], ac_ref[...], ar_ref[...], b_ref[...], c_ref[...], *hs,
                         grp=g)
        y_ref[...] = jnp.concatenate(res[:SSM_PAIRS], axis=1)
        for j in range(SSM_PAIRS):
            h_scr[SSM_PAIRS * g + j] = res[SSM_PAIRS + j]

    return pl.pallas_call(
        body, name="ssd_fwd", grid=(nc, SSM_GROUPS),
        in_specs=[sp["xs"], sp["col"], sp["col"], sp["row"], sp["b"], sp["c"]],
        out_specs=[sp["xs"], sp["st"]],
        out_shape=[SDS((S, SSM_HEADS * SSM_P), F32), SDS((nc, nhp, SSM_N, LANES), F32)],
        scratch_shapes=[pltpu.VMEM((nhp, SSM_N, LANES), F32)],
        compiler_params=_cparams(dimension_semantics=("arbitrary", "arbitrary")),
    )(xs, dt_col, acs_col, acs_row, bc, bc)


def _ssd_bwd(xs, dt_col, acs_col, acs_row, bc, hs, dy):
    S = xs.shape[0]
    nc, nhp = S // CHUNK, SSM_HEADS // 2
    sp = _ssd_specs(nc, True)

    def body(xs_ref, dt_ref, ac_ref, ar_ref, b_ref, c_ref, hs_ref, dy_ref,
             dxs_ref, ddt_ref, dac_ref, dar_ref, db_ref, dc_ref, dh_scr):
        c, g = pl.program_id(0), pl.program_id(1)

        @pl.when(c == 0)
        def _():
            for j in range(SSM_PAIRS):
                dh_scr[SSM_PAIRS * g + j] = jnp.zeros((SSM_N, LANES), F32)

        _, vjp = jax.vjp(functools.partial(_ssd_group, grp=g), *_lane_pieces(xs_ref[...]), dt_ref[...], ac_ref[...],
                         ar_ref[...], b_ref[...], c_ref[...], *[hs_ref[0, j] for j in range(SSM_PAIRS)])
        grads = vjp(tuple(_lane_pieces(dy_ref[...])) + tuple(dh_scr[SSM_PAIRS * g + j] for j in range(SSM_PAIRS)))
        dxs_ref[...] = jnp.concatenate(grads[:SSM_PAIRS], axis=1)
        ddt, dac, dar, db, dc = grads[SSM_PAIRS:SSM_PAIRS + 5]
        for j in range(SSM_PAIRS):
            dh_scr[SSM_PAIRS * g + j] = grads[SSM_PAIRS + 5 + j]
        db_ref[...] = db
        dc_ref[...] = dc

        @pl.when(g == 0)
        def _():
            ddt_ref[...] = ddt
            dac_ref[...] = dac
            dar_ref[...] = dar

        @pl.when(g > 0)
        def _():
            ddt_ref[...] += ddt
            dac_ref[...] += dac
            dar_ref[...] += dar

    return pl.pallas_call(
        body, name="ssd_bwd", grid=(nc, SSM_GROUPS),
        in_specs=[sp["xs"], sp["col"], sp["col"], sp["row"], sp["b"], sp["c"], sp["st"], sp["xs"]],
        out_specs=[sp["xs"], sp["col"], sp["col"], sp["row"], sp["b"], sp["b"]],
        out_shape=[SDS((S, SSM_HEADS * SSM_P), F32), SDS((S, 16), F32), SDS((S, 16), F32), SDS((16, S), F32),
                   SDS((S, SSM_GROUPS * SSM_N), F32), SDS((S, SSM_GROUPS * SSM_N), F32)],
        scratch_shapes=[pltpu.VMEM((nhp, SSM_N, LANES), F32)],
        compiler_params=_cparams(dimension_semantics=("arbitrary", "arbitrary")),
    )(xs, dt_col, acs_col, acs_row, bc, bc, hs, dy)


ATT_T = 1024


def _pick_col(blk, h):
    return jnp.sum(jnp.where(_iota(blk.shape, 1) == h, blk, 0.0), axis=1, keepdims=True)


def _pick_row(blk, h):
    return jnp.sum(jnp.where(_iota(blk.shape, 0) == h, blk, 0.0), axis=0, keepdims=True)


def _pair_norm(x, g2, first):
    x2 = x * x
    sa = jnp.sum(jnp.where(first, x2, 0.0), axis=1, keepdims=True)
    sb = jnp.sum(jnp.where(first, 0.0, x2), axis=1, keepdims=True)
    r = jnp.where(first, lax.rsqrt(sa * (1.0 / ATT_D) + EPS), lax.rsqrt(sb * (1.0 / ATT_D) + EPS))
    return x * r * g2, r


def _pair_norm_bwd(dxn, x, r, g2, first):
    t = dxn * g2
    tx = t * x
    ma = jnp.sum(jnp.where(first, tx, 0.0), axis=1, keepdims=True)
    mb = jnp.sum(jnp.where(first, 0.0, tx), axis=1, keepdims=True)
    dx = r * (t - x * (r * r) * (jnp.where(first, ma, mb) * (1.0 / ATT_D)))
    return dx, jnp.sum(dxn * x * r, axis=0, keepdims=True)


def _fox_fwd(src, q_c0, k_c0, v_c0, gq2, gk2, cum_col, cum_row3):
    S = src.shape[0]
    T = ATT_T
    nq, nhp = S // T, ATT_HEADS // 2
    qb0, kb0, vb0 = q_c0 // LANES, k_c0 // LANES, v_c0 // LANES
    scale = ATT_D ** -0.5

    def body(q_ref, kraw_ref, v_ref, gq_ref, gk_ref, cc_ref, cr_ref, o_ref, l_ref, k_ref):
        hp, i = pl.program_id(0), pl.program_id(1)
        first = _iota((1, LANES), 1) < ATT_D

        @pl.when(i == 0)
        def _():
            k_ref[...] = _pair_norm(kraw_ref[...], gk_ref[...], first)[0].astype(BF16)

        H = T // 2

        def attend(tile, half):
            rows = pl.ds(half * H, H)
            row0 = tile * T + half * H
            klen = row0 + H
            q = (_pair_norm(q_ref[rows, :], gq_ref[...], first)[0] * scale).astype(BF16)
            zero = jnp.zeros_like(q)
            cc = cc_ref[rows, :]
            k = k_ref[0:klen, :]
            v = v_ref[0:klen, :].astype(BF16)
            allowed = _iota((H, klen), 0) + row0 >= _iota((H, klen), 1)
            outs, lses = [], []
            for hh in range(2):
                sel = first if hh == 0 else jnp.logical_not(first)
                ck = jnp.concatenate([_pick_row(cr_ref[j], 2 * hp + hh) for j in range(tile + 1)], axis=1)[:, :klen]
                s = _bdot(jnp.where(sel, q, zero), k, _NT) + (_pick_col(cc, 2 * hp + hh) - ck)
                s = jnp.where(allowed, s, NEG)
                m = jnp.max(s, axis=1, keepdims=True)
                p = jnp.exp(s - m)
                l = jnp.sum(p, axis=1, keepdims=True)
                outs.append(_bdot(p, v, _NN) / l)
                lses.append(m + jnp.log(l))
            o_ref[rows, :] = jnp.where(first, outs[0], outs[1]).astype(o_ref.dtype)
            l_ref[rows, :] = jnp.where(first, lses[0], lses[1])

        for tile in range(nq):
            @pl.when(i == tile)
            def _(tile=tile):
                for half in range(2):
                    attend(tile, half)

    gain = pl.BlockSpec((1, LANES), lambda hp, i: (0, 0))
    return pl.pallas_call(
        body, name="fox_fwd", grid=(nhp, nq),
        in_specs=[pl.BlockSpec((T, LANES), lambda hp, i: (i, qb0 + hp)), pl.BlockSpec((S, LANES), lambda hp, i: (0, kb0 + hp)),
                  pl.BlockSpec((S, LANES), lambda hp, i: (0, vb0 + hp)), gain, gain,
                  pl.BlockSpec((T, 16), lambda hp, i: (i, 0)), pl.BlockSpec((nq, 16, T), lambda hp, i: (0, 0, 0))],
        out_specs=[pl.BlockSpec((T, LANES), lambda hp, i: (i, hp))] * 2,
        out_shape=[SDS((S, ATT_HEADS * ATT_D), BF16), SDS((S, ATT_HEADS * ATT_D), F32)],
        scratch_shapes=[pltpu.VMEM((S, LANES), BF16)],
        compiler_params=_cparams(dimension_semantics=("arbitrary", "arbitrary")),
    )(src, src, src, gq2, gk2, cum_col, cum_row3)


def _fox_bwd(src, q_c0, k_c0, v_c0, gq2, gk2, cum_col, cum_row3, lse, dsrc, d_c0):
    S = src.shape[0]
    T = ATT_T
    nq, nhp = S // T, ATT_HEADS // 2
    qb0, kb0, vb0, db0 = q_c0 // LANES, k_c0 // LANES, v_c0 // LANES, d_c0 // LANES
    scale = ATT_D ** -0.5

    def body(q_ref, kraw_ref, v_ref, gq_ref, gk_ref, cc_ref, cr_ref, l_ref, do_ref,
             dq_ref, dk_ref, dv_ref, dc_ref, dg_ref, k_ref, dk_acc, dv_acc):
        hp, i = pl.program_id(0), pl.program_id(1)
        first = _iota((1, LANES), 1) < ATT_D

        @pl.when(i == 0)
        def _():
            k_ref[...] = _pair_norm(kraw_ref[...], gk_ref[...], first)[0].astype(BF16)
            dk_acc[...] = jnp.zeros_like(dk_acc)
            dv_acc[...] = jnp.zeros_like(dv_acc)
            dc_ref[...] = jnp.zeros_like(dc_ref)
            dg_ref[...] = jnp.zeros_like(dg_ref)

        H = T // 2

        def backprop(tile, half):
            rows = pl.ds(half * H, H)
            row0 = tile * T + half * H
            klen = row0 + H
            q_raw = q_ref[rows, :]
            qn, rq = _pair_norm(q_raw, gq_ref[...], first)
            q = (qn * scale).astype(BF16)
            zq = jnp.zeros_like(q)
            dob = do_ref[rows, :].astype(BF16)
            zd = jnp.zeros_like(dob)
            lse_blk, cc = l_ref[rows, :], cc_ref[rows, :]
            k = k_ref[0:klen, :]
            zk = jnp.zeros_like(k)
            allowed = _iota((H, klen), 0) + row0 >= _iota((H, klen), 1)
            dq = jnp.zeros((H, LANES), F32)
            for hh in range(2):
                sel = first if hh == 0 else jnp.logical_not(first)
                qh, doh = jnp.where(sel, q, zq), jnp.where(sel, dob, zd)
                bias_q = _pick_col(cc, 2 * hp + hh) - jnp.max(jnp.where(sel, lse_blk, NEG), axis=1, keepdims=True)
                ck = jnp.concatenate([_pick_row(cr_ref[j], 2 * hp + hh) for j in range(tile + 1)], axis=1)[:, :klen]
                p = jnp.exp(jnp.where(allowed, _bdot(qh, k, _NT) + (bias_q - ck), NEG))
                dp = _bdot(doh, v_ref[0:klen, :], _NT)
                ds = p * (dp - jnp.sum(p * dp, axis=1, keepdims=True))
                dv_acc[0:klen, :] += _bdot(p, doh, _TN)
                dk_acc[0:klen, :] += _bdot(ds, qh, _TN)
                dcs = jnp.sum(ds, axis=0, keepdims=True)
                for j in range(tile + 1):
                    n = min(T, klen - j * T)
                    dc_ref[0, j, hh:hh + 1, 0:n] -= dcs[:, j * T:j * T + n]
                dq = dq + _bdot(ds, jnp.where(sel, k, zk), _NN)
            dq_raw, dgq = _pair_norm_bwd(dq * scale, q_raw, rq, gq_ref[...], first)
            dq_ref[rows, :] = dq_raw.astype(dq_ref.dtype)
            dg_ref[0, 0:1, :] += dgq

        for tile in range(nq):
            @pl.when(i == tile)
            def _(tile=tile):
                for half in range(2):
                    backprop(tile, half)

        @pl.when(i == nq - 1)
        def _():
            k_raw = kraw_ref[...]
            rk = _pair_norm(k_raw, gk_ref[...], first)[1]
            dk_raw, dgk = _pair_norm_bwd(dk_acc[...], k_raw, rk, gk_ref[...], first)
            dk_ref[...] = dk_raw.astype(dk_ref.dtype)
            dv_ref[...] = dv_acc[...].astype(dv_ref.dtype)
            dg_ref[0, 1:2, :] = dgk

    gain = pl.BlockSpec((1, LANES), lambda hp, i: (0, 0))
    band = SDS((S, ATT_HEADS * ATT_D), BF16)
    return pl.pallas_call(
        body, name="fox_bwd", grid=(nhp, nq),
        in_specs=[pl.BlockSpec((T, LANES), lambda hp, i: (i, qb0 + hp)), pl.BlockSpec((S, LANES), lambda hp, i: (0, kb0 + hp)),
                  pl.BlockSpec((S, LANES), lambda hp, i: (0, vb0 + hp)), gain, gain,
                  pl.BlockSpec((T, 16), lambda hp, i: (i, 0)), pl.BlockSpec((nq, 16, T), lambda hp, i: (0, 0, 0)),
                  pl.BlockSpec((T, LANES), lambda hp, i: (i, hp)), pl.BlockSpec((T, LANES), lambda hp, i: (i, db0 + hp))],
        out_specs=[pl.BlockSpec((T, LANES), lambda hp, i: (i, hp)), pl.BlockSpec((S, LANES), lambda hp, i: (0, hp)),
                   pl.BlockSpec((S, LANES), lambda hp, i: (0, hp)), pl.BlockSpec((1, nq, 8, T), lambda hp, i: (hp, 0, 0, 0)),
                   pl.BlockSpec((1, 8, LANES), lambda hp, i: (hp, 0, 0))],
        out_shape=[band, band, band, SDS((nhp, nq, 8, T), F32), SDS((nhp, 8, LANES), F32)],
        scratch_shapes=[pltpu.VMEM((S, LANES), BF16), pltpu.VMEM((S, LANES), F32), pltpu.VMEM((S, LANES), F32)],
        compiler_params=_cparams(dimension_semantics=("arbitrary", "arbitrary")),
    )(src, src, src, gq2, gk2, cum_col, cum_row3, lse, dsrc)


def _fold_gains(dg):
    def body(d_ref, o_ref):
        t = d_ref[0]
        for h in range(1, dg.shape[0]):
            t = t + d_ref[h]
        o_ref[...] = t + pltpu.roll(t, ATT_D, axis=1)

    return pl.pallas_call(body, name="fold_gains", out_shape=SDS(dg.shape[1:], F32), compiler_params=_cparams())(dg)


def _adamw_math(w, g, m, v):
    m = ADAM_B1 * m + (1.0 - ADAM_B1) * g
    v = ADAM_B2 * v + (1.0 - ADAM_B2) * jnp.square(g)
    m_hat = m / (1.0 - ADAM_B1 ** ADAM_STEP)
    v_hat = v / (1.0 - ADAM_B2 ** ADAM_STEP)
    delta = -ADAM_LR * (m_hat / (jnp.sqrt(v_hat) + ADAM_EPS) + ADAM_WD * w)
    return delta, m, v


def _reduce_adamw(parts, w, m, v, *, tr, name, tc=None):
    R, C = w.shape
    tr, tc = min(tr, R), tc or C
    nparts = parts.shape[0]

    def body(p_ref, w_ref, m_ref, v_ref, g_ref, d_ref, nm_ref, nv_ref):
        g = p_ref[0].astype(F32)
        for s in range(1, nparts):
            g = g + p_ref[s].astype(F32)
        g_ref[...] = g
        d_ref[...], nm_ref[...], nv_ref[...] = _adamw_math(w_ref[...], g, m_ref[...], v_ref[...])

    blk = pl.BlockSpec((tr, tc), lambda i, j: (i, j))
    return pl.pallas_call(
        body, name=name, grid=(R // tr, C // tc),
        in_specs=[pl.BlockSpec((nparts, tr, tc), lambda i, j: (0, i, j)), blk, blk, blk], out_specs=[blk] * 4,
        out_shape=[SDS((R, C), F32)] * 4, compiler_params=_cparams(dimension_semantics=("parallel", "parallel")),
    )(parts, w, m, v)


def _adamw(w, g, m, v, *, name):
    def body(w_ref, g_ref, m_ref, v_ref, d_ref, nm_ref, nv_ref):
        d_ref[...], nm_ref[...], nv_ref[...] = _adamw_math(w_ref[...], g_ref[...], m_ref[...], v_ref[...])

    return pl.pallas_call(body, name=name, out_shape=[SDS(w.shape, F32)] * 3, compiler_params=_cparams())(w, g, m, v)


def _peers():
    x, y, c = lax.axis_index("x"), lax.axis_index("y"), lax.axis_index("c")
    out = []
    for k in range(1, N_DEV):
        px, py, pc = x ^ ((k >> 2) & 1), y ^ ((k >> 1) & 1), c ^ (k & 1)
        out.append(((px, py, pc), 4 * px + 2 * py + pc))
    return 4 * x + 2 * y + c, out


_HBM = pl.BlockSpec(memory_space=pltpu.HBM)
_SEM = pl.BlockSpec(memory_space=pltpu.SEMAPHORE)
_DATAFLOW = pltpu.SideEffectType.DATAFLOW_SIDE_EFFECTING


NEAR = (1, 2, 4, 6)


def _plan_peers(scatter, ks=tuple(range(1, N_DEV))):
    return lambda me, peers: [(peers[k - 1][0], peers[k - 1][1] if scatter else None, me, k - 1) for k in ks]


def _plan_relay(me, peers):
    return [(peers[0][0], peers[k - 1][1], peers[k - 1][1], j) for j, k in enumerate((2, 4, 6))]


def _plan_pair(me, peers):
    return [(peers[0][0], peers[k - 1][1], j, j) for j, k in enumerate((1, 3, 5, 7))]


def _plan_chips(me, peers):
    return [(peers[k - 1][0], k // 2, k // 2, k // 2) for k in (2, 4, 6)]


def _copy(src, dst, c, send_sems, recv_sems):
    dev, s_slot, d_slot, i = c
    return pltpu.make_async_remote_copy(
        src_ref=src if s_slot is None else src.at[s_slot], dst_ref=dst.at[d_slot], send_sem=send_sems.at[i],
        recv_sem=recv_sems.at[i], device_id=dev, device_id_type=MESH)


def _copies_start(items, *, name):
    n = len(items)
    bufs = [it[0] for it in items] + [it[1] for it in items if it[1] is not None]
    nb = len(bufs)

    def body(*refs):
        srcs, extra, sems, token = refs[:n], iter(refs[n:nb]), refs[nb:nb + 2 * n], refs[-1]
        me, peers = _peers()
        for a, (_, land, plan) in enumerate(items):
            dst = srcs[a] if land is None else next(extra)
            for c in plan(me, peers):
                _copy(srcs[a], dst, c, sems[2 * a], sems[2 * a + 1]).start()
        token[...] = jnp.zeros_like(token)

    res = pl.pallas_call(
        body, name=name,
        out_shape=([pltpu.SemaphoreType.DMA((N_DEV - 1,))] * (2 * n) + [pltpu.HBM(b.shape, b.dtype) for b in bufs]
                   + [SDS((8, LANES), F32)]),
        in_specs=[_HBM] * nb, out_specs=[_SEM] * (2 * n) + [_HBM] * nb + [pl.BlockSpec(memory_space=pltpu.VMEM)],
        input_output_aliases={i: 2 * n + i for i in range(nb)},
        compiler_params=pltpu.CompilerParams(has_side_effects=_DATAFLOW),
    )(*[pltpu.with_memory_space_constraint(b, pltpu.HBM) for b in bufs])
    sems, thru, token = res[:2 * n], list(res[2 * n:2 * n + nb]), res[-1]
    extra = iter(thru[n:])
    return [(thru[a], None if it[1] is None else next(extra), sems[2 * a], sems[2 * a + 1], it[2])
            for a, it in enumerate(items)], token


def _copies_wait(handles, after, *, name):
    n = len(handles)
    after = list(after) if isinstance(after, (list, tuple)) else [after]
    bufs = [h[0] for h in handles] + [h[1] for h in handles if h[1] is not None]
    nb = len(bufs)

    def body(*refs):
        srcs, extra, sems = refs[:n], iter(refs[n:nb]), refs[nb:nb + 2 * n]
        me, peers = _peers()
        for a, h in enumerate(handles):
            dst = srcs[a] if h[1] is None else next(extra)
            for c in h[4](me, peers):
                cp = _copy(srcs[a], dst, c, sems[2 * a], sems[2 * a + 1])
                cp.wait_send()
                cp.wait_recv()

    flat_sems = [s for h in handles for s in (h[2], h[3])]
    res = pl.pallas_call(
        body, name=name, out_shape=[pltpu.HBM(b.shape, b.dtype) for b in bufs],
        in_specs=[_HBM] * nb + [_SEM] * (2 * n) + [pl.BlockSpec(memory_space=pl.ANY)] * len(after), out_specs=[_HBM] * nb,
        input_output_aliases={i: i for i in range(nb)},
        compiler_params=pltpu.CompilerParams(has_side_effects=_DATAFLOW),
    )(*bufs, *flat_sems, *after)
    extra = iter(res[n:])
    return [(res[a], res[a] if h[1] is None else next(extra)) for a, h in enumerate(handles)]


def _exchange_start(arrays, *, scatter, name, near=()):
    items = []
    for a, arr in enumerate(arrays):
        land = lax.empty(arr.shape if scatter else (N_DEV,) + arr.shape, arr.dtype)
        items.append((arr, land, _plan_peers(scatter, NEAR) if a in near else _plan_peers(scatter)))
    return _copies_start(items, name=name)


MOVE_ROWS, MOVE_SLOTS = 512, 3


def _move_rows(src, moves, rows, *, name):
    C = src.shape[1]
    covered = max(dst + n for _, n, dst in moves)
    tail = rows - covered
    assert sum(n for _, n, _ in moves) == covered
    chunks = [(lo + o, min(MOVE_ROWS, n - o), dst + o) for lo, n, dst in moves for o in range(0, n, MOVE_ROWS)]
    nch = len(chunks)

    def body(src_ref, o_ref, buf, sin, sout, *zero):
        def fetch(i):
            lo, n, _ = chunks[i]
            return pltpu.make_async_copy(src_ref.at[pl.ds(lo, n)], buf.at[i % MOVE_SLOTS, pl.ds(0, n)], sin.at[i % MOVE_SLOTS])

        def store(i):
            _, n, dst = chunks[i]
            return pltpu.make_async_copy(buf.at[i % MOVE_SLOTS, pl.ds(0, n)], o_ref.at[pl.ds(dst, n)], sout.at[i % MOVE_SLOTS])

        if tail:
            zero[0][...] = jnp.zeros_like(zero[0])
            fill = pltpu.make_async_copy(zero[0], o_ref.at[pl.ds(covered, tail)], zero[1])
            fill.start()
        for i in range(nch):
            if i >= MOVE_SLOTS:
                store(i - MOVE_SLOTS).wait()
            fetch(i).start()
            if i >= 1:
                fetch(i - 1).wait()
                store(i - 1).start()
        fetch(nch - 1).wait()
        store(nch - 1).start()
        for i in range(max(0, nch - MOVE_SLOTS), nch):
            store(i).wait()
        if tail:
            fill.wait()

    anyspec = pl.BlockSpec(memory_space=pl.ANY)
    dma = pltpu.SemaphoreType.DMA
    return pl.pallas_call(
        body, name=name, in_specs=[anyspec], out_specs=anyspec, out_shape=SDS((rows, C), src.dtype),
        scratch_shapes=([pltpu.VMEM((MOVE_SLOTS, MOVE_ROWS, C), src.dtype), dma((MOVE_SLOTS,)), dma((MOVE_SLOTS,))]
                        + ([pltpu.VMEM((tail, C), src.dtype), dma] if tail else [])),
        compiler_params=_cparams())(src)


def _pair_sum(a, b, *, name):
    n, R, C = a.shape
    tc = 256

    def body(a_ref, b_ref, o_ref):
        o_ref[...] = (a_ref[...].astype(F32) + b_ref[...].astype(F32)).astype(o_ref.dtype)

    blk = pl.BlockSpec((1, R, tc), lambda i, j: (i, 0, j))
    return pl.pallas_call(body, name=name, grid=(n, C // tc), in_specs=[blk, blk], out_specs=blk,
                          out_shape=SDS(a.shape, a.dtype), compiler_params=_cparams(dimension_semantics=("parallel", "parallel")))(a, b)


def _own_slot(landed, own, me):
    return lax.dynamic_update_slice(landed, own[None], (me,) + (0,) * own.ndim)


SMALL = (("g_mix", 1024), ("conv_w", 6144), ("conv_b", 1536), ("dt_bias", 16), ("a_log", 16), ("d_skip", 16),
         ("ssm_norm_w", 1024), ("g_q", 64), ("g_k", 64), ("f_bias", 16), ("g_xattn", 1024), ("g_mem", 1024),
         ("xg_q", 256), ("xg_k", 256), ("g_mlp", 1024), ("loss", 1))
NOT_PARAMS = ("conv_w", "loss")
SLAB_ROWS = 112
BIG = ("w_in", "w_out", "xq_w", "xkv_w", "xo_w", "w_up", "w_down")
WEIGHTS = ("g_mix", "w_in", "conv_w", "conv_b", "dt_bias", "a_log", "d_skip", "ssm_norm_w", "g_q", "g_k", "f_bias", "w_out",
           "g_xattn", "g_mem", "xq_w", "xkv_w", "xg_q", "xg_k", "xo_w", "g_mlp", "w_up", "w_down")
O_Z, O_XS, O_B, O_C, O_DT, O_Q, O_K, O_V, O_F, O_END = 0, 1024, 2048, 2304, 2560, 2576, 3600, 4624, 5648, 5664
IN_ROW_MOVES = ((O_Z, O_B - O_Z, C_Z), (O_Q, O_F - O_Q, C_Q), (O_B, O_Q - O_B, C_B), (O_F, O_END - O_F, C_DTF + 16))


def _pack_small(vals):
    rows = []
    for name, size in SMALL:
        flat = vals[name].reshape(-1).astype(F32)
        pad = -size % LANES
        rows.append(jnp.pad(flat, (0, pad)).reshape(-1, LANES))
    slab = jnp.concatenate(rows, axis=0)
    return jnp.pad(slab, ((0, SLAB_ROWS - slab.shape[0]), (0, 0)))


def _unpack_small(slab):
    out, r = {}, 0
    for name, size in SMALL:
        nr = -(-size // LANES)
        out[name] = slab[r:r + nr].reshape(-1)[:size]
        r += nr
    return out


def _step(p, m, v, x, mem, target):
    S = x.shape[0]
    TM = 256
    me = 4 * lax.axis_index("x") + 2 * lax.axis_index("y") + lax.axis_index("c")

    def rms(u, g, name):
        return _rw_fwd(_rms_fn, [_whole(u)], [_whole(g)], [(D_MODEL, BF16)], tm=TM, name=name)[0]

    def pin(param, token):
        return param + token[0:1, 0:1]

    def landed_with_own(pairs, scatter):
        out = []
        for src, land in pairs:
            own = lax.dynamic_index_in_dim(src, me, 0, keepdims=False) if scatter else src
            out.append(_own_slot(land, own, me))
        return out

    w_in_own, m_in_own, v_in_own = p["w_in"].T, m["w_in"].T, v["w_in"].T
    ag, ag_token = _exchange_start([w_in_own.astype(BF16), p["conv_w"]] + [p[n].astype(BF16) for n in BIG[1:]],
                                   scatter=False, name="allgather_start", near=(0, 2, 3, 4, 5, 6, 7))
    h1 = rms(x, pin(p["g_mix"], ag_token), "rms_mix")
    stand_in = {"conv_w": jnp.zeros((4, 1536), F32), "loss": jnp.zeros((1,), F32)}
    slabs = [_pack_small({**d, **stand_in}) for d in (p, m, v)]
    (win_src, win_land), convw_pair = _copies_wait(ag[:2], [h1, w_in_own, m_in_own, v_in_own] + slabs,
                                                   name="allgather_wait_in")
    relay, token = _copies_start([(win_land, None, _plan_relay)], name="allgather_relay_start")
    win_land = _copies_wait(relay, token, name="allgather_relay_wait")[0][1]
    win_g, convw_g = landed_with_own([(win_src, win_land), convw_pair], False)
    w_in_o = win_g.reshape(O_END, D_MODEL)
    w_in_t = _move_rows(w_in_o, IN_ROW_MOVES, P_COLS, name="w_in_rows")
    conv_w = convw_g.transpose(1, 0, 2).reshape(4, 1536)
    cw_xs, cw_bc = conv_w[:, :1024], conv_w[:, 1024:]
    cb_xs, cb_bc = p["conv_b"][:, :1024], p["conv_b"][:, 1024:]
    dt_bias, a_log, f_bias = p["dt_bias"].reshape(16, 1), p["a_log"].reshape(16, 1), p["f_bias"].reshape(16, 1)

    proj = _matmul(h1, w_in_t, mode="nt", tm=1024, tn=640, tk=1024, name="mm_in")
    xs_c = _conv_fwd(proj, C_XS, 1024, cw_xs, cb_xs, name="conv_xs")
    bc_c = _conv_fwd(proj, C_B, 512, cw_bc, cb_bc, name="conv_bc")
    dtf_t = proj[:, C_DTF:C_DTF + 32].T
    dt_t, acs_t, cum_t = _dtf_fwd(dtf_t, dt_bias, a_log, f_bias)
    dt_col, acs_col, cum_col = dt_t.T, acs_t.T, cum_t.T
    cum_row3 = cum_t.reshape(16, S // ATT_T, ATT_T).transpose(1, 0, 2)
    y_ssd, hs = _ssd_fwd(xs_c, dt_col, acs_col, acs_t, bc_c)
    gate_rows = [_whole(y_ssd), _whole(xs_c), (proj, C_Z, 1024)]
    gate_pars = [_whole(p["d_skip"]), _whole(p["ssm_norm_w"])]
    y_ssm = _rw_fwd(_gate_fn, gate_rows, gate_pars, [(1024, BF16)], tm=TM, name="gate")[0]
    gq2, gk2 = jnp.tile(p["g_q"], (1, 2)), jnp.tile(p["g_k"], (1, 2))
    o, lse = _fox_fwd(proj, C_Q, C_K, C_V, gq2, gk2, cum_col, cum_row3)
    mixed = jnp.concatenate([y_ssm, o], axis=1)
    arrived = _copies_wait(ag[2:], mixed, name="allgather_wait_rest")
    relay, token = _copies_start([(land, None, _plan_relay) for _, land in arrived], name="allgather_relay_rest_start")
    wout_g, = landed_with_own([(arrived[0][0], _copies_wait(relay[:1], token, name="allgather_relay_out_wait")[0][1])], False)
    w_out = wout_g.reshape(2 * D_MODEL, D_MODEL)
    x1, h2 = _matmul(mixed, w_out, mode="nn", tm=1024, tn=1024, tk=2048, name="mm_out", extras=(x,),
                     row_params=(p["g_xattn"],), epilogue=_ep_residual_rms, out_dtypes=[F32, BF16])
    relayed = _copies_wait(relay[1:], x1, name="allgather_relay_rest_wait")
    xq_g, xkv_w, xo_g, w_up, wdown_g = landed_with_own(
        [(src, land) for (src, _), (_, land) in zip(arrived[1:], relayed, strict=True)], False)
    xq_w = xq_g.reshape(D_MODEL, D_MODEL)
    xo_w = xo_g.reshape(D_MODEL, D_MODEL)
    w_down = wdown_g.reshape(4 * D_MODEL, D_MODEL)

    mem_n = rms(mem, p["g_mem"], "rms_mem")
    q2 = _matmul(h2, xq_w, mode="nn", tm=1024, tn=512, tk=1024, name="mm_xq")
    kv = _matmul(mem_n, xkv_w, mode="nn", b_shards=True, tm=256, tn=256, tk=1024, name="mm_xkv")
    xa_rows = [(q2, X_D * h, X_D) for h in range(X_HEADS)]
    xa_pars = ([(kv, X_D * h, X_D) for h in range(X_HEADS)] + [(kv, D_MODEL + X_D * h, X_D) for h in range(X_HEADS)]
               + [_whole(p["xg_q"]), _whole(p["xg_k"])])
    o2 = _rw_fwd(_xattn_fn, xa_rows, xa_pars, [(D_MODEL, BF16)], tm=2 * TM, name="xattn")[0]
    x2, h3 = _matmul(o2, xo_w, mode="nn", tm=1024, tn=1024, tk=1024, name="mm_xo", extras=(x1,),
                     row_params=(p["g_mlp"],), epilogue=_ep_residual_rms, out_dtypes=[F32, BF16])

    a, usq = _matmul(h3, w_up, mode="nn", b_shards=True, tm=2048, tn=512, tk=1024, name="mm_up", out_dtypes=[F32, BF16],
                     epilogue=lambda acc: (acc, jnp.square(jax.nn.relu(acc))))
    dy, loss_part = _matmul(usq, w_down, mode="nn", tm=1024, tn=512, tk=2048, name="mm_down", extras=(x2, target),
                            epilogue=functools.partial(_ep_loss, width=D_MODEL), sums=[(1, 1)])

    def row_shards(a):
        r, c = a.shape
        return a.reshape(N_DEV, r // N_DEV, c)

    g = {"loss": loss_part}
    g["w_down"] = _matmul(usq, dy, mode="tn", out_dtype=GRAD_WIRE, tm=1024, tn=1024, tk=1024, name="mm_d_wdown")
    da = _matmul(dy, w_down, mode="nt", tm=1024, tn=1024, tk=1024, name="mm_d_usq", out_dtype=BF16, extras=(a,),
                 epilogue=lambda acc, av: (2.0 * jax.nn.relu(av) * acc,))
    g["w_up"] = _matmul(h3, da, mode="tn", out_shards=True, out_dtype=GRAD_WIRE, tm=1024, tn=512, tk=1024, name="mm_d_wup")
    sent_mlp, token = _exchange_start([row_shards(g["w_down"]), g["w_up"]], scatter=True,
                                      name="grads_start_mlp")
    dx2, g["g_mlp"] = _matmul(da, w_up, mode="nt", b_shards=True, tm=1024, tn=1024, tk=512, name="mm_d_h3",
                              extras=(x2, dy), row_params=(pin(p["g_mlp"], token),), epilogue=_ep_rms_bwd,
                              sums=[(1, D_MODEL)])

    g["xo_w"] = _matmul(o2, dx2, mode="tn", out_dtype=GRAD_WIRE, tm=1024, tn=1024, tk=1024, name="mm_d_wxo")
    do2 = _matmul(dx2, xo_w, mode="nt", tm=1024, tn=512, tk=1024, name="mm_d_o2")
    dq2, dkv, g["xg_q"], g["xg_k"] = _rw_bwd(_xattn_fn, xa_rows, xa_pars, [_whole(do2)], tm=2 * TM, name="xattn_bwd",
                                             row_grads=[BF16] * X_HEADS, join_rows=True, join_params=2 * X_HEADS)
    g["xq_w"] = _matmul(h2, dq2, mode="tn", out_dtype=GRAD_WIRE, tm=1024, tn=1024, tk=1024, name="mm_d_wxq")
    dx1, g["g_xattn"] = _matmul(dq2, xq_w, mode="nt", tm=1024, tn=1024, tk=1024, name="mm_d_h2", extras=(x1, dx2),
                                row_params=(p["g_xattn"],), epilogue=_ep_rms_bwd, sums=[(1, D_MODEL)])
    g["xkv_w"] = _matmul(mem_n, dkv, mode="tn", out_shards=True, out_dtype=GRAD_WIRE, tm=1024, tn=256, tk=256,
                         name="mm_d_wxkv")
    dmem_n = _matmul(dkv, xkv_w, mode="nt", b_shards=True, tm=256, tn=1024, tk=256, name="mm_d_memn")
    g["g_mem"] = _rw_bwd(_rms_fn, [_whole(mem)], [_whole(p["g_mem"])], [_whole(dmem_n)], tm=TM, name="rms_mem_bwd",
                         row_grads=[None])[0]

    g["w_out"] = _matmul(mixed, dx1, mode="tn", out_dtype=GRAD_WIRE, tm=1024, tn=1024, tk=1024, name="mm_d_wout")
    sent_mid, token = _exchange_start(
        [row_shards(g["w_out"]), row_shards(g["xq_w"]), g["xkv_w"], row_shards(g["xo_w"])], scatter=True,
        name="grads_start_mid")
    dmixed = _matmul(dx1, w_out, mode="nt", tm=1024, tn=1024, tk=1024, name="mm_d_mixed")
    dq, dk, dv, dcum4, dgain = _fox_bwd(proj, C_Q, C_K, C_V, pin(gq2, token), gk2, cum_col, cum_row3, lse, dmixed, 1024)
    gains = _fold_gains(dgain)
    g["g_q"], g["g_k"] = gains[0:1, :ATT_D], gains[1:2, :ATT_D]
    dy_ssd, dxs_g, dz, g["d_skip"], g["ssm_norm_w"] = _rw_bwd(
        _gate_fn, gate_rows, gate_pars, [(dmixed, 0, 1024)], tm=TM, name="gate_bwd", row_grads=[F32, F32, BF16])
    dxs_s, ddt_col, dacs_col, dacs_row, d_b, d_c = _ssd_bwd(xs_c, dt_col, acs_col, acs_t, bc_c, hs, dy_ssd)
    dcum_t = dcum4[:, :, 0:2, :].transpose(0, 2, 1, 3).reshape(16, S)
    ddtf_t, ddtb, dalog, dfb = _dtf_bwd(dtf_t, dt_bias, a_log, f_bias, ddt_col.T, dacs_col.T, dacs_row, dcum_t)
    g["dt_bias"], g["a_log"], g["f_bias"] = ddtb, dalog, dfb
    dxs_raw, dcw_xs, dcb_xs = _conv_bwd(proj, C_XS, 1024, cw_xs, cb_xs, [dxs_s, dxs_g], name="conv_xs_bwd")
    dbc_raw, dcw_bc, dcb_bc = _conv_bwd(proj, C_B, 512, cw_bc, cb_bc, [jnp.concatenate([d_b, d_c], axis=1)],
                                        name="conv_bc_bwd")
    g["conv_w"] = jnp.concatenate([dcw_xs, dcw_bc], axis=1)
    g["conv_b"] = jnp.concatenate([dcb_xs, dcb_bc], axis=1)
    ddtf = jnp.pad(ddtf_t.T.astype(BF16), ((0, 0), (0, P_COLS - C_DTF - 32)))
    dproj = jnp.concatenate([dz, dxs_raw, dq, dk, dv, dbc_raw, ddtf], axis=1)
    dw_in_p = _matmul(dproj, h1, mode="tn", out_dtype=GRAD_WIRE, tm=640, tn=1024, tk=1024, name="mm_d_win")
    g["w_in"] = _move_rows(dw_in_p, [(dst, n, lo) for lo, n, dst in IN_ROW_MOVES], O_END, name="d_w_in_rows")
    half = N_DEV // 2
    send_in = row_shards(g["w_in"])
    pair, token = _copies_start([(send_in, lax.empty((half,) + send_in.shape[1:], send_in.dtype), _plan_pair)],
                                name="grads_in_pair_start")
    send_in, from_sibling = _copies_wait(pair, token, name="grads_in_pair_wait")[0]
    mine = jnp.stack([lax.dynamic_index_in_dim(send_in, me ^ (2 * j), 0, keepdims=False) for j in range(half)])
    chip_sums = _pair_sum(mine, from_sibling, name="grads_in_pair_sum")
    sent_in, token = _copies_start([(chip_sums, lax.empty(chip_sums.shape, chip_sums.dtype), _plan_chips)],
                                   name="grads_in_chip_start")
    grad_x, g["g_mix"] = _matmul(dproj, w_in_t, mode="nn", tm=1024, tn=1024, tk=1152, name="mm_d_h1", extras=(x, dx1),
                                 row_params=(pin(p["g_mix"], token),), epilogue=_ep_rms_bwd, sums=[(1, D_MODEL)])
    sent_small, _ = _exchange_start([_pack_small(g)], scatter=False, name="small_grads_start")

    grads, delta, new_m, new_v = {}, {}, {}, {}

    def update(names, sent, after, wait_name):
        parts = landed_with_own(_copies_wait(sent, after, name=wait_name), True)
        for name, part in zip(names, parts, strict=True):
            grads[name], delta[name], new_m[name], new_v[name] = _reduce_adamw(part, p[name], m[name], v[name], tr=128,
                                                                                name="adamw_" + name)

    update(("w_down", "w_up"), sent_mlp, grad_x, "grads_wait_mlp")
    update(("w_out", "xq_w", "xkv_w", "xo_w"), sent_mid, delta["w_up"], "grads_wait_mid")
    chip_sums, landed = _copies_wait(sent_in, delta["xo_w"], name="grads_in_chip_wait")[0]
    part = lax.dynamic_update_slice(landed, chip_sums[0:1], (0, 0, 0))
    res = _reduce_adamw(part, w_in_own, m_in_own, v_in_own, tr=part.shape[1], tc=256, name="adamw_w_in")
    grads["w_in"], delta["w_in"], new_m["w_in"], new_v["w_in"] = [r.T for r in res]
    small_parts = landed_with_own(_copies_wait(sent_small, delta["w_in"], name="small_grads_wait"), False)[0]
    sg, sd, sm, sv = _reduce_adamw(small_parts, *slabs, tr=SLAB_ROWS, name="adamw_small")
    for dst, slab in ((grads, sg), (delta, sd), (new_m, sm), (new_v, sv)):
        for name, flat in _unpack_small(slab).items():
            if name not in NOT_PARAMS:
                dst[name] = flat.reshape(p[name].shape)
    loss = _unpack_small(sg)["loss"][0]
    cw_shard = p["conv_w"].shape[1]
    grads["conv_w"] = lax.dynamic_slice(_unpack_small(sg)["conv_w"].reshape(4, 1536), (0, me * cw_shard), (4, cw_shard))
    delta["conv_w"], new_m["conv_w"], new_v["conv_w"] = _adamw(p["conv_w"], grads["conv_w"], m["conv_w"], v["conv_w"],
                                                               name="adamw_conv_w")
    return loss, grad_x, grads, delta, new_m, new_v


def kernel(x, mem, g_mix, w_in, conv_w, conv_b, dt_bias, a_log, d_skip, ssm_norm_w, g_q, g_k, f_bias, w_out, g_xattn, g_mem, xq_w, xkv_w, xg_q, xg_k, xo_w, g_mlp, w_up, w_down, loss_target, m_g_mix, m_w_in, m_conv_w, m_conv_b, m_dt_bias, m_a_log, m_d_skip, m_ssm_norm_w, m_g_q, m_g_k, m_f_bias, m_w_out, m_g_xattn, m_g_mem, m_xq_w, m_xkv_w, m_xg_q, m_xg_k, m_xo_w, m_g_mlp, m_w_up, m_w_down, v_g_mix, v_w_in, v_conv_w, v_conv_b, v_dt_bias, v_a_log, v_d_skip, v_ssm_norm_w, v_g_q, v_g_k, v_f_bias, v_w_out, v_g_xattn, v_g_mem, v_xq_w, v_xkv_w, v_xg_q, v_xg_k, v_xo_w, v_g_mlp, v_w_up, v_w_down):
    args = locals()
    drop = lambda t: t[0] if t.ndim == 3 else t
    p = {n: drop(args[n]) for n in WEIGHTS}
    m = {n: drop(args["m_" + n]) for n in WEIGHTS}
    v = {n: drop(args["v_" + n]) for n in WEIGHTS}
    loss, grad_x, grads, delta, new_m, new_v = _step(p, m, v, x[0], mem[0], loss_target[0])
    outs = [loss, grad_x[None]]
    for d in (grads, delta, new_m, new_v):
        outs += [d[n].reshape(args[n].shape) for n in WEIGHTS]
    return tuple(outs)
```

```python
import functools

import jax
import jax.numpy as jnp
from jax import lax
from jax.experimental import pallas as pl
from jax.experimental.pallas import tpu as pltpu

F32, BF16 = jnp.float32, jnp.bfloat16
SDS = jax.ShapeDtypeStruct
HI = lax.Precision.HIGHEST
MESH = pl.DeviceIdType.MESH

N_DEV = 8
EPS = 1e-5
D_MODEL = 1024
SSM_HEADS, SSM_P, SSM_N, SSM_GROUPS, CHUNK = 16, 64, 128, 2, 128
ATT_HEADS, ATT_D = 16, 64
X_HEADS, X_D = 4, 256
LANES = 128
VMEM_LIMIT = 48 * 1024 * 1024
NEG = -1e30

GRAD_WIRE = BF16
ADAM_LR, ADAM_B1, ADAM_B2, ADAM_EPS, ADAM_WD, ADAM_STEP = 0.001, 0.9, 0.999, 1e-08, 0.01, 10

C_Z, C_XS, C_Q, C_K, C_V, C_B, C_C, C_DTF, P_COLS = 0, 1024, 2048, 3072, 4096, 5120, 5376, 5632, 5760

_NN = (((1,), (0,)), ((), ()))
_NT = (((1,), (1,)), ((), ()))
_TN = (((0,), (0,)), ((), ()))


def _cparams(**kw):
    return pltpu.CompilerParams(vmem_limit_bytes=VMEM_LIMIT, **kw)


def _bdot(a, b, dn):
    return lax.dot_general(a.astype(BF16), b.astype(BF16), dn, preferred_element_type=F32)


@jax.custom_vjp
def mm_nn(a, b):
    return _bdot(a, b, _NN)


mm_nn.defvjp(lambda a, b: (mm_nn(a, b), (a, b)), lambda r, g: (_bdot(g, r[1], _NT), _bdot(r[0], g, _TN)))


@jax.custom_vjp
def mm_nt(a, b):
    return _bdot(a, b, _NT)


mm_nt.defvjp(lambda a, b: (mm_nt(a, b), (a, b)), lambda r, g: (_bdot(g, r[1], _NN), _bdot(g, r[0], _TN)))


@jax.custom_vjp
def mm_tn(a, b):
    return _bdot(a, b, _TN)


mm_tn.defvjp(lambda a, b: (mm_tn(a, b), (a, b)), lambda r, g: (_bdot(r[1], g, _NT), _bdot(r[0], g, _NN)))


def _cdot(x, c):
    return jnp.dot(x, c, precision=HI, preferred_element_type=F32)


def _iota(shape, dim):
    return lax.broadcasted_iota(jnp.int32, shape, dim)


def _matmul(a, b, *, mode, tm, tn, tk, name, out_dtype=F32, add=None, extras=(), epilogue=None, out_dtypes=None,
            b_shards=False, out_shards=False, row_params=(), sums=()):
    if mode == "tn":
        K, M = a.shape
    else:
        M, K = a.shape
    if b_shards:
        N = b.shape[1] if mode == "nt" else b.shape[0] * b.shape[2]
        tn, tk = (tn, b.shape[2]) if mode == "nt" else (b.shape[2], tk)
    else:
        N = b.shape[0] if mode == "nt" else b.shape[1]
    tm, tn, tk = min(tm, M), min(tn, N), min(tk, K)
    assert M % tm == 0 and N % tn == 0 and K % tk == 0, (name, M, N, K, tm, tn, tk)
    assert not b_shards or (K // tk if mode == "nt" else N // tn) == b.shape[0], name
    assert not (out_shards and (extras or add is not None)), name
    nk = K // tk
    dn = {"nn": _NN, "nt": _NT, "tn": _TN}[mode]
    if add is not None:
        extras, epilogue = (add,), lambda acc, r: (acc + r,)
    elif epilogue is None:
        epilogue = lambda acc: (acc,)
    out_dtypes = out_dtypes or [out_dtype]
    ne, no, ns = len(extras) + len(row_params), len(out_dtypes), len(sums)
    assert all(s == (1, 1) or (s == (1, N) and tn == N) for s in sums), name

    def body(*refs):
        a_ref, b_ref = refs[:2]
        e_refs, o_refs, s_refs = refs[2:2 + ne], refs[2 + ne:2 + ne + no], refs[2 + ne + no:2 + ne + no + ns]

        def finish(acc):
            res = epilogue(acc, *[e[...] for e in e_refs])
            for o_ref, v in zip(o_refs, res[:no], strict=True):
                o_ref[...] = v.astype(o_ref.dtype)
            first_tile = jnp.logical_and(pl.program_id(0) == 0, pl.program_id(1) == 0)
            for s_ref, v in zip(s_refs, res[no:], strict=True):
                @pl.when(first_tile)
                def _(s_ref=s_ref, v=v):
                    s_ref[...] = v

                @pl.when(jnp.logical_not(first_tile))
                def _(s_ref=s_ref, v=v):
                    s_ref[...] += v

        prod = _bdot(a_ref[...], b_ref[...], dn)
        if nk == 1:
            finish(prod)
            return
        acc_ref = refs[-1]
        k = pl.program_id(2)

        @pl.when(k == 0)
        def _():
            acc_ref[...] = prod

        @pl.when(jnp.logical_and(k > 0, k < nk - 1))
        def _():
            acc_ref[...] += prod

        @pl.when(k == nk - 1)
        def _():
            finish(acc_ref[...] + prod)

    a_spec = pl.BlockSpec((tk, tm), lambda i, j, k: (k, i)) if mode == "tn" else pl.BlockSpec((tm, tk), lambda i, j, k: (i, k))
    if b_shards and mode == "nt":
        b_spec = pl.BlockSpec((None, tn, tk), lambda i, j, k: (k, j, 0))
    elif b_shards:
        b_spec = pl.BlockSpec((None, tk, tn), lambda i, j, k: (j, k, 0))
    elif mode == "nt":
        b_spec = pl.BlockSpec((tn, tk), lambda i, j, k: (j, k))
    else:
        b_spec = pl.BlockSpec((tk, tn), lambda i, j, k: (k, j))
    if out_shards:
        o_spec, o_shape = pl.BlockSpec((None, tm, tn), lambda i, j, k: (j, i, 0)), (N // tn, M, tn)
    else:
        o_spec, o_shape = pl.BlockSpec((tm, tn), lambda i, j, k: (i, j)), (M, N)
    row_spec = pl.BlockSpec((1, tn), lambda i, j, k: (0, j))
    sum_specs = [pl.BlockSpec(s, lambda i, j, k: (0, 0)) for s in sums]
    res = pl.pallas_call(
        body, name=name, grid=(M // tm, N // tn, nk),
        in_specs=[a_spec, b_spec] + [o_spec] * len(extras) + [row_spec] * len(row_params),
        out_specs=[o_spec] * no + sum_specs, out_shape=[SDS(o_shape, dt) for dt in out_dtypes] + [SDS(s, F32) for s in sums],
        scratch_shapes=[pltpu.VMEM((tm, tn), F32)] if nk > 1 else [],
        compiler_params=_cparams(dimension_semantics=(("arbitrary",) * 3 if sums else ("parallel", "parallel", "arbitrary"))),
    )(a, b, *extras, *row_params)
    return res[0] if no + ns == 1 else res


def _row_spec(tm, spec):
    _, c0, w = spec
    assert c0 % w == 0
    return pl.BlockSpec((tm, w), functools.partial(lambda i, cb: (i, cb), cb=c0 // w))


def _par_spec(spec):
    arr, c0, w = spec
    assert c0 % w == 0
    return pl.BlockSpec((arr.shape[0], w), functools.partial(lambda i, cb: (0, cb), cb=c0 // w))


def _whole(arr):
    return (arr, 0, arr.shape[1])


def _rw_fwd(fn, rows, params, outs, *, tm, name):
    M = rows[0][0].shape[0]
    nr, npar = len(rows), len(params)

    def body(*refs):
        rv = [r[...].astype(F32) for r in refs[:nr]]
        pv = [p[...].astype(F32) for p in refs[nr:nr + npar]]
        res = fn(*rv, *pv)
        for o_ref, v in zip(refs[nr + npar:], res, strict=True):
            o_ref[...] = v.astype(o_ref.dtype)

    return pl.pallas_call(
        body, name=name, grid=(M // tm,),
        in_specs=[_row_spec(tm, r) for r in rows] + [_par_spec(p) for p in params],
        out_specs=[pl.BlockSpec((tm, w), lambda i: (i, 0)) for w, _ in outs],
        out_shape=[SDS((M, w), dt) for w, dt in outs],
        compiler_params=_cparams(dimension_semantics=("parallel",)),
    )(*[r[0] for r in rows], *[p[0] for p in params])


def _rw_bwd(fn, rows, params, cts, *, tm, name, row_grads, adds=None, join_rows=False, join_params=0):
    M = rows[0][0].shape[0]
    adds = adds or {}
    nr, npar, nc = len(rows), len(params), len(cts)
    add_keys = sorted(adds)
    want = [k for k in range(nr) if row_grads[k] is not None]

    def body(*refs):
        pos = 0
        r_refs = refs[pos:pos + nr]; pos += nr
        p_refs = refs[pos:pos + npar]; pos += npar
        c_refs = refs[pos:pos + nc]; pos += nc
        a_refs = dict(zip(add_keys, refs[pos:pos + len(add_keys)])); pos += len(add_keys)
        n_row_out = 1 if join_rows else len(want)
        dr_refs = refs[pos:pos + n_row_out]; pos += n_row_out
        dp_refs = refs[pos:]
        rv = [r[...].astype(F32) for r in r_refs]
        pv = [p[...].astype(F32) for p in p_refs]
        _, vjp = jax.vjp(fn, *rv, *pv)
        g = vjp(tuple(c[...].astype(F32) for c in c_refs))
        row_vals = []
        for k in want:
            v = g[k]
            if k in a_refs:
                v = v + a_refs[k][...].astype(F32)
            row_vals.append(v)
        if join_rows:
            row_vals = [jnp.concatenate(row_vals, axis=1)]
        for ref, v in zip(dr_refs, row_vals, strict=True):
            ref[...] = v.astype(ref.dtype)
        par_vals = list(g[nr:])
        if join_params:
            par_vals = [jnp.concatenate(par_vals[:join_params], axis=1)] + par_vals[join_params:]
        first = pl.program_id(0) == 0
        for ref, v in zip(dp_refs, par_vals, strict=True):
            @pl.when(first)
            def _(ref=ref, v=v):
                ref[...] = v

            @pl.when(jnp.logical_not(first))
            def _(ref=ref, v=v):
                ref[...] += v

    row_out = [(rows[k][2], row_grads[k]) for k in want]
    if join_rows:
        row_out = [(sum(w for w, _ in row_out), row_out[0][1])]
    par_out = [(p[0].shape[0], p[2]) for p in params]
    if join_params:
        par_out = [(par_out[0][0], sum(w for _, w in par_out[:join_params]))] + par_out[join_params:]
    res = pl.pallas_call(
        body, name=name, grid=(M // tm,),
        in_specs=([_row_spec(tm, r) for r in rows] + [_par_spec(p) for p in params] + [_row_spec(tm, c) for c in cts]
                  + [_row_spec(tm, adds[k]) for k in add_keys]),
        out_specs=([pl.BlockSpec((tm, w), lambda i: (i, 0)) for w, _ in row_out]
                   + [pl.BlockSpec(s, lambda i: (0, 0)) for s in par_out]),
        out_shape=[SDS((M, w), dt) for w, dt in row_out] + [SDS(s, F32) for s in par_out],
        compiler_params=_cparams(dimension_semantics=("arbitrary",)),
    )(*[r[0] for r in rows], *[p[0] for p in params], *[c[0] for c in cts], *[adds[k][0] for k in add_keys])
    return res


def _rms_fn(x, g):
    r = lax.rsqrt(jnp.mean(x * x, axis=-1, keepdims=True) + EPS)
    return (x * r * g,)


def _ep_residual_rms(acc, res, g):
    x = acc + res
    return x, _rms_fn(x, g)[0]


def _ep_rms_bwd(dh, x, dres, g):
    r = lax.rsqrt(jnp.mean(x * x, axis=-1, keepdims=True) + EPS)
    t = dh * g
    dx = dres + r * (t - x * (r * r) * jnp.mean(t * x, axis=-1, keepdims=True))
    return dx, jnp.sum(dh * x * r, axis=0, keepdims=True)


def _ep_loss(acc, res, target, *, width):
    e = acc + res - target
    return e * (1.0 / width), jnp.sum(jnp.sum(e * e, axis=1, keepdims=True), axis=0, keepdims=True) * (0.5 / width)


def _seg_mats(width, seg):
    n = width // seg
    p = (_iota((width, n), 0) // seg == _iota((width, n), 1)).astype(F32)
    e = (_iota((n, width), 1) // seg == _iota((n, width), 0)).astype(F32)
    return p, e


def _gate_fn(y, xs, z, dskip, w):
    width = SSM_HEADS * SSM_P
    _, e = _seg_mats(width, SSM_P)
    y = (y + _cdot(dskip, e) * xs) * (z * jax.nn.sigmoid(z))
    g0 = _iota((1, width), 1) < width // SSM_GROUPS
    y2 = y * y
    gw = width // SSM_GROUPS
    ms0 = jnp.sum(jnp.where(g0, y2, 0.0), axis=-1, keepdims=True) * (1.0 / gw)
    ms1 = jnp.sum(jnp.where(g0, 0.0, y2), axis=-1, keepdims=True) * (1.0 / gw)
    r = jnp.where(g0, lax.rsqrt(ms0 + EPS), lax.rsqrt(ms1 + EPS))
    return (y * r * w,)


def _xattn_fn(q0, q1, q2, q3, k0, k1, k2, k3, v0, v1, v2, v3, gq, gk):
    def norm(u, g):
        return u * lax.rsqrt(jnp.mean(u * u, axis=-1, keepdims=True) + EPS) * g
    outs = []
    for q, k, v in ((q0, k0, v0), (q1, k1, v1), (q2, k2, v2), (q3, k3, v3)):
        s = mm_nt(norm(q, gq), norm(k, gk)) * (X_D ** -0.5)
        p = jnp.exp(s - lax.stop_gradient(jnp.max(s, axis=-1, keepdims=True)))
        p = p / jnp.sum(p, axis=-1, keepdims=True)
        outs.append(mm_nn(p, v))
    return (jnp.concatenate(outs, axis=-1),)


CONV_TC = 256


def _shift_down(u, k):
    if k == 0:
        return u
    return jnp.where(_iota(u.shape, 0) >= k, pltpu.roll(u, k, axis=0), 0.0)


def _shift_up(u, k):
    if k == 0:
        return u
    n = u.shape[0]
    return jnp.where(_iota(u.shape, 0) < n - k, pltpu.roll(u, n - k, axis=0), 0.0)


def _conv_pre(u, w_ref, b):
    pre = b + w_ref[3:4, :] * u
    for k in (1, 2, 3):
        pre = pre + w_ref[3 - k:4 - k, :] * _shift_down(u, k)
    return pre


def _conv_fwd(src, c0, width, w, b, *, name):
    S = src.shape[0]
    cb0 = c0 // CONV_TC

    def body(u_ref, w_ref, b_ref, o_ref):
        pre = _conv_pre(u_ref[...], w_ref, b_ref[...])
        o_ref[...] = pre * jax.nn.sigmoid(pre)

    return pl.pallas_call(
        body, name=name, grid=(width // CONV_TC,),
        in_specs=[pl.BlockSpec((S, CONV_TC), lambda j: (0, cb0 + j)), pl.BlockSpec((4, CONV_TC), lambda j: (0, j)),
                  pl.BlockSpec((1, CONV_TC), lambda j: (0, j))],
        out_specs=pl.BlockSpec((S, CONV_TC), lambda j: (0, j)), out_shape=SDS((S, width), F32),
        compiler_params=_cparams(dimension_semantics=("parallel",)),
    )(src, w, b)


def _conv_bwd(src, c0, width, w, b, douts, *, name):
    S = src.shape[0]
    cb0 = c0 // CONV_TC
    nd = len(douts)

    def body(*refs):
        u_ref, w_ref, b_ref = refs[:3]
        d_refs = refs[3:3 + nd]
        du_ref, dw_ref, db_ref = refs[3 + nd:]
        u = u_ref[...]
        pre = _conv_pre(u, w_ref, b_ref[...])
        sg = jax.nn.sigmoid(pre)
        dout = d_refs[0][...]
        for r in d_refs[1:]:
            dout = dout + r[...]
        dpre = dout * (sg * (1.0 + pre * (1.0 - sg)))
        du = w_ref[3:4, :] * dpre
        dw_ref[3:4, :] = jnp.sum(dpre * u, axis=0, keepdims=True)
        for k in (1, 2, 3):
            du = du + w_ref[3 - k:4 - k, :] * _shift_up(dpre, k)
            dw_ref[3 - k:4 - k, :] = jnp.sum(dpre * _shift_down(u, k), axis=0, keepdims=True)
        du_ref[...] = du.astype(du_ref.dtype)
        db_ref[...] = jnp.sum(dpre, axis=0, keepdims=True)

    return pl.pallas_call(
        body, name=name, grid=(width // CONV_TC,),
        in_specs=[pl.BlockSpec((S, CONV_TC), lambda j: (0, cb0 + j)), pl.BlockSpec((4, CONV_TC), lambda j: (0, j)),
                  pl.BlockSpec((1, CONV_TC), lambda j: (0, j))] + [pl.BlockSpec((S, CONV_TC), lambda j: (0, j))] * nd,
        out_specs=[pl.BlockSpec((S, CONV_TC), lambda j: (0, j)), pl.BlockSpec((4, CONV_TC), lambda j: (0, j)),
                   pl.BlockSpec((1, CONV_TC), lambda j: (0, j))],
        out_shape=[SDS((S, width), BF16), SDS((4, width), F32), SDS((1, width), F32)],
        compiler_params=_cparams(dimension_semantics=("parallel",)),
    )(src, w, b, *douts)


def _softplus(x):
    return jnp.maximum(x, 0.0) + jnp.log(1.0 + jnp.exp(-jnp.abs(x)))


def _prefix_sum(x, seg):
    n = x.shape[1]
    pos = _iota(x.shape, 1) % seg
    k = 1
    while k < seg:
        x = x + jnp.where(pos >= k, pltpu.roll(x, k, axis=1), 0.0)
        k *= 2
    return x


def _suffix_sum(x, seg):
    n = x.shape[1]
    pos = _iota(x.shape, 1) % seg
    k = 1
    while k < seg:
        x = x + jnp.where(pos + k < seg, pltpu.roll(x, n - k, axis=1), 0.0)
        k *= 2
    return x


def _dtf_fwd(dtf_t, dt_bias, a_log, f_bias):
    S = dtf_t.shape[1]

    def body(x_ref, db_ref, al_ref, fb_ref, dt_ref, acs_ref, cum_ref):
        dt = _softplus(x_ref[0:16, :] + db_ref[...])
        dt_ref[...] = dt
        acs_ref[...] = _prefix_sum(dt * (-jnp.exp(al_ref[...])), CHUNK)
        cum_ref[...] = _prefix_sum(-_softplus(-(x_ref[16:32, :] + fb_ref[...])), S)

    return pl.pallas_call(body, name="dtf_fwd", out_shape=[SDS((16, S), F32)] * 3, compiler_params=_cparams())(
        dtf_t, dt_bias, a_log, f_bias)


def _dtf_bwd(dtf_t, dt_bias, a_log, f_bias, d_dt, d_acs_a, d_acs_b, d_cum):
    S = dtf_t.shape[1]

    def body(x_ref, db_ref, al_ref, fb_ref, ddt_ref, da1_ref, da2_ref, dc_ref, dx_ref, ddb_ref, dal_ref, dfb_ref):
        xd = x_ref[0:16, :] + db_ref[...]
        dt = _softplus(xd)
        a = -jnp.exp(al_ref[...])
        d_da = _suffix_sum(da1_ref[...] + da2_ref[...], CHUNK)
        d_dt = ddt_ref[...] + d_da * a
        dal_ref[...] = jnp.sum(d_da * dt, axis=1, keepdims=True) * a
        d_xd = d_dt * jax.nn.sigmoid(xd)
        ddb_ref[...] = jnp.sum(d_xd, axis=1, keepdims=True)
        xf = x_ref[16:32, :] + fb_ref[...]
        d_xf = _suffix_sum(dc_ref[...], S) * jax.nn.sigmoid(-xf)
        dfb_ref[...] = jnp.sum(d_xf, axis=1, keepdims=True)
        dx_ref[0:16, :] = d_xd
        dx_ref[16:32, :] = d_xf

    return pl.pallas_call(body, name="dtf_bwd", out_shape=[SDS((32, S), F32)] + [SDS((16, 1), F32)] * 3,
                          compiler_params=_cparams())(dtf_t, dt_bias, a_log, f_bias, d_dt, d_acs_a, d_acs_b, d_cum)


SSM_PAIRS = SSM_HEADS // 2 // SSM_GROUPS


def _ssd_pair(xs, dtc, acol, arow, bm, cm, cbm, h, hp):
    L = CHUNK
    first = _iota((1, LANES), 1) < SSM_P
    i16, s16 = _iota((L, 16), 1), _iota((16, L), 0)
    ha, hb = 2 * hp, 2 * hp + 1

    def selc(blk, hh):
        return jnp.sum(jnp.where(i16 == hh, blk, 0.0), axis=1, keepdims=True)

    def selr(blk, hh):
        return jnp.sum(jnp.where(s16 == hh, blk, 0.0), axis=0, keepdims=True)

    x = xs * jnp.where(first, selc(dtc, ha), selc(dtc, hb))
    ca, cb, ra, rb = selc(acol, ha), selc(acol, hb), selr(arow, ha), selr(arow, hb)
    tri = _iota((L, L), 0) >= _iota((L, L), 1)
    la = jnp.exp(jnp.where(tri, ca - ra, NEG))
    lb = jnp.exp(jnp.where(tri, cb - rb, NEG))
    y = jnp.where(first, mm_nn(cbm * la, x), mm_nn(cbm * lb, x))
    y = y + jnp.where(first, jnp.exp(ca), jnp.exp(cb)) * mm_nn(cm, h)
    last = _iota((1, L), 1) == L - 1
    ala = jnp.sum(jnp.where(last, ra, 0.0), axis=1, keepdims=True)
    alb = jnp.sum(jnp.where(last, rb, 0.0), axis=1, keepdims=True)
    dec = jnp.where(first, jnp.exp(ala - ca), jnp.exp(alb - cb))
    hn = jnp.where(first, jnp.exp(ala), jnp.exp(alb)) * h + mm_tn(bm, x * dec)
    return y, hn


def _ssd_group(*args, grp):
    xs, (dtc, acol, arow, bm, cm), hs = args[:SSM_PAIRS], args[SSM_PAIRS:SSM_PAIRS + 5], args[SSM_PAIRS + 5:]
    cbm = mm_nt(cm, bm)
    res = [_ssd_pair(xs[j], dtc, acol, arow, bm, cm, cbm, hs[j], SSM_PAIRS * grp + j) for j in range(SSM_PAIRS)]
    return tuple(r[0] for r in res) + tuple(r[1] for r in res)


def _ssd_specs(nc, rev):
    L = CHUNK
    cidx = (lambda c: nc - 1 - c) if rev else (lambda c: c)
    return dict(
        xs=pl.BlockSpec((L, SSM_PAIRS * LANES), lambda c, g: (cidx(c), g)),
        col=pl.BlockSpec((L, 16), lambda c, g: (cidx(c), 0)),
        row=pl.BlockSpec((16, L), lambda c, g: (0, cidx(c))),
        b=pl.BlockSpec((L, SSM_N), lambda c, g: (cidx(c), g)),
        c=pl.BlockSpec((L, SSM_N), lambda c, g: (cidx(c), SSM_GROUPS + g)),
        st=pl.BlockSpec((1, SSM_PAIRS, SSM_N, LANES), lambda c, g: (cidx(c), g, 0, 0)),
    )


def _lane_pieces(v):
    return [v[:, LANES * j:LANES * (j + 1)] for j in range(v.shape[1] // LANES)]


def _ssd_fwd(xs, dt_col, acs_col, acs_row, bc):
    S = xs.shape[0]
    nc, nhp = S // CHUNK, SSM_HEADS // 2
    sp = _ssd_specs(nc, False)

    def body(xs_ref, dt_ref, ac_ref, ar_ref, b_ref, c_ref, y_ref, hs_ref, h_scr):
        c, g = pl.program_id(0), pl.program_id(1)

        @pl.when(c == 0)
        def _():
            for j in range(SSM_PAIRS):
                h_scr[SSM_PAIRS * g + j] = jnp.zeros((SSM_N, LANES), F32)

        hs = [h_scr[SSM_PAIRS * g + j] for j in range(SSM_PAIRS)]
        for j in range(SSM_PAIRS):
            hs_ref[0, j] = hs[j]
        res = _ssd_group(*_lane_pieces(xs_ref[...]), dt_ref[...], ac_ref[...], ar_ref[...], b_ref[...], c_ref[...], *hs,
                         grp=g)
        y_ref[...] = jnp.concatenate(res[:SSM_PAIRS], axis=1)
        for j in range(SSM_PAIRS):
            h_scr[SSM_PAIRS * g + j] = res[SSM_PAIRS + j]

    return pl.pallas_call(
        body, name="ssd_fwd", grid=(nc, SSM_GROUPS),
        in_specs=[sp["xs"], sp["col"], sp["col"], sp["row"], sp["b"], sp["c"]],
        out_specs=[sp["xs"], sp["st"]],
        out_shape=[SDS((S, SSM_HEADS * SSM_P), F32), SDS((nc, nhp, SSM_N, LANES), F32)],
        scratch_shapes=[pltpu.VMEM((nhp, SSM_N, LANES), F32)],
        compiler_params=_cparams(dimension_semantics=("arbitrary", "arbitrary")),
    )(xs, dt_col, acs_col, acs_row, bc, bc)


def _ssd_bwd(xs, dt_col, acs_col, acs_row, bc, hs, dy):
    S = xs.shape[0]
    nc, nhp = S // CHUNK, SSM_HEADS // 2
    sp = _ssd_specs(nc, True)

    def body(xs_ref, dt_ref, ac_ref, ar_ref, b_ref, c_ref, hs_ref, dy_ref,
             dxs_ref, ddt_ref, dac_ref, dar_ref, db_ref, dc_ref, dh_scr):
        c, g = pl.program_id(0), pl.program_id(1)

        @pl.when(c == 0)
        def _():
            for j in range(SSM_PAIRS):
                dh_scr[SSM_PAIRS * g + j] = jnp.zeros((SSM_N, LANES), F32)

        _, vjp = jax.vjp(functools.partial(_ssd_group, grp=g), *_lane_pieces(xs_ref[...]), dt_ref[...], ac_ref[...],
                         ar_ref[...], b_ref[...], c_ref[...], *[hs_ref[0, j] for j in range(SSM_PAIRS)])
        grads = vjp(tuple(_lane_pieces(dy_ref[...])) + tuple(dh_scr[SSM_PAIRS * g + j] for j in range(SSM_PAIRS)))
        dxs_ref[...] = jnp.concatenate(grads[:SSM_PAIRS], axis=1)
        ddt, dac, dar, db, dc = grads[SSM_PAIRS:SSM_PAIRS + 5]
        for j in range(SSM_PAIRS):
            dh_scr[SSM_PAIRS * g + j] = grads[SSM_PAIRS + 5 + j]
        db_ref[...] = db
        dc_ref[...] = dc

        @pl.when(g == 0)
        def _():
            ddt_ref[...] = ddt
            dac_ref[...] = dac
            dar_ref[...] = dar

        @pl.when(g > 0)
        def _():
            ddt_ref[...] += ddt
            dac_ref[...] += dac
            dar_ref[...] += dar

    return pl.pallas_call(
        body, name="ssd_bwd", grid=(nc, SSM_GROUPS),
        in_specs=[sp["xs"], sp["col"], sp["col"], sp["row"], sp["b"], sp["c"], sp["st"], sp["xs"]],
        out_specs=[sp["xs"], sp["col"], sp["col"], sp["row"], sp["b"], sp["b"]],
        out_shape=[SDS((S, SSM_HEADS * SSM_P), F32), SDS((S, 16), F32), SDS((S, 16), F32), SDS((16, S), F32),
                   SDS((S, SSM_GROUPS * SSM_N), F32), SDS((S, SSM_GROUPS * SSM_N), F32)],
        scratch_shapes=[pltpu.VMEM((nhp, SSM_N, LANES), F32)],
        compiler_params=_cparams(dimension_semantics=("arbitrary", "arbitrary")),
    )(xs, dt_col, acs_col, acs_row, bc, bc, hs, dy)


ATT_T = 2048
ATT_U = 512


def _pick_col(blk, h):
    return jnp.sum(jnp.where(_iota(blk.shape, 1) == h, blk, 0.0), axis=1, keepdims=True)


def _pick_row(blk, h):
    return jnp.sum(jnp.where(_iota(blk.shape, 0) == h, blk, 0.0), axis=0, keepdims=True)


def _pair_norm(x, g2, first):
    x2 = x * x
    sa = jnp.sum(jnp.where(first, x2, 0.0), axis=1, keepdims=True)
    sb = jnp.sum(jnp.where(first, 0.0, x2), axis=1, keepdims=True)
    r = jnp.where(first, lax.rsqrt(sa * (1.0 / ATT_D) + EPS), lax.rsqrt(sb * (1.0 / ATT_D) + EPS))
    return x * r * g2, r


def _pair_norm_bwd(dxn, x, r, g2, first):
    t = dxn * g2
    tx = t * x
    ma = jnp.sum(jnp.where(first, tx, 0.0), axis=1, keepdims=True)
    mb = jnp.sum(jnp.where(first, 0.0, tx), axis=1, keepdims=True)
    dx = r * (t - x * (r * r) * (jnp.where(first, ma, mb) * (1.0 / ATT_D)))
    return dx, jnp.sum(dxn * x * r, axis=0, keepdims=True)


def _fox_fwd(src, q_c0, k_c0, v_c0, gq2, gk2, cum_col, cum_row3):
    S = src.shape[0]
    T = ATT_T
    nq, nhp = S // T, ATT_HEADS // 2
    qb0, kb0, vb0 = q_c0 // LANES, k_c0 // LANES, v_c0 // LANES
    scale = ATT_D ** -0.5

    def body(q_ref, kraw_ref, v_ref, gq_ref, gk_ref, cc_ref, cr_ref, o_ref, l_ref, k_ref):
        hp, i = pl.program_id(0), pl.program_id(1)
        first = _iota((1, LANES), 1) < ATT_D

        @pl.when(i == 0)
        def _():
            k_ref[...] = _pair_norm(kraw_ref[...], gk_ref[...], first)[0].astype(BF16)

        H = ATT_U

        def attend(tile, half):
            rows = pl.ds(half * H, H)
            row0 = tile * T + half * H
            klen = row0 + H
            q = (_pair_norm(q_ref[rows, :], gq_ref[...], first)[0] * scale).astype(BF16)
            zero = jnp.zeros_like(q)
            cc = cc_ref[rows, :]
            k = k_ref[0:klen, :]
            v = v_ref[0:klen, :].astype(BF16)
            allowed = _iota((H, klen), 0) + row0 >= _iota((H, klen), 1)
            outs, lses = [], []
            for hh in range(2):
                sel = first if hh == 0 else jnp.logical_not(first)
                ck = jnp.concatenate([_pick_row(cr_ref[j], 2 * hp + hh) for j in range(tile + 1)], axis=1)[:, :klen]
                s = _bdot(jnp.where(sel, q, zero), k, _NT) + (_pick_col(cc, 2 * hp + hh) - ck)
                s = jnp.where(allowed, s, NEG)
                m = jnp.max(s, axis=1, keepdims=True)
                p = jnp.exp(s - m)
                l = jnp.sum(p, axis=1, keepdims=True)
                outs.append(_bdot(p, v, _NN) / l)
                lses.append(m + jnp.log(l))
            o_ref[rows, :] = jnp.where(first, outs[0], outs[1]).astype(o_ref.dtype)
            l_ref[rows, :] = jnp.where(first, lses[0], lses[1])

        for tile in range(nq):
            @pl.when(i == tile)
            def _(tile=tile):
                for half in range(T // H):
                    attend(tile, half)

    gain = pl.BlockSpec((1, LANES), lambda hp, i: (0, 0))
    return pl.pallas_call(
        body, name="fox_fwd", grid=(nhp, nq),
        in_specs=[pl.BlockSpec((T, LANES), lambda hp, i: (i, qb0 + hp)), pl.BlockSpec((S, LANES), lambda hp, i: (0, kb0 + hp)),
                  pl.BlockSpec((S, LANES), lambda hp, i: (0, vb0 + hp)), gain, gain,
                  pl.BlockSpec((T, 16), lambda hp, i: (i, 0)), pl.BlockSpec((nq, 16, T), lambda hp, i: (0, 0, 0))],
        out_specs=[pl.BlockSpec((T, LANES), lambda hp, i: (i, hp))] * 2,
        out_shape=[SDS((S, ATT_HEADS * ATT_D), BF16), SDS((S, ATT_HEADS * ATT_D), F32)],
        scratch_shapes=[pltpu.VMEM((S, LANES), BF16)],
        compiler_params=_cparams(dimension_semantics=("arbitrary", "arbitrary")),
    )(src, src, src, gq2, gk2, cum_col, cum_row3)


def _fox_bwd(src, q_c0, k_c0, v_c0, gq2, gk2, cum_col, cum_row3, lse, dsrc, d_c0):
    S = src.shape[0]
    T = ATT_T
    nq, nhp = S // T, ATT_HEADS // 2
    qb0, kb0, vb0, db0 = q_c0 // LANES, k_c0 // LANES, v_c0 // LANES, d_c0 // LANES
    scale = ATT_D ** -0.5

    def body(q_ref, kraw_ref, v_ref, gq_ref, gk_ref, cc_ref, cr_ref, l_ref, do_ref,
             dq_ref, dk_ref, dv_ref, dc_ref, dg_ref, k_ref, dk_acc, dv_acc):
        hp, i = pl.program_id(0), pl.program_id(1)
        first = _iota((1, LANES), 1) < ATT_D

        @pl.when(i == 0)
        def _():
            k_ref[...] = _pair_norm(kraw_ref[...], gk_ref[...], first)[0].astype(BF16)
            dk_acc[...] = jnp.zeros_like(dk_acc)
            dv_acc[...] = jnp.zeros_like(dv_acc)
            dc_ref[...] = jnp.zeros_like(dc_ref)
            dg_ref[...] = jnp.zeros_like(dg_ref)

        H = ATT_U

        def backprop(tile, half):
            rows = pl.ds(half * H, H)
            row0 = tile * T + half * H
            klen = row0 + H
            q_raw = q_ref[rows, :]
            qn, rq = _pair_norm(q_raw, gq_ref[...], first)
            q = (qn * scale).astype(BF16)
            zq = jnp.zeros_like(q)
            dob = do_ref[rows, :].astype(BF16)
            zd = jnp.zeros_like(dob)
            lse_blk, cc = l_ref[rows, :], cc_ref[rows, :]
            k = k_ref[0:klen, :]
            zk = jnp.zeros_like(k)
            allowed = _iota((H, klen), 0) + row0 >= _iota((H, klen), 1)
            dq = jnp.zeros((H, LANES), F32)
            for hh in range(2):
                sel = first if hh == 0 else jnp.logical_not(first)
                qh, doh = jnp.where(sel, q, zq), jnp.where(sel, dob, zd)
                bias_q = _pick_col(cc, 2 * hp + hh) - jnp.max(jnp.where(sel, lse_blk, NEG), axis=1, keepdims=True)
                ck = jnp.concatenate([_pick_row(cr_ref[j], 2 * hp + hh) for j in range(tile + 1)], axis=1)[:, :klen]
                p = jnp.exp(jnp.where(allowed, _bdot(qh, k, _NT) + (bias_q - ck), NEG))
                dp = _bdot(doh, v_ref[0:klen, :], _NT)
                ds = p * (dp - jnp.sum(p * dp, axis=1, keepdims=True))
                dv_acc[0:klen, :] += _bdot(p, doh, _TN)
                dk_acc[0:klen, :] += _bdot(ds, qh, _TN)
                dcs = jnp.sum(ds, axis=0, keepdims=True)
                for j in range(tile + 1):
                    n = min(T, klen - j * T)
                    dc_ref[0, j, hh:hh + 1, 0:n] -= dcs[:, j * T:j * T + n]
                dq = dq + _bdot(ds, jnp.where(sel, k, zk), _NN)
            dq_raw, dgq = _pair_norm_bwd(dq * scale, q_raw, rq, gq_ref[...], first)
            dq_ref[rows, :] = dq_raw.astype(dq_ref.dtype)
            dg_ref[0, 0:1, :] += dgq

        for tile in range(nq):
            @pl.when(i == tile)
            def _(tile=tile):
                for half in range(T // H):
                    backprop(tile, half)

        @pl.when(i == nq - 1)
        def _():
            k_raw = kraw_ref[...]
            rk = _pair_norm(k_raw, gk_ref[...], first)[1]
            dk_raw, dgk = _pair_norm_bwd(dk_acc[...], k_raw, rk, gk_ref[...], first)
            dk_ref[...] = dk_raw.astype(dk_ref.dtype)
            dv_ref[...] = dv_acc[...].astype(dv_ref.dtype)
            dg_ref[0, 1:2, :] = dgk

    gain = pl.BlockSpec((1, LANES), lambda hp, i: (0, 0))
    band = SDS((S, ATT_HEADS * ATT_D), BF16)
    return pl.pallas_call(
        body, name="fox_bwd", grid=(nhp, nq),
        in_specs=[pl.BlockSpec((T, LANES), lambda hp, i: (i, qb0 + hp)), pl.BlockSpec((S, LANES), lambda hp, i: (0, kb0 + hp)),
                  pl.BlockSpec((S, LANES), lambda hp, i: (0, vb0 + hp)), gain, gain,
                  pl.BlockSpec((T, 16), lambda hp, i: (i, 0)), pl.BlockSpec((nq, 16, T), lambda hp, i: (0, 0, 0)),
                  pl.BlockSpec((T, LANES), lambda hp, i: (i, hp)), pl.BlockSpec((T, LANES), lambda hp, i: (i, db0 + hp))],
        out_specs=[pl.BlockSpec((T, LANES), lambda hp, i: (i, hp)), pl.BlockSpec((S, LANES), lambda hp, i: (0, hp)),
                   pl.BlockSpec((S, LANES), lambda hp, i: (0, hp)), pl.BlockSpec((1, nq, 8, T), lambda hp, i: (hp, 0, 0, 0)),
                   pl.BlockSpec((1, 8, LANES), lambda hp, i: (hp, 0, 0))],
        out_shape=[band, band, band, SDS((nhp, nq, 8, T), F32), SDS((nhp, 8, LANES), F32)],
        scratch_shapes=[pltpu.VMEM((S, LANES), BF16), pltpu.VMEM((S, LANES), F32), pltpu.VMEM((S, LANES), F32)],
        compiler_params=_cparams(dimension_semantics=("arbitrary", "arbitrary")),
    )(src, src, src, gq2, gk2, cum_col, cum_row3, lse, dsrc)


def _fold_gains(dg):
    def body(d_ref, o_ref):
        t = d_ref[0]
        for h in range(1, dg.shape[0]):
            t = t + d_ref[h]
        o_ref[...] = t + pltpu.roll(t, ATT_D, axis=1)

    return pl.pallas_call(body, name="fold_gains", out_shape=SDS(dg.shape[1:], F32), compiler_params=_cparams())(dg)


def _adamw_math(w, g, m, v):
    m = ADAM_B1 * m + (1.0 - ADAM_B1) * g
    v = ADAM_B2 * v + (1.0 - ADAM_B2) * jnp.square(g)
    m_hat = m / (1.0 - ADAM_B1 ** ADAM_STEP)
    v_hat = v / (1.0 - ADAM_B2 ** ADAM_STEP)
    delta = -ADAM_LR * (m_hat / (jnp.sqrt(v_hat) + ADAM_EPS) + ADAM_WD * w)
    return delta, m, v


def _reduce_adamw(parts, w, m, v, *, tr, name, tc=None):
    R, C = w.shape
    tr, tc = min(tr, R), tc or C
    nparts = parts.shape[0]

    def body(p_ref, w_ref, m_ref, v_ref, g_ref, d_ref, nm_ref, nv_ref):
        g = p_ref[0].astype(F32)
        for s in range(1, nparts):
            g = g + p_ref[s].astype(F32)
        g_ref[...] = g
        d_ref[...], nm_ref[...], nv_ref[...] = _adamw_math(w_ref[...], g, m_ref[...], v_ref[...])

    blk = pl.BlockSpec((tr, tc), lambda i, j: (i, j))
    return pl.pallas_call(
        body, name=name, grid=(R // tr, C // tc),
        in_specs=[pl.BlockSpec((nparts, tr, tc), lambda i, j: (0, i, j)), blk, blk, blk], out_specs=[blk] * 4,
        out_shape=[SDS((R, C), F32)] * 4, compiler_params=_cparams(dimension_semantics=("parallel", "parallel")),
    )(parts, w, m, v)


def _adamw(w, g, m, v, *, name):
    def body(w_ref, g_ref, m_ref, v_ref, d_ref, nm_ref, nv_ref):
        d_ref[...], nm_ref[...], nv_ref[...] = _adamw_math(w_ref[...], g_ref[...], m_ref[...], v_ref[...])

    return pl.pallas_call(body, name=name, out_shape=[SDS(w.shape, F32)] * 3, compiler_params=_cparams())(w, g, m, v)


def _peers():
    x, y, c = lax.axis_index("x"), lax.axis_index("y"), lax.axis_index("c")
    out = []
    for k in range(1, N_DEV):
        px, py, pc = x ^ ((k >> 2) & 1), y ^ ((k >> 1) & 1), c ^ (k & 1)
        out.append(((px, py, pc), 4 * px + 2 * py + pc))
    return 4 * x + 2 * y + c, out


_HBM = pl.BlockSpec(memory_space=pltpu.HBM)
_SEM = pl.BlockSpec(memory_space=pltpu.SEMAPHORE)
_DATAFLOW = pltpu.SideEffectType.DATAFLOW_SIDE_EFFECTING


NEAR = (1, 2, 4, 6)


def _plan_peers(scatter, ks=tuple(range(1, N_DEV))):
    return lambda me, peers: [(peers[k - 1][0], peers[k - 1][1] if scatter else None, me, k - 1) for k in ks]


def _plan_relay(me, peers):
    return [(peers[0][0], peers[k - 1][1], peers[k - 1][1], j) for j, k in enumerate((2, 4, 6))]


def _plan_pair(me, peers):
    return [(peers[0][0], peers[k - 1][1], j, j) for j, k in enumerate((1, 3, 5, 7))]


def _plan_chips(me, peers):
    return [(peers[k - 1][0], k // 2, k // 2, k // 2) for k in (2, 4, 6)]


def _copy(src, dst, c, send_sems, recv_sems):
    dev, s_slot, d_slot, i = c
    return pltpu.make_async_remote_copy(
        src_ref=src if s_slot is None else src.at[s_slot], dst_ref=dst.at[d_slot], send_sem=send_sems.at[i],
        recv_sem=recv_sems.at[i], device_id=dev, device_id_type=MESH)


def _copies_start(items, *, name):
    n = len(items)
    bufs = [it[0] for it in items] + [it[1] for it in items if it[1] is not None]
    nb = len(bufs)

    def body(*refs):
        srcs, extra, sems, token = refs[:n], iter(refs[n:nb]), refs[nb:nb + 2 * n], refs[-1]
        me, peers = _peers()
        for a, (_, land, plan) in enumerate(items):
            dst = srcs[a] if land is None else next(extra)
            for c in plan(me, peers):
                _copy(srcs[a], dst, c, sems[2 * a], sems[2 * a + 1]).start()
        token[...] = jnp.zeros_like(token)

    res = pl.pallas_call(
        body, name=name,
        out_shape=([pltpu.SemaphoreType.DMA((N_DEV - 1,))] * (2 * n) + [pltpu.HBM(b.shape, b.dtype) for b in bufs]
                   + [SDS((8, LANES), F32)]),
        in_specs=[_HBM] * nb, out_specs=[_SEM] * (2 * n) + [_HBM] * nb + [pl.BlockSpec(memory_space=pltpu.VMEM)],
        input_output_aliases={i: 2 * n + i for i in range(nb)},
        compiler_params=pltpu.CompilerParams(has_side_effects=_DATAFLOW),
    )(*[pltpu.with_memory_space_constraint(b, pltpu.HBM) for b in bufs])
    sems, thru, token = res[:2 * n], list(res[2 * n:2 * n + nb]), res[-1]
    extra = iter(thru[n:])
    return [(thru[a], None if it[1] is None else next(extra), sems[2 * a], sems[2 * a + 1], it[2])
            for a, it in enumerate(items)], token


def _copies_wait(handles, after, *, name):
    n = len(handles)
    after = list(after) if isinstance(after, (list, tuple)) else [after]
    bufs = [h[0] for h in handles] + [h[1] for h in handles if h[1] is not None]
    nb = len(bufs)

    def body(*refs):
        srcs, extra, sems = refs[:n], iter(refs[n:nb]), refs[nb:nb + 2 * n]
        me, peers = _peers()
        for a, h in enumerate(handles):
            dst = srcs[a] if h[1] is None else next(extra)
            for c in h[4](me, peers):
                cp = _copy(srcs[a], dst, c, sems[2 * a], sems[2 * a + 1])
                cp.wait_send()
                cp.wait_recv()

    flat_sems = [s for h in handles for s in (h[2], h[3])]
    res = pl.pallas_call(
        body, name=name, out_shape=[pltpu.HBM(b.shape, b.dtype) for b in bufs],
        in_specs=[_HBM] * nb + [_SEM] * (2 * n) + [pl.BlockSpec(memory_space=pl.ANY)] * len(after), out_specs=[_HBM] * nb,
        input_output_aliases={i: i for i in range(nb)},
        compiler_params=pltpu.CompilerParams(has_side_effects=_DATAFLOW),
    )(*bufs, *flat_sems, *after)
    extra = iter(res[n:])
    return [(res[a], res[a] if h[1] is None else next(extra)) for a, h in enumerate(handles)]


def _exchange_start(arrays, *, scatter, name, near=()):
    items = []
    for a, arr in enumerate(arrays):
        land = lax.empty(arr.shape if scatter else (N_DEV,) + arr.shape, arr.dtype)
        items.append((arr, land, _plan_peers(scatter, NEAR) if a in near else _plan_peers(scatter)))
    return _copies_start(items, name=name)


MOVE_ROWS, MOVE_SLOTS = 512, 3


def _move_rows(src, moves, rows, *, name):
    C = src.shape[1]
    covered = max(dst + n for _, n, dst in moves)
    tail = rows - covered
    assert sum(n for _, n, _ in moves) == covered
    chunks = [(lo + o, min(MOVE_ROWS, n - o), dst + o) for lo, n, dst in moves for o in range(0, n, MOVE_ROWS)]
    nch = len(chunks)

    def body(src_ref, o_ref, buf, sin, sout, *zero):
        def fetch(i):
            lo, n, _ = chunks[i]
            return pltpu.make_async_copy(src_ref.at[pl.ds(lo, n)], buf.at[i % MOVE_SLOTS, pl.ds(0, n)], sin.at[i % MOVE_SLOTS])

        def store(i):
            _, n, dst = chunks[i]
            return pltpu.make_async_copy(buf.at[i % MOVE_SLOTS, pl.ds(0, n)], o_ref.at[pl.ds(dst, n)], sout.at[i % MOVE_SLOTS])

        if tail:
            zero[0][...] = jnp.zeros_like(zero[0])
            fill = pltpu.make_async_copy(zero[0], o_ref.at[pl.ds(covered, tail)], zero[1])
            fill.start()
        for i in range(nch):
            if i >= MOVE_SLOTS:
                store(i - MOVE_SLOTS).wait()
            fetch(i).start()
            if i >= 1:
                fetch(i - 1).wait()
                store(i - 1).start()
        fetch(nch - 1).wait()
        store(nch - 1).start()
        for i in range(max(0, nch - MOVE_SLOTS), nch):
            store(i).wait()
        if tail:
            fill.wait()

    anyspec = pl.BlockSpec(memory_space=pl.ANY)
    dma = pltpu.SemaphoreType.DMA
    return pl.pallas_call(
        body, name=name, in_specs=[anyspec], out_specs=anyspec, out_shape=SDS((rows, C), src.dtype),
        scratch_shapes=([pltpu.VMEM((MOVE_SLOTS, MOVE_ROWS, C), src.dtype), dma((MOVE_SLOTS,)), dma((MOVE_SLOTS,))]
                        + ([pltpu.VMEM((tail, C), src.dtype), dma] if tail else [])),
        compiler_params=_cparams())(src)


def _pair_sum(a, b, *, name):
    n, R, C = a.shape
    tc = 256

    def body(a_ref, b_ref, o_ref):
        o_ref[...] = (a_ref[...].astype(F32) + b_ref[...].astype(F32)).astype(o_ref.dtype)

    blk = pl.BlockSpec((1, R, tc), lambda i, j: (i, 0, j))
    return pl.pallas_call(body, name=name, grid=(n, C // tc), in_specs=[blk, blk], out_specs=blk,
                          out_shape=SDS(a.shape, a.dtype), compiler_params=_cparams(dimension_semantics=("parallel", "parallel")))(a, b)


def _own_slot(landed, own, me):
    return lax.dynamic_update_slice(landed, own[None], (me,) + (0,) * own.ndim)


SMALL = (("g_mix", 1024), ("conv_w", 6144), ("conv_b", 1536), ("dt_bias", 16), ("a_log", 16), ("d_skip", 16),
         ("ssm_norm_w", 1024), ("g_q", 64), ("g_k", 64), ("f_bias", 16), ("g_xattn", 1024), ("g_mem", 1024),
         ("xg_q", 256), ("xg_k", 256), ("g_mlp", 1024), ("loss", 1))
NOT_PARAMS = ("conv_w", "loss")
SLAB_ROWS = 112
BIG = ("w_in", "w_out", "xq_w", "xkv_w", "xo_w", "w_up", "w_down")
WEIGHTS = ("g_mix", "w_in", "conv_w", "conv_b", "dt_bias", "a_log", "d_skip", "ssm_norm_w", "g_q", "g_k", "f_bias", "w_out",
           "g_xattn", "g_mem", "xq_w", "xkv_w", "xg_q", "xg_k", "xo_w", "g_mlp", "w_up", "w_down")
O_Z, O_XS, O_B, O_C, O_DT, O_Q, O_K, O_V, O_F, O_END = 0, 1024, 2048, 2304, 2560, 2576, 3600, 4624, 5648, 5664
IN_ROW_MOVES = ((O_Z, O_B - O_Z, C_Z), (O_Q, O_F - O_Q, C_Q), (O_B, O_Q - O_B, C_B), (O_F, O_END - O_F, C_DTF + 16))


def _pack_small(vals):
    rows = []
    for name, size in SMALL:
        flat = vals[name].reshape(-1).astype(F32)
        pad = -size % LANES
        rows.append(jnp.pad(flat, (0, pad)).reshape(-1, LANES))
    slab = jnp.concatenate(rows, axis=0)
    return jnp.pad(slab, ((0, SLAB_ROWS - slab.shape[0]), (0, 0)))


def _unpack_small(slab):
    out, r = {}, 0
    for name, size in SMALL:
        nr = -(-size // LANES)
        out[name] = slab[r:r + nr].reshape(-1)[:size]
        r += nr
    return out


def _step(p, m, v, x, mem, target):
    S = x.shape[0]
    TM = 256
    me = 4 * lax.axis_index("x") + 2 * lax.axis_index("y") + lax.axis_index("c")

    def rms(u, g, name):
        return _rw_fwd(_rms_fn, [_whole(u)], [_whole(g)], [(D_MODEL, BF16)], tm=TM, name=name)[0]

    def pin(param, token):
        return param + token[0:1, 0:1]

    def landed_with_own(pairs, scatter):
        out = []
        for src, land in pairs:
            own = lax.dynamic_index_in_dim(src, me, 0, keepdims=False) if scatter else src
            out.append(_own_slot(land, own, me))
        return out

    w_in_own, m_in_own, v_in_own = p["w_in"].T, m["w_in"].T, v["w_in"].T
    ag, ag_token = _exchange_start([w_in_own.astype(BF16), p["conv_w"]] + [p[n].astype(BF16) for n in BIG[1:]],
                                   scatter=False, name="allgather_start", near=(0, 2, 3, 4, 5, 6, 7))
    h1 = rms(x, pin(p["g_mix"], ag_token), "rms_mix")
    stand_in = {"conv_w": jnp.zeros((4, 1536), F32), "loss": jnp.zeros((1,), F32)}
    slabs = [_pack_small({**d, **stand_in}) for d in (p, m, v)]
    (win_src, win_land), convw_pair = _copies_wait(ag[:2], [h1, w_in_own, m_in_own, v_in_own] + slabs,
                                                   name="allgather_wait_in")
    relay, token = _copies_start([(win_land, None, _plan_relay)], name="allgather_relay_start")
    win_land = _copies_wait(relay, token, name="allgather_relay_wait")[0][1]
    win_g, convw_g = landed_with_own([(win_src, win_land), convw_pair], False)
    w_in_o = win_g.reshape(O_END, D_MODEL)
    w_in_t = _move_rows(w_in_o, IN_ROW_MOVES, P_COLS, name="w_in_rows")
    conv_w = convw_g.transpose(1, 0, 2).reshape(4, 1536)
    cw_xs, cw_bc = conv_w[:, :1024], conv_w[:, 1024:]
    cb_xs, cb_bc = p["conv_b"][:, :1024], p["conv_b"][:, 1024:]
    dt_bias, a_log, f_bias = p["dt_bias"].reshape(16, 1), p["a_log"].reshape(16, 1), p["f_bias"].reshape(16, 1)

    proj = _matmul(h1, w_in_t, mode="nt", tm=1024, tn=640, tk=1024, name="mm_in")
    xs_c = _conv_fwd(proj, C_XS, 1024, cw_xs, cb_xs, name="conv_xs")
    bc_c = _conv_fwd(proj, C_B, 512, cw_bc, cb_bc, name="conv_bc")
    dtf_t = proj[:, C_DTF:C_DTF + 32].T
    dt_t, acs_t, cum_t = _dtf_fwd(dtf_t, dt_bias, a_log, f_bias)
    dt_col, acs_col, cum_col = dt_t.T, acs_t.T, cum_t.T
    cum_row3 = cum_t.reshape(16, S // ATT_T, ATT_T).transpose(1, 0, 2)
    y_ssd, hs = _ssd_fwd(xs_c, dt_col, acs_col, acs_t, bc_c)
    gate_rows = [_whole(y_ssd), _whole(xs_c), (proj, C_Z, 1024)]
    gate_pars = [_whole(p["d_skip"]), _whole(p["ssm_norm_w"])]
    y_ssm = _rw_fwd(_gate_fn, gate_rows, gate_pars, [(1024, BF16)], tm=TM, name="gate")[0]
    gq2, gk2 = jnp.tile(p["g_q"], (1, 2)), jnp.tile(p["g_k"], (1, 2))
    o, lse = _fox_fwd(proj, C_Q, C_K, C_V, gq2, gk2, cum_col, cum_row3)
    mixed = jnp.concatenate([y_ssm, o], axis=1)
    arrived = _copies_wait(ag[2:], mixed, name="allgather_wait_rest")
    relay, token = _copies_start([(land, None, _plan_relay) for _, land in arrived], name="allgather_relay_rest_start")
    wout_g, = landed_with_own([(arrived[0][0], _copies_wait(relay[:1], token, name="allgather_relay_out_wait")[0][1])], False)
    w_out = wout_g.reshape(2 * D_MODEL, D_MODEL)
    x1, h2 = _matmul(mixed, w_out, mode="nn", tm=1024, tn=1024, tk=2048, name="mm_out", extras=(x,),
                     row_params=(p["g_xattn"],), epilogue=_ep_residual_rms, out_dtypes=[F32, BF16])
    relayed = _copies_wait(relay[1:], x1, name="allgather_relay_rest_wait")
    xq_g, xkv_w, xo_g, w_up, wdown_g = landed_with_own(
        [(src, land) for (src, _), (_, land) in zip(arrived[1:], relayed, strict=True)], False)
    xq_w = xq_g.reshape(D_MODEL, D_MODEL)
    xo_w = xo_g.reshape(D_MODEL, D_MODEL)
    w_down = wdown_g.reshape(4 * D_MODEL, D_MODEL)

    mem_n = rms(mem, p["g_mem"], "rms_mem")
    q2 = _matmul(h2, xq_w, mode="nn", tm=1024, tn=512, tk=1024, name="mm_xq")
    kv = _matmul(mem_n, xkv_w, mode="nn", b_shards=True, tm=256, tn=256, tk=1024, name="mm_xkv")
    xa_rows = [(q2, X_D * h, X_D) for h in range(X_HEADS)]
    xa_pars = ([(kv, X_D * h, X_D) for h in range(X_HEADS)] + [(kv, D_MODEL + X_D * h, X_D) for h in range(X_HEADS)]
               + [_whole(p["xg_q"]), _whole(p["xg_k"])])
    o2 = _rw_fwd(_xattn_fn, xa_rows, xa_pars, [(D_MODEL, BF16)], tm=2 * TM, name="xattn")[0]
    x2, h3 = _matmul(o2, xo_w, mode="nn", tm=1024, tn=1024, tk=1024, name="mm_xo", extras=(x1,),
                     row_params=(p["g_mlp"],), epilogue=_ep_residual_rms, out_dtypes=[F32, BF16])

    a, usq = _matmul(h3, w_up, mode="nn", b_shards=True, tm=2048, tn=512, tk=1024, name="mm_up", out_dtypes=[F32, BF16],
                     epilogue=lambda acc: (acc, jnp.square(jax.nn.relu(acc))))
    dy, loss_part = _matmul(usq, w_down, mode="nn", tm=1024, tn=512, tk=2048, name="mm_down", extras=(x2, target),
                            epilogue=functools.partial(_ep_loss, width=D_MODEL), sums=[(1, 1)])

    def row_shards(a):
        r, c = a.shape
        return a.reshape(N_DEV, r // N_DEV, c)

    g = {"loss": loss_part}
    g["w_down"] = _matmul(usq, dy, mode="tn", out_dtype=GRAD_WIRE, tm=1024, tn=1024, tk=1024, name="mm_d_wdown")
    da = _matmul(dy, w_down, mode="nt", tm=1024, tn=1024, tk=1024, name="mm_d_usq", out_dtype=BF16, extras=(a,),
                 epilogue=lambda acc, av: (2.0 * jax.nn.relu(av) * acc,))
    g["w_up"] = _matmul(h3, da, mode="tn", out_shards=True, out_dtype=GRAD_WIRE, tm=1024, tn=512, tk=1024, name="mm_d_wup")
    sent_mlp, token = _exchange_start([row_shards(g["w_down"]), g["w_up"]], scatter=True,
                                      name="grads_start_mlp")
    dx2, g["g_mlp"] = _matmul(da, w_up, mode="nt", b_shards=True, tm=1024, tn=1024, tk=512, name="mm_d_h3",
                              extras=(x2, dy), row_params=(pin(p["g_mlp"], token),), epilogue=_ep_rms_bwd,
                              sums=[(1, D_MODEL)])

    g["xo_w"] = _matmul(o2, dx2, mode="tn", out_dtype=GRAD_WIRE, tm=1024, tn=1024, tk=1024, name="mm_d_wxo")
    do2 = _matmul(dx2, xo_w, mode="nt", tm=1024, tn=512, tk=1024, name="mm_d_o2")
    dq2, dkv, g["xg_q"], g["xg_k"] = _rw_bwd(_xattn_fn, xa_rows, xa_pars, [_whole(do2)], tm=2 * TM, name="xattn_bwd",
                                             row_grads=[BF16] * X_HEADS, join_rows=True, join_params=2 * X_HEADS)
    g["xq_w"] = _matmul(h2, dq2, mode="tn", out_dtype=GRAD_WIRE, tm=1024, tn=1024, tk=1024, name="mm_d_wxq")
    dx1, g["g_xattn"] = _matmul(dq2, xq_w, mode="nt", tm=1024, tn=1024, tk=1024, name="mm_d_h2", extras=(x1, dx2),
                                row_params=(p["g_xattn"],), epilogue=_ep_rms_bwd, sums=[(1, D_MODEL)])
    g["xkv_w"] = _matmul(mem_n, dkv, mode="tn", out_shards=True, out_dtype=GRAD_WIRE, tm=1024, tn=256, tk=256,
                         name="mm_d_wxkv")
    dmem_n = _matmul(dkv, xkv_w, mode="nt", b_shards=True, tm=256, tn=1024, tk=256, name="mm_d_memn")
    g["g_mem"] = _rw_bwd(_rms_fn, [_whole(mem)], [_whole(p["g_mem"])], [_whole(dmem_n)], tm=TM, name="rms_mem_bwd",
                         row_grads=[None])[0]

    g["w_out"] = _matmul(mixed, dx1, mode="tn", out_dtype=GRAD_WIRE, tm=1024, tn=1024, tk=1024, name="mm_d_wout")
    sent_mid, token = _exchange_start(
        [row_shards(g["w_out"]), row_shards(g["xq_w"]), g["xkv_w"], row_shards(g["xo_w"])], scatter=True,
        name="grads_start_mid")
    dmixed = _matmul(dx1, w_out, mode="nt", tm=1024, tn=1024, tk=1024, name="mm_d_mixed")
    dq, dk, dv, dcum4, dgain = _fox_bwd(proj, C_Q, C_K, C_V, pin(gq2, token), gk2, cum_col, cum_row3, lse, dmixed, 1024)
    gains = _fold_gains(dgain)
    g["g_q"], g["g_k"] = gains[0:1, :ATT_D], gains[1:2, :ATT_D]
    dy_ssd, dxs_g, dz, g["d_skip"], g["ssm_norm_w"] = _rw_bwd(
        _gate_fn, gate_rows, gate_pars, [(dmixed, 0, 1024)], tm=TM, name="gate_bwd", row_grads=[F32, F32, BF16])
    dxs_s, ddt_col, dacs_col, dacs_row, d_b, d_c = _ssd_bwd(xs_c, dt_col, acs_col, acs_t, bc_c, hs, dy_ssd)
    dcum_t = dcum4[:, :, 0:2, :].transpose(0, 2, 1, 3).reshape(16, S)
    ddtf_t, ddtb, dalog, dfb = _dtf_bwd(dtf_t, dt_bias, a_log, f_bias, ddt_col.T, dacs_col.T, dacs_row, dcum_t)
    g["dt_bias"], g["a_log"], g["f_bias"] = ddtb, dalog, dfb
    dxs_raw, dcw_xs, dcb_xs = _conv_bwd(proj, C_XS, 1024, cw_xs, cb_xs, [dxs_s, dxs_g], name="conv_xs_bwd")
    dbc_raw, dcw_bc, dcb_bc = _conv_bwd(proj, C_B, 512, cw_bc, cb_bc, [jnp.concatenate([d_b, d_c], axis=1)],
                                        name="conv_bc_bwd")
    g["conv_w"] = jnp.concatenate([dcw_xs, dcw_bc], axis=1)
    g["conv_b"] = jnp.concatenate([dcb_xs, dcb_bc], axis=1)
    ddtf = jnp.pad(ddtf_t.T.astype(BF16), ((0, 0), (0, P_COLS - C_DTF - 32)))
    dproj = jnp.concatenate([dz, dxs_raw, dq, dk, dv, dbc_raw, ddtf], axis=1)
    dw_in_p = _matmul(dproj, h1, mode="tn", out_dtype=GRAD_WIRE, tm=640, tn=1024, tk=1024, name="mm_d_win")
    g["w_in"] = _move_rows(dw_in_p, [(dst, n, lo) for lo, n, dst in IN_ROW_MOVES], O_END, name="d_w_in_rows")
    half = N_DEV // 2
    send_in = row_shards(g["w_in"])
    pair, token = _copies_start([(send_in, lax.empty((half,) + send_in.shape[1:], send_in.dtype), _plan_pair)],
                                name="grads_in_pair_start")
    send_in, from_sibling = _copies_wait(pair, token, name="grads_in_pair_wait")[0]
    mine = jnp.stack([lax.dynamic_index_in_dim(send_in, me ^ (2 * j), 0, keepdims=False) for j in range(half)])
    chip_sums = _pair_sum(mine, from_sibling, name="grads_in_pair_sum")
    sent_in, token = _copies_start([(chip_sums, lax.empty(chip_sums.shape, chip_sums.dtype), _plan_chips)],
                                   name="grads_in_chip_start")
    grad_x, g["g_mix"] = _matmul(dproj, w_in_t, mode="nn", tm=1024, tn=1024, tk=1152, name="mm_d_h1", extras=(x, dx1),
                                 row_params=(pin(p["g_mix"], token),), epilogue=_ep_rms_bwd, sums=[(1, D_MODEL)])
    sent_small, _ = _exchange_start([_pack_small(g)], scatter=False, name="small_grads_start")

    grads, delta, new_m, new_v = {}, {}, {}, {}

    def update(names, sent, after, wait_name):
        parts = landed_with_own(_copies_wait(sent, after, name=wait_name), True)
        for name, part in zip(names, parts, strict=True):
            grads[name], delta[name], new_m[name], new_v[name] = _reduce_adamw(part, p[name], m[name], v[name], tr=128,
                                                                                name="adamw_" + name)

    update(("w_down", "w_up"), sent_mlp, grad_x, "grads_wait_mlp")
    update(("w_out", "xq_w", "xkv_w", "xo_w"), sent_mid, delta["w_up"], "grads_wait_mid")
    chip_sums, landed = _copies_wait(sent_in, delta["xo_w"], name="grads_in_chip_wait")[0]
    part = lax.dynamic_update_slice(landed, chip_sums[0:1], (0, 0, 0))
    res = _reduce_adamw(part, w_in_own, m_in_own, v_in_own, tr=part.shape[1], tc=256, name="adamw_w_in")
    grads["w_in"], delta["w_in"], new_m["w_in"], new_v["w_in"] = [r.T for r in res]
    small_parts = landed_with_own(_copies_wait(sent_small, delta["w_in"], name="small_grads_wait"), False)[0]
    sg, sd, sm, sv = _reduce_adamw(small_parts, *slabs, tr=SLAB_ROWS, name="adamw_small")
    for dst, slab in ((grads, sg), (delta, sd), (new_m, sm), (new_v, sv)):
        for name, flat in _unpack_small(slab).items():
            if name not in NOT_PARAMS:
                dst[name] = flat.reshape(p[name].shape)
    loss = _unpack_small(sg)["loss"][0]
    cw_shard = p["conv_w"].shape[1]
    grads["conv_w"] = lax.dynamic_slice(_unpack_small(sg)["conv_w"].reshape(4, 1536), (0, me * cw_shard), (4, cw_shard))
    delta["conv_w"], new_m["conv_w"], new_v["conv_w"] = _adamw(p["conv_w"], grads["conv_w"], m["conv_w"], v["conv_w"],
                                                               name="adamw_conv_w")
    return loss, grad_x, grads, delta, new_m, new_v


def kernel(x, mem, g_mix, w_in, conv_w, conv_b, dt_bias, a_log, d_skip, ssm_norm_w, g_q, g_k, f_bias, w_out, g_xattn, g_mem, xq_w, xkv_w, xg_q, xg_k, xo_w, g_mlp, w_up, w_down, loss_target, m_g_mix, m_w_in, m_conv_w, m_conv_b, m_dt_bias, m_a_log, m_d_skip, m_ssm_norm_w, m_g_q, m_g_k, m_f_bias, m_w_out, m_g_xattn, m_g_mem, m_xq_w, m_xkv_w, m_xg_q, m_xg_k, m_xo_w, m_g_mlp, m_w_up, m_w_down, v_g_mix, v_w_in, v_conv_w, v_conv_b, v_dt_bias, v_a_log, v_d_skip, v_ssm_norm_w, v_g_q, v_g_k, v_f_bias, v_w_out, v_g_xattn, v_g_mem, v_xq_w, v_xkv_w, v_xg_q, v_xg_k, v_xo_w, v_g_mlp, v_w_up, v_w_down):
    args = locals()
    drop = lambda t: t[0] if t.ndim == 3 else t
    p = {n: drop(args[n]) for n in WEIGHTS}
    m = {n: drop(args["m_" + n]) for n in WEIGHTS}
    v = {n: drop(args["v_" + n]) for n in WEIGHTS}
    loss, grad_x, grads, delta, new_m, new_v = _step(p, m, v, x[0], mem[0], loss_target[0])
    outs = [loss, grad_x[None]]
    for d in (grads, delta, new_m, new_v):
        outs += [d[n].reshape(args[n].shape) for n in WEIGHTS]
    return tuple(outs)
```

```python
import functools

import jax
import jax.numpy as jnp
from jax import lax
from jax.experimental import pallas as pl
from jax.experimental.pallas import tpu as pltpu

F32, BF16 = jnp.float32, jnp.bfloat16
SDS = jax.ShapeDtypeStruct
HI = lax.Precision.HIGHEST
MESH = pl.DeviceIdType.MESH

N_DEV = 8
EPS = 1e-5
D_MODEL = 1024
SSM_HEADS, SSM_P, SSM_N, SSM_GROUPS, CHUNK = 16, 64, 128, 2, 128
ATT_HEADS, ATT_D = 16, 64
X_HEADS, X_D = 4, 256
LANES = 128
VMEM_LIMIT = 48 * 1024 * 1024
NEG = -1e30

GRAD_WIRE = BF16
ADAM_LR, ADAM_B1, ADAM_B2, ADAM_EPS, ADAM_WD, ADAM_STEP = 0.001, 0.9, 0.999, 1e-08, 0.01, 10

C_Z, C_XS, C_Q, C_K, C_V, C_B, C_C, C_DTF, P_COLS = 0, 1024, 2048, 3072, 4096, 5120, 5376, 5632, 5760

_NN = (((1,), (0,)), ((), ()))
_NT = (((1,), (1,)), ((), ()))
_TN = (((0,), (0,)), ((), ()))


def _cparams(**kw):
    return pltpu.CompilerParams(vmem_limit_bytes=VMEM_LIMIT, **kw)


def _bdot(a, b, dn):
    return lax.dot_general(a.astype(BF16), b.astype(BF16), dn, preferred_element_type=F32)


@jax.custom_vjp
def mm_nn(a, b):
    return _bdot(a, b, _NN)


mm_nn.defvjp(lambda a, b: (mm_nn(a, b), (a, b)), lambda r, g: (_bdot(g, r[1], _NT), _bdot(r[0], g, _TN)))


@jax.custom_vjp
def mm_nt(a, b):
    return _bdot(a, b, _NT)


mm_nt.defvjp(lambda a, b: (mm_nt(a, b), (a, b)), lambda r, g: (_bdot(g, r[1], _NN), _bdot(g, r[0], _TN)))


@jax.custom_vjp
def mm_tn(a, b):
    return _bdot(a, b, _TN)


mm_tn.defvjp(lambda a, b: (mm_tn(a, b), (a, b)), lambda r, g: (_bdot(r[1], g, _NT), _bdot(r[0], g, _NN)))


def _cdot(x, c):
    return jnp.dot(x, c, precision=HI, preferred_element_type=F32)


def _iota(shape, dim):
    return lax.broadcasted_iota(jnp.int32, shape, dim)


def _matmul(a, b, *, mode, tm, tn, tk, name, out_dtype=F32, add=None, extras=(), epilogue=None, out_dtypes=None,
            b_shards=False, out_shards=False, row_params=(), sums=()):
    if mode == "tn":
        K, M = a.shape
    else:
        M, K = a.shape
    if b_shards:
        N = b.shape[1] if mode == "nt" else b.shape[0] * b.shape[2]
        tn, tk = (tn, b.shape[2]) if mode == "nt" else (b.shape[2], tk)
    else:
        N = b.shape[0] if mode == "nt" else b.shape[1]
    tm, tn, tk = min(tm, M), min(tn, N), min(tk, K)
    assert M % tm == 0 and N % tn == 0 and K % tk == 0, (name, M, N, K, tm, tn, tk)
    assert not b_shards or (K // tk if mode == "nt" else N // tn) == b.shape[0], name
    assert not (out_shards and (extras or add is not None)), name
    nk = K // tk
    dn = {"nn": _NN, "nt": _NT, "tn": _TN}[mode]
    if add is not None:
        extras, epilogue = (add,), lambda acc, r: (acc + r,)
    elif epilogue is None:
        epilogue = lambda acc: (acc,)
    out_dtypes = out_dtypes or [out_dtype]
    ne, no, ns = len(extras) + len(row_params), len(out_dtypes), len(sums)
    assert all(s == (1, 1) or (s == (1, N) and tn == N) for s in sums), name

    def body(*refs):
        a_ref, b_ref = refs[:2]
        e_refs, o_refs, s_refs = refs[2:2 + ne], refs[2 + ne:2 + ne + no], refs[2 + ne + no:2 + ne + no + ns]

        def finish(acc):
            res = epilogue(acc, *[e[...] for e in e_refs])
            for o_ref, v in zip(o_refs, res[:no], strict=True):
                o_ref[...] = v.astype(o_ref.dtype)
            first_tile = jnp.logical_and(pl.program_id(0) == 0, pl.program_id(1) == 0)
            for s_ref, v in zip(s_refs, res[no:], strict=True):
                @pl.when(first_tile)
                def _(s_ref=s_ref, v=v):
                    s_ref[...] = v

                @pl.when(jnp.logical_not(first_tile))
                def _(s_ref=s_ref, v=v):
                    s_ref[...] += v

        prod = _bdot(a_ref[...], b_ref[...], dn)
        if nk == 1:
            finish(prod)
            return
        acc_ref = refs[-1]
        k = pl.program_id(2)

        @pl.when(k == 0)
        def _():
            acc_ref[...] = prod

        @pl.when(jnp.logical_and(k > 0, k < nk - 1))
        def _():
            acc_ref[...] += prod

        @pl.when(k == nk - 1)
        def _():
            finish(acc_ref[...] + prod)

    a_spec = pl.BlockSpec((tk, tm), lambda i, j, k: (k, i)) if mode == "tn" else pl.BlockSpec((tm, tk), lambda i, j, k: (i, k))
    if b_shards and mode == "nt":
        b_spec = pl.BlockSpec((None, tn, tk), lambda i, j, k: (k, j, 0))
    elif b_shards:
        b_spec = pl.BlockSpec((None, tk, tn), lambda i, j, k: (j, k, 0))
    elif mode == "nt":
        b_spec = pl.BlockSpec((tn, tk), lambda i, j, k: (j, k))
    else:
        b_spec = pl.BlockSpec((tk, tn), lambda i, j, k: (k, j))
    if out_shards:
        o_spec, o_shape = pl.BlockSpec((None, tm, tn), lambda i, j, k: (j, i, 0)), (N // tn, M, tn)
    else:
        o_spec, o_shape = pl.BlockSpec((tm, tn), lambda i, j, k: (i, j)), (M, N)
    row_spec = pl.BlockSpec((1, tn), lambda i, j, k: (0, j))
    sum_specs = [pl.BlockSpec(s, lambda i, j, k: (0, 0)) for s in sums]
    res = pl.pallas_call(
        body, name=name, grid=(M // tm, N // tn, nk),
        in_specs=[a_spec, b_spec] + [o_spec] * len(extras) + [row_spec] * len(row_params),
        out_specs=[o_spec] * no + sum_specs, out_shape=[SDS(o_shape, dt) for dt in out_dtypes] + [SDS(s, F32) for s in sums],
        scratch_shapes=[pltpu.VMEM((tm, tn), F32)] if nk > 1 else [],
        compiler_params=_cparams(dimension_semantics=(("arbitrary",) * 3 if sums else ("parallel", "parallel", "arbitrary"))),
    )(a, b, *extras, *row_params)
    return res[0] if no + ns == 1 else res


def _row_spec(tm, spec):
    _, c0, w = spec
    assert c0 % w == 0
    return pl.BlockSpec((tm, w), functools.partial(lambda i, cb: (i, cb), cb=c0 // w))


def _par_spec(spec):
    arr, c0, w = spec
    assert c0 % w == 0
    return pl.BlockSpec((arr.shape[0], w), functools.partial(lambda i, cb: (0, cb), cb=c0 // w))


def _whole(arr):
    return (arr, 0, arr.shape[1])


def _rw_fwd(fn, rows, params, outs, *, tm, name):
    M = rows[0][0].shape[0]
    nr, npar = len(rows), len(params)

    def body(*refs):
        rv = [r[...].astype(F32) for r in refs[:nr]]
        pv = [p[...].astype(F32) for p in refs[nr:nr + npar]]
        res = fn(*rv, *pv)
        for o_ref, v in zip(refs[nr + npar:], res, strict=True):
            o_ref[...] = v.astype(o_ref.dtype)

    return pl.pallas_call(
        body, name=name, grid=(M // tm,),
        in_specs=[_row_spec(tm, r) for r in rows] + [_par_spec(p) for p in params],
        out_specs=[pl.BlockSpec((tm, w), lambda i: (i, 0)) for w, _ in outs],
        out_shape=[SDS((M, w), dt) for w, dt in outs],
        compiler_params=_cparams(dimension_semantics=("parallel",)),
    )(*[r[0] for r in rows], *[p[0] for p in params])


def _rw_bwd(fn, rows, params, cts, *, tm, name, row_grads, adds=None, join_rows=False, join_params=0):
    M = rows[0][0].shape[0]
    adds = adds or {}
    nr, npar, nc = len(rows), len(params), len(cts)
    add_keys = sorted(adds)
    want = [k for k in range(nr) if row_grads[k] is not None]

    def body(*refs):
        pos = 0
        r_refs = refs[pos:pos + nr]; pos += nr
        p_refs = refs[pos:pos + npar]; pos += npar
        c_refs = refs[pos:pos + nc]; pos += nc
        a_refs = dict(zip(add_keys, refs[pos:pos + len(add_keys)])); pos += len(add_keys)
        n_row_out = 1 if join_rows else len(want)
        dr_refs = refs[pos:pos + n_row_out]; pos += n_row_out
        dp_refs = refs[pos:]
        rv = [r[...].astype(F32) for r in r_refs]
        pv = [p[...].astype(F32) for p in p_refs]
        _, vjp = jax.vjp(fn, *rv, *pv)
        g = vjp(tuple(c[...].astype(F32) for c in c_refs))
        row_vals = []
        for k in want:
            v = g[k]
            if k in a_refs:
                v = v + a_refs[k][...].astype(F32)
            row_vals.append(v)
        if join_rows:
            row_vals = [jnp.concatenate(row_vals, axis=1)]
        for ref, v in zip(dr_refs, row_vals, strict=True):
            ref[...] = v.astype(ref.dtype)
        par_vals = list(g[nr:])
        if join_params:
            par_vals = [jnp.concatenate(par_vals[:join_params], axis=1)] + par_vals[join_params:]
        first = pl.program_id(0) == 0
        for ref, v in zip(dp_refs, par_vals, strict=True):
            @pl.when(first)
            def _(ref=ref, v=v):
                ref[...] = v

            @pl.when(jnp.logical_not(first))
            def _(ref=ref, v=v):
                ref[...] += v

    row_out = [(rows[k][2], row_grads[k]) for k in want]
    if join_rows:
        row_out = [(sum(w for w, _ in row_out), row_out[0][1])]
    par_out = [(p[0].shape[0], p[2]) for p in params]
    if join_params:
        par_out = [(par_out[0][0], sum(w for _, w in par_out[:join_params]))] + par_out[join_params:]
    res = pl.pallas_call(
        body, name=name, grid=(M // tm,),
        in_specs=([_row_spec(tm, r) for r in rows] + [_par_spec(p) for p in params] + [_row_spec(tm, c) for c in cts]
                  + [_row_spec(tm, adds[k]) for k in add_keys]),
        out_specs=([pl.BlockSpec((tm, w), lambda i: (i, 0)) for w, _ in row_out]
                   + [pl.BlockSpec(s, lambda i: (0, 0)) for s in par_out]),
        out_shape=[SDS((M, w), dt) for w, dt in row_out] + [SDS(s, F32) for s in par_out],
        compiler_params=_cparams(dimension_semantics=("arbitrary",)),
    )(*[r[0] for r in rows], *[p[0] for p in params], *[c[0] for c in cts], *[adds[k][0] for k in add_keys])
    return res


def _rms_fn(x, g):
    r = lax.rsqrt(jnp.mean(x * x, axis=-1, keepdims=True) + EPS)
    return (x * r * g,)


def _ep_residual_rms(acc, res, g):
    x = acc + res
    return x, _rms_fn(x, g)[0]


def _ep_rms_bwd(dh, x, dres, g):
    r = lax.rsqrt(jnp.mean(x * x, axis=-1, keepdims=True) + EPS)
    t = dh * g
    dx = dres + r * (t - x * (r * r) * jnp.mean(t * x, axis=-1, keepdims=True))
    return dx, jnp.sum(dh * x * r, axis=0, keepdims=True)


def _ep_loss(acc, res, target, *, width):
    e = acc + res - target
    return e * (1.0 / width), jnp.sum(jnp.sum(e * e, axis=1, keepdims=True), axis=0, keepdims=True) * (0.5 / width)


def _seg_mats(width, seg):
    n = width // seg
    p = (_iota((width, n), 0) // seg == _iota((width, n), 1)).astype(F32)
    e = (_iota((n, width), 1) // seg == _iota((n, width), 0)).astype(F32)
    return p, e


def _gate_fn(y, xs, z, dskip, w):
    width = SSM_HEADS * SSM_P
    _, e = _seg_mats(width, SSM_P)
    y = (y + _cdot(dskip, e) * xs) * (z * jax.nn.sigmoid(z))
    g0 = _iota((1, width), 1) < width // SSM_GROUPS
    y2 = y * y
    gw = width // SSM_GROUPS
    ms0 = jnp.sum(jnp.where(g0, y2, 0.0), axis=-1, keepdims=True) * (1.0 / gw)
    ms1 = jnp.sum(jnp.where(g0, 0.0, y2), axis=-1, keepdims=True) * (1.0 / gw)
    r = jnp.where(g0, lax.rsqrt(ms0 + EPS), lax.rsqrt(ms1 + EPS))
    return (y * r * w,)


def _xattn_fn(q0, q1, q2, q3, k0, k1, k2, k3, v0, v1, v2, v3, gq, gk):
    def norm(u, g):
        return u * lax.rsqrt(jnp.mean(u * u, axis=-1, keepdims=True) + EPS) * g
    outs = []
    for q, k, v in ((q0, k0, v0), (q1, k1, v1), (q2, k2, v2), (q3, k3, v3)):
        s = mm_nt(norm(q, gq), norm(k, gk)) * (X_D ** -0.5)
        p = jnp.exp(s - lax.stop_gradient(jnp.max(s, axis=-1, keepdims=True)))
        p = p / jnp.sum(p, axis=-1, keepdims=True)
        outs.append(mm_nn(p, v))
    return (jnp.concatenate(outs, axis=-1),)


CONV_TC = 256


def _shift_down(u, k):
    if k == 0:
        return u
    return jnp.where(_iota(u.shape, 0) >= k, pltpu.roll(u, k, axis=0), 0.0)


def _shift_up(u, k):
    if k == 0:
        return u
    n = u.shape[0]
    return jnp.where(_iota(u.shape, 0) < n - k, pltpu.roll(u, n - k, axis=0), 0.0)


def _conv_pre(u, w_ref, b):
    pre = b + w_ref[3:4, :] * u
    for k in (1, 2, 3):
        pre = pre + w_ref[3 - k:4 - k, :] * _shift_down(u, k)
    return pre


def _conv_fwd(src, c0, width, w, b, *, name):
    S = src.shape[0]
    cb0 = c0 // CONV_TC

    def body(u_ref, w_ref, b_ref, o_ref):
        pre = _conv_pre(u_ref[...], w_ref, b_ref[...])
        o_ref[...] = pre * jax.nn.sigmoid(pre)

    return pl.pallas_call(
        body, name=name, grid=(width // CONV_TC,),
        in_specs=[pl.BlockSpec((S, CONV_TC), lambda j: (0, cb0 + j)), pl.BlockSpec((4, CONV_TC), lambda j: (0, j)),
                  pl.BlockSpec((1, CONV_TC), lambda j: (0, j))],
        out_specs=pl.BlockSpec((S, CONV_TC), lambda j: (0, j)), out_shape=SDS((S, width), F32),
        compiler_params=_cparams(dimension_semantics=("parallel",)),
    )(src, w, b)


def _conv_bwd(src, c0, width, w, b, douts, *, name):
    S = src.shape[0]
    cb0 = c0 // CONV_TC
    nd = len(douts)

    def body(*refs):
        u_ref, w_ref, b_ref = refs[:3]
        d_refs = refs[3:3 + nd]
        du_ref, dw_ref, db_ref = refs[3 + nd:]
        u = u_ref[...]
        pre = _conv_pre(u, w_ref, b_ref[...])
        sg = jax.nn.sigmoid(pre)
        dout = d_refs[0][...]
        for r in d_refs[1:]:
            dout = dout + r[...]
        dpre = dout * (sg * (1.0 + pre * (1.0 - sg)))
        du = w_ref[3:4, :] * dpre
        dw_ref[3:4, :] = jnp.sum(dpre * u, axis=0, keepdims=True)
        for k in (1, 2, 3):
            du = du + w_ref[3 - k:4 - k, :] * _shift_up(dpre, k)
            dw_ref[3 - k:4 - k, :] = jnp.sum(dpre * _shift_down(u, k), axis=0, keepdims=True)
        du_ref[...] = du.astype(du_ref.dtype)
        db_ref[...] = jnp.sum(dpre, axis=0, keepdims=True)

    return pl.pallas_call(
        body, name=name, grid=(width // CONV_TC,),
        in_specs=[pl.BlockSpec((S, CONV_TC), lambda j: (0, cb0 + j)), pl.BlockSpec((4, CONV_TC), lambda j: (0, j)),
                  pl.BlockSpec((1, CONV_TC), lambda j: (0, j))] + [pl.BlockSpec((S, CONV_TC), lambda j: (0, j))] * nd,
        out_specs=[pl.BlockSpec((S, CONV_TC), lambda j: (0, j)), pl.BlockSpec((4, CONV_TC), lambda j: (0, j)),
                   pl.BlockSpec((1, CONV_TC), lambda j: (0, j))],
        out_shape=[SDS((S, width), BF16), SDS((4, width), F32), SDS((1, width), F32)],
        compiler_params=_cparams(dimension_semantics=("parallel",)),
    )(src, w, b, *douts)


def _softplus(x):
    return jnp.maximum(x, 0.0) + jnp.log(1.0 + jnp.exp(-jnp.abs(x)))


def _prefix_sum(x, seg):
    n = x.shape[1]
    pos = _iota(x.shape, 1) % seg
    k = 1
    while k < seg:
        x = x + jnp.where(pos >= k, pltpu.roll(x, k, axis=1), 0.0)
        k *= 2
    return x


def _suffix_sum(x, seg):
    n = x.shape[1]
    pos = _iota(x.shape, 1) % seg
    k = 1
    while k < seg:
        x = x + jnp.where(pos + k < seg, pltpu.roll(x, n - k, axis=1), 0.0)
        k *= 2
    return x


def _dtf_fwd(dtf_t, dt_bias, a_log, f_bias):
    S = dtf_t.shape[1]

    def body(x_ref, db_ref, al_ref, fb_ref, dt_ref, acs_ref, cum_ref):
        dt = _softplus(x_ref[0:16, :] + db_ref[...])
        dt_ref[...] = dt
        acs_ref[...] = _prefix_sum(dt * (-jnp.exp(al_ref[...])), CHUNK)
        cum_ref[...] = _prefix_sum(-_softplus(-(x_ref[16:32, :] + fb_ref[...])), S)

    return pl.pallas_call(body, name="dtf_fwd", out_shape=[SDS((16, S), F32)] * 3, compiler_params=_cparams())(
        dtf_t, dt_bias, a_log, f_bias)


def _dtf_bwd(dtf_t, dt_bias, a_log, f_bias, d_dt, d_acs_a, d_acs_b, d_cum):
    S = dtf_t.shape[1]

    def body(x_ref, db_ref, al_ref, fb_ref, ddt_ref, da1_ref, da2_ref, dc_ref, dx_ref, ddb_ref, dal_ref, dfb_ref):
        xd = x_ref[0:16, :] + db_ref[...]
        dt = _softplus(xd)
        a = -jnp.exp(al_ref[...])
        d_da = _suffix_sum(da1_ref[...] + da2_ref[...], CHUNK)
        d_dt = ddt_ref[...] + d_da * a
        dal_ref[...] = jnp.sum(d_da * dt, axis=1, keepdims=True) * a
        d_xd = d_dt * jax.nn.sigmoid(xd)
        ddb_ref[...] = jnp.sum(d_xd, axis=1, keepdims=True)
        xf = x_ref[16:32, :] + fb_ref[...]
        d_xf = _suffix_sum(dc_ref[...], S) * jax.nn.sigmoid(-xf)
        dfb_ref[...] = jnp.sum(d_xf, axis=1, keepdims=True)
        dx_ref[0:16, :] = d_xd
        dx_ref[16:32, :] = d_xf

    return pl.pallas_call(body, name="dtf_bwd", out_shape=[SDS((32, S), F32)] + [SDS((16, 1), F32)] * 3,
                          compiler_params=_cparams())(dtf_t, dt_bias, a_log, f_bias, d_dt, d_acs_a, d_acs_b, d_cum)


SSM_PAIRS = SSM_HEADS // 2 // SSM_GROUPS


def _ssd_pair(xs, dtc, acol, arow, bm, cm, cbm, h, hp):
    L = CHUNK
    first = _iota((1, LANES), 1) < SSM_P
    i16, s16 = _iota((L, 16), 1), _iota((16, L), 0)
    ha, hb = 2 * hp, 2 * hp + 1

    def selc(blk, hh):
        return jnp.sum(jnp.where(i16 == hh, blk, 0.0), axis=1, keepdims=True)

    def selr(blk, hh):
        return jnp.sum(jnp.where(s16 == hh, blk, 0.0), axis=0, keepdims=True)

    x = xs * jnp.where(first, selc(dtc, ha), selc(dtc, hb))
    ca, cb, ra, rb = selc(acol, ha), selc(acol, hb), selr(arow, ha), selr(arow, hb)
    tri = _iota((L, L), 0) >= _iota((L, L), 1)
    la = jnp.exp(jnp.where(tri, ca - ra, NEG))
    lb = jnp.exp(jnp.where(tri, cb - rb, NEG))
    y = jnp.where(first, mm_nn(cbm * la, x), mm_nn(cbm * lb, x))
    y = y + jnp.where(first, jnp.exp(ca), jnp.exp(cb)) * mm_nn(cm, h)
    last = _iota((1, L), 1) == L - 1
    ala = jnp.sum(jnp.where(last, ra, 0.0), axis=1, keepdims=True)
    alb = jnp.sum(jnp.where(last, rb, 0.0), axis=1, keepdims=True)
    dec = jnp.where(first, jnp.exp(ala - ca), jnp.exp(alb - cb))
    hn = jnp.where(first, jnp.exp(ala), jnp.exp(alb)) * h + mm_tn(bm, x * dec)
    return y, hn


def _ssd_group(*args, grp):
    xs, (dtc, acol, arow, bm, cm), hs = args[:SSM_PAIRS], args[SSM_PAIRS:SSM_PAIRS + 5], args[SSM_PAIRS + 5:]
    cbm = mm_nt(cm, bm)
    res = [_ssd_pair(xs[j], dtc, acol, arow, bm, cm, cbm, hs[j], SSM_PAIRS * grp + j) for j in range(SSM_PAIRS)]
    return tuple(r[0] for r in res) + tuple(r[1] for r in res)


def _ssd_specs(nc, rev):
    L = CHUNK
    cidx = (lambda c: nc - 1 - c) if rev else (lambda c: c)
    return dict(
        xs=pl.BlockSpec((L, SSM_PAIRS * LANES), lambda c, g: (cidx(c), g)),
        col=pl.BlockSpec((L, 16), lambda c, g: (cidx(c), 0)),
        row=pl.BlockSpec((16, L), lambda c, g: (0, cidx(c))),
        b=pl.BlockSpec((L, SSM_N), lambda c, g: (cidx(c), g)),
        c=pl.BlockSpec((L, SSM_N), lambda c, g: (cidx(c), SSM_GROUPS + g)),
        st=pl.BlockSpec((1, SSM_PAIRS, SSM_N, LANES), lambda c, g: (cidx(c), g, 0, 0)),
    )


def _lane_pieces(v):
    return [v[:, LANES * j:LANES * (j + 1)] for j in range(v.shape[1] // LANES)]


def _ssd_fwd(xs, dt_col, acs_col, acs_row, bc):
    S = xs.shape[0]
    nc, nhp = S // CHUNK, SSM_HEADS // 2
    sp = _ssd_specs(nc, False)

    def body(xs_ref, dt_ref, ac_ref, ar_ref, b_ref, c_ref, y_ref, hs_ref, h_scr):
        c, g = pl.program_id(0), pl.program_id(1)

        @pl.when(c == 0)
        def _():
            for j in range(SSM_PAIRS):
                h_scr[SSM_PAIRS * g + j] = jnp.zeros((SSM_N, LANES), F32)

        hs = [h_scr[SSM_PAIRS * g + j] for j in range(SSM_PAIRS)]
        for j in range(SSM_PAIRS):
            hs_ref[0, j] = hs[j]
        res = _ssd_group(*_lane_pieces(xs_ref[...]), dt_ref[...], ac_ref[...], ar_ref[...], b_ref[...], c_ref[...], *hs,
                         grp=g)
        y_ref[...] = jnp.concatenate(res[:SSM_PAIRS], axis=1)
        for j in range(SSM_PAIRS):
            h_scr[SSM_PAIRS * g + j] = res[SSM_PAIRS + j]

    return pl.pallas_call(
        body, name="ssd_fwd", grid=(nc, SSM_GROUPS),
        in_specs=[sp["xs"], sp["col"], sp["col"], sp["row"], sp["b"], sp["c"]],
        out_specs=[sp["xs"], sp["st"]],
        out_shape=[SDS((S, SSM_HEADS * SSM_P), F32), SDS((nc, nhp, SSM_N, LANES), F32)],
        scratch_shapes=[pltpu.VMEM((nhp, SSM_N, LANES), F32)],
        compiler_params=_cparams(dimension_semantics=("arbitrary", "arbitrary")),
    )(xs, dt_col, acs_col, acs_row, bc, bc)


def _ssd_bwd(xs, dt_col, acs_col, acs_row, bc, hs, dy):
    S = xs.shape[0]
    nc, nhp = S // CHUNK, SSM_HEADS // 2
    sp = _ssd_specs(nc, True)

    def body(xs_ref, dt_ref, ac_ref, ar_ref, b_ref, c_ref, hs_ref, dy_ref,
             dxs_ref, ddt_ref, dac_ref, dar_ref, db_ref, dc_ref, dh_scr):
        c, g = pl.program_id(0), pl.program_id(1)

        @pl.when(c == 0)
        def _():
            for j in range(SSM_PAIRS):
                dh_scr[SSM_PAIRS * g + j] = jnp.zeros((SSM_N, LANES), F32)

        _, vjp = jax.vjp(functools.partial(_ssd_group, grp=g), *_lane_pieces(xs_ref[...]), dt_ref[...], ac_ref[...],
                         ar_ref[...], b_ref[...], c_ref[...], *[hs_ref[0, j] for j in range(SSM_PAIRS)])
        grads = vjp(tuple(_lane_pieces(dy_ref[...])) + tuple(dh_scr[SSM_PAIRS * g + j] for j in range(SSM_PAIRS)))
        dxs_ref[...] = jnp.concatenate(grads[:SSM_PAIRS], axis=1)
        ddt, dac, dar, db, dc = grads[SSM_PAIRS:SSM_PAIRS + 5]
        for j in range(SSM_PAIRS):
            dh_scr[SSM_PAIRS * g + j] = grads[SSM_PAIRS + 5 + j]
        db_ref[...] = db
        dc_ref[...] = dc

        @pl.when(g == 0)
        def _():
            ddt_ref[...] = ddt
            dac_ref[...] = dac
            dar_ref[...] = dar

        @pl.when(g > 0)
        def _():
            ddt_ref[...] += ddt
            dac_ref[...] += dac
            dar_ref[...] += dar

    return pl.pallas_call(
        body, name="ssd_bwd", grid=(nc, SSM_GROUPS),
        in_specs=[sp["xs"], sp["col"], sp["col"], sp["row"], sp["b"], sp["c"], sp["st"], sp["xs"]],
        out_specs=[sp["xs"], sp["col"], sp["col"], sp["row"], sp["b"], sp["b"]],
        out_shape=[SDS((S, SSM_HEADS * SSM_P), F32), SDS((S, 16), F32), SDS((S, 16), F32), SDS((16, S), F32),
                   SDS((S, SSM_GROUPS * SSM_N), F32), SDS((S, SSM_GROUPS * SSM_N), F32)],
        scratch_shapes=[pltpu.VMEM((nhp, SSM_N, LANES), F32)],
        compiler_params=_cparams(dimension_semantics=("arbitrary", "arbitrary")),
    )(xs, dt_col, acs_col, acs_row, bc, bc, hs, dy)


ATT_T = 2048
ATT_U = 512


def _pick_col(blk, h):
    return jnp.sum(jnp.where(_iota(blk.shape, 1) == h, blk, 0.0), axis=1, keepdims=True)


def _pick_row(blk, h):
    return jnp.sum(jnp.where(_iota(blk.shape, 0) == h, blk, 0.0), axis=0, keepdims=True)


def _pair_norm(x, g2, first):
    x2 = x * x
    sa = jnp.sum(jnp.where(first, x2, 0.0), axis=1, keepdims=True)
    sb = jnp.sum(jnp.where(first, 0.0, x2), axis=1, keepdims=True)
    r = jnp.where(first, lax.rsqrt(sa * (1.0 / ATT_D) + EPS), lax.rsqrt(sb * (1.0 / ATT_D) + EPS))
    return x * r * g2, r


def _pair_norm_bwd(dxn, x, r, g2, first):
    t = dxn * g2
    tx = t * x
    ma = jnp.sum(jnp.where(first, tx, 0.0), axis=1, keepdims=True)
    mb = jnp.sum(jnp.where(first, 0.0, tx), axis=1, keepdims=True)
    dx = r * (t - x * (r * r) * (jnp.where(first, ma, mb) * (1.0 / ATT_D)))
    return dx, jnp.sum(dxn * x * r, axis=0, keepdims=True)


def _fox_fwd(src, q_c0, k_c0, v_c0, gq2, gk2, cum_col, cum_row3):
    S = src.shape[0]
    T = ATT_T
    nq, nhp = S // T, ATT_HEADS // 2
    qb0, kb0, vb0 = q_c0 // LANES, k_c0 // LANES, v_c0 // LANES
    scale = ATT_D ** -0.5

    def body(q_ref, kraw_ref, v_ref, gq_ref, gk_ref, cc_ref, cr_ref, o_ref, l_ref, k_ref):
        hp, i = pl.program_id(0), pl.program_id(1)
        first = _iota((1, LANES), 1) < ATT_D

        @pl.when(i == 0)
        def _():
            k_ref[...] = _pair_norm(kraw_ref[...], gk_ref[...], first)[0].astype(BF16)

        H = ATT_U

        def attend(tile, half):
            rows = pl.ds(half * H, H)
            row0 = tile * T + half * H
            klen = row0 + H
            q = (_pair_norm(q_ref[rows, :], gq_ref[...], first)[0] * scale).astype(BF16)
            zero = jnp.zeros_like(q)
            cc = cc_ref[rows, :]
            k = k_ref[0:klen, :]
            v = v_ref[0:klen, :].astype(BF16)
            allowed = _iota((H, klen), 0) + row0 >= _iota((H, klen), 1)
            outs, lses = [], []
            for hh in range(2):
                sel = first if hh == 0 else jnp.logical_not(first)
                ck = jnp.concatenate([_pick_row(cr_ref[j], 2 * hp + hh) for j in range(tile + 1)], axis=1)[:, :klen]
                s = _bdot(jnp.where(sel, q, zero), k, _NT) + (_pick_col(cc, 2 * hp + hh) - ck)
                s = jnp.where(allowed, s, NEG)
                m = jnp.max(s, axis=1, keepdims=True)
                p = jnp.exp(s - m)
                l = jnp.sum(p, axis=1, keepdims=True)
                outs.append(_bdot(p, v, _NN) / l)
                lses.append(m + jnp.log(l))
            o_ref[rows, :] = jnp.where(first, outs[0], outs[1]).astype(o_ref.dtype)
            l_ref[rows, :] = jnp.where(first, lses[0], lses[1])

        for tile in range(nq):
            @pl.when(i == tile)
            def _(tile=tile):
                for half in range(T // H):
                    attend(tile, half)

    gain = pl.BlockSpec((1, LANES), lambda hp, i: (0, 0))
    return pl.pallas_call(
        body, name="fox_fwd", grid=(nhp, nq),
        in_specs=[pl.BlockSpec((T, LANES), lambda hp, i: (i, qb0 + hp)), pl.BlockSpec((S, LANES), lambda hp, i: (0, kb0 + hp)),
                  pl.BlockSpec((S, LANES), lambda hp, i: (0, vb0 + hp)), gain, gain,
                  pl.BlockSpec((T, 16), lambda hp, i: (i, 0)), pl.BlockSpec((nq, 16, T), lambda hp, i: (0, 0, 0))],
        out_specs=[pl.BlockSpec((T, LANES), lambda hp, i: (i, hp))] * 2,
        out_shape=[SDS((S, ATT_HEADS * ATT_D), BF16), SDS((S, ATT_HEADS * ATT_D), F32)],
        scratch_shapes=[pltpu.VMEM((S, LANES), BF16)],
        compiler_params=_cparams(dimension_semantics=("arbitrary", "arbitrary")),
    )(src, src, src, gq2, gk2, cum_col, cum_row3)


def _fox_bwd(src, q_c0, k_c0, v_c0, gq2, gk2, cum_col, cum_row3, lse, dsrc, d_c0):
    S = src.shape[0]
    T = ATT_T
    nq, nhp = S // T, ATT_HEADS // 2
    qb0, kb0, vb0, db0 = q_c0 // LANES, k_c0 // LANES, v_c0 // LANES, d_c0 // LANES
    scale = ATT_D ** -0.5

    def body(q_ref, kraw_ref, v_ref, gq_ref, gk_ref, cc_ref, cr_ref, l_ref, do_ref,
             dq_ref, dk_ref, dv_ref, dc_ref, dg_ref, k_ref, dk_acc, dv_acc):
        hp, i = pl.program_id(0), pl.program_id(1)
        first = _iota((1, LANES), 1) < ATT_D

        @pl.when(i == 0)
        def _():
            k_ref[...] = _pair_norm(kraw_ref[...], gk_ref[...], first)[0].astype(BF16)
            dk_acc[...] = jnp.zeros_like(dk_acc)
            dv_acc[...] = jnp.zeros_like(dv_acc)
            dc_ref[...] = jnp.zeros_like(dc_ref)
            dg_ref[...] = jnp.zeros_like(dg_ref)

        H = ATT_U

        def backprop(tile, half):
            rows = pl.ds(half * H, H)
            row0 = tile * T + half * H
            klen = row0 + H
            q_raw = q_ref[rows, :]
            qn, rq = _pair_norm(q_raw, gq_ref[...], first)
            q = (qn * scale).astype(BF16)
            zq = jnp.zeros_like(q)
            dob = do_ref[rows, :].astype(BF16)
            zd = jnp.zeros_like(dob)
            lse_blk, cc = l_ref[rows, :], cc_ref[rows, :]
            k = k_ref[0:klen, :]
            zk = jnp.zeros_like(k)
            allowed = _iota((H, klen), 0) + row0 >= _iota((H, klen), 1)
            dq = jnp.zeros((H, LANES), F32)
            for hh in range(2):
                sel = first if hh == 0 else jnp.logical_not(first)
                qh, doh = jnp.where(sel, q, zq), jnp.where(sel, dob, zd)
                bias_q = _pick_col(cc, 2 * hp + hh) - jnp.max(jnp.where(sel, lse_blk, NEG), axis=1, keepdims=True)
                ck = jnp.concatenate([_pick_row(cr_ref[j], 2 * hp + hh) for j in range(tile + 1)], axis=1)[:, :klen]
                p = jnp.exp(jnp.where(allowed, _bdot(qh, k, _NT) + (bias_q - ck), NEG))
                dp = _bdot(doh, v_ref[0:klen, :], _NT)
                ds = p * (dp - jnp.sum(p * dp, axis=1, keepdims=True))
                dv_acc[0:klen, :] += _bdot(p, doh, _TN)
                dk_acc[0:klen, :] += _bdot(ds, qh, _TN)
                dcs = jnp.sum(ds, axis=0, keepdims=True)
                for j in range(tile + 1):
                    n = min(T, klen - j * T)
                    dc_ref[0, j, hh:hh + 1, 0:n] -= dcs[:, j * T:j * T + n]
                dq = dq + _bdot(ds, jnp.where(sel, k, zk), _NN)
            dq_raw, dgq = _pair_norm_bwd(dq * scale, q_raw, rq, gq_ref[...], first)
            dq_ref[rows, :] = dq_raw.astype(dq_ref.dtype)
            dg_ref[0, 0:1, :] += dgq

        for tile in range(nq):
            @pl.when(i == tile)
            def _(tile=tile):
                for half in range(T // H):
                    backprop(tile, half)

        @pl.when(i == nq - 1)
        def _():
            k_raw = kraw_ref[...]
            rk = _pair_norm(k_raw, gk_ref[...], first)[1]
            dk_raw, dgk = _pair_norm_bwd(dk_acc[...], k_raw, rk, gk_ref[...], first)
            dk_ref[...] = dk_raw.astype(dk_ref.dtype)
            dv_ref[...] = dv_acc[...].astype(dv_ref.dtype)
            dg_ref[0, 1:2, :] = dgk

    gain = pl.BlockSpec((1, LANES), lambda hp, i: (0, 0))
    band = SDS((S, ATT_HEADS * ATT_D), BF16)
    return pl.pallas_call(
        body, name="fox_bwd", grid=(nhp, nq),
        in_specs=[pl.BlockSpec((T, LANES), lambda hp, i: (i, qb0 + hp)), pl.BlockSpec((S, LANES), lambda hp, i: (0, kb0 + hp)),
                  pl.BlockSpec((S, LANES), lambda hp, i: (0, vb0 + hp)), gain, gain,
                  pl.BlockSpec((T, 16), lambda hp, i: (i, 0)), pl.BlockSpec((nq, 16, T), lambda hp, i: (0, 0, 0)),
                  pl.BlockSpec((T, LANES), lambda hp, i: (i, hp)), pl.BlockSpec((T, LANES), lambda hp, i: (i, db0 + hp))],
        out_specs=[pl.BlockSpec((T, LANES), lambda hp, i: (i, hp)), pl.BlockSpec((S, LANES), lambda hp, i: (0, hp)),
                   pl.BlockSpec((S, LANES), lambda hp, i: (0, hp)), pl.BlockSpec((1, nq, 8, T), lambda hp, i: (hp, 0, 0, 0)),
                   pl.BlockSpec((1, 8, LANES), lambda hp, i: (hp, 0, 0))],
        out_shape=[band, band, band, SDS((nhp, nq, 8, T), F32), SDS((nhp, 8, LANES), F32)],
        scratch_shapes=[pltpu.VMEM((S, LANES), BF16), pltpu.VMEM((S, LANES), F32), pltpu.VMEM((S, LANES), F32)],
        compiler_params=_cparams(dimension_semantics=("arbitrary", "arbitrary")),
    )(src, src, src, gq2, gk2, cum_col, cum_row3, lse, dsrc)


def _fold_gains(dg):
    def body(d_ref, o_ref):
        t = d_ref[0]
        for h in range(1, dg.shape[0]):
            t = t + d_ref[h]
        o_ref[...] = t + pltpu.roll(t, ATT_D, axis=1)

    return pl.pallas_call(body, name="fold_gains", out_shape=SDS(dg.shape[1:], F32), compiler_params=_cparams())(dg)


def _adamw_math(w, g, m, v):
    m = ADAM_B1 * m + (1.0 - ADAM_B1) * g
    v = ADAM_B2 * v + (1.0 - ADAM_B2) * jnp.square(g)
    m_hat = m / (1.0 - ADAM_B1 ** ADAM_STEP)
    v_hat = v / (1.0 - ADAM_B2 ** ADAM_STEP)
    delta = -ADAM_LR * (m_hat / (jnp.sqrt(v_hat) + ADAM_EPS) + ADAM_WD * w)
    return delta, m, v


def _reduce_adamw(parts, w, m, v, *, tr, name, tc=None):
    R, C = w.shape
    tr, tc = min(tr, R), tc or C
    nparts = parts.shape[0]

    def body(p_ref, w_ref, m_ref, v_ref, g_ref, d_ref, nm_ref, nv_ref):
        g = p_ref[0].astype(F32)
        for s in range(1, nparts):
            g = g + p_ref[s].astype(F32)
        g_ref[...] = g
        d_ref[...], nm_ref[...], nv_ref[...] = _adamw_math(w_ref[...], g, m_ref[...], v_ref[...])

    blk = pl.BlockSpec((tr, tc), lambda i, j: (i, j))
    return pl.pallas_call(
        body, name=name, grid=(R // tr, C // tc),
        in_specs=[pl.BlockSpec((nparts, tr, tc), lambda i, j: (0, i, j)), blk, blk, blk], out_specs=[blk] * 4,
        out_shape=[SDS((R, C), F32)] * 4, compiler_params=_cparams(dimension_semantics=("parallel", "parallel")),
    )(parts, w, m, v)


def _adamw(w, g, m, v, *, name):
    def body(w_ref, g_ref, m_ref, v_ref, d_ref, nm_ref, nv_ref):
        d_ref[...], nm_ref[...], nv_ref[...] = _adamw_math(w_ref[...], g_ref[...], m_ref[...], v_ref[...])

    return pl.pallas_call(body, name=name, out_shape=[SDS(w.shape, F32)] * 3, compiler_params=_cparams())(w, g, m, v)


def _peers():
    x, y, c = lax.axis_index("x"), lax.axis_index("y"), lax.axis_index("c")
    out = []
    for k in range(1, N_DEV):
        px, py, pc = x ^ ((k >> 2) & 1), y ^ ((k >> 1) & 1), c ^ (k & 1)
        out.append(((px, py, pc), 4 * px + 2 * py + pc))
    return 4 * x + 2 * y + c, out


_HBM = pl.BlockSpec(memory_space=pltpu.HBM)
_SEM = pl.BlockSpec(memory_space=pltpu.SEMAPHORE)
_DATAFLOW = pltpu.SideEffectType.DATAFLOW_SIDE_EFFECTING


NEAR = (1, 2, 4, 6)


def _plan_peers(scatter, ks=tuple(range(1, N_DEV))):
    return lambda me, peers: [(peers[k - 1][0], peers[k - 1][1] if scatter else None, me, k - 1) for k in ks]


def _plan_relay(me, peers):
    return [(peers[0][0], peers[k - 1][1], peers[k - 1][1], j) for j, k in enumerate((2, 4, 6))]


def _plan_pair(me, peers):
    return [(peers[0][0], peers[k - 1][1], j, j) for j, k in enumerate((1, 3, 5, 7))]


def _plan_chips(me, peers):
    return [(peers[k - 1][0], k // 2, k // 2, k // 2) for k in (2, 4, 6)]


def _copy(src, dst, c, send_sems, recv_sems):
    dev, s_slot, d_slot, i = c
    return pltpu.make_async_remote_copy(
        src_ref=src if s_slot is None else src.at[s_slot], dst_ref=dst.at[d_slot], send_sem=send_sems.at[i],
        recv_sem=recv_sems.at[i], device_id=dev, device_id_type=MESH)


def _copies_start(items, *, name):
    n = len(items)
    bufs = [it[0] for it in items] + [it[1] for it in items if it[1] is not None]
    nb = len(bufs)

    def body(*refs):
        srcs, extra, sems, token = refs[:n], iter(refs[n:nb]), refs[nb:nb + 2 * n], refs[-1]
        me, peers = _peers()
        for a, (_, land, plan) in enumerate(items):
            dst = srcs[a] if land is None else next(extra)
            for c in plan(me, peers):
                _copy(srcs[a], dst, c, sems[2 * a], sems[2 * a + 1]).start()
        token[...] = jnp.zeros_like(token)

    res = pl.pallas_call(
        body, name=name,
        out_shape=([pltpu.SemaphoreType.DMA((N_DEV - 1,))] * (2 * n) + [pltpu.HBM(b.shape, b.dtype) for b in bufs]
                   + [SDS((8, LANES), F32)]),
        in_specs=[_HBM] * nb, out_specs=[_SEM] * (2 * n) + [_HBM] * nb + [pl.BlockSpec(memory_space=pltpu.VMEM)],
        input_output_aliases={i: 2 * n + i for i in range(nb)},
        compiler_params=pltpu.CompilerParams(has_side_effects=_DATAFLOW),
    )(*[pltpu.with_memory_space_constraint(b, pltpu.HBM) for b in bufs])
    sems, thru, token = res[:2 * n], list(res[2 * n:2 * n + nb]), res[-1]
    extra = iter(thru[n:])
    return [(thru[a], None if it[1] is None else next(extra), sems[2 * a], sems[2 * a + 1], it[2])
            for a, it in enumerate(items)], token


def _copies_wait(handles, after, *, name):
    n = len(handles)
    after = list(after) if isinstance(after, (list, tuple)) else [after]
    bufs = [h[0] for h in handles] + [h[1] for h in handles if h[1] is not None]
    nb = len(bufs)

    def body(*refs):
        srcs, extra, sems = refs[:n], iter(refs[n:nb]), refs[nb:nb + 2 * n]
        me, peers = _peers()
        for a, h in enumerate(handles):
            dst = srcs[a] if h[1] is None else next(extra)
            for c in h[4](me, peers):
                cp = _copy(srcs[a], dst, c, sems[2 * a], sems[2 * a + 1])
                cp.wait_send()
                cp.wait_recv()

    flat_sems = [s for h in handles for s in (h[2], h[3])]
    res = pl.pallas_call(
        body, name=name, out_shape=[pltpu.HBM(b.shape, b.dtype) for b in bufs],
        in_specs=[_HBM] * nb + [_SEM] * (2 * n) + [pl.BlockSpec(memory_space=pl.ANY)] * len(after), out_specs=[_HBM] * nb,
        input_output_aliases={i: i for i in range(nb)},
        compiler_params=pltpu.CompilerParams(has_side_effects=_DATAFLOW),
    )(*bufs, *flat_sems, *after)
    extra = iter(res[n:])
    return [(res[a], res[a] if h[1] is None else next(extra)) for a, h in enumerate(handles)]


def _exchange_start(arrays, *, scatter, name, near=()):
    items = []
    for a, arr in enumerate(arrays):
        land = lax.empty(arr.shape if scatter else (N_DEV,) + arr.shape, arr.dtype)
        items.append((arr, land, _plan_peers(scatter, NEAR) if a in near else _plan_peers(scatter)))
    return _copies_start(items, name=name)


MOVE_ROWS, MOVE_SLOTS = 512, 3


def _move_rows(src, moves, rows, *, name):
    C = src.shape[1]
    covered = max(dst + n for _, n, dst in moves)
    tail = rows - covered
    assert sum(n for _, n, _ in moves) == covered
    chunks = [(lo + o, min(MOVE_ROWS, n - o), dst + o) for lo, n, dst in moves for o in range(0, n, MOVE_ROWS)]
    nch = len(chunks)

    def body(src_ref, o_ref, buf, sin, sout, *zero):
        def fetch(i):
            lo, n, _ = chunks[i]
            return pltpu.make_async_copy(src_ref.at[pl.ds(lo, n)], buf.at[i % MOVE_SLOTS, pl.ds(0, n)], sin.at[i % MOVE_SLOTS])

        def store(i):
            _, n, dst = chunks[i]
            return pltpu.make_async_copy(buf.at[i % MOVE_SLOTS, pl.ds(0, n)], o_ref.at[pl.ds(dst, n)], sout.at[i % MOVE_SLOTS])

        if tail:
            zero[0][...] = jnp.zeros_like(zero[0])
            fill = pltpu.make_async_copy(zero[0], o_ref.at[pl.ds(covered, tail)], zero[1])
            fill.start()
        for i in range(nch):
            if i >= MOVE_SLOTS:
                store(i - MOVE_SLOTS).wait()
            fetch(i).start()
            if i >= 1:
                fetch(i - 1).wait()
                store(i - 1).start()
        fetch(nch - 1).wait()
        store(nch - 1).start()
        for i in range(max(0, nch - MOVE_SLOTS), nch):
            store(i).wait()
        if tail:
            fill.wait()

    anyspec = pl.BlockSpec(memory_space=pl.ANY)
    dma = pltpu.SemaphoreType.DMA
    return pl.pallas_call(
        body, name=name, in_specs=[anyspec], out_specs=anyspec, out_shape=SDS((rows, C), src.dtype),
        scratch_shapes=([pltpu.VMEM((MOVE_SLOTS, MOVE_ROWS, C), src.dtype), dma((MOVE_SLOTS,)), dma((MOVE_SLOTS,))]
                        + ([pltpu.VMEM((tail, C), src.dtype), dma] if tail else [])),
        compiler_params=_cparams())(src)


def _pair_sum(a, b, *, name):
    n, R, C = a.shape
    tc = 256

    def body(a_ref, b_ref, o_ref):
        o_ref[...] = (a_ref[...].astype(F32) + b_ref[...].astype(F32)).astype(o_ref.dtype)

    blk = pl.BlockSpec((1, R, tc), lambda i, j: (i, 0, j))
    return pl.pallas_call(body, name=name, grid=(n, C // tc), in_specs=[blk, blk], out_specs=blk,
                          out_shape=SDS(a.shape, a.dtype), compiler_params=_cparams(dimension_semantics=("parallel", "parallel")))(a, b)


def _own_slot(landed, own, me):
    return lax.dynamic_update_slice(landed, own[None], (me,) + (0,) * own.ndim)


SMALL = (("g_mix", 1024), ("conv_w", 6144), ("conv_b", 1536), ("dt_bias", 16), ("a_log", 16), ("d_skip", 16),
         ("ssm_norm_w", 1024), ("g_q", 64), ("g_k", 64), ("f_bias", 16), ("g_xattn", 1024), ("g_mem", 1024),
         ("xg_q", 256), ("xg_k", 256), ("g_mlp", 1024), ("loss", 1))
NOT_PARAMS = ("conv_w", "loss")
SLAB_ROWS = 112
BIG = ("w_in", "w_out", "xq_w", "xkv_w", "xo_w", "w_up", "w_down")
WEIGHTS = ("g_mix", "w_in", "conv_w", "conv_b", "dt_bias", "a_log", "d_skip", "ssm_norm_w", "g_q", "g_k", "f_bias", "w_out",
           "g_xattn", "g_mem", "xq_w", "xkv_w", "xg_q", "xg_k", "xo_w", "g_mlp", "w_up", "w_down")
O_Z, O_XS, O_B, O_C, O_DT, O_Q, O_K, O_V, O_F, O_END = 0, 1024, 2048, 2304, 2560, 2576, 3600, 4624, 5648, 5664
IN_ROW_MOVES = ((O_Z, O_B - O_Z, C_Z), (O_Q, O_F - O_Q, C_Q), (O_B, O_Q - O_B, C_B), (O_F, O_END - O_F, C_DTF + 16))


def _pack_small(vals):
    rows = []
    for name, size in SMALL:
        flat = vals[name].reshape(-1).astype(F32)
        pad = -size % LANES
        rows.append(jnp.pad(flat, (0, pad)).reshape(-1, LANES))
    slab = jnp.concatenate(rows, axis=0)
    return jnp.pad(slab, ((0, SLAB_ROWS - slab.shape[0]), (0, 0)))


def _unpack_small(slab):
    out, r = {}, 0
    for name, size in SMALL:
        nr = -(-size // LANES)
        out[name] = slab[r:r + nr].reshape(-1)[:size]
        r += nr
    return out


def _step(p, m, v, x, mem, target):
    S = x.shape[0]
    TM = 256
    me = 4 * lax.axis_index("x") + 2 * lax.axis_index("y") + lax.axis_index("c")

    def rms(u, g, name):
        return _rw_fwd(_rms_fn, [_whole(u)], [_whole(g)], [(D_MODEL, BF16)], tm=TM, name=name)[0]

    def pin(param, token):
        return param + token[0:1, 0:1]

    def landed_with_own(pairs, scatter):
        out = []
        for src, land in pairs:
            own = lax.dynamic_index_in_dim(src, me, 0, keepdims=False) if scatter else src
            out.append(_own_slot(land, own, me))
        return out

    w_in_own, m_in_own, v_in_own = p["w_in"].T, m["w_in"].T, v["w_in"].T
    ag, ag_token = _exchange_start([w_in_own.astype(BF16), p["conv_w"]] + [p[n].astype(BF16) for n in BIG[1:]],
                                   scatter=False, name="allgather_start", near=(0, 2, 3, 4, 5, 6, 7))
    h1 = rms(x, pin(p["g_mix"], ag_token), "rms_mix")
    stand_in = {"conv_w": jnp.zeros((4, 1536), F32), "loss": jnp.zeros((1,), F32)}
    slabs = [_pack_small({**d, **stand_in}) for d in (p, m, v)]
    (win_src, win_land), convw_pair = _copies_wait(ag[:2], [h1, w_in_own, m_in_own, v_in_own] + slabs,
                                                   name="allgather_wait_in")
    relay, token = _copies_start([(win_land, None, _plan_relay)], name="allgather_relay_start")
    win_land = _copies_wait(relay, token, name="allgather_relay_wait")[0][1]
    win_g, convw_g = landed_with_own([(win_src, win_land), convw_pair], False)
    w_in_o = win_g.reshape(O_END, D_MODEL)
    w_in_t = _move_rows(w_in_o, IN_ROW_MOVES, P_COLS, name="w_in_rows")
    conv_w = convw_g.transpose(1, 0, 2).reshape(4, 1536)
    cw_xs, cw_bc = conv_w[:, :1024], conv_w[:, 1024:]
    cb_xs, cb_bc = p["conv_b"][:, :1024], p["conv_b"][:, 1024:]
    dt_bias, a_log, f_bias = p["dt_bias"].reshape(16, 1), p["a_log"].reshape(16, 1), p["f_bias"].reshape(16, 1)

    proj = _matmul(h1, w_in_t, mode="nt", tm=1024, tn=640, tk=1024, name="mm_in")
    xs_c = _conv_fwd(proj, C_XS, 1024, cw_xs, cb_xs, name="conv_xs")
    bc_c = _conv_fwd(proj, C_B, 512, cw_bc, cb_bc, name="conv_bc")
    dtf_t = proj[:, C_DTF:C_DTF + 32].T
    dt_t, acs_t, cum_t = _dtf_fwd(dtf_t, dt_bias, a_log, f_bias)
    dt_col, acs_col, cum_col = dt_t.T, acs_t.T, cum_t.T
    cum_row3 = cum_t.reshape(16, S // ATT_T, ATT_T).transpose(1, 0, 2)
    y_ssd, hs = _ssd_fwd(xs_c, dt_col, acs_col, acs_t, bc_c)
    gate_rows = [_whole(y_ssd), _whole(xs_c), (proj, C_Z, 1024)]
    gate_pars = [_whole(p["d_skip"]), _whole(p["ssm_norm_w"])]
    y_ssm = _rw_fwd(_gate_fn, gate_rows, gate_pars, [(1024, BF16)], tm=TM, name="gate")[0]
    gq2, gk2 = jnp.tile(p["g_q"], (1, 2)), jnp.tile(p["g_k"], (1, 2))
    o, lse = _fox_fwd(proj, C_Q, C_K, C_V, gq2, gk2, cum_col, cum_row3)
    mixed = jnp.concatenate([y_ssm, o], axis=1)
    arrived = _copies_wait(ag[2:], mixed, name="allgather_wait_rest")
    relay, token = _copies_start([(land, None, _plan_relay) for _, land in arrived], name="allgather_relay_rest_start")
    wout_g, = landed_with_own([(arrived[0][0], _copies_wait(relay[:1], token, name="allgather_relay_out_wait")[0][1])], False)
    w_out = wout_g.reshape(2 * D_MODEL, D_MODEL)
    x1, h2 = _matmul(mixed, w_out, mode="nn", tm=1024, tn=1024, tk=2048, name="mm_out", extras=(x,),
                     row_params=(p["g_xattn"],), epilogue=_ep_residual_rms, out_dtypes=[F32, BF16])
    relayed = _copies_wait(relay[1:], x1, name="allgather_relay_rest_wait")
    xq_g, xkv_w, xo_g, w_up, wdown_g = landed_with_own(
        [(src, land) for (src, _), (_, land) in zip(arrived[1:], relayed, strict=True)], False)
    xq_w = xq_g.reshape(D_MODEL, D_MODEL)
    xo_w = xo_g.reshape(D_MODEL, D_MODEL)
    w_down = wdown_g.reshape(4 * D_MODEL, D_MODEL)

    mem_n = rms(mem, p["g_mem"], "rms_mem")
    q2 = _matmul(h2, xq_w, mode="nn", tm=1024, tn=512, tk=1024, name="mm_xq")
    kv = _matmul(mem_n, xkv_w, mode="nn", b_shards=True, tm=256, tn=256, tk=1024, name="mm_xkv")
    xa_rows = [(q2, X_D * h, X_D) for h in range(X_HEADS)]
    xa_pars = ([(kv, X_D * h, X_D) for h in range(X_HEADS)] + [(kv, D_MODEL + X_D * h, X_D) for h in range(X_HEADS)]
               + [_whole(p["xg_q"]), _whole(p["xg_k"])])
    o2 = _rw_fwd(_xattn_fn, xa_rows, xa_pars, [(D_MODEL, BF16)], tm=2 * TM, name="xattn")[0]
    x2, h3 = _matmul(o2, xo_w, mode="nn", tm=1024, tn=1024, tk=1024, name="mm_xo", extras=(x1,),
                     row_params=(p["g_mlp"],), epilogue=_ep_residual_rms, out_dtypes=[F32, BF16])

    a, usq = _matmul(h3, w_up, mode="nn", b_shards=True, tm=2048, tn=512, tk=1024, name="mm_up", out_dtypes=[F32, BF16],
                     epilogue=lambda acc: (acc, jnp.square(jax.nn.relu(acc))))
    dy, loss_part = _matmul(usq, w_down, mode="nn", tm=1024, tn=512, tk=2048, name="mm_down", extras=(x2, target),
                            epilogue=functools.partial(_ep_loss, width=D_MODEL), sums=[(1, 1)])

    def row_shards(a):
        r, c = a.shape
        return a.reshape(N_DEV, r // N_DEV, c)

    g = {"loss": loss_part}
    g["w_down"] = _matmul(usq, dy, mode="tn", out_dtype=GRAD_WIRE, tm=1024, tn=1024, tk=1024, name="mm_d_wdown")
    da = _matmul(dy, w_down, mode="nt", tm=1024, tn=1024, tk=1024, name="mm_d_usq", out_dtype=BF16, extras=(a,),
                 epilogue=lambda acc, av: (2.0 * jax.nn.relu(av) * acc,))
    g["w_up"] = _matmul(h3, da, mode="tn", out_shards=True, out_dtype=GRAD_WIRE, tm=1024, tn=512, tk=1024, name="mm_d_wup")
    sent_mlp, token = _exchange_start([row_shards(g["w_down"]), g["w_up"]], scatter=True,
                                      name="grads_start_mlp")
    dx2, g["g_mlp"] = _matmul(da, w_up, mode="nt", b_shards=True, tm=1024, tn=1024, tk=512, name="mm_d_h3",
                              extras=(x2, dy), row_params=(pin(p["g_mlp"], token),), epilogue=_ep_rms_bwd,
                              sums=[(1, D_MODEL)])

    g["xo_w"] = _matmul(o2, dx2, mode="tn", out_dtype=GRAD_WIRE, tm=1024, tn=1024, tk=1024, name="mm_d_wxo")
    do2 = _matmul(dx2, xo_w, mode="nt", tm=1024, tn=512, tk=1024, name="mm_d_o2")
    dq2, dkv, g["xg_q"], g["xg_k"] = _rw_bwd(_xattn_fn, xa_rows, xa_pars, [_whole(do2)], tm=2 * TM, name="xattn_bwd",
                                             row_grads=[BF16] * X_HEADS, join_rows=True, join_params=2 * X_HEADS)
    g["xq_w"] = _matmul(h2, dq2, mode="tn", out_dtype=GRAD_WIRE, tm=1024, tn=1024, tk=1024, name="mm_d_wxq")
    dx1, g["g_xattn"] = _matmul(dq2, xq_w, mode="nt", tm=1024, tn=1024, tk=1024, name="mm_d_h2", extras=(x1, dx2),
                                row_params=(p["g_xattn"],), epilogue=_ep_rms_bwd, sums=[(1, D_MODEL)])
    g["xkv_w"] = _matmul(mem_n, dkv, mode="tn", out_shards=True, out_dtype=GRAD_WIRE, tm=1024, tn=256, tk=256,
                         name="mm_d_wxkv")
    dmem_n = _matmul(dkv, xkv_w, mode="nt", b_shards=True, tm=256, tn=1024, tk=256, name="mm_d_memn")
    g["g_mem"] = _rw_bwd(_rms_fn, [_whole(mem)], [_whole(p["g_mem"])], [_whole(dmem_n)], tm=TM, name="rms_mem_bwd",
                         row_grads=[None])[0]

    g["w_out"] = _matmul(mixed, dx1, mode="tn", out_dtype=GRAD_WIRE, tm=1024, tn=1024, tk=1024, name="mm_d_wout")
    sent_mid, token = _exchange_start(
        [row_shards(g["w_out"]), row_shards(g["xq_w"]), g["xkv_w"], row_shards(g["xo_w"])], scatter=True,
        name="grads_start_mid")
    dmixed = _matmul(dx1, w_out, mode="nt", tm=1024, tn=1024, tk=1024, name="mm_d_mixed")
    dq, dk, dv, dcum4, dgain = _fox_bwd(proj, C_Q, C_K, C_V, pin(gq2, token), gk2, cum_col, cum_row3, lse, dmixed, 1024)
    gains = _fold_gains(dgain)
    g["g_q"], g["g_k"] = gains[0:1, :ATT_D], gains[1:2, :ATT_D]
    dy_ssd, dxs_g, dz, g["d_skip"], g["ssm_norm_w"] = _rw_bwd(
        _gate_fn, gate_rows, gate_pars, [(dmixed, 0, 1024)], tm=TM, name="gate_bwd", row_grads=[F32, F32, BF16])
    dxs_s, ddt_col, dacs_col, dacs_row, d_b, d_c = _ssd_bwd(xs_c, dt_col, acs_col, acs_t, bc_c, hs, dy_ssd)
    dcum_t = dcum4[:, :, 0:2, :].transpose(0, 2, 1, 3).reshape(16, S)
    ddtf_t, ddtb, dalog, dfb = _dtf_bwd(dtf_t, dt_bias, a_log, f_bias, ddt_col.T, dacs_col.T, dacs_row, dcum_t)
    g["dt_bias"], g["a_log"], g["f_bias"] = ddtb, dalog, dfb
    dxs_raw, dcw_xs, dcb_xs = _conv_bwd(proj, C_XS, 1024, cw_xs, cb_xs, [dxs_s, dxs_g], name="conv_xs_bwd")
    dbc_raw, dcw_bc, dcb_bc = _conv_bwd(proj, C_B, 512, cw_bc, cb_bc, [jnp.concatenate([d_b, d_c], axis=1)],
                                        name="conv_bc_bwd")
    g["conv_w"] = jnp.concatenate([dcw_xs, dcw_bc], axis=1)
    g["conv_b"] = jnp.concatenate([dcb_xs, dcb_bc], axis=1)
    ddtf = jnp.pad(ddtf_t.T.astype(BF16), ((0, 0), (0, P_COLS - C_DTF - 32)))
    dproj = jnp.concatenate([dz, dxs_raw, dq, dk, dv, dbc_raw, ddtf], axis=1)
    dw_in_p = _matmul(dproj, h1, mode="tn", out_dtype=GRAD_WIRE, tm=640, tn=1024, tk=1024, name="mm_d_win")
    g["w_in"] = _move_rows(dw_in_p, [(dst, n, lo) for lo, n, dst in IN_ROW_MOVES], O_END, name="d_w_in_rows")
    half = N_DEV // 2
    send_in = row_shards(g["w_in"])
    pair, token = _copies_start([(send_in, lax.empty((half,) + send_in.shape[1:], send_in.dtype), _plan_pair)],
                                name="grads_in_pair_start")
    grads, delta, new_m, new_v = {}, {}, {}, {}

    def update(names, sent, after, wait_name):
        parts = landed_with_own(_copies_wait(sent, after, name=wait_name), True)
        for name, part in zip(names, parts, strict=True):
            grads[name], delta[name], new_m[name], new_v[name] = _reduce_adamw(part, p[name], m[name], v[name], tr=128,
                                                                                name="adamw_" + name)

    update(("w_down", "w_up"), sent_mlp, token, "grads_wait_mlp")
    send_in, from_sibling = _copies_wait(pair, delta["w_up"], name="grads_in_pair_wait")[0]
    mine = jnp.stack([lax.dynamic_index_in_dim(send_in, me ^ (2 * j), 0, keepdims=False) for j in range(half)])
    chip_sums = _pair_sum(mine, from_sibling, name="grads_in_pair_sum")
    sent_in, token = _copies_start([(chip_sums, lax.empty(chip_sums.shape, chip_sums.dtype), _plan_chips)],
                                   name="grads_in_chip_start")
    grad_x, g["g_mix"] = _matmul(dproj, w_in_t, mode="nn", tm=1024, tn=1024, tk=1152, name="mm_d_h1", extras=(x, dx1),
                                 row_params=(pin(p["g_mix"], token),), epilogue=_ep_rms_bwd, sums=[(1, D_MODEL)])
    sent_small, _ = _exchange_start([_pack_small(g)], scatter=False, name="small_grads_start")

    update(("w_out", "xq_w", "xkv_w", "xo_w"), sent_mid, grad_x, "grads_wait_mid")
    chip_sums, landed = _copies_wait(sent_in, delta["xo_w"], name="grads_in_chip_wait")[0]
    part = lax.dynamic_update_slice(landed, chip_sums[0:1], (0, 0, 0))
    res = _reduce_adamw(part, w_in_own, m_in_own, v_in_own, tr=part.shape[1], tc=256, name="adamw_w_in")
    grads["w_in"], delta["w_in"], new_m["w_in"], new_v["w_in"] = [r.T for r in res]
    small_parts = landed_with_own(_copies_wait(sent_small, delta["w_in"], name="small_grads_wait"), False)[0]
    sg, sd, sm, sv = _reduce_adamw(small_parts, *slabs, tr=SLAB_ROWS, name="adamw_small")
    for dst, slab in ((grads, sg), (delta, sd), (new_m, sm), (new_v, sv)):
        for name, flat in _unpack_small(slab).items():
            if name not in NOT_PARAMS:
                dst[name] = flat.reshape(p[name].shape)
    loss = _unpack_small(sg)["loss"][0]
    cw_shard = p["conv_w"].shape[1]
    grads["conv_w"] = lax.dynamic_slice(_unpack_small(sg)["conv_w"].reshape(4, 1536), (0, me * cw_shard), (4, cw_shard))
    delta["conv_w"], new_m["conv_w"], new_v["conv_w"] = _adamw(p["conv_w"], grads["conv_w"], m["conv_w"], v["conv_w"],
                                                               name="adamw_conv_w")
    return loss, grad_x, grads, delta, new_m, new_v


def kernel(x, mem, g_mix, w_in, conv_w, conv_b, dt_bias, a_log, d_skip, ssm_norm_w, g_q, g_k, f_bias, w_out, g_xattn, g_mem, xq_w, xkv_w, xg_q, xg_k, xo_w, g_mlp, w_up, w_down, loss_target, m_g_mix, m_w_in, m_conv_w, m_conv_b, m_dt_bias, m_a_log, m_d_skip, m_ssm_norm_w, m_g_q, m_g_k, m_f_bias, m_w_out, m_g_xattn, m_g_mem, m_xq_w, m_xkv_w, m_xg_q, m_xg_k, m_xo_w, m_g_mlp, m_w_up, m_w_down, v_g_mix, v_w_in, v_conv_w, v_conv_b, v_dt_bias, v_a_log, v_d_skip, v_ssm_norm_w, v_g_q, v_g_k, v_f_bias, v_w_out, v_g_xattn, v_g_mem, v_xq_w, v_xkv_w, v_xg_q, v_xg_k, v_xo_w, v_g_mlp, v_w_up, v_w_down):
    args = locals()
    drop = lambda t: t[0] if t.ndim == 3 else t
    p = {n: drop(args[n]) for n in WEIGHTS}
    m = {n: drop(args["m_" + n]) for n in WEIGHTS}
    v = {n: drop(args["v_" + n]) for n in WEIGHTS}
    loss, grad_x, grads, delta, new_m, new_v = _step(p, m, v, x[0], mem[0], loss_target[0])
    outs = [loss, grad_x[None]]
    for d in (grads, delta, new_m, new_v):
        outs += [d[n].reshape(args[n].shape) for n in WEIGHTS]
    return tuple(outs)
```

```python
import functools

import jax
import jax.numpy as jnp
from jax import lax
from jax.experimental import pallas as pl
from jax.experimental.pallas import tpu as pltpu

F32, BF16 = jnp.float32, jnp.bfloat16
SDS = jax.ShapeDtypeStruct
HI = lax.Precision.HIGHEST
MESH = pl.DeviceIdType.MESH

N_DEV = 8
EPS = 1e-5
D_MODEL = 1024
SSM_HEADS, SSM_P, SSM_N, SSM_GROUPS, CHUNK = 16, 64, 128, 2, 128
ATT_HEADS, ATT_D = 16, 64
X_HEADS, X_D = 4, 256
LANES = 128
VMEM_LIMIT = 48 * 1024 * 1024
NEG = -1e30

GRAD_WIRE = BF16
ADAM_LR, ADAM_B1, ADAM_B2, ADAM_EPS, ADAM_WD, ADAM_STEP = 0.001, 0.9, 0.999, 1e-08, 0.01, 10

C_Z, C_XS, C_Q, C_K, C_V, C_B, C_C, C_DTF, P_COLS = 0, 1024, 2048, 3072, 4096, 5120, 5376, 5632, 5760

_NN = (((1,), (0,)), ((), ()))
_NT = (((1,), (1,)), ((), ()))
_TN = (((0,), (0,)), ((), ()))


def _cparams(**kw):
    return pltpu.CompilerParams(vmem_limit_bytes=VMEM_LIMIT, **kw)


def _bdot(a, b, dn):
    return lax.dot_general(a.astype(BF16), b.astype(BF16), dn, preferred_element_type=F32)


@jax.custom_vjp
def mm_nn(a, b):
    return _bdot(a, b, _NN)


mm_nn.defvjp(lambda a, b: (mm_nn(a, b), (a, b)), lambda r, g: (_bdot(g, r[1], _NT), _bdot(r[0], g, _TN)))


@jax.custom_vjp
def mm_nt(a, b):
    return _bdot(a, b, _NT)


mm_nt.defvjp(lambda a, b: (mm_nt(a, b), (a, b)), lambda r, g: (_bdot(g, r[1], _NN), _bdot(g, r[0], _TN)))


@jax.custom_vjp
def mm_tn(a, b):
    return _bdot(a, b, _TN)


mm_tn.defvjp(lambda a, b: (mm_tn(a, b), (a, b)), lambda r, g: (_bdot(r[1], g, _NT), _bdot(r[0], g, _NN)))


def _cdot(x, c):
    return jnp.dot(x, c, precision=HI, preferred_element_type=F32)


def _iota(shape, dim):
    return lax.broadcasted_iota(jnp.int32, shape, dim)


def _matmul(a, b, *, mode, tm, tn, tk, name, out_dtype=F32, add=None, extras=(), epilogue=None, out_dtypes=None,
            b_shards=False, out_shards=False, row_params=(), sums=()):
    if mode == "tn":
        K, M = a.shape
    else:
        M, K = a.shape
    if b_shards:
        N = b.shape[1] if mode == "nt" else b.shape[0] * b.shape[2]
        tn, tk = (tn, b.shape[2]) if mode == "nt" else (b.shape[2], tk)
    else:
        N = b.shape[0] if mode == "nt" else b.shape[1]
    tm, tn, tk = min(tm, M), min(tn, N), min(tk, K)
    assert M % tm == 0 and N % tn == 0 and K % tk == 0, (name, M, N, K, tm, tn, tk)
    assert not b_shards or (K // tk if mode == "nt" else N // tn) == b.shape[0], name
    assert not (out_shards and (extras or add is not None)), name
    nk = K // tk
    dn = {"nn": _NN, "nt": _NT, "tn": _TN}[mode]
    if add is not None:
        extras, epilogue = (add,), lambda acc, r: (acc + r,)
    elif epilogue is None:
        epilogue = lambda acc: (acc,)
    out_dtypes = out_dtypes or [out_dtype]
    ne, no, ns = len(extras) + len(row_params), len(out_dtypes), len(sums)
    assert all(s == (1, 1) or (s == (1, N) and tn == N) for s in sums), name

    def body(*refs):
        a_ref, b_ref = refs[:2]
        e_refs, o_refs, s_refs = refs[2:2 + ne], refs[2 + ne:2 + ne + no], refs[2 + ne + no:2 + ne + no + ns]

        def finish(acc):
            res = epilogue(acc, *[e[...] for e in e_refs])
            for o_ref, v in zip(o_refs, res[:no], strict=True):
                o_ref[...] = v.astype(o_ref.dtype)
            first_tile = jnp.logical_and(pl.program_id(0) == 0, pl.program_id(1) == 0)
            for s_ref, v in zip(s_refs, res[no:], strict=True):
                @pl.when(first_tile)
                def _(s_ref=s_ref, v=v):
                    s_ref[...] = v

                @pl.when(jnp.logical_not(first_tile))
                def _(s_ref=s_ref, v=v):
                    s_ref[...] += v

        prod = _bdot(a_ref[...], b_ref[...], dn)
        if nk == 1:
            finish(prod)
            return
        acc_ref = refs[-1]
        k = pl.program_id(2)

        @pl.when(k == 0)
        def _():
            acc_ref[...] = prod

        @pl.when(jnp.logical_and(k > 0, k < nk - 1))
        def _():
            acc_ref[...] += prod

        @pl.when(k == nk - 1)
        def _():
            finish(acc_ref[...] + prod)

    a_spec = pl.BlockSpec((tk, tm), lambda i, j, k: (k, i)) if mode == "tn" else pl.BlockSpec((tm, tk), lambda i, j, k: (i, k))
    if b_shards and mode == "nt":
        b_spec = pl.BlockSpec((None, tn, tk), lambda i, j, k: (k, j, 0))
    elif b_shards:
        b_spec = pl.BlockSpec((None, tk, tn), lambda i, j, k: (j, k, 0))
    elif mode == "nt":
        b_spec = pl.BlockSpec((tn, tk), lambda i, j, k: (j, k))
    else:
        b_spec = pl.BlockSpec((tk, tn), lambda i, j, k: (k, j))
    if out_shards:
        o_spec, o_shape = pl.BlockSpec((None, tm, tn), lambda i, j, k: (j, i, 0)), (N // tn, M, tn)
    else:
        o_spec, o_shape = pl.BlockSpec((tm, tn), lambda i, j, k: (i, j)), (M, N)
    row_spec = pl.BlockSpec((1, tn), lambda i, j, k: (0, j))
    sum_specs = [pl.BlockSpec(s, lambda i, j, k: (0, 0)) for s in sums]
    res = pl.pallas_call(
        body, name=name, grid=(M // tm, N // tn, nk),
        in_specs=[a_spec, b_spec] + [o_spec] * len(extras) + [row_spec] * len(row_params),
        out_specs=[o_spec] * no + sum_specs, out_shape=[SDS(o_shape, dt) for dt in out_dtypes] + [SDS(s, F32) for s in sums],
        scratch_shapes=[pltpu.VMEM((tm, tn), F32)] if nk > 1 else [],
        compiler_params=_cparams(dimension_semantics=(("arbitrary",) * 3 if sums else ("parallel", "parallel", "arbitrary"))),
    )(a, b, *extras, *row_params)
    return res[0] if no + ns == 1 else res


def _row_spec(tm, spec):
    _, c0, w = spec
    assert c0 % w == 0
    return pl.BlockSpec((tm, w), functools.partial(lambda i, cb: (i, cb), cb=c0 // w))


def _par_spec(spec):
    arr, c0, w = spec
    assert c0 % w == 0
    return pl.BlockSpec((arr.shape[0], w), functools.partial(lambda i, cb: (0, cb), cb=c0 // w))


def _whole(arr):
    return (arr, 0, arr.shape[1])


def _rw_fwd(fn, rows, params, outs, *, tm, name):
    M = rows[0][0].shape[0]
    nr, npar = len(rows), len(params)

    def body(*refs):
        rv = [r[...].astype(F32) for r in refs[:nr]]
        pv = [p[...].astype(F32) for p in refs[nr:nr + npar]]
        res = fn(*rv, *pv)
        for o_ref, v in zip(refs[nr + npar:], res, strict=True):
            o_ref[...] = v.astype(o_ref.dtype)

    return pl.pallas_call(
        body, name=name, grid=(M // tm,),
        in_specs=[_row_spec(tm, r) for r in rows] + [_par_spec(p) for p in params],
        out_specs=[pl.BlockSpec((tm, w), lambda i: (i, 0)) for w, _ in outs],
        out_shape=[SDS((M, w), dt) for w, dt in outs],
        compiler_params=_cparams(dimension_semantics=("parallel",)),
    )(*[r[0] for r in rows], *[p[0] for p in params])


def _rw_bwd(fn, rows, params, cts, *, tm, name, row_grads, adds=None, join_rows=False, join_params=0):
    M = rows[0][0].shape[0]
    adds = adds or {}
    nr, npar, nc = len(rows), len(params), len(cts)
    add_keys = sorted(adds)
    want = [k for k in range(nr) if row_grads[k] is not None]

    def body(*refs):
        pos = 0
        r_refs = refs[pos:pos + nr]; pos += nr
        p_refs = refs[pos:pos + npar]; pos += npar
        c_refs = refs[pos:pos + nc]; pos += nc
        a_refs = dict(zip(add_keys, refs[pos:pos + len(add_keys)])); pos += len(add_keys)
        n_row_out = 1 if join_rows else len(want)
        dr_refs = refs[pos:pos + n_row_out]; pos += n_row_out
        dp_refs = refs[pos:]
        rv = [r[...].astype(F32) for r in r_refs]
        pv = [p[...].astype(F32) for p in p_refs]
        _, vjp = jax.vjp(fn, *rv, *pv)
        g = vjp(tuple(c[...].astype(F32) for c in c_refs))
        row_vals = []
        for k in want:
            v = g[k]
            if k in a_refs:
                v = v + a_refs[k][...].astype(F32)
            row_vals.append(v)
        if join_rows:
            row_vals = [jnp.concatenate(row_vals, axis=1)]
        for ref, v in zip(dr_refs, row_vals, strict=True):
            ref[...] = v.astype(ref.dtype)
        par_vals = list(g[nr:])
        if join_params:
            par_vals = [jnp.concatenate(par_vals[:join_params], axis=1)] + par_vals[join_params:]
        first = pl.program_id(0) == 0
        for ref, v in zip(dp_refs, par_vals, strict=True):
            @pl.when(first)
            def _(ref=ref, v=v):
                ref[...] = v

            @pl.when(jnp.logical_not(first))
            def _(ref=ref, v=v):
                ref[...] += v

    row_out = [(rows[k][2], row_grads[k]) for k in want]
    if join_rows:
        row_out = [(sum(w for w, _ in row_out), row_out[0][1])]
    par_out = [(p[0].shape[0], p[2]) for p in params]
    if join_params:
        par_out = [(par_out[0][0], sum(w for _, w in par_out[:join_params]))] + par_out[join_params:]
    res = pl.pallas_call(
        body, name=name, grid=(M // tm,),
        in_specs=([_row_spec(tm, r) for r in rows] + [_par_spec(p) for p in params] + [_row_spec(tm, c) for c in cts]
                  + [_row_spec(tm, adds[k]) for k in add_keys]),
        out_specs=([pl.BlockSpec((tm, w), lambda i: (i, 0)) for w, _ in row_out]
                   + [pl.BlockSpec(s, lambda i: (0, 0)) for s in par_out]),
        out_shape=[SDS((M, w), dt) for w, dt in row_out] + [SDS(s, F32) for s in par_out],
        compiler_params=_cparams(dimension_semantics=("arbitrary",)),
    )(*[r[0] for r in rows], *[p[0] for p in params], *[c[0] for c in cts], *[adds[k][0] for k in add_keys])
    return res


def _rms_fn(x, g):
    r = lax.rsqrt(jnp.mean(x * x, axis=-1, keepdims=True) + EPS)
    return (x * r * g,)


def _ep_residual_rms(acc, res, g):
    x = acc + res
    return x, _rms_fn(x, g)[0]


def _ep_rms_bwd(dh, x, dres, g):
    r = lax.rsqrt(jnp.mean(x * x, axis=-1, keepdims=True) + EPS)
    t = dh * g
    dx = dres + r * (t - x * (r * r) * jnp.mean(t * x, axis=-1, keepdims=True))
    return dx, jnp.sum(dh * x * r, axis=0, keepdims=True)


def _ep_loss(acc, res, target, *, width):
    e = acc + res - target
    return e * (1.0 / width), jnp.sum(jnp.sum(e * e, axis=1, keepdims=True), axis=0, keepdims=True) * (0.5 / width)


def _seg_mats(width, seg):
    n = width // seg
    p = (_iota((width, n), 0) // seg == _iota((width, n), 1)).astype(F32)
    e = (_iota((n, width), 1) // seg == _iota((n, width), 0)).astype(F32)
    return p, e


def _gate_fn(y, xs, z, dskip, w):
    width = SSM_HEADS * SSM_P
    _, e = _seg_mats(width, SSM_P)
    y = (y + _cdot(dskip, e) * xs) * (z * jax.nn.sigmoid(z))
    g0 = _iota((1, width), 1) < width // SSM_GROUPS
    y2 = y * y
    gw = width // SSM_GROUPS
    ms0 = jnp.sum(jnp.where(g0, y2, 0.0), axis=-1, keepdims=True) * (1.0 / gw)
    ms1 = jnp.sum(jnp.where(g0, 0.0, y2), axis=-1, keepdims=True) * (1.0 / gw)
    r = jnp.where(g0, lax.rsqrt(ms0 + EPS), lax.rsqrt(ms1 + EPS))
    return (y * r * w,)


def _xattn_fn(q0, q1, q2, q3, k0, k1, k2, k3, v0, v1, v2, v3, gq, gk):
    def norm(u, g):
        return u * lax.rsqrt(jnp.mean(u * u, axis=-1, keepdims=True) + EPS) * g
    outs = []
    for q, k, v in ((q0, k0, v0), (q1, k1, v1), (q2, k2, v2), (q3, k3, v3)):
        s = mm_nt(norm(q, gq), norm(k, gk)) * (X_D ** -0.5)
        p = jnp.exp(s - lax.stop_gradient(jnp.max(s, axis=-1, keepdims=True)))
        p = p / jnp.sum(p, axis=-1, keepdims=True)
        outs.append(mm_nn(p, v))
    return (jnp.concatenate(outs, axis=-1),)


CONV_TC = 256


def _shift_down(u, k):
    if k == 0:
        return u
    return jnp.where(_iota(u.shape, 0) >= k, pltpu.roll(u, k, axis=0), 0.0)


def _shift_up(u, k):
    if k == 0:
        return u
    n = u.shape[0]
    return jnp.where(_iota(u.shape, 0) < n - k, pltpu.roll(u, n - k, axis=0), 0.0)


def _conv_pre(u, w_ref, b):
    pre = b + w_ref[3:4, :] * u
    for k in (1, 2, 3):
        pre = pre + w_ref[3 - k:4 - k, :] * _shift_down(u, k)
    return pre


def _conv_fwd(src, c0, width, w, b, *, name):
    S = src.shape[0]
    cb0 = c0 // CONV_TC

    def body(u_ref, w_ref, b_ref, o_ref):
        pre = _conv_pre(u_ref[...], w_ref, b_ref[...])
        o_ref[...] = pre * jax.nn.sigmoid(pre)

    return pl.pallas_call(
        body, name=name, grid=(width // CONV_TC,),
        in_specs=[pl.BlockSpec((S, CONV_TC), lambda j: (0, cb0 + j)), pl.BlockSpec((4, CONV_TC), lambda j: (0, j)),
                  pl.BlockSpec((1, CONV_TC), lambda j: (0, j))],
        out_specs=pl.BlockSpec((S, CONV_TC), lambda j: (0, j)), out_shape=SDS((S, width), F32),
        compiler_params=_cparams(dimension_semantics=("parallel",)),
    )(src, w, b)


def _conv_bwd(src, c0, width, w, b, douts, *, name):
    S = src.shape[0]
    cb0 = c0 // CONV_TC
    nd = len(douts)

    def body(*refs):
        u_ref, w_ref, b_ref = refs[:3]
        d_refs = refs[3:3 + nd]
        du_ref, dw_ref, db_ref = refs[3 + nd:]
        u = u_ref[...]
        pre = _conv_pre(u, w_ref, b_ref[...])
        sg = jax.nn.sigmoid(pre)
        dout = d_refs[0][...]
        for r in d_refs[1:]:
            dout = dout + r[...]
        dpre = dout * (sg * (1.0 + pre * (1.0 - sg)))
        du = w_ref[3:4, :] * dpre
        dw_ref[3:4, :] = jnp.sum(dpre * u, axis=0, keepdims=True)
        for k in (1, 2, 3):
            du = du + w_ref[3 - k:4 - k, :] * _shift_up(dpre, k)
            dw_ref[3 - k:4 - k, :] = jnp.sum(dpre * _shift_down(u, k), axis=0, keepdims=True)
        du_ref[...] = du.astype(du_ref.dtype)
        db_ref[...] = jnp.sum(dpre, axis=0, keepdims=True)

    return pl.pallas_call(
        body, name=name, grid=(width // CONV_TC,),
        in_specs=[pl.BlockSpec((S, CONV_TC), lambda j: (0, cb0 + j)), pl.BlockSpec((4, CONV_TC), lambda j: (0, j)),
                  pl.BlockSpec((1, CONV_TC), lambda j: (0, j))] + [pl.BlockSpec((S, CONV_TC), lambda j: (0, j))] * nd,
        out_specs=[pl.BlockSpec((S, CONV_TC), lambda j: (0, j)), pl.BlockSpec((4, CONV_TC), lambda j: (0, j)),
                   pl.BlockSpec((1, CONV_TC), lambda j: (0, j))],
        out_shape=[SDS((S, width), BF16), SDS((4, width), F32), SDS((1, width), F32)],
        compiler_params=_cparams(dimension_semantics=("parallel",)),
    )(src, w, b, *douts)


def _softplus(x):
    return jnp.maximum(x, 0.0) + jnp.log(1.0 + jnp.exp(-jnp.abs(x)))


def _prefix_sum(x, seg):
    n = x.shape[1]
    pos = _iota(x.shape, 1) % seg
    k = 1
    while k < seg:
        x = x + jnp.where(pos >= k, pltpu.roll(x, k, axis=1), 0.0)
        k *= 2
    return x


def _suffix_sum(x, seg):
    n = x.shape[1]
    pos = _iota(x.shape, 1) % seg
    k = 1
    while k < seg:
        x = x + jnp.where(pos + k < seg, pltpu.roll(x, n - k, axis=1), 0.0)
        k *= 2
    return x


def _dtf_fwd(dtf_t, dt_bias, a_log, f_bias):
    S = dtf_t.shape[1]

    def body(x_ref, db_ref, al_ref, fb_ref, dt_ref, acs_ref, cum_ref):
        dt = _softplus(x_ref[0:16, :] + db_ref[...])
        dt_ref[...] = dt
        acs_ref[...] = _prefix_sum(dt * (-jnp.exp(al_ref[...])), CHUNK)
        cum_ref[...] = _prefix_sum(-_softplus(-(x_ref[16:32, :] + fb_ref[...])), S)

    return pl.pallas_call(body, name="dtf_fwd", out_shape=[SDS((16, S), F32)] * 3, compiler_params=_cparams())(
        dtf_t, dt_bias, a_log, f_bias)


def _dtf_bwd(dtf_t, dt_bias, a_log, f_bias, d_dt, d_acs_a, d_acs_b, d_cum):
    S = dtf_t.shape[1]

    def body(x_ref, db_ref, al_ref, fb_ref, ddt_ref, da1_ref, da2_ref, dc_ref, dx_ref, ddb_ref, dal_ref, dfb_ref):
        xd = x_ref[0:16, :] + db_ref[...]
        dt = _softplus(xd)
        a = -jnp.exp(al_ref[...])
        d_da = _suffix_sum(da1_ref[...] + da2_ref[...], CHUNK)
        d_dt = ddt_ref[...] + d_da * a
        dal_ref[...] = jnp.sum(d_da * dt, axis=1, keepdims=True) * a
        d_xd = d_dt * jax.nn.sigmoid(xd)
        ddb_ref[...] = jnp.sum(d_xd, axis=1, keepdims=True)
        xf = x_ref[16:32, :] + fb_ref[...]
        d_xf = _suffix_sum(dc_ref[...], S) * jax.nn.sigmoid(-xf)
        dfb_ref[...] = jnp.sum(d_xf, axis=1, keepdims=True)
        dx_ref[0:16, :] = d_xd
        dx_ref[16:32, :] = d_xf

    return pl.pallas_call(body, name="dtf_bwd", out_shape=[SDS((32, S), F32)] + [SDS((16, 1), F32)] * 3,
                          compiler_params=_cparams())(dtf_t, dt_bias, a_log, f_bias, d_dt, d_acs_a, d_acs_b, d_cum)


SSM_PAIRS = SSM_HEADS // 2 // SSM_GROUPS


def _ssd_pair(xs, dtc, acol, arow, bm, cm, cbm, h, hp):
    L = CHUNK
    first = _iota((1, LANES), 1) < SSM_P
    i16, s16 = _iota((L, 16), 1), _iota((16, L), 0)
    ha, hb = 2 * hp, 2 * hp + 1

    def selc(blk, hh):
        return jnp.sum(jnp.where(i16 == hh, blk, 0.0), axis=1, keepdims=True)

    def selr(blk, hh):
        return jnp.sum(jnp.where(s16 == hh, blk, 0.0), axis=0, keepdims=True)

    x = xs * jnp.where(first, selc(dtc, ha), selc(dtc, hb))
    ca, cb, ra, rb = selc(acol, ha), selc(acol, hb), selr(arow, ha), selr(arow, hb)
    tri = _iota((L, L), 0) >= _iota((L, L), 1)
    la = jnp.exp(jnp.where(tri, ca - ra, NEG))
    lb = jnp.exp(jnp.where(tri, cb - rb, NEG))
    y = jnp.where(first, mm_nn(cbm * la, x), mm_nn(cbm * lb, x))
    y = y + jnp.where(first, jnp.exp(ca), jnp.exp(cb)) * mm_nn(cm, h)
    last = _iota((1, L), 1) == L - 1
    ala = jnp.sum(jnp.where(last, ra, 0.0), axis=1, keepdims=True)
    alb = jnp.sum(jnp.where(last, rb, 0.0), axis=1, keepdims=True)
    dec = jnp.where(first, jnp.exp(ala - ca), jnp.exp(alb - cb))
    hn = jnp.where(first, jnp.exp(ala), jnp.exp(alb)) * h + mm_tn(bm, x * dec)
    return y, hn


def _ssd_group(*args, grp):
    xs, (dtc, acol, arow, bm, cm), hs = args[:SSM_PAIRS], args[SSM_PAIRS:SSM_PAIRS + 5], args[SSM_PAIRS + 5:]
    cbm = mm_nt(cm, bm)
    res = [_ssd_pair(xs[j], dtc, acol, arow, bm, cm, cbm, hs[j], SSM_PAIRS * grp + j) for j in range(SSM_PAIRS)]
    return tuple(r[0] for r in res) + tuple(r[1] for r in res)


def _ssd_specs(nc, rev):
    L = CHUNK
    cidx = (lambda c: nc - 1 - c) if rev else (lambda c: c)
    return dict(
        xs=pl.BlockSpec((L, SSM_HEADS * SSM_P), lambda c: (cidx(c), 0)),
        col=pl.BlockSpec((L, 16), lambda c: (cidx(c), 0)),
        row=pl.BlockSpec((16, L), lambda c: (0, cidx(c))),
        bc=pl.BlockSpec((L, 2 * SSM_GROUPS * SSM_N), lambda c: (cidx(c), 0)),
        grp=pl.BlockSpec((L, SSM_GROUPS * SSM_N), lambda c: (cidx(c), 0)),
        st=pl.BlockSpec((1, SSM_HEADS // 2, SSM_N, LANES), lambda c: (cidx(c), 0, 0, 0)),
    )


def _lane_pieces(v):
    return [v[:, LANES * j:LANES * (j + 1)] for j in range(v.shape[1] // LANES)]


def _ssd_fwd(xs, dt_col, acs_col, acs_row, bc):
    S = xs.shape[0]
    nc, nhp = S // CHUNK, SSM_HEADS // 2
    sp = _ssd_specs(nc, False)

    def body(xs_ref, dt_ref, ac_ref, ar_ref, bc_ref, y_ref, hs_ref, h_scr):
        @pl.when(pl.program_id(0) == 0)
        def _():
            h_scr[...] = jnp.zeros_like(h_scr)

        xp, bcp = _lane_pieces(xs_ref[...]), _lane_pieces(bc_ref[...])
        ys = []
        for g in range(SSM_GROUPS):
            hs = [h_scr[SSM_PAIRS * g + j] for j in range(SSM_PAIRS)]
            for j in range(SSM_PAIRS):
                hs_ref[0, SSM_PAIRS * g + j] = hs[j]
            res = _ssd_group(*xp[SSM_PAIRS * g:SSM_PAIRS * (g + 1)], dt_ref[...], ac_ref[...], ar_ref[...], bcp[g],
                             bcp[SSM_GROUPS + g], *hs, grp=g)
            ys += res[:SSM_PAIRS]
            for j in range(SSM_PAIRS):
                h_scr[SSM_PAIRS * g + j] = res[SSM_PAIRS + j]
        y_ref[...] = jnp.concatenate(ys, axis=1)

    return pl.pallas_call(
        body, name="ssd_fwd", grid=(nc,),
        in_specs=[sp["xs"], sp["col"], sp["col"], sp["row"], sp["bc"]],
        out_specs=[sp["xs"], sp["st"]],
        out_shape=[SDS((S, SSM_HEADS * SSM_P), F32), SDS((nc, nhp, SSM_N, LANES), F32)],
        scratch_shapes=[pltpu.VMEM((nhp, SSM_N, LANES), F32)],
        compiler_params=_cparams(dimension_semantics=("arbitrary",)),
    )(xs, dt_col, acs_col, acs_row, bc)


def _ssd_bwd(xs, dt_col, acs_col, acs_row, bc, hs, dy):
    S = xs.shape[0]
    nc, nhp = S // CHUNK, SSM_HEADS // 2
    sp = _ssd_specs(nc, True)

    def body(xs_ref, dt_ref, ac_ref, ar_ref, bc_ref, hs_ref, dy_ref,
             dxs_ref, ddt_ref, dac_ref, dar_ref, db_ref, dc_ref, dh_scr):
        @pl.when(pl.program_id(0) == 0)
        def _():
            dh_scr[...] = jnp.zeros_like(dh_scr)

        xp, bcp, dyp = _lane_pieces(xs_ref[...]), _lane_pieces(bc_ref[...]), _lane_pieces(dy_ref[...])
        dxs, dbs, dcs, shared = [], [], [], None
        for g in range(SSM_GROUPS):
            mine = slice(SSM_PAIRS * g, SSM_PAIRS * (g + 1))
            _, vjp = jax.vjp(functools.partial(_ssd_group, grp=g), *xp[mine], dt_ref[...], ac_ref[...], ar_ref[...],
                             bcp[g], bcp[SSM_GROUPS + g], *[hs_ref[0, SSM_PAIRS * g + j] for j in range(SSM_PAIRS)])
            grads = vjp(tuple(dyp[mine]) + tuple(dh_scr[SSM_PAIRS * g + j] for j in range(SSM_PAIRS)))
            dxs += grads[:SSM_PAIRS]
            ddt, dac, dar, db, dc = grads[SSM_PAIRS:SSM_PAIRS + 5]
            shared = (ddt, dac, dar) if shared is None else (shared[0] + ddt, shared[1] + dac, shared[2] + dar)
            dbs.append(db)
            dcs.append(dc)
            for j in range(SSM_PAIRS):
                dh_scr[SSM_PAIRS * g + j] = grads[SSM_PAIRS + 5 + j]
        dxs_ref[...] = jnp.concatenate(dxs, axis=1)
        ddt_ref[...], dac_ref[...], dar_ref[...] = shared
        db_ref[...] = jnp.concatenate(dbs, axis=1)
        dc_ref[...] = jnp.concatenate(dcs, axis=1)

    return pl.pallas_call(
        body, name="ssd_bwd", grid=(nc,),
        in_specs=[sp["xs"], sp["col"], sp["col"], sp["row"], sp["bc"], sp["st"], sp["xs"]],
        out_specs=[sp["xs"], sp["col"], sp["col"], sp["row"], sp["grp"], sp["grp"]],
        out_shape=[SDS((S, SSM_HEADS * SSM_P), F32), SDS((S, 16), F32), SDS((S, 16), F32), SDS((16, S), F32),
                   SDS((S, SSM_GROUPS * SSM_N), F32), SDS((S, SSM_GROUPS * SSM_N), F32)],
        scratch_shapes=[pltpu.VMEM((nhp, SSM_N, LANES), F32)],
        compiler_params=_cparams(dimension_semantics=("arbitrary",)),
    )(xs, dt_col, acs_col, acs_row, bc, hs, dy)


ATT_T = 2048
ATT_U = 512


def _pick_col(blk, h):
    return jnp.sum(jnp.where(_iota(blk.shape, 1) == h, blk, 0.0), axis=1, keepdims=True)


def _pick_row(blk, h):
    return jnp.sum(jnp.where(_iota(blk.shape, 0) == h, blk, 0.0), axis=0, keepdims=True)


def _pair_norm(x, g2, first):
    x2 = x * x
    sa = jnp.sum(jnp.where(first, x2, 0.0), axis=1, keepdims=True)
    sb = jnp.sum(jnp.where(first, 0.0, x2), axis=1, keepdims=True)
    r = jnp.where(first, lax.rsqrt(sa * (1.0 / ATT_D) + EPS), lax.rsqrt(sb * (1.0 / ATT_D) + EPS))
    return x * r * g2, r


def _pair_norm_bwd(dxn, x, r, g2, first):
    t = dxn * g2
    tx = t * x
    ma = jnp.sum(jnp.where(first, tx, 0.0), axis=1, keepdims=True)
    mb = jnp.sum(jnp.where(first, 0.0, tx), axis=1, keepdims=True)
    dx = r * (t - x * (r * r) * (jnp.where(first, ma, mb) * (1.0 / ATT_D)))
    return dx, jnp.sum(dxn * x * r, axis=0, keepdims=True)


def _fox_fwd(src, q_c0, k_c0, v_c0, gq2, gk2, cum_col, cum_row3):
    S = src.shape[0]
    T = ATT_T
    nq, nhp = S // T, ATT_HEADS // 2
    qb0, kb0, vb0 = q_c0 // LANES, k_c0 // LANES, v_c0 // LANES
    scale = ATT_D ** -0.5

    def body(q_ref, kraw_ref, v_ref, gq_ref, gk_ref, cc_ref, cr_ref, o_ref, l_ref, k_ref):
        hp, i = pl.program_id(0), pl.program_id(1)
        first = _iota((1, LANES), 1) < ATT_D

        @pl.when(i == 0)
        def _():
            k_ref[...] = _pair_norm(kraw_ref[...], gk_ref[...], first)[0].astype(BF16)

        H = ATT_U

        def attend(tile, half):
            rows = pl.ds(half * H, H)
            row0 = tile * T + half * H
            klen = row0 + H
            q = (_pair_norm(q_ref[rows, :], gq_ref[...], first)[0] * scale).astype(BF16)
            zero = jnp.zeros_like(q)
            cc = cc_ref[rows, :]
            k = k_ref[0:klen, :]
            v = v_ref[0:klen, :].astype(BF16)
            allowed = _iota((H, klen), 0) + row0 >= _iota((H, klen), 1)
            outs, lses = [], []
            for hh in range(2):
                sel = first if hh == 0 else jnp.logical_not(first)
                ck = jnp.concatenate([_pick_row(cr_ref[j], 2 * hp + hh) for j in range(tile + 1)], axis=1)[:, :klen]
                s = _bdot(jnp.where(sel, q, zero), k, _NT) + (_pick_col(cc, 2 * hp + hh) - ck)
                s = jnp.where(allowed, s, NEG)
                m = jnp.max(s, axis=1, keepdims=True)
                p = jnp.exp(s - m)
                l = jnp.sum(p, axis=1, keepdims=True)
                outs.append(_bdot(p, v, _NN) / l)
                lses.append(m + jnp.log(l))
            o_ref[rows, :] = jnp.where(first, outs[0], outs[1]).astype(o_ref.dtype)
            l_ref[rows, :] = jnp.where(first, lses[0], lses[1])

        for tile in range(nq):
            @pl.when(i == tile)
            def _(tile=tile):
                for half in range(T // H):
                    attend(tile, half)

    gain = pl.BlockSpec((1, LANES), lambda hp, i: (0, 0))
    return pl.pallas_call(
        body, name="fox_fwd", grid=(nhp, nq),
        in_specs=[pl.BlockSpec((T, LANES), lambda hp, i: (i, qb0 + hp)), pl.BlockSpec((S, LANES), lambda hp, i: (0, kb0 + hp)),
                  pl.BlockSpec((S, LANES), lambda hp, i: (0, vb0 + hp)), gain, gain,
                  pl.BlockSpec((T, 16), lambda hp, i: (i, 0)), pl.BlockSpec((nq, 16, T), lambda hp, i: (0, 0, 0))],
        out_specs=[pl.BlockSpec((T, LANES), lambda hp, i: (i, hp))] * 2,
        out_shape=[SDS((S, ATT_HEADS * ATT_D), BF16), SDS((S, ATT_HEADS * ATT_D), F32)],
        scratch_shapes=[pltpu.VMEM((S, LANES), BF16)],
        compiler_params=_cparams(dimension_semantics=("arbitrary", "arbitrary")),
    )(src, src, src, gq2, gk2, cum_col, cum_row3)


def _fox_bwd(src, q_c0, k_c0, v_c0, gq2, gk2, cum_col, cum_row3, lse, dsrc, d_c0):
    S = src.shape[0]
    T = ATT_T
    nq, nhp = S // T, ATT_HEADS // 2
    qb0, kb0, vb0, db0 = q_c0 // LANES, k_c0 // LANES, v_c0 // LANES, d_c0 // LANES
    scale = ATT_D ** -0.5

    def body(q_ref, kraw_ref, v_ref, gq_ref, gk_ref, cc_ref, cr_ref, l_ref, do_ref,
             dq_ref, dk_ref, dv_ref, dc_ref, dg_ref, k_ref, dk_acc, dv_acc):
        hp, i = pl.program_id(0), pl.program_id(1)
        first = _iota((1, LANES), 1) < ATT_D

        @pl.when(i == 0)
        def _():
            k_ref[...] = _pair_norm(kraw_ref[...], gk_ref[...], first)[0].astype(BF16)
            dk_acc[...] = jnp.zeros_like(dk_acc)
            dv_acc[...] = jnp.zeros_like(dv_acc)
            dc_ref[...] = jnp.zeros_like(dc_ref)
            dg_ref[...] = jnp.zeros_like(dg_ref)

        H = ATT_U

        def backprop(tile, half):
            rows = pl.ds(half * H, H)
            row0 = tile * T + half * H
            klen = row0 + H
            q_raw = q_ref[rows, :]
            qn, rq = _pair_norm(q_raw, gq_ref[...], first)
            q = (qn * scale).astype(BF16)
            zq = jnp.zeros_like(q)
            dob = do_ref[rows, :].astype(BF16)
            zd = jnp.zeros_like(dob)
            lse_blk, cc = l_ref[rows, :], cc_ref[rows, :]
            k = k_ref[0:klen, :]
            zk = jnp.zeros_like(k)
            allowed = _iota((H, klen), 0) + row0 >= _iota((H, klen), 1)
            dq = jnp.zeros((H, LANES), F32)
            for hh in range(2):
                sel = first if hh == 0 else jnp.logical_not(first)
                qh, doh = jnp.where(sel, q, zq), jnp.where(sel, dob, zd)
                bias_q = _pick_col(cc, 2 * hp + hh) - jnp.max(jnp.where(sel, lse_blk, NEG), axis=1, keepdims=True)
                ck = jnp.concatenate([_pick_row(cr_ref[j], 2 * hp + hh) for j in range(tile + 1)], axis=1)[:, :klen]
                p = jnp.exp(jnp.where(allowed, _bdot(qh, k, _NT) + (bias_q - ck), NEG))
                dp = _bdot(doh, v_ref[0:klen, :], _NT)
                ds = p * (dp - jnp.sum(p * dp, axis=1, keepdims=True))
                dv_acc[0:klen, :] += _bdot(p, doh, _TN)
                dk_acc[0:klen, :] += _bdot(ds, qh, _TN)
                dcs = jnp.sum(ds, axis=0, keepdims=True)
                for j in range(tile + 1):
                    n = min(T, klen - j * T)
                    dc_ref[0, j, hh:hh + 1, 0:n] -= dcs[:, j * T:j * T + n]
                dq = dq + _bdot(ds, jnp.where(sel, k, zk), _NN)
            dq_raw, dgq = _pair_norm_bwd(dq * scale, q_raw, rq, gq_ref[...], first)
            dq_ref[rows, :] = dq_raw.astype(dq_ref.dtype)
            dg_ref[0, 0:1, :] += dgq

        for tile in range(nq):
            @pl.when(i == tile)
            def _(tile=tile):
                for half in range(T // H):
                    backprop(tile, half)

        @pl.when(i == nq - 1)
        def _():
            k_raw = kraw_ref[...]
            rk = _pair_norm(k_raw, gk_ref[...], first)[1]
            dk_raw, dgk = _pair_norm_bwd(dk_acc[...], k_raw, rk, gk_ref[...], first)
            dk_ref[...] = dk_raw.astype(dk_ref.dtype)
            dv_ref[...] = dv_acc[...].astype(dv_ref.dtype)
            dg_ref[0, 1:2, :] = dgk

    gain = pl.BlockSpec((1, LANES), lambda hp, i: (0, 0))
    band = SDS((S, ATT_HEADS * ATT_D), BF16)
    return pl.pallas_call(
        body, name="fox_bwd", grid=(nhp, nq),
        in_specs=[pl.BlockSpec((T, LANES), lambda hp, i: (i, qb0 + hp)), pl.BlockSpec((S, LANES), lambda hp, i: (0, kb0 + hp)),
                  pl.BlockSpec((S, LANES), lambda hp, i: (0, vb0 + hp)), gain, gain,
                  pl.BlockSpec((T, 16), lambda hp, i: (i, 0)), pl.BlockSpec((nq, 16, T), lambda hp, i: (0, 0, 0)),
                  pl.BlockSpec((T, LANES), lambda hp, i: (i, hp)), pl.BlockSpec((T, LANES), lambda hp, i: (i, db0 + hp))],
        out_specs=[pl.BlockSpec((T, LANES), lambda hp, i: (i, hp)), pl.BlockSpec((S, LANES), lambda hp, i: (0, hp)),
                   pl.BlockSpec((S, LANES), lambda hp, i: (0, hp)), pl.BlockSpec((1, nq, 8, T), lambda hp, i: (hp, 0, 0, 0)),
                   pl.BlockSpec((1, 8, LANES), lambda hp, i: (hp, 0, 0))],
        out_shape=[band, band, band, SDS((nhp, nq, 8, T), F32), SDS((nhp, 8, LANES), F32)],
        scratch_shapes=[pltpu.VMEM((S, LANES), BF16), pltpu.VMEM((S, LANES), F32), pltpu.VMEM((S, LANES), F32)],
        compiler_params=_cparams(dimension_semantics=("arbitrary", "arbitrary")),
    )(src, src, src, gq2, gk2, cum_col, cum_row3, lse, dsrc)


def _fold_gains(dg):
    def body(d_ref, o_ref):
        t = d_ref[0]
        for h in range(1, dg.shape[0]):
            t = t + d_ref[h]
        o_ref[...] = t + pltpu.roll(t, ATT_D, axis=1)

    return pl.pallas_call(body, name="fold_gains", out_shape=SDS(dg.shape[1:], F32), compiler_params=_cparams())(dg)


def _adamw_math(w, g, m, v):
    m = ADAM_B1 * m + (1.0 - ADAM_B1) * g
    v = ADAM_B2 * v + (1.0 - ADAM_B2) * jnp.square(g)
    m_hat = m / (1.0 - ADAM_B1 ** ADAM_STEP)
    v_hat = v / (1.0 - ADAM_B2 ** ADAM_STEP)
    delta = -ADAM_LR * (m_hat / (jnp.sqrt(v_hat) + ADAM_EPS) + ADAM_WD * w)
    return delta, m, v


def _reduce_adamw(parts, w, m, v, *, tr, name, tc=None):
    R, C = w.shape
    tr, tc = min(tr, R), tc or C
    nparts = parts.shape[0]

    def body(p_ref, w_ref, m_ref, v_ref, g_ref, d_ref, nm_ref, nv_ref):
        g = p_ref[0].astype(F32)
        for s in range(1, nparts):
            g = g + p_ref[s].astype(F32)
        g_ref[...] = g
        d_ref[...], nm_ref[...], nv_ref[...] = _adamw_math(w_ref[...], g, m_ref[...], v_ref[...])

    blk = pl.BlockSpec((tr, tc), lambda i, j: (i, j))
    return pl.pallas_call(
        body, name=name, grid=(R // tr, C // tc),
        in_specs=[pl.BlockSpec((nparts, tr, tc), lambda i, j: (0, i, j)), blk, blk, blk], out_specs=[blk] * 4,
        out_shape=[SDS((R, C), F32)] * 4, compiler_params=_cparams(dimension_semantics=("parallel", "parallel")),
    )(parts, w, m, v)


def _adamw(w, g, m, v, *, name):
    def body(w_ref, g_ref, m_ref, v_ref, d_ref, nm_ref, nv_ref):
        d_ref[...], nm_ref[...], nv_ref[...] = _adamw_math(w_ref[...], g_ref[...], m_ref[...], v_ref[...])

    return pl.pallas_call(body, name=name, out_shape=[SDS(w.shape, F32)] * 3, compiler_params=_cparams())(w, g, m, v)


def _peers():
    x, y, c = lax.axis_index("x"), lax.axis_index("y"), lax.axis_index("c")
    out = []
    for k in range(1, N_DEV):
        px, py, pc = x ^ ((k >> 2) & 1), y ^ ((k >> 1) & 1), c ^ (k & 1)
        out.append(((px, py, pc), 4 * px + 2 * py + pc))
    return 4 * x + 2 * y + c, out


_HBM = pl.BlockSpec(memory_space=pltpu.HBM)
_SEM = pl.BlockSpec(memory_space=pltpu.SEMAPHORE)
_DATAFLOW = pltpu.SideEffectType.DATAFLOW_SIDE_EFFECTING


NEAR = (1, 2, 4, 6)


def _plan_peers(scatter, ks=tuple(range(1, N_DEV))):
    return lambda me, peers: [(peers[k - 1][0], peers[k - 1][1] if scatter else None, me, k - 1) for k in ks]


def _plan_relay(me, peers):
    return [(peers[0][0], peers[k - 1][1], peers[k - 1][1], j) for j, k in enumerate((2, 4, 6))]


def _plan_pair(me, peers):
    return [(peers[0][0], peers[k - 1][1], j, j) for j, k in enumerate((1, 3, 5, 7))]


def _plan_chips(me, peers):
    return [(peers[k - 1][0], k // 2, k // 2, k // 2) for k in (2, 4, 6)]


def _copy(src, dst, c, send_sems, recv_sems):
    dev, s_slot, d_slot, i = c
    return pltpu.make_async_remote_copy(
        src_ref=src if s_slot is None else src.at[s_slot], dst_ref=dst.at[d_slot], send_sem=send_sems.at[i],
        recv_sem=recv_sems.at[i], device_id=dev, device_id_type=MESH)


def _copies_start(items, *, name):
    n = len(items)
    bufs = [it[0] for it in items] + [it[1] for it in items if it[1] is not None]
    nb = len(bufs)

    def body(*refs):
        srcs, extra, sems, token = refs[:n], iter(refs[n:nb]), refs[nb:nb + 2 * n], refs[-1]
        me, peers = _peers()
        for a, (_, land, plan) in enumerate(items):
            dst = srcs[a] if land is None else next(extra)
            for c in plan(me, peers):
                _copy(srcs[a], dst, c, sems[2 * a], sems[2 * a + 1]).start()
        token[...] = jnp.zeros_like(token)

    res = pl.pallas_call(
        body, name=name,
        out_shape=([pltpu.SemaphoreType.DMA((N_DEV - 1,))] * (2 * n) + [pltpu.HBM(b.shape, b.dtype) for b in bufs]
                   + [SDS((8, LANES), F32)]),
        in_specs=[_HBM] * nb, out_specs=[_SEM] * (2 * n) + [_HBM] * nb + [pl.BlockSpec(memory_space=pltpu.VMEM)],
        input_output_aliases={i: 2 * n + i for i in range(nb)},
        compiler_params=pltpu.CompilerParams(has_side_effects=_DATAFLOW),
    )(*[pltpu.with_memory_space_constraint(b, pltpu.HBM) for b in bufs])
    sems, thru, token = res[:2 * n], list(res[2 * n:2 * n + nb]), res[-1]
    extra = iter(thru[n:])
    return [(thru[a], None if it[1] is None else next(extra), sems[2 * a], sems[2 * a + 1], it[2])
            for a, it in enumerate(items)], token


def _copies_wait(handles, after, *, name):
    n = len(handles)
    after = list(after) if isinstance(after, (list, tuple)) else [after]
    bufs = [h[0] for h in handles] + [h[1] for h in handles if h[1] is not None]
    nb = len(bufs)

    def body(*refs):
        srcs, extra, sems = refs[:n], iter(refs[n:nb]), refs[nb:nb + 2 * n]
        me, peers = _peers()
        for a, h in enumerate(handles):
            dst = srcs[a] if h[1] is None else next(extra)
            for c in h[4](me, peers):
                cp = _copy(srcs[a], dst, c, sems[2 * a], sems[2 * a + 1])
                cp.wait_send()
                cp.wait_recv()

    flat_sems = [s for h in handles for s in (h[2], h[3])]
    res = pl.pallas_call(
        body, name=name, out_shape=[pltpu.HBM(b.shape, b.dtype) for b in bufs],
        in_specs=[_HBM] * nb + [_SEM] * (2 * n) + [pl.BlockSpec(memory_space=pl.ANY)] * len(after), out_specs=[_HBM] * nb,
        input_output_aliases={i: i for i in range(nb)},
        compiler_params=pltpu.CompilerParams(has_side_effects=_DATAFLOW),
    )(*bufs, *flat_sems, *after)
    extra = iter(res[n:])
    return [(res[a], res[a] if h[1] is None else next(extra)) for a, h in enumerate(handles)]


def _exchange_start(arrays, *, scatter, name, near=()):
    items = []
    for a, arr in enumerate(arrays):
        land = lax.empty(arr.shape if scatter else (N_DEV,) + arr.shape, arr.dtype)
        items.append((arr, land, _plan_peers(scatter, NEAR) if a in near else _plan_peers(scatter)))
    return _copies_start(items, name=name)


MOVE_ROWS, MOVE_SLOTS = 512, 3


def _move_rows(src, moves, rows, *, name):
    C = src.shape[1]
    covered = max(dst + n for _, n, dst in moves)
    tail = rows - covered
    assert sum(n for _, n, _ in moves) == covered
    chunks = [(lo + o, min(MOVE_ROWS, n - o), dst + o) for lo, n, dst in moves for o in range(0, n, MOVE_ROWS)]
    nch = len(chunks)

    def body(src_ref, o_ref, buf, sin, sout, *zero):
        def fetch(i):
            lo, n, _ = chunks[i]
            return pltpu.make_async_copy(src_ref.at[pl.ds(lo, n)], buf.at[i % MOVE_SLOTS, pl.ds(0, n)], sin.at[i % MOVE_SLOTS])

        def store(i):
            _, n, dst = chunks[i]
            return pltpu.make_async_copy(buf.at[i % MOVE_SLOTS, pl.ds(0, n)], o_ref.at[pl.ds(dst, n)], sout.at[i % MOVE_SLOTS])

        if tail:
            zero[0][...] = jnp.zeros_like(zero[0])
            fill = pltpu.make_async_copy(zero[0], o_ref.at[pl.ds(covered, tail)], zero[1])
            fill.start()
        for i in range(nch):
            if i >= MOVE_SLOTS:
                store(i - MOVE_SLOTS).wait()
            fetch(i).start()
            if i >= 1:
                fetch(i - 1).wait()
                store(i - 1).start()
        fetch(nch - 1).wait()
        store(nch - 1).start()
        for i in range(max(0, nch - MOVE_SLOTS), nch):
            store(i).wait()
        if tail:
            fill.wait()

    anyspec = pl.BlockSpec(memory_space=pl.ANY)
    dma = pltpu.SemaphoreType.DMA
    return pl.pallas_call(
        body, name=name, in_specs=[anyspec], out_specs=anyspec, out_shape=SDS((rows, C), src.dtype),
        scratch_shapes=([pltpu.VMEM((MOVE_SLOTS, MOVE_ROWS, C), src.dtype), dma((MOVE_SLOTS,)), dma((MOVE_SLOTS,))]
                        + ([pltpu.VMEM((tail, C), src.dtype), dma] if tail else [])),
        compiler_params=_cparams())(src)


def _pair_sum(a, b, *, name):
    n, R, C = a.shape
    tc = 256

    def body(a_ref, b_ref, o_ref):
        o_ref[...] = (a_ref[...].astype(F32) + b_ref[...].astype(F32)).astype(o_ref.dtype)

    blk = pl.BlockSpec((1, R, tc), lambda i, j: (i, 0, j))
    return pl.pallas_call(body, name=name, grid=(n, C // tc), in_specs=[blk, blk], out_specs=blk,
                          out_shape=SDS(a.shape, a.dtype), compiler_params=_cparams(dimension_semantics=("parallel", "parallel")))(a, b)


def _own_slot(landed, own, me):
    return lax.dynamic_update_slice(landed, own[None], (me,) + (0,) * own.ndim)


SMALL = (("g_mix", 1024), ("conv_w", 6144), ("conv_b", 1536), ("dt_bias", 16), ("a_log", 16), ("d_skip", 16),
         ("ssm_norm_w", 1024), ("g_q", 64), ("g_k", 64), ("f_bias", 16), ("g_xattn", 1024), ("g_mem", 1024),
         ("xg_q", 256), ("xg_k", 256), ("g_mlp", 1024), ("loss", 1))
NOT_PARAMS = ("conv_w", "loss")
SLAB_ROWS = 112
BIG = ("w_in", "w_out", "xq_w", "xkv_w", "xo_w", "w_up", "w_down")
WEIGHTS = ("g_mix", "w_in", "conv_w", "conv_b", "dt_bias", "a_log", "d_skip", "ssm_norm_w", "g_q", "g_k", "f_bias", "w_out",
           "g_xattn", "g_mem", "xq_w", "xkv_w", "xg_q", "xg_k", "xo_w", "g_mlp", "w_up", "w_down")
O_Z, O_XS, O_B, O_C, O_DT, O_Q, O_K, O_V, O_F, O_END = 0, 1024, 2048, 2304, 2560, 2576, 3600, 4624, 5648, 5664
IN_ROW_MOVES = ((O_Z, O_B - O_Z, C_Z), (O_Q, O_F - O_Q, C_Q), (O_B, O_Q - O_B, C_B), (O_F, O_END - O_F, C_DTF + 16))


def _pack_small(vals):
    rows = []
    for name, size in SMALL:
        flat = vals[name].reshape(-1).astype(F32)
        pad = -size % LANES
        rows.append(jnp.pad(flat, (0, pad)).reshape(-1, LANES))
    slab = jnp.concatenate(rows, axis=0)
    return jnp.pad(slab, ((0, SLAB_ROWS - slab.shape[0]), (0, 0)))


def _unpack_small(slab):
    out, r = {}, 0
    for name, size in SMALL:
        nr = -(-size // LANES)
        out[name] = slab[r:r + nr].reshape(-1)[:size]
        r += nr
    return out


def _step(p, m, v, x, mem, target):
    S = x.shape[0]
    TM = 256
    me = 4 * lax.axis_index("x") + 2 * lax.axis_index("y") + lax.axis_index("c")

    def rms(u, g, name):
        return _rw_fwd(_rms_fn, [_whole(u)], [_whole(g)], [(D_MODEL, BF16)], tm=TM, name=name)[0]

    def pin(param, token):
        return param + token[0:1, 0:1]

    def landed_with_own(pairs, scatter):
        out = []
        for src, land in pairs:
            own = lax.dynamic_index_in_dim(src, me, 0, keepdims=False) if scatter else src
            out.append(_own_slot(land, own, me))
        return out

    w_in_own, m_in_own, v_in_own = p["w_in"].T, m["w_in"].T, v["w_in"].T
    ag, ag_token = _exchange_start([w_in_own.astype(BF16), p["conv_w"]] + [p[n].astype(BF16) for n in BIG[1:]],
                                   scatter=False, name="allgather_start", near=(0, 2, 3, 4, 5, 6, 7))
    h1 = rms(x, pin(p["g_mix"], ag_token), "rms_mix")
    stand_in = {"conv_w": jnp.zeros((4, 1536), F32), "loss": jnp.zeros((1,), F32)}
    slabs = [_pack_small({**d, **stand_in}) for d in (p, m, v)]
    (win_src, win_land), convw_pair = _copies_wait(ag[:2], [h1, w_in_own, m_in_own, v_in_own] + slabs,
                                                   name="allgather_wait_in")
    relay, token = _copies_start([(win_land, None, _plan_relay)], name="allgather_relay_start")
    win_land = _copies_wait(relay, token, name="allgather_relay_wait")[0][1]
    win_g, convw_g = landed_with_own([(win_src, win_land), convw_pair], False)
    w_in_o = win_g.reshape(O_END, D_MODEL)
    w_in_t = _move_rows(w_in_o, IN_ROW_MOVES, P_COLS, name="w_in_rows")
    conv_w = convw_g.transpose(1, 0, 2).reshape(4, 1536)
    cw_xs, cw_bc = conv_w[:, :1024], conv_w[:, 1024:]
    cb_xs, cb_bc = p["conv_b"][:, :1024], p["conv_b"][:, 1024:]
    dt_bias, a_log, f_bias = p["dt_bias"].reshape(16, 1), p["a_log"].reshape(16, 1), p["f_bias"].reshape(16, 1)

    proj = _matmul(h1, w_in_t, mode="nt", tm=1024, tn=640, tk=1024, name="mm_in")
    xs_c = _conv_fwd(proj, C_XS, 1024, cw_xs, cb_xs, name="conv_xs")
    bc_c = _conv_fwd(proj, C_B, 512, cw_bc, cb_bc, name="conv_bc")
    dtf_t = proj[:, C_DTF:C_DTF + 32].T
    dt_t, acs_t, cum_t = _dtf_fwd(dtf_t, dt_bias, a_log, f_bias)
    dt_col, acs_col, cum_col = dt_t.T, acs_t.T, cum_t.T
    cum_row3 = cum_t.reshape(16, S // ATT_T, ATT_T).transpose(1, 0, 2)
    y_ssd, hs = _ssd_fwd(xs_c, dt_col, acs_col, acs_t, bc_c)
    gate_rows = [_whole(y_ssd), _whole(xs_c), (proj, C_Z, 1024)]
    gate_pars = [_whole(p["d_skip"]), _whole(p["ssm_norm_w"])]
    y_ssm = _rw_fwd(_gate_fn, gate_rows, gate_pars, [(1024, BF16)], tm=TM, name="gate")[0]
    gq2, gk2 = jnp.tile(p["g_q"], (1, 2)), jnp.tile(p["g_k"], (1, 2))
    o, lse = _fox_fwd(proj, C_Q, C_K, C_V, gq2, gk2, cum_col, cum_row3)
    mixed = jnp.concatenate([y_ssm, o], axis=1)
    arrived = _copies_wait(ag[2:], mixed, name="allgather_wait_rest")
    relay, token = _copies_start([(land, None, _plan_relay) for _, land in arrived], name="allgather_relay_rest_start")
    wout_g, = landed_with_own([(arrived[0][0], _copies_wait(relay[:1], token, name="allgather_relay_out_wait")[0][1])], False)
    w_out = wout_g.reshape(2 * D_MODEL, D_MODEL)
    x1, h2 = _matmul(mixed, w_out, mode="nn", tm=1024, tn=1024, tk=2048, name="mm_out", extras=(x,),
                     row_params=(p["g_xattn"],), epilogue=_ep_residual_rms, out_dtypes=[F32, BF16])
    relayed = _copies_wait(relay[1:], x1, name="allgather_relay_rest_wait")
    xq_g, xkv_w, xo_g, w_up, wdown_g = landed_with_own(
        [(src, land) for (src, _), (_, land) in zip(arrived[1:], relayed, strict=True)], False)
    xq_w = xq_g.reshape(D_MODEL, D_MODEL)
    xo_w = xo_g.reshape(D_MODEL, D_MODEL)
    w_down = wdown_g.reshape(4 * D_MODEL, D_MODEL)

    mem_n = rms(mem, p["g_mem"], "rms_mem")
    q2 = _matmul(h2, xq_w, mode="nn", tm=1024, tn=512, tk=1024, name="mm_xq")
    kv = _matmul(mem_n, xkv_w, mode="nn", b_shards=True, tm=256, tn=256, tk=1024, name="mm_xkv")
    xa_rows = [(q2, X_D * h, X_D) for h in range(X_HEADS)]
    xa_pars = ([(kv, X_D * h, X_D) for h in range(X_HEADS)] + [(kv, D_MODEL + X_D * h, X_D) for h in range(X_HEADS)]
               + [_whole(p["xg_q"]), _whole(p["xg_k"])])
    o2 = _rw_fwd(_xattn_fn, xa_rows, xa_pars, [(D_MODEL, BF16)], tm=2 * TM, name="xattn")[0]
    x2, h3 = _matmul(o2, xo_w, mode="nn", tm=1024, tn=1024, tk=1024, name="mm_xo", extras=(x1,),
                     row_params=(p["g_mlp"],), epilogue=_ep_residual_rms, out_dtypes=[F32, BF16])

    a, usq = _matmul(h3, w_up, mode="nn", b_shards=True, tm=2048, tn=512, tk=1024, name="mm_up", out_dtypes=[F32, BF16],
                     epilogue=lambda acc: (acc, jnp.square(jax.nn.relu(acc))))
    dy, loss_part = _matmul(usq, w_down, mode="nn", tm=1024, tn=512, tk=2048, name="mm_down", extras=(x2, target),
                            epilogue=functools.partial(_ep_loss, width=D_MODEL), sums=[(1, 1)])

    def row_shards(a):
        r, c = a.shape
        return a.reshape(N_DEV, r // N_DEV, c)

    g = {"loss": loss_part}
    g["w_down"] = _matmul(usq, dy, mode="tn", out_dtype=GRAD_WIRE, tm=1024, tn=1024, tk=1024, name="mm_d_wdown")
    da = _matmul(dy, w_down, mode="nt", tm=1024, tn=1024, tk=1024, name="mm_d_usq", out_dtype=BF16, extras=(a,),
                 epilogue=lambda acc, av: (2.0 * jax.nn.relu(av) * acc,))
    g["w_up"] = _matmul(h3, da, mode="tn", out_shards=True, out_dtype=GRAD_WIRE, tm=1024, tn=512, tk=1024, name="mm_d_wup")
    sent_mlp, token = _exchange_start([row_shards(g["w_down"]), g["w_up"]], scatter=True,
                                      name="grads_start_mlp")
    dx2, g["g_mlp"] = _matmul(da, w_up, mode="nt", b_shards=True, tm=1024, tn=1024, tk=512, name="mm_d_h3",
                              extras=(x2, dy), row_params=(pin(p["g_mlp"], token),), epilogue=_ep_rms_bwd,
                              sums=[(1, D_MODEL)])

    g["xo_w"] = _matmul(o2, dx2, mode="tn", out_dtype=GRAD_WIRE, tm=1024, tn=1024, tk=1024, name="mm_d_wxo")
    do2 = _matmul(dx2, xo_w, mode="nt", tm=1024, tn=512, tk=1024, name="mm_d_o2")
    dq2, dkv, g["xg_q"], g["xg_k"] = _rw_bwd(_xattn_fn, xa_rows, xa_pars, [_whole(do2)], tm=2 * TM, name="xattn_bwd",
                                             row_grads=[BF16] * X_HEADS, join_rows=True, join_params=2 * X_HEADS)
    g["xq_w"] = _matmul(h2, dq2, mode="tn", out_dtype=GRAD_WIRE, tm=1024, tn=1024, tk=1024, name="mm_d_wxq")
    dx1, g["g_xattn"] = _matmul(dq2, xq_w, mode="nt", tm=1024, tn=1024, tk=1024, name="mm_d_h2", extras=(x1, dx2),
                                row_params=(p["g_xattn"],), epilogue=_ep_rms_bwd, sums=[(1, D_MODEL)])
    g["xkv_w"] = _matmul(mem_n, dkv, mode="tn", out_shards=True, out_dtype=GRAD_WIRE, tm=1024, tn=256, tk=256,
                         name="mm_d_wxkv")
    dmem_n = _matmul(dkv, xkv_w, mode="nt", b_shards=True, tm=256, tn=1024, tk=256, name="mm_d_memn")
    g["g_mem"] = _rw_bwd(_rms_fn, [_whole(mem)], [_whole(p["g_mem"])], [_whole(dmem_n)], tm=TM, name="rms_mem_bwd",
                         row_grads=[None])[0]

    g["w_out"] = _matmul(mixed, dx1, mode="tn", out_dtype=GRAD_WIRE, tm=1024, tn=1024, tk=1024, name="mm_d_wout")
    sent_mid, token = _exchange_start(
        [row_shards(g["w_out"]), row_shards(g["xq_w"]), g["xkv_w"], row_shards(g["xo_w"])], scatter=True,
        name="grads_start_mid")
    dmixed = _matmul(dx1, w_out, mode="nt", tm=1024, tn=1024, tk=1024, name="mm_d_mixed")
    dq, dk, dv, dcum4, dgain = _fox_bwd(proj, C_Q, C_K, C_V, pin(gq2, token), gk2, cum_col, cum_row3, lse, dmixed, 1024)
    gains = _fold_gains(dgain)
    g["g_q"], g["g_k"] = gains[0:1, :ATT_D], gains[1:2, :ATT_D]
    dy_ssd, dxs_g, dz, g["d_skip"], g["ssm_norm_w"] = _rw_bwd(
        _gate_fn, gate_rows, gate_pars, [(dmixed, 0, 1024)], tm=TM, name="gate_bwd", row_grads=[F32, F32, BF16])
    dxs_s, ddt_col, dacs_col, dacs_row, d_b, d_c = _ssd_bwd(xs_c, dt_col, acs_col, acs_t, bc_c, hs, dy_ssd)
    dcum_t = dcum4[:, :, 0:2, :].transpose(0, 2, 1, 3).reshape(16, S)
    ddtf_t, ddtb, dalog, dfb = _dtf_bwd(dtf_t, dt_bias, a_log, f_bias, ddt_col.T, dacs_col.T, dacs_row, dcum_t)
    g["dt_bias"], g["a_log"], g["f_bias"] = ddtb, dalog, dfb
    dxs_raw, dcw_xs, dcb_xs = _conv_bwd(proj, C_XS, 1024, cw_xs, cb_xs, [dxs_s, dxs_g], name="conv_xs_bwd")
    dbc_raw, dcw_bc, dcb_bc = _conv_bwd(proj, C_B, 512, cw_bc, cb_bc, [jnp.concatenate([d_b, d_c], axis=1)],
                                        name="conv_bc_bwd")
    g["conv_w"] = jnp.concatenate([dcw_xs, dcw_bc], axis=1)
    g["conv_b"] = jnp.concatenate([dcb_xs, dcb_bc], axis=1)
    ddtf = jnp.pad(ddtf_t.T.astype(BF16), ((0, 0), (0, P_COLS - C_DTF - 32)))
    dproj = jnp.concatenate([dz, dxs_raw, dq, dk, dv, dbc_raw, ddtf], axis=1)
    dw_in_p = _matmul(dproj, h1, mode="tn", out_dtype=GRAD_WIRE, tm=640, tn=1024, tk=1024, name="mm_d_win")
    g["w_in"] = _move_rows(dw_in_p, [(dst, n, lo) for lo, n, dst in IN_ROW_MOVES], O_END, name="d_w_in_rows")
    half = N_DEV // 2
    send_in = row_shards(g["w_in"])
    pair, token = _copies_start([(send_in, lax.empty((half,) + send_in.shape[1:], send_in.dtype), _plan_pair)],
                                name="grads_in_pair_start")
    grads, delta, new_m, new_v = {}, {}, {}, {}

    def update(names, sent, after, wait_name):
        parts = landed_with_own(_copies_wait(sent, after, name=wait_name), True)
        for name, part in zip(names, parts, strict=True):
            grads[name], delta[name], new_m[name], new_v[name] = _reduce_adamw(part, p[name], m[name], v[name], tr=128,
                                                                                name="adamw_" + name)

    update(("w_down", "w_up"), sent_mlp, token, "grads_wait_mlp")
    send_in, from_sibling = _copies_wait(pair, delta["w_up"], name="grads_in_pair_wait")[0]
    mine = jnp.stack([lax.dynamic_index_in_dim(send_in, me ^ (2 * j), 0, keepdims=False) for j in range(half)])
    chip_sums = _pair_sum(mine, from_sibling, name="grads_in_pair_sum")
    sent_in, token = _copies_start([(chip_sums, lax.empty(chip_sums.shape, chip_sums.dtype), _plan_chips)],
                                   name="grads_in_chip_start")
    grad_x, g["g_mix"] = _matmul(dproj, w_in_t, mode="nn", tm=1024, tn=1024, tk=1152, name="mm_d_h1", extras=(x, dx1),
                                 row_params=(pin(p["g_mix"], token),), epilogue=_ep_rms_bwd, sums=[(1, D_MODEL)])
    sent_small, _ = _exchange_start([_pack_small(g)], scatter=False, name="small_grads_start")

    update(("w_out", "xq_w", "xkv_w", "xo_w"), sent_mid, grad_x, "grads_wait_mid")
    chip_sums, landed = _copies_wait(sent_in, delta["xo_w"], name="grads_in_chip_wait")[0]
    part = lax.dynamic_update_slice(landed, chip_sums[0:1], (0, 0, 0))
    res = _reduce_adamw(part, w_in_own, m_in_own, v_in_own, tr=part.shape[1], tc=256, name="adamw_w_in")
    grads["w_in"], delta["w_in"], new_m["w_in"], new_v["w_in"] = [r.T for r in res]
    small_parts = landed_with_own(_copies_wait(sent_small, delta["w_in"], name="small_grads_wait"), False)[0]
    sg, sd, sm, sv = _reduce_adamw(small_parts, *slabs, tr=SLAB_ROWS, name="adamw_small")
    for dst, slab in ((grads, sg), (delta, sd), (new_m, sm), (new_v, sv)):
        for name, flat in _unpack_small(slab).items():
            if name not in NOT_PARAMS:
                dst[name] = flat.reshape(p[name].shape)
    loss = _unpack_small(sg)["loss"][0]
    cw_shard = p["conv_w"].shape[1]
    grads["conv_w"] = lax.dynamic_slice(_unpack_small(sg)["conv_w"].reshape(4, 1536), (0, me * cw_shard), (4, cw_shard))
    delta["conv_w"], new_m["conv_w"], new_v["conv_w"] = _adamw(p["conv_w"], grads["conv_w"], m["conv_w"], v["conv_w"],
                                                               name="adamw_conv_w")
    return loss, grad_x, grads, delta, new_m, new_v


def kernel(x, mem, g_mix, w_in, conv_w, conv_b, dt_bias, a_log, d_skip, ssm_norm_w, g_q, g_k, f_bias, w_out, g_xattn, g_mem, xq_w, xkv_w, xg_q, xg_k, xo_w, g_mlp, w_up, w_down, loss_target, m_g_mix, m_w_in, m_conv_w, m_conv_b, m_dt_bias, m_a_log, m_d_skip, m_ssm_norm_w, m_g_q, m_g_k, m_f_bias, m_w_out, m_g_xattn, m_g_mem, m_xq_w, m_xkv_w, m_xg_q, m_xg_k, m_xo_w, m_g_mlp, m_w_up, m_w_down, v_g_mix, v_w_in, v_conv_w, v_conv_b, v_dt_bias, v_a_log, v_d_skip, v_ssm_norm_w, v_g_q, v_g_k, v_f_bias, v_w_out, v_g_xattn, v_g_mem, v_xq_w, v_xkv_w, v_xg_q, v_xg_k, v_xo_w, v_g_mlp, v_w_up, v_w_down):
    args = locals()
    drop = lambda t: t[0] if t.ndim == 3 else t
    p = {n: drop(args[n]) for n in WEIGHTS}
    m = {n: drop(args["m_" + n]) for n in WEIGHTS}
    v = {n: drop(args["v_" + n]) for n in WEIGHTS}
    loss, grad_x, grads, delta, new_m, new_v = _step(p, m, v, x[0], mem[0], loss_target[0])
    outs = [loss, grad_x[None]]
    for d in (grads, delta, new_m, new_v):
        outs += [d[n].reshape(args[n].shape) for n in WEIGHTS]
    return tuple(outs)
```

```python
import functools

import jax
import jax.numpy as jnp
from jax import lax
from jax.experimental import pallas as pl
from jax.experimental.pallas import tpu as pltpu

F32, BF16 = jnp.float32, jnp.bfloat16
SDS = jax.ShapeDtypeStruct
HI = lax.Precision.HIGHEST
MESH = pl.DeviceIdType.MESH

N_DEV = 8
EPS = 1e-5
D_MODEL = 1024
SSM_HEADS, SSM_P, SSM_N, SSM_GROUPS, CHUNK = 16, 64, 128, 2, 128
ATT_HEADS, ATT_D = 16, 64
X_HEADS, X_D = 4, 256
LANES = 128
VMEM_LIMIT = 48 * 1024 * 1024
NEG = -1e30

GRAD_WIRE = BF16
ADAM_LR, ADAM_B1, ADAM_B2, ADAM_EPS, ADAM_WD, ADAM_STEP = 0.001, 0.9, 0.999, 1e-08, 0.01, 10

C_Z, C_XS, C_Q, C_K, C_V, C_B, C_C, C_DTF, P_COLS = 0, 1024, 2048, 3072, 4096, 5120, 5376, 5632, 5760

_NN = (((1,), (0,)), ((), ()))
_NT = (((1,), (1,)), ((), ()))
_TN = (((0,), (0,)), ((), ()))


def _cparams(**kw):
    return pltpu.CompilerParams(vmem_limit_bytes=VMEM_LIMIT, **kw)


def _bdot(a, b, dn):
    return lax.dot_general(a.astype(BF16), b.astype(BF16), dn, preferred_element_type=F32)


@jax.custom_vjp
def mm_nn(a, b):
    return _bdot(a, b, _NN)


mm_nn.defvjp(lambda a, b: (mm_nn(a, b), (a, b)), lambda r, g: (_bdot(g, r[1], _NT), _bdot(r[0], g, _TN)))


@jax.custom_vjp
def mm_nt(a, b):
    return _bdot(a, b, _NT)


mm_nt.defvjp(lambda a, b: (mm_nt(a, b), (a, b)), lambda r, g: (_bdot(g, r[1], _NN), _bdot(g, r[0], _TN)))


@jax.custom_vjp
def mm_tn(a, b):
    return _bdot(a, b, _TN)


mm_tn.defvjp(lambda a, b: (mm_tn(a, b), (a, b)), lambda r, g: (_bdot(r[1], g, _NT), _bdot(r[0], g, _NN)))


def _cdot(x, c):
    return jnp.dot(x, c, precision=HI, preferred_element_type=F32)


def _iota(shape, dim):
    return lax.broadcasted_iota(jnp.int32, shape, dim)


def _matmul(a, b, *, mode, tm, tn, tk, name, out_dtype=F32, add=None, extras=(), epilogue=None, out_dtypes=None,
            b_shards=False, out_shards=False, row_params=(), sums=()):
    if mode == "tn":
        K, M = a.shape
    else:
        M, K = a.shape
    if b_shards:
        N = b.shape[1] if mode == "nt" else b.shape[0] * b.shape[2]
        tn, tk = (tn, b.shape[2]) if mode == "nt" else (b.shape[2], tk)
    else:
        N = b.shape[0] if mode == "nt" else b.shape[1]
    tm, tn, tk = min(tm, M), min(tn, N), min(tk, K)
    assert M % tm == 0 and N % tn == 0 and K % tk == 0, (name, M, N, K, tm, tn, tk)
    assert not b_shards or (K // tk if mode == "nt" else N // tn) == b.shape[0], name
    assert not (out_shards and (extras or add is not None)), name
    nk = K // tk
    dn = {"nn": _NN, "nt": _NT, "tn": _TN}[mode]
    if add is not None:
        extras, epilogue = (add,), lambda acc, r: (acc + r,)
    elif epilogue is None:
        epilogue = lambda acc: (acc,)
    out_dtypes = out_dtypes or [out_dtype]
    ne, no, ns = len(extras) + len(row_params), len(out_dtypes), len(sums)
    assert all(s == (1, 1) or (s == (1, N) and tn == N) for s in sums), name

    def body(*refs):
        a_ref, b_ref = refs[:2]
        e_refs, o_refs, s_refs = refs[2:2 + ne], refs[2 + ne:2 + ne + no], refs[2 + ne + no:2 + ne + no + ns]

        def finish(acc):
            res = epilogue(acc, *[e[...] for e in e_refs])
            for o_ref, v in zip(o_refs, res[:no], strict=True):
                o_ref[...] = v.astype(o_ref.dtype)
            first_tile = jnp.logical_and(pl.program_id(0) == 0, pl.program_id(1) == 0)
            for s_ref, v in zip(s_refs, res[no:], strict=True):
                @pl.when(first_tile)
                def _(s_ref=s_ref, v=v):
                    s_ref[...] = v

                @pl.when(jnp.logical_not(first_tile))
                def _(s_ref=s_ref, v=v):
                    s_ref[...] += v

        prod = _bdot(a_ref[...], b_ref[...], dn)
        if nk == 1:
            finish(prod)
            return
        acc_ref = refs[-1]
        k = pl.program_id(2)

        @pl.when(k == 0)
        def _():
            acc_ref[...] = prod

        @pl.when(jnp.logical_and(k > 0, k < nk - 1))
        def _():
            acc_ref[...] += prod

        @pl.when(k == nk - 1)
        def _():
            finish(acc_ref[...] + prod)

    a_spec = pl.BlockSpec((tk, tm), lambda i, j, k: (k, i)) if mode == "tn" else pl.BlockSpec((tm, tk), lambda i, j, k: (i, k))
    if b_shards and mode == "nt":
        b_spec = pl.BlockSpec((None, tn, tk), lambda i, j, k: (k, j, 0))
    elif b_shards:
        b_spec = pl.BlockSpec((None, tk, tn), lambda i, j, k: (j, k, 0))
    elif mode == "nt":
        b_spec = pl.BlockSpec((tn, tk), lambda i, j, k: (j, k))
    else:
        b_spec = pl.BlockSpec((tk, tn), lambda i, j, k: (k, j))
    if out_shards:
        o_spec, o_shape = pl.BlockSpec((None, tm, tn), lambda i, j, k: (j, i, 0)), (N // tn, M, tn)
    else:
        o_spec, o_shape = pl.BlockSpec((tm, tn), lambda i, j, k: (i, j)), (M, N)
    row_spec = pl.BlockSpec((1, tn), lambda i, j, k: (0, j))
    sum_specs = [pl.BlockSpec(s, lambda i, j, k: (0, 0)) for s in sums]
    res = pl.pallas_call(
        body, name=name, grid=(M // tm, N // tn, nk),
        in_specs=[a_spec, b_spec] + [o_spec] * len(extras) + [row_spec] * len(row_params),
        out_specs=[o_spec] * no + sum_specs, out_shape=[SDS(o_shape, dt) for dt in out_dtypes] + [SDS(s, F32) for s in sums],
        scratch_shapes=[pltpu.VMEM((tm, tn), F32)] if nk > 1 else [],
        compiler_params=_cparams(dimension_semantics=(("arbitrary",) * 3 if sums else ("parallel", "parallel", "arbitrary"))),
    )(a, b, *extras, *row_params)
    return res[0] if no + ns == 1 else res


def _row_spec(tm, spec):
    _, c0, w = spec
    assert c0 % w == 0
    return pl.BlockSpec((tm, w), functools.partial(lambda i, cb: (i, cb), cb=c0 // w))


def _par_spec(spec):
    arr, c0, w = spec
    assert c0 % w == 0
    return pl.BlockSpec((arr.shape[0], w), functools.partial(lambda i, cb: (0, cb), cb=c0 // w))


def _whole(arr):
    return (arr, 0, arr.shape[1])


def _rw_fwd(fn, rows, params, outs, *, tm, name):
    M = rows[0][0].shape[0]
    nr, npar = len(rows), len(params)

    def body(*refs):
        rv = [r[...].astype(F32) for r in refs[:nr]]
        pv = [p[...].astype(F32) for p in refs[nr:nr + npar]]
        res = fn(*rv, *pv)
        for o_ref, v in zip(refs[nr + npar:], res, strict=True):
            o_ref[...] = v.astype(o_ref.dtype)

    return pl.pallas_call(
        body, name=name, grid=(M // tm,),
        in_specs=[_row_spec(tm, r) for r in rows] + [_par_spec(p) for p in params],
        out_specs=[pl.BlockSpec((tm, w), lambda i: (i, 0)) for w, _ in outs],
        out_shape=[SDS((M, w), dt) for w, dt in outs],
        compiler_params=_cparams(dimension_semantics=("parallel",)),
    )(*[r[0] for r in rows], *[p[0] for p in params])


def _rw_bwd(fn, rows, params, cts, *, tm, name, row_grads, adds=None, join_rows=False, join_params=0):
    M = rows[0][0].shape[0]
    adds = adds or {}
    nr, npar, nc = len(rows), len(params), len(cts)
    add_keys = sorted(adds)
    want = [k for k in range(nr) if row_grads[k] is not None]

    def body(*refs):
        pos = 0
        r_refs = refs[pos:pos + nr]; pos += nr
        p_refs = refs[pos:pos + npar]; pos += npar
        c_refs = refs[pos:pos + nc]; pos += nc
        a_refs = dict(zip(add_keys, refs[pos:pos + len(add_keys)])); pos += len(add_keys)
        n_row_out = 1 if join_rows else len(want)
        dr_refs = refs[pos:pos + n_row_out]; pos += n_row_out
        dp_refs = refs[pos:]
        rv = [r[...].astype(F32) for r in r_refs]
        pv = [p[...].astype(F32) for p in p_refs]
        _, vjp = jax.vjp(fn, *rv, *pv)
        g = vjp(tuple(c[...].astype(F32) for c in c_refs))
        row_vals = []
        for k in want:
            v = g[k]
            if k in a_refs:
                v = v + a_refs[k][...].astype(F32)
            row_vals.append(v)
        if join_rows:
            row_vals = [jnp.concatenate(row_vals, axis=1)]
        for ref, v in zip(dr_refs, row_vals, strict=True):
            ref[...] = v.astype(ref.dtype)
        par_vals = list(g[nr:])
        if join_params:
            par_vals = [jnp.concatenate(par_vals[:join_params], axis=1)] + par_vals[join_params:]
        first = pl.program_id(0) == 0
        for ref, v in zip(dp_refs, par_vals, strict=True):
            @pl.when(first)
            def _(ref=ref, v=v):
                ref[...] = v

            @pl.when(jnp.logical_not(first))
            def _(ref=ref, v=v):
                ref[...] += v

    row_out = [(rows[k][2], row_grads[k]) for k in want]
    if join_rows:
        row_out = [(sum(w for w, _ in row_out), row_out[0][1])]
    par_out = [(p[0].shape[0], p[2]) for p in params]
    if join_params:
        par_out = [(par_out[0][0], sum(w for _, w in par_out[:join_params]))] + par_out[join_params:]
    res = pl.pallas_call(
        body, name=name, grid=(M // tm,),
        in_specs=([_row_spec(tm, r) for r in rows] + [_par_spec(p) for p in params] + [_row_spec(tm, c) for c in cts]
                  + [_row_spec(tm, adds[k]) for k in add_keys]),
        out_specs=([pl.BlockSpec((tm, w), lambda i: (i, 0)) for w, _ in row_out]
                   + [pl.BlockSpec(s, lambda i: (0, 0)) for s in par_out]),
        out_shape=[SDS((M, w), dt) for w, dt in row_out] + [SDS(s, F32) for s in par_out],
        compiler_params=_cparams(dimension_semantics=("arbitrary",)),
    )(*[r[0] for r in rows], *[p[0] for p in params], *[c[0] for c in cts], *[adds[k][0] for k in add_keys])
    return res


def _rms_fn(x, g):
    r = lax.rsqrt(jnp.mean(x * x, axis=-1, keepdims=True) + EPS)
    return (x * r * g,)


def _ep_residual_rms(acc, res, g):
    x = acc + res
    return x, _rms_fn(x, g)[0]


def _ep_rms_bwd(dh, x, dres, g):
    r = lax.rsqrt(jnp.mean(x * x, axis=-1, keepdims=True) + EPS)
    t = dh * g
    dx = dres + r * (t - x * (r * r) * jnp.mean(t * x, axis=-1, keepdims=True))
    return dx, jnp.sum(dh * x * r, axis=0, keepdims=True)


def _ep_loss(acc, res, target, *, width):
    e = acc + res - target
    return e * (1.0 / width), jnp.sum(jnp.sum(e * e, axis=1, keepdims=True), axis=0, keepdims=True) * (0.5 / width)


def _seg_mats(width, seg):
    n = width // seg
    p = (_iota((width, n), 0) // seg == _iota((width, n), 1)).astype(F32)
    e = (_iota((n, width), 1) // seg == _iota((n, width), 0)).astype(F32)
    return p, e


def _gate_fn(y, xs, z, dskip, w):
    width = SSM_HEADS * SSM_P
    _, e = _seg_mats(width, SSM_P)
    y = (y + _cdot(dskip, e) * xs) * (z * jax.nn.sigmoid(z))
    g0 = _iota((1, width), 1) < width // SSM_GROUPS
    y2 = y * y
    gw = width // SSM_GROUPS
    ms0 = jnp.sum(jnp.where(g0, y2, 0.0), axis=-1, keepdims=True) * (1.0 / gw)
    ms1 = jnp.sum(jnp.where(g0, 0.0, y2), axis=-1, keepdims=True) * (1.0 / gw)
    r = jnp.where(g0, lax.rsqrt(ms0 + EPS), lax.rsqrt(ms1 + EPS))
    return (y * r * w,)


def _xattn_fn(q0, q1, q2, q3, k0, k1, k2, k3, v0, v1, v2, v3, gq, gk):
    def norm(u, g):
        return u * lax.rsqrt(jnp.mean(u * u, axis=-1, keepdims=True) + EPS) * g
    outs = []
    for q, k, v in ((q0, k0, v0), (q1, k1, v1), (q2, k2, v2), (q3, k3, v3)):
        s = mm_nt(norm(q, gq), norm(k, gk)) * (X_D ** -0.5)
        p = jnp.exp(s - lax.stop_gradient(jnp.max(s, axis=-1, keepdims=True)))
        p = p / jnp.sum(p, axis=-1, keepdims=True)
        outs.append(mm_nn(p, v))
    return (jnp.concatenate(outs, axis=-1),)


CONV_TC = 256


def _shift_down(u, k):
    if k == 0:
        return u
    return jnp.where(_iota(u.shape, 0) >= k, pltpu.roll(u, k, axis=0), 0.0)


def _shift_up(u, k):
    if k == 0:
        return u
    n = u.shape[0]
    return jnp.where(_iota(u.shape, 0) < n - k, pltpu.roll(u, n - k, axis=0), 0.0)


def _conv_pre(u, w_ref, b):
    pre = b + w_ref[3:4, :] * u
    for k in (1, 2, 3):
        pre = pre + w_ref[3 - k:4 - k, :] * _shift_down(u, k)
    return pre


def _conv_fwd(src, c0, width, w, b, *, name):
    S = src.shape[0]
    cb0 = c0 // CONV_TC

    def body(u_ref, w_ref, b_ref, o_ref):
        pre = _conv_pre(u_ref[...], w_ref, b_ref[...])
        o_ref[...] = pre * jax.nn.sigmoid(pre)

    return pl.pallas_call(
        body, name=name, grid=(width // CONV_TC,),
        in_specs=[pl.BlockSpec((S, CONV_TC), lambda j: (0, cb0 + j)), pl.BlockSpec((4, CONV_TC), lambda j: (0, j)),
                  pl.BlockSpec((1, CONV_TC), lambda j: (0, j))],
        out_specs=pl.BlockSpec((S, CONV_TC), lambda j: (0, j)), out_shape=SDS((S, width), F32),
        compiler_params=_cparams(dimension_semantics=("parallel",)),
    )(src, w, b)


def _conv_bwd(src, c0, width, w, b, douts, *, name):
    S = src.shape[0]
    cb0 = c0 // CONV_TC
    nd = len(douts)

    def body(*refs):
        u_ref, w_ref, b_ref = refs[:3]
        d_refs = refs[3:3 + nd]
        du_ref, dw_ref, db_ref = refs[3 + nd:]
        u = u_ref[...]
        pre = _conv_pre(u, w_ref, b_ref[...])
        sg = jax.nn.sigmoid(pre)
        dout = d_refs[0][...]
        for r in d_refs[1:]:
            dout = dout + r[...]
        dpre = dout * (sg * (1.0 + pre * (1.0 - sg)))
        du = w_ref[3:4, :] * dpre
        dw_ref[3:4, :] = jnp.sum(dpre * u, axis=0, keepdims=True)
        for k in (1, 2, 3):
            du = du + w_ref[3 - k:4 - k, :] * _shift_up(dpre, k)
            dw_ref[3 - k:4 - k, :] = jnp.sum(dpre * _shift_down(u, k), axis=0, keepdims=True)
        du_ref[...] = du.astype(du_ref.dtype)
        db_ref[...] = jnp.sum(dpre, axis=0, keepdims=True)

    return pl.pallas_call(
        body, name=name, grid=(width // CONV_TC,),
        in_specs=[pl.BlockSpec((S, CONV_TC), lambda j: (0, cb0 + j)), pl.BlockSpec((4, CONV_TC), lambda j: (0, j)),
                  pl.BlockSpec((1, CONV_TC), lambda j: (0, j))] + [pl.BlockSpec((S, CONV_TC), lambda j: (0, j))] * nd,
        out_specs=[pl.BlockSpec((S, CONV_TC), lambda j: (0, j)), pl.BlockSpec((4, CONV_TC), lambda j: (0, j)),
                   pl.BlockSpec((1, CONV_TC), lambda j: (0, j))],
        out_shape=[SDS((S, width), BF16), SDS((4, width), F32), SDS((1, width), F32)],
        compiler_params=_cparams(dimension_semantics=("parallel",)),
    )(src, w, b, *douts)


def _softplus(x):
    return jnp.maximum(x, 0.0) + jnp.log(1.0 + jnp.exp(-jnp.abs(x)))


def _prefix_sum(x, seg):
    n = x.shape[1]
    pos = _iota(x.shape, 1) % seg
    k = 1
    while k < seg:
        x = x + jnp.where(pos >= k, pltpu.roll(x, k, axis=1), 0.0)
        k *= 2
    return x


def _suffix_sum(x, seg):
    n = x.shape[1]
    pos = _iota(x.shape, 1) % seg
    k = 1
    while k < seg:
        x = x + jnp.where(pos + k < seg, pltpu.roll(x, n - k, axis=1), 0.0)
        k *= 2
    return x


def _dtf_fwd(dtf_t, dt_bias, a_log, f_bias):
    S = dtf_t.shape[1]

    def body(x_ref, db_ref, al_ref, fb_ref, dt_ref, acs_ref, cum_ref):
        dt = _softplus(x_ref[0:16, :] + db_ref[...])
        dt_ref[...] = dt
        acs_ref[...] = _prefix_sum(dt * (-jnp.exp(al_ref[...])), CHUNK)
        cum_ref[...] = _prefix_sum(-_softplus(-(x_ref[16:32, :] + fb_ref[...])), S)

    return pl.pallas_call(body, name="dtf_fwd", out_shape=[SDS((16, S), F32)] * 3, compiler_params=_cparams())(
        dtf_t, dt_bias, a_log, f_bias)


def _dtf_bwd(dtf_t, dt_bias, a_log, f_bias, d_dt, d_acs_a, d_acs_b, d_cum):
    S = dtf_t.shape[1]

    def body(x_ref, db_ref, al_ref, fb_ref, ddt_ref, da1_ref, da2_ref, dc_ref, dx_ref, ddb_ref, dal_ref, dfb_ref):
        xd = x_ref[0:16, :] + db_ref[...]
        dt = _softplus(xd)
        a = -jnp.exp(al_ref[...])
        d_da = _suffix_sum(da1_ref[...] + da2_ref[...], CHUNK)
        d_dt = ddt_ref[...] + d_da * a
        dal_ref[...] = jnp.sum(d_da * dt, axis=1, keepdims=True) * a
        d_xd = d_dt * jax.nn.sigmoid(xd)
        ddb_ref[...] = jnp.sum(d_xd, axis=1, keepdims=True)
        xf = x_ref[16:32, :] + fb_ref[...]
        d_xf = _suffix_sum(dc_ref[...], S) * jax.nn.sigmoid(-xf)
        dfb_ref[...] = jnp.sum(d_xf, axis=1, keepdims=True)
        dx_ref[0:16, :] = d_xd
        dx_ref[16:32, :] = d_xf

    return pl.pallas_call(body, name="dtf_bwd", out_shape=[SDS((32, S), F32)] + [SDS((16, 1), F32)] * 3,
                          compiler_params=_cparams())(dtf_t, dt_bias, a_log, f_bias, d_dt, d_acs_a, d_acs_b, d_cum)


SSM_PAIRS = SSM_HEADS // 2 // SSM_GROUPS


def _ssd_pair(xs, dtc, acol, arow, bm, cm, cbm, h, hp):
    L = CHUNK
    first = _iota((1, LANES), 1) < SSM_P
    i16, s16 = _iota((L, 16), 1), _iota((16, L), 0)
    ha, hb = 2 * hp, 2 * hp + 1

    def selc(blk, hh):
        return jnp.sum(jnp.where(i16 == hh, blk, 0.0), axis=1, keepdims=True)

    def selr(blk, hh):
        return jnp.sum(jnp.where(s16 == hh, blk, 0.0), axis=0, keepdims=True)

    x = xs * jnp.where(first, selc(dtc, ha), selc(dtc, hb))
    ca, cb, ra, rb = selc(acol, ha), selc(acol, hb), selr(arow, ha), selr(arow, hb)
    tri = _iota((L, L), 0) >= _iota((L, L), 1)
    la = jnp.exp(jnp.where(tri, ca - ra, NEG))
    lb = jnp.exp(jnp.where(tri, cb - rb, NEG))
    y = jnp.where(first, mm_nn(cbm * la, x), mm_nn(cbm * lb, x))
    y = y + jnp.where(first, jnp.exp(ca), jnp.exp(cb)) * mm_nn(cm, h)
    last = _iota((1, L), 1) == L - 1
    ala = jnp.sum(jnp.where(last, ra, 0.0), axis=1, keepdims=True)
    alb = jnp.sum(jnp.where(last, rb, 0.0), axis=1, keepdims=True)
    dec = jnp.where(first, jnp.exp(ala - ca), jnp.exp(alb - cb))
    hn = jnp.where(first, jnp.exp(ala), jnp.exp(alb)) * h + mm_tn(bm, x * dec)
    return y, hn


def _ssd_group(*args, grp):
    xs, (dtc, acol, arow, bm, cm), hs = args[:SSM_PAIRS], args[SSM_PAIRS:SSM_PAIRS + 5], args[SSM_PAIRS + 5:]
    cbm = mm_nt(cm, bm)
    res = [_ssd_pair(xs[j], dtc, acol, arow, bm, cm, cbm, hs[j], SSM_PAIRS * grp + j) for j in range(SSM_PAIRS)]
    return tuple(r[0] for r in res) + tuple(r[1] for r in res)


def _ssd_specs(nc, rev):
    L = CHUNK
    cidx = (lambda c: nc - 1 - c) if rev else (lambda c: c)
    return dict(
        xs=pl.BlockSpec((L, SSM_HEADS * SSM_P), lambda c: (cidx(c), 0)),
        col=pl.BlockSpec((L, 16), lambda c: (cidx(c), 0)),
        row=pl.BlockSpec((16, L), lambda c: (0, cidx(c))),
        bc=pl.BlockSpec((L, 2 * SSM_GROUPS * SSM_N), lambda c: (cidx(c), 0)),
        grp=pl.BlockSpec((L, SSM_GROUPS * SSM_N), lambda c: (cidx(c), 0)),
        st=pl.BlockSpec((1, SSM_HEADS // 2, SSM_N, LANES), lambda c: (cidx(c), 0, 0, 0)),
    )


def _lane_pieces(v):
    return [v[:, LANES * j:LANES * (j + 1)] for j in range(v.shape[1] // LANES)]


def _ssd_fwd(xs, dt_col, acs_col, acs_row, bc):
    S = xs.shape[0]
    nc, nhp = S // CHUNK, SSM_HEADS // 2
    sp = _ssd_specs(nc, False)

    def body(xs_ref, dt_ref, ac_ref, ar_ref, bc_ref, y_ref, hs_ref, h_scr):
        @pl.when(pl.program_id(0) == 0)
        def _():
            h_scr[...] = jnp.zeros_like(h_scr)

        xp, bcp = _lane_pieces(xs_ref[...]), _lane_pieces(bc_ref[...])
        ys = []
        for g in range(SSM_GROUPS):
            hs = [h_scr[SSM_PAIRS * g + j] for j in range(SSM_PAIRS)]
            for j in range(SSM_PAIRS):
                hs_ref[0, SSM_PAIRS * g + j] = hs[j]
            res = _ssd_group(*xp[SSM_PAIRS * g:SSM_PAIRS * (g + 1)], dt_ref[...], ac_ref[...], ar_ref[...], bcp[g],
                             bcp[SSM_GROUPS + g], *hs, grp=g)
            ys += res[:SSM_PAIRS]
            for j in range(SSM_PAIRS):
                h_scr[SSM_PAIRS * g + j] = res[SSM_PAIRS + j]
        y_ref[...] = jnp.concatenate(ys, axis=1)

    return pl.pallas_call(
        body, name="ssd_fwd", grid=(nc,),
        in_specs=[sp["xs"], sp["col"], sp["col"], sp["row"], sp["bc"]],
        out_specs=[sp["xs"], sp["st"]],
        out_shape=[SDS((S, SSM_HEADS * SSM_P), F32), SDS((nc, nhp, SSM_N, LANES), F32)],
        scratch_shapes=[pltpu.VMEM((nhp, SSM_N, LANES), F32)],
        compiler_params=_cparams(dimension_semantics=("arbitrary",)),
    )(xs, dt_col, acs_col, acs_row, bc)


def _ssd_bwd(xs, dt_col, acs_col, acs_row, bc, hs, dy):
    S = xs.shape[0]
    nc, nhp = S // CHUNK, SSM_HEADS // 2
    L = CHUNK
    rc = lambda c: nc - 1 - c
    sp = dict(
        xs=pl.BlockSpec((L, SSM_PAIRS * LANES), lambda c, g: (rc(c), g)),
        col=pl.BlockSpec((L, 16), lambda c, g: (rc(c), 0)),
        row=pl.BlockSpec((16, L), lambda c, g: (0, rc(c))),
        b=pl.BlockSpec((L, SSM_N), lambda c, g: (rc(c), g)),
        c=pl.BlockSpec((L, SSM_N), lambda c, g: (rc(c), SSM_GROUPS + g)),
        st=pl.BlockSpec((1, SSM_PAIRS, SSM_N, LANES), lambda c, g: (rc(c), g, 0, 0)),
    )

    def body(xs_ref, dt_ref, ac_ref, ar_ref, b_ref, c_ref, hs_ref, dy_ref,
             dxs_ref, ddt_ref, dac_ref, dar_ref, db_ref, dc_ref, dh_scr):
        c, g = pl.program_id(0), pl.program_id(1)

        @pl.when(c == 0)
        def _():
            for j in range(SSM_PAIRS):
                dh_scr[SSM_PAIRS * g + j] = jnp.zeros((SSM_N, LANES), F32)

        _, vjp = jax.vjp(functools.partial(_ssd_group, grp=g), *_lane_pieces(xs_ref[...]), dt_ref[...], ac_ref[...],
                         ar_ref[...], b_ref[...], c_ref[...], *[hs_ref[0, j] for j in range(SSM_PAIRS)])
        grads = vjp(tuple(_lane_pieces(dy_ref[...])) + tuple(dh_scr[SSM_PAIRS * g + j] for j in range(SSM_PAIRS)))
        dxs_ref[...] = jnp.concatenate(grads[:SSM_PAIRS], axis=1)
        ddt, dac, dar, db, dc = grads[SSM_PAIRS:SSM_PAIRS + 5]
        for j in range(SSM_PAIRS):
            dh_scr[SSM_PAIRS * g + j] = grads[SSM_PAIRS + 5 + j]
        db_ref[...] = db
        dc_ref[...] = dc

        @pl.when(g == 0)
        def _():
            ddt_ref[...] = ddt
            dac_ref[...] = dac
            dar_ref[...] = dar

        @pl.when(g > 0)
        def _():
            ddt_ref[...] += ddt
            dac_ref[...] += dac
            dar_ref[...] += dar

    return pl.pallas_call(
        body, name="ssd_bwd", grid=(nc, SSM_GROUPS),
        in_specs=[sp["xs"], sp["col"], sp["col"], sp["row"], sp["b"], sp["c"], sp["st"], sp["xs"]],
        out_specs=[sp["xs"], sp["col"], sp["col"], sp["row"], sp["b"], sp["b"]],
        out_shape=[SDS((S, SSM_HEADS * SSM_P), F32), SDS((S, 16), F32), SDS((S, 16), F32), SDS((16, S), F32),
                   SDS((S, SSM_GROUPS * SSM_N), F32), SDS((S, SSM_GROUPS * SSM_N), F32)],
        scratch_shapes=[pltpu.VMEM((nhp, SSM_N, LANES), F32)],
        compiler_params=_cparams(dimension_semantics=("arbitrary", "arbitrary")),
    )(xs, dt_col, acs_col, acs_row, bc, bc, hs, dy)


ATT_T = 2048
ATT_U = 512


def _pick_col(blk, h):
    return jnp.sum(jnp.where(_iota(blk.shape, 1) == h, blk, 0.0), axis=1, keepdims=True)


def _pick_row(blk, h):
    return jnp.sum(jnp.where(_iota(blk.shape, 0) == h, blk, 0.0), axis=0, keepdims=True)


def _pair_norm(x, g2, first):
    x2 = x * x
    sa = jnp.sum(jnp.where(first, x2, 0.0), axis=1, keepdims=True)
    sb = jnp.sum(jnp.where(first, 0.0, x2), axis=1, keepdims=True)
    r = jnp.where(first, lax.rsqrt(sa * (1.0 / ATT_D) + EPS), lax.rsqrt(sb * (1.0 / ATT_D) + EPS))
    return x * r * g2, r


def _pair_norm_bwd(dxn, x, r, g2, first):
    t = dxn * g2
    tx = t * x
    ma = jnp.sum(jnp.where(first, tx, 0.0), axis=1, keepdims=True)
    mb = jnp.sum(jnp.where(first, 0.0, tx), axis=1, keepdims=True)
    dx = r * (t - x * (r * r) * (jnp.where(first, ma, mb) * (1.0 / ATT_D)))
    return dx, jnp.sum(dxn * x * r, axis=0, keepdims=True)


def _fox_fwd(src, q_c0, k_c0, v_c0, gq2, gk2, cum_col, cum_row3):
    S = src.shape[0]
    T = ATT_T
    nq, nhp = S // T, ATT_HEADS // 2
    qb0, kb0, vb0 = q_c0 // LANES, k_c0 // LANES, v_c0 // LANES
    scale = ATT_D ** -0.5

    def body(q_ref, kraw_ref, v_ref, gq_ref, gk_ref, cc_ref, cr_ref, o_ref, l_ref, k_ref):
        hp, i = pl.program_id(0), pl.program_id(1)
        first = _iota((1, LANES), 1) < ATT_D

        @pl.when(i == 0)
        def _():
            k_ref[...] = _pair_norm(kraw_ref[...], gk_ref[...], first)[0].astype(BF16)

        H = ATT_U

        def attend(tile, half):
            rows = pl.ds(half * H, H)
            row0 = tile * T + half * H
            klen = row0 + H
            q = (_pair_norm(q_ref[rows, :], gq_ref[...], first)[0] * scale).astype(BF16)
            zero = jnp.zeros_like(q)
            cc = cc_ref[rows, :]
            k = k_ref[0:klen, :]
            v = v_ref[0:klen, :].astype(BF16)
            allowed = _iota((H, klen), 0) + row0 >= _iota((H, klen), 1)
            outs, lses = [], []
            for hh in range(2):
                sel = first if hh == 0 else jnp.logical_not(first)
                ck = jnp.concatenate([_pick_row(cr_ref[j], 2 * hp + hh) for j in range(tile + 1)], axis=1)[:, :klen]
                s = _bdot(jnp.where(sel, q, zero), k, _NT) + (_pick_col(cc, 2 * hp + hh) - ck)
                s = jnp.where(allowed, s, NEG)
                m = jnp.max(s, axis=1, keepdims=True)
                p = jnp.exp(s - m)
                l = jnp.sum(p, axis=1, keepdims=True)
                outs.append(_bdot(p, v, _NN) / l)
                lses.append(m + jnp.log(l))
            o_ref[rows, :] = jnp.where(first, outs[0], outs[1]).astype(o_ref.dtype)
            l_ref[rows, :] = jnp.where(first, lses[0], lses[1])

        for tile in range(nq):
            @pl.when(i == tile)
            def _(tile=tile):
                for half in range(T // H):
                    attend(tile, half)

    gain = pl.BlockSpec((1, LANES), lambda hp, i: (0, 0))
    return pl.pallas_call(
        body, name="fox_fwd", grid=(nhp, nq),
        in_specs=[pl.BlockSpec((T, LANES), lambda hp, i: (i, qb0 + hp)), pl.BlockSpec((S, LANES), lambda hp, i: (0, kb0 + hp)),
                  pl.BlockSpec((S, LANES), lambda hp, i: (0, vb0 + hp)), gain, gain,
                  pl.BlockSpec((T, 16), lambda hp, i: (i, 0)), pl.BlockSpec((nq, 16, T), lambda hp, i: (0, 0, 0))],
        out_specs=[pl.BlockSpec((T, LANES), lambda hp, i: (i, hp))] * 2,
        out_shape=[SDS((S, ATT_HEADS * ATT_D), BF16), SDS((S, ATT_HEADS * ATT_D), F32)],
        scratch_shapes=[pltpu.VMEM((S, LANES), BF16)],
        compiler_params=_cparams(dimension_semantics=("arbitrary", "arbitrary")),
    )(src, src, src, gq2, gk2, cum_col, cum_row3)


def _fox_bwd(src, q_c0, k_c0, v_c0, gq2, gk2, cum_col, cum_row3, lse, dsrc, d_c0):
    S = src.shape[0]
    T = ATT_T
    nq, nhp = S // T, ATT_HEADS // 2
    qb0, kb0, vb0, db0 = q_c0 // LANES, k_c0 // LANES, v_c0 // LANES, d_c0 // LANES
    scale = ATT_D ** -0.5

    def body(q_ref, kraw_ref, v_ref, gq_ref, gk_ref, cc_ref, cr_ref, l_ref, do_ref,
             dq_ref, dk_ref, dv_ref, dc_ref, dg_ref, k_ref, dk_acc, dv_acc):
        hp, i = pl.program_id(0), pl.program_id(1)
        first = _iota((1, LANES), 1) < ATT_D

        @pl.when(i == 0)
        def _():
            k_ref[...] = _pair_norm(kraw_ref[...], gk_ref[...], first)[0].astype(BF16)
            dk_acc[...] = jnp.zeros_like(dk_acc)
            dv_acc[...] = jnp.zeros_like(dv_acc)
            dc_ref[...] = jnp.zeros_like(dc_ref)
            dg_ref[...] = jnp.zeros_like(dg_ref)

        H = ATT_U

        def backprop(tile, half):
            rows = pl.ds(half * H, H)
            row0 = tile * T + half * H
            klen = row0 + H
            q_raw = q_ref[rows, :]
            qn, rq = _pair_norm(q_raw, gq_ref[...], first)
            q = (qn * scale).astype(BF16)
            zq = jnp.zeros_like(q)
            dob = do_ref[rows, :].astype(BF16)
            zd = jnp.zeros_like(dob)
            lse_blk, cc = l_ref[rows, :], cc_ref[rows, :]
            k = k_ref[0:klen, :]
            zk = jnp.zeros_like(k)
            allowed = _iota((H, klen), 0) + row0 >= _iota((H, klen), 1)
            dq = jnp.zeros((H, LANES), F32)
            for hh in range(2):
                sel = first if hh == 0 else jnp.logical_not(first)
                qh, doh = jnp.where(sel, q, zq), jnp.where(sel, dob, zd)
                bias_q = _pick_col(cc, 2 * hp + hh) - jnp.max(jnp.where(sel, lse_blk, NEG), axis=1, keepdims=True)
                ck = jnp.concatenate([_pick_row(cr_ref[j], 2 * hp + hh) for j in range(tile + 1)], axis=1)[:, :klen]
                p = jnp.exp(jnp.where(allowed, _bdot(qh, k, _NT) + (bias_q - ck), NEG))
                dp = _bdot(doh, v_ref[0:klen, :], _NT)
                ds = p * (dp - jnp.sum(p * dp, axis=1, keepdims=True))
                dv_acc[0:klen, :] += _bdot(p, doh, _TN)
                dk_acc[0:klen, :] += _bdot(ds, qh, _TN)
                dcs = jnp.sum(ds, axis=0, keepdims=True)
                for j in range(tile + 1):
                    n = min(T, klen - j * T)
                    dc_ref[0, j, hh:hh + 1, 0:n] -= dcs[:, j * T:j * T + n]
                dq = dq + _bdot(ds, jnp.where(sel, k, zk), _NN)
            dq_raw, dgq = _pair_norm_bwd(dq * scale, q_raw, rq, gq_ref[...], first)
            dq_ref[rows, :] = dq_raw.astype(dq_ref.dtype)
            dg_ref[0, 0:1, :] += dgq

        for tile in range(nq):
            @pl.when(i == tile)
            def _(tile=tile):
                for half in range(T // H):
                    backprop(tile, half)

        @pl.when(i == nq - 1)
        def _():
            k_raw = kraw_ref[...]
            rk = _pair_norm(k_raw, gk_ref[...], first)[1]
            dk_raw, dgk = _pair_norm_bwd(dk_acc[...], k_raw, rk, gk_ref[...], first)
            dk_ref[...] = dk_raw.astype(dk_ref.dtype)
            dv_ref[...] = dv_acc[...].astype(dv_ref.dtype)
            dg_ref[0, 1:2, :] = dgk

    gain = pl.BlockSpec((1, LANES), lambda hp, i: (0, 0))
    band = SDS((S, ATT_HEADS * ATT_D), BF16)
    return pl.pallas_call(
        body, name="fox_bwd", grid=(nhp, nq),
        in_specs=[pl.BlockSpec((T, LANES), lambda hp, i: (i, qb0 + hp)), pl.BlockSpec((S, LANES), lambda hp, i: (0, kb0 + hp)),
                  pl.BlockSpec((S, LANES), lambda hp, i: (0, vb0 + hp)), gain, gain,
                  pl.BlockSpec((T, 16), lambda hp, i: (i, 0)), pl.BlockSpec((nq, 16, T), lambda hp, i: (0, 0, 0)),
                  pl.BlockSpec((T, LANES), lambda hp, i: (i, hp)), pl.BlockSpec((T, LANES), lambda hp, i: (i, db0 + hp))],
        out_specs=[pl.BlockSpec((T, LANES), lambda hp, i: (i, hp)), pl.BlockSpec((S, LANES), lambda hp, i: (0, hp)),
                   pl.BlockSpec((S, LANES), lambda hp, i: (0, hp)), pl.BlockSpec((1, nq, 8, T), lambda hp, i: (hp, 0, 0, 0)),
                   pl.BlockSpec((1, 8, LANES), lambda hp, i: (hp, 0, 0))],
        out_shape=[band, band, band, SDS((nhp, nq, 8, T), F32), SDS((nhp, 8, LANES), F32)],
        scratch_shapes=[pltpu.VMEM((S, LANES), BF16), pltpu.VMEM((S, LANES), F32), pltpu.VMEM((S, LANES), F32)],
        compiler_params=_cparams(dimension_semantics=("arbitrary", "arbitrary")),
    )(src, src, src, gq2, gk2, cum_col, cum_row3, lse, dsrc)


def _fold_gains(dg):
    def body(d_ref, o_ref):
        t = d_ref[0]
        for h in range(1, dg.shape[0]):
            t = t + d_ref[h]
        o_ref[...] = t + pltpu.roll(t, ATT_D, axis=1)

    return pl.pallas_call(body, name="fold_gains", out_shape=SDS(dg.shape[1:], F32), compiler_params=_cparams())(dg)


def _adamw_math(w, g, m, v):
    m = ADAM_B1 * m + (1.0 - ADAM_B1) * g
    v = ADAM_B2 * v + (1.0 - ADAM_B2) * jnp.square(g)
    m_hat = m / (1.0 - ADAM_B1 ** ADAM_STEP)
    v_hat = v / (1.0 - ADAM_B2 ** ADAM_STEP)
    delta = -ADAM_LR * (m_hat / (jnp.sqrt(v_hat) + ADAM_EPS) + ADAM_WD * w)
    return delta, m, v


def _reduce_adamw(parts, w, m, v, *, tr, name, tc=None):
    R, C = w.shape
    tr, tc = min(tr, R), tc or C
    nparts = parts.shape[0]

    def body(p_ref, w_ref, m_ref, v_ref, g_ref, d_ref, nm_ref, nv_ref):
        g = p_ref[0].astype(F32)
        for s in range(1, nparts):
            g = g + p_ref[s].astype(F32)
        g_ref[...] = g
        d_ref[...], nm_ref[...], nv_ref[...] = _adamw_math(w_ref[...], g, m_ref[...], v_ref[...])

    blk = pl.BlockSpec((tr, tc), lambda i, j: (i, j))
    return pl.pallas_call(
        body, name=name, grid=(R // tr, C // tc),
        in_specs=[pl.BlockSpec((nparts, tr, tc), lambda i, j: (0, i, j)), blk, blk, blk], out_specs=[blk] * 4,
        out_shape=[SDS((R, C), F32)] * 4, compiler_params=_cparams(dimension_semantics=("parallel", "parallel")),
    )(parts, w, m, v)


def _adamw(w, g, m, v, *, name):
    def body(w_ref, g_ref, m_ref, v_ref, d_ref, nm_ref, nv_ref):
        d_ref[...], nm_ref[...], nv_ref[...] = _adamw_math(w_ref[...], g_ref[...], m_ref[...], v_ref[...])

    return pl.pallas_call(body, name=name, out_shape=[SDS(w.shape, F32)] * 3, compiler_params=_cparams())(w, g, m, v)


def _peers():
    x, y, c = lax.axis_index("x"), lax.axis_index("y"), lax.axis_index("c")
    out = []
    for k in range(1, N_DEV):
        px, py, pc = x ^ ((k >> 2) & 1), y ^ ((k >> 1) & 1), c ^ (k & 1)
        out.append(((px, py, pc), 4 * px + 2 * py + pc))
    return 4 * x + 2 * y + c, out


_HBM = pl.BlockSpec(memory_space=pltpu.HBM)
_SEM = pl.BlockSpec(memory_space=pltpu.SEMAPHORE)
_DATAFLOW = pltpu.SideEffectType.DATAFLOW_SIDE_EFFECTING


NEAR = (1, 2, 4, 6)


def _plan_peers(scatter, ks=tuple(range(1, N_DEV))):
    return lambda me, peers: [(peers[k - 1][0], peers[k - 1][1] if scatter else None, me, k - 1) for k in ks]


def _plan_relay(me, peers):
    return [(peers[0][0], peers[k - 1][1], peers[k - 1][1], j) for j, k in enumerate((2, 4, 6))]


def _plan_pair(me, peers):
    return [(peers[0][0], peers[k - 1][1], j, j) for j, k in enumerate((1, 3, 5, 7))]


def _plan_chips(me, peers):
    return [(peers[k - 1][0], k // 2, k // 2, k // 2) for k in (2, 4, 6)]


def _copy(src, dst, c, send_sems, recv_sems):
    dev, s_slot, d_slot, i = c
    return pltpu.make_async_remote_copy(
        src_ref=src if s_slot is None else src.at[s_slot], dst_ref=dst.at[d_slot], send_sem=send_sems.at[i],
        recv_sem=recv_sems.at[i], device_id=dev, device_id_type=MESH)


def _copies_start(items, *, name):
    n = len(items)
    bufs = [it[0] for it in items] + [it[1] for it in items if it[1] is not None]
    nb = len(bufs)

    def body(*refs):
        srcs, extra, sems, token = refs[:n], iter(refs[n:nb]), refs[nb:nb + 2 * n], refs[-1]
        me, peers = _peers()
        for a, (_, land, plan) in enumerate(items):
            dst = srcs[a] if land is None else next(extra)
            for c in plan(me, peers):
                _copy(srcs[a], dst, c, sems[2 * a], sems[2 * a + 1]).start()
        token[...] = jnp.zeros_like(token)

    res = pl.pallas_call(
        body, name=name,
        out_shape=([pltpu.SemaphoreType.DMA((N_DEV - 1,))] * (2 * n) + [pltpu.HBM(b.shape, b.dtype) for b in bufs]
                   + [SDS((8, LANES), F32)]),
        in_specs=[_HBM] * nb, out_specs=[_SEM] * (2 * n) + [_HBM] * nb + [pl.BlockSpec(memory_space=pltpu.VMEM)],
        input_output_aliases={i: 2 * n + i for i in range(nb)},
        compiler_params=pltpu.CompilerParams(has_side_effects=_DATAFLOW),
    )(*[pltpu.with_memory_space_constraint(b, pltpu.HBM) for b in bufs])
    sems, thru, token = res[:2 * n], list(res[2 * n:2 * n + nb]), res[-1]
    extra = iter(thru[n:])
    return [(thru[a], None if it[1] is None else next(extra), sems[2 * a], sems[2 * a + 1], it[2])
            for a, it in enumerate(items)], token


def _copies_wait(handles, after, *, name):
    n = len(handles)
    after = list(after) if isinstance(after, (list, tuple)) else [after]
    bufs = [h[0] for h in handles] + [h[1] for h in handles if h[1] is not None]
    nb = len(bufs)

    def body(*refs):
        srcs, extra, sems = refs[:n], iter(refs[n:nb]), refs[nb:nb + 2 * n]
        me, peers = _peers()
        for a, h in enumerate(handles):
            dst = srcs[a] if h[1] is None else next(extra)
            for c in h[4](me, peers):
                cp = _copy(srcs[a], dst, c, sems[2 * a], sems[2 * a + 1])
                cp.wait_send()
                cp.wait_recv()

    flat_sems = [s for h in handles for s in (h[2], h[3])]
    res = pl.pallas_call(
        body, name=name, out_shape=[pltpu.HBM(b.shape, b.dtype) for b in bufs],
        in_specs=[_HBM] * nb + [_SEM] * (2 * n) + [pl.BlockSpec(memory_space=pl.ANY)] * len(after), out_specs=[_HBM] * nb,
        input_output_aliases={i: i for i in range(nb)},
        compiler_params=pltpu.CompilerParams(has_side_effects=_DATAFLOW),
    )(*bufs, *flat_sems, *after)
    extra = iter(res[n:])
    return [(res[a], res[a] if h[1] is None else next(extra)) for a, h in enumerate(handles)]


def _exchange_start(arrays, *, scatter, name, near=()):
    items = []
    for a, arr in enumerate(arrays):
        land = lax.empty(arr.shape if scatter else (N_DEV,) + arr.shape, arr.dtype)
        items.append((arr, land, _plan_peers(scatter, NEAR) if a in near else _plan_peers(scatter)))
    return _copies_start(items, name=name)


MOVE_ROWS, MOVE_SLOTS = 512, 3


def _move_rows(src, moves, rows, *, name):
    C = src.shape[1]
    covered = max(dst + n for _, n, dst in moves)
    tail = rows - covered
    assert sum(n for _, n, _ in moves) == covered
    chunks = [(lo + o, min(MOVE_ROWS, n - o), dst + o) for lo, n, dst in moves for o in range(0, n, MOVE_ROWS)]
    nch = len(chunks)

    def body(src_ref, o_ref, buf, sin, sout, *zero):
        def fetch(i):
            lo, n, _ = chunks[i]
            return pltpu.make_async_copy(src_ref.at[pl.ds(lo, n)], buf.at[i % MOVE_SLOTS, pl.ds(0, n)], sin.at[i % MOVE_SLOTS])

        def store(i):
            _, n, dst = chunks[i]
            return pltpu.make_async_copy(buf.at[i % MOVE_SLOTS, pl.ds(0, n)], o_ref.at[pl.ds(dst, n)], sout.at[i % MOVE_SLOTS])

        if tail:
            zero[0][...] = jnp.zeros_like(zero[0])
            fill = pltpu.make_async_copy(zero[0], o_ref.at[pl.ds(covered, tail)], zero[1])
            fill.start()
        for i in range(nch):
            if i >= MOVE_SLOTS:
                store(i - MOVE_SLOTS).wait()
            fetch(i).start()
            if i >= 1:
                fetch(i - 1).wait()
                store(i - 1).start()
        fetch(nch - 1).wait()
        store(nch - 1).start()
        for i in range(max(0, nch - MOVE_SLOTS), nch):
            store(i).wait()
        if tail:
            fill.wait()

    anyspec = pl.BlockSpec(memory_space=pl.ANY)
    dma = pltpu.SemaphoreType.DMA
    return pl.pallas_call(
        body, name=name, in_specs=[anyspec], out_specs=anyspec, out_shape=SDS((rows, C), src.dtype),
        scratch_shapes=([pltpu.VMEM((MOVE_SLOTS, MOVE_ROWS, C), src.dtype), dma((MOVE_SLOTS,)), dma((MOVE_SLOTS,))]
                        + ([pltpu.VMEM((tail, C), src.dtype), dma] if tail else [])),
        compiler_params=_cparams())(src)


def _pair_sum(a, b, *, name):
    n, R, C = a.shape
    tc = 256

    def body(a_ref, b_ref, o_ref):
        o_ref[...] = (a_ref[...].astype(F32) + b_ref[...].astype(F32)).astype(o_ref.dtype)

    blk = pl.BlockSpec((1, R, tc), lambda i, j: (i, 0, j))
    return pl.pallas_call(body, name=name, grid=(n, C // tc), in_specs=[blk, blk], out_specs=blk,
                          out_shape=SDS(a.shape, a.dtype), compiler_params=_cparams(dimension_semantics=("parallel", "parallel")))(a, b)


def _own_slot(landed, own, me):
    return lax.dynamic_update_slice(landed, own[None], (me,) + (0,) * own.ndim)


SMALL = (("g_mix", 1024), ("conv_w", 6144), ("conv_b", 1536), ("dt_bias", 16), ("a_log", 16), ("d_skip", 16),
         ("ssm_norm_w", 1024), ("g_q", 64), ("g_k", 64), ("f_bias", 16), ("g_xattn", 1024), ("g_mem", 1024),
         ("xg_q", 256), ("xg_k", 256), ("g_mlp", 1024), ("loss", 1))
NOT_PARAMS = ("conv_w", "loss")
SLAB_ROWS = 112
BIG = ("w_in", "w_out", "xq_w", "xkv_w", "xo_w", "w_up", "w_down")
WEIGHTS = ("g_mix", "w_in", "conv_w", "conv_b", "dt_bias", "a_log", "d_skip", "ssm_norm_w", "g_q", "g_k", "f_bias", "w_out",
           "g_xattn", "g_mem", "xq_w", "xkv_w", "xg_q", "xg_k", "xo_w", "g_mlp", "w_up", "w_down")
O_Z, O_XS, O_B, O_C, O_DT, O_Q, O_K, O_V, O_F, O_END = 0, 1024, 2048, 2304, 2560, 2576, 3600, 4624, 5648, 5664
IN_ROW_MOVES = ((O_Z, O_B - O_Z, C_Z), (O_Q, O_F - O_Q, C_Q), (O_B, O_Q - O_B, C_B), (O_F, O_END - O_F, C_DTF + 16))


def _pack_small(vals):
    rows = []
    for name, size in SMALL:
        flat = vals[name].reshape(-1).astype(F32)
        pad = -size % LANES
        rows.append(jnp.pad(flat, (0, pad)).reshape(-1, LANES))
    slab = jnp.concatenate(rows, axis=0)
    return jnp.pad(slab, ((0, SLAB_ROWS - slab.shape[0]), (0, 0)))


def _unpack_small(slab):
    out, r = {}, 0
    for name, size in SMALL:
        nr = -(-size // LANES)
        out[name] = slab[r:r + nr].reshape(-1)[:size]
        r += nr
    return out


def _step(p, m, v, x, mem, target):
    S = x.shape[0]
    TM = 256
    me = 4 * lax.axis_index("x") + 2 * lax.axis_index("y") + lax.axis_index("c")

    def rms(u, g, name):
        return _rw_fwd(_rms_fn, [_whole(u)], [_whole(g)], [(D_MODEL, BF16)], tm=TM, name=name)[0]

    def pin(param, token):
        return param + token[0:1, 0:1]

    def landed_with_own(pairs, scatter):
        out = []
        for src, land in pairs:
            own = lax.dynamic_index_in_dim(src, me, 0, keepdims=False) if scatter else src
            out.append(_own_slot(land, own, me))
        return out

    w_in_own, m_in_own, v_in_own = p["w_in"].T, m["w_in"].T, v["w_in"].T
    ag, ag_token = _exchange_start([w_in_own.astype(BF16), p["conv_w"]] + [p[n].astype(BF16) for n in BIG[1:]],
                                   scatter=False, name="allgather_start", near=(0, 2, 3, 4, 5, 6, 7))
    h1 = rms(x, pin(p["g_mix"], ag_token), "rms_mix")
    stand_in = {"conv_w": jnp.zeros((4, 1536), F32), "loss": jnp.zeros((1,), F32)}
    slabs = [_pack_small({**d, **stand_in}) for d in (p, m, v)]
    (win_src, win_land), convw_pair = _copies_wait(ag[:2], [h1, w_in_own, m_in_own, v_in_own] + slabs,
                                                   name="allgather_wait_in")
    relay, token = _copies_start([(win_land, None, _plan_relay)], name="allgather_relay_start")
    win_land = _copies_wait(relay, token, name="allgather_relay_wait")[0][1]
    win_g, convw_g = landed_with_own([(win_src, win_land), convw_pair], False)
    w_in_o = win_g.reshape(O_END, D_MODEL)
    w_in_t = _move_rows(w_in_o, IN_ROW_MOVES, P_COLS, name="w_in_rows")
    conv_w = convw_g.transpose(1, 0, 2).reshape(4, 1536)
    cw_xs, cw_bc = conv_w[:, :1024], conv_w[:, 1024:]
    cb_xs, cb_bc = p["conv_b"][:, :1024], p["conv_b"][:, 1024:]
    dt_bias, a_log, f_bias = p["dt_bias"].reshape(16, 1), p["a_log"].reshape(16, 1), p["f_bias"].reshape(16, 1)

    proj = _matmul(h1, w_in_t, mode="nt", tm=1024, tn=640, tk=1024, name="mm_in")
    xs_c = _conv_fwd(proj, C_XS, 1024, cw_xs, cb_xs, name="conv_xs")
    bc_c = _conv_fwd(proj, C_B, 512, cw_bc, cb_bc, name="conv_bc")
    dtf_t = proj[:, C_DTF:C_DTF + 32].T
    dt_t, acs_t, cum_t = _dtf_fwd(dtf_t, dt_bias, a_log, f_bias)
    dt_col, acs_col, cum_col = dt_t.T, acs_t.T, cum_t.T
    cum_row3 = cum_t.reshape(16, S // ATT_T, ATT_T).transpose(1, 0, 2)
    y_ssd, hs = _ssd_fwd(xs_c, dt_col, acs_col, acs_t, bc_c)
    gate_rows = [_whole(y_ssd), _whole(xs_c), (proj, C_Z, 1024)]
    gate_pars = [_whole(p["d_skip"]), _whole(p["ssm_norm_w"])]
    y_ssm = _rw_fwd(_gate_fn, gate_rows, gate_pars, [(1024, BF16)], tm=TM, name="gate")[0]
    gq2, gk2 = jnp.tile(p["g_q"], (1, 2)), jnp.tile(p["g_k"], (1, 2))
    o, lse = _fox_fwd(proj, C_Q, C_K, C_V, gq2, gk2, cum_col, cum_row3)
    mixed = jnp.concatenate([y_ssm, o], axis=1)
    arrived = _copies_wait(ag[2:], mixed, name="allgather_wait_rest")
    relay, token = _copies_start([(land, None, _plan_relay) for _, land in arrived], name="allgather_relay_rest_start")
    wout_g, = landed_with_own([(arrived[0][0], _copies_wait(relay[:1], token, name="allgather_relay_out_wait")[0][1])], False)
    w_out = wout_g.reshape(2 * D_MODEL, D_MODEL)
    x1, h2 = _matmul(mixed, w_out, mode="nn", tm=1024, tn=1024, tk=2048, name="mm_out", extras=(x,),
                     row_params=(p["g_xattn"],), epilogue=_ep_residual_rms, out_dtypes=[F32, BF16])
    relayed = _copies_wait(relay[1:], x1, name="allgather_relay_rest_wait")
    xq_g, xkv_w, xo_g, w_up, wdown_g = landed_with_own(
        [(src, land) for (src, _), (_, land) in zip(arrived[1:], relayed, strict=True)], False)
    xq_w = xq_g.reshape(D_MODEL, D_MODEL)
    xo_w = xo_g.reshape(D_MODEL, D_MODEL)
    w_down = wdown_g.reshape(4 * D_MODEL, D_MODEL)

    mem_n = rms(mem, p["g_mem"], "rms_mem")
    q2 = _matmul(h2, xq_w, mode="nn", tm=1024, tn=512, tk=1024, name="mm_xq")
    kv = _matmul(mem_n, xkv_w, mode="nn", b_shards=True, tm=256, tn=256, tk=1024, name="mm_xkv")
    xa_rows = [(q2, X_D * h, X_D) for h in range(X_HEADS)]
    xa_pars = ([(kv, X_D * h, X_D) for h in range(X_HEADS)] + [(kv, D_MODEL + X_D * h, X_D) for h in range(X_HEADS)]
               + [_whole(p["xg_q"]), _whole(p["xg_k"])])
    o2 = _rw_fwd(_xattn_fn, xa_rows, xa_pars, [(D_MODEL, BF16)], tm=2 * TM, name="xattn")[0]
    x2, h3 = _matmul(o2, xo_w, mode="nn", tm=1024, tn=1024, tk=1024, name="mm_xo", extras=(x1,),
                     row_params=(p["g_mlp"],), epilogue=_ep_residual_rms, out_dtypes=[F32, BF16])

    a, usq = _matmul(h3, w_up, mode="nn", b_shards=True, tm=2048, tn=512, tk=1024, name="mm_up", out_dtypes=[F32, BF16],
                     epilogue=lambda acc: (acc, jnp.square(jax.nn.relu(acc))))
    dy, loss_part = _matmul(usq, w_down, mode="nn", tm=1024, tn=512, tk=2048, name="mm_down", extras=(x2, target),
                            epilogue=functools.partial(_ep_loss, width=D_MODEL), sums=[(1, 1)])

    def row_shards(a):
        r, c = a.shape
        return a.reshape(N_DEV, r // N_DEV, c)

    g = {"loss": loss_part}
    g["w_down"] = _matmul(usq, dy, mode="tn", out_dtype=GRAD_WIRE, tm=1024, tn=1024, tk=1024, name="mm_d_wdown")
    da = _matmul(dy, w_down, mode="nt", tm=1024, tn=1024, tk=1024, name="mm_d_usq", out_dtype=BF16, extras=(a,),
                 epilogue=lambda acc, av: (2.0 * jax.nn.relu(av) * acc,))
    g["w_up"] = _matmul(h3, da, mode="tn", out_shards=True, out_dtype=GRAD_WIRE, tm=1024, tn=512, tk=1024, name="mm_d_wup")
    sent_mlp, token = _exchange_start([row_shards(g["w_down"]), g["w_up"]], scatter=True,
                                      name="grads_start_mlp")
    dx2, g["g_mlp"] = _matmul(da, w_up, mode="nt", b_shards=True, tm=1024, tn=1024, tk=512, name="mm_d_h3",
                              extras=(x2, dy), row_params=(pin(p["g_mlp"], token),), epilogue=_ep_rms_bwd,
                              sums=[(1, D_MODEL)])

    g["xo_w"] = _matmul(o2, dx2, mode="tn", out_dtype=GRAD_WIRE, tm=1024, tn=1024, tk=1024, name="mm_d_wxo")
    do2 = _matmul(dx2, xo_w, mode="nt", tm=1024, tn=512, tk=1024, name="mm_d_o2")
    dq2, dkv, g["xg_q"], g["xg_k"] = _rw_bwd(_xattn_fn, xa_rows, xa_pars, [_whole(do2)], tm=2 * TM, name="xattn_bwd",
                                             row_grads=[BF16] * X_HEADS, join_rows=True, join_params=2 * X_HEADS)
    g["xq_w"] = _matmul(h2, dq2, mode="tn", out_dtype=GRAD_WIRE, tm=1024, tn=1024, tk=1024, name="mm_d_wxq")
    dx1, g["g_xattn"] = _matmul(dq2, xq_w, mode="nt", tm=1024, tn=1024, tk=1024, name="mm_d_h2", extras=(x1, dx2),
                                row_params=(p["g_xattn"],), epilogue=_ep_rms_bwd, sums=[(1, D_MODEL)])
    g["xkv_w"] = _matmul(mem_n, dkv, mode="tn", out_shards=True, out_dtype=GRAD_WIRE, tm=1024, tn=256, tk=256,
                         name="mm_d_wxkv")
    dmem_n = _matmul(dkv, xkv_w, mode="nt", b_shards=True, tm=256, tn=1024, tk=256, name="mm_d_memn")
    g["g_mem"] = _rw_bwd(_rms_fn, [_whole(mem)], [_whole(p["g_mem"])], [_whole(dmem_n)], tm=TM, name="rms_mem_bwd",
                         row_grads=[None])[0]

    g["w_out"] = _matmul(mixed, dx1, mode="tn", out_dtype=GRAD_WIRE, tm=1024, tn=1024, tk=1024, name="mm_d_wout")
    sent_mid, token = _exchange_start(
        [row_shards(g["w_out"]), row_shards(g["xq_w"]), g["xkv_w"], row_shards(g["xo_w"])], scatter=True,
        name="grads_start_mid")
    dmixed = _matmul(dx1, w_out, mode="nt", tm=1024, tn=1024, tk=1024, name="mm_d_mixed")
    dq, dk, dv, dcum4, dgain = _fox_bwd(proj, C_Q, C_K, C_V, pin(gq2, token), gk2, cum_col, cum_row3, lse, dmixed, 1024)
    gains = _fold_gains(dgain)
    g["g_q"], g["g_k"] = gains[0:1, :ATT_D], gains[1:2, :ATT_D]
    dy_ssd, dxs_g, dz, g["d_skip"], g["ssm_norm_w"] = _rw_bwd(
        _gate_fn, gate_rows, gate_pars, [(dmixed, 0, 1024)], tm=TM, name="gate_bwd", row_grads=[F32, F32, BF16])
    dxs_s, ddt_col, dacs_col, dacs_row, d_b, d_c = _ssd_bwd(xs_c, dt_col, acs_col, acs_t, bc_c, hs, dy_ssd)
    dcum_t = dcum4[:, :, 0:2, :].transpose(0, 2, 1, 3).reshape(16, S)
    ddtf_t, ddtb, dalog, dfb = _dtf_bwd(dtf_t, dt_bias, a_log, f_bias, ddt_col.T, dacs_col.T, dacs_row, dcum_t)
    g["dt_bias"], g["a_log"], g["f_bias"] = ddtb, dalog, dfb
    dxs_raw, dcw_xs, dcb_xs = _conv_bwd(proj, C_XS, 1024, cw_xs, cb_xs, [dxs_s, dxs_g], name="conv_xs_bwd")
    dbc_raw, dcw_bc, dcb_bc = _conv_bwd(proj, C_B, 512, cw_bc, cb_bc, [jnp.concatenate([d_b, d_c], axis=1)],
                                        name="conv_bc_bwd")
    g["conv_w"] = jnp.concatenate([dcw_xs, dcw_bc], axis=1)
    g["conv_b"] = jnp.concatenate([dcb_xs, dcb_bc], axis=1)
    ddtf = jnp.pad(ddtf_t.T.astype(BF16), ((0, 0), (0, P_COLS - C_DTF - 32)))
    dproj = jnp.concatenate([dz, dxs_raw, dq, dk, dv, dbc_raw, ddtf], axis=1)
    dw_in_p = _matmul(dproj, h1, mode="tn", out_dtype=GRAD_WIRE, tm=640, tn=1024, tk=1024, name="mm_d_win")
    g["w_in"] = _move_rows(dw_in_p, [(dst, n, lo) for lo, n, dst in IN_ROW_MOVES], O_END, name="d_w_in_rows")
    half = N_DEV // 2
    send_in = row_shards(g["w_in"])
    pair, token = _copies_start([(send_in, lax.empty((half,) + send_in.shape[1:], send_in.dtype), _plan_pair)],
                                name="grads_in_pair_start")
    grads, delta, new_m, new_v = {}, {}, {}, {}

    def update(names, sent, after, wait_name):
        parts = landed_with_own(_copies_wait(sent, after, name=wait_name), True)
        for name, part in zip(names, parts, strict=True):
            grads[name], delta[name], new_m[name], new_v[name] = _reduce_adamw(part, p[name], m[name], v[name], tr=128,
                                                                                name="adamw_" + name)

    update(("w_down", "w_up"), sent_mlp, token, "grads_wait_mlp")
    send_in, from_sibling = _copies_wait(pair, delta["w_up"], name="grads_in_pair_wait")[0]
    mine = jnp.stack([lax.dynamic_index_in_dim(send_in, me ^ (2 * j), 0, keepdims=False) for j in range(half)])
    chip_sums = _pair_sum(mine, from_sibling, name="grads_in_pair_sum")
    sent_in, token = _copies_start([(chip_sums, lax.empty(chip_sums.shape, chip_sums.dtype), _plan_chips)],
                                   name="grads_in_chip_start")
    grad_x, g["g_mix"] = _matmul(dproj, w_in_t, mode="nn", tm=1024, tn=1024, tk=1152, name="mm_d_h1", extras=(x, dx1),
                                 row_params=(pin(p["g_mix"], token),), epilogue=_ep_rms_bwd, sums=[(1, D_MODEL)])
    sent_small, _ = _exchange_start([_pack_small(g)], scatter=False, name="small_grads_start")

    update(("w_out", "xq_w", "xkv_w", "xo_w"), sent_mid, grad_x, "grads_wait_mid")
    chip_sums, landed = _copies_wait(sent_in, delta["xo_w"], name="grads_in_chip_wait")[0]
    part = lax.dynamic_update_slice(landed, chip_sums[0:1], (0, 0, 0))
    res = _reduce_adamw(part, w_in_own, m_in_own, v_in_own, tr=part.shape[1], tc=256, name="adamw_w_in")
    grads["w_in"], delta["w_in"], new_m["w_in"], new_v["w_in"] = [r.T for r in res]
    small_parts = landed_with_own(_copies_wait(sent_small, delta["w_in"], name="small_grads_wait"), False)[0]
    sg, sd, sm, sv = _reduce_adamw(small_parts, *slabs, tr=SLAB_ROWS, name="adamw_small")
    for dst, slab in ((grads, sg), (delta, sd), (new_m, sm), (new_v, sv)):
        for name, flat in _unpack_small(slab).items():
            if name not in NOT_PARAMS:
                dst[name] = flat.reshape(p[name].shape)
    loss = _unpack_small(sg)["loss"][0]
    cw_shard = p["conv_w"].shape[1]
    grads["conv_w"] = lax.dynamic_slice(_unpack_small(sg)["conv_w"].reshape(4, 1536), (0, me * cw_shard), (4, cw_shard))
    delta["conv_w"], new_m["conv_w"], new_v["conv_w"] = _adamw(p["conv_w"], grads["conv_w"], m["conv_w"], v["conv_w"],
                                                               name="adamw_conv_w")
    return loss, grad_x, grads, delta, new_m, new_v


def kernel(x, mem, g_mix, w_in, conv_w, conv_b, dt_bias, a_log, d_skip, ssm_norm_w, g_q, g_k, f_bias, w_out, g_xattn, g_mem, xq_w, xkv_w, xg_q, xg_k, xo_w, g_mlp, w_up, w_down, loss_target, m_g_mix, m_w_in, m_conv_w, m_conv_b, m_dt_bias, m_a_log, m_d_skip, m_ssm_norm_w, m_g_q, m_g_k, m_f_bias, m_w_out, m_g_xattn, m_g_mem, m_xq_w, m_xkv_w, m_xg_q, m_xg_k, m_xo_w, m_g_mlp, m_w_up, m_w_down, v_g_mix, v_w_in, v_conv_w, v_conv_b, v_dt_bias, v_a_log, v_d_skip, v_ssm_norm_w, v_g_q, v_g_k, v_f_bias, v_w_out, v_g_xattn, v_g_mem, v_xq_w, v_xkv_w, v_xg_q, v_xg_k, v_xo_w, v_g_mlp, v_w_up, v_w_down):
    args = locals()
    drop = lambda t: t[0] if t.ndim == 3 else t
    p = {n: drop(args[n]) for n in WEIGHTS}
    m = {n: drop(args["m_" + n]) for n in WEIGHTS}
    v = {n: drop(args["v_" + n]) for n in WEIGHTS}
    loss, grad_x, grads, delta, new_m, new_v = _step(p, m, v, x[0], mem[0], loss_target[0])
    outs = [loss, grad_x[None]]
    for d in (grads, delta, new_m, new_v):
        outs += [d[n].reshape(args[n].shape) for n in WEIGHTS]
    return tuple(outs)
```
